```python
import jax, jax.numpy as jnp
from jax import lax
import numpy as np

D_MODEL = 2048
BATCH = 8
SEQ = 4096
DEPTH = 1

HEAD_DIM = 128
N_HEADS = D_MODEL // HEAD_DIM
N_HEADS_B = N_HEADS // 4
N_HEADS_A = N_HEADS - N_HEADS_B
DILATION_PATTERNS = ((128, 1), (512, 4), (2048, 16))
HEADS_PER_GROUP = N_HEADS_A // len(DILATION_PATTERNS)
WIDTH_A = HEADS_PER_GROUP * HEAD_DIM
WIDTH_B = N_HEADS_B * HEAD_DIM
GRID_W = 64
NA_ROWS = 8
NA_COLS = 16
D_FF = 4 * D_MODEL
N_BRANCHES = 2
BAND_BLOCK = 64
EPS = 1e-6
NEG = -1e30

kernel_name = "hybrid_dilated_neighbourhood_gated_encoder"


def _rmsnorm(x, g):
    xf = x.astype(jnp.float32)
    y = xf * lax.rsqrt(jnp.mean(xf * xf, axis=-1, keepdims=True) + EPS)
    return (y * g.astype(jnp.float32)).astype(x.dtype)


def _alibi_slopes(n):
    return jnp.asarray(2.0 ** (-8.0 * np.arange(1, n + 1) / n), dtype=jnp.float32)


def _banded_attention(q, k, v, slopes, half_window, stride):
    B, H, N, L, hd = q.shape
    Qb = BAND_BLOCK
    W = half_window
    nb = -(-L // Qb)
    Lp = nb * Qb
    Kb = Qb + 2 * W
    qp = jnp.pad(q, ((0, 0), (0, 0), (0, 0), (0, Lp - L), (0, 0)))
    kp = jnp.pad(k, ((0, 0), (0, 0), (0, 0), (W, Lp - L + W), (0, 0)))
    vp = jnp.pad(v, ((0, 0), (0, 0), (0, 0), (W, Lp - L + W), (0, 0)))
    key_idx = np.arange(nb)[:, None] * Qb + np.arange(Kb)[None, :]
    kb = jnp.take(kp, key_idx, axis=3)
    vb = jnp.take(vp, key_idx, axis=3)
    qb = qp.reshape(B, H, N, nb, Qb, hd)
    s = jnp.einsum('bhniqd,bhnikd->bhniqk', qb, kb).astype(jnp.float32) * (HEAD_DIM ** -0.5)
    qpos = np.arange(Lp).reshape(nb, Qb)
    kpos = key_idx - W
    rel = np.abs(kpos[:, None, :] - qpos[:, :, None])
    valid = (rel <= W) & (kpos >= 0)[:, None, :] & (kpos < L)[:, None, :]
    bias = -(slopes * stride)[:, None, None, None, None] * rel.astype(np.float32)[None, None]
    s = jnp.where(valid, s + bias, NEG)
    m = jnp.max(s, axis=-1, keepdims=True)
    p = jnp.exp(s - m)
    den = jnp.sum(p, axis=-1, keepdims=True)
    o = jnp.einsum('bhniqk,bhnikd->bhniqd', (p / den).astype(v.dtype), vb)
    lse = (m + jnp.log(den))[..., 0]
    o = o.reshape(B, H, N, Lp, hd)[:, :, :, :L]
    lse = lse.reshape(B, H, N, Lp)[:, :, :, :L]
    return o, lse


def _dilated_mixer(q, k, v):
    B, S = q.shape[0], q.shape[1]
    slopes = _alibi_slopes(N_HEADS_A)
    outs, lses = [], []
    for g, (window, d) in enumerate(DILATION_PATTERNS):
        hs = slice(g * HEADS_PER_GROUP, (g + 1) * HEADS_PER_GROUP)

        def to_residue(t):
            return t[:, :, hs].reshape(B, S // d, d, HEADS_PER_GROUP, HEAD_DIM).transpose(0, 3, 2, 1, 4)

        o, lse = _banded_attention(to_residue(q), to_residue(k), to_residue(v),
                                   slopes[hs], window // (2 * d), d)
        outs.append(o.transpose(0, 3, 2, 1, 4).reshape(B, S, HEADS_PER_GROUP, HEAD_DIM))
        lses.append(lse.transpose(0, 3, 2, 1).reshape(B, S, HEADS_PER_GROUP))
    o_all = jnp.stack(outs, axis=0).astype(jnp.float32)
    alpha = jax.nn.softmax(jnp.stack(lses, axis=0), axis=0)
    y = jnp.sum(alpha[..., None] * o_all, axis=0)
    return y.reshape(B, S, WIDTH_A).astype(q.dtype)


def _neighbourhood_mixer(q, k, v, rpb):
    B, S, H, hd = q.shape
    rows = S // GRID_W
    kh = min(NA_ROWS, rows)
    kw = NA_COLS

    def grid(t):
        return t.reshape(B, rows, GRID_W, H, hd).transpose(0, 3, 1, 2, 4)

    row_start = np.clip(np.arange(rows) - kh // 2, 0, rows - kh)
    row_idx = row_start[:, None] + np.arange(kh)[None, :]
    kg = jnp.take(grid(k), row_idx, axis=2).reshape(B, H, rows, kh * GRID_W, hd)
    vg = jnp.take(grid(v), row_idx, axis=2).reshape(B, H, rows, kh * GRID_W, hd)
    col = np.arange(GRID_W)
    col_start = np.clip(col - kw // 2, 0, GRID_W - kw)
    col_ok = (col[None, :] >= col_start[:, None]) & (col[None, :] < col_start[:, None] + kw)
    mask = np.tile(col_ok, (1, kh))
    dr = row_idx - np.arange(rows)[:, None]
    dc = np.clip(col[None, :] - col[:, None], -(kw - 1), kw - 1)
    bias = rpb.astype(jnp.float32)[:, dr + NA_ROWS - 1][..., dc + NA_COLS - 1]
    bias = bias.transpose(0, 1, 3, 2, 4).reshape(H, rows, GRID_W, kh * GRID_W)
    s = jnp.einsum('bhrqd,bhrkd->bhrqk', grid(q), kg).astype(jnp.float32) * (HEAD_DIM ** -0.5) + bias
    s = jnp.where(mask, s, NEG)
    p = jax.nn.softmax(s, axis=-1)
    o = jnp.einsum('bhrqk,bhrkd->bhrqd', p.astype(v.dtype), vg)
    return o.transpose(0, 2, 3, 1, 4).reshape(B, S, WIDTH_B)


def _mixer_block(h, w_qkv, w_gate, b_gate, rpb, w_proj_a, w_proj_b, w_out):
    B, S, _ = h.shape
    qkv = (h @ w_qkv).reshape(B, S, 3, N_HEADS, HEAD_DIM)
    q, k, v = qkv[:, :, 0], qkv[:, :, 1], qkv[:, :, 2]
    y_a = _dilated_mixer(q[:, :, :N_HEADS_A], k[:, :, :N_HEADS_A], v[:, :, :N_HEADS_A])
    y_b = _neighbourhood_mixer(q[:, :, N_HEADS_A:], k[:, :, N_HEADS_A:], v[:, :, N_HEADS_A:], rpb)
    gates = jax.nn.sigmoid(h @ w_gate + b_gate).reshape(B, S, N_BRANCHES, D_MODEL)
    merged = gates[:, :, 0] * (y_a @ w_proj_a) + gates[:, :, 1] * (y_b @ w_proj_b)
    return merged @ w_out


def _sqrelu_mlp(h, w_up, w_down):
    return jnp.square(jax.nn.relu(h @ w_up)) @ w_down


def _fwd_setup_inputs(seed: int = 0) -> dict:
    key = jax.random.key(seed)
    ks = jax.random.split(key, 14)
    f32 = jnp.float32

    def nrm(k, shape, scale):
        return jax.random.normal(k, shape, f32) * scale

    return {
        "x": nrm(ks[0], (BATCH, SEQ, D_MODEL), 1.0),
        "norm_mix": 1.0 + nrm(ks[1], (DEPTH, D_MODEL), 0.02),
        "w_qkv": nrm(ks[2], (DEPTH, D_MODEL, 3 * N_HEADS * HEAD_DIM), D_MODEL ** -0.5),
        "w_gate": nrm(ks[3], (DEPTH, D_MODEL, N_BRANCHES * D_MODEL), D_MODEL ** -0.5),
        "b_gate": nrm(ks[4], (DEPTH, N_BRANCHES * D_MODEL), 0.02),
        "rpb": nrm(ks[5], (DEPTH, N_HEADS_B, 2 * NA_ROWS - 1, 2 * NA_COLS - 1), 0.1),
        "w_proj_a": nrm(ks[6], (DEPTH, WIDTH_A, D_MODEL), WIDTH_A ** -0.5),
        "w_proj_b": nrm(ks[7], (DEPTH, WIDTH_B, D_MODEL), WIDTH_B ** -0.5),
        "w_out": nrm(ks[8], (DEPTH, D_MODEL, D_MODEL), D_MODEL ** -0.5),
        "norm_mlp": 1.0 + nrm(ks[9], (DEPTH, D_MODEL), 0.02),
        "w_up": nrm(ks[10], (DEPTH, D_MODEL, D_FF), D_MODEL ** -0.5),
        "w_down": nrm(ks[11], (DEPTH, D_FF, D_MODEL), D_FF ** -0.5),
        "norm_final": 1.0 + nrm(ks[12], (D_MODEL,), 0.02),
    }


def _fwd_reference(x, norm_mix, w_qkv, w_gate, b_gate, rpb, w_proj_a, w_proj_b, w_out,
              norm_mlp, w_up, w_down, norm_final):
    for l in range(DEPTH):
        h = _rmsnorm(x, norm_mix[l])
        x = x + _mixer_block(h, w_qkv[l], w_gate[l], b_gate[l], rpb[l],
                             w_proj_a[l], w_proj_b[l], w_out[l])
        h = _rmsnorm(x, norm_mlp[l])
        x = x + _sqrelu_mlp(h, w_up[l], w_down[l])
    return _rmsnorm(x, norm_final)


import jax as _jax
import jax.numpy as _jnp

TWIN_FORMAT = 'train_step'
FWD_PARAMS = ['x', 'norm_mix', 'w_qkv', 'w_gate', 'b_gate', 'rpb', 'w_proj_a', 'w_proj_b', 'w_out', 'norm_mlp', 'w_up', 'w_down', 'norm_final']
TWIN_WEIGHTS = ['norm_mix', 'w_qkv', 'w_gate', 'b_gate', 'rpb', 'w_proj_a', 'w_proj_b', 'w_out', 'norm_mlp', 'w_up', 'w_down', 'norm_final']
TWIN_DIFF_INPUT = 'x'
TWIN_INPUTS = ['x', 'norm_mix', 'w_qkv', 'w_gate', 'b_gate', 'rpb', 'w_proj_a', 'w_proj_b', 'w_out', 'norm_mlp', 'w_up', 'w_down', 'norm_final', 'loss_target', 'm_norm_mix', 'm_w_qkv', 'm_w_gate', 'm_b_gate', 'm_rpb', 'm_w_proj_a', 'm_w_proj_b', 'm_w_out', 'm_norm_mlp', 'm_w_up', 'm_w_down', 'm_norm_final', 'v_norm_mix', 'v_w_qkv', 'v_w_gate', 'v_b_gate', 'v_rpb', 'v_w_proj_a', 'v_w_proj_b', 'v_w_out', 'v_norm_mlp', 'v_w_up', 'v_w_down', 'v_norm_final']
TWIN_OUTPUTS = ['loss', 'grad_x', 'grad_norm_mix', 'grad_w_qkv', 'grad_w_gate', 'grad_b_gate', 'grad_rpb', 'grad_w_proj_a', 'grad_w_proj_b', 'grad_w_out', 'grad_norm_mlp', 'grad_w_up', 'grad_w_down', 'grad_norm_final', 'delta_norm_mix', 'delta_w_qkv', 'delta_w_gate', 'delta_b_gate', 'delta_rpb', 'delta_w_proj_a', 'delta_w_proj_b', 'delta_w_out', 'delta_norm_mlp', 'delta_w_up', 'delta_w_down', 'delta_norm_final', 'new_m_norm_mix', 'new_m_w_qkv', 'new_m_w_gate', 'new_m_b_gate', 'new_m_rpb', 'new_m_w_proj_a', 'new_m_w_proj_b', 'new_m_w_out', 'new_m_norm_mlp', 'new_m_w_up', 'new_m_w_down', 'new_m_norm_final', 'new_v_norm_mix', 'new_v_w_qkv', 'new_v_w_gate', 'new_v_b_gate', 'new_v_rpb', 'new_v_w_proj_a', 'new_v_w_proj_b', 'new_v_w_out', 'new_v_norm_mlp', 'new_v_w_up', 'new_v_w_down', 'new_v_norm_final']
TWIN_LEAF_KINDS = {'loss': 'loss', 'grad_x': 'grad_x', 'grad_norm_mix': 'grad_w', 'grad_w_qkv': 'grad_w', 'grad_w_gate': 'grad_w', 'grad_b_gate': 'grad_w', 'grad_rpb': 'grad_w', 'grad_w_proj_a': 'grad_w', 'grad_w_proj_b': 'grad_w', 'grad_w_out': 'grad_w', 'grad_norm_mlp': 'grad_w', 'grad_w_up': 'grad_w', 'grad_w_down': 'grad_w', 'grad_norm_final': 'grad_w', 'delta_norm_mix': 'delta_w', 'delta_w_qkv': 'delta_w', 'delta_w_gate': 'delta_w', 'delta_b_gate': 'delta_w', 'delta_rpb': 'delta_w', 'delta_w_proj_a': 'delta_w', 'delta_w_proj_b': 'delta_w', 'delta_w_out': 'delta_w', 'delta_norm_mlp': 'delta_w', 'delta_w_up': 'delta_w', 'delta_w_down': 'delta_w', 'delta_norm_final': 'delta_w', 'new_m_norm_mix': 'new_m', 'new_m_w_qkv': 'new_m', 'new_m_w_gate': 'new_m', 'new_m_b_gate': 'new_m', 'new_m_rpb': 'new_m', 'new_m_w_proj_a': 'new_m', 'new_m_w_proj_b': 'new_m', 'new_m_w_out': 'new_m', 'new_m_norm_mlp': 'new_m', 'new_m_w_up': 'new_m', 'new_m_w_down': 'new_m', 'new_m_norm_final': 'new_m', 'new_v_norm_mix': 'new_v', 'new_v_w_qkv': 'new_v', 'new_v_w_gate': 'new_v', 'new_v_b_gate': 'new_v', 'new_v_rpb': 'new_v', 'new_v_w_proj_a': 'new_v', 'new_v_w_proj_b': 'new_v', 'new_v_w_out': 'new_v', 'new_v_norm_mlp': 'new_v', 'new_v_w_up': 'new_v', 'new_v_w_down': 'new_v', 'new_v_norm_final': 'new_v'}


def _forward(args):
    return _fwd_reference(*[args[k] for k in FWD_PARAMS])


def _output_shape():
    def fwd():
        inp = _fwd_setup_inputs(0)
        return _fwd_reference(*[inp[k] for k in FWD_PARAMS])
    out = _jax.eval_shape(fwd)
    return out.shape, out.dtype

N_MICROBATCH = 1
ADAM_LR = 0.001
ADAM_B1 = 0.9
ADAM_B2 = 0.999
ADAM_EPS = 1e-08
ADAM_WD = 0.01
ADAM_STEP = 10
PER_EXAMPLE_BATCH_AXIS = {'x': 0, 'loss_target': 0}
SHARED_INPUTS = []
_WEIGHT_DTYPES = {'norm_mix': _jnp.float32, 'w_qkv': _jnp.float32, 'w_gate': _jnp.float32, 'b_gate': _jnp.float32, 'rpb': _jnp.float32, 'w_proj_a': _jnp.float32, 'w_proj_b': _jnp.float32, 'w_out': _jnp.float32, 'norm_mlp': _jnp.float32, 'w_up': _jnp.float32, 'w_down': _jnp.float32, 'norm_final': _jnp.float32}
MOMENT_SCALE = {'norm_mix': 2.801479e-02, 'w_qkv': 1.529443e-02, 'w_gate': 4.481742e-03, 'b_gate': 4.526473e-03, 'rpb': 1.055970e-02, 'w_proj_a': 1.251792e-02, 'w_proj_b': 1.028709e-02, 'w_out': 1.621651e-02, 'norm_mlp': 8.138402e-02, 'w_up': 3.952357e-02, 'w_down': 8.139095e-02, 'norm_final': 1.615579e+01}


def _to_microbatches(a, axis):
    t = _jnp.moveaxis(a, axis, 0)
    t = t.reshape((N_MICROBATCH, t.shape[0] // N_MICROBATCH) + t.shape[1:])
    return _jnp.moveaxis(t, 1, axis + 1)


def setup_inputs(seed: int = 0) -> dict:
    inp = _fwd_setup_inputs(seed)
    key = _jax.random.fold_in(_jax.random.key(seed), 7919)
    shape, _ = _output_shape()
    out = dict(inp)
    out["loss_target"] = _jax.random.normal(_jax.random.fold_in(key, 0), shape, _jnp.float32)
    for i, name in enumerate(TWIN_WEIGHTS):
        w = inp[name].astype(_jnp.float32)
        if MOMENT_SCALE is None:
            s = _jnp.sqrt(_jnp.mean(_jnp.square(w)) + 1e-30)
        else:
            s = MOMENT_SCALE[name]
        km, kv = _jax.random.split(_jax.random.fold_in(key, i + 1))
        out[name] = w
        out["m_" + name] = s * _jax.random.normal(km, w.shape, _jnp.float32)
        out["v_" + name] = (s * s) * _jax.random.uniform(kv, w.shape, _jnp.float32, 0.5, 1.5)
    if N_MICROBATCH > 1:
        for name, axis in PER_EXAMPLE_BATCH_AXIS.items():
            out[name] = _to_microbatches(out[name], axis)
    return {'x': out['x'], 'norm_mix': out['norm_mix'], 'w_qkv': out['w_qkv'], 'w_gate': out['w_gate'], 'b_gate': out['b_gate'], 'rpb': out['rpb'], 'w_proj_a': out['w_proj_a'], 'w_proj_b': out['w_proj_b'], 'w_out': out['w_out'], 'norm_mlp': out['norm_mlp'], 'w_up': out['w_up'], 'w_down': out['w_down'], 'norm_final': out['norm_final'], 'loss_target': out['loss_target'], 'm_norm_mix': out['m_norm_mix'], 'm_w_qkv': out['m_w_qkv'], 'm_w_gate': out['m_w_gate'], 'm_b_gate': out['m_b_gate'], 'm_rpb': out['m_rpb'], 'm_w_proj_a': out['m_w_proj_a'], 'm_w_proj_b': out['m_w_proj_b'], 'm_w_out': out['m_w_out'], 'm_norm_mlp': out['m_norm_mlp'], 'm_w_up': out['m_w_up'], 'm_w_down': out['m_w_down'], 'm_norm_final': out['m_norm_final'], 'v_norm_mix': out['v_norm_mix'], 'v_w_qkv': out['v_w_qkv'], 'v_w_gate': out['v_w_gate'], 'v_b_gate': out['v_b_gate'], 'v_rpb': out['v_rpb'], 'v_w_proj_a': out['v_w_proj_a'], 'v_w_proj_b': out['v_w_proj_b'], 'v_w_out': out['v_w_out'], 'v_norm_mlp': out['v_norm_mlp'], 'v_w_up': out['v_w_up'], 'v_w_down': out['v_w_down'], 'v_norm_final': out['v_norm_final']}


def _loss(weights, diff, rest, loss_target):
    with _jax.named_scope("forward"):
        args = {**rest, TWIN_DIFF_INPUT: diff, **{k: w.astype(_WEIGHT_DTYPES[k]) for k, w in weights.items()}}
        y = _forward(args)
    with _jax.named_scope("loss_head"):
        err = _jnp.square(y.astype(_jnp.float32) - loss_target)
        return 0.5 * _jnp.sum(_jnp.mean(err, axis=-1)) if err.ndim else 0.5 * err


def _adamw(w, g, m, v):
    m = ADAM_B1 * m + (1.0 - ADAM_B1) * g
    v = ADAM_B2 * v + (1.0 - ADAM_B2) * _jnp.square(g)
    m_hat = m / (1.0 - ADAM_B1 ** ADAM_STEP)
    v_hat = v / (1.0 - ADAM_B2 ** ADAM_STEP)
    delta = -ADAM_LR * (m_hat / (_jnp.sqrt(v_hat) + ADAM_EPS) + ADAM_WD * w)
    return delta, m, v


def reference(x, norm_mix, w_qkv, w_gate, b_gate, rpb, w_proj_a, w_proj_b, w_out, norm_mlp, w_up, w_down, norm_final, loss_target, m_norm_mix, m_w_qkv, m_w_gate, m_b_gate, m_rpb, m_w_proj_a, m_w_proj_b, m_w_out, m_norm_mlp, m_w_up, m_w_down, m_norm_final, v_norm_mix, v_w_qkv, v_w_gate, v_b_gate, v_rpb, v_w_proj_a, v_w_proj_b, v_w_out, v_norm_mlp, v_w_up, v_w_down, v_norm_final):
    given = dict(x=x, norm_mix=norm_mix, w_qkv=w_qkv, w_gate=w_gate, b_gate=b_gate, rpb=rpb, w_proj_a=w_proj_a, w_proj_b=w_proj_b, w_out=w_out, norm_mlp=norm_mlp, w_up=w_up, w_down=w_down, norm_final=norm_final, loss_target=loss_target, m_norm_mix=m_norm_mix, m_w_qkv=m_w_qkv, m_w_gate=m_w_gate, m_b_gate=m_b_gate, m_rpb=m_rpb, m_w_proj_a=m_w_proj_a, m_w_proj_b=m_w_proj_b, m_w_out=m_w_out, m_norm_mlp=m_norm_mlp, m_w_up=m_w_up, m_w_down=m_w_down, m_norm_final=m_norm_final, v_norm_mix=v_norm_mix, v_w_qkv=v_w_qkv, v_w_gate=v_w_gate, v_b_gate=v_b_gate, v_rpb=v_rpb, v_w_proj_a=v_w_proj_a, v_w_proj_b=v_w_proj_b, v_w_out=v_w_out, v_norm_mlp=v_norm_mlp, v_w_up=v_w_up, v_w_down=v_w_down, v_norm_final=v_norm_final)
    weights = {n: given[n] for n in TWIN_WEIGHTS}
    shared = {n: given[n] for n in SHARED_INPUTS}
    per_example = {n: given[n] for n in ['x']}
    grad_fn = _jax.value_and_grad(_loss, argnums=(0, 1))

    def one_microbatch(ex, loss_target):
        ex = dict(ex)
        diff = ex.pop(TWIN_DIFF_INPUT)
        return grad_fn(weights, diff, {**shared, **ex}, loss_target)

    if N_MICROBATCH == 1:
        loss, (grad_w, grad_x) = one_microbatch(per_example, given["loss_target"])
    else:
        def body(carry, xs):
            loss_sum, grad_sum = carry
            l_k, (gw_k, gx_k) = one_microbatch(xs[0], xs[1])
            with _jax.named_scope("update"):
                return (loss_sum + l_k, _jax.tree.map(_jnp.add, grad_sum, gw_k)), gx_k

        init = (_jnp.zeros((), _jnp.float32), _jax.tree.map(_jnp.zeros_like, weights))
        (loss, grad_w), grad_x = _jax.lax.scan(body, init, (per_example, given["loss_target"]))
    with _jax.named_scope("update"):
        delta_w, new_m, new_v = {}, {}, {}
        for n in TWIN_WEIGHTS:
            delta_w[n], new_m[n], new_v[n] = _adamw(weights[n], grad_w[n], given["m_" + n], given["v_" + n])
    return (loss, grad_x, *[grad_w[n] for n in TWIN_WEIGHTS], *[delta_w[n] for n in TWIN_WEIGHTS],
            *[new_m[n] for n in TWIN_WEIGHTS], *[new_v[n] for n in TWIN_WEIGHTS])
```

```python
import functools
import math

import numpy as np
import jax
import jax.numpy as jnp
from jax import lax
from jax.experimental import pallas as pl
from jax.experimental.pallas import tpu as pltpu

BF = jnp.bfloat16
F32 = jnp.float32
MESH = pl.DeviceIdType.MESH

HEAD_DIM = 128
N_HEADS = 16
N_HEADS_A = 12
QKV_W = N_HEADS * HEAD_DIM
DILATIONS = (1, 4, 16)
HALF_WINDOW = 64
GRID_W = 64
NA_ROWS = 8
NA_COLS = 16
RPB_ROWS = 2 * NA_ROWS - 1
RPB_COLS = 2 * NA_COLS - 1
EPS = 1e-6
NEG = -1e30
SCALE = HEAD_DIM ** -0.5

ADAM_LR = 0.001
ADAM_B1 = 0.9
ADAM_B2 = 0.999
ADAM_EPS = 1e-08
ADAM_WD = 0.01
ADAM_STEP = 10

N_CHIPS = 4
VMEM_LIMIT_BYTES = 48 * 1024 * 1024
QB = 128
KB = QB + 2 * HALF_WINDOW


def _cparams(sem=None):
    return pltpu.CompilerParams(dimension_semantics=sem, vmem_limit_bytes=VMEM_LIMIT_BYTES)


def _tile(dim, want):
    t = min(dim, want)
    assert dim % t == 0, (dim, want)
    return t


NN = ((1,), (0,))
NT = ((1,), (1,))
TN = ((0,), (0,))


def _matmul(name, a, b, a_spec, b_spec, dims, grid, acc_shape, extras, outs, epilogue, precision=None):
    n_ex, n_out, nk = len(extras), len(outs), grid[2]

    def body(*refs):
        a_ref, b_ref = refs[0], refs[1]
        ex_refs = refs[2:2 + n_ex]
        out_refs = refs[2 + n_ex:2 + n_ex + n_out]
        acc_ref = refs[-1]
        k = pl.program_id(2)

        @pl.when(k == 0)
        def _():
            acc_ref[...] = jnp.zeros_like(acc_ref)

        acc_ref[...] += lax.dot_general(a_ref[...], b_ref[...], (dims, ((), ())),
                                        preferred_element_type=F32, precision=precision)

        @pl.when(k == nk - 1)
        def _():
            epilogue(acc_ref[...], ex_refs, out_refs)

    return pl.pallas_call(
        body, name=name, grid=grid,
        in_specs=[a_spec, b_spec] + [s for _, s in extras],
        out_specs=[s for _, s in outs],
        out_shape=[sh for sh, _ in outs],
        scratch_shapes=[pltpu.VMEM(acc_shape, F32)],
        compiler_params=_cparams(("parallel", "parallel", "arbitrary")),
    )(a, b, *[e for e, _ in extras])


def _store(dtype):
    def epilogue(acc, ex, outs):
        outs[0][...] = acc.astype(dtype)
    return epilogue


def _sds(shape, dtype):
    return jax.ShapeDtypeStruct(shape, dtype)


def _mm_nn_cols(name, a, g, out_dtype, epilogue=None, extras=(), outs=None, tm=1024, tn=1024, tk=1024):
    M, K = a.shape
    _, _, Nq = g.shape
    tm, tn, tk = _tile(M, tm), _tile(Nq, tn), _tile(K, tk)
    q = Nq // tn
    grid = (M // tm, N_CHIPS * q, K // tk)
    if outs is None:
        outs = [(_sds((M, N_CHIPS * Nq), out_dtype), pl.BlockSpec((tm, tn), lambda i, j, k: (i, j)))]
    return _matmul(name, a, g, pl.BlockSpec((tm, tk), lambda i, j, k: (i, k)),
                   pl.BlockSpec((None, tk, tn), lambda i, j, k: (j // q, k, j % q)), NN, grid, (tm, tn),
                   list(extras), outs, epilogue or _store(out_dtype)), (tm, tn, tk)


def _rms_fwd(name, x, g):
    S, D = x.shape
    tm = _tile(S, 256)

    def body(x_ref, g_ref, h_ref):
        xv = x_ref[...]
        r = lax.rsqrt(jnp.mean(xv * xv, axis=-1, keepdims=True) + EPS)
        h_ref[...] = ((xv * r) * g_ref[...]).astype(BF)

    row = pl.BlockSpec((tm, D), lambda i: (i, 0))
    return pl.pallas_call(
        body, name=name, grid=(S // tm,), in_specs=[row, pl.BlockSpec((1, D), lambda i: (0, 0))],
        out_specs=row, out_shape=_sds((S, D), BF), compiler_params=_cparams(("parallel",)),
    )(x, g)


def _rms_bwd(name, dh, x, g, dres):
    S, D = x.shape
    tm = _tile(S, 256)

    def body(dh_ref, x_ref, g_ref, dres_ref, dx_ref, dxb_ref, dg_ref):
        xv = x_ref[...]
        r = lax.rsqrt(jnp.mean(xv * xv, axis=-1, keepdims=True) + EPS)
        n = xv * r
        dhv = dh_ref[...]
        dyg = dhv * g_ref[...]
        dx = dres_ref[...] + r * (dyg - n * jnp.mean(dyg * n, axis=-1, keepdims=True))
        dx_ref[...] = dx
        dxb_ref[...] = dx.astype(BF)

        @pl.when(pl.program_id(0) == 0)
        def _():
            dg_ref[...] = jnp.zeros_like(dg_ref)

        dg_ref[...] += jnp.sum(dhv * n, axis=0, keepdims=True)

    row = pl.BlockSpec((tm, D), lambda i: (i, 0))
    vec = pl.BlockSpec((1, D), lambda i: (0, 0))
    return pl.pallas_call(
        body, name=name, grid=(S // tm,), in_specs=[row, row, vec, row],
        out_specs=[row, row, vec],
        out_shape=[_sds((S, D), F32), _sds((S, D), BF), _sds((1, D), F32)],
        compiler_params=_cparams(("arbitrary",)),
    )(dh, x, g, dres)


def _loss_head(x2, target, g):
    S, D = x2.shape
    tm = _tile(S, 256)

    def body(x_ref, t_ref, g_ref, loss_ref, dx_ref, dxb_ref, dg_ref):
        xv = x_ref[...]
        gv = g_ref[...]
        r = lax.rsqrt(jnp.mean(xv * xv, axis=-1, keepdims=True) + EPS)
        n = xv * r
        e = n * gv - t_ref[...]
        dy = e * (1.0 / D)
        dyg = dy * gv
        dx = r * (dyg - n * jnp.mean(dyg * n, axis=-1, keepdims=True))
        dx_ref[...] = dx
        dxb_ref[...] = dx.astype(BF)

        @pl.when(pl.program_id(0) == 0)
        def _():
            dg_ref[...] = jnp.zeros_like(dg_ref)
            loss_ref[...] = jnp.zeros_like(loss_ref)

        dg_ref[...] += jnp.sum(dy * n, axis=0, keepdims=True)
        per_row = jnp.mean(e * e, axis=-1, keepdims=True)
        loss_ref[...] += 0.5 * jnp.sum(per_row, axis=0, keepdims=True)

    row = pl.BlockSpec((tm, D), lambda i: (i, 0))
    vec = pl.BlockSpec((1, D), lambda i: (0, 0))
    return pl.pallas_call(
        body, name="loss_head", grid=(S // tm,), in_specs=[row, row, vec],
        out_specs=[pl.BlockSpec((1, 1), lambda i: (0, 0)), row, row, vec],
        out_shape=[_sds((1, 1), F32), _sds((S, D), F32), _sds((S, D), BF), _sds((1, D), F32)],
        compiler_params=_cparams(("arbitrary",)),
    )(x2, target, g)


def _band_scores(qkv_ref, i, L, coef):
    q0 = pl.multiple_of(i * QB, QB)
    ks = pl.multiple_of(jnp.clip(i * QB - HALF_WINDOW, 0, L - KB), HALF_WINDOW)
    q = qkv_ref[0, pl.ds(q0, QB), :]
    k = qkv_ref[1, pl.ds(ks, KB), :]
    v = qkv_ref[2, pl.ds(ks, KB), :]
    s = lax.dot_general(q, k, (NT, ((), ())), preferred_element_type=F32) * SCALE
    qpos = q0 + lax.broadcasted_iota(jnp.int32, (QB, KB), 0)
    kpos = ks + lax.broadcasted_iota(jnp.int32, (QB, KB), 1)
    rel = jnp.abs(kpos - qpos)
    valid = rel <= HALF_WINDOW
    s = jnp.where(valid, s - coef * rel.astype(F32), NEG)
    return q0, ks, q, k, v, s, valid


def _alibi_coef(group, d):
    h = (4 * group + 1 + pl.program_id(1)).astype(F32)
    slope = jnp.exp(jnp.full((1, 1), -(8.0 / N_HEADS_A) * math.log(2.0), F32) * h)
    return slope * float(d)


def _attn_a_fwd(qkv3, group, d):
    _, S, _ = qkv3.shape
    L = S // d
    assert L % QB == 0 and L >= KB
    view = qkv3.reshape(3, L, d * QKV_W)

    def body(qkv_ref, o_ref, lse_ref):
        coef = _alibi_coef(group, d)

        def step(i, carry):
            q0, _, _, _, v, s, _ = _band_scores(qkv_ref, i, L, coef)
            m = jnp.max(s, axis=-1, keepdims=True)
            p = jnp.exp(s - m)
            den = jnp.sum(p, axis=-1, keepdims=True)
            o_ref[pl.ds(q0, QB), :] = jnp.dot((p / den).astype(BF), v, preferred_element_type=F32)
            lse_ref[pl.ds(q0, QB), :] = jnp.broadcast_to(m + jnp.log(den), (QB, HEAD_DIM))
            return carry

        lax.fori_loop(0, L // QB, step, 0)

    out = pl.BlockSpec((L, HEAD_DIM), lambda r, j: (0, r * 4 + j))
    o, lse = pl.pallas_call(
        body, name=f"attn_a_fwd_d{d}", grid=(d, 4),
        in_specs=[pl.BlockSpec((3, L, HEAD_DIM), lambda r, j: (0, 0, r * N_HEADS + 4 * group + j))],
        out_specs=[out, out],
        out_shape=[_sds((L, d * 512), F32), _sds((L, d * 512), F32)],
        compiler_params=_cparams(("parallel", "parallel")),
    )(view)
    return o.reshape(S, 512), lse.reshape(S, 512)


def _attn_a_combine(os_, lses):
    S, W = os_[0].shape
    tm = _tile(S, 512)

    def body(o0, o1, o2, l0, l1, l2, y_ref, lj_ref):
        ls = [l0[...], l1[...], l2[...]]
        m = jnp.maximum(jnp.maximum(ls[0], ls[1]), ls[2])
        es = [jnp.exp(l - m) for l in ls]
        den = es[0] + es[1] + es[2]
        y = (es[0] / den) * o0[...] + (es[1] / den) * o1[...] + (es[2] / den) * o2[...]
        y_ref[...] = y.astype(BF)
        lj_ref[...] = m + jnp.log(den)

    row = pl.BlockSpec((tm, W), lambda i: (i, 0))
    return pl.pallas_call(
        body, name="attn_a_combine", grid=(S // tm,), in_specs=[row] * 6, out_specs=[row, row],
        out_shape=[_sds((S, W), BF), _sds((S, W), F32)], compiler_params=_cparams(("parallel",)),
    )(*os_, *lses)


def _attn_a_bwd(qkv3, dy, y, lj, dqkv3, group, d):
    _, S, _ = qkv3.shape
    L = S // d
    view = qkv3.reshape(3, L, d * QKV_W)

    def body(qkv_ref, dy_ref, y_ref, lj_ref, _, out_ref, dk_acc, dv_acc):
        coef = _alibi_coef(group, d)
        dk_acc[...] = jnp.zeros_like(dk_acc)
        dv_acc[...] = jnp.zeros_like(dv_acc)

        def step(i, carry):
            q0, ks, q, k, v, s, valid = _band_scores(qkv_ref, i, L, coef)
            rows = pl.ds(q0, QB)
            dyv = dy_ref[rows, :]
            delta = jnp.sum(dyv.astype(F32) * y_ref[rows, :].astype(F32), axis=-1, keepdims=True)
            p = jnp.where(valid, jnp.exp(s - jnp.tile(lj_ref[rows, :], (1, KB // HEAD_DIM))), 0.0)
            dp = lax.dot_general(dyv, v, (NT, ((), ())), preferred_element_type=F32)
            ds = (p * (dp - delta)).astype(BF)
            out_ref[0, rows, :] = (jnp.dot(ds, k, preferred_element_type=F32) * SCALE).astype(BF)
            keys = pl.ds(ks, KB)
            dk_acc[keys, :] += lax.dot_general(ds, q, (TN, ((), ())), preferred_element_type=F32) * SCALE
            dv_acc[keys, :] += lax.dot_general(p.astype(BF), dyv, (TN, ((), ())), preferred_element_type=F32)
            return carry

        lax.fori_loop(0, L // QB, step, 0)
        out_ref[1] = dk_acc[...].astype(BF)
        out_ref[2] = dv_acc[...].astype(BF)

    heads = pl.BlockSpec((3, L, HEAD_DIM), lambda r, j: (0, 0, r * N_HEADS + 4 * group + j))
    row = pl.BlockSpec((L, HEAD_DIM), lambda r, j: (0, r * 4 + j))
    out = pl.pallas_call(
        body, name=f"attn_a_bwd_d{d}", grid=(d, 4),
        in_specs=[heads, row, row, row, pl.BlockSpec(memory_space=pl.ANY)],
        out_specs=heads, out_shape=_sds((3, L, d * QKV_W), BF),
        scratch_shapes=[pltpu.VMEM((L, HEAD_DIM), F32), pltpu.VMEM((L, HEAD_DIM), F32)],
        input_output_aliases={4: 0},
        compiler_params=_cparams(("parallel", "parallel")),
    )(view, dy.reshape(L, d * 512), y.reshape(L, d * 512), lj.reshape(L, d * 512),
      dqkv3.reshape(3, L, d * QKV_W))
    return out.reshape(3, S, QKV_W)


def _toeplitz_onehot():
    oh = np.zeros((64, GRID_W, 128), np.float32)
    for qc in range(GRID_W):
        for m in range(128):
            kc = m % GRID_W
            dc = int(np.clip(kc - qc, -(NA_COLS - 1), NA_COLS - 1)) + NA_COLS - 1
            oh[(m // GRID_W) * 32 + dc, qc, m] = 1.0
    return oh.reshape(64, GRID_W * 128)


def _nbr_scores(qkv_ref, e2_ref, r, rows, ok):
    rs = jnp.clip(r - NA_ROWS // 2, 0, rows - NA_ROWS)
    q0 = pl.multiple_of(r * GRID_W, GRID_W)
    k0 = pl.multiple_of(rs * GRID_W, GRID_W)
    q = qkv_ref[0, pl.ds(q0, GRID_W), :]
    k = qkv_ref[1, pl.ds(k0, NA_ROWS * GRID_W), :]
    v = qkv_ref[2, pl.ds(k0, NA_ROWS * GRID_W), :]
    s = lax.dot_general(q, k, (NT, ((), ())), preferred_element_type=F32) * SCALE
    first = rs - r + NA_ROWS - 1
    bias = jnp.concatenate([e2_ref[first + 2 * pair] for pair in range(NA_ROWS // 2)], axis=1)
    s = jnp.where(ok, s + bias, NEG)
    return q0, k0, first, q, k, v, s


def _nbr_col_ok():
    qc = lax.broadcasted_iota(jnp.int32, (GRID_W, NA_ROWS * GRID_W), 0)
    kc = lax.broadcasted_iota(jnp.int32, (GRID_W, NA_ROWS * GRID_W), 1) % GRID_W
    cs = jnp.clip(qc - NA_COLS // 2, 0, GRID_W - NA_COLS)
    return (kc >= cs) & (kc < cs + NA_COLS)


def _attn_b_fwd(qkv3, e2):
    _, S, _ = qkv3.shape
    rows = S // GRID_W
    assert rows >= NA_ROWS

    def body(qkv_ref, e2_ref, o_ref, lse_ref):
        ok = _nbr_col_ok()

        def step(r, carry):
            q0, _, _, _, _, v, s = _nbr_scores(qkv_ref, e2_ref, r, rows, ok)
            m = jnp.max(s, axis=-1, keepdims=True)
            p = jnp.exp(s - m)
            den = jnp.sum(p, axis=-1, keepdims=True)
            o = jnp.dot((p / den).astype(BF), v, preferred_element_type=F32)
            o_ref[pl.ds(q0, GRID_W), :] = o.astype(BF)
            lse_ref[pl.ds(q0, GRID_W), :] = jnp.broadcast_to(m + jnp.log(den), (GRID_W, HEAD_DIM))
            return carry

        lax.fori_loop(0, rows, step, 0)

    out = pl.BlockSpec((S, HEAD_DIM), lambda h: (0, h))
    return pl.pallas_call(
        body, name="attn_b_fwd", grid=(4,),
        in_specs=[pl.BlockSpec((3, S, HEAD_DIM), lambda h: (0, 0, N_HEADS_A + h)),
                  pl.BlockSpec((None, RPB_ROWS - 1, GRID_W, 128), lambda h: (h, 0, 0, 0))],
        out_specs=[out, out], out_shape=[_sds((S, 512), BF), _sds((S, 512), F32)],
        compiler_params=_cparams(("parallel",)),
    )(qkv3, e2)


def _attn_b_bwd(qkv3, e2, dy, y, lse, dqkv3):
    _, S, _ = qkv3.shape
    rows = S // GRID_W
    nk = NA_ROWS * GRID_W

    def body(qkv_ref, e2_ref, dy_ref, y_ref, lse_ref, _, out_ref, de2_ref, dk_acc, dv_acc):
        ok = _nbr_col_ok()
        dk_acc[...] = jnp.zeros_like(dk_acc)
        dv_acc[...] = jnp.zeros_like(dv_acc)
        de2_ref[...] = jnp.zeros_like(de2_ref)

        def step(r, carry):
            q0, k0, first, q, k, v, s = _nbr_scores(qkv_ref, e2_ref, r, rows, ok)
            qrows = pl.ds(q0, GRID_W)
            dyv = dy_ref[qrows, :]
            delta = jnp.sum(dyv.astype(F32) * y_ref[qrows, :].astype(F32), axis=-1, keepdims=True)
            p = jnp.where(ok, jnp.exp(s - jnp.tile(lse_ref[qrows, :], (1, nk // HEAD_DIM))), 0.0)
            dp = lax.dot_general(dyv, v, (NT, ((), ())), preferred_element_type=F32)
            ds = p * (dp - delta)
            for pair in range(NA_ROWS // 2):
                de2_ref[first + 2 * pair] += ds[:, pair * 128:(pair + 1) * 128]
            dsb = ds.astype(BF)
            out_ref[0, qrows, :] = (jnp.dot(dsb, k, preferred_element_type=F32) * SCALE).astype(BF)
            keys = pl.ds(k0, nk)
            dk_acc[keys, :] += lax.dot_general(dsb, q, (TN, ((), ())), preferred_element_type=F32) * SCALE
            dv_acc[keys, :] += lax.dot_general(p.astype(BF), dyv, (TN, ((), ())), preferred_element_type=F32)
            return carry

        lax.fori_loop(0, rows, step, 0)
        out_ref[1] = dk_acc[...].astype(BF)
        out_ref[2] = dv_acc[...].astype(BF)

    heads = pl.BlockSpec((3, S, HEAD_DIM), lambda h: (0, 0, N_HEADS_A + h))
    row = pl.BlockSpec((S, HEAD_DIM), lambda h: (0, h))
    table = pl.BlockSpec((None, RPB_ROWS - 1, GRID_W, 128), lambda h: (h, 0, 0, 0))
    return pl.pallas_call(
        body, name="attn_b_bwd", grid=(4,),
        in_specs=[heads, table, row, row, row, pl.BlockSpec(memory_space=pl.ANY)],
        out_specs=[heads, table],
        out_shape=[_sds((3, S, QKV_W), BF), _sds((4, RPB_ROWS - 1, GRID_W, 128), F32)],
        scratch_shapes=[pltpu.VMEM((S, HEAD_DIM), F32), pltpu.VMEM((S, HEAD_DIM), F32)],
        input_output_aliases={5: 0},
        compiler_params=_cparams(("parallel",)),
    )(qkv3, e2, dy, y, lse, dqkv3)


def _rpb_to_table(rpb):
    pad = jnp.pad(rpb, ((0, 0), (0, 0), (0, 1)))
    pairs = jnp.concatenate([pad[:, :-1], pad[:, 1:]], axis=-1).reshape(4 * (RPB_ROWS - 1), 64)
    onehot = jnp.asarray(_toeplitz_onehot())
    n = onehot.shape[1]
    tn = 2048
    full = lambda i, j, k: (0, 0)
    (e2,) = _matmul("rpb_table", pairs, onehot, pl.BlockSpec(pairs.shape, full),
                    pl.BlockSpec((64, tn), lambda i, j, k: (0, j)), NN, (1, n // tn, 1), (pairs.shape[0], tn), [],
                    [(_sds((pairs.shape[0], n), F32), pl.BlockSpec((pairs.shape[0], tn), lambda i, j, k: (0, j)))],
                    _store(F32), precision=lax.Precision.HIGHEST)
    return e2.reshape(4, RPB_ROWS - 1, GRID_W, 128)


def _table_grad_to_rpb(de2):
    onehot = jnp.asarray(_toeplitz_onehot())
    n = onehot.shape[1]
    flat = de2.reshape(4 * (RPB_ROWS - 1), n)
    tk = 2048
    (dpairs,) = _matmul("rpb_table_grad", flat, onehot, pl.BlockSpec((flat.shape[0], tk), lambda i, j, k: (0, k)),
                        pl.BlockSpec((64, tk), lambda i, j, k: (0, k)), NT, (1, 1, n // tk), (flat.shape[0], 64), [],
                        [(_sds((flat.shape[0], 64), F32), pl.BlockSpec((flat.shape[0], 64), lambda i, j, k: (0, 0)))],
                        _store(F32), precision=lax.Precision.HIGHEST)
    dpairs = dpairs.reshape(4, RPB_ROWS - 1, 64)
    zero = jnp.zeros((4, 1, RPB_COLS), F32)
    return (jnp.concatenate([dpairs[:, :, :RPB_COLS], zero], axis=1)
            + jnp.concatenate([zero, dpairs[:, :, 32:32 + RPB_COLS]], axis=1))


HBM = pl.BlockSpec(memory_space=pl.ANY)


def _place():
    x, y, c = lax.axis_index("x"), lax.axis_index("y"), lax.axis_index("c")
    chips = [(1 - x, y), (x, 1 - y), (1 - x, 1 - y)]
    return x, y, c, chips


def _remote(src, dst, send_sem, recv_sem, to):
    return pltpu.make_async_remote_copy(src_ref=src, dst_ref=dst, send_sem=send_sem, recv_sem=recv_sem,
                                        device_id=to, device_id_type=MESH)


def _gather_weights(shards):
    n = len(shards)

    def body(*refs):
        ins, outs = refs[:n], refs[n:2 * n]
        send_sems, recv_sems, local_sems = refs[2 * n:]
        x, y, c, chips = _place()
        me = 2 * x + y
        sibling = (x, y, 1 - c)
        local_copies, sends = [], []
        for w in range(n):
            half = ins[w].shape[0] // 2
            mine = pl.ds(c * half, half)
            local = pltpu.make_async_copy(ins[w], outs[w].at[me], local_sems.at[w])
            local.start()
            local_copies.append(local)
            for j, chip in enumerate(chips):
                cp = _remote(ins[w].at[mine, :], outs[w].at[me, mine, :], send_sems.at[w, j], recv_sems.at[w, j],
                             (*chip, c))
                cp.start()
                sends.append(cp)
        for w in range(n):
            half = ins[w].shape[0] // 2
            mine = pl.ds(c * half, half)
            for j, chip in enumerate(chips):
                landed = outs[w].at[2 * chip[0] + chip[1], mine, :]
                _remote(landed, landed, send_sems.at[w, j], recv_sems.at[w, j], (*chip, c)).wait_recv()
                cp = _remote(landed, landed, send_sems.at[w, 3 + j], recv_sems.at[w, 3 + j], sibling)
                cp.start()
                sends.append(cp)
        for w in range(n):
            half = ins[w].shape[0] // 2
            theirs = pl.ds((1 - c) * half, half)
            for j, chip in enumerate(chips):
                landed = outs[w].at[2 * chip[0] + chip[1], theirs, :]
                _remote(landed, landed, send_sems.at[w, 3 + j], recv_sems.at[w, 3 + j], sibling).wait_recv()
        for cp in sends:
            cp.wait_send()
        for local in local_copies:
            local.wait()

    return pl.pallas_call(
        body, name="gather_weights",
        out_shape=[_sds((N_CHIPS,) + s.shape, s.dtype) for s in shards],
        in_specs=[HBM] * n, out_specs=[HBM] * n,
        scratch_shapes=[pltpu.SemaphoreType.DMA((n, 6)), pltpu.SemaphoreType.DMA((n, 6)),
                        pltpu.SemaphoreType.DMA((n,))],
    )(*shards)


def _swap_halves(partials):
    n = len(partials)

    def body(*refs):
        ins, outs = refs[:n], refs[n:2 * n]
        send_sems, recv_sems = refs[2 * n:]
        x, y, c, _ = _place()
        copies = []
        for w in range(n):
            half = ins[w].shape[1] // 2
            cp = _remote(ins[w].at[:, pl.ds((1 - c) * half, half), :], outs[w], send_sems.at[w], recv_sems.at[w],
                         (x, y, 1 - c))
            cp.start()
            copies.append(cp)
        for cp in copies:
            cp.wait()

    return pl.pallas_call(
        body, name="grad_swap_halves",
        out_shape=[_sds((p.shape[0], p.shape[1] // 2, p.shape[2]), p.dtype) for p in partials],
        in_specs=[HBM] * n, out_specs=[HBM] * n,
        scratch_shapes=[pltpu.SemaphoreType.DMA((n,)), pltpu.SemaphoreType.DMA((n,))],
    )(*partials)


def _scatter_chip_sums(sums):
    n = len(sums)

    def body(*refs):
        ins, outs = refs[:n], refs[n:2 * n]
        send_sems, recv_sems = refs[2 * n:]
        _, _, c, chips = _place()
        copies = []
        for w in range(n):
            for j, chip in enumerate(chips):
                cp = _remote(ins[w].at[2 * chip[0] + chip[1]], outs[w].at[j], send_sems.at[w, j], recv_sems.at[w, j],
                             (*chip, c))
                cp.start()
                copies.append(cp)
        for cp in copies:
            cp.wait()

    return pl.pallas_call(
        body, name="grad_scatter",
        out_shape=[_sds((3,) + s.shape[1:], s.dtype) for s in sums],
        in_specs=[HBM] * n, out_specs=[HBM] * n,
        scratch_shapes=[pltpu.SemaphoreType.DMA((n, 3)), pltpu.SemaphoreType.DMA((n, 3))],
    )(*sums)


def _join_halves(halves):
    n = len(halves)

    def body(*refs):
        ins, outs = refs[:n], refs[n:2 * n]
        send_sems, recv_sems, local_sems = refs[2 * n:]
        x, y, c, _ = _place()
        copies = []
        for w in range(n):
            half = ins[w].shape[0]
            mine = pl.ds(c * half, half)
            local = pltpu.make_async_copy(ins[w], outs[w].at[mine, :], local_sems.at[w])
            local.start()
            cp = _remote(ins[w], outs[w].at[mine, :], send_sems.at[w], recv_sems.at[w], (x, y, 1 - c))
            cp.start()
            copies.append((local, cp))
        for w, (local, cp) in enumerate(copies):
            half = ins[w].shape[0]
            theirs = outs[w].at[pl.ds((1 - c) * half, half), :]
            cp.wait_send()
            _remote(ins[w], theirs, send_sems.at[w], recv_sems.at[w], (x, y, 1 - c)).wait_recv()
            local.wait()

    return pl.pallas_call(
        body, name="grad_join_halves",
        out_shape=[_sds((2 * h.shape[0], h.shape[1]), h.dtype) for h in halves],
        in_specs=[HBM] * n, out_specs=[HBM] * n,
        scratch_shapes=[pltpu.SemaphoreType.DMA((n,)), pltpu.SemaphoreType.DMA((n,)),
                        pltpu.SemaphoreType.DMA((n,))],
    )(*halves)


def _gather_small(vec):
    m_per, n = vec.shape

    def body(x_ref, out_ref, send_sems, recv_sems, local_sem):
        x, y, c, chips = _place()
        me, sibling = (x, y, c), (x, y, 1 - c)

        def rows(px, py, pc):
            return out_ref.at[pl.ds((4 * px + 2 * py + pc) * m_per, m_per), :]

        def copy(k, block, to, src=None):
            return _remote(rows(*block) if src is None else src, rows(*block), send_sems.at[k], recv_sems.at[k], to)

        mine = pltpu.make_async_copy(x_ref, rows(*me), local_sem)
        mine.start()
        first = [copy(0, me, sibling, src=x_ref)]
        first += [copy(1 + j, me, (*chip, c), src=x_ref) for j, chip in enumerate(chips)]
        for cp in first:
            cp.start()
        passed = [copy(4 + j, (*chip, c), sibling) for j, chip in enumerate(chips)]
        for j, chip in enumerate(chips):
            copy(1 + j, (*chip, c), me).wait_recv()
            passed[j].start()
        copy(0, sibling, me).wait_recv()
        for j, chip in enumerate(chips):
            copy(4 + j, (*chip, 1 - c), me).wait_recv()
        for cp in first + passed:
            cp.wait_send()
        mine.wait()

    return pl.pallas_call(
        body, name="gather_small_grads",
        out_shape=_sds((8 * m_per, n), vec.dtype),
        in_specs=[pl.BlockSpec(memory_space=pltpu.VMEM)], out_specs=pl.BlockSpec(memory_space=pltpu.VMEM),
        scratch_shapes=[pltpu.SemaphoreType.DMA((7,)), pltpu.SemaphoreType.DMA((7,)), pltpu.SemaphoreType.DMA],
    )(vec)


def _add_sibling(name, partial, received, c):
    _, R, C = partial.shape
    half = R // 2
    tr = _tile(half, 256)
    nb = half // tr

    def body(c_ref, p_ref, r_ref, o_ref):
        o_ref[...] = (p_ref[...].astype(F32) + r_ref[...].astype(F32)).astype(BF)

    return pl.pallas_call(
        body, name=name,
        grid_spec=pltpu.PrefetchScalarGridSpec(
            num_scalar_prefetch=1, grid=(N_CHIPS, nb),
            in_specs=[pl.BlockSpec((None, tr, C), lambda j, i, cr: (j, cr[0] * nb + i, 0)),
                      pl.BlockSpec((None, tr, C), lambda j, i, cr: (j, i, 0))],
            out_specs=pl.BlockSpec((None, tr, C), lambda j, i, cr: (j, i, 0))),
        out_shape=_sds((N_CHIPS, half, C), BF), compiler_params=_cparams(("parallel", "parallel")),
    )(c, partial, received)


def _add_chips(name, sums, received, me):
    _, half, C = sums.shape
    tr = _tile(half, 256)

    def body(me_ref, s_ref, r_ref, o_ref):
        acc = s_ref[...].astype(F32)
        for j in range(3):
            acc = acc + r_ref[j].astype(F32)
        o_ref[...] = acc

    return pl.pallas_call(
        body, name=name,
        grid_spec=pltpu.PrefetchScalarGridSpec(
            num_scalar_prefetch=1, grid=(half // tr,),
            in_specs=[pl.BlockSpec((None, tr, C), lambda i, mr: (mr[0], i, 0)),
                      pl.BlockSpec((3, tr, C), lambda i, mr: (0, i, 0))],
            out_specs=pl.BlockSpec((tr, C), lambda i, mr: (i, 0))),
        out_shape=_sds((half, C), F32), compiler_params=_cparams(("parallel",)),
    )(me, sums, received)


def _adamw_math(w, g, m, v):
    m = ADAM_B1 * m + (1.0 - ADAM_B1) * g
    v = ADAM_B2 * v + (1.0 - ADAM_B2) * (g * g)
    m_hat = m / (1.0 - ADAM_B1 ** ADAM_STEP)
    v_hat = v / (1.0 - ADAM_B2 ** ADAM_STEP)
    delta = -ADAM_LR * (m_hat / (jnp.sqrt(v_hat) + ADAM_EPS) + ADAM_WD * w)
    return delta, m, v


def _adamw(name, w, g, m, v):
    R, C = w.shape
    tr = _tile(R, 128)

    def body(w_ref, g_ref, m_ref, v_ref, go_ref, d_ref, mo_ref, vo_ref):
        gv = g_ref[...]
        go_ref[...] = gv
        d_ref[...], mo_ref[...], vo_ref[...] = _adamw_math(w_ref[...], gv, m_ref[...], v_ref[...])

    row = pl.BlockSpec((tr, C), lambda i: (i, 0))
    return pl.pallas_call(
        body, name=name, grid=(R // tr,), in_specs=[row] * 4, out_specs=[row] * 4,
        out_shape=[_sds((R, C), F32)] * 4, compiler_params=_cparams(("parallel",)),
    )(w, g, m, v)


def _adamw_small(gathered, w, m, v):
    rows, n = w.shape

    def body(ga_ref, w_ref, m_ref, v_ref, go_ref, d_ref, mo_ref, vo_ref):
        g = ga_ref[pl.ds(0, rows), :]
        for dev in range(1, 8):
            g = g + ga_ref[pl.ds(dev * rows, rows), :]
        go_ref[...] = g
        d_ref[...], mo_ref[...], vo_ref[...] = _adamw_math(w_ref[...], g, m_ref[...], v_ref[...])

    return pl.pallas_call(
        body, name="adamw_small", out_shape=[_sds((rows, n), F32)] * 4,
        compiler_params=_cparams(),
    )(gathered, w, m, v)


def _forward_backward(x, target, norm_mix, b_gate, rpb, norm_mlp, norm_final, gq, gg, gpa, gpb, gout, gup, gdown):
    S, D = x.shape
    F = gup.shape[2] * N_CHIPS
    wout = gout.reshape(D, D)
    wdown = gdown.reshape(F, D)

    h1 = _rms_fwd("rms_mix", x, norm_mix)
    nq = QKV_W // 512
    (qkv3,), _ = _mm_nn_cols(
        "qkv", h1, gq, BF, tn=512,
        outs=[(_sds((3, S, QKV_W), BF), pl.BlockSpec((None, _tile(S, 1024), 512), lambda i, j, k: (j // nq, i, j % nq)))])

    tg = _tile(gg.shape[2], 1024)
    ng = D // tg

    def gate_epilogue(acc, ex, outs):
        outs[0][...] = jax.nn.sigmoid(acc + ex[0][...])

    (g3,), _ = _mm_nn_cols(
        "gate", h1, gg, F32, epilogue=gate_epilogue, tn=tg,
        extras=[(b_gate, pl.BlockSpec((1, tg), lambda i, j, k: (0, j)))],
        outs=[(_sds((2, S, D), F32), pl.BlockSpec((None, _tile(S, 1024), tg), lambda i, j, k: (j // ng, i, j % ng)))])

    outs_a = [_attn_a_fwd(qkv3, grp, d) for grp, d in enumerate(DILATIONS)]
    y_a, lj = _attn_a_combine([o for o, _ in outs_a], [l for _, l in outs_a])
    e2 = _rpb_to_table(rpb)
    y_b, lse_b = _attn_b_fwd(qkv3, e2)

    (pa,), (tm, tp, _) = _mm_nn_cols("proj_a", y_a, gpa, F32, tn=512)

    def merge_epilogue(acc, ex, outs):
        g = ex[0][...]
        outs[0][...] = acc
        outs[1][...] = (g[0] * ex[1][...] + g[1] * acc).astype(BF)

    tile = pl.BlockSpec((tm, tp), lambda i, j, k: (i, j))
    gates = pl.BlockSpec((2, tm, tp), lambda i, j, k: (0, i, j))
    (pb, merged), _ = _mm_nn_cols(
        "proj_b_merge", y_b, gpb, F32, epilogue=merge_epilogue, tn=512,
        extras=[(g3, gates), (pa, tile)],
        outs=[(_sds((S, D), F32), tile), (_sds((S, D), BF), tile)])

    def residual_epilogue(acc, ex, outs):
        outs[0][...] = acc + ex[0][...]

    def nn_plain(name, a, w, res):
        M, K = a.shape
        N = w.shape[1]
        bm, bn, bk = _tile(M, 1024), _tile(N, 1024), _tile(K, 1024)
        t = pl.BlockSpec((bm, bn), lambda i, j, k: (i, j))
        return _matmul(name, a, w, pl.BlockSpec((bm, bk), lambda i, j, k: (i, k)),
                       pl.BlockSpec((bk, bn), lambda i, j, k: (k, j)), NN, (M // bm, N // bn, K // bk), (bm, bn),
                       [(res, t)], [(_sds((M, N), F32), t)], residual_epilogue)[0]

    x1 = nn_plain("out_proj", merged, wout, x)
    h2 = _rms_fwd("rms_mlp", x1, norm_mlp)

    def up_epilogue(acc, ex, outs):
        ru = jnp.maximum(acc, 0.0)
        outs[0][...] = (ru * ru).astype(BF)
        outs[1][...] = ru.astype(BF)

    tu = _tile(gup.shape[2], 1024)
    ut = pl.BlockSpec((_tile(S, 1024), tu), lambda i, j, k: (i, j))
    (act, ru), _ = _mm_nn_cols("mlp_up", h2, gup, BF, epilogue=up_epilogue, tn=tu,
                               outs=[(_sds((S, F), BF), ut), (_sds((S, F), BF), ut)])
    x2 = nn_plain("mlp_down", act, wdown, x1)

    loss, dx2, dx2b, d_norm_final = _loss_head(x2, target, norm_final.reshape(1, D))

    def nt_rows(name, a, w, epilogue, extras, outs, bn=1024):
        M, N = a.shape
        K = w.shape[0]
        bm, bn, bk = _tile(M, 1024), _tile(K, bn), _tile(N, 1024)
        return _matmul(name, a, w, pl.BlockSpec((bm, bk), lambda i, j, k: (i, k)),
                       pl.BlockSpec((bn, bk), lambda i, j, k: (j, k)), NT, (M // bm, K // bn, N // bk), (bm, bn),
                       extras(bm, bn), outs(bm, bn), epilogue)

    def nt_cols(name, a_spec_fn, a, g, M, epilogue, extras, outs, bk):
        _, K, Nq = g.shape
        bm, bn, bk = _tile(M, 1024), _tile(K, 1024), _tile(Nq, bk)
        q = Nq // bk
        return _matmul(name, a, g, a_spec_fn(bm, bk), pl.BlockSpec((None, bn, bk), lambda i, j, k: (k // q, j, k % q)),
                       NT, (M // bm, K // bn, N_CHIPS * q), (bm, bn), extras(bm, bn), outs(bm, bn), epilogue)

    def tn_grad(name, a, a_spec_fn, b, b_spec_fn, Kin, N, out_shape, out_spec_fn, bn=1024):
        bm, bn, bk = _tile(Kin, 1024), _tile(N, bn), _tile(S, 1024)
        return _matmul(name, a, b, a_spec_fn(bk, bm), b_spec_fn(bk, bn), TN, (Kin // bm, N // bn, S // bk), (bm, bn),
                       [], [(_sds(out_shape, BF), out_spec_fn(bm, bn))], _store(BF))[0]

    plain_a = lambda bk, bm: pl.BlockSpec((bk, bm), lambda i, j, k: (k, i))
    plain_b = lambda bk, bn: pl.BlockSpec((bk, bn), lambda i, j, k: (k, j))
    plain_o = lambda bm, bn: pl.BlockSpec((bm, bn), lambda i, j, k: (i, j))
    a_rows = lambda bm, bk: pl.BlockSpec((bm, bk), lambda i, j, k: (i, k))

    def cols_o(Nq):
        def spec(bm, bn):
            q = Nq // bn
            return pl.BlockSpec((None, bm, bn), lambda i, j, k: (j // q, i, j % q))
        return spec

    def du_epilogue(acc, ex, outs):
        outs[0][...] = (acc * (2.0 * ex[0][...].astype(F32))).astype(BF)

    (du,) = nt_rows("mlp_down_dx", dx2b, wdown, du_epilogue,
                    lambda bm, bn: [(ru, plain_o(bm, bn))], lambda bm, bn: [(_sds((S, F), BF), plain_o(bm, bn))])
    dw_down = tn_grad("mlp_down_dw", act, plain_a, dx2b, plain_b, F, D, (F, D), plain_o)

    fq = gup.shape[2]
    (dh2,) = nt_cols("mlp_up_dx", a_rows, du, gup, S, _store(F32), lambda bm, bn: [],
                     lambda bm, bn: [(_sds((S, D), F32), plain_o(bm, bn))], 1024)
    dw_up = tn_grad("mlp_up_dw", h2, plain_a, du, plain_b, D, F, (N_CHIPS, D, fq), cols_o(fq), bn=min(fq, 1024))
    dx1, dx1b, d_norm_mlp = _rms_bwd("rms_mlp_bwd", dh2, x1, norm_mlp, dx2)

    def merge_bwd_epilogue(acc, ex, outs):
        g, pav, pbv = ex[0][...], ex[1][...], ex[2][...]
        outs[0][...] = (acc * g[0]).astype(BF)
        outs[1][...] = (acc * g[1]).astype(BF)
        dga = acc * pav * g[0] * (1.0 - g[0])
        dgb = acc * pbv * g[1] * (1.0 - g[1])
        outs[2][0] = dga.astype(BF)
        outs[2][1] = dgb.astype(BF)
        outs[3][...] = jnp.concatenate([jnp.sum(dga, axis=0, keepdims=True), jnp.sum(dgb, axis=0, keepdims=True)], 0)

    def pair(bm, bn):
        return pl.BlockSpec((2, bm, bn), lambda i, j, k: (0, i, j))

    n_row_blocks = S // _tile(S, 1024)
    dpa, dpb, dg3, db_gate = nt_rows(
        "out_proj_dx", dx1b, wout, merge_bwd_epilogue,
        lambda bm, bn: [(g3, pair(bm, bn)), (pa, plain_o(bm, bn)), (pb, plain_o(bm, bn))],
        lambda bm, bn: [(_sds((S, D), BF), plain_o(bm, bn)), (_sds((S, D), BF), plain_o(bm, bn)),
                        (_sds((2, S, D), BF), pair(bm, bn)),
                        (_sds((n_row_blocks, 2, D), F32), pl.BlockSpec((None, 2, bn), lambda i, j, k: (i, 0, j)))],
        bn=512)
    dw_out = tn_grad("out_proj_dw", merged, plain_a, dx1b, plain_b, D, D, (D, D), plain_o)

    pq = gpa.shape[2]
    proj_dx = lambda name, dproj, g: nt_cols(name, a_rows, dproj, g, S, _store(BF), lambda bm, bn: [],
                                             lambda bm, bn: [(_sds((S, 512), BF), plain_o(bm, bn))], 512)[0]
    dy_a = proj_dx("proj_a_dx", dpa, gpa)
    dy_b = proj_dx("proj_b_dx", dpb, gpb)
    dw_pa = tn_grad("proj_a_dw", y_a, plain_a, dpa, plain_b, 512, D, (N_CHIPS, 512, pq), cols_o(pq), bn=min(pq, 512))
    dw_pb = tn_grad("proj_b_dw", y_b, plain_a, dpb, plain_b, 512, D, (N_CHIPS, 512, pq), cols_o(pq), bn=min(pq, 512))

    dqkv3 = lax.empty((3, S, QKV_W), BF)
    for grp, d in enumerate(DILATIONS):
        dqkv3 = _attn_a_bwd(qkv3, dy_a, y_a, lj, dqkv3, grp, d)
    dqkv3, de2 = _attn_b_bwd(qkv3, e2, dy_b, y_b, lse_b, dqkv3)
    d_rpb = _table_grad_to_rpb(de2)

    def stacked_a(width):
        def spec(bm, bk):
            q = width // bk
            return pl.BlockSpec((None, bm, bk), lambda i, j, k: (k // q, i, k % q))
        return spec

    def stacked_b(width):
        def spec(bk, bn):
            q = width // bn
            return pl.BlockSpec((None, bk, bn), lambda i, j, k: (j // q, k, j % q))
        return spec

    (dh1_q,) = nt_cols("qkv_dx", stacked_a(QKV_W), dqkv3, gq, S, _store(F32), lambda bm, bn: [],
                       lambda bm, bn: [(_sds((S, D), F32), plain_o(bm, bn))], 512)

    def add_epilogue(acc, ex, outs):
        outs[0][...] = acc + ex[0][...]

    (dh1,) = nt_cols("gate_dx", stacked_a(D), dg3, gg, S, add_epilogue, lambda bm, bn: [(dh1_q, plain_o(bm, bn))],
                     lambda bm, bn: [(_sds((S, D), F32), plain_o(bm, bn))], gg.shape[2])
    dw_qkv = tn_grad("qkv_dw", h1, plain_a, dqkv3, stacked_b(QKV_W), D, 3 * QKV_W, (N_CHIPS,) + gq.shape[1:],
                     cols_o(gq.shape[2]), bn=512)
    dw_gate = tn_grad("gate_dw", h1, plain_a, dg3, stacked_b(D), D, 2 * D, (N_CHIPS,) + gg.shape[1:],
                      cols_o(gg.shape[2]), bn=gg.shape[2])
    grad_x, _, d_norm_mix = _rms_bwd("rms_mix_bwd", dh1, x, norm_mix, dx1)

    partials = [dw_qkv, dw_gate, dw_pa, dw_pb, dw_out.reshape(N_CHIPS, D // N_CHIPS, D), dw_up,
                dw_down.reshape(N_CHIPS, F // N_CHIPS, D)]
    small = [d_norm_mix, jnp.sum(db_gate, axis=0).reshape(1, 2 * D), d_rpb, d_norm_mlp, d_norm_final]
    return loss, grad_x, partials, small


def _pack_small(parts, width):
    flat = jnp.concatenate([p.reshape(-1) for p in parts])
    return jnp.pad(flat, (0, 8 * width - flat.shape[0])).reshape(8, width)


def kernel(x, norm_mix, w_qkv, w_gate, b_gate, rpb, w_proj_a, w_proj_b, w_out, norm_mlp, w_up, w_down, norm_final, loss_target, m_norm_mix, m_w_qkv, m_w_gate, m_b_gate, m_rpb, m_w_proj_a, m_w_proj_b, m_w_out, m_norm_mlp, m_w_up, m_w_down, m_norm_final, v_norm_mix, v_w_qkv, v_w_gate, v_b_gate, v_rpb, v_w_proj_a, v_w_proj_b, v_w_out, v_norm_mlp, v_w_up, v_w_down, v_norm_final):
    big = [w_qkv[0], w_gate[0], w_proj_a[0], w_proj_b[0], w_out[0], w_up[0], w_down[0]]
    big_m = [m_w_qkv[0], m_w_gate[0], m_w_proj_a[0], m_w_proj_b[0], m_w_out[0], m_w_up[0], m_w_down[0]]
    big_v = [v_w_qkv[0], v_w_gate[0], v_w_proj_a[0], v_w_proj_b[0], v_w_out[0], v_w_up[0], v_w_down[0]]
    names = ["qkv", "gate", "proj_a", "proj_b", "out", "up", "down"]

    gathered = _gather_weights([w.astype(BF) for w in big])
    loss, grad_x, partials, small = _forward_backward(
        x[0], loss_target[0], norm_mix, b_gate, rpb[0], norm_mlp, norm_final, *gathered)

    c = lax.axis_index("c").astype(jnp.int32).reshape(1)
    me = (2 * lax.axis_index("x") + lax.axis_index("y")).astype(jnp.int32).reshape(1)
    from_sibling = _swap_halves(partials)
    chip_sums = [_add_sibling(f"grad_add_sibling_{n}", p, r, c) for n, p, r in zip(names, partials, from_sibling)]
    from_chips = _scatter_chip_sums(chip_sums)
    halves = [_add_chips(f"grad_add_chips_{n}", s, r, me) for n, s, r in zip(names, chip_sums, from_chips)]
    grads = _join_halves(halves)
    big_out = [_adamw(f"adamw_{n}", w, g, m, v) for n, w, g, m, v in zip(names, big, grads, big_m, big_v)]

    small_w = [norm_mix, b_gate, rpb, norm_mlp, norm_final]
    count = sum(int(np.prod(p.shape)) for p in small_w)
    width = -(-count // (8 * 128)) * 128
    packed = _adamw_small(_gather_small(_pack_small(small, width)), _pack_small(small_w, width),
                          _pack_small([m_norm_mix, m_b_gate, m_rpb, m_norm_mlp, m_norm_final], width),
                          _pack_small([v_norm_mix, v_b_gate, v_rpb, v_norm_mlp, v_norm_final], width))

    def unpack(flat2d):
        flat, out, at = flat2d.reshape(-1), [], 0
        for p in small_w:
            size = int(np.prod(p.shape))
            out.append(flat[at:at + size].reshape(p.shape))
            at += size
        return out

    small_out = [unpack(a) for a in packed]

    def ordered(kind):
        sm = small_out[kind]
        bg = [o[kind][None] for o in big_out]
        return [sm[0], bg[0], bg[1], sm[1], sm[2], bg[2], bg[3], bg[4], sm[3], bg[5], bg[6], sm[4]]

    total = lax.psum(loss[0, 0], ("x", "y", "c"))
    return (total, grad_x[None], *ordered(0), *ordered(1), *ordered(2), *ordered(3))
```

```python
import functools
import math

import numpy as np
import jax
import jax.numpy as jnp
from jax import lax
from jax.experimental import pallas as pl
from jax.experimental.pallas import tpu as pltpu

BF = jnp.bfloat16
F32 = jnp.float32
MESH = pl.DeviceIdType.MESH

HEAD_DIM = 128
N_HEADS = 16
N_HEADS_A = 12
QKV_W = N_HEADS * HEAD_DIM
DILATIONS = (1, 4, 16)
HALF_WINDOW = 64
GRID_W = 64
NA_ROWS = 8
NA_COLS = 16
RPB_ROWS = 2 * NA_ROWS - 1
RPB_COLS = 2 * NA_COLS - 1
EPS = 1e-6
NEG = -1e30
SCALE = HEAD_DIM ** -0.5

ADAM_LR = 0.001
ADAM_B1 = 0.9
ADAM_B2 = 0.999
ADAM_EPS = 1e-08
ADAM_WD = 0.01
ADAM_STEP = 10

N_CHIPS = 4
VMEM_LIMIT_BYTES = 48 * 1024 * 1024
QB = 128
KB = QB + 2 * HALF_WINDOW


def _cparams(sem=None):
    return pltpu.CompilerParams(dimension_semantics=sem, vmem_limit_bytes=VMEM_LIMIT_BYTES)


def _tile(dim, want):
    t = min(dim, want)
    assert dim % t == 0, (dim, want)
    return t


NN = ((1,), (0,))
NT = ((1,), (1,))
TN = ((0,), (0,))


def _matmul(name, a, b, a_spec, b_spec, dims, grid, acc_shape, extras, outs, epilogue, precision=None):
    n_ex, n_out, nk = len(extras), len(outs), grid[2]

    def body(*refs):
        a_ref, b_ref = refs[0], refs[1]
        ex_refs = refs[2:2 + n_ex]
        out_refs = refs[2 + n_ex:2 + n_ex + n_out]
        acc_ref = refs[-1]
        k = pl.program_id(2)

        @pl.when(k == 0)
        def _():
            acc_ref[...] = jnp.zeros_like(acc_ref)

        acc_ref[...] += lax.dot_general(a_ref[...], b_ref[...], (dims, ((), ())),
                                        preferred_element_type=F32, precision=precision)

        @pl.when(k == nk - 1)
        def _():
            epilogue(acc_ref[...], ex_refs, out_refs)

    return pl.pallas_call(
        body, name=name, grid=grid,
        in_specs=[a_spec, b_spec] + [s for _, s in extras],
        out_specs=[s for _, s in outs],
        out_shape=[sh for sh, _ in outs],
        scratch_shapes=[pltpu.VMEM(acc_shape, F32)],
        compiler_params=_cparams(("parallel", "parallel", "arbitrary")),
    )(a, b, *[e for e, _ in extras])


def _store(dtype):
    def epilogue(acc, ex, outs):
        outs[0][...] = acc.astype(dtype)
    return epilogue


def _sds(shape, dtype):
    return jax.ShapeDtypeStruct(shape, dtype)


def _mm_nn_cols(name, a, g, out_dtype, epilogue=None, extras=(), outs=None, tm=1024, tn=1024, tk=1024):
    M, K = a.shape
    _, _, Nq = g.shape
    tm, tn, tk = _tile(M, tm), _tile(Nq, tn), _tile(K, tk)
    q = Nq // tn
    grid = (M // tm, N_CHIPS * q, K // tk)
    if outs is None:
        outs = [(_sds((M, N_CHIPS * Nq), out_dtype), pl.BlockSpec((tm, tn), lambda i, j, k: (i, j)))]
    return _matmul(name, a, g, pl.BlockSpec((tm, tk), lambda i, j, k: (i, k)),
                   pl.BlockSpec((None, tk, tn), lambda i, j, k: (j // q, k, j % q)), NN, grid, (tm, tn),
                   list(extras), outs, epilogue or _store(out_dtype)), (tm, tn, tk)


def _rms_fwd(name, x, g):
    S, D = x.shape
    tm = _tile(S, 256)

    def body(x_ref, g_ref, h_ref):
        xv = x_ref[...]
        r = lax.rsqrt(jnp.mean(xv * xv, axis=-1, keepdims=True) + EPS)
        h_ref[...] = ((xv * r) * g_ref[...]).astype(BF)

    row = pl.BlockSpec((tm, D), lambda i: (i, 0))
    return pl.pallas_call(
        body, name=name, grid=(S // tm,), in_specs=[row, pl.BlockSpec((1, D), lambda i: (0, 0))],
        out_specs=row, out_shape=_sds((S, D), BF), compiler_params=_cparams(("parallel",)),
    )(x, g)


def _rms_bwd(name, dh, x, g, dres):
    S, D = x.shape
    tm = _tile(S, 256)

    def body(dh_ref, x_ref, g_ref, dres_ref, dx_ref, dxb_ref, dg_ref):
        xv = x_ref[...]
        r = lax.rsqrt(jnp.mean(xv * xv, axis=-1, keepdims=True) + EPS)
        n = xv * r
        dhv = dh_ref[...]
        dyg = dhv * g_ref[...]
        dx = dres_ref[...] + r * (dyg - n * jnp.mean(dyg * n, axis=-1, keepdims=True))
        dx_ref[...] = dx
        dxb_ref[...] = dx.astype(BF)

        @pl.when(pl.program_id(0) == 0)
        def _():
            dg_ref[...] = jnp.zeros_like(dg_ref)

        dg_ref[...] += jnp.sum(dhv * n, axis=0, keepdims=True)

    row = pl.BlockSpec((tm, D), lambda i: (i, 0))
    vec = pl.BlockSpec((1, D), lambda i: (0, 0))
    return pl.pallas_call(
        body, name=name, grid=(S // tm,), in_specs=[row, row, vec, row],
        out_specs=[row, row, vec],
        out_shape=[_sds((S, D), F32), _sds((S, D), BF), _sds((1, D), F32)],
        compiler_params=_cparams(("arbitrary",)),
    )(dh, x, g, dres)


def _loss_head(x2, target, g):
    S, D = x2.shape
    tm = _tile(S, 256)

    def body(x_ref, t_ref, g_ref, loss_ref, dx_ref, dxb_ref, dg_ref):
        xv = x_ref[...]
        gv = g_ref[...]
        r = lax.rsqrt(jnp.mean(xv * xv, axis=-1, keepdims=True) + EPS)
        n = xv * r
        e = n * gv - t_ref[...]
        dy = e * (1.0 / D)
        dyg = dy * gv
        dx = r * (dyg - n * jnp.mean(dyg * n, axis=-1, keepdims=True))
        dx_ref[...] = dx
        dxb_ref[...] = dx.astype(BF)

        @pl.when(pl.program_id(0) == 0)
        def _():
            dg_ref[...] = jnp.zeros_like(dg_ref)
            loss_ref[...] = jnp.zeros_like(loss_ref)

        dg_ref[...] += jnp.sum(dy * n, axis=0, keepdims=True)
        per_row = jnp.mean(e * e, axis=-1, keepdims=True)
        loss_ref[...] += 0.5 * jnp.sum(per_row, axis=0, keepdims=True)

    row = pl.BlockSpec((tm, D), lambda i: (i, 0))
    vec = pl.BlockSpec((1, D), lambda i: (0, 0))
    return pl.pallas_call(
        body, name="loss_head", grid=(S // tm,), in_specs=[row, row, vec],
        out_specs=[pl.BlockSpec((1, 1), lambda i: (0, 0)), row, row, vec],
        out_shape=[_sds((1, 1), F32), _sds((S, D), F32), _sds((S, D), BF), _sds((1, D), F32)],
        compiler_params=_cparams(("arbitrary",)),
    )(x2, target, g)


def _band_scores(qkv_ref, i, L, coef):
    q0 = pl.multiple_of(i * QB, QB)
    ks = pl.multiple_of(jnp.clip(i * QB - HALF_WINDOW, 0, L - KB), HALF_WINDOW)
    q = qkv_ref[0, pl.ds(q0, QB), :]
    k = qkv_ref[1, pl.ds(ks, KB), :]
    v = qkv_ref[2, pl.ds(ks, KB), :]
    s = lax.dot_general(q, k, (NT, ((), ())), preferred_element_type=F32) * SCALE
    qpos = q0 + lax.broadcasted_iota(jnp.int32, (QB, KB), 0)
    kpos = ks + lax.broadcasted_iota(jnp.int32, (QB, KB), 1)
    rel = jnp.abs(kpos - qpos)
    valid = rel <= HALF_WINDOW
    s = jnp.where(valid, s - coef * rel.astype(F32), NEG)
    return q0, ks, q, k, v, s, valid


def _alibi_coef(group, d):
    h = (4 * group + 1 + pl.program_id(1)).astype(F32)
    slope = jnp.exp(jnp.full((1, 1), -(8.0 / N_HEADS_A) * math.log(2.0), F32) * h)
    return slope * float(d)


def _attn_a_fwd(qkv3, group, d):
    _, S, _ = qkv3.shape
    L = S // d
    assert L % QB == 0 and L >= KB
    view = qkv3.reshape(3, L, d * QKV_W)

    def body(qkv_ref, o_ref, lse_ref):
        coef = _alibi_coef(group, d)

        def step(i, carry):
            q0, _, _, _, v, s, _ = _band_scores(qkv_ref, i, L, coef)
            m = jnp.max(s, axis=-1, keepdims=True)
            p = jnp.exp(s - m)
            den = jnp.sum(p, axis=-1, keepdims=True)
            o_ref[pl.ds(q0, QB), :] = jnp.dot((p / den).astype(BF), v, preferred_element_type=F32)
            lse_ref[pl.ds(q0, QB), :] = jnp.broadcast_to(m + jnp.log(den), (QB, HEAD_DIM))
            return carry

        lax.fori_loop(0, L // QB, step, 0)

    out = pl.BlockSpec((L, HEAD_DIM), lambda r, j: (0, r * 4 + j))
    o, lse = pl.pallas_call(
        body, name=f"attn_a_fwd_d{d}", grid=(d, 4),
        in_specs=[pl.BlockSpec((3, L, HEAD_DIM), lambda r, j: (0, 0, r * N_HEADS + 4 * group + j))],
        out_specs=[out, out],
        out_shape=[_sds((L, d * 512), F32), _sds((L, d * 512), F32)],
        compiler_params=_cparams(("parallel", "parallel")),
    )(view)
    return o.reshape(S, 512), lse.reshape(S, 512)


def _attn_a_combine(os_, lses):
    S, W = os_[0].shape
    tm = _tile(S, 512)

    def body(o0, o1, o2, l0, l1, l2, y_ref, lj_ref):
        ls = [l0[...], l1[...], l2[...]]
        m = jnp.maximum(jnp.maximum(ls[0], ls[1]), ls[2])
        es = [jnp.exp(l - m) for l in ls]
        den = es[0] + es[1] + es[2]
        y = (es[0] / den) * o0[...] + (es[1] / den) * o1[...] + (es[2] / den) * o2[...]
        y_ref[...] = y.astype(BF)
        lj_ref[...] = m + jnp.log(den)

    row = pl.BlockSpec((tm, W), lambda i: (i, 0))
    return pl.pallas_call(
        body, name="attn_a_combine", grid=(S // tm,), in_specs=[row] * 6, out_specs=[row, row],
        out_shape=[_sds((S, W), BF), _sds((S, W), F32)], compiler_params=_cparams(("parallel",)),
    )(*os_, *lses)


def _attn_a_bwd(qkv3, dy, y, lj, dqkv3, group, d):
    _, S, _ = qkv3.shape
    L = S // d
    view = qkv3.reshape(3, L, d * QKV_W)

    def body(qkv_ref, dy_ref, y_ref, lj_ref, _, out_ref, dk_acc, dv_acc):
        coef = _alibi_coef(group, d)
        dk_acc[...] = jnp.zeros_like(dk_acc)
        dv_acc[...] = jnp.zeros_like(dv_acc)

        def step(i, carry):
            q0, ks, q, k, v, s, valid = _band_scores(qkv_ref, i, L, coef)
            rows = pl.ds(q0, QB)
            dyv = dy_ref[rows, :]
            delta = jnp.sum(dyv.astype(F32) * y_ref[rows, :].astype(F32), axis=-1, keepdims=True)
            p = jnp.where(valid, jnp.exp(s - jnp.tile(lj_ref[rows, :], (1, KB // HEAD_DIM))), 0.0)
            dp = lax.dot_general(dyv, v, (NT, ((), ())), preferred_element_type=F32)
            ds = (p * (dp - delta)).astype(BF)
            out_ref[0, rows, :] = (jnp.dot(ds, k, preferred_element_type=F32) * SCALE).astype(BF)
            keys = pl.ds(ks, KB)
            dk_acc[keys, :] += lax.dot_general(ds, q, (TN, ((), ())), preferred_element_type=F32) * SCALE
            dv_acc[keys, :] += lax.dot_general(p.astype(BF), dyv, (TN, ((), ())), preferred_element_type=F32)
            return carry

        lax.fori_loop(0, L // QB, step, 0)
        out_ref[1] = dk_acc[...].astype(BF)
        out_ref[2] = dv_acc[...].astype(BF)

    heads = pl.BlockSpec((3, L, HEAD_DIM), lambda r, j: (0, 0, r * N_HEADS + 4 * group + j))
    row = pl.BlockSpec((L, HEAD_DIM), lambda r, j: (0, r * 4 + j))
    out = pl.pallas_call(
        body, name=f"attn_a_bwd_d{d}", grid=(d, 4),
        in_specs=[heads, row, row, row, pl.BlockSpec(memory_space=pl.ANY)],
        out_specs=heads, out_shape=_sds((3, L, d * QKV_W), BF),
        scratch_shapes=[pltpu.VMEM((L, HEAD_DIM), F32), pltpu.VMEM((L, HEAD_DIM), F32)],
        input_output_aliases={4: 0},
        compiler_params=_cparams(("parallel", "parallel")),
    )(view, dy.reshape(L, d * 512), y.reshape(L, d * 512), lj.reshape(L, d * 512),
      dqkv3.reshape(3, L, d * QKV_W))
    return out.reshape(3, S, QKV_W)


def _toeplitz_onehot():
    oh = np.zeros((64, GRID_W, 128), np.float32)
    for qc in range(GRID_W):
        for m in range(128):
            kc = m % GRID_W
            dc = int(np.clip(kc - qc, -(NA_COLS - 1), NA_COLS - 1)) + NA_COLS - 1
            oh[(m // GRID_W) * 32 + dc, qc, m] = 1.0
    return oh.reshape(64, GRID_W * 128)


def _nbr_scores(qkv_ref, e2_ref, r, rows, ok):
    rs = jnp.clip(r - NA_ROWS // 2, 0, rows - NA_ROWS)
    q0 = pl.multiple_of(r * GRID_W, GRID_W)
    k0 = pl.multiple_of(rs * GRID_W, GRID_W)
    q = qkv_ref[0, pl.ds(q0, GRID_W), :]
    k = qkv_ref[1, pl.ds(k0, NA_ROWS * GRID_W), :]
    v = qkv_ref[2, pl.ds(k0, NA_ROWS * GRID_W), :]
    s = lax.dot_general(q, k, (NT, ((), ())), preferred_element_type=F32) * SCALE
    first = rs - r + NA_ROWS - 1
    bias = jnp.concatenate([e2_ref[first + 2 * pair] for pair in range(NA_ROWS // 2)], axis=1)
    s = jnp.where(ok, s + bias, NEG)
    return q0, k0, first, q, k, v, s


def _nbr_col_ok():
    qc = lax.broadcasted_iota(jnp.int32, (GRID_W, NA_ROWS * GRID_W), 0)
    kc = lax.broadcasted_iota(jnp.int32, (GRID_W, NA_ROWS * GRID_W), 1) % GRID_W
    cs = jnp.clip(qc - NA_COLS // 2, 0, GRID_W - NA_COLS)
    return (kc >= cs) & (kc < cs + NA_COLS)


def _attn_b_fwd(qkv3, e2):
    _, S, _ = qkv3.shape
    rows = S // GRID_W
    assert rows >= NA_ROWS

    def body(qkv_ref, e2_ref, o_ref, lse_ref):
        ok = _nbr_col_ok()

        def step(r, carry):
            q0, _, _, _, _, v, s = _nbr_scores(qkv_ref, e2_ref, r, rows, ok)
            m = jnp.max(s, axis=-1, keepdims=True)
            p = jnp.exp(s - m)
            den = jnp.sum(p, axis=-1, keepdims=True)
            o = jnp.dot((p / den).astype(BF), v, preferred_element_type=F32)
            o_ref[pl.ds(q0, GRID_W), :] = o.astype(BF)
            lse_ref[pl.ds(q0, GRID_W), :] = jnp.broadcast_to(m + jnp.log(den), (GRID_W, HEAD_DIM))
            return carry

        lax.fori_loop(0, rows, step, 0)

    out = pl.BlockSpec((S, HEAD_DIM), lambda h: (0, h))
    return pl.pallas_call(
        body, name="attn_b_fwd", grid=(4,),
        in_specs=[pl.BlockSpec((3, S, HEAD_DIM), lambda h: (0, 0, N_HEADS_A + h)),
                  pl.BlockSpec((None, RPB_ROWS - 1, GRID_W, 128), lambda h: (h, 0, 0, 0))],
        out_specs=[out, out], out_shape=[_sds((S, 512), BF), _sds((S, 512), F32)],
        compiler_params=_cparams(("parallel",)),
    )(qkv3, e2)


def _attn_b_bwd(qkv3, e2, dy, y, lse, dqkv3):
    _, S, _ = qkv3.shape
    rows = S // GRID_W
    nk = NA_ROWS * GRID_W

    def body(qkv_ref, e2_ref, dy_ref, y_ref, lse_ref, _, out_ref, de2_ref, dk_acc, dv_acc):
        ok = _nbr_col_ok()
        dk_acc[...] = jnp.zeros_like(dk_acc)
        dv_acc[...] = jnp.zeros_like(dv_acc)
        de2_ref[...] = jnp.zeros_like(de2_ref)

        def step(r, carry):
            q0, k0, first, q, k, v, s = _nbr_scores(qkv_ref, e2_ref, r, rows, ok)
            qrows = pl.ds(q0, GRID_W)
            dyv = dy_ref[qrows, :]
            delta = jnp.sum(dyv.astype(F32) * y_ref[qrows, :].astype(F32), axis=-1, keepdims=True)
            p = jnp.where(ok, jnp.exp(s - jnp.tile(lse_ref[qrows, :], (1, nk // HEAD_DIM))), 0.0)
            dp = lax.dot_general(dyv, v, (NT, ((), ())), preferred_element_type=F32)
            ds = p * (dp - delta)
            for pair in range(NA_ROWS // 2):
                de2_ref[first + 2 * pair] += ds[:, pair * 128:(pair + 1) * 128]
            dsb = ds.astype(BF)
            out_ref[0, qrows, :] = (jnp.dot(dsb, k, preferred_element_type=F32) * SCALE).astype(BF)
            keys = pl.ds(k0, nk)
            dk_acc[keys, :] += lax.dot_general(dsb, q, (TN, ((), ())), preferred_element_type=F32) * SCALE
            dv_acc[keys, :] += lax.dot_general(p.astype(BF), dyv, (TN, ((), ())), preferred_element_type=F32)
            return carry

        lax.fori_loop(0, rows, step, 0)
        out_ref[1] = dk_acc[...].astype(BF)
        out_ref[2] = dv_acc[...].astype(BF)

    heads = pl.BlockSpec((3, S, HEAD_DIM), lambda h: (0, 0, N_HEADS_A + h))
    row = pl.BlockSpec((S, HEAD_DIM), lambda h: (0, h))
    table = pl.BlockSpec((None, RPB_ROWS - 1, GRID_W, 128), lambda h: (h, 0, 0, 0))
    return pl.pallas_call(
        body, name="attn_b_bwd", grid=(4,),
        in_specs=[heads, table, row, row, row, pl.BlockSpec(memory_space=pl.ANY)],
        out_specs=[heads, table],
        out_shape=[_sds((3, S, QKV_W), BF), _sds((4, RPB_ROWS - 1, GRID_W, 128), F32)],
        scratch_shapes=[pltpu.VMEM((S, HEAD_DIM), F32), pltpu.VMEM((S, HEAD_DIM), F32)],
        input_output_aliases={5: 0},
        compiler_params=_cparams(("parallel",)),
    )(qkv3, e2, dy, y, lse, dqkv3)


def _rpb_to_table(rpb):
    pad = jnp.pad(rpb, ((0, 0), (0, 0), (0, 1)))
    pairs = jnp.concatenate([pad[:, :-1], pad[:, 1:]], axis=-1).reshape(4 * (RPB_ROWS - 1), 64)
    onehot = jnp.asarray(_toeplitz_onehot())
    n = onehot.shape[1]
    tn = 2048
    full = lambda i, j, k: (0, 0)
    (e2,) = _matmul("rpb_table", pairs, onehot, pl.BlockSpec(pairs.shape, full),
                    pl.BlockSpec((64, tn), lambda i, j, k: (0, j)), NN, (1, n // tn, 1), (pairs.shape[0], tn), [],
                    [(_sds((pairs.shape[0], n), F32), pl.BlockSpec((pairs.shape[0], tn), lambda i, j, k: (0, j)))],
                    _store(F32), precision=lax.Precision.HIGHEST)
    return e2.reshape(4, RPB_ROWS - 1, GRID_W, 128)


def _table_grad_to_rpb(de2):
    onehot = jnp.asarray(_toeplitz_onehot())
    n = onehot.shape[1]
    flat = de2.reshape(4 * (RPB_ROWS - 1), n)
    tk = 2048
    (dpairs,) = _matmul("rpb_table_grad", flat, onehot, pl.BlockSpec((flat.shape[0], tk), lambda i, j, k: (0, k)),
                        pl.BlockSpec((64, tk), lambda i, j, k: (0, k)), NT, (1, 1, n // tk), (flat.shape[0], 64), [],
                        [(_sds((flat.shape[0], 64), F32), pl.BlockSpec((flat.shape[0], 64), lambda i, j, k: (0, 0)))],
                        _store(F32), precision=lax.Precision.HIGHEST)
    dpairs = dpairs.reshape(4, RPB_ROWS - 1, 64)
    zero = jnp.zeros((4, 1, RPB_COLS), F32)
    return (jnp.concatenate([dpairs[:, :, :RPB_COLS], zero], axis=1)
            + jnp.concatenate([zero, dpairs[:, :, 32:32 + RPB_COLS]], axis=1))


HBM = pl.BlockSpec(memory_space=pl.ANY)


def _place():
    x, y, c = lax.axis_index("x"), lax.axis_index("y"), lax.axis_index("c")
    chips = [(1 - x, y), (x, 1 - y), (1 - x, 1 - y)]
    return x, y, c, chips


def _remote(src, dst, send_sem, recv_sem, to):
    return pltpu.make_async_remote_copy(src_ref=src, dst_ref=dst, send_sem=send_sem, recv_sem=recv_sem,
                                        device_id=to, device_id_type=MESH)


def _place_shard(name, w, me):
    R, C = w.shape
    tr = _tile(R, 256)

    def body(me_ref, w_ref, o_ref):
        o_ref[...] = w_ref[...].astype(BF)

    return pl.pallas_call(
        body, name=name,
        grid_spec=pltpu.PrefetchScalarGridSpec(
            num_scalar_prefetch=1, grid=(R // tr,),
            in_specs=[pl.BlockSpec((tr, C), lambda i, mr: (i, 0))],
            out_specs=pl.BlockSpec((None, tr, C), lambda i, mr: (mr[0], i, 0))),
        out_shape=_sds((N_CHIPS, R, C), BF), compiler_params=_cparams(("parallel",)),
    )(me, w)


def _gather_weights(bufs):
    n = len(bufs)

    def body(*refs):
        bufs_ = refs[n:2 * n]
        send_sems, recv_sems = refs[2 * n:]
        x, y, c, chips = _place()
        me = 2 * x + y
        sibling = (x, y, 1 - c)
        sends = []
        for w in range(n):
            half = bufs_[w].shape[1] // 2
            mine = bufs_[w].at[me, pl.ds(c * half, half), :]
            for j, chip in enumerate(chips):
                cp = _remote(mine, mine, send_sems.at[w, j], recv_sems.at[w, j], (*chip, c))
                cp.start()
                sends.append(cp)
        for w in range(n):
            half = bufs_[w].shape[1] // 2
            for j, chip in enumerate(chips):
                landed = bufs_[w].at[2 * chip[0] + chip[1], pl.ds(c * half, half), :]
                _remote(landed, landed, send_sems.at[w, j], recv_sems.at[w, j], (*chip, c)).wait_recv()
                cp = _remote(landed, landed, send_sems.at[w, 3 + j], recv_sems.at[w, 3 + j], sibling)
                cp.start()
                sends.append(cp)
        for w in range(n):
            half = bufs_[w].shape[1] // 2
            for j, chip in enumerate(chips):
                landed = bufs_[w].at[2 * chip[0] + chip[1], pl.ds((1 - c) * half, half), :]
                _remote(landed, landed, send_sems.at[w, 3 + j], recv_sems.at[w, 3 + j], sibling).wait_recv()
        for cp in sends:
            cp.wait_send()

    return pl.pallas_call(
        body, name="gather_weights",
        out_shape=[_sds(b.shape, b.dtype) for b in bufs],
        in_specs=[HBM] * n, out_specs=[HBM] * n,
        input_output_aliases={w: w for w in range(n)},
        scratch_shapes=[pltpu.SemaphoreType.DMA((n, 6)), pltpu.SemaphoreType.DMA((n, 6))],
    )(*bufs)


def _swap_halves(partials):
    n = len(partials)

    def body(*refs):
        ins, outs = refs[:n], refs[n:2 * n]
        send_sems, recv_sems = refs[2 * n:]
        x, y, c, _ = _place()
        copies = []
        for w in range(n):
            half = ins[w].shape[1] // 2
            cp = _remote(ins[w].at[:, pl.ds((1 - c) * half, half), :], outs[w], send_sems.at[w], recv_sems.at[w],
                         (x, y, 1 - c))
            cp.start()
            copies.append(cp)
        for cp in copies:
            cp.wait()

    return pl.pallas_call(
        body, name="grad_swap_halves",
        out_shape=[_sds((p.shape[0], p.shape[1] // 2, p.shape[2]), p.dtype) for p in partials],
        in_specs=[HBM] * n, out_specs=[HBM] * n,
        scratch_shapes=[pltpu.SemaphoreType.DMA((n,)), pltpu.SemaphoreType.DMA((n,))],
    )(*partials)


def _scatter_chip_sums(sums):
    n = len(sums)

    def body(*refs):
        ins, outs = refs[:n], refs[n:2 * n]
        send_sems, recv_sems = refs[2 * n:]
        _, _, c, chips = _place()
        copies = []
        for w in range(n):
            for j, chip in enumerate(chips):
                cp = _remote(ins[w].at[2 * chip[0] + chip[1]], outs[w].at[j], send_sems.at[w, j], recv_sems.at[w, j],
                             (*chip, c))
                cp.start()
                copies.append(cp)
        for cp in copies:
            cp.wait()

    return pl.pallas_call(
        body, name="grad_scatter",
        out_shape=[_sds((3,) + s.shape[1:], s.dtype) for s in sums],
        in_specs=[HBM] * n, out_specs=[HBM] * n,
        scratch_shapes=[pltpu.SemaphoreType.DMA((n, 3)), pltpu.SemaphoreType.DMA((n, 3))],
    )(*sums)


def _join_halves(bufs):
    n = len(bufs)

    def body(*refs):
        bufs_ = refs[n:2 * n]
        send_sems, recv_sems = refs[2 * n:]
        x, y, c, _ = _place()
        copies = []
        for w in range(n):
            half = bufs_[w].shape[0] // 2
            mine = bufs_[w].at[pl.ds(c * half, half), :]
            cp = _remote(mine, mine, send_sems.at[w], recv_sems.at[w], (x, y, 1 - c))
            cp.start()
            copies.append(cp)
        for w, cp in enumerate(copies):
            half = bufs_[w].shape[0] // 2
            theirs = bufs_[w].at[pl.ds((1 - c) * half, half), :]
            cp.wait_send()
            _remote(theirs, theirs, send_sems.at[w], recv_sems.at[w], (x, y, 1 - c)).wait_recv()

    return pl.pallas_call(
        body, name="grad_join_halves",
        out_shape=[_sds(b.shape, b.dtype) for b in bufs],
        in_specs=[HBM] * n, out_specs=[HBM] * n,
        input_output_aliases={w: w for w in range(n)},
        scratch_shapes=[pltpu.SemaphoreType.DMA((n,)), pltpu.SemaphoreType.DMA((n,))],
    )(*bufs)


def _gather_small(vec):
    m_per, n = vec.shape

    def body(x_ref, out_ref, send_sems, recv_sems, local_sem):
        x, y, c, chips = _place()
        me, sibling = (x, y, c), (x, y, 1 - c)

        def rows(px, py, pc):
            return out_ref.at[pl.ds((4 * px + 2 * py + pc) * m_per, m_per), :]

        def copy(k, block, to, src=None):
            return _remote(rows(*block) if src is None else src, rows(*block), send_sems.at[k], recv_sems.at[k], to)

        mine = pltpu.make_async_copy(x_ref, rows(*me), local_sem)
        mine.start()
        first = [copy(0, me, sibling, src=x_ref)]
        first += [copy(1 + j, me, (*chip, c), src=x_ref) for j, chip in enumerate(chips)]
        for cp in first:
            cp.start()
        passed = [copy(4 + j, (*chip, c), sibling) for j, chip in enumerate(chips)]
        for j, chip in enumerate(chips):
            copy(1 + j, (*chip, c), me).wait_recv()
            passed[j].start()
        copy(0, sibling, me).wait_recv()
        for j, chip in enumerate(chips):
            copy(4 + j, (*chip, 1 - c), me).wait_recv()
        for cp in first + passed:
            cp.wait_send()
        mine.wait()

    return pl.pallas_call(
        body, name="gather_small_grads",
        out_shape=_sds((8 * m_per, n), vec.dtype),
        in_specs=[pl.BlockSpec(memory_space=pltpu.VMEM)], out_specs=pl.BlockSpec(memory_space=pltpu.VMEM),
        scratch_shapes=[pltpu.SemaphoreType.DMA((7,)), pltpu.SemaphoreType.DMA((7,)), pltpu.SemaphoreType.DMA],
    )(vec)


def _add_sibling(name, partial, received, c):
    _, R, C = partial.shape
    half = R // 2
    tr = _tile(half, 256)
    nb = half // tr

    def body(c_ref, p_ref, r_ref, o_ref):
        o_ref[...] = (p_ref[...].astype(F32) + r_ref[...].astype(F32)).astype(BF)

    return pl.pallas_call(
        body, name=name,
        grid_spec=pltpu.PrefetchScalarGridSpec(
            num_scalar_prefetch=1, grid=(N_CHIPS, nb),
            in_specs=[pl.BlockSpec((None, tr, C), lambda j, i, cr: (j, cr[0] * nb + i, 0)),
                      pl.BlockSpec((None, tr, C), lambda j, i, cr: (j, i, 0))],
            out_specs=pl.BlockSpec((None, tr, C), lambda j, i, cr: (j, i, 0))),
        out_shape=_sds((N_CHIPS, half, C), BF), compiler_params=_cparams(("parallel", "parallel")),
    )(c, partial, received)


def _add_chips(name, sums, received, me_c):
    _, half, C = sums.shape
    tr = _tile(half, 256)
    nb = half // tr

    def body(mc_ref, s_ref, r_ref, o_ref):
        acc = s_ref[...].astype(F32)
        for j in range(3):
            acc = acc + r_ref[j].astype(F32)
        o_ref[...] = acc

    return pl.pallas_call(
        body, name=name,
        grid_spec=pltpu.PrefetchScalarGridSpec(
            num_scalar_prefetch=1, grid=(nb,),
            in_specs=[pl.BlockSpec((None, tr, C), lambda i, mc: (mc[0], i, 0)),
                      pl.BlockSpec((3, tr, C), lambda i, mc: (0, i, 0))],
            out_specs=pl.BlockSpec((tr, C), lambda i, mc: (mc[1] * nb + i, 0))),
        out_shape=_sds((2 * half, C), F32), compiler_params=_cparams(("parallel",)),
    )(me_c, sums, received)


def _adamw_math(w, g, m, v):
    m = ADAM_B1 * m + (1.0 - ADAM_B1) * g
    v = ADAM_B2 * v + (1.0 - ADAM_B2) * (g * g)
    m_hat = m / (1.0 - ADAM_B1 ** ADAM_STEP)
    v_hat = v / (1.0 - ADAM_B2 ** ADAM_STEP)
    delta = -ADAM_LR * (m_hat / (jnp.sqrt(v_hat) + ADAM_EPS) + ADAM_WD * w)
    return delta, m, v


def _adamw(name, w, g, m, v):
    R, C = w.shape
    tr = _tile(R, 128)

    def body(w_ref, g_ref, m_ref, v_ref, go_ref, d_ref, mo_ref, vo_ref):
        gv = g_ref[...]
        go_ref[...] = gv
        d_ref[...], mo_ref[...], vo_ref[...] = _adamw_math(w_ref[...], gv, m_ref[...], v_ref[...])

    row = pl.BlockSpec((tr, C), lambda i: (i, 0))
    return pl.pallas_call(
        body, name=name, grid=(R // tr,), in_specs=[row] * 4, out_specs=[row] * 4,
        out_shape=[_sds((R, C), F32)] * 4, compiler_params=_cparams(("parallel",)),
    )(w, g, m, v)


def _adamw_small(gathered, w, m, v):
    rows, n = w.shape

    def body(ga_ref, w_ref, m_ref, v_ref, go_ref, d_ref, mo_ref, vo_ref):
        g = ga_ref[pl.ds(0, rows), :]
        for dev in range(1, 8):
            g = g + ga_ref[pl.ds(dev * rows, rows), :]
        go_ref[...] = g
        d_ref[...], mo_ref[...], vo_ref[...] = _adamw_math(w_ref[...], g, m_ref[...], v_ref[...])

    return pl.pallas_call(
        body, name="adamw_small", out_shape=[_sds((rows, n), F32)] * 4,
        compiler_params=_cparams(),
    )(gathered, w, m, v)


def _forward_backward(x, target, norm_mix, b_gate, rpb, norm_mlp, norm_final, gq, gg, gpa, gpb, gout, gup, gdown):
    S, D = x.shape
    F = gup.shape[2] * N_CHIPS
    wout = gout.reshape(D, D)
    wdown = gdown.reshape(F, D)

    h1 = _rms_fwd("rms_mix", x, norm_mix)
    nq = QKV_W // 512
    (qkv3,), _ = _mm_nn_cols(
        "qkv", h1, gq, BF, tn=512,
        outs=[(_sds((3, S, QKV_W), BF), pl.BlockSpec((None, _tile(S, 1024), 512), lambda i, j, k: (j // nq, i, j % nq)))])

    tg = _tile(gg.shape[2], 1024)
    ng = D // tg

    def gate_epilogue(acc, ex, outs):
        outs[0][...] = jax.nn.sigmoid(acc + ex[0][...])

    (g3,), _ = _mm_nn_cols(
        "gate", h1, gg, F32, epilogue=gate_epilogue, tn=tg,
        extras=[(b_gate, pl.BlockSpec((1, tg), lambda i, j, k: (0, j)))],
        outs=[(_sds((2, S, D), F32), pl.BlockSpec((None, _tile(S, 1024), tg), lambda i, j, k: (j // ng, i, j % ng)))])

    outs_a = [_attn_a_fwd(qkv3, grp, d) for grp, d in enumerate(DILATIONS)]
    y_a, lj = _attn_a_combine([o for o, _ in outs_a], [l for _, l in outs_a])
    e2 = _rpb_to_table(rpb)
    y_b, lse_b = _attn_b_fwd(qkv3, e2)

    (pa,), (tm, tp, _) = _mm_nn_cols("proj_a", y_a, gpa, F32, tn=512)

    def merge_epilogue(acc, ex, outs):
        g = ex[0][...]
        outs[0][...] = acc
        outs[1][...] = (g[0] * ex[1][...] + g[1] * acc).astype(BF)

    tile = pl.BlockSpec((tm, tp), lambda i, j, k: (i, j))
    gates = pl.BlockSpec((2, tm, tp), lambda i, j, k: (0, i, j))
    (pb, merged), _ = _mm_nn_cols(
        "proj_b_merge", y_b, gpb, F32, epilogue=merge_epilogue, tn=512,
        extras=[(g3, gates), (pa, tile)],
        outs=[(_sds((S, D), F32), tile), (_sds((S, D), BF), tile)])

    def residual_epilogue(acc, ex, outs):
        outs[0][...] = acc + ex[0][...]

    def nn_plain(name, a, w, res):
        M, K = a.shape
        N = w.shape[1]
        bm, bn, bk = _tile(M, 1024), _tile(N, 1024), _tile(K, 1024)
        t = pl.BlockSpec((bm, bn), lambda i, j, k: (i, j))
        return _matmul(name, a, w, pl.BlockSpec((bm, bk), lambda i, j, k: (i, k)),
                       pl.BlockSpec((bk, bn), lambda i, j, k: (k, j)), NN, (M // bm, N // bn, K // bk), (bm, bn),
                       [(res, t)], [(_sds((M, N), F32), t)], residual_epilogue)[0]

    x1 = nn_plain("out_proj", merged, wout, x)
    h2 = _rms_fwd("rms_mlp", x1, norm_mlp)

    def up_epilogue(acc, ex, outs):
        ru = jnp.maximum(acc, 0.0)
        outs[0][...] = (ru * ru).astype(BF)
        outs[1][...] = ru.astype(BF)

    tu = _tile(gup.shape[2], 1024)
    ut = pl.BlockSpec((_tile(S, 1024), tu), lambda i, j, k: (i, j))
    (act, ru), _ = _mm_nn_cols("mlp_up", h2, gup, BF, epilogue=up_epilogue, tn=tu,
                               outs=[(_sds((S, F), BF), ut), (_sds((S, F), BF), ut)])
    x2 = nn_plain("mlp_down", act, wdown, x1)

    loss, dx2, dx2b, d_norm_final = _loss_head(x2, target, norm_final.reshape(1, D))

    def nt_rows(name, a, w, epilogue, extras, outs, bn=1024):
        M, N = a.shape
        K = w.shape[0]
        bm, bn, bk = _tile(M, 1024), _tile(K, bn), _tile(N, 1024)
        return _matmul(name, a, w, pl.BlockSpec((bm, bk), lambda i, j, k: (i, k)),
                       pl.BlockSpec((bn, bk), lambda i, j, k: (j, k)), NT, (M // bm, K // bn, N // bk), (bm, bn),
                       extras(bm, bn), outs(bm, bn), epilogue)

    def nt_cols(name, a_spec_fn, a, g, M, epilogue, extras, outs, bk):
        _, K, Nq = g.shape
        bm, bn, bk = _tile(M, 1024), _tile(K, 1024), _tile(Nq, bk)
        q = Nq // bk
        return _matmul(name, a, g, a_spec_fn(bm, bk), pl.BlockSpec((None, bn, bk), lambda i, j, k: (k // q, j, k % q)),
                       NT, (M // bm, K // bn, N_CHIPS * q), (bm, bn), extras(bm, bn), outs(bm, bn), epilogue)

    def tn_grad(name, a, a_spec_fn, b, b_spec_fn, Kin, N, out_shape, out_spec_fn, bn=1024):
        bm, bn, bk = _tile(Kin, 1024), _tile(N, bn), _tile(S, 1024)
        return _matmul(name, a, b, a_spec_fn(bk, bm), b_spec_fn(bk, bn), TN, (Kin // bm, N // bn, S // bk), (bm, bn),
                       [], [(_sds(out_shape, BF), out_spec_fn(bm, bn))], _store(BF))[0]

    plain_a = lambda bk, bm: pl.BlockSpec((bk, bm), lambda i, j, k: (k, i))
    plain_b = lambda bk, bn: pl.BlockSpec((bk, bn), lambda i, j, k: (k, j))
    plain_o = lambda bm, bn: pl.BlockSpec((bm, bn), lambda i, j, k: (i, j))
    a_rows = lambda bm, bk: pl.BlockSpec((bm, bk), lambda i, j, k: (i, k))

    def cols_o(Nq):
        def spec(bm, bn):
            q = Nq // bn
            return pl.BlockSpec((None, bm, bn), lambda i, j, k: (j // q, i, j % q))
        return spec

    def du_epilogue(acc, ex, outs):
        outs[0][...] = (acc * (2.0 * ex[0][...].astype(F32))).astype(BF)

    (du,) = nt_rows("mlp_down_dx", dx2b, wdown, du_epilogue,
                    lambda bm, bn: [(ru, plain_o(bm, bn))], lambda bm, bn: [(_sds((S, F), BF), plain_o(bm, bn))])
    dw_down = tn_grad("mlp_down_dw", act, plain_a, dx2b, plain_b, F, D, (F, D), plain_o)

    fq = gup.shape[2]
    (dh2,) = nt_cols("mlp_up_dx", a_rows, du, gup, S, _store(F32), lambda bm, bn: [],
                     lambda bm, bn: [(_sds((S, D), F32), plain_o(bm, bn))], 1024)
    dw_up = tn_grad("mlp_up_dw", h2, plain_a, du, plain_b, D, F, (N_CHIPS, D, fq), cols_o(fq), bn=min(fq, 1024))
    dx1, dx1b, d_norm_mlp = _rms_bwd("rms_mlp_bwd", dh2, x1, norm_mlp, dx2)

    def merge_bwd_epilogue(acc, ex, outs):
        g, pav, pbv = ex[0][...], ex[1][...], ex[2][...]
        outs[0][...] = (acc * g[0]).astype(BF)
        outs[1][...] = (acc * g[1]).astype(BF)
        dga = acc * pav * g[0] * (1.0 - g[0])
        dgb = acc * pbv * g[1] * (1.0 - g[1])
        outs[2][0] = dga.astype(BF)
        outs[2][1] = dgb.astype(BF)
        outs[3][...] = jnp.concatenate([jnp.sum(dga, axis=0, keepdims=True), jnp.sum(dgb, axis=0, keepdims=True)], 0)

    def pair(bm, bn):
        return pl.BlockSpec((2, bm, bn), lambda i, j, k: (0, i, j))

    n_row_blocks = S // _tile(S, 1024)
    dpa, dpb, dg3, db_gate = nt_rows(
        "out_proj_dx", dx1b, wout, merge_bwd_epilogue,
        lambda bm, bn: [(g3, pair(bm, bn)), (pa, plain_o(bm, bn)), (pb, plain_o(bm, bn))],
        lambda bm, bn: [(_sds((S, D), BF), plain_o(bm, bn)), (_sds((S, D), BF), plain_o(bm, bn)),
                        (_sds((2, S, D), BF), pair(bm, bn)),
                        (_sds((n_row_blocks, 2, D), F32), pl.BlockSpec((None, 2, bn), lambda i, j, k: (i, 0, j)))],
        bn=512)
    dw_out = tn_grad("out_proj_dw", merged, plain_a, dx1b, plain_b, D, D, (D, D), plain_o)

    pq = gpa.shape[2]
    proj_dx = lambda name, dproj, g: nt_cols(name, a_rows, dproj, g, S, _store(BF), lambda bm, bn: [],
                                             lambda bm, bn: [(_sds((S, 512), BF), plain_o(bm, bn))], 512)[0]
    dy_a = proj_dx("proj_a_dx", dpa, gpa)
    dy_b = proj_dx("proj_b_dx", dpb, gpb)
    dw_pa = tn_grad("proj_a_dw", y_a, plain_a, dpa, plain_b, 512, D, (N_CHIPS, 512, pq), cols_o(pq), bn=min(pq, 512))
    dw_pb = tn_grad("proj_b_dw", y_b, plain_a, dpb, plain_b, 512, D, (N_CHIPS, 512, pq), cols_o(pq), bn=min(pq, 512))

    dqkv3 = lax.empty((3, S, QKV_W), BF)
    for grp, d in enumerate(DILATIONS):
        dqkv3 = _attn_a_bwd(qkv3, dy_a, y_a, lj, dqkv3, grp, d)
    dqkv3, de2 = _attn_b_bwd(qkv3, e2, dy_b, y_b, lse_b, dqkv3)
    d_rpb = _table_grad_to_rpb(de2)

    def stacked_a(width):
        def spec(bm, bk):
            q = width // bk
            return pl.BlockSpec((None, bm, bk), lambda i, j, k: (k // q, i, k % q))
        return spec

    def stacked_b(width):
        def spec(bk, bn):
            q = width // bn
            return pl.BlockSpec((None, bk, bn), lambda i, j, k: (j // q, k, j % q))
        return spec

    (dh1_q,) = nt_cols("qkv_dx", stacked_a(QKV_W), dqkv3, gq, S, _store(F32), lambda bm, bn: [],
                       lambda bm, bn: [(_sds((S, D), F32), plain_o(bm, bn))], 512)

    def add_epilogue(acc, ex, outs):
        outs[0][...] = acc + ex[0][...]

    (dh1,) = nt_cols("gate_dx", stacked_a(D), dg3, gg, S, add_epilogue, lambda bm, bn: [(dh1_q, plain_o(bm, bn))],
                     lambda bm, bn: [(_sds((S, D), F32), plain_o(bm, bn))], gg.shape[2])
    dw_qkv = tn_grad("qkv_dw", h1, plain_a, dqkv3, stacked_b(QKV_W), D, 3 * QKV_W, (N_CHIPS,) + gq.shape[1:],
                     cols_o(gq.shape[2]), bn=512)
    dw_gate = tn_grad("gate_dw", h1, plain_a, dg3, stacked_b(D), D, 2 * D, (N_CHIPS,) + gg.shape[1:],
                      cols_o(gg.shape[2]), bn=gg.shape[2])
    grad_x, _, d_norm_mix = _rms_bwd("rms_mix_bwd", dh1, x, norm_mix, dx1)

    partials = [dw_qkv, dw_gate, dw_pa, dw_pb, dw_out.reshape(N_CHIPS, D // N_CHIPS, D), dw_up,
                dw_down.reshape(N_CHIPS, F // N_CHIPS, D)]
    small = [d_norm_mix, jnp.sum(db_gate, axis=0).reshape(1, 2 * D), d_rpb, d_norm_mlp, d_norm_final]
    return loss, grad_x, partials, small


def _pack_small(parts, width):
    flat = jnp.concatenate([p.reshape(-1) for p in parts])
    return jnp.pad(flat, (0, 8 * width - flat.shape[0])).reshape(8, width)


def kernel(x, norm_mix, w_qkv, w_gate, b_gate, rpb, w_proj_a, w_proj_b, w_out, norm_mlp, w_up, w_down, norm_final, loss_target, m_norm_mix, m_w_qkv, m_w_gate, m_b_gate, m_rpb, m_w_proj_a, m_w_proj_b, m_w_out, m_norm_mlp, m_w_up, m_w_down, m_norm_final, v_norm_mix, v_w_qkv, v_w_gate, v_b_gate, v_rpb, v_w_proj_a, v_w_proj_b, v_w_out, v_norm_mlp, v_w_up, v_w_down, v_norm_final):
    big = [w_qkv[0], w_gate[0], w_proj_a[0], w_proj_b[0], w_out[0], w_up[0], w_down[0]]
    big_m = [m_w_qkv[0], m_w_gate[0], m_w_proj_a[0], m_w_proj_b[0], m_w_out[0], m_w_up[0], m_w_down[0]]
    big_v = [v_w_qkv[0], v_w_gate[0], v_w_proj_a[0], v_w_proj_b[0], v_w_out[0], v_w_up[0], v_w_down[0]]
    names = ["qkv", "gate", "proj_a", "proj_b", "out", "up", "down"]

    c = lax.axis_index("c").astype(jnp.int32).reshape(1)
    me = (2 * lax.axis_index("x") + lax.axis_index("y")).astype(jnp.int32).reshape(1)
    gathered = _gather_weights([_place_shard(f"place_{n}", w, me) for n, w in zip(names, big)])
    loss, grad_x, partials, small = _forward_backward(
        x[0], loss_target[0], norm_mix, b_gate, rpb[0], norm_mlp, norm_final, *gathered)

    from_sibling = _swap_halves(partials)
    chip_sums = [_add_sibling(f"grad_add_sibling_{n}", p, r, c) for n, p, r in zip(names, partials, from_sibling)]
    from_chips = _scatter_chip_sums(chip_sums)
    halves = [_add_chips(f"grad_add_chips_{n}", s, r, jnp.concatenate([me, c])) for n, s, r in zip(names, chip_sums, from_chips)]
    grads = _join_halves(halves)
    big_out = [_adamw(f"adamw_{n}", w, g, m, v) for n, w, g, m, v in zip(names, big, grads, big_m, big_v)]

    small_w = [norm_mix, b_gate, rpb, norm_mlp, norm_final]
    count = sum(int(np.prod(p.shape)) for p in small_w)
    width = -(-count // (8 * 128)) * 128
    packed = _adamw_small(_gather_small(_pack_small(small, width)), _pack_small(small_w, width),
                          _pack_small([m_norm_mix, m_b_gate, m_rpb, m_norm_mlp, m_norm_final], width),
                          _pack_small([v_norm_mix, v_b_gate, v_rpb, v_norm_mlp, v_norm_final], width))

    def unpack(flat2d):
        flat, out, at = flat2d.reshape(-1), [], 0
        for p in small_w:
            size = int(np.prod(p.shape))
            out.append(flat[at:at + size].reshape(p.shape))
            at += size
        return out

    small_out = [unpack(a) for a in packed]

    def ordered(kind):
        sm = small_out[kind]
        bg = [o[kind][None] for o in big_out]
        return [sm[0], bg[0], bg[1], sm[1], sm[2], bg[2], bg[3], bg[4], sm[3], bg[5], bg[6], sm[4]]

    total = lax.psum(loss[0, 0], ("x", "y", "c"))
    return (total, grad_x[None], *ordered(0), *ordered(1), *ordered(2), *ordered(3))
```

```python
import functools
import math

import numpy as np
import jax
import jax.numpy as jnp
from jax import lax
from jax.experimental import pallas as pl
from jax.experimental.pallas import tpu as pltpu

BF = jnp.bfloat16
F32 = jnp.float32
MESH = pl.DeviceIdType.MESH

HEAD_DIM = 128
N_HEADS = 16
N_HEADS_A = 12
QKV_W = N_HEADS * HEAD_DIM
DILATIONS = (1, 4, 16)
HALF_WINDOW = 64
GRID_W = 64
NA_ROWS = 8
NA_COLS = 16
RPB_ROWS = 2 * NA_ROWS - 1
RPB_COLS = 2 * NA_COLS - 1
EPS = 1e-6
NEG = -1e30
SCALE = HEAD_DIM ** -0.5

ADAM_LR = 0.001
ADAM_B1 = 0.9
ADAM_B2 = 0.999
ADAM_EPS = 1e-08
ADAM_WD = 0.01
ADAM_STEP = 10

N_CHIPS = 4
VMEM_LIMIT_BYTES = 48 * 1024 * 1024
QB = 128
KB = QB + 2 * HALF_WINDOW


def _cparams(sem=None):
    return pltpu.CompilerParams(dimension_semantics=sem, vmem_limit_bytes=VMEM_LIMIT_BYTES)


def _tile(dim, want):
    t = min(dim, want)
    assert dim % t == 0, (dim, want)
    return t


NN = ((1,), (0,))
NT = ((1,), (1,))
TN = ((0,), (0,))


def _matmul(name, a, b, a_spec, b_spec, dims, grid, acc_shape, extras, outs, epilogue, precision=None):
    n_ex, n_out, nk = len(extras), len(outs), grid[2]

    def body(*refs):
        a_ref, b_ref = refs[0], refs[1]
        ex_refs = refs[2:2 + n_ex]
        out_refs = refs[2 + n_ex:2 + n_ex + n_out]
        acc_ref = refs[-1]
        k = pl.program_id(2)

        @pl.when(k == 0)
        def _():
            acc_ref[...] = jnp.zeros_like(acc_ref)

        acc_ref[...] += lax.dot_general(a_ref[...], b_ref[...], (dims, ((), ())),
                                        preferred_element_type=F32, precision=precision)

        @pl.when(k == nk - 1)
        def _():
            epilogue(acc_ref[...], ex_refs, out_refs)

    return pl.pallas_call(
        body, name=name, grid=grid,
        in_specs=[a_spec, b_spec] + [s for _, s in extras],
        out_specs=[s for _, s in outs],
        out_shape=[sh for sh, _ in outs],
        scratch_shapes=[pltpu.VMEM(acc_shape, F32)],
        compiler_params=_cparams(("parallel", "parallel", "arbitrary")),
    )(a, b, *[e for e, _ in extras])


def _store(dtype):
    def epilogue(acc, ex, outs):
        outs[0][...] = acc.astype(dtype)
    return epilogue


def _sds(shape, dtype):
    return jax.ShapeDtypeStruct(shape, dtype)


def _mm_nn_cols(name, a, g, out_dtype, epilogue=None, extras=(), outs=None, tm=1024, tn=1024, tk=1024):
    M, K = a.shape
    _, _, Nq = g.shape
    tm, tn, tk = _tile(M, tm), _tile(Nq, tn), _tile(K, tk)
    q = Nq // tn
    grid = (M // tm, N_CHIPS * q, K // tk)
    if outs is None:
        outs = [(_sds((M, N_CHIPS * Nq), out_dtype), pl.BlockSpec((tm, tn), lambda i, j, k: (i, j)))]
    return _matmul(name, a, g, pl.BlockSpec((tm, tk), lambda i, j, k: (i, k)),
                   pl.BlockSpec((None, tk, tn), lambda i, j, k: (j // q, k, j % q)), NN, grid, (tm, tn),
                   list(extras), outs, epilogue or _store(out_dtype)), (tm, tn, tk)


def _rms_fwd(name, x, g):
    S, D = x.shape
    tm = _tile(S, 256)

    def body(x_ref, g_ref, h_ref):
        xv = x_ref[...]
        r = lax.rsqrt(jnp.mean(xv * xv, axis=-1, keepdims=True) + EPS)
        h_ref[...] = ((xv * r) * g_ref[...]).astype(BF)

    row = pl.BlockSpec((tm, D), lambda i: (i, 0))
    return pl.pallas_call(
        body, name=name, grid=(S // tm,), in_specs=[row, pl.BlockSpec((1, D), lambda i: (0, 0))],
        out_specs=row, out_shape=_sds((S, D), BF), compiler_params=_cparams(("parallel",)),
    )(x, g)


def _rms_bwd(name, dh, x, g, dres):
    S, D = x.shape
    tm = _tile(S, 256)

    def body(dh_ref, x_ref, g_ref, dres_ref, dx_ref, dxb_ref, dg_ref):
        xv = x_ref[...]
        r = lax.rsqrt(jnp.mean(xv * xv, axis=-1, keepdims=True) + EPS)
        n = xv * r
        dhv = dh_ref[...]
        dyg = dhv * g_ref[...]
        dx = dres_ref[...] + r * (dyg - n * jnp.mean(dyg * n, axis=-1, keepdims=True))
        dx_ref[...] = dx
        dxb_ref[...] = dx.astype(BF)

        @pl.when(pl.program_id(0) == 0)
        def _():
            dg_ref[...] = jnp.zeros_like(dg_ref)

        dg_ref[...] += jnp.sum(dhv * n, axis=0, keepdims=True)

    row = pl.BlockSpec((tm, D), lambda i: (i, 0))
    vec = pl.BlockSpec((1, D), lambda i: (0, 0))
    return pl.pallas_call(
        body, name=name, grid=(S // tm,), in_specs=[row, row, vec, row],
        out_specs=[row, row, vec],
        out_shape=[_sds((S, D), F32), _sds((S, D), BF), _sds((1, D), F32)],
        compiler_params=_cparams(("arbitrary",)),
    )(dh, x, g, dres)


def _loss_head(x2, target, g):
    S, D = x2.shape
    tm = _tile(S, 256)

    def body(x_ref, t_ref, g_ref, loss_ref, dx_ref, dxb_ref, dg_ref):
        xv = x_ref[...]
        gv = g_ref[...]
        r = lax.rsqrt(jnp.mean(xv * xv, axis=-1, keepdims=True) + EPS)
        n = xv * r
        e = n * gv - t_ref[...]
        dy = e * (1.0 / D)
        dyg = dy * gv
        dx = r * (dyg - n * jnp.mean(dyg * n, axis=-1, keepdims=True))
        dx_ref[...] = dx
        dxb_ref[...] = dx.astype(BF)

        @pl.when(pl.program_id(0) == 0)
        def _():
            dg_ref[...] = jnp.zeros_like(dg_ref)
            loss_ref[...] = jnp.zeros_like(loss_ref)

        dg_ref[...] += jnp.sum(dy * n, axis=0, keepdims=True)
        per_row = jnp.mean(e * e, axis=-1, keepdims=True)
        loss_ref[...] += 0.5 * jnp.sum(per_row, axis=0, keepdims=True)

    row = pl.BlockSpec((tm, D), lambda i: (i, 0))
    vec = pl.BlockSpec((1, D), lambda i: (0, 0))
    return pl.pallas_call(
        body, name="loss_head", grid=(S // tm,), in_specs=[row, row, vec],
        out_specs=[pl.BlockSpec((1, 1), lambda i: (0, 0)), row, row, vec],
        out_shape=[_sds((1, 1), F32), _sds((S, D), F32), _sds((S, D), BF), _sds((1, D), F32)],
        compiler_params=_cparams(("arbitrary",)),
    )(x2, target, g)


def _band_scores(qkv_ref, i, L, coef):
    q0 = pl.multiple_of(i * QB, QB)
    ks = pl.multiple_of(jnp.clip(i * QB - HALF_WINDOW, 0, L - KB), HALF_WINDOW)
    q = qkv_ref[0, pl.ds(q0, QB), :]
    k = qkv_ref[1, pl.ds(ks, KB), :]
    v = qkv_ref[2, pl.ds(ks, KB), :]
    s = lax.dot_general(q, k, (NT, ((), ())), preferred_element_type=F32) * SCALE
    qpos = q0 + lax.broadcasted_iota(jnp.int32, (QB, KB), 0)
    kpos = ks + lax.broadcasted_iota(jnp.int32, (QB, KB), 1)
    rel = jnp.abs(kpos - qpos)
    valid = rel <= HALF_WINDOW
    s = jnp.where(valid, s - coef * rel.astype(F32), NEG)
    return q0, ks, q, k, v, s, valid


def _alibi_coef(group, d):
    h = (4 * group + 1 + pl.program_id(1)).astype(F32)
    slope = jnp.exp(jnp.full((1, 1), -(8.0 / N_HEADS_A) * math.log(2.0), F32) * h)
    return slope * float(d)


def _attn_a_fwd(qkv3, group, d):
    _, S, _ = qkv3.shape
    L = S // d
    assert L % QB == 0 and L >= KB
    view = qkv3.reshape(3, L, d * QKV_W)

    def body(qkv_ref, o_ref, lse_ref):
        coef = _alibi_coef(group, d)

        def step(i, carry):
            q0, _, _, _, v, s, _ = _band_scores(qkv_ref, i, L, coef)
            m = jnp.max(s, axis=-1, keepdims=True)
            p = jnp.exp(s - m)
            den = jnp.sum(p, axis=-1, keepdims=True)
            o_ref[pl.ds(q0, QB), :] = jnp.dot((p / den).astype(BF), v, preferred_element_type=F32)
            lse_ref[pl.ds(q0, QB), :] = jnp.broadcast_to(m + jnp.log(den), (QB, HEAD_DIM))
            return carry

        lax.fori_loop(0, L // QB, step, 0)

    out = pl.BlockSpec((L, HEAD_DIM), lambda r, j: (0, r * 4 + j))
    o, lse = pl.pallas_call(
        body, name=f"attn_a_fwd_d{d}", grid=(d, 4),
        in_specs=[pl.BlockSpec((3, L, HEAD_DIM), lambda r, j: (0, 0, r * N_HEADS + 4 * group + j))],
        out_specs=[out, out],
        out_shape=[_sds((L, d * 512), F32), _sds((L, d * 512), F32)],
        compiler_params=_cparams(("parallel", "parallel")),
    )(view)
    return o.reshape(S, 512), lse.reshape(S, 512)


def _attn_a_combine(os_, lses):
    S, W = os_[0].shape
    tm = _tile(S, 512)

    def body(o0, o1, o2, l0, l1, l2, y_ref, lj_ref):
        ls = [l0[...], l1[...], l2[...]]
        m = jnp.maximum(jnp.maximum(ls[0], ls[1]), ls[2])
        es = [jnp.exp(l - m) for l in ls]
        den = es[0] + es[1] + es[2]
        y = (es[0] / den) * o0[...] + (es[1] / den) * o1[...] + (es[2] / den) * o2[...]
        y_ref[...] = y.astype(BF)
        lj_ref[...] = m + jnp.log(den)

    row = pl.BlockSpec((tm, W), lambda i: (i, 0))
    return pl.pallas_call(
        body, name="attn_a_combine", grid=(S // tm,), in_specs=[row] * 6, out_specs=[row, row],
        out_shape=[_sds((S, W), BF), _sds((S, W), F32)], compiler_params=_cparams(("parallel",)),
    )(*os_, *lses)


def _attn_a_bwd(qkv3, dy, y, lj, dqkv3, group, d):
    _, S, _ = qkv3.shape
    L = S // d
    view = qkv3.reshape(3, L, d * QKV_W)

    def body(qkv_ref, dy_ref, y_ref, lj_ref, _, out_ref, dk_acc, dv_acc):
        coef = _alibi_coef(group, d)
        dk_acc[...] = jnp.zeros_like(dk_acc)
        dv_acc[...] = jnp.zeros_like(dv_acc)

        def step(i, carry):
            q0, ks, q, k, v, s, valid = _band_scores(qkv_ref, i, L, coef)
            rows = pl.ds(q0, QB)
            dyv = dy_ref[rows, :]
            delta = jnp.sum(dyv.astype(F32) * y_ref[rows, :].astype(F32), axis=-1, keepdims=True)
            p = jnp.where(valid, jnp.exp(s - jnp.tile(lj_ref[rows, :], (1, KB // HEAD_DIM))), 0.0)
            dp = lax.dot_general(dyv, v, (NT, ((), ())), preferred_element_type=F32)
            ds = (p * (dp - delta)).astype(BF)
            out_ref[0, rows, :] = (jnp.dot(ds, k, preferred_element_type=F32) * SCALE).astype(BF)
            keys = pl.ds(ks, KB)
            dk_acc[keys, :] += lax.dot_general(ds, q, (TN, ((), ())), preferred_element_type=F32) * SCALE
            dv_acc[keys, :] += lax.dot_general(p.astype(BF), dyv, (TN, ((), ())), preferred_element_type=F32)
            return carry

        lax.fori_loop(0, L // QB, step, 0)
        out_ref[1] = dk_acc[...].astype(BF)
        out_ref[2] = dv_acc[...].astype(BF)

    heads = pl.BlockSpec((3, L, HEAD_DIM), lambda r, j: (0, 0, r * N_HEADS + 4 * group + j))
    row = pl.BlockSpec((L, HEAD_DIM), lambda r, j: (0, r * 4 + j))
    out = pl.pallas_call(
        body, name=f"attn_a_bwd_d{d}", grid=(d, 4),
        in_specs=[heads, row, row, row, pl.BlockSpec(memory_space=pl.ANY)],
        out_specs=heads, out_shape=_sds((3, L, d * QKV_W), BF),
        scratch_shapes=[pltpu.VMEM((L, HEAD_DIM), F32), pltpu.VMEM((L, HEAD_DIM), F32)],
        input_output_aliases={4: 0},
        compiler_params=_cparams(("parallel", "parallel")),
    )(view, dy.reshape(L, d * 512), y.reshape(L, d * 512), lj.reshape(L, d * 512),
      dqkv3.reshape(3, L, d * QKV_W))
    return out.reshape(3, S, QKV_W)


def _toeplitz_onehot():
    oh = np.zeros((64, GRID_W, 128), np.float32)
    for qc in range(GRID_W):
        for m in range(128):
            kc = m % GRID_W
            dc = int(np.clip(kc - qc, -(NA_COLS - 1), NA_COLS - 1)) + NA_COLS - 1
            oh[(m // GRID_W) * 32 + dc, qc, m] = 1.0
    return oh.reshape(64, GRID_W * 128)


def _nbr_scores(qkv_ref, e2_ref, r, rows, ok):
    rs = jnp.clip(r - NA_ROWS // 2, 0, rows - NA_ROWS)
    q0 = pl.multiple_of(r * GRID_W, GRID_W)
    k0 = pl.multiple_of(rs * GRID_W, GRID_W)
    q = qkv_ref[0, pl.ds(q0, GRID_W), :]
    k = qkv_ref[1, pl.ds(k0, NA_ROWS * GRID_W), :]
    v = qkv_ref[2, pl.ds(k0, NA_ROWS * GRID_W), :]
    s = lax.dot_general(q, k, (NT, ((), ())), preferred_element_type=F32) * SCALE
    first = rs - r + NA_ROWS - 1
    bias = jnp.concatenate([e2_ref[first + 2 * pair] for pair in range(NA_ROWS // 2)], axis=1)
    s = jnp.where(ok, s + bias, NEG)
    return q0, k0, first, q, k, v, s


def _nbr_col_ok():
    qc = lax.broadcasted_iota(jnp.int32, (GRID_W, NA_ROWS * GRID_W), 0)
    kc = lax.broadcasted_iota(jnp.int32, (GRID_W, NA_ROWS * GRID_W), 1) % GRID_W
    cs = jnp.clip(qc - NA_COLS // 2, 0, GRID_W - NA_COLS)
    return (kc >= cs) & (kc < cs + NA_COLS)


def _attn_b_fwd(qkv3, e2):
    _, S, _ = qkv3.shape
    rows = S // GRID_W
    assert rows >= NA_ROWS

    def body(qkv_ref, e2_ref, o_ref, lse_ref):
        ok = _nbr_col_ok()

        def step(r, carry):
            q0, _, _, _, _, v, s = _nbr_scores(qkv_ref, e2_ref, r, rows, ok)
            m = jnp.max(s, axis=-1, keepdims=True)
            p = jnp.exp(s - m)
            den = jnp.sum(p, axis=-1, keepdims=True)
            o = jnp.dot((p / den).astype(BF), v, preferred_element_type=F32)
            o_ref[pl.ds(q0, GRID_W), :] = o.astype(BF)
            lse_ref[pl.ds(q0, GRID_W), :] = jnp.broadcast_to(m + jnp.log(den), (GRID_W, HEAD_DIM))
            return carry

        lax.fori_loop(0, rows, step, 0)

    out = pl.BlockSpec((S, HEAD_DIM), lambda h: (0, h))
    return pl.pallas_call(
        body, name="attn_b_fwd", grid=(4,),
        in_specs=[pl.BlockSpec((3, S, HEAD_DIM), lambda h: (0, 0, N_HEADS_A + h)),
                  pl.BlockSpec((None, RPB_ROWS - 1, GRID_W, 128), lambda h: (h, 0, 0, 0))],
        out_specs=[out, out], out_shape=[_sds((S, 512), BF), _sds((S, 512), F32)],
        compiler_params=_cparams(("parallel",)),
    )(qkv3, e2)


def _attn_b_bwd(qkv3, e2, dy, y, lse, dqkv3):
    _, S, _ = qkv3.shape
    rows = S // GRID_W
    nk = NA_ROWS * GRID_W

    def body(qkv_ref, e2_ref, dy_ref, y_ref, lse_ref, _, out_ref, de2_ref, dk_acc, dv_acc):
        ok = _nbr_col_ok()
        dk_acc[...] = jnp.zeros_like(dk_acc)
        dv_acc[...] = jnp.zeros_like(dv_acc)
        de2_ref[...] = jnp.zeros_like(de2_ref)

        def step(r, carry):
            q0, k0, first, q, k, v, s = _nbr_scores(qkv_ref, e2_ref, r, rows, ok)
            qrows = pl.ds(q0, GRID_W)
            dyv = dy_ref[qrows, :]
            delta = jnp.sum(dyv.astype(F32) * y_ref[qrows, :].astype(F32), axis=-1, keepdims=True)
            p = jnp.where(ok, jnp.exp(s - jnp.tile(lse_ref[qrows, :], (1, nk // HEAD_DIM))), 0.0)
            dp = lax.dot_general(dyv, v, (NT, ((), ())), preferred_element_type=F32)
            ds = p * (dp - delta)
            for pair in range(NA_ROWS // 2):
                de2_ref[first + 2 * pair] += ds[:, pair * 128:(pair + 1) * 128]
            dsb = ds.astype(BF)
            out_ref[0, qrows, :] = (jnp.dot(dsb, k, preferred_element_type=F32) * SCALE).astype(BF)
            keys = pl.ds(k0, nk)
            dk_acc[keys, :] += lax.dot_general(dsb, q, (TN, ((), ())), preferred_element_type=F32) * SCALE
            dv_acc[keys, :] += lax.dot_general(p.astype(BF), dyv, (TN, ((), ())), preferred_element_type=F32)
            return carry

        lax.fori_loop(0, rows, step, 0)
        out_ref[1] = dk_acc[...].astype(BF)
        out_ref[2] = dv_acc[...].astype(BF)

    heads = pl.BlockSpec((3, S, HEAD_DIM), lambda h: (0, 0, N_HEADS_A + h))
    row = pl.BlockSpec((S, HEAD_DIM), lambda h: (0, h))
    table = pl.BlockSpec((None, RPB_ROWS - 1, GRID_W, 128), lambda h: (h, 0, 0, 0))
    return pl.pallas_call(
        body, name="attn_b_bwd", grid=(4,),
        in_specs=[heads, table, row, row, row, pl.BlockSpec(memory_space=pl.ANY)],
        out_specs=[heads, table],
        out_shape=[_sds((3, S, QKV_W), BF), _sds((4, RPB_ROWS - 1, GRID_W, 128), F32)],
        scratch_shapes=[pltpu.VMEM((S, HEAD_DIM), F32), pltpu.VMEM((S, HEAD_DIM), F32)],
        input_output_aliases={5: 0},
        compiler_params=_cparams(("parallel",)),
    )(qkv3, e2, dy, y, lse, dqkv3)


def _rpb_to_table(rpb):
    pad = jnp.pad(rpb, ((0, 0), (0, 0), (0, 1)))
    pairs = jnp.concatenate([pad[:, :-1], pad[:, 1:]], axis=-1).reshape(4 * (RPB_ROWS - 1), 64)
    onehot = jnp.asarray(_toeplitz_onehot())
    n = onehot.shape[1]
    tn = 2048
    full = lambda i, j, k: (0, 0)
    (e2,) = _matmul("rpb_table", pairs, onehot, pl.BlockSpec(pairs.shape, full),
                    pl.BlockSpec((64, tn), lambda i, j, k: (0, j)), NN, (1, n // tn, 1), (pairs.shape[0], tn), [],
                    [(_sds((pairs.shape[0], n), F32), pl.BlockSpec((pairs.shape[0], tn), lambda i, j, k: (0, j)))],
                    _store(F32), precision=lax.Precision.HIGHEST)
    return e2.reshape(4, RPB_ROWS - 1, GRID_W, 128)


def _table_grad_to_rpb(de2):
    onehot = jnp.asarray(_toeplitz_onehot())
    n = onehot.shape[1]
    flat = de2.reshape(4 * (RPB_ROWS - 1), n)
    tk = 2048
    (dpairs,) = _matmul("rpb_table_grad", flat, onehot, pl.BlockSpec((flat.shape[0], tk), lambda i, j, k: (0, k)),
                        pl.BlockSpec((64, tk), lambda i, j, k: (0, k)), NT, (1, 1, n // tk), (flat.shape[0], 64), [],
                        [(_sds((flat.shape[0], 64), F32), pl.BlockSpec((flat.shape[0], 64), lambda i, j, k: (0, 0)))],
                        _store(F32), precision=lax.Precision.HIGHEST)
    dpairs = dpairs.reshape(4, RPB_ROWS - 1, 64)
    zero = jnp.zeros((4, 1, RPB_COLS), F32)
    return (jnp.concatenate([dpairs[:, :, :RPB_COLS], zero], axis=1)
            + jnp.concatenate([zero, dpairs[:, :, 32:32 + RPB_COLS]], axis=1))


HBM = pl.BlockSpec(memory_space=pl.ANY)


def _place():
    x, y, c = lax.axis_index("x"), lax.axis_index("y"), lax.axis_index("c")
    chips = [(1 - x, y), (x, 1 - y), (1 - x, 1 - y)]
    return x, y, c, chips


def _remote(src, dst, send_sem, recv_sem, to):
    return pltpu.make_async_remote_copy(src_ref=src, dst_ref=dst, send_sem=send_sem, recv_sem=recv_sem,
                                        device_id=to, device_id_type=MESH)


def _place_shard(name, w, me):
    R, C = w.shape
    tr = _tile(R, 256)

    def body(me_ref, w_ref, o_ref):
        o_ref[...] = w_ref[...].astype(BF)

    return pl.pallas_call(
        body, name=name,
        grid_spec=pltpu.PrefetchScalarGridSpec(
            num_scalar_prefetch=1, grid=(R // tr,),
            in_specs=[pl.BlockSpec((tr, C), lambda i, mr: (i, 0))],
            out_specs=pl.BlockSpec((None, tr, C), lambda i, mr: (mr[0], i, 0))),
        out_shape=_sds((N_CHIPS, R, C), BF), compiler_params=_cparams(("parallel",)),
    )(me, w)


SEM = pl.BlockSpec(memory_space=pltpu.SEMAPHORE)
IN_HBM = pl.BlockSpec(memory_space=pltpu.HBM)
DATAFLOW = pltpu.SideEffectType.DATAFLOW_SIDE_EFFECTING


def _in_hbm(a):
    return pltpu.with_memory_space_constraint(a, pltpu.HBM)


def _copy_start(name, bufs, copies, n_copies, earlier=None, after=None):
    n = len(bufs)
    n_extra = (2 if earlier is not None else 0) + (1 if after is not None else 0)

    def body(*refs):
        ins = refs[:n]
        if earlier is not None:
            for k, (src, dst, to) in enumerate(earlier[0](ins)):
                cp = _remote(src, dst, refs[n].at[k], refs[n + 1].at[k], to)
                cp.wait_send()
                cp.wait_recv()
        send_sems, recv_sems = refs[n + n_extra], refs[n + n_extra + 1]
        for k, (src, dst, to) in enumerate(copies(ins)):
            _remote(src, dst, send_sems.at[k], recv_sems.at[k], to).start()
        refs[-1][...] = jnp.zeros((8, 128), F32)

    operands = [_in_hbm(b) for b in bufs]
    in_specs = [IN_HBM] * n
    if earlier is not None:
        operands += [earlier[1], earlier[2]]
        in_specs += [SEM, SEM]
    if after is not None:
        operands.append(after)
        in_specs.append(HBM)
    outs = pl.pallas_call(
        body, name=name,
        out_shape=(pltpu.SemaphoreType.DMA((n_copies,)), pltpu.SemaphoreType.DMA((n_copies,)),
                   *[pltpu.HBM(b.shape, b.dtype) for b in bufs], _sds((8, 128), F32)),
        in_specs=in_specs,
        out_specs=(SEM, SEM, *[IN_HBM] * n, pl.BlockSpec(memory_space=pltpu.VMEM)),
        input_output_aliases={i: 2 + i for i in range(n)},
        compiler_params=pltpu.CompilerParams(has_side_effects=DATAFLOW),
    )(*operands)
    return outs[0], outs[1], list(outs[2:2 + n]), outs[-1]


def _copy_wait(name, bufs, copies, send_sems, recv_sems, after):
    n = len(bufs)

    def body(*refs):
        ins = refs[:n]
        for k, (src, dst, to) in enumerate(copies(ins)):
            cp = _remote(src, dst, refs[n].at[k], refs[n + 1].at[k], to)
            cp.wait_send()
            cp.wait_recv()

    return list(pl.pallas_call(
        body, name=name,
        out_shape=tuple(pltpu.HBM(b.shape, b.dtype) for b in bufs),
        in_specs=[IN_HBM] * n + [SEM, SEM, HBM], out_specs=tuple([IN_HBM] * n),
        input_output_aliases={i: i for i in range(n)},
        compiler_params=pltpu.CompilerParams(has_side_effects=DATAFLOW),
    )(*bufs, send_sems, recv_sems, after))


def _gather_hop1(bufs):
    x, y, c, chips = _place()
    out = []
    for b in bufs:
        half = b.shape[1] // 2
        mine = b.at[2 * x + y, pl.ds(c * half, half), :]
        out += [(mine, mine, (*chip, c)) for chip in chips]
    return out


def _gather_hop2(bufs):
    x, y, c, chips = _place()
    out = []
    for b in bufs:
        half = b.shape[1] // 2
        for chip in chips:
            landed = b.at[2 * chip[0] + chip[1], pl.ds(c * half, half), :]
            out.append((landed, landed, (x, y, 1 - c)))
    return out


def _swap_copies(bufs):
    x, y, c, _ = _place()
    n = len(bufs) // 2
    out = []
    for p, land in zip(bufs[:n], bufs[n:]):
        half = p.shape[1] // 2
        out.append((p.at[:, pl.ds((1 - c) * half, half), :], land, (x, y, 1 - c)))
    return out


def _scatter_copies(bufs):
    _, _, c, chips = _place()
    n = len(bufs) // 2
    out = []
    for s_, land in zip(bufs[:n], bufs[n:]):
        out += [(s_.at[2 * chip[0] + chip[1]], land.at[j], (*chip, c)) for j, chip in enumerate(chips)]
    return out


def _join_copies(bufs):
    x, y, c, _ = _place()
    out = []
    for b in bufs:
        half = b.shape[0] // 2
        mine = b.at[pl.ds(c * half, half), :]
        out.append((mine, mine, (x, y, 1 - c)))
    return out


def _gather_small(vec):
    m_per, n = vec.shape

    def body(x_ref, out_ref, send_sems, recv_sems, local_sem):
        x, y, c, chips = _place()
        me, sibling = (x, y, c), (x, y, 1 - c)

        def rows(px, py, pc):
            return out_ref.at[pl.ds((4 * px + 2 * py + pc) * m_per, m_per), :]

        def copy(k, block, to, src=None):
            return _remote(rows(*block) if src is None else src, rows(*block), send_sems.at[k], recv_sems.at[k], to)

        mine = pltpu.make_async_copy(x_ref, rows(*me), local_sem)
        mine.start()
        first = [copy(0, me, sibling, src=x_ref)]
        first += [copy(1 + j, me, (*chip, c), src=x_ref) for j, chip in enumerate(chips)]
        for cp in first:
            cp.start()
        passed = [copy(4 + j, (*chip, c), sibling) for j, chip in enumerate(chips)]
        for j, chip in enumerate(chips):
            copy(1 + j, (*chip, c), me).wait_recv()
            passed[j].start()
        copy(0, sibling, me).wait_recv()
        for j, chip in enumerate(chips):
            copy(4 + j, (*chip, 1 - c), me).wait_recv()
        for cp in first + passed:
            cp.wait_send()
        mine.wait()

    return pl.pallas_call(
        body, name="gather_small_grads",
        out_shape=_sds((8 * m_per, n), vec.dtype),
        in_specs=[pl.BlockSpec(memory_space=pltpu.VMEM)], out_specs=pl.BlockSpec(memory_space=pltpu.VMEM),
        scratch_shapes=[pltpu.SemaphoreType.DMA((7,)), pltpu.SemaphoreType.DMA((7,)), pltpu.SemaphoreType.DMA],
    )(vec)


def _add_sibling(name, partial, received, c):
    _, R, C = partial.shape
    half = R // 2
    tr = _tile(half, 256)
    nb = half // tr

    def body(c_ref, p_ref, r_ref, o_ref):
        o_ref[...] = (p_ref[...].astype(F32) + r_ref[...].astype(F32)).astype(BF)

    return pl.pallas_call(
        body, name=name,
        grid_spec=pltpu.PrefetchScalarGridSpec(
            num_scalar_prefetch=1, grid=(N_CHIPS, nb),
            in_specs=[pl.BlockSpec((None, tr, C), lambda j, i, cr: (j, cr[0] * nb + i, 0)),
                      pl.BlockSpec((None, tr, C), lambda j, i, cr: (j, i, 0))],
            out_specs=pl.BlockSpec((None, tr, C), lambda j, i, cr: (j, i, 0))),
        out_shape=_sds((N_CHIPS, half, C), BF), compiler_params=_cparams(("parallel", "parallel")),
    )(c, partial, received)


def _add_chips(name, sums, received, me_c):
    _, half, C = sums.shape
    tr = _tile(half, 256)
    nb = half // tr

    def body(mc_ref, s_ref, r_ref, o_ref):
        acc = s_ref[...].astype(F32)
        for j in range(3):
            acc = acc + r_ref[j].astype(F32)
        o_ref[...] = acc

    return pl.pallas_call(
        body, name=name,
        grid_spec=pltpu.PrefetchScalarGridSpec(
            num_scalar_prefetch=1, grid=(nb,),
            in_specs=[pl.BlockSpec((None, tr, C), lambda i, mc: (mc[0], i, 0)),
                      pl.BlockSpec((3, tr, C), lambda i, mc: (0, i, 0))],
            out_specs=pl.BlockSpec((tr, C), lambda i, mc: (mc[1] * nb + i, 0))),
        out_shape=_sds((2 * half, C), F32), compiler_params=_cparams(("parallel",)),
    )(me_c, sums, received)


def _adamw_math(w, g, m, v):
    m = ADAM_B1 * m + (1.0 - ADAM_B1) * g
    v = ADAM_B2 * v + (1.0 - ADAM_B2) * (g * g)
    m_hat = m / (1.0 - ADAM_B1 ** ADAM_STEP)
    v_hat = v / (1.0 - ADAM_B2 ** ADAM_STEP)
    delta = -ADAM_LR * (m_hat / (jnp.sqrt(v_hat) + ADAM_EPS) + ADAM_WD * w)
    return delta, m, v


def _adamw(name, w, g, m, v):
    R, C = w.shape
    tr = _tile(R, 128)

    def body(w_ref, g_ref, m_ref, v_ref, go_ref, d_ref, mo_ref, vo_ref):
        gv = g_ref[...]
        go_ref[...] = gv
        d_ref[...], mo_ref[...], vo_ref[...] = _adamw_math(w_ref[...], gv, m_ref[...], v_ref[...])

    row = pl.BlockSpec((tr, C), lambda i: (i, 0))
    return pl.pallas_call(
        body, name=name, grid=(R // tr,), in_specs=[row] * 4, out_specs=[row] * 4,
        out_shape=[_sds((R, C), F32)] * 4, compiler_params=_cparams(("parallel",)),
    )(w, g, m, v)


def _adamw_small(gathered, w, m, v):
    rows, n = w.shape

    def body(ga_ref, w_ref, m_ref, v_ref, go_ref, d_ref, mo_ref, vo_ref):
        g = ga_ref[pl.ds(0, rows), :]
        for dev in range(1, 8):
            g = g + ga_ref[pl.ds(dev * rows, rows), :]
        go_ref[...] = g
        d_ref[...], mo_ref[...], vo_ref[...] = _adamw_math(w_ref[...], g, m_ref[...], v_ref[...])

    return pl.pallas_call(
        body, name="adamw_small", out_shape=[_sds((rows, n), F32)] * 4,
        compiler_params=_cparams(),
    )(gathered, w, m, v)


class _Exchange:
    GATHER = (("qkv",), ("gate", "proj_a", "proj_b", "out"), ("up", "down"))
    REDUCE = {"mlp": ("down", "up"), "mix": ("out", "proj_a", "proj_b"), "in": ("qkv", "gate")}

    def __init__(self, shards, me, c):
        self.me, self.c = me, c
        self.tokens = []
        self.hop1, self.hop2, self.stage, self.grads = {}, {}, {}, {}
        placed = {n: _place_shard(f"place_{n}", w, me) for n, w in shards.items()}
        for g, names in enumerate(self.GATHER):
            bufs = self.tie([placed[n] for n in names])
            send, recv, thru, token = _copy_start(f"gather{g}_start", bufs, _gather_hop1, 3 * len(names))
            self.hop1[g] = (send, recv, thru)
            self.tokens = [token]

    def tie(self, arrays):
        if not self.tokens:
            return arrays
        tied, _ = lax.optimization_barrier((arrays, tuple(self.tokens)))
        self.tokens = []
        return tied

    def forward(self, g, after):
        send, recv, thru = self.hop1.pop(g)
        send2, recv2, thru2, token = _copy_start(f"gather{g}_forward", thru, _gather_hop2, len(thru) * 3,
                                                 earlier=(_gather_hop1, send, recv), after=after)
        self.hop2[g] = (send2, recv2, thru2)
        self.tokens.append(token)

    def weights(self, g, after):
        send, recv, thru = self.hop2.pop(g)
        return _copy_wait(f"gather{g}_wait", thru, _gather_hop2, send, recv, after)

    def reduce(self, key, after=None, partials=None):
        names = self.REDUCE[key]
        n = len(names)
        if partials is not None:
            lands = [lax.empty((p.shape[0], p.shape[1] // 2, p.shape[2]), p.dtype) for p in partials]
            send, recv, thru, token = _copy_start(f"reduce_{key}_swap", list(partials) + lands, _swap_copies, n,
                                                  after=after)
            self.stage[key] = ("swap", send, recv, thru)
            self.tokens.append(token)
            return
        kind, send, recv, thru = self.stage.pop(key)
        if kind == "swap":
            thru = _copy_wait(f"reduce_{key}_swap_wait", thru, _swap_copies, send, recv, after)
            sums = [_add_sibling(f"reduce_{nm}_add_sibling", p, r, self.c)
                    for nm, p, r in zip(names, thru[:n], thru[n:])]
            lands = [lax.empty((3,) + s_.shape[1:], s_.dtype) for s_ in sums]
            send, recv, thru, token = _copy_start(f"reduce_{key}_scatter", sums + lands, _scatter_copies, 3 * n)
            self.stage[key] = ("scatter", send, recv, thru)
            self.tokens.append(token)
        elif kind == "scatter":
            thru = _copy_wait(f"reduce_{key}_scatter_wait", thru, _scatter_copies, send, recv, after)
            me_c = jnp.concatenate([self.me, self.c])
            halves = [_add_chips(f"reduce_{nm}_add_chips", s_, r, me_c)
                      for nm, s_, r in zip(names, thru[:n], thru[n:])]
            send, recv, thru, token = _copy_start(f"reduce_{key}_join", halves, _join_copies, n)
            self.stage[key] = ("join", send, recv, thru)
            self.tokens.append(token)
        else:
            thru = _copy_wait(f"reduce_{key}_join_wait", thru, _join_copies, send, recv, after)
            self.grads.update(zip(names, thru))


def _forward_backward(x, target, norm_mix, b_gate, rpb, norm_mlp, norm_final, ex):
    S, D = x.shape

    (x_t,) = ex.tie([x])
    h1 = _rms_fwd("rms_mix", x_t, norm_mix)
    ex.forward(0, h1)
    (rpb_t,) = ex.tie([rpb])
    e2 = _rpb_to_table(rpb_t)
    (gq,) = ex.weights(0, e2)
    nq = QKV_W // 512
    (qkv3,), _ = _mm_nn_cols(
        "qkv", h1, gq, BF, tn=512,
        outs=[(_sds((3, S, QKV_W), BF), pl.BlockSpec((None, _tile(S, 1024), 512), lambda i, j, k: (j // nq, i, j % nq)))])

    ex.forward(1, qkv3)
    (qkv3_t,) = ex.tie([qkv3])
    outs_a = [_attn_a_fwd(qkv3_t, 0, DILATIONS[0])]
    gg, gpa, gpb, gout = ex.weights(1, outs_a[0][0])
    wout = gout.reshape(D, D)

    tg = _tile(gg.shape[2], 1024)
    ng = D // tg

    def gate_epilogue(acc, ex_, outs):
        outs[0][...] = jax.nn.sigmoid(acc + ex_[0][...])

    (g3,), _ = _mm_nn_cols(
        "gate", h1, gg, F32, epilogue=gate_epilogue, tn=tg,
        extras=[(b_gate, pl.BlockSpec((1, tg), lambda i, j, k: (0, j)))],
        outs=[(_sds((2, S, D), F32), pl.BlockSpec((None, _tile(S, 1024), tg), lambda i, j, k: (j // ng, i, j % ng)))])

    outs_a += [_attn_a_fwd(qkv3, grp, d) for grp, d in enumerate(DILATIONS) if grp > 0]
    y_a, lj = _attn_a_combine([o for o, _ in outs_a], [l for _, l in outs_a])
    y_b, lse_b = _attn_b_fwd(qkv3, e2)

    (pa,), (tm, tp, _) = _mm_nn_cols("proj_a", y_a, gpa, F32, tn=512)

    def merge_epilogue(acc, ex_, outs):
        g = ex_[0][...]
        outs[0][...] = acc
        outs[1][...] = (g[0] * ex_[1][...] + g[1] * acc).astype(BF)

    tile = pl.BlockSpec((tm, tp), lambda i, j, k: (i, j))
    gates = pl.BlockSpec((2, tm, tp), lambda i, j, k: (0, i, j))
    (pb, merged), _ = _mm_nn_cols(
        "proj_b_merge", y_b, gpb, F32, epilogue=merge_epilogue, tn=512,
        extras=[(g3, gates), (pa, tile)],
        outs=[(_sds((S, D), F32), tile), (_sds((S, D), BF), tile)])

    def residual_epilogue(acc, ex_, outs):
        outs[0][...] = acc + ex_[0][...]

    def nn_plain(name, a, w, res):
        M, K = a.shape
        N = w.shape[1]
        bm, bn, bk = _tile(M, 1024), _tile(N, 1024), _tile(K, 1024)
        t = pl.BlockSpec((bm, bn), lambda i, j, k: (i, j))
        return _matmul(name, a, w, pl.BlockSpec((bm, bk), lambda i, j, k: (i, k)),
                       pl.BlockSpec((bk, bn), lambda i, j, k: (k, j)), NN, (M // bm, N // bn, K // bk), (bm, bn),
                       [(res, t)], [(_sds((M, N), F32), t)], residual_epilogue)[0]

    ex.forward(2, merged)
    (merged_t,) = ex.tie([merged])
    x1 = nn_plain("out_proj", merged_t, wout, x)
    h2 = _rms_fwd("rms_mlp", x1, norm_mlp)
    gup, gdown = ex.weights(2, h2)
    F = gup.shape[2] * N_CHIPS
    wdown = gdown.reshape(F, D)

    def up_epilogue(acc, ex_, outs):
        ru = jnp.maximum(acc, 0.0)
        outs[0][...] = (ru * ru).astype(BF)
        outs[1][...] = ru.astype(BF)

    tu = _tile(gup.shape[2], 1024)
    ut = pl.BlockSpec((_tile(S, 1024), tu), lambda i, j, k: (i, j))
    (act, ru), _ = _mm_nn_cols("mlp_up", h2, gup, BF, epilogue=up_epilogue, tn=tu,
                               outs=[(_sds((S, F), BF), ut), (_sds((S, F), BF), ut)])
    x2 = nn_plain("mlp_down", act, wdown, x1)

    loss, dx2, dx2b, d_norm_final = _loss_head(x2, target, norm_final.reshape(1, D))

    def nt_rows(name, a, w, epilogue, extras, outs, bn=1024):
        M, N = a.shape
        K = w.shape[0]
        bm, bn, bk = _tile(M, 1024), _tile(K, bn), _tile(N, 1024)
        return _matmul(name, a, w, pl.BlockSpec((bm, bk), lambda i, j, k: (i, k)),
                       pl.BlockSpec((bn, bk), lambda i, j, k: (j, k)), NT, (M // bm, K // bn, N // bk), (bm, bn),
                       extras(bm, bn), outs(bm, bn), epilogue)

    def nt_cols(name, a_spec_fn, a, g, M, epilogue, extras, outs, bk):
        _, K, Nq = g.shape
        bm, bn, bk = _tile(M, 1024), _tile(K, 1024), _tile(Nq, bk)
        q = Nq // bk
        return _matmul(name, a, g, a_spec_fn(bm, bk), pl.BlockSpec((None, bn, bk), lambda i, j, k: (k // q, j, k % q)),
                       NT, (M // bm, K // bn, N_CHIPS * q), (bm, bn), extras(bm, bn), outs(bm, bn), epilogue)

    def tn_grad(name, a, a_spec_fn, b, b_spec_fn, Kin, N, out_shape, out_spec_fn, bn=1024):
        bm, bn, bk = _tile(Kin, 1024), _tile(N, bn), _tile(S, 1024)
        return _matmul(name, a, b, a_spec_fn(bk, bm), b_spec_fn(bk, bn), TN, (Kin // bm, N // bn, S // bk), (bm, bn),
                       [], [(_sds(out_shape, BF), out_spec_fn(bm, bn))], _store(BF))[0]

    plain_a = lambda bk, bm: pl.BlockSpec((bk, bm), lambda i, j, k: (k, i))
    plain_b = lambda bk, bn: pl.BlockSpec((bk, bn), lambda i, j, k: (k, j))
    plain_o = lambda bm, bn: pl.BlockSpec((bm, bn), lambda i, j, k: (i, j))
    a_rows = lambda bm, bk: pl.BlockSpec((bm, bk), lambda i, j, k: (i, k))

    def cols_o(Nq):
        def spec(bm, bn):
            q = Nq // bn
            return pl.BlockSpec((None, bm, bn), lambda i, j, k: (j // q, i, j % q))
        return spec

    def du_epilogue(acc, ex_, outs):
        outs[0][...] = (acc * (2.0 * ex_[0][...].astype(F32))).astype(BF)

    dw_down = tn_grad("mlp_down_dw", act, plain_a, dx2b, plain_b, F, D, (F, D), plain_o)
    (du,) = nt_rows("mlp_down_dx", dx2b, wdown, du_epilogue,
                    lambda bm, bn: [(ru, plain_o(bm, bn))], lambda bm, bn: [(_sds((S, F), BF), plain_o(bm, bn))])

    fq = gup.shape[2]
    dw_up = tn_grad("mlp_up_dw", h2, plain_a, du, plain_b, D, F, (N_CHIPS, D, fq), cols_o(fq), bn=min(fq, 1024))
    ex.reduce("mlp", partials=[dw_down.reshape(N_CHIPS, F // N_CHIPS, D), dw_up])
    (du_t,) = ex.tie([du])
    (dh2,) = nt_cols("mlp_up_dx", a_rows, du_t, gup, S, _store(F32), lambda bm, bn: [],
                     lambda bm, bn: [(_sds((S, D), F32), plain_o(bm, bn))], 1024)
    ex.reduce("mlp", after=dh2)
    (dh2_t,) = ex.tie([dh2])
    dx1, dx1b, d_norm_mlp = _rms_bwd("rms_mlp_bwd", dh2_t, x1, norm_mlp, dx2)

    def merge_bwd_epilogue(acc, ex_, outs):
        g, pav, pbv = ex_[0][...], ex_[1][...], ex_[2][...]
        outs[0][...] = (acc * g[0]).astype(BF)
        outs[1][...] = (acc * g[1]).astype(BF)
        dga = acc * pav * g[0] * (1.0 - g[0])
        dgb = acc * pbv * g[1] * (1.0 - g[1])
        outs[2][0] = dga.astype(BF)
        outs[2][1] = dgb.astype(BF)
        outs[3][...] = jnp.concatenate([jnp.sum(dga, axis=0, keepdims=True), jnp.sum(dgb, axis=0, keepdims=True)], 0)

    def pair(bm, bn):
        return pl.BlockSpec((2, bm, bn), lambda i, j, k: (0, i, j))

    n_row_blocks = S // _tile(S, 1024)
    dpa, dpb, dg3, db_gate = nt_rows(
        "out_proj_dx", dx1b, wout, merge_bwd_epilogue,
        lambda bm, bn: [(g3, pair(bm, bn)), (pa, plain_o(bm, bn)), (pb, plain_o(bm, bn))],
        lambda bm, bn: [(_sds((S, D), BF), plain_o(bm, bn)), (_sds((S, D), BF), plain_o(bm, bn)),
                        (_sds((2, S, D), BF), pair(bm, bn)),
                        (_sds((n_row_blocks, 2, D), F32), pl.BlockSpec((None, 2, bn), lambda i, j, k: (i, 0, j)))],
        bn=512)
    dw_out = tn_grad("out_proj_dw", merged, plain_a, dx1b, plain_b, D, D, (D, D), plain_o)

    pq = gpa.shape[2]
    proj_dx = lambda name, dproj, g: nt_cols(name, a_rows, dproj, g, S, _store(BF), lambda bm, bn: [],
                                             lambda bm, bn: [(_sds((S, 512), BF), plain_o(bm, bn))], 512)[0]
    dw_pa = tn_grad("proj_a_dw", y_a, plain_a, dpa, plain_b, 512, D, (N_CHIPS, 512, pq), cols_o(pq), bn=min(pq, 512))
    dw_pb = tn_grad("proj_b_dw", y_b, plain_a, dpb, plain_b, 512, D, (N_CHIPS, 512, pq), cols_o(pq), bn=min(pq, 512))
    ex.reduce("mix", partials=[dw_out.reshape(N_CHIPS, D // N_CHIPS, D), dw_pa, dw_pb])
    (dpa_t, dpb_t) = ex.tie([dpa, dpb])
    dy_a = proj_dx("proj_a_dx", dpa_t, gpa)
    dy_b = proj_dx("proj_b_dx", dpb_t, gpb)

    dqkv3 = lax.empty((3, S, QKV_W), BF)
    dqkv3 = _attn_a_bwd(qkv3, dy_a, y_a, lj, dqkv3, 0, DILATIONS[0])
    ex.reduce("mix", after=dqkv3)
    (dy_a_t,) = ex.tie([dy_a])
    for grp, d in enumerate(DILATIONS):
        if grp > 0:
            dqkv3 = _attn_a_bwd(qkv3, dy_a_t, y_a, lj, dqkv3, grp, d)
    dqkv3, de2 = _attn_b_bwd(qkv3, e2, dy_b, y_b, lse_b, dqkv3)
    d_rpb = _table_grad_to_rpb(de2)

    def stacked_a(width):
        def spec(bm, bk):
            q = width // bk
            return pl.BlockSpec((None, bm, bk), lambda i, j, k: (k // q, i, k % q))
        return spec

    def stacked_b(width):
        def spec(bk, bn):
            q = width // bn
            return pl.BlockSpec((None, bk, bn), lambda i, j, k: (j // q, k, j % q))
        return spec

    dw_qkv = tn_grad("qkv_dw", h1, plain_a, dqkv3, stacked_b(QKV_W), D, 3 * QKV_W, (N_CHIPS,) + gq.shape[1:],
                     cols_o(gq.shape[2]), bn=512)
    dw_gate = tn_grad("gate_dw", h1, plain_a, dg3, stacked_b(D), D, 2 * D, (N_CHIPS,) + gg.shape[1:],
                      cols_o(gg.shape[2]), bn=gg.shape[2])
    ex.reduce("in", partials=[dw_qkv, dw_gate])
    ex.reduce("mlp", after=dw_gate)
    (dqkv3_t,) = ex.tie([dqkv3])
    (dh1_q,) = nt_cols("qkv_dx", stacked_a(QKV_W), dqkv3_t, gq, S, _store(F32), lambda bm, bn: [],
                       lambda bm, bn: [(_sds((S, D), F32), plain_o(bm, bn))], 512)
    ex.reduce("in", after=dh1_q)
    ex.reduce("mix", after=dh1_q)
    (dg3_t,) = ex.tie([dg3])

    def add_epilogue(acc, ex_, outs):
        outs[0][...] = acc + ex_[0][...]

    (dh1,) = nt_cols("gate_dx", stacked_a(D), dg3_t, gg, S, add_epilogue, lambda bm, bn: [(dh1_q, plain_o(bm, bn))],
                     lambda bm, bn: [(_sds((S, D), F32), plain_o(bm, bn))], gg.shape[2])
    grad_x, _, d_norm_mix = _rms_bwd("rms_mix_bwd", dh1, x, norm_mix, dx1)
    ex.reduce("mlp", after=grad_x)
    ex.reduce("mix", after=grad_x)

    small = [d_norm_mix, jnp.sum(db_gate, axis=0).reshape(1, 2 * D), d_rpb, d_norm_mlp, d_norm_final]
    return loss, grad_x, small


def _pack_small(parts, width):
    flat = jnp.concatenate([p.reshape(-1) for p in parts])
    return jnp.pad(flat, (0, 8 * width - flat.shape[0])).reshape(8, width)


def kernel(x, norm_mix, w_qkv, w_gate, b_gate, rpb, w_proj_a, w_proj_b, w_out, norm_mlp, w_up, w_down, norm_final, loss_target, m_norm_mix, m_w_qkv, m_w_gate, m_b_gate, m_rpb, m_w_proj_a, m_w_proj_b, m_w_out, m_norm_mlp, m_w_up, m_w_down, m_norm_final, v_norm_mix, v_w_qkv, v_w_gate, v_b_gate, v_rpb, v_w_proj_a, v_w_proj_b, v_w_out, v_norm_mlp, v_w_up, v_w_down, v_norm_final):
    names = ["qkv", "gate", "proj_a", "proj_b", "out", "up", "down"]
    big = dict(zip(names, [w_qkv[0], w_gate[0], w_proj_a[0], w_proj_b[0], w_out[0], w_up[0], w_down[0]]))
    big_m = dict(zip(names, [m_w_qkv[0], m_w_gate[0], m_w_proj_a[0], m_w_proj_b[0], m_w_out[0], m_w_up[0], m_w_down[0]]))
    big_v = dict(zip(names, [v_w_qkv[0], v_w_gate[0], v_w_proj_a[0], v_w_proj_b[0], v_w_out[0], v_w_up[0], v_w_down[0]]))

    c = lax.axis_index("c").astype(jnp.int32).reshape(1)
    me = (2 * lax.axis_index("x") + lax.axis_index("y")).astype(jnp.int32).reshape(1)
    ex = _Exchange(big, me, c)
    loss, grad_x, small = _forward_backward(x[0], loss_target[0], norm_mix, b_gate, rpb[0], norm_mlp, norm_final, ex)

    def adamw(group):
        return {n: _adamw(f"adamw_{n}", big[n], ex.grads[n], big_m[n], big_v[n]) for n in _Exchange.REDUCE[group]}

    big_out = {**adamw("mlp"), **adamw("mix")}
    ex.reduce("in", after=big_out["proj_b"][0])

    small_w = [norm_mix, b_gate, rpb, norm_mlp, norm_final]
    count = sum(int(np.prod(p.shape)) for p in small_w)
    width = -(-count // (8 * 128)) * 128
    (packed_grads,) = ex.tie([_pack_small(small, width)])
    packed = _adamw_small(_gather_small(packed_grads), _pack_small(small_w, width),
                          _pack_small([m_norm_mix, m_b_gate, m_rpb, m_norm_mlp, m_norm_final], width),
                          _pack_small([v_norm_mix, v_b_gate, v_rpb, v_norm_mlp, v_norm_final], width))
    ex.reduce("in", after=packed[0])
    big_out.update(adamw("in"))

    def unpack(flat2d):
        flat, out, at = flat2d.reshape(-1), [], 0
        for p in small_w:
            size = int(np.prod(p.shape))
            out.append(flat[at:at + size].reshape(p.shape))
            at += size
        return out

    small_out = [unpack(a) for a in packed]

    def ordered(kind):
        sm = small_out[kind]
        bg = {n: o[kind][None] for n, o in big_out.items()}
        return [sm[0], bg["qkv"], bg["gate"], sm[1], sm[2], bg["proj_a"], bg["proj_b"], bg["out"], sm[3],
                bg["up"], bg["down"], sm[4]]

    total = lax.psum(loss[0, 0], ("x", "y", "c"))
    return (total, grad_x[None], *ordered(0), *ordered(1), *ordered(2), *ordered(3))
```

```python
import functools
import math

import numpy as np
import jax
import jax.numpy as jnp
from jax import lax
from jax.experimental import pallas as pl
from jax.experimental.pallas import tpu as pltpu

BF = jnp.bfloat16
F32 = jnp.float32
MESH = pl.DeviceIdType.MESH

HEAD_DIM = 128
N_HEADS = 16
N_HEADS_A = 12
QKV_W = N_HEADS * HEAD_DIM
DILATIONS = (1, 4, 16)
HALF_WINDOW = 64
GRID_W = 64
NA_ROWS = 8
NA_COLS = 16
RPB_ROWS = 2 * NA_ROWS - 1
RPB_COLS = 2 * NA_COLS - 1
EPS = 1e-6
NEG = -1e30
SCALE = HEAD_DIM ** -0.5

ADAM_LR = 0.001
ADAM_B1 = 0.9
ADAM_B2 = 0.999
ADAM_EPS = 1e-08
ADAM_WD = 0.01
ADAM_STEP = 10

N_CHIPS = 4
VMEM_LIMIT_BYTES = 48 * 1024 * 1024
QB = 128
KB = QB + 2 * HALF_WINDOW


def _cparams(sem=None):
    return pltpu.CompilerParams(dimension_semantics=sem, vmem_limit_bytes=VMEM_LIMIT_BYTES)


def _tile(dim, want):
    t = min(dim, want)
    assert dim % t == 0, (dim, want)
    return t


class _ProgramOrder:
    def __init__(self):
        self.last = None

    def call(self, body, operands, in_specs, *, prefetch=(), grid=None, out_specs=None, chain_output=0, **kwargs):
        operands, in_specs = list(operands), list(in_specs)
        lead = len(prefetch) + len(operands)
        if self.last is not None and not any(op is self.last for op in operands):
            operands.append(self.last)
            in_specs.append(pl.BlockSpec(memory_space=pl.ANY))
            inner = body

            def body(*refs):
                return inner(*refs[:lead], *refs[lead + 1:])

        if prefetch:
            kwargs["grid_spec"] = pltpu.PrefetchScalarGridSpec(
                num_scalar_prefetch=len(prefetch), grid=grid, in_specs=in_specs, out_specs=out_specs)
        else:
            kwargs.update(in_specs=in_specs, out_specs=out_specs)
            if grid is not None:
                kwargs["grid"] = grid
        out = pl.pallas_call(body, **kwargs)(*prefetch, *operands)
        self.last = out[chain_output] if isinstance(out, (tuple, list)) else out
        return out


ORDER = _ProgramOrder()


NN = ((1,), (0,))
NT = ((1,), (1,))
TN = ((0,), (0,))


def _matmul(name, a, b, a_spec, b_spec, dims, grid, acc_shape, extras, outs, epilogue, precision=None):
    n_ex, n_out, nk = len(extras), len(outs), grid[2]

    def body(*refs):
        a_ref, b_ref = refs[0], refs[1]
        ex_refs = refs[2:2 + n_ex]
        out_refs = refs[2 + n_ex:2 + n_ex + n_out]
        acc_ref = refs[-1]
        k = pl.program_id(2)

        @pl.when(k == 0)
        def _():
            acc_ref[...] = jnp.zeros_like(acc_ref)

        acc_ref[...] += lax.dot_general(a_ref[...], b_ref[...], (dims, ((), ())),
                                        preferred_element_type=F32, precision=precision)

        @pl.when(k == nk - 1)
        def _():
            epilogue(acc_ref[...], ex_refs, out_refs)

    return ORDER.call(
        body, [a, b] + [e for e, _ in extras], [a_spec, b_spec] + [s for _, s in extras], name=name, grid=grid,
        out_specs=[s for _, s in outs],
        out_shape=[sh for sh, _ in outs],
        scratch_shapes=[pltpu.VMEM(acc_shape, F32)],
        compiler_params=_cparams(("parallel", "parallel", "arbitrary")),
    )


def _store(dtype):
    def epilogue(acc, ex, outs):
        outs[0][...] = acc.astype(dtype)
    return epilogue


def _sds(shape, dtype):
    return jax.ShapeDtypeStruct(shape, dtype)


def _mm_nn_cols(name, a, g, out_dtype, epilogue=None, extras=(), outs=None, tm=1024, tn=1024, tk=1024):
    M, K = a.shape
    _, _, Nq = g.shape
    tm, tn, tk = _tile(M, tm), _tile(Nq, tn), _tile(K, tk)
    q = Nq // tn
    grid = (M // tm, N_CHIPS * q, K // tk)
    if outs is None:
        outs = [(_sds((M, N_CHIPS * Nq), out_dtype), pl.BlockSpec((tm, tn), lambda i, j, k: (i, j)))]
    return _matmul(name, a, g, pl.BlockSpec((tm, tk), lambda i, j, k: (i, k)),
                   pl.BlockSpec((None, tk, tn), lambda i, j, k: (j // q, k, j % q)), NN, grid, (tm, tn),
                   list(extras), outs, epilogue or _store(out_dtype)), (tm, tn, tk)


def _rms_fwd(name, x, g):
    S, D = x.shape
    tm = _tile(S, 256)

    def body(x_ref, g_ref, h_ref):
        xv = x_ref[...]
        r = lax.rsqrt(jnp.mean(xv * xv, axis=-1, keepdims=True) + EPS)
        h_ref[...] = ((xv * r) * g_ref[...]).astype(BF)

    row = pl.BlockSpec((tm, D), lambda i: (i, 0))
    return ORDER.call(
        body, [x, g], [row, pl.BlockSpec((1, D), lambda i: (0, 0))], name=name, grid=(S // tm,),
        out_specs=row, out_shape=_sds((S, D), BF), compiler_params=_cparams(("parallel",)),
    )


def _rms_bwd(name, dh, x, g, dres):
    S, D = x.shape
    tm = _tile(S, 256)

    def body(dh_ref, x_ref, g_ref, dres_ref, dx_ref, dxb_ref, dg_ref):
        xv = x_ref[...]
        r = lax.rsqrt(jnp.mean(xv * xv, axis=-1, keepdims=True) + EPS)
        n = xv * r
        dhv = dh_ref[...]
        dyg = dhv * g_ref[...]
        dx = dres_ref[...] + r * (dyg - n * jnp.mean(dyg * n, axis=-1, keepdims=True))
        dx_ref[...] = dx
        dxb_ref[...] = dx.astype(BF)

        @pl.when(pl.program_id(0) == 0)
        def _():
            dg_ref[...] = jnp.zeros_like(dg_ref)

        dg_ref[...] += jnp.sum(dhv * n, axis=0, keepdims=True)

    row = pl.BlockSpec((tm, D), lambda i: (i, 0))
    vec = pl.BlockSpec((1, D), lambda i: (0, 0))
    return ORDER.call(
        body, [dh, x, g, dres], [row, row, vec, row], name=name, grid=(S // tm,),
        out_specs=[row, row, vec],
        out_shape=[_sds((S, D), F32), _sds((S, D), BF), _sds((1, D), F32)],
        compiler_params=_cparams(("arbitrary",)),
    )


def _loss_head(x2, target, g):
    S, D = x2.shape
    tm = _tile(S, 256)

    def body(x_ref, t_ref, g_ref, loss_ref, dx_ref, dxb_ref, dg_ref):
        xv = x_ref[...]
        gv = g_ref[...]
        r = lax.rsqrt(jnp.mean(xv * xv, axis=-1, keepdims=True) + EPS)
        n = xv * r
        e = n * gv - t_ref[...]
        dy = e * (1.0 / D)
        dyg = dy * gv
        dx = r * (dyg - n * jnp.mean(dyg * n, axis=-1, keepdims=True))
        dx_ref[...] = dx
        dxb_ref[...] = dx.astype(BF)

        @pl.when(pl.program_id(0) == 0)
        def _():
            dg_ref[...] = jnp.zeros_like(dg_ref)
            loss_ref[...] = jnp.zeros_like(loss_ref)

        dg_ref[...] += jnp.sum(dy * n, axis=0, keepdims=True)
        per_row = jnp.mean(e * e, axis=-1, keepdims=True)
        loss_ref[...] += 0.5 * jnp.sum(per_row, axis=0, keepdims=True)

    row = pl.BlockSpec((tm, D), lambda i: (i, 0))
    vec = pl.BlockSpec((1, D), lambda i: (0, 0))
    return ORDER.call(
        body, [x2, target, g], [row, row, vec], name="loss_head", grid=(S // tm,),
        out_specs=[pl.BlockSpec((1, 1), lambda i: (0, 0)), row, row, vec],
        out_shape=[_sds((1, 1), F32), _sds((S, D), F32), _sds((S, D), BF), _sds((1, D), F32)],
        compiler_params=_cparams(("arbitrary",)), chain_output=1,
    )


def _band_scores(qkv_ref, i, L, coef):
    q0 = pl.multiple_of(i * QB, QB)
    ks = pl.multiple_of(jnp.clip(i * QB - HALF_WINDOW, 0, L - KB), HALF_WINDOW)
    q = qkv_ref[0, pl.ds(q0, QB), :]
    k = qkv_ref[1, pl.ds(ks, KB), :]
    v = qkv_ref[2, pl.ds(ks, KB), :]
    s = lax.dot_general(q, k, (NT, ((), ())), preferred_element_type=F32) * SCALE
    qpos = q0 + lax.broadcasted_iota(jnp.int32, (QB, KB), 0)
    kpos = ks + lax.broadcasted_iota(jnp.int32, (QB, KB), 1)
    rel = jnp.abs(kpos - qpos)
    valid = rel <= HALF_WINDOW
    s = jnp.where(valid, s - coef * rel.astype(F32), NEG)
    return q0, ks, q, k, v, s, valid


def _alibi_coef(group, d):
    h = (4 * group + 1 + pl.program_id(1)).astype(F32)
    slope = jnp.exp(jnp.full((1, 1), -(8.0 / N_HEADS_A) * math.log(2.0), F32) * h)
    return slope * float(d)


def _attn_a_fwd(qkv3, group, d):
    _, S, _ = qkv3.shape
    L = S // d
    assert L % QB == 0 and L >= KB
    view = qkv3.reshape(3, L, d * QKV_W)

    def body(qkv_ref, o_ref, lse_ref):
        coef = _alibi_coef(group, d)

        def step(i, carry):
            q0, _, _, _, v, s, _ = _band_scores(qkv_ref, i, L, coef)
            m = jnp.max(s, axis=-1, keepdims=True)
            p = jnp.exp(s - m)
            den = jnp.sum(p, axis=-1, keepdims=True)
            o_ref[pl.ds(q0, QB), :] = jnp.dot((p / den).astype(BF), v, preferred_element_type=F32)
            lse_ref[pl.ds(q0, QB), :] = jnp.broadcast_to(m + jnp.log(den), (QB, HEAD_DIM))
            return carry

        lax.fori_loop(0, L // QB, step, 0)

    out = pl.BlockSpec((L, HEAD_DIM), lambda r, j: (0, r * 4 + j))
    o, lse = ORDER.call(
        body, [view], [pl.BlockSpec((3, L, HEAD_DIM), lambda r, j: (0, 0, r * N_HEADS + 4 * group + j))],
        name=f"attn_a_fwd_d{d}", grid=(d, 4),
        out_specs=[out, out],
        out_shape=[_sds((L, d * 512), F32), _sds((L, d * 512), F32)],
        compiler_params=_cparams(("parallel", "parallel")),
    )
    return o.reshape(S, 512), lse.reshape(S, 512)


def _attn_a_combine(os_, lses):
    S, W = os_[0].shape
    tm = _tile(S, 512)

    def body(o0, o1, o2, l0, l1, l2, y_ref, lj_ref):
        ls = [l0[...], l1[...], l2[...]]
        m = jnp.maximum(jnp.maximum(ls[0], ls[1]), ls[2])
        es = [jnp.exp(l - m) for l in ls]
        den = es[0] + es[1] + es[2]
        y = (es[0] / den) * o0[...] + (es[1] / den) * o1[...] + (es[2] / den) * o2[...]
        y_ref[...] = y.astype(BF)
        lj_ref[...] = m + jnp.log(den)

    row = pl.BlockSpec((tm, W), lambda i: (i, 0))
    return ORDER.call(
        body, [*os_, *lses], [row] * 6, name="attn_a_combine", grid=(S // tm,), out_specs=[row, row],
        out_shape=[_sds((S, W), BF), _sds((S, W), F32)], compiler_params=_cparams(("parallel",)),
    )


def _attn_a_bwd(qkv3, dy, y, lj, dqkv3, group, d):
    _, S, _ = qkv3.shape
    L = S // d
    view = qkv3.reshape(3, L, d * QKV_W)

    def body(qkv_ref, dy_ref, y_ref, lj_ref, _, out_ref, dk_acc, dv_acc):
        coef = _alibi_coef(group, d)
        dk_acc[...] = jnp.zeros_like(dk_acc)
        dv_acc[...] = jnp.zeros_like(dv_acc)

        def step(i, carry):
            q0, ks, q, k, v, s, valid = _band_scores(qkv_ref, i, L, coef)
            rows = pl.ds(q0, QB)
            dyv = dy_ref[rows, :]
            delta = jnp.sum(dyv.astype(F32) * y_ref[rows, :].astype(F32), axis=-1, keepdims=True)
            p = jnp.where(valid, jnp.exp(s - jnp.tile(lj_ref[rows, :], (1, KB // HEAD_DIM))), 0.0)
            dp = lax.dot_general(dyv, v, (NT, ((), ())), preferred_element_type=F32)
            ds = (p * (dp - delta)).astype(BF)
            out_ref[0, rows, :] = (jnp.dot(ds, k, preferred_element_type=F32) * SCALE).astype(BF)
            keys = pl.ds(ks, KB)
            dk_acc[keys, :] += lax.dot_general(ds, q, (TN, ((), ())), preferred_element_type=F32) * SCALE
            dv_acc[keys, :] += lax.dot_general(p.astype(BF), dyv, (TN, ((), ())), preferred_element_type=F32)
            return carry

        lax.fori_loop(0, L // QB, step, 0)
        out_ref[1] = dk_acc[...].astype(BF)
        out_ref[2] = dv_acc[...].astype(BF)

    heads = pl.BlockSpec((3, L, HEAD_DIM), lambda r, j: (0, 0, r * N_HEADS + 4 * group + j))
    row = pl.BlockSpec((L, HEAD_DIM), lambda r, j: (0, r * 4 + j))
    out = ORDER.call(
        body, [view, dy.reshape(L, d * 512), y.reshape(L, d * 512), lj.reshape(L, d * 512),
               dqkv3.reshape(3, L, d * QKV_W)],
        [heads, row, row, row, pl.BlockSpec(memory_space=pl.ANY)], name=f"attn_a_bwd_d{d}", grid=(d, 4),
        out_specs=heads, out_shape=_sds((3, L, d * QKV_W), BF),
        scratch_shapes=[pltpu.VMEM((L, HEAD_DIM), F32), pltpu.VMEM((L, HEAD_DIM), F32)],
        input_output_aliases={4: 0},
        compiler_params=_cparams(("parallel", "parallel")),
    )
    return out.reshape(3, S, QKV_W)


def _toeplitz_onehot():
    oh = np.zeros((64, GRID_W, 128), np.float32)
    for qc in range(GRID_W):
        for m in range(128):
            kc = m % GRID_W
            dc = int(np.clip(kc - qc, -(NA_COLS - 1), NA_COLS - 1)) + NA_COLS - 1
            oh[(m // GRID_W) * 32 + dc, qc, m] = 1.0
    return oh.reshape(64, GRID_W * 128)


def _nbr_scores(qkv_ref, e2_ref, r, rows, ok):
    rs = jnp.clip(r - NA_ROWS // 2, 0, rows - NA_ROWS)
    q0 = pl.multiple_of(r * GRID_W, GRID_W)
    k0 = pl.multiple_of(rs * GRID_W, GRID_W)
    q = qkv_ref[0, pl.ds(q0, GRID_W), :]
    k = qkv_ref[1, pl.ds(k0, NA_ROWS * GRID_W), :]
    v = qkv_ref[2, pl.ds(k0, NA_ROWS * GRID_W), :]
    s = lax.dot_general(q, k, (NT, ((), ())), preferred_element_type=F32) * SCALE
    first = rs - r + NA_ROWS - 1
    bias = jnp.concatenate([e2_ref[first + 2 * pair] for pair in range(NA_ROWS // 2)], axis=1)
    s = jnp.where(ok, s + bias, NEG)
    return q0, k0, first, q, k, v, s


def _nbr_col_ok():
    qc = lax.broadcasted_iota(jnp.int32, (GRID_W, NA_ROWS * GRID_W), 0)
    kc = lax.broadcasted_iota(jnp.int32, (GRID_W, NA_ROWS * GRID_W), 1) % GRID_W
    cs = jnp.clip(qc - NA_COLS // 2, 0, GRID_W - NA_COLS)
    return (kc >= cs) & (kc < cs + NA_COLS)


def _attn_b_fwd(qkv3, e2):
    _, S, _ = qkv3.shape
    rows = S // GRID_W
    assert rows >= NA_ROWS

    def body(qkv_ref, e2_ref, o_ref, lse_ref):
        ok = _nbr_col_ok()

        def step(r, carry):
            q0, _, _, _, _, v, s = _nbr_scores(qkv_ref, e2_ref, r, rows, ok)
            m = jnp.max(s, axis=-1, keepdims=True)
            p = jnp.exp(s - m)
            den = jnp.sum(p, axis=-1, keepdims=True)
            o = jnp.dot((p / den).astype(BF), v, preferred_element_type=F32)
            o_ref[pl.ds(q0, GRID_W), :] = o.astype(BF)
            lse_ref[pl.ds(q0, GRID_W), :] = jnp.broadcast_to(m + jnp.log(den), (GRID_W, HEAD_DIM))
            return carry

        lax.fori_loop(0, rows, step, 0)

    out = pl.BlockSpec((S, HEAD_DIM), lambda h: (0, h))
    return ORDER.call(
        body, [qkv3, e2],
        [pl.BlockSpec((3, S, HEAD_DIM), lambda h: (0, 0, N_HEADS_A + h)),
         pl.BlockSpec((None, RPB_ROWS - 1, GRID_W, 128), lambda h: (h, 0, 0, 0))],
        name="attn_b_fwd", grid=(4,),
        out_specs=[out, out], out_shape=[_sds((S, 512), BF), _sds((S, 512), F32)],
        compiler_params=_cparams(("parallel",)),
    )


def _attn_b_bwd(qkv3, e2, dy, y, lse, dqkv3):
    _, S, _ = qkv3.shape
    rows = S // GRID_W
    nk = NA_ROWS * GRID_W

    def body(qkv_ref, e2_ref, dy_ref, y_ref, lse_ref, _, out_ref, de2_ref, dk_acc, dv_acc):
        ok = _nbr_col_ok()
        dk_acc[...] = jnp.zeros_like(dk_acc)
        dv_acc[...] = jnp.zeros_like(dv_acc)
        de2_ref[...] = jnp.zeros_like(de2_ref)

        def step(r, carry):
            q0, k0, first, q, k, v, s = _nbr_scores(qkv_ref, e2_ref, r, rows, ok)
            qrows = pl.ds(q0, GRID_W)
            dyv = dy_ref[qrows, :]
            delta = jnp.sum(dyv.astype(F32) * y_ref[qrows, :].astype(F32), axis=-1, keepdims=True)
            p = jnp.where(ok, jnp.exp(s - jnp.tile(lse_ref[qrows, :], (1, nk // HEAD_DIM))), 0.0)
            dp = lax.dot_general(dyv, v, (NT, ((), ())), preferred_element_type=F32)
            ds = p * (dp - delta)
            for pair in range(NA_ROWS // 2):
                de2_ref[first + 2 * pair] += ds[:, pair * 128:(pair + 1) * 128]
            dsb = ds.astype(BF)
            out_ref[0, qrows, :] = (jnp.dot(dsb, k, preferred_element_type=F32) * SCALE).astype(BF)
            keys = pl.ds(k0, nk)
            dk_acc[keys, :] += lax.dot_general(dsb, q, (TN, ((), ())), preferred_element_type=F32) * SCALE
            dv_acc[keys, :] += lax.dot_general(p.astype(BF), dyv, (TN, ((), ())), preferred_element_type=F32)
            return carry

        lax.fori_loop(0, rows, step, 0)
        out_ref[1] = dk_acc[...].astype(BF)
        out_ref[2] = dv_acc[...].astype(BF)

    heads = pl.BlockSpec((3, S, HEAD_DIM), lambda h: (0, 0, N_HEADS_A + h))
    row = pl.BlockSpec((S, HEAD_DIM), lambda h: (0, h))
    table = pl.BlockSpec((None, RPB_ROWS - 1, GRID_W, 128), lambda h: (h, 0, 0, 0))
    return ORDER.call(
        body, [qkv3, e2, dy, y, lse, dqkv3],
        [heads, table, row, row, row, pl.BlockSpec(memory_space=pl.ANY)], name="attn_b_bwd", grid=(4,),
        out_specs=[heads, table],
        out_shape=[_sds((3, S, QKV_W), BF), _sds((4, RPB_ROWS - 1, GRID_W, 128), F32)],
        scratch_shapes=[pltpu.VMEM((S, HEAD_DIM), F32), pltpu.VMEM((S, HEAD_DIM), F32)],
        input_output_aliases={5: 0},
        compiler_params=_cparams(("parallel",)), chain_output=1,
    )


def _rpb_to_table(rpb):
    pad = jnp.pad(rpb, ((0, 0), (0, 0), (0, 1)))
    pairs = jnp.concatenate([pad[:, :-1], pad[:, 1:]], axis=-1).reshape(4 * (RPB_ROWS - 1), 64)
    onehot = jnp.asarray(_toeplitz_onehot())
    n = onehot.shape[1]
    tn = 2048
    full = lambda i, j, k: (0, 0)
    (e2,) = _matmul("rpb_table", pairs, onehot, pl.BlockSpec(pairs.shape, full),
                    pl.BlockSpec((64, tn), lambda i, j, k: (0, j)), NN, (1, n // tn, 1), (pairs.shape[0], tn), [],
                    [(_sds((pairs.shape[0], n), F32), pl.BlockSpec((pairs.shape[0], tn), lambda i, j, k: (0, j)))],
                    _store(F32), precision=lax.Precision.HIGHEST)
    return e2.reshape(4, RPB_ROWS - 1, GRID_W, 128)


def _table_grad_to_rpb(de2):
    onehot = jnp.asarray(_toeplitz_onehot())
    n = onehot.shape[1]
    flat = de2.reshape(4 * (RPB_ROWS - 1), n)
    tk = 2048
    (dpairs,) = _matmul("rpb_table_grad", flat, onehot, pl.BlockSpec((flat.shape[0], tk), lambda i, j, k: (0, k)),
                        pl.BlockSpec((64, tk), lambda i, j, k: (0, k)), NT, (1, 1, n // tk), (flat.shape[0], 64), [],
                        [(_sds((flat.shape[0], 64), F32), pl.BlockSpec((flat.shape[0], 64), lambda i, j, k: (0, 0)))],
                        _store(F32), precision=lax.Precision.HIGHEST)
    dpairs = dpairs.reshape(4, RPB_ROWS - 1, 64)
    zero = jnp.zeros((4, 1, RPB_COLS), F32)
    return (jnp.concatenate([dpairs[:, :, :RPB_COLS], zero], axis=1)
            + jnp.concatenate([zero, dpairs[:, :, 32:32 + RPB_COLS]], axis=1))


HBM = pl.BlockSpec(memory_space=pl.ANY)


def _place():
    x, y, c = lax.axis_index("x"), lax.axis_index("y"), lax.axis_index("c")
    chips = [(1 - x, y), (x, 1 - y), (1 - x, 1 - y)]
    return x, y, c, chips


def _remote(src, dst, send_sem, recv_sem, to):
    return pltpu.make_async_remote_copy(src_ref=src, dst_ref=dst, send_sem=send_sem, recv_sem=recv_sem,
                                        device_id=to, device_id_type=MESH)


def _place_shard(name, w, me):
    R, C = w.shape
    tr = _tile(R, 256)

    def body(me_ref, w_ref, o_ref):
        o_ref[...] = w_ref[...].astype(BF)

    return ORDER.call(
        body, [w], [pl.BlockSpec((tr, C), lambda i, mr: (i, 0))], prefetch=(me,), name=name, grid=(R // tr,),
        out_specs=pl.BlockSpec((None, tr, C), lambda i, mr: (mr[0], i, 0)),
        out_shape=_sds((N_CHIPS, R, C), BF), compiler_params=_cparams(("parallel",)),
    )


SEM = pl.BlockSpec(memory_space=pltpu.SEMAPHORE)
IN_HBM = pl.BlockSpec(memory_space=pltpu.HBM)
DATAFLOW = pltpu.SideEffectType.DATAFLOW_SIDE_EFFECTING


def _in_hbm(a):
    return pltpu.with_memory_space_constraint(a, pltpu.HBM)


def _copy_start(name, bufs, copies, n_copies, earlier=None):
    n = len(bufs)
    after = None if any(b is ORDER.last for b in bufs) else ORDER.last
    n_extra = (2 if earlier is not None else 0) + (1 if after is not None else 0)

    def body(*refs):
        ins = refs[:n]
        if earlier is not None:
            for k, (src, dst, to) in enumerate(earlier[0](ins)):
                cp = _remote(src, dst, refs[n].at[k], refs[n + 1].at[k], to)
                cp.wait_send()
                cp.wait_recv()
        send_sems, recv_sems = refs[n + n_extra], refs[n + n_extra + 1]
        for k, (src, dst, to) in enumerate(copies(ins)):
            _remote(src, dst, send_sems.at[k], recv_sems.at[k], to).start()
        refs[-1][...] = jnp.zeros((8, 128), F32)

    operands = [_in_hbm(b) for b in bufs]
    in_specs = [IN_HBM] * n
    if earlier is not None:
        operands += [earlier[1], earlier[2]]
        in_specs += [SEM, SEM]
    if after is not None:
        operands.append(after)
        in_specs.append(HBM)
    outs = pl.pallas_call(
        body, name=name,
        out_shape=(pltpu.SemaphoreType.DMA((n_copies,)), pltpu.SemaphoreType.DMA((n_copies,)),
                   *[pltpu.HBM(b.shape, b.dtype) for b in bufs], _sds((8, 128), F32)),
        in_specs=in_specs,
        out_specs=(SEM, SEM, *[IN_HBM] * n, pl.BlockSpec(memory_space=pltpu.VMEM)),
        input_output_aliases={i: 2 + i for i in range(n)},
        compiler_params=pltpu.CompilerParams(has_side_effects=DATAFLOW),
    )(*operands)
    ORDER.last = outs[-1]
    return outs[0], outs[1], list(outs[2:2 + n])


def _copy_wait(name, bufs, copies, send_sems, recv_sems):
    n = len(bufs)
    after = ORDER.last

    def body(*refs):
        ins = refs[:n]
        for k, (src, dst, to) in enumerate(copies(ins)):
            cp = _remote(src, dst, refs[n].at[k], refs[n + 1].at[k], to)
            cp.wait_send()
            cp.wait_recv()

    outs = list(pl.pallas_call(
        body, name=name,
        out_shape=tuple(pltpu.HBM(b.shape, b.dtype) for b in bufs),
        in_specs=[IN_HBM] * n + [SEM, SEM, HBM], out_specs=tuple([IN_HBM] * n),
        input_output_aliases={i: i for i in range(n)},
        compiler_params=pltpu.CompilerParams(has_side_effects=DATAFLOW),
    )(*bufs, send_sems, recv_sems, after))
    ORDER.last = outs[0]
    return outs


def _gather_hop1(bufs):
    x, y, c, chips = _place()
    out = []
    for b in bufs:
        half = b.shape[1] // 2
        mine = b.at[2 * x + y, pl.ds(c * half, half), :]
        out += [(mine, mine, (*chip, c)) for chip in chips]
    return out


def _gather_hop2(bufs):
    x, y, c, chips = _place()
    out = []
    for b in bufs:
        half = b.shape[1] // 2
        for chip in chips:
            landed = b.at[2 * chip[0] + chip[1], pl.ds(c * half, half), :]
            out.append((landed, landed, (x, y, 1 - c)))
    return out


def _swap_copies(bufs):
    x, y, c, _ = _place()
    n = len(bufs) // 2
    out = []
    for p, land in zip(bufs[:n], bufs[n:]):
        half = p.shape[1] // 2
        out.append((p.at[:, pl.ds((1 - c) * half, half), :], land, (x, y, 1 - c)))
    return out


def _scatter_copies(bufs):
    _, _, c, chips = _place()
    n = len(bufs) // 2
    out = []
    for s_, land in zip(bufs[:n], bufs[n:]):
        out += [(s_.at[2 * chip[0] + chip[1]], land.at[j], (*chip, c)) for j, chip in enumerate(chips)]
    return out


def _join_copies(bufs):
    x, y, c, _ = _place()
    out = []
    for b in bufs:
        half = b.shape[0] // 2
        mine = b.at[pl.ds(c * half, half), :]
        out.append((mine, mine, (x, y, 1 - c)))
    return out


def _gather_small(vec):
    m_per, n = vec.shape

    def body(x_ref, out_ref, send_sems, recv_sems, local_sem):
        x, y, c, chips = _place()
        me, sibling = (x, y, c), (x, y, 1 - c)

        def rows(px, py, pc):
            return out_ref.at[pl.ds((4 * px + 2 * py + pc) * m_per, m_per), :]

        def copy(k, block, to, src=None):
            return _remote(rows(*block) if src is None else src, rows(*block), send_sems.at[k], recv_sems.at[k], to)

        mine = pltpu.make_async_copy(x_ref, rows(*me), local_sem)
        mine.start()
        first = [copy(0, me, sibling, src=x_ref)]
        first += [copy(1 + j, me, (*chip, c), src=x_ref) for j, chip in enumerate(chips)]
        for cp in first:
            cp.start()
        passed = [copy(4 + j, (*chip, c), sibling) for j, chip in enumerate(chips)]
        for j, chip in enumerate(chips):
            copy(1 + j, (*chip, c), me).wait_recv()
            passed[j].start()
        copy(0, sibling, me).wait_recv()
        for j, chip in enumerate(chips):
            copy(4 + j, (*chip, 1 - c), me).wait_recv()
        for cp in first + passed:
            cp.wait_send()
        mine.wait()

    return ORDER.call(
        body, [vec], [pl.BlockSpec(memory_space=pltpu.VMEM)], name="gather_small_grads",
        out_shape=_sds((8 * m_per, n), vec.dtype), out_specs=pl.BlockSpec(memory_space=pltpu.VMEM),
        scratch_shapes=[pltpu.SemaphoreType.DMA((7,)), pltpu.SemaphoreType.DMA((7,)), pltpu.SemaphoreType.DMA],
    )


def _add_sibling(name, partial, received, c):
    _, R, C = partial.shape
    half = R // 2
    tr = _tile(half, 256)
    nb = half // tr

    def body(c_ref, p_ref, r_ref, o_ref):
        o_ref[...] = (p_ref[...].astype(F32) + r_ref[...].astype(F32)).astype(BF)

    return ORDER.call(
        body, [partial, received],
        [pl.BlockSpec((None, tr, C), lambda j, i, cr: (j, cr[0] * nb + i, 0)),
         pl.BlockSpec((None, tr, C), lambda j, i, cr: (j, i, 0))],
        prefetch=(c,), name=name, grid=(N_CHIPS, nb),
        out_specs=pl.BlockSpec((None, tr, C), lambda j, i, cr: (j, i, 0)),
        out_shape=_sds((N_CHIPS, half, C), BF), compiler_params=_cparams(("parallel", "parallel")),
    )


def _add_chips(name, sums, received, me_c):
    _, half, C = sums.shape
    tr = _tile(half, 256)
    nb = half // tr

    def body(mc_ref, s_ref, r_ref, o_ref):
        acc = s_ref[...].astype(F32)
        for j in range(3):
            acc = acc + r_ref[j].astype(F32)
        o_ref[...] = acc

    return ORDER.call(
        body, [sums, received],
        [pl.BlockSpec((None, tr, C), lambda i, mc: (mc[0], i, 0)),
         pl.BlockSpec((3, tr, C), lambda i, mc: (0, i, 0))],
        prefetch=(me_c,), name=name, grid=(nb,),
        out_specs=pl.BlockSpec((tr, C), lambda i, mc: (mc[1] * nb + i, 0)),
        out_shape=_sds((2 * half, C), F32), compiler_params=_cparams(("parallel",)),
    )


def _adamw_math(w, g, m, v):
    m = ADAM_B1 * m + (1.0 - ADAM_B1) * g
    v = ADAM_B2 * v + (1.0 - ADAM_B2) * (g * g)
    m_hat = m / (1.0 - ADAM_B1 ** ADAM_STEP)
    v_hat = v / (1.0 - ADAM_B2 ** ADAM_STEP)
    delta = -ADAM_LR * (m_hat / (jnp.sqrt(v_hat) + ADAM_EPS) + ADAM_WD * w)
    return delta, m, v


def _adamw(name, w, g, m, v):
    R, C = w.shape
    tr = _tile(R, 128)

    def body(w_ref, g_ref, m_ref, v_ref, go_ref, d_ref, mo_ref, vo_ref):
        gv = g_ref[...]
        go_ref[...] = gv
        d_ref[...], mo_ref[...], vo_ref[...] = _adamw_math(w_ref[...], gv, m_ref[...], v_ref[...])

    row = pl.BlockSpec((tr, C), lambda i: (i, 0))
    return ORDER.call(
        body, [w, g, m, v], [row] * 4, name=name, grid=(R // tr,), out_specs=[row] * 4,
        out_shape=[_sds((R, C), F32)] * 4, compiler_params=_cparams(("parallel",)), chain_output=1,
    )


def _adamw_small(gathered, w, m, v):
    rows, n = w.shape

    def body(ga_ref, w_ref, m_ref, v_ref, go_ref, d_ref, mo_ref, vo_ref):
        g = ga_ref[pl.ds(0, rows), :]
        for dev in range(1, 8):
            g = g + ga_ref[pl.ds(dev * rows, rows), :]
        go_ref[...] = g
        d_ref[...], mo_ref[...], vo_ref[...] = _adamw_math(w_ref[...], g, m_ref[...], v_ref[...])

    whole = pl.BlockSpec(memory_space=pltpu.VMEM)
    return ORDER.call(
        body, [gathered, w, m, v], [whole] * 4, name="adamw_small", out_specs=[whole] * 4,
        out_shape=[_sds((rows, n), F32)] * 4, compiler_params=_cparams(), chain_output=1,
    )


class _Exchange:
    GATHER = (("qkv",), ("gate", "proj_a", "proj_b", "out"), ("up", "down"))
    REDUCE = {"mlp": ("down", "up"), "mix": ("out", "proj_a", "proj_b"), "in": ("qkv", "gate")}

    def __init__(self, shards, me, c):
        self.me, self.c = me, c
        self.hop1, self.hop2, self.stage, self.grads = {}, {}, {}, {}
        for g, names in enumerate(self.GATHER):
            bufs = [_place_shard(f"place_{n}", shards[n], me) for n in names]
            self.hop1[g] = _copy_start(f"gather{g}_start", bufs, _gather_hop1, 3 * len(names))

    def forward(self, g):
        send, recv, thru = self.hop1.pop(g)
        self.hop2[g] = _copy_start(f"gather{g}_forward", thru, _gather_hop2, len(thru) * 3,
                                   earlier=(_gather_hop1, send, recv))

    def weights(self, g):
        send, recv, thru = self.hop2.pop(g)
        return _copy_wait(f"gather{g}_wait", thru, _gather_hop2, send, recv)

    def reduce(self, key, partials=None):
        names = self.REDUCE[key]
        n = len(names)
        if partials is not None:
            lands = [lax.empty((p.shape[0], p.shape[1] // 2, p.shape[2]), p.dtype) for p in partials]
            self.stage[key] = ("swap",) + _copy_start(f"reduce_{key}_swap", list(partials) + lands, _swap_copies, n)
            return
        kind, send, recv, thru = self.stage.pop(key)
        if kind == "swap":
            thru = _copy_wait(f"reduce_{key}_swap_wait", thru, _swap_copies, send, recv)
            sums = [_add_sibling(f"reduce_{nm}_add_sibling", p, r, self.c)
                    for nm, p, r in zip(names, thru[:n], thru[n:])]
            lands = [lax.empty((3,) + s_.shape[1:], s_.dtype) for s_ in sums]
            self.stage[key] = ("scatter",) + _copy_start(f"reduce_{key}_scatter", sums + lands, _scatter_copies, 3 * n)
        elif kind == "scatter":
            thru = _copy_wait(f"reduce_{key}_scatter_wait", thru, _scatter_copies, send, recv)
            me_c = jnp.concatenate([self.me, self.c])
            halves = [_add_chips(f"reduce_{nm}_add_chips", s_, r, me_c)
                      for nm, s_, r in zip(names, thru[:n], thru[n:])]
            self.stage[key] = ("join",) + _copy_start(f"reduce_{key}_join", halves, _join_copies, n)
        else:
            thru = _copy_wait(f"reduce_{key}_join_wait", thru, _join_copies, send, recv)
            self.grads.update(zip(names, thru))


def _forward_backward(x, target, norm_mix, b_gate, rpb, norm_mlp, norm_final, ex):
    S, D = x.shape

    h1 = _rms_fwd("rms_mix", x, norm_mix)
    ex.forward(0)
    e2 = _rpb_to_table(rpb)
    (gq,) = ex.weights(0)
    nq = QKV_W // 512
    (qkv3,), _ = _mm_nn_cols(
        "qkv", h1, gq, BF, tn=512,
        outs=[(_sds((3, S, QKV_W), BF), pl.BlockSpec((None, _tile(S, 1024), 512), lambda i, j, k: (j // nq, i, j % nq)))])

    ex.forward(1)
    outs_a = [_attn_a_fwd(qkv3, 0, DILATIONS[0])]
    gg, gpa, gpb, gout = ex.weights(1)
    wout = gout.reshape(D, D)

    tg = _tile(gg.shape[2], 1024)
    ng = D // tg

    def gate_epilogue(acc, ex_, outs):
        outs[0][...] = jax.nn.sigmoid(acc + ex_[0][...])

    (g3,), _ = _mm_nn_cols(
        "gate", h1, gg, F32, epilogue=gate_epilogue, tn=tg,
        extras=[(b_gate, pl.BlockSpec((1, tg), lambda i, j, k: (0, j)))],
        outs=[(_sds((2, S, D), F32), pl.BlockSpec((None, _tile(S, 1024), tg), lambda i, j, k: (j // ng, i, j % ng)))])

    outs_a += [_attn_a_fwd(qkv3, grp, d) for grp, d in enumerate(DILATIONS) if grp > 0]
    y_a, lj = _attn_a_combine([o for o, _ in outs_a], [l for _, l in outs_a])
    y_b, lse_b = _attn_b_fwd(qkv3, e2)

    (pa,), (tm, tp, _) = _mm_nn_cols("proj_a", y_a, gpa, F32, tn=512)

    def merge_epilogue(acc, ex_, outs):
        g = ex_[0][...]
        outs[0][...] = acc
        outs[1][...] = (g[0] * ex_[1][...] + g[1] * acc).astype(BF)

    tile = pl.BlockSpec((tm, tp), lambda i, j, k: (i, j))
    gates = pl.BlockSpec((2, tm, tp), lambda i, j, k: (0, i, j))
    (pb, merged), _ = _mm_nn_cols(
        "proj_b_merge", y_b, gpb, F32, epilogue=merge_epilogue, tn=512,
        extras=[(g3, gates), (pa, tile)],
        outs=[(_sds((S, D), F32), tile), (_sds((S, D), BF), tile)])

    def residual_epilogue(acc, ex_, outs):
        outs[0][...] = acc + ex_[0][...]

    def nn_plain(name, a, w, res):
        M, K = a.shape
        N = w.shape[1]
        bm, bn, bk = _tile(M, 1024), _tile(N, 1024), _tile(K, 1024)
        t = pl.BlockSpec((bm, bn), lambda i, j, k: (i, j))
        return _matmul(name, a, w, pl.BlockSpec((bm, bk), lambda i, j, k: (i, k)),
                       pl.BlockSpec((bk, bn), lambda i, j, k: (k, j)), NN, (M // bm, N // bn, K // bk), (bm, bn),
                       [(res, t)], [(_sds((M, N), F32), t)], residual_epilogue)[0]

    ex.forward(2)
    x1 = nn_plain("out_proj", merged, wout, x)
    h2 = _rms_fwd("rms_mlp", x1, norm_mlp)
    gup, gdown = ex.weights(2)
    F = gup.shape[2] * N_CHIPS
    wdown = gdown.reshape(F, D)

    def up_epilogue(acc, ex_, outs):
        ru = jnp.maximum(acc, 0.0)
        outs[0][...] = (ru * ru).astype(BF)
        outs[1][...] = ru.astype(BF)

    tu = _tile(gup.shape[2], 1024)
    ut = pl.BlockSpec((_tile(S, 1024), tu), lambda i, j, k: (i, j))
    (act, ru), _ = _mm_nn_cols("mlp_up", h2, gup, BF, epilogue=up_epilogue, tn=tu,
                               outs=[(_sds((S, F), BF), ut), (_sds((S, F), BF), ut)])
    x2 = nn_plain("mlp_down", act, wdown, x1)

    loss, dx2, dx2b, d_norm_final = _loss_head(x2, target, norm_final.reshape(1, D))

    def nt_rows(name, a, w, epilogue, extras, outs, bn=1024):
        M, N = a.shape
        K = w.shape[0]
        bm, bn, bk = _tile(M, 1024), _tile(K, bn), _tile(N, 1024)
        return _matmul(name, a, w, pl.BlockSpec((bm, bk), lambda i, j, k: (i, k)),
                       pl.BlockSpec((bn, bk), lambda i, j, k: (j, k)), NT, (M // bm, K // bn, N // bk), (bm, bn),
                       extras(bm, bn), outs(bm, bn), epilogue)

    def nt_cols(name, a_spec_fn, a, g, M, epilogue, extras, outs, bk):
        _, K, Nq = g.shape
        bm, bn, bk = _tile(M, 1024), _tile(K, 1024), _tile(Nq, bk)
        q = Nq // bk
        return _matmul(name, a, g, a_spec_fn(bm, bk), pl.BlockSpec((None, bn, bk), lambda i, j, k: (k // q, j, k % q)),
                       NT, (M // bm, K // bn, N_CHIPS * q), (bm, bn), extras(bm, bn), outs(bm, bn), epilogue)

    def tn_grad(name, a, a_spec_fn, b, b_spec_fn, Kin, N, out_shape, out_spec_fn, bn=1024):
        bm, bn, bk = _tile(Kin, 1024), _tile(N, bn), _tile(S, 1024)
        return _matmul(name, a, b, a_spec_fn(bk, bm), b_spec_fn(bk, bn), TN, (Kin // bm, N // bn, S // bk), (bm, bn),
                       [], [(_sds(out_shape, BF), out_spec_fn(bm, bn))], _store(BF))[0]

    plain_a = lambda bk, bm: pl.BlockSpec((bk, bm), lambda i, j, k: (k, i))
    plain_b = lambda bk, bn: pl.BlockSpec((bk, bn), lambda i, j, k: (k, j))
    plain_o = lambda bm, bn: pl.BlockSpec((bm, bn), lambda i, j, k: (i, j))
    a_rows = lambda bm, bk: pl.BlockSpec((bm, bk), lambda i, j, k: (i, k))

    def cols_o(Nq):
        def spec(bm, bn):
            q = Nq // bn
            return pl.BlockSpec((None, bm, bn), lambda i, j, k: (j // q, i, j % q))
        return spec

    def du_epilogue(acc, ex_, outs):
        outs[0][...] = (acc * (2.0 * ex_[0][...].astype(F32))).astype(BF)

    dw_down = tn_grad("mlp_down_dw", act, plain_a, dx2b, plain_b, F, D, (F, D), plain_o)
    (du,) = nt_rows("mlp_down_dx", dx2b, wdown, du_epilogue,
                    lambda bm, bn: [(ru, plain_o(bm, bn))], lambda bm, bn: [(_sds((S, F), BF), plain_o(bm, bn))])

    fq = gup.shape[2]
    dw_up = tn_grad("mlp_up_dw", h2, plain_a, du, plain_b, D, F, (N_CHIPS, D, fq), cols_o(fq), bn=min(fq, 1024))
    ex.reduce("mlp", partials=[dw_down.reshape(N_CHIPS, F // N_CHIPS, D), dw_up])
    (dh2,) = nt_cols("mlp_up_dx", a_rows, du, gup, S, _store(F32), lambda bm, bn: [],
                     lambda bm, bn: [(_sds((S, D), F32), plain_o(bm, bn))], 1024)
    ex.reduce("mlp")
    dx1, dx1b, d_norm_mlp = _rms_bwd("rms_mlp_bwd", dh2, x1, norm_mlp, dx2)

    def merge_bwd_epilogue(acc, ex_, outs):
        g, pav, pbv = ex_[0][...], ex_[1][...], ex_[2][...]
        outs[0][...] = (acc * g[0]).astype(BF)
        outs[1][...] = (acc * g[1]).astype(BF)
        dga = acc * pav * g[0] * (1.0 - g[0])
        dgb = acc * pbv * g[1] * (1.0 - g[1])
        outs[2][0] = dga.astype(BF)
        outs[2][1] = dgb.astype(BF)
        outs[3][...] = jnp.concatenate([jnp.sum(dga, axis=0, keepdims=True), jnp.sum(dgb, axis=0, keepdims=True)], 0)

    def pair(bm, bn):
        return pl.BlockSpec((2, bm, bn), lambda i, j, k: (0, i, j))

    n_row_blocks = S // _tile(S, 1024)
    dpa, dpb, dg3, db_gate = nt_rows(
        "out_proj_dx", dx1b, wout, merge_bwd_epilogue,
        lambda bm, bn: [(g3, pair(bm, bn)), (pa, plain_o(bm, bn)), (pb, plain_o(bm, bn))],
        lambda bm, bn: [(_sds((S, D), BF), plain_o(bm, bn)), (_sds((S, D), BF), plain_o(bm, bn)),
                        (_sds((2, S, D), BF), pair(bm, bn)),
                        (_sds((n_row_blocks, 2, D), F32), pl.BlockSpec((None, 2, bn), lambda i, j, k: (i, 0, j)))],
        bn=512)
    dw_out = tn_grad("out_proj_dw", merged, plain_a, dx1b, plain_b, D, D, (D, D), plain_o)

    pq = gpa.shape[2]
    proj_dx = lambda name, dproj, g: nt_cols(name, a_rows, dproj, g, S, _store(BF), lambda bm, bn: [],
                                             lambda bm, bn: [(_sds((S, 512), BF), plain_o(bm, bn))], 512)[0]
    dw_pa = tn_grad("proj_a_dw", y_a, plain_a, dpa, plain_b, 512, D, (N_CHIPS, 512, pq), cols_o(pq), bn=min(pq, 512))
    dw_pb = tn_grad("proj_b_dw", y_b, plain_a, dpb, plain_b, 512, D, (N_CHIPS, 512, pq), cols_o(pq), bn=min(pq, 512))
    ex.reduce("mix", partials=[dw_out.reshape(N_CHIPS, D // N_CHIPS, D), dw_pa, dw_pb])
    dy_a = proj_dx("proj_a_dx", dpa, gpa)
    dy_b = proj_dx("proj_b_dx", dpb, gpb)

    dqkv3 = lax.empty((3, S, QKV_W), BF)
    dqkv3 = _attn_a_bwd(qkv3, dy_a, y_a, lj, dqkv3, 0, DILATIONS[0])
    ex.reduce("mix")
    for grp, d in enumerate(DILATIONS):
        if grp > 0:
            dqkv3 = _attn_a_bwd(qkv3, dy_a, y_a, lj, dqkv3, grp, d)
    dqkv3, de2 = _attn_b_bwd(qkv3, e2, dy_b, y_b, lse_b, dqkv3)
    d_rpb = _table_grad_to_rpb(de2)

    def stacked_a(width):
        def spec(bm, bk):
            q = width // bk
            return pl.BlockSpec((None, bm, bk), lambda i, j, k: (k // q, i, k % q))
        return spec

    def stacked_b(width):
        def spec(bk, bn):
            q = width // bn
            return pl.BlockSpec((None, bk, bn), lambda i, j, k: (j // q, k, j % q))
        return spec

    dw_qkv = tn_grad("qkv_dw", h1, plain_a, dqkv3, stacked_b(QKV_W), D, 3 * QKV_W, (N_CHIPS,) + gq.shape[1:],
                     cols_o(gq.shape[2]), bn=512)
    dw_gate = tn_grad("gate_dw", h1, plain_a, dg3, stacked_b(D), D, 2 * D, (N_CHIPS,) + gg.shape[1:],
                      cols_o(gg.shape[2]), bn=gg.shape[2])
    ex.reduce("in", partials=[dw_qkv, dw_gate])
    ex.reduce("mlp")
    (dh1_q,) = nt_cols("qkv_dx", stacked_a(QKV_W), dqkv3, gq, S, _store(F32), lambda bm, bn: [],
                       lambda bm, bn: [(_sds((S, D), F32), plain_o(bm, bn))], 512)
    ex.reduce("in")
    ex.reduce("mix")

    def add_epilogue(acc, ex_, outs):
        outs[0][...] = acc + ex_[0][...]

    (dh1,) = nt_cols("gate_dx", stacked_a(D), dg3, gg, S, add_epilogue, lambda bm, bn: [(dh1_q, plain_o(bm, bn))],
                     lambda bm, bn: [(_sds((S, D), F32), plain_o(bm, bn))], gg.shape[2])
    grad_x, _, d_norm_mix = _rms_bwd("rms_mix_bwd", dh1, x, norm_mix, dx1)
    ex.reduce("mlp")
    ex.reduce("mix")

    small = [d_norm_mix, jnp.sum(db_gate, axis=0).reshape(1, 2 * D), d_rpb, d_norm_mlp, d_norm_final]
    return loss, grad_x, small


def _pack_small(parts, width):
    flat = jnp.concatenate([p.reshape(-1) for p in parts])
    return jnp.pad(flat, (0, 8 * width - flat.shape[0])).reshape(8, width)


def kernel(x, norm_mix, w_qkv, w_gate, b_gate, rpb, w_proj_a, w_proj_b, w_out, norm_mlp, w_up, w_down, norm_final, loss_target, m_norm_mix, m_w_qkv, m_w_gate, m_b_gate, m_rpb, m_w_proj_a, m_w_proj_b, m_w_out, m_norm_mlp, m_w_up, m_w_down, m_norm_final, v_norm_mix, v_w_qkv, v_w_gate, v_b_gate, v_rpb, v_w_proj_a, v_w_proj_b, v_w_out, v_norm_mlp, v_w_up, v_w_down, v_norm_final):
    names = ["qkv", "gate", "proj_a", "proj_b", "out", "up", "down"]
    big = dict(zip(names, [w_qkv[0], w_gate[0], w_proj_a[0], w_proj_b[0], w_out[0], w_up[0], w_down[0]]))
    big_m = dict(zip(names, [m_w_qkv[0], m_w_gate[0], m_w_proj_a[0], m_w_proj_b[0], m_w_out[0], m_w_up[0], m_w_down[0]]))
    big_v = dict(zip(names, [v_w_qkv[0], v_w_gate[0], v_w_proj_a[0], v_w_proj_b[0], v_w_out[0], v_w_up[0], v_w_down[0]]))

    c = lax.axis_index("c").astype(jnp.int32).reshape(1)
    me = (2 * lax.axis_index("x") + lax.axis_index("y")).astype(jnp.int32).reshape(1)
    ORDER.last = None
    ex = _Exchange(big, me, c)
    loss, grad_x, small = _forward_backward(x[0], loss_target[0], norm_mix, b_gate, rpb[0], norm_mlp, norm_final, ex)

    def adamw(group):
        return {n: _adamw(f"adamw_{n}", big[n], ex.grads[n], big_m[n], big_v[n]) for n in _Exchange.REDUCE[group]}

    big_out = {**adamw("mlp"), **adamw("mix")}
    ex.reduce("in")

    small_w = [norm_mix, b_gate, rpb, norm_mlp, norm_final]
    count = sum(int(np.prod(p.shape)) for p in small_w)
    width = -(-count // (8 * 128)) * 128
    packed = _adamw_small(_gather_small(_pack_small(small, width)), _pack_small(small_w, width),
                          _pack_small([m_norm_mix, m_b_gate, m_rpb, m_norm_mlp, m_norm_final], width),
                          _pack_small([v_norm_mix, v_b_gate, v_rpb, v_norm_mlp, v_norm_final], width))
    ex.reduce("in")
    big_out.update(adamw("in"))

    def unpack(flat2d):
        flat, out, at = flat2d.reshape(-1), [], 0
        for p in small_w:
            size = int(np.prod(p.shape))
            out.append(flat[at:at + size].reshape(p.shape))
            at += size
        return out

    small_out = [unpack(a) for a in packed]

    def ordered(kind):
        sm = small_out[kind]
        bg = {n: o[kind][None] for n, o in big_out.items()}
        return [sm[0], bg["qkv"], bg["gate"], sm[1], sm[2], bg["proj_a"], bg["proj_b"], bg["out"], sm[3],
                bg["up"], bg["down"], sm[4]]

    total = lax.psum(loss[0, 0], ("x", "y", "c"))
    return (total, grad_x[None], *ordered(0), *ordered(1), *ordered(2), *ordered(3))
```

```python
import functools
import math

import numpy as np
import jax
import jax.numpy as jnp
from jax import lax
from jax.experimental import pallas as pl
from jax.experimental.pallas import tpu as pltpu

BF = jnp.bfloat16
F32 = jnp.float32
MESH = pl.DeviceIdType.MESH

HEAD_DIM = 128
N_HEADS = 16
N_HEADS_A = 12
QKV_W = N_HEADS * HEAD_DIM
DILATIONS = (1, 4, 16)
HALF_WINDOW = 64
GRID_W = 64
NA_ROWS = 8
NA_COLS = 16
RPB_ROWS = 2 * NA_ROWS - 1
RPB_COLS = 2 * NA_COLS - 1
EPS = 1e-6
NEG = -1e30
SCALE = HEAD_DIM ** -0.5

ADAM_LR = 0.001
ADAM_B1 = 0.9
ADAM_B2 = 0.999
ADAM_EPS = 1e-08
ADAM_WD = 0.01
ADAM_STEP = 10

N_CHIPS = 4
VMEM_LIMIT_BYTES = 48 * 1024 * 1024
QB = 128
KB = QB + 2 * HALF_WINDOW


def _cparams(sem=None):
    return pltpu.CompilerParams(dimension_semantics=sem, vmem_limit_bytes=VMEM_LIMIT_BYTES)


def _tile(dim, want):
    t = min(dim, want)
    assert dim % t == 0, (dim, want)
    return t


class _ProgramOrder:
    def __init__(self):
        self.last = None

    def call(self, body, operands, in_specs, *, prefetch=(), grid=None, out_specs=None, chain_output=0, **kwargs):
        operands, in_specs = list(operands), list(in_specs)
        lead = len(prefetch) + len(operands)
        if self.last is not None and not any(op is self.last for op in operands):
            operands.append(self.last)
            in_specs.append(pl.BlockSpec(memory_space=pl.ANY))
            inner = body

            def body(*refs):
                return inner(*refs[:lead], *refs[lead + 1:])

        if prefetch:
            kwargs["grid_spec"] = pltpu.PrefetchScalarGridSpec(
                num_scalar_prefetch=len(prefetch), grid=grid, in_specs=in_specs, out_specs=out_specs)
        else:
            kwargs.update(in_specs=in_specs, out_specs=out_specs)
            if grid is not None:
                kwargs["grid"] = grid
        out = pl.pallas_call(body, **kwargs)(*prefetch, *operands)
        self.last = out[chain_output] if isinstance(out, (tuple, list)) else out
        return out


ORDER = _ProgramOrder()


NN = ((1,), (0,))
NT = ((1,), (1,))
TN = ((0,), (0,))


def _matmul(name, a, b, a_spec, b_spec, dims, grid, acc_shape, extras, outs, epilogue, precision=None):
    n_ex, n_out, nk = len(extras), len(outs), grid[2]

    def body(*refs):
        a_ref, b_ref = refs[0], refs[1]
        ex_refs = refs[2:2 + n_ex]
        out_refs = refs[2 + n_ex:2 + n_ex + n_out]

        def dot():
            return lax.dot_general(a_ref[...], b_ref[...], (dims, ((), ())),
                                   preferred_element_type=F32, precision=precision)

        if nk == 1:
            epilogue(dot(), ex_refs, out_refs)
            return
        acc_ref = refs[-1]
        k = pl.program_id(2)

        @pl.when(k == 0)
        def _():
            acc_ref[...] = dot()

        if nk > 2:
            @pl.when((k > 0) & (k < nk - 1))
            def _():
                acc_ref[...] += dot()

        @pl.when(k == nk - 1)
        def _():
            epilogue(acc_ref[...] + dot(), ex_refs, out_refs)

    return ORDER.call(
        body, [a, b] + [e for e, _ in extras], [a_spec, b_spec] + [s for _, s in extras], name=name, grid=grid,
        out_specs=[s for _, s in outs],
        out_shape=[sh for sh, _ in outs],
        scratch_shapes=[pltpu.VMEM(acc_shape, F32)] if nk > 1 else [],
        compiler_params=_cparams(("parallel", "parallel", "arbitrary")),
    )


def _store(dtype):
    def epilogue(acc, ex, outs):
        outs[0][...] = acc.astype(dtype)
    return epilogue


def _sds(shape, dtype):
    return jax.ShapeDtypeStruct(shape, dtype)


def _mm_nn_cols(name, a, g, out_dtype, epilogue=None, extras=(), outs=None, tm=1024, tn=1024, tk=2048):
    M, K = a.shape
    _, _, Nq = g.shape
    tm, tn, tk = _tile(M, tm), _tile(Nq, tn), _tile(K, tk)
    q = Nq // tn
    grid = (M // tm, N_CHIPS * q, K // tk)
    if outs is None:
        outs = [(_sds((M, N_CHIPS * Nq), out_dtype), pl.BlockSpec((tm, tn), lambda i, j, k: (i, j)))]
    return _matmul(name, a, g, pl.BlockSpec((tm, tk), lambda i, j, k: (i, k)),
                   pl.BlockSpec((None, tk, tn), lambda i, j, k: (j // q, k, j % q)), NN, grid, (tm, tn),
                   list(extras), outs, epilogue or _store(out_dtype)), (tm, tn, tk)


def _rms_fwd(name, x, g):
    S, D = x.shape
    tm = _tile(S, 256)

    def body(x_ref, g_ref, h_ref):
        xv = x_ref[...]
        r = lax.rsqrt(jnp.mean(xv * xv, axis=-1, keepdims=True) + EPS)
        h_ref[...] = ((xv * r) * g_ref[...]).astype(BF)

    row = pl.BlockSpec((tm, D), lambda i: (i, 0))
    return ORDER.call(
        body, [x, g], [row, pl.BlockSpec((1, D), lambda i: (0, 0))], name=name, grid=(S // tm,),
        out_specs=row, out_shape=_sds((S, D), BF), compiler_params=_cparams(("parallel",)),
    )


def _rms_bwd(name, dh, x, g, dres):
    S, D = x.shape
    tm = _tile(S, 256)

    def body(dh_ref, x_ref, g_ref, dres_ref, dx_ref, dxb_ref, dg_ref):
        xv = x_ref[...]
        r = lax.rsqrt(jnp.mean(xv * xv, axis=-1, keepdims=True) + EPS)
        n = xv * r
        dhv = dh_ref[...]
        dyg = dhv * g_ref[...]
        dx = dres_ref[...] + r * (dyg - n * jnp.mean(dyg * n, axis=-1, keepdims=True))
        dx_ref[...] = dx
        dxb_ref[...] = dx.astype(BF)

        @pl.when(pl.program_id(0) == 0)
        def _():
            dg_ref[...] = jnp.zeros_like(dg_ref)

        dg_ref[...] += jnp.sum(dhv * n, axis=0, keepdims=True)

    row = pl.BlockSpec((tm, D), lambda i: (i, 0))
    vec = pl.BlockSpec((1, D), lambda i: (0, 0))
    return ORDER.call(
        body, [dh, x, g, dres], [row, row, vec, row], name=name, grid=(S // tm,),
        out_specs=[row, row, vec],
        out_shape=[_sds((S, D), F32), _sds((S, D), BF), _sds((1, D), F32)],
        compiler_params=_cparams(("arbitrary",)),
    )


def _loss_head(x2, target, g):
    S, D = x2.shape
    tm = _tile(S, 256)

    def body(x_ref, t_ref, g_ref, loss_ref, dx_ref, dxb_ref, dg_ref):
        xv = x_ref[...]
        gv = g_ref[...]
        r = lax.rsqrt(jnp.mean(xv * xv, axis=-1, keepdims=True) + EPS)
        n = xv * r
        e = n * gv - t_ref[...]
        dy = e * (1.0 / D)
        dyg = dy * gv
        dx = r * (dyg - n * jnp.mean(dyg * n, axis=-1, keepdims=True))
        dx_ref[...] = dx
        dxb_ref[...] = dx.astype(BF)

        @pl.when(pl.program_id(0) == 0)
        def _():
            dg_ref[...] = jnp.zeros_like(dg_ref)
            loss_ref[...] = jnp.zeros_like(loss_ref)

        dg_ref[...] += jnp.sum(dy * n, axis=0, keepdims=True)
        per_row = jnp.mean(e * e, axis=-1, keepdims=True)
        loss_ref[...] += 0.5 * jnp.sum(per_row, axis=0, keepdims=True)

    row = pl.BlockSpec((tm, D), lambda i: (i, 0))
    vec = pl.BlockSpec((1, D), lambda i: (0, 0))
    return ORDER.call(
        body, [x2, target, g], [row, row, vec], name="loss_head", grid=(S // tm,),
        out_specs=[pl.BlockSpec((1, 1), lambda i: (0, 0)), row, row, vec],
        out_shape=[_sds((1, 1), F32), _sds((S, D), F32), _sds((S, D), BF), _sds((1, D), F32)],
        compiler_params=_cparams(("arbitrary",)), chain_output=1,
    )


def _band_scores(qkv_ref, i, L, coef):
    q0 = pl.multiple_of(i * QB, QB)
    ks = pl.multiple_of(jnp.clip(i * QB - HALF_WINDOW, 0, L - KB), HALF_WINDOW)
    q = qkv_ref[0, pl.ds(q0, QB), :]
    k = qkv_ref[1, pl.ds(ks, KB), :]
    v = qkv_ref[2, pl.ds(ks, KB), :]
    s = lax.dot_general(q, k, (NT, ((), ())), preferred_element_type=F32) * SCALE
    qpos = q0 + lax.broadcasted_iota(jnp.int32, (QB, KB), 0)
    kpos = ks + lax.broadcasted_iota(jnp.int32, (QB, KB), 1)
    rel = jnp.abs(kpos - qpos)
    valid = rel <= HALF_WINDOW
    s = jnp.where(valid, s - coef * rel.astype(F32), NEG)
    return q0, ks, q, k, v, s, valid


def _alibi_coef(group, d):
    h = (4 * group + 1 + pl.program_id(1)).astype(F32)
    slope = jnp.exp(jnp.full((1, 1), -(8.0 / N_HEADS_A) * math.log(2.0), F32) * h)
    return slope * float(d)


def _dilated_view(qkv3, group, d):
    _, S, _ = qkv3.shape
    L = S // d
    if d == 1:
        return qkv3, pl.BlockSpec((3, L, HEAD_DIM), lambda r, j: (0, 0, 4 * group + j))
    cols = qkv3[:, :, 512 * group:512 * (group + 1)].reshape(3, L, d * 512)
    return cols, pl.BlockSpec((3, L, HEAD_DIM), lambda r, j: (0, 0, r * 4 + j))


def _attn_a_fwd(qkv3, group, d):
    _, S, _ = qkv3.shape
    L = S // d
    assert L % QB == 0 and L >= KB
    view, heads = _dilated_view(qkv3, group, d)

    def body(qkv_ref, o_ref, lse_ref):
        coef = _alibi_coef(group, d)

        def step(i, carry):
            q0, _, _, _, v, s, _ = _band_scores(qkv_ref, i, L, coef)
            m = jnp.max(s, axis=-1, keepdims=True)
            p = jnp.exp(s - m)
            den = jnp.sum(p, axis=-1, keepdims=True)
            o_ref[pl.ds(q0, QB), :] = jnp.dot((p / den).astype(BF), v, preferred_element_type=F32)
            lse_ref[pl.ds(q0, QB), :] = jnp.broadcast_to(m + jnp.log(den), (QB, HEAD_DIM))
            return carry

        lax.fori_loop(0, L // QB, step, 0)

    out = pl.BlockSpec((L, HEAD_DIM), lambda r, j: (0, r * 4 + j))
    o, lse = ORDER.call(
        body, [view], [heads],
        name=f"attn_a_fwd_d{d}", grid=(d, 4),
        out_specs=[out, out],
        out_shape=[_sds((L, d * 512), F32), _sds((L, d * 512), F32)],
        compiler_params=_cparams(("parallel", "parallel")),
    )
    return o.reshape(S, 512), lse.reshape(S, 512)


def _attn_a_combine(os_, lses):
    S, W = os_[0].shape
    tm = _tile(S, 512)

    def body(o0, o1, o2, l0, l1, l2, y_ref, lj_ref):
        ls = [l0[...], l1[...], l2[...]]
        m = jnp.maximum(jnp.maximum(ls[0], ls[1]), ls[2])
        es = [jnp.exp(l - m) for l in ls]
        den = es[0] + es[1] + es[2]
        y = (es[0] / den) * o0[...] + (es[1] / den) * o1[...] + (es[2] / den) * o2[...]
        y_ref[...] = y.astype(BF)
        lj_ref[...] = m + jnp.log(den)

    row = pl.BlockSpec((tm, W), lambda i: (i, 0))
    return ORDER.call(
        body, [*os_, *lses], [row] * 6, name="attn_a_combine", grid=(S // tm,), out_specs=[row, row],
        out_shape=[_sds((S, W), BF), _sds((S, W), F32)], compiler_params=_cparams(("parallel",)),
    )


def _attn_a_bwd(qkv3, dy, y, lj, dqkv3, group, d):
    _, S, _ = qkv3.shape
    L = S // d
    view, heads = _dilated_view(qkv3, group, d)

    def body(qkv_ref, dy_ref, y_ref, lj_ref, *rest):
        out_ref, dk_acc, dv_acc = rest[-3:]
        coef = _alibi_coef(group, d)
        dk_acc[...] = jnp.zeros_like(dk_acc)
        dv_acc[...] = jnp.zeros_like(dv_acc)

        def step(i, carry):
            q0, ks, q, k, v, s, valid = _band_scores(qkv_ref, i, L, coef)
            rows = pl.ds(q0, QB)
            dyv = dy_ref[rows, :]
            delta = jnp.sum(dyv.astype(F32) * y_ref[rows, :].astype(F32), axis=-1, keepdims=True)
            p = jnp.where(valid, jnp.exp(s - jnp.tile(lj_ref[rows, :], (1, KB // HEAD_DIM))), 0.0)
            dp = lax.dot_general(dyv, v, (NT, ((), ())), preferred_element_type=F32)
            ds = (p * (dp - delta)).astype(BF)
            out_ref[0, rows, :] = (jnp.dot(ds, k, preferred_element_type=F32) * SCALE).astype(BF)
            keys = pl.ds(ks, KB)
            dk_acc[keys, :] += lax.dot_general(ds, q, (TN, ((), ())), preferred_element_type=F32) * SCALE
            dv_acc[keys, :] += lax.dot_general(p.astype(BF), dyv, (TN, ((), ())), preferred_element_type=F32)
            return carry

        lax.fori_loop(0, L // QB, step, 0)
        out_ref[1] = dk_acc[...].astype(BF)
        out_ref[2] = dv_acc[...].astype(BF)

    row = pl.BlockSpec((L, HEAD_DIM), lambda r, j: (0, r * 4 + j))
    operands = [view, dy.reshape(L, d * 512), y.reshape(L, d * 512), lj.reshape(L, d * 512)]
    scratch = [pltpu.VMEM((L, HEAD_DIM), F32), pltpu.VMEM((L, HEAD_DIM), F32)]
    if d == 1:
        return ORDER.call(
            body, operands + [dqkv3], [heads, row, row, row, pl.BlockSpec(memory_space=pl.ANY)],
            name=f"attn_a_bwd_d{d}", grid=(d, 4), out_specs=heads, out_shape=_sds((3, S, QKV_W), BF),
            scratch_shapes=scratch, input_output_aliases={4: 0}, compiler_params=_cparams(("parallel", "parallel")))
    out = ORDER.call(
        body, operands, [heads, row, row, row], name=f"attn_a_bwd_d{d}", grid=(d, 4),
        out_specs=heads, out_shape=_sds((3, L, d * 512), BF),
        scratch_shapes=scratch, compiler_params=_cparams(("parallel", "parallel")))
    return lax.dynamic_update_slice(dqkv3, out.reshape(3, S, 512), (0, 0, 512 * group))


def _toeplitz_onehot():
    oh = np.zeros((64, GRID_W, 128), np.float32)
    for qc in range(GRID_W):
        for m in range(128):
            kc = m % GRID_W
            dc = int(np.clip(kc - qc, -(NA_COLS - 1), NA_COLS - 1)) + NA_COLS - 1
            oh[(m // GRID_W) * 32 + dc, qc, m] = 1.0
    return oh.reshape(64, GRID_W * 128)


def _nbr_scores(qkv_ref, e2_ref, r, rows, ok):
    rs = jnp.clip(r - NA_ROWS // 2, 0, rows - NA_ROWS)
    q0 = pl.multiple_of(r * GRID_W, GRID_W)
    k0 = pl.multiple_of(rs * GRID_W, GRID_W)
    q = qkv_ref[0, pl.ds(q0, GRID_W), :]
    k = qkv_ref[1, pl.ds(k0, NA_ROWS * GRID_W), :]
    v = qkv_ref[2, pl.ds(k0, NA_ROWS * GRID_W), :]
    s = lax.dot_general(q, k, (NT, ((), ())), preferred_element_type=F32) * SCALE
    first = rs - r + NA_ROWS - 1
    bias = jnp.concatenate([e2_ref[first + 2 * pair] for pair in range(NA_ROWS // 2)], axis=1)
    s = jnp.where(ok, s + bias, NEG)
    return q0, k0, first, q, k, v, s


def _nbr_col_ok():
    qc = lax.broadcasted_iota(jnp.int32, (GRID_W, NA_ROWS * GRID_W), 0)
    kc = lax.broadcasted_iota(jnp.int32, (GRID_W, NA_ROWS * GRID_W), 1) % GRID_W
    cs = jnp.clip(qc - NA_COLS // 2, 0, GRID_W - NA_COLS)
    return (kc >= cs) & (kc < cs + NA_COLS)


def _attn_b_fwd(qkv3, e2):
    _, S, _ = qkv3.shape
    rows = S // GRID_W
    assert rows >= NA_ROWS

    def body(qkv_ref, e2_ref, o_ref, lse_ref):
        ok = _nbr_col_ok()

        def step(r, carry):
            q0, _, _, _, _, v, s = _nbr_scores(qkv_ref, e2_ref, r, rows, ok)
            m = jnp.max(s, axis=-1, keepdims=True)
            p = jnp.exp(s - m)
            den = jnp.sum(p, axis=-1, keepdims=True)
            o = jnp.dot((p / den).astype(BF), v, preferred_element_type=F32)
            o_ref[pl.ds(q0, GRID_W), :] = o.astype(BF)
            lse_ref[pl.ds(q0, GRID_W), :] = jnp.broadcast_to(m + jnp.log(den), (GRID_W, HEAD_DIM))
            return carry

        lax.fori_loop(0, rows, step, 0)

    out = pl.BlockSpec((S, HEAD_DIM), lambda h: (0, h))
    return ORDER.call(
        body, [qkv3, e2],
        [pl.BlockSpec((3, S, HEAD_DIM), lambda h: (0, 0, N_HEADS_A + h)),
         pl.BlockSpec((None, RPB_ROWS - 1, GRID_W, 128), lambda h: (h, 0, 0, 0))],
        name="attn_b_fwd", grid=(4,),
        out_specs=[out, out], out_shape=[_sds((S, 512), BF), _sds((S, 512), F32)],
        compiler_params=_cparams(("parallel",)),
    )


def _attn_b_bwd(qkv3, e2, dy, y, lse, dqkv3):
    _, S, _ = qkv3.shape
    rows = S // GRID_W
    nk = NA_ROWS * GRID_W

    def body(qkv_ref, e2_ref, dy_ref, y_ref, lse_ref, _, out_ref, de2_ref, dk_acc, dv_acc):
        ok = _nbr_col_ok()
        dk_acc[...] = jnp.zeros_like(dk_acc)
        dv_acc[...] = jnp.zeros_like(dv_acc)
        de2_ref[...] = jnp.zeros_like(de2_ref)

        def step(r, carry):
            q0, k0, first, q, k, v, s = _nbr_scores(qkv_ref, e2_ref, r, rows, ok)
            qrows = pl.ds(q0, GRID_W)
            dyv = dy_ref[qrows, :]
            delta = jnp.sum(dyv.astype(F32) * y_ref[qrows, :].astype(F32), axis=-1, keepdims=True)
            p = jnp.where(ok, jnp.exp(s - jnp.tile(lse_ref[qrows, :], (1, nk // HEAD_DIM))), 0.0)
            dp = lax.dot_general(dyv, v, (NT, ((), ())), preferred_element_type=F32)
            ds = p * (dp - delta)
            for pair in range(NA_ROWS // 2):
                de2_ref[first + 2 * pair] += ds[:, pair * 128:(pair + 1) * 128]
            dsb = ds.astype(BF)
            out_ref[0, qrows, :] = (jnp.dot(dsb, k, preferred_element_type=F32) * SCALE).astype(BF)
            keys = pl.ds(k0, nk)
            dk_acc[keys, :] += lax.dot_general(dsb, q, (TN, ((), ())), preferred_element_type=F32) * SCALE
            dv_acc[keys, :] += lax.dot_general(p.astype(BF), dyv, (TN, ((), ())), preferred_element_type=F32)
            return carry

        lax.fori_loop(0, rows, step, 0)
        out_ref[1] = dk_acc[...].astype(BF)
        out_ref[2] = dv_acc[...].astype(BF)

    heads = pl.BlockSpec((3, S, HEAD_DIM), lambda h: (0, 0, N_HEADS_A + h))
    row = pl.BlockSpec((S, HEAD_DIM), lambda h: (0, h))
    table = pl.BlockSpec((None, RPB_ROWS - 1, GRID_W, 128), lambda h: (h, 0, 0, 0))
    return ORDER.call(
        body, [qkv3, e2, dy, y, lse, dqkv3],
        [heads, table, row, row, row, pl.BlockSpec(memory_space=pl.ANY)], name="attn_b_bwd", grid=(4,),
        out_specs=[heads, table],
        out_shape=[_sds((3, S, QKV_W), BF), _sds((4, RPB_ROWS - 1, GRID_W, 128), F32)],
        scratch_shapes=[pltpu.VMEM((S, HEAD_DIM), F32), pltpu.VMEM((S, HEAD_DIM), F32)],
        input_output_aliases={5: 0},
        compiler_params=_cparams(("parallel",)), chain_output=1,
    )


def _rpb_to_table(rpb):
    pad = jnp.pad(rpb, ((0, 0), (0, 0), (0, 1)))
    pairs = jnp.concatenate([pad[:, :-1], pad[:, 1:]], axis=-1).reshape(4 * (RPB_ROWS - 1), 64)
    onehot = jnp.asarray(_toeplitz_onehot())
    n = onehot.shape[1]
    tn = 2048
    full = lambda i, j, k: (0, 0)
    (e2,) = _matmul("rpb_table", pairs, onehot, pl.BlockSpec(pairs.shape, full),
                    pl.BlockSpec((64, tn), lambda i, j, k: (0, j)), NN, (1, n // tn, 1), (pairs.shape[0], tn), [],
                    [(_sds((pairs.shape[0], n), F32), pl.BlockSpec((pairs.shape[0], tn), lambda i, j, k: (0, j)))],
                    _store(F32), precision=lax.Precision.HIGHEST)
    return e2.reshape(4, RPB_ROWS - 1, GRID_W, 128)


def _table_grad_to_rpb(de2):
    onehot = jnp.asarray(_toeplitz_onehot())
    n = onehot.shape[1]
    flat = de2.reshape(4 * (RPB_ROWS - 1), n)
    tk = 2048
    (dpairs,) = _matmul("rpb_table_grad", flat, onehot, pl.BlockSpec((flat.shape[0], tk), lambda i, j, k: (0, k)),
                        pl.BlockSpec((64, tk), lambda i, j, k: (0, k)), NT, (1, 1, n // tk), (flat.shape[0], 64), [],
                        [(_sds((flat.shape[0], 64), F32), pl.BlockSpec((flat.shape[0], 64), lambda i, j, k: (0, 0)))],
                        _store(F32), precision=lax.Precision.HIGHEST)
    dpairs = dpairs.reshape(4, RPB_ROWS - 1, 64)
    zero = jnp.zeros((4, 1, RPB_COLS), F32)
    return (jnp.concatenate([dpairs[:, :, :RPB_COLS], zero], axis=1)
            + jnp.concatenate([zero, dpairs[:, :, 32:32 + RPB_COLS]], axis=1))


HBM = pl.BlockSpec(memory_space=pl.ANY)


def _place():
    x, y, c = lax.axis_index("x"), lax.axis_index("y"), lax.axis_index("c")
    chips = [(1 - x, y), (x, 1 - y), (1 - x, 1 - y)]
    return x, y, c, chips


def _remote(src, dst, send_sem, recv_sem, to):
    return pltpu.make_async_remote_copy(src_ref=src, dst_ref=dst, send_sem=send_sem, recv_sem=recv_sem,
                                        device_id=to, device_id_type=MESH)


def _place_shard(name, w, me):
    R, C = w.shape
    tr = _tile(R, 256)

    def body(me_ref, w_ref, o_ref):
        o_ref[...] = w_ref[...].astype(BF)

    return ORDER.call(
        body, [w], [pl.BlockSpec((tr, C), lambda i, mr: (i, 0))], prefetch=(me,), name=name, grid=(R // tr,),
        out_specs=pl.BlockSpec((None, tr, C), lambda i, mr: (mr[0], i, 0)),
        out_shape=_sds((N_CHIPS, R, C), BF), compiler_params=_cparams(("parallel",)),
    )


SEM = pl.BlockSpec(memory_space=pltpu.SEMAPHORE)
IN_HBM = pl.BlockSpec(memory_space=pltpu.HBM)
DATAFLOW = pltpu.SideEffectType.DATAFLOW_SIDE_EFFECTING


def _in_hbm(a):
    return pltpu.with_memory_space_constraint(a, pltpu.HBM)


def _copy_start(name, bufs, copies, n_copies, earlier=None):
    n = len(bufs)
    after = None if any(b is ORDER.last for b in bufs) else ORDER.last
    n_extra = (2 if earlier is not None else 0) + (1 if after is not None else 0)

    def body(*refs):
        ins = refs[:n]
        if earlier is not None:
            for k, (src, dst, to) in enumerate(earlier[0](ins)):
                cp = _remote(src, dst, refs[n].at[k], refs[n + 1].at[k], to)
                cp.wait_send()
                cp.wait_recv()
        send_sems, recv_sems = refs[n + n_extra], refs[n + n_extra + 1]
        for k, (src, dst, to) in enumerate(copies(ins)):
            _remote(src, dst, send_sems.at[k], recv_sems.at[k], to).start()
        refs[-1][...] = jnp.zeros((8, 128), F32)

    operands = [_in_hbm(b) for b in bufs]
    in_specs = [IN_HBM] * n
    if earlier is not None:
        operands += [earlier[1], earlier[2]]
        in_specs += [SEM, SEM]
    if after is not None:
        operands.append(after)
        in_specs.append(HBM)
    outs = pl.pallas_call(
        body, name=name,
        out_shape=(pltpu.SemaphoreType.DMA((n_copies,)), pltpu.SemaphoreType.DMA((n_copies,)),
                   *[pltpu.HBM(b.shape, b.dtype) for b in bufs], _sds((8, 128), F32)),
        in_specs=in_specs,
        out_specs=(SEM, SEM, *[IN_HBM] * n, pl.BlockSpec(memory_space=pltpu.VMEM)),
        input_output_aliases={i: 2 + i for i in range(n)},
        compiler_params=pltpu.CompilerParams(has_side_effects=DATAFLOW),
    )(*operands)
    ORDER.last = outs[-1]
    return outs[0], outs[1], list(outs[2:2 + n])


def _copy_wait(name, bufs, copies, send_sems, recv_sems):
    n = len(bufs)
    after = ORDER.last

    def body(*refs):
        ins = refs[:n]
        for k, (src, dst, to) in enumerate(copies(ins)):
            cp = _remote(src, dst, refs[n].at[k], refs[n + 1].at[k], to)
            cp.wait_send()
            cp.wait_recv()

    outs = list(pl.pallas_call(
        body, name=name,
        out_shape=tuple(pltpu.HBM(b.shape, b.dtype) for b in bufs),
        in_specs=[IN_HBM] * n + [SEM, SEM, HBM], out_specs=tuple([IN_HBM] * n),
        input_output_aliases={i: i for i in range(n)},
        compiler_params=pltpu.CompilerParams(has_side_effects=DATAFLOW),
    )(*bufs, send_sems, recv_sems, after))
    ORDER.last = outs[0]
    return outs


def _gather_hop1(bufs):
    x, y, c, chips = _place()
    out = []
    for b in bufs:
        half = b.shape[1] // 2
        mine = b.at[2 * x + y, pl.ds(c * half, half), :]
        out += [(mine, mine, (*chip, c)) for chip in chips]
    return out


def _gather_hop2(bufs):
    x, y, c, chips = _place()
    out = []
    for b in bufs:
        half = b.shape[1] // 2
        for chip in chips:
            landed = b.at[2 * chip[0] + chip[1], pl.ds(c * half, half), :]
            out.append((landed, landed, (x, y, 1 - c)))
    return out


def _swap_copies(bufs):
    x, y, c, _ = _place()
    n = len(bufs) // 2
    out = []
    for p, land in zip(bufs[:n], bufs[n:]):
        half = p.shape[1] // 2
        out.append((p.at[:, pl.ds((1 - c) * half, half), :], land, (x, y, 1 - c)))
    return out


def _scatter_copies(bufs):
    _, _, c, chips = _place()
    n = len(bufs) // 2
    out = []
    for s_, land in zip(bufs[:n], bufs[n:]):
        out += [(s_.at[2 * chip[0] + chip[1]], land.at[j], (*chip, c)) for j, chip in enumerate(chips)]
    return out


def _join_copies(bufs):
    x, y, c, _ = _place()
    out = []
    for b in bufs:
        half = b.shape[0] // 2
        mine = b.at[pl.ds(c * half, half), :]
        out.append((mine, mine, (x, y, 1 - c)))
    return out


def _gather_small(vec):
    m_per, n = vec.shape

    def body(x_ref, out_ref, send_sems, recv_sems, local_sem):
        x, y, c, chips = _place()
        me, sibling = (x, y, c), (x, y, 1 - c)

        def rows(px, py, pc):
            return out_ref.at[pl.ds((4 * px + 2 * py + pc) * m_per, m_per), :]

        def copy(k, block, to, src=None):
            return _remote(rows(*block) if src is None else src, rows(*block), send_sems.at[k], recv_sems.at[k], to)

        mine = pltpu.make_async_copy(x_ref, rows(*me), local_sem)
        mine.start()
        first = [copy(0, me, sibling, src=x_ref)]
        first += [copy(1 + j, me, (*chip, c), src=x_ref) for j, chip in enumerate(chips)]
        for cp in first:
            cp.start()
        passed = [copy(4 + j, (*chip, c), sibling) for j, chip in enumerate(chips)]
        for j, chip in enumerate(chips):
            copy(1 + j, (*chip, c), me).wait_recv()
            passed[j].start()
        copy(0, sibling, me).wait_recv()
        for j, chip in enumerate(chips):
            copy(4 + j, (*chip, 1 - c), me).wait_recv()
        for cp in first + passed:
            cp.wait_send()
        mine.wait()

    return ORDER.call(
        body, [vec], [pl.BlockSpec(memory_space=pltpu.VMEM)], name="gather_small_grads",
        out_shape=_sds((8 * m_per, n), vec.dtype), out_specs=pl.BlockSpec(memory_space=pltpu.VMEM),
        scratch_shapes=[pltpu.SemaphoreType.DMA((7,)), pltpu.SemaphoreType.DMA((7,)), pltpu.SemaphoreType.DMA],
    )


def _add_sibling(name, partial, received, c):
    _, R, C = partial.shape
    half = R // 2
    tr = _tile(half, 256)
    nb = half // tr

    def body(c_ref, p_ref, r_ref, o_ref):
        o_ref[...] = (p_ref[...].astype(F32) + r_ref[...].astype(F32)).astype(BF)

    return ORDER.call(
        body, [partial, received],
        [pl.BlockSpec((None, tr, C), lambda j, i, cr: (j, cr[0] * nb + i, 0)),
         pl.BlockSpec((None, tr, C), lambda j, i, cr: (j, i, 0))],
        prefetch=(c,), name=name, grid=(N_CHIPS, nb),
        out_specs=pl.BlockSpec((None, tr, C), lambda j, i, cr: (j, i, 0)),
        out_shape=_sds((N_CHIPS, half, C), BF), compiler_params=_cparams(("parallel", "parallel")),
    )


def _add_chips(name, sums, received, me_c):
    _, half, C = sums.shape
    tr = _tile(half, 256)
    nb = half // tr

    def body(mc_ref, s_ref, r_ref, o_ref):
        acc = s_ref[...].astype(F32)
        for j in range(3):
            acc = acc + r_ref[j].astype(F32)
        o_ref[...] = acc

    return ORDER.call(
        body, [sums, received],
        [pl.BlockSpec((None, tr, C), lambda i, mc: (mc[0], i, 0)),
         pl.BlockSpec((3, tr, C), lambda i, mc: (0, i, 0))],
        prefetch=(me_c,), name=name, grid=(nb,),
        out_specs=pl.BlockSpec((tr, C), lambda i, mc: (mc[1] * nb + i, 0)),
        out_shape=_sds((2 * half, C), F32), compiler_params=_cparams(("parallel",)),
    )


def _adamw_math(w, g, m, v):
    m = ADAM_B1 * m + (1.0 - ADAM_B1) * g
    v = ADAM_B2 * v + (1.0 - ADAM_B2) * (g * g)
    m_hat = m / (1.0 - ADAM_B1 ** ADAM_STEP)
    v_hat = v / (1.0 - ADAM_B2 ** ADAM_STEP)
    delta = -ADAM_LR * (m_hat / (jnp.sqrt(v_hat) + ADAM_EPS) + ADAM_WD * w)
    return delta, m, v


def _adamw(name, w, g, m, v):
    R, C = w.shape
    tr = _tile(R, 128)

    def body(w_ref, g_ref, m_ref, v_ref, go_ref, d_ref, mo_ref, vo_ref):
        gv = g_ref[...]
        go_ref[...] = gv
        d_ref[...], mo_ref[...], vo_ref[...] = _adamw_math(w_ref[...], gv, m_ref[...], v_ref[...])

    row = pl.BlockSpec((tr, C), lambda i: (i, 0))
    return ORDER.call(
        body, [w, g, m, v], [row] * 4, name=name, grid=(R // tr,), out_specs=[row] * 4,
        out_shape=[_sds((R, C), F32)] * 4, compiler_params=_cparams(("parallel",)), chain_output=1,
    )


def _adamw_small(gathered, w, m, v):
    rows, n = w.shape

    def body(ga_ref, w_ref, m_ref, v_ref, go_ref, d_ref, mo_ref, vo_ref):
        g = ga_ref[pl.ds(0, rows), :]
        for dev in range(1, 8):
            g = g + ga_ref[pl.ds(dev * rows, rows), :]
        go_ref[...] = g
        d_ref[...], mo_ref[...], vo_ref[...] = _adamw_math(w_ref[...], g, m_ref[...], v_ref[...])

    whole = pl.BlockSpec(memory_space=pltpu.VMEM)
    return ORDER.call(
        body, [gathered, w, m, v], [whole] * 4, name="adamw_small", out_specs=[whole] * 4,
        out_shape=[_sds((rows, n), F32)] * 4, compiler_params=_cparams(), chain_output=1,
    )


class _Exchange:
    GATHER = (("qkv",), ("gate", "proj_a", "proj_b", "out"), ("up", "down"))
    REDUCE = {"mlp": ("down", "up"), "mix": ("out", "proj_a", "proj_b"), "in": ("qkv", "gate")}

    def __init__(self, shards, me, c):
        self.me, self.c = me, c
        self.hop1, self.hop2, self.stage, self.grads = {}, {}, {}, {}
        for g, names in enumerate(self.GATHER):
            bufs = [_place_shard(f"place_{n}", shards[n], me) for n in names]
            self.hop1[g] = _copy_start(f"gather{g}_start", bufs, _gather_hop1, 3 * len(names))

    def forward(self, g):
        send, recv, thru = self.hop1.pop(g)
        self.hop2[g] = _copy_start(f"gather{g}_forward", thru, _gather_hop2, len(thru) * 3,
                                   earlier=(_gather_hop1, send, recv))

    def weights(self, g):
        send, recv, thru = self.hop2.pop(g)
        return _copy_wait(f"gather{g}_wait", thru, _gather_hop2, send, recv)

    def reduce(self, key, partials=None):
        names = self.REDUCE[key]
        n = len(names)
        if partials is not None:
            lands = [lax.empty((p.shape[0], p.shape[1] // 2, p.shape[2]), p.dtype) for p in partials]
            self.stage[key] = ("swap",) + _copy_start(f"reduce_{key}_swap", list(partials) + lands, _swap_copies, n)
            return
        kind, send, recv, thru = self.stage.pop(key)
        if kind == "swap":
            thru = _copy_wait(f"reduce_{key}_swap_wait", thru, _swap_copies, send, recv)
            sums = [_add_sibling(f"reduce_{nm}_add_sibling", p, r, self.c)
                    for nm, p, r in zip(names, thru[:n], thru[n:])]
            lands = [lax.empty((3,) + s_.shape[1:], s_.dtype) for s_ in sums]
            self.stage[key] = ("scatter",) + _copy_start(f"reduce_{key}_scatter", sums + lands, _scatter_copies, 3 * n)
        elif kind == "scatter":
            thru = _copy_wait(f"reduce_{key}_scatter_wait", thru, _scatter_copies, send, recv)
            me_c = jnp.concatenate([self.me, self.c])
            halves = [_add_chips(f"reduce_{nm}_add_chips", s_, r, me_c)
                      for nm, s_, r in zip(names, thru[:n], thru[n:])]
            self.stage[key] = ("join",) + _copy_start(f"reduce_{key}_join", halves, _join_copies, n)
        else:
            thru = _copy_wait(f"reduce_{key}_join_wait", thru, _join_copies, send, recv)
            self.grads.update(zip(names, thru))


def _forward_backward(x, target, norm_mix, b_gate, rpb, norm_mlp, norm_final, ex):
    S, D = x.shape

    h1 = _rms_fwd("rms_mix", x, norm_mix)
    ex.forward(0)
    e2 = _rpb_to_table(rpb)
    (gq,) = ex.weights(0)
    nq = QKV_W // 512
    (qkv3,), _ = _mm_nn_cols(
        "qkv", h1, gq, BF, tn=512,
        outs=[(_sds((3, S, QKV_W), BF), pl.BlockSpec((None, _tile(S, 1024), 512), lambda i, j, k: (j // nq, i, j % nq)))])

    ex.forward(1)
    outs_a = [_attn_a_fwd(qkv3, 0, DILATIONS[0])]
    gg, gpa, gpb, gout = ex.weights(1)
    wout = gout.reshape(D, D)

    tg = _tile(gg.shape[2], 1024)
    ng = D // tg

    def gate_epilogue(acc, ex_, outs):
        outs[0][...] = jax.nn.sigmoid(acc + ex_[0][...])

    (g3,), _ = _mm_nn_cols(
        "gate", h1, gg, F32, epilogue=gate_epilogue, tn=tg,
        extras=[(b_gate, pl.BlockSpec((1, tg), lambda i, j, k: (0, j)))],
        outs=[(_sds((2, S, D), F32), pl.BlockSpec((None, _tile(S, 1024), tg), lambda i, j, k: (j // ng, i, j % ng)))])

    outs_a += [_attn_a_fwd(qkv3, grp, d) for grp, d in enumerate(DILATIONS) if grp > 0]
    y_a, lj = _attn_a_combine([o for o, _ in outs_a], [l for _, l in outs_a])
    y_b, lse_b = _attn_b_fwd(qkv3, e2)

    (pa,), (tm, tp, _) = _mm_nn_cols("proj_a", y_a, gpa, F32, tn=512)

    def merge_epilogue(acc, ex_, outs):
        g = ex_[0][...]
        outs[0][...] = acc
        outs[1][...] = (g[0] * ex_[1][...] + g[1] * acc).astype(BF)

    tile = pl.BlockSpec((tm, tp), lambda i, j, k: (i, j))
    gates = pl.BlockSpec((2, tm, tp), lambda i, j, k: (0, i, j))
    (pb, merged), _ = _mm_nn_cols(
        "proj_b_merge", y_b, gpb, F32, epilogue=merge_epilogue, tn=512,
        extras=[(g3, gates), (pa, tile)],
        outs=[(_sds((S, D), F32), tile), (_sds((S, D), BF), tile)])

    def residual_epilogue(acc, ex_, outs):
        outs[0][...] = acc + ex_[0][...]

    def nn_plain(name, a, w, res):
        M, K = a.shape
        N = w.shape[1]
        bm, bn, bk = _tile(M, 1024), _tile(N, 1024), _tile(K, 2048)
        t = pl.BlockSpec((bm, bn), lambda i, j, k: (i, j))
        return _matmul(name, a, w, pl.BlockSpec((bm, bk), lambda i, j, k: (i, k)),
                       pl.BlockSpec((bk, bn), lambda i, j, k: (k, j)), NN, (M // bm, N // bn, K // bk), (bm, bn),
                       [(res, t)], [(_sds((M, N), F32), t)], residual_epilogue)[0]

    ex.forward(2)
    x1 = nn_plain("out_proj", merged, wout, x)
    h2 = _rms_fwd("rms_mlp", x1, norm_mlp)
    gup, gdown = ex.weights(2)
    F = gup.shape[2] * N_CHIPS
    wdown = gdown.reshape(F, D)

    def up_epilogue(acc, ex_, outs):
        ru = jnp.maximum(acc, 0.0)
        outs[0][...] = (ru * ru).astype(BF)
        outs[1][...] = ru.astype(BF)

    tu = _tile(gup.shape[2], 1024)
    ut = pl.BlockSpec((_tile(S, 1024), tu), lambda i, j, k: (i, j))
    (act, ru), _ = _mm_nn_cols("mlp_up", h2, gup, BF, epilogue=up_epilogue, tn=tu,
                               outs=[(_sds((S, F), BF), ut), (_sds((S, F), BF), ut)])
    x2 = nn_plain("mlp_down", act, wdown, x1)

    loss, dx2, dx2b, d_norm_final = _loss_head(x2, target, norm_final.reshape(1, D))

    def nt_rows(name, a, w, epilogue, extras, outs, bn=1024):
        M, N = a.shape
        K = w.shape[0]
        bm, bn, bk = _tile(M, 1024), _tile(K, bn), _tile(N, 2048)
        return _matmul(name, a, w, pl.BlockSpec((bm, bk), lambda i, j, k: (i, k)),
                       pl.BlockSpec((bn, bk), lambda i, j, k: (j, k)), NT, (M // bm, K // bn, N // bk), (bm, bn),
                       extras(bm, bn), outs(bm, bn), epilogue)

    def nt_cols(name, a_spec_fn, a, g, M, epilogue, extras, outs, bk):
        _, K, Nq = g.shape
        bm, bn, bk = _tile(M, 1024), _tile(K, 1024), _tile(Nq, bk)
        q = Nq // bk
        return _matmul(name, a, g, a_spec_fn(bm, bk), pl.BlockSpec((None, bn, bk), lambda i, j, k: (k // q, j, k % q)),
                       NT, (M // bm, K // bn, N_CHIPS * q), (bm, bn), extras(bm, bn), outs(bm, bn), epilogue)

    def tn_grad(name, a, a_spec_fn, b, b_spec_fn, Kin, N, out_shape, out_spec_fn, bn=1024):
        bm, bn, bk = _tile(Kin, 1024), _tile(N, bn), _tile(S, 2048)
        return _matmul(name, a, b, a_spec_fn(bk, bm), b_spec_fn(bk, bn), TN, (Kin // bm, N // bn, S // bk), (bm, bn),
                       [], [(_sds(out_shape, BF), out_spec_fn(bm, bn))], _store(BF))[0]

    plain_a = lambda bk, bm: pl.BlockSpec((bk, bm), lambda i, j, k: (k, i))
    plain_b = lambda bk, bn: pl.BlockSpec((bk, bn), lambda i, j, k: (k, j))
    plain_o = lambda bm, bn: pl.BlockSpec((bm, bn), lambda i, j, k: (i, j))
    a_rows = lambda bm, bk: pl.BlockSpec((bm, bk), lambda i, j, k: (i, k))

    def cols_o(Nq):
        def spec(bm, bn):
            q = Nq // bn
            return pl.BlockSpec((None, bm, bn), lambda i, j, k: (j // q, i, j % q))
        return spec

    def du_epilogue(acc, ex_, outs):
        outs[0][...] = (acc * (2.0 * ex_[0][...].astype(F32))).astype(BF)

    dw_down = tn_grad("mlp_down_dw", act, plain_a, dx2b, plain_b, F, D, (F, D), plain_o)
    (du,) = nt_rows("mlp_down_dx", dx2b, wdown, du_epilogue,
                    lambda bm, bn: [(ru, plain_o(bm, bn))], lambda bm, bn: [(_sds((S, F), BF), plain_o(bm, bn))])

    fq = gup.shape[2]
    dw_up = tn_grad("mlp_up_dw", h2, plain_a, du, plain_b, D, F, (N_CHIPS, D, fq), cols_o(fq), bn=min(fq, 1024))
    ex.reduce("mlp", partials=[dw_down.reshape(N_CHIPS, F // N_CHIPS, D), dw_up])
    (dh2,) = nt_cols("mlp_up_dx", a_rows, du, gup, S, _store(F32), lambda bm, bn: [],
                     lambda bm, bn: [(_sds((S, D), F32), plain_o(bm, bn))], 2048)
    ex.reduce("mlp")
    dx1, dx1b, d_norm_mlp = _rms_bwd("rms_mlp_bwd", dh2, x1, norm_mlp, dx2)

    def merge_bwd_epilogue(acc, ex_, outs):
        g, pav, pbv = ex_[0][...], ex_[1][...], ex_[2][...]
        outs[0][...] = (acc * g[0]).astype(BF)
        outs[1][...] = (acc * g[1]).astype(BF)
        dga = acc * pav * g[0] * (1.0 - g[0])
        dgb = acc * pbv * g[1] * (1.0 - g[1])
        outs[2][0] = dga.astype(BF)
        outs[2][1] = dgb.astype(BF)
        outs[3][...] = jnp.concatenate([jnp.sum(dga, axis=0, keepdims=True), jnp.sum(dgb, axis=0, keepdims=True)], 0)

    def pair(bm, bn):
        return pl.BlockSpec((2, bm, bn), lambda i, j, k: (0, i, j))

    n_row_blocks = S // _tile(S, 1024)
    dpa, dpb, dg3, db_gate = nt_rows(
        "out_proj_dx", dx1b, wout, merge_bwd_epilogue,
        lambda bm, bn: [(g3, pair(bm, bn)), (pa, plain_o(bm, bn)), (pb, plain_o(bm, bn))],
        lambda bm, bn: [(_sds((S, D), BF), plain_o(bm, bn)), (_sds((S, D), BF), plain_o(bm, bn)),
                        (_sds((2, S, D), BF), pair(bm, bn)),
                        (_sds((n_row_blocks, 2, D), F32), pl.BlockSpec((None, 2, bn), lambda i, j, k: (i, 0, j)))],
        bn=512)
    dw_out = tn_grad("out_proj_dw", merged, plain_a, dx1b, plain_b, D, D, (D, D), plain_o)

    pq = gpa.shape[2]
    proj_dx = lambda name, dproj, g: nt_cols(name, a_rows, dproj, g, S, _store(BF), lambda bm, bn: [],
                                             lambda bm, bn: [(_sds((S, 512), BF), plain_o(bm, bn))], 512)[0]
    dw_pa = tn_grad("proj_a_dw", y_a, plain_a, dpa, plain_b, 512, D, (N_CHIPS, 512, pq), cols_o(pq), bn=min(pq, 512))
    dw_pb = tn_grad("proj_b_dw", y_b, plain_a, dpb, plain_b, 512, D, (N_CHIPS, 512, pq), cols_o(pq), bn=min(pq, 512))
    ex.reduce("mix", partials=[dw_out.reshape(N_CHIPS, D // N_CHIPS, D), dw_pa, dw_pb])
    dy_a = proj_dx("proj_a_dx", dpa, gpa)
    dy_b = proj_dx("proj_b_dx", dpb, gpb)

    dqkv3 = lax.empty((3, S, QKV_W), BF)
    dqkv3 = _attn_a_bwd(qkv3, dy_a, y_a, lj, dqkv3, 0, DILATIONS[0])
    ex.reduce("mix")
    for grp, d in enumerate(DILATIONS):
        if grp > 0:
            dqkv3 = _attn_a_bwd(qkv3, dy_a, y_a, lj, dqkv3, grp, d)
    dqkv3, de2 = _attn_b_bwd(qkv3, e2, dy_b, y_b, lse_b, dqkv3)
    d_rpb = _table_grad_to_rpb(de2)

    def stacked_a(width):
        def spec(bm, bk):
            q = width // bk
            return pl.BlockSpec((None, bm, bk), lambda i, j, k: (k // q, i, k % q))
        return spec

    def stacked_b(width):
        def spec(bk, bn):
            q = width // bn
            return pl.BlockSpec((None, bk, bn), lambda i, j, k: (j // q, k, j % q))
        return spec

    dw_qkv = tn_grad("qkv_dw", h1, plain_a, dqkv3, stacked_b(QKV_W), D, 3 * QKV_W, (N_CHIPS,) + gq.shape[1:],
                     cols_o(gq.shape[2]), bn=512)
    dw_gate = tn_grad("gate_dw", h1, plain_a, dg3, stacked_b(D), D, 2 * D, (N_CHIPS,) + gg.shape[1:],
                      cols_o(gg.shape[2]), bn=gg.shape[2])
    ex.reduce("in", partials=[dw_qkv, dw_gate])
    ex.reduce("mlp")
    (dh1_q,) = nt_cols("qkv_dx", stacked_a(QKV_W), dqkv3, gq, S, _store(F32), lambda bm, bn: [],
                       lambda bm, bn: [(_sds((S, D), F32), plain_o(bm, bn))], 512)
    ex.reduce("in")
    ex.reduce("mix")

    def add_epilogue(acc, ex_, outs):
        outs[0][...] = acc + ex_[0][...]

    (dh1,) = nt_cols("gate_dx", stacked_a(D), dg3, gg, S, add_epilogue, lambda bm, bn: [(dh1_q, plain_o(bm, bn))],
                     lambda bm, bn: [(_sds((S, D), F32), plain_o(bm, bn))], gg.shape[2])
    grad_x, _, d_norm_mix = _rms_bwd("rms_mix_bwd", dh1, x, norm_mix, dx1)
    ex.reduce("mlp")
    ex.reduce("mix")

    small = [d_norm_mix, jnp.sum(db_gate, axis=0).reshape(1, 2 * D), d_rpb, d_norm_mlp, d_norm_final]
    return loss, grad_x, small


def _pack_small(parts, width):
    flat = jnp.concatenate([p.reshape(-1) for p in parts])
    return jnp.pad(flat, (0, 8 * width - flat.shape[0])).reshape(8, width)


def kernel(x, norm_mix, w_qkv, w_gate, b_gate, rpb, w_proj_a, w_proj_b, w_out, norm_mlp, w_up, w_down, norm_final, loss_target, m_norm_mix, m_w_qkv, m_w_gate, m_b_gate, m_rpb, m_w_proj_a, m_w_proj_b, m_w_out, m_norm_mlp, m_w_up, m_w_down, m_norm_final, v_norm_mix, v_w_qkv, v_w_gate, v_b_gate, v_rpb, v_w_proj_a, v_w_proj_b, v_w_out, v_norm_mlp, v_w_up, v_w_down, v_norm_final):
    names = ["qkv", "gate", "proj_a", "proj_b", "out", "up", "down"]
    big = dict(zip(names, [w_qkv[0], w_gate[0], w_proj_a[0], w_proj_b[0], w_out[0], w_up[0], w_down[0]]))
    big_m = dict(zip(names, [m_w_qkv[0], m_w_gate[0], m_w_proj_a[0], m_w_proj_b[0], m_w_out[0], m_w_up[0], m_w_down[0]]))
    big_v = dict(zip(names, [v_w_qkv[0], v_w_gate[0], v_w_proj_a[0], v_w_proj_b[0], v_w_out[0], v_w_up[0], v_w_down[0]]))

    c = lax.axis_index("c").astype(jnp.int32).reshape(1)
    me = (2 * lax.axis_index("x") + lax.axis_index("y")).astype(jnp.int32).reshape(1)
    ORDER.last = None
    ex = _Exchange(big, me, c)
    loss, grad_x, small = _forward_backward(x[0], loss_target[0], norm_mix, b_gate, rpb[0], norm_mlp, norm_final, ex)

    def adamw(group):
        return {n: _adamw(f"adamw_{n}", big[n], ex.grads[n], big_m[n], big_v[n]) for n in _Exchange.REDUCE[group]}

    big_out = {**adamw("mlp"), **adamw("mix")}
    ex.reduce("in")

    small_w = [norm_mix, b_gate, rpb, norm_mlp, norm_final]
    count = sum(int(np.prod(p.shape)) for p in small_w)
    width = -(-count // (8 * 128)) * 128
    packed = _adamw_small(_gather_small(_pack_small(small, width)), _pack_small(small_w, width),
                          _pack_small([m_norm_mix, m_b_gate, m_rpb, m_norm_mlp, m_norm_final], width),
                          _pack_small([v_norm_mix, v_b_gate, v_rpb, v_norm_mlp, v_norm_final], width))
    ex.reduce("in")
    big_out.update(adamw("in"))

    def unpack(flat2d):
        flat, out, at = flat2d.reshape(-1), [], 0
        for p in small_w:
            size = int(np.prod(p.shape))
            out.append(flat[at:at + size].reshape(p.shape))
            at += size
        return out

    small_out = [unpack(a) for a in packed]

    def ordered(kind):
        sm = small_out[kind]
        bg = {n: o[kind][None] for n, o in big_out.items()}
        return [sm[0], bg["qkv"], bg["gate"], sm[1], sm[2], bg["proj_a"], bg["proj_b"], bg["out"], sm[3],
                bg["up"], bg["down"], sm[4]]

    total = lax.psum(loss[0, 0], ("x", "y", "c"))
    return (total, grad_x[None], *ordered(0), *ordered(1), *ordered(2), *ordered(3))
```

```python
import functools
import math

import numpy as np
import jax
import jax.numpy as jnp
from jax import lax
from jax.experimental import pallas as pl
from jax.experimental.pallas import tpu as pltpu

BF = jnp.bfloat16
F32 = jnp.float32
MESH = pl.DeviceIdType.MESH

HEAD_DIM = 128
N_HEADS = 16
N_HEADS_A = 12
QKV_W = N_HEADS * HEAD_DIM
DILATIONS = (1, 4, 16)
HALF_WINDOW = 64
GRID_W = 64
NA_ROWS = 8
NA_COLS = 16
RPB_ROWS = 2 * NA_ROWS - 1
RPB_COLS = 2 * NA_COLS - 1
EPS = 1e-6
NEG = -1e30
SCALE = HEAD_DIM ** -0.5

ADAM_LR = 0.001
ADAM_B1 = 0.9
ADAM_B2 = 0.999
ADAM_EPS = 1e-08
ADAM_WD = 0.01
ADAM_STEP = 10

N_CHIPS = 4
VMEM_LIMIT_BYTES = 48 * 1024 * 1024
QB = 256
NBR_SIDE = 4


def _key_rows(L):
    return min(QB + 2 * HALF_WINDOW, L)


def _cparams(sem=None):
    return pltpu.CompilerParams(dimension_semantics=sem, vmem_limit_bytes=VMEM_LIMIT_BYTES)


def _tile(dim, want):
    t = min(dim, want)
    assert dim % t == 0, (dim, want)
    return t


class _ProgramOrder:
    def __init__(self):
        self.last = None

    def call(self, body, operands, in_specs, *, prefetch=(), grid=None, out_specs=None, chain_output=0, **kwargs):
        operands, in_specs = list(operands), list(in_specs)
        lead = len(prefetch) + len(operands)
        if self.last is not None and not any(op is self.last for op in operands):
            operands.append(self.last)
            in_specs.append(pl.BlockSpec(memory_space=pl.ANY))
            inner = body

            def body(*refs):
                return inner(*refs[:lead], *refs[lead + 1:])

        if prefetch:
            kwargs["grid_spec"] = pltpu.PrefetchScalarGridSpec(
                num_scalar_prefetch=len(prefetch), grid=grid, in_specs=in_specs, out_specs=out_specs)
        else:
            kwargs.update(in_specs=in_specs, out_specs=out_specs)
            if grid is not None:
                kwargs["grid"] = grid
        out = pl.pallas_call(body, **kwargs)(*prefetch, *operands)
        self.last = out[chain_output] if isinstance(out, (tuple, list)) else out
        return out


ORDER = _ProgramOrder()


NN = ((1,), (0,))
NT = ((1,), (1,))
TN = ((0,), (0,))


def _matmul(name, a, b, a_spec, b_spec, dims, grid, acc_shape, extras, outs, epilogue, precision=None):
    n_ex, n_out, nk = len(extras), len(outs), grid[2]

    def body(*refs):
        a_ref, b_ref = refs[0], refs[1]
        ex_refs = refs[2:2 + n_ex]
        out_refs = refs[2 + n_ex:2 + n_ex + n_out]

        def dot():
            return lax.dot_general(a_ref[...], b_ref[...], (dims, ((), ())),
                                   preferred_element_type=F32, precision=precision)

        if nk == 1:
            epilogue(dot(), ex_refs, out_refs)
            return
        acc_ref = refs[-1]
        k = pl.program_id(2)

        @pl.when(k == 0)
        def _():
            acc_ref[...] = dot()

        if nk > 2:
            @pl.when((k > 0) & (k < nk - 1))
            def _():
                acc_ref[...] += dot()

        @pl.when(k == nk - 1)
        def _():
            epilogue(acc_ref[...] + dot(), ex_refs, out_refs)

    return ORDER.call(
        body, [a, b] + [e for e, _ in extras], [a_spec, b_spec] + [s for _, s in extras], name=name, grid=grid,
        out_specs=[s for _, s in outs],
        out_shape=[sh for sh, _ in outs],
        scratch_shapes=[pltpu.VMEM(acc_shape, F32)] if nk > 1 else [],
        compiler_params=_cparams(("parallel", "parallel", "arbitrary")),
    )


def _store(dtype):
    def epilogue(acc, ex, outs):
        outs[0][...] = acc.astype(dtype)
    return epilogue


def _sds(shape, dtype):
    return jax.ShapeDtypeStruct(shape, dtype)


def _mm_nn_cols(name, a, g, out_dtype, epilogue=None, extras=(), outs=None, tm=1024, tn=1024, tk=2048):
    M, K = a.shape
    _, _, Nq = g.shape
    tm, tn, tk = _tile(M, tm), _tile(Nq, tn), _tile(K, tk)
    q = Nq // tn
    grid = (M // tm, N_CHIPS * q, K // tk)
    if outs is None:
        outs = [(_sds((M, N_CHIPS * Nq), out_dtype), pl.BlockSpec((tm, tn), lambda i, j, k: (i, j)))]
    return _matmul(name, a, g, pl.BlockSpec((tm, tk), lambda i, j, k: (i, k)),
                   pl.BlockSpec((None, tk, tn), lambda i, j, k: (j // q, k, j % q)), NN, grid, (tm, tn),
                   list(extras), outs, epilogue or _store(out_dtype)), (tm, tn, tk)


def _rms_fwd(name, x, g):
    S, D = x.shape
    tm = _tile(S, 256)

    def body(x_ref, g_ref, h_ref):
        xv = x_ref[...]
        r = lax.rsqrt(jnp.mean(xv * xv, axis=-1, keepdims=True) + EPS)
        h_ref[...] = ((xv * r) * g_ref[...]).astype(BF)

    row = pl.BlockSpec((tm, D), lambda i: (i, 0))
    return ORDER.call(
        body, [x, g], [row, pl.BlockSpec((1, D), lambda i: (0, 0))], name=name, grid=(S // tm,),
        out_specs=row, out_shape=_sds((S, D), BF), compiler_params=_cparams(("parallel",)),
    )


def _rms_bwd(name, dh, x, g, dres):
    S, D = x.shape
    tm = _tile(S, 256)

    def body(dh_ref, x_ref, g_ref, dres_ref, dx_ref, dxb_ref, dg_ref):
        xv = x_ref[...]
        r = lax.rsqrt(jnp.mean(xv * xv, axis=-1, keepdims=True) + EPS)
        n = xv * r
        dhv = dh_ref[...]
        dyg = dhv * g_ref[...]
        dx = dres_ref[...] + r * (dyg - n * jnp.mean(dyg * n, axis=-1, keepdims=True))
        dx_ref[...] = dx
        dxb_ref[...] = dx.astype(BF)

        @pl.when(pl.program_id(0) == 0)
        def _():
            dg_ref[...] = jnp.zeros_like(dg_ref)

        dg_ref[...] += jnp.sum(dhv * n, axis=0, keepdims=True)

    row = pl.BlockSpec((tm, D), lambda i: (i, 0))
    vec = pl.BlockSpec((1, D), lambda i: (0, 0))
    return ORDER.call(
        body, [dh, x, g, dres], [row, row, vec, row], name=name, grid=(S // tm,),
        out_specs=[row, row, vec],
        out_shape=[_sds((S, D), F32), _sds((S, D), BF), _sds((1, D), F32)],
        compiler_params=_cparams(("arbitrary",)),
    )


def _loss_head(x2, target, g):
    S, D = x2.shape
    tm = _tile(S, 256)

    def body(x_ref, t_ref, g_ref, loss_ref, dx_ref, dxb_ref, dg_ref):
        xv = x_ref[...]
        gv = g_ref[...]
        r = lax.rsqrt(jnp.mean(xv * xv, axis=-1, keepdims=True) + EPS)
        n = xv * r
        e = n * gv - t_ref[...]
        dy = e * (1.0 / D)
        dyg = dy * gv
        dx = r * (dyg - n * jnp.mean(dyg * n, axis=-1, keepdims=True))
        dx_ref[...] = dx
        dxb_ref[...] = dx.astype(BF)

        @pl.when(pl.program_id(0) == 0)
        def _():
            dg_ref[...] = jnp.zeros_like(dg_ref)
            loss_ref[...] = jnp.zeros_like(loss_ref)

        dg_ref[...] += jnp.sum(dy * n, axis=0, keepdims=True)
        per_row = jnp.mean(e * e, axis=-1, keepdims=True)
        loss_ref[...] += 0.5 * jnp.sum(per_row, axis=0, keepdims=True)

    row = pl.BlockSpec((tm, D), lambda i: (i, 0))
    vec = pl.BlockSpec((1, D), lambda i: (0, 0))
    return ORDER.call(
        body, [x2, target, g], [row, row, vec], name="loss_head", grid=(S // tm,),
        out_specs=[pl.BlockSpec((1, 1), lambda i: (0, 0)), row, row, vec],
        out_shape=[_sds((1, 1), F32), _sds((S, D), F32), _sds((S, D), BF), _sds((1, D), F32)],
        compiler_params=_cparams(("arbitrary",)), chain_output=1,
    )


def _band_scores(qkv_ref, i, L, coef):
    KB = _key_rows(L)
    q0 = pl.multiple_of(i * QB, QB)
    ks = pl.multiple_of(jnp.clip(i * QB - HALF_WINDOW, 0, L - KB), HALF_WINDOW)
    q = qkv_ref[0, pl.ds(q0, QB), :]
    k = qkv_ref[1, pl.ds(ks, KB), :]
    v = qkv_ref[2, pl.ds(ks, KB), :]
    s = lax.dot_general(q, k, (NT, ((), ())), preferred_element_type=F32) * SCALE
    qpos = q0 + lax.broadcasted_iota(jnp.int32, (QB, KB), 0)
    kpos = ks + lax.broadcasted_iota(jnp.int32, (QB, KB), 1)
    rel = jnp.abs(kpos - qpos)
    valid = rel <= HALF_WINDOW
    s = jnp.where(valid, s - coef * rel.astype(F32), NEG)
    return q0, ks, q, k, v, s, valid


def _alibi_coef(group, d):
    h = (4 * group + 1 + pl.program_id(1)).astype(F32)
    slope = jnp.exp(jnp.full((1, 1), -(8.0 / N_HEADS_A) * math.log(2.0), F32) * h)
    return slope * float(d)


def _dilated_view(qkv3, group, d):
    _, S, _ = qkv3.shape
    L = S // d
    if d == 1:
        return qkv3, pl.BlockSpec((3, L, HEAD_DIM), lambda r, j: (0, 0, 4 * group + j))
    cols = qkv3[:, :, 512 * group:512 * (group + 1)].reshape(3, L, d * 512)
    return cols, pl.BlockSpec((3, L, HEAD_DIM), lambda r, j: (0, 0, r * 4 + j))


def _attn_a_fwd(qkv3, group, d):
    _, S, _ = qkv3.shape
    L = S // d
    assert L % QB == 0
    view, heads = _dilated_view(qkv3, group, d)

    def body(qkv_ref, o_ref, lse_ref):
        coef = _alibi_coef(group, d)

        def step(i, carry):
            blocks = [_band_scores(qkv_ref, n_side * i + u, L, coef) for u in range(n_side)]
            soft = []
            for q0, _, _, _, v, s, _ in blocks:
                m = jnp.max(s, axis=-1, keepdims=True)
                p = jnp.exp(s - m)
                den = jnp.sum(p, axis=-1, keepdims=True)
                soft.append((q0, (p / den).astype(BF), v, m + jnp.log(den)))
            for q0, pn, v, lse in soft:
                o_ref[pl.ds(q0, QB), :] = jnp.dot(pn, v, preferred_element_type=F32)
                lse_ref[pl.ds(q0, QB), :] = jnp.broadcast_to(lse, (QB, HEAD_DIM))
            return carry

        n_side = min(2, L // QB)
        lax.fori_loop(0, L // QB // n_side, step, 0)

    out = pl.BlockSpec((L, HEAD_DIM), lambda r, j: (0, r * 4 + j))
    o, lse = ORDER.call(
        body, [view], [heads],
        name=f"attn_a_fwd_d{d}", grid=(d, 4),
        out_specs=[out, out],
        out_shape=[_sds((L, d * 512), F32), _sds((L, d * 512), F32)],
        compiler_params=_cparams(("parallel", "parallel")),
    )
    return o.reshape(S, 512), lse.reshape(S, 512)


def _attn_a_combine(os_, lses):
    S, W = os_[0].shape
    tm = _tile(S, 512)

    def body(o0, o1, o2, l0, l1, l2, y_ref, lj_ref):
        ls = [l0[...], l1[...], l2[...]]
        m = jnp.maximum(jnp.maximum(ls[0], ls[1]), ls[2])
        es = [jnp.exp(l - m) for l in ls]
        den = es[0] + es[1] + es[2]
        y = (es[0] / den) * o0[...] + (es[1] / den) * o1[...] + (es[2] / den) * o2[...]
        y_ref[...] = y.astype(BF)
        lj_ref[...] = m + jnp.log(den)

    row = pl.BlockSpec((tm, W), lambda i: (i, 0))
    return ORDER.call(
        body, [*os_, *lses], [row] * 6, name="attn_a_combine", grid=(S // tm,), out_specs=[row, row],
        out_shape=[_sds((S, W), BF), _sds((S, W), F32)], compiler_params=_cparams(("parallel",)),
    )


def _attn_a_bwd(qkv3, dy, y, lj, dqkv3, group, d):
    _, S, _ = qkv3.shape
    L = S // d
    view, heads = _dilated_view(qkv3, group, d)

    def body(qkv_ref, dy_ref, y_ref, lj_ref, *rest):
        out_ref, dk_acc, dv_acc = rest[-3:]
        coef = _alibi_coef(group, d)
        dk_acc[...] = jnp.zeros_like(dk_acc)
        dv_acc[...] = jnp.zeros_like(dv_acc)

        def step(i, carry):
            blocks = [_band_scores(qkv_ref, n_side * i + u, L, coef) for u in range(n_side)]
            dys = [dy_ref[pl.ds(b[0], QB), :] for b in blocks]
            dps = [lax.dot_general(dyv, b[4], (NT, ((), ())), preferred_element_type=F32) for dyv, b in zip(dys, blocks)]
            grads = []
            for (q0, ks, q, k, v, s, valid), dyv, dp in zip(blocks, dys, dps):
                rows = pl.ds(q0, QB)
                delta = jnp.sum(dyv.astype(F32) * y_ref[rows, :].astype(F32), axis=-1, keepdims=True)
                p = jnp.where(valid, jnp.exp(s - jnp.tile(lj_ref[rows, :], (1, _key_rows(L) // HEAD_DIM))), 0.0)
                grads.append(((p * (dp - delta)).astype(BF), p.astype(BF)))
            for (q0, ks, q, k, v, s, valid), dyv, (ds, pb) in zip(blocks, dys, grads):
                out_ref[0, pl.ds(q0, QB), :] = (jnp.dot(ds, k, preferred_element_type=F32) * SCALE).astype(BF)
                keys = pl.ds(ks, _key_rows(L))
                dk_acc[keys, :] += lax.dot_general(ds, q, (TN, ((), ())), preferred_element_type=F32) * SCALE
                dv_acc[keys, :] += lax.dot_general(pb, dyv, (TN, ((), ())), preferred_element_type=F32)
            return carry

        n_side = min(2, L // QB)
        lax.fori_loop(0, L // QB // n_side, step, 0)
        out_ref[1] = dk_acc[...].astype(BF)
        out_ref[2] = dv_acc[...].astype(BF)

    row = pl.BlockSpec((L, HEAD_DIM), lambda r, j: (0, r * 4 + j))
    operands = [view, dy.reshape(L, d * 512), y.reshape(L, d * 512), lj.reshape(L, d * 512)]
    scratch = [pltpu.VMEM((L, HEAD_DIM), F32), pltpu.VMEM((L, HEAD_DIM), F32)]
    if d == 1:
        return ORDER.call(
            body, operands + [dqkv3], [heads, row, row, row, pl.BlockSpec(memory_space=pl.ANY)],
            name=f"attn_a_bwd_d{d}", grid=(d, 4), out_specs=heads, out_shape=_sds((3, S, QKV_W), BF),
            scratch_shapes=scratch, input_output_aliases={4: 0}, compiler_params=_cparams(("parallel", "parallel")))
    out = ORDER.call(
        body, operands, [heads, row, row, row], name=f"attn_a_bwd_d{d}", grid=(d, 4),
        out_specs=heads, out_shape=_sds((3, L, d * 512), BF),
        scratch_shapes=scratch, compiler_params=_cparams(("parallel", "parallel")))
    return lax.dynamic_update_slice(dqkv3, out.reshape(3, S, 512), (0, 0, 512 * group))


def _toeplitz_onehot():
    oh = np.zeros((64, GRID_W, 128), np.float32)
    for qc in range(GRID_W):
        for m in range(128):
            kc = m % GRID_W
            dc = int(np.clip(kc - qc, -(NA_COLS - 1), NA_COLS - 1)) + NA_COLS - 1
            oh[(m // GRID_W) * 32 + dc, qc, m] = 1.0
    return oh.reshape(64, GRID_W * 128)


def _nbr_scores(qkv_ref, e2_ref, r, rows, ok):
    rs = jnp.clip(r - NA_ROWS // 2, 0, rows - NA_ROWS)
    q0 = pl.multiple_of(r * GRID_W, GRID_W)
    k0 = pl.multiple_of(rs * GRID_W, GRID_W)
    q = qkv_ref[0, pl.ds(q0, GRID_W), :]
    k = qkv_ref[1, pl.ds(k0, NA_ROWS * GRID_W), :]
    v = qkv_ref[2, pl.ds(k0, NA_ROWS * GRID_W), :]
    s = lax.dot_general(q, k, (NT, ((), ())), preferred_element_type=F32) * SCALE
    first = rs - r + NA_ROWS - 1
    bias = jnp.concatenate([e2_ref[first + 2 * pair] for pair in range(NA_ROWS // 2)], axis=1)
    s = jnp.where(ok, s + bias, NEG)
    return q0, k0, first, q, k, v, s


def _nbr_col_ok():
    qc = lax.broadcasted_iota(jnp.int32, (GRID_W, NA_ROWS * GRID_W), 0)
    kc = lax.broadcasted_iota(jnp.int32, (GRID_W, NA_ROWS * GRID_W), 1) % GRID_W
    cs = jnp.clip(qc - NA_COLS // 2, 0, GRID_W - NA_COLS)
    return (kc >= cs) & (kc < cs + NA_COLS)


def _attn_b_fwd(qkv3, e2):
    _, S, _ = qkv3.shape
    rows = S // GRID_W
    assert rows >= NA_ROWS

    def body(qkv_ref, e2_ref, o_ref, lse_ref):
        ok = _nbr_col_ok()

        def step(i, carry):
            blocks = [_nbr_scores(qkv_ref, e2_ref, NBR_SIDE * i + u, rows, ok) for u in range(NBR_SIDE)]
            soft = []
            for q0, _, _, _, _, v, s in blocks:
                m = jnp.max(s, axis=-1, keepdims=True)
                p = jnp.exp(s - m)
                den = jnp.sum(p, axis=-1, keepdims=True)
                soft.append((q0, (p / den).astype(BF), v, m + jnp.log(den)))
            for q0, pn, v, lse in soft:
                o_ref[pl.ds(q0, GRID_W), :] = jnp.dot(pn, v, preferred_element_type=F32).astype(BF)
                lse_ref[pl.ds(q0, GRID_W), :] = jnp.broadcast_to(lse, (GRID_W, HEAD_DIM))
            return carry

        lax.fori_loop(0, rows // NBR_SIDE, step, 0)

    out = pl.BlockSpec((S, HEAD_DIM), lambda h: (0, h))
    return ORDER.call(
        body, [qkv3, e2],
        [pl.BlockSpec((3, S, HEAD_DIM), lambda h: (0, 0, N_HEADS_A + h)),
         pl.BlockSpec((None, RPB_ROWS - 1, GRID_W, 128), lambda h: (h, 0, 0, 0))],
        name="attn_b_fwd", grid=(4,),
        out_specs=[out, out], out_shape=[_sds((S, 512), BF), _sds((S, 512), F32)],
        compiler_params=_cparams(("parallel",)),
    )


def _attn_b_bwd(qkv3, e2, dy, y, lse, dqkv3):
    _, S, _ = qkv3.shape
    rows = S // GRID_W
    nk = NA_ROWS * GRID_W

    def body(qkv_ref, e2_ref, dy_ref, y_ref, lse_ref, _, out_ref, de2_ref, dk_acc, dv_acc):
        ok = _nbr_col_ok()
        dk_acc[...] = jnp.zeros_like(dk_acc)
        dv_acc[...] = jnp.zeros_like(dv_acc)
        de2_ref[...] = jnp.zeros_like(de2_ref)

        def step(i, carry):
            blocks = [_nbr_scores(qkv_ref, e2_ref, NBR_SIDE * i + u, rows, ok) for u in range(NBR_SIDE)]
            dys = [dy_ref[pl.ds(b[0], GRID_W), :] for b in blocks]
            dps = [lax.dot_general(dyv, b[5], (NT, ((), ())), preferred_element_type=F32) for dyv, b in zip(dys, blocks)]
            grads = []
            for (q0, k0, first, q, k, v, s), dyv, dp in zip(blocks, dys, dps):
                qrows = pl.ds(q0, GRID_W)
                delta = jnp.sum(dyv.astype(F32) * y_ref[qrows, :].astype(F32), axis=-1, keepdims=True)
                p = jnp.where(ok, jnp.exp(s - jnp.tile(lse_ref[qrows, :], (1, nk // HEAD_DIM))), 0.0)
                ds = p * (dp - delta)
                for pair in range(NA_ROWS // 2):
                    de2_ref[first + 2 * pair] += ds[:, pair * 128:(pair + 1) * 128]
                grads.append((ds.astype(BF), p.astype(BF)))
            for (q0, k0, first, q, k, v, s), dyv, (dsb, pb) in zip(blocks, dys, grads):
                out_ref[0, pl.ds(q0, GRID_W), :] = (jnp.dot(dsb, k, preferred_element_type=F32) * SCALE).astype(BF)
                keys = pl.ds(k0, nk)
                dk_acc[keys, :] += lax.dot_general(dsb, q, (TN, ((), ())), preferred_element_type=F32) * SCALE
                dv_acc[keys, :] += lax.dot_general(pb, dyv, (TN, ((), ())), preferred_element_type=F32)
            return carry

        lax.fori_loop(0, rows // NBR_SIDE, step, 0)
        out_ref[1] = dk_acc[...].astype(BF)
        out_ref[2] = dv_acc[...].astype(BF)

    heads = pl.BlockSpec((3, S, HEAD_DIM), lambda h: (0, 0, N_HEADS_A + h))
    row = pl.BlockSpec((S, HEAD_DIM), lambda h: (0, h))
    table = pl.BlockSpec((None, RPB_ROWS - 1, GRID_W, 128), lambda h: (h, 0, 0, 0))
    return ORDER.call(
        body, [qkv3, e2, dy, y, lse, dqkv3],
        [heads, table, row, row, row, pl.BlockSpec(memory_space=pl.ANY)], name="attn_b_bwd", grid=(4,),
        out_specs=[heads, table],
        out_shape=[_sds((3, S, QKV_W), BF), _sds((4, RPB_ROWS - 1, GRID_W, 128), F32)],
        scratch_shapes=[pltpu.VMEM((S, HEAD_DIM), F32), pltpu.VMEM((S, HEAD_DIM), F32)],
        input_output_aliases={5: 0},
        compiler_params=_cparams(("parallel",)), chain_output=1,
    )


def _rpb_to_table(rpb):
    pad = jnp.pad(rpb, ((0, 0), (0, 0), (0, 1)))
    pairs = jnp.concatenate([pad[:, :-1], pad[:, 1:]], axis=-1).reshape(4 * (RPB_ROWS - 1), 64)
    onehot = jnp.asarray(_toeplitz_onehot())
    n = onehot.shape[1]
    tn = 2048
    full = lambda i, j, k: (0, 0)
    (e2,) = _matmul("rpb_table", pairs, onehot, pl.BlockSpec(pairs.shape, full),
                    pl.BlockSpec((64, tn), lambda i, j, k: (0, j)), NN, (1, n // tn, 1), (pairs.shape[0], tn), [],
                    [(_sds((pairs.shape[0], n), F32), pl.BlockSpec((pairs.shape[0], tn), lambda i, j, k: (0, j)))],
                    _store(F32), precision=lax.Precision.HIGHEST)
    return e2.reshape(4, RPB_ROWS - 1, GRID_W, 128)


def _table_grad_to_rpb(de2):
    onehot = jnp.asarray(_toeplitz_onehot())
    n = onehot.shape[1]
    flat = de2.reshape(4 * (RPB_ROWS - 1), n)
    tk = 2048
    (dpairs,) = _matmul("rpb_table_grad", flat, onehot, pl.BlockSpec((flat.shape[0], tk), lambda i, j, k: (0, k)),
                        pl.BlockSpec((64, tk), lambda i, j, k: (0, k)), NT, (1, 1, n // tk), (flat.shape[0], 64), [],
                        [(_sds((flat.shape[0], 64), F32), pl.BlockSpec((flat.shape[0], 64), lambda i, j, k: (0, 0)))],
                        _store(F32), precision=lax.Precision.HIGHEST)
    dpairs = dpairs.reshape(4, RPB_ROWS - 1, 64)
    zero = jnp.zeros((4, 1, RPB_COLS), F32)
    return (jnp.concatenate([dpairs[:, :, :RPB_COLS], zero], axis=1)
            + jnp.concatenate([zero, dpairs[:, :, 32:32 + RPB_COLS]], axis=1))


HBM = pl.BlockSpec(memory_space=pl.ANY)


def _place():
    x, y, c = lax.axis_index("x"), lax.axis_index("y"), lax.axis_index("c")
    chips = [(1 - x, y), (x, 1 - y), (1 - x, 1 - y)]
    return x, y, c, chips


def _remote(src, dst, send_sem, recv_sem, to):
    return pltpu.make_async_remote_copy(src_ref=src, dst_ref=dst, send_sem=send_sem, recv_sem=recv_sem,
                                        device_id=to, device_id_type=MESH)


def _place_shard(name, w, me):
    R, C = w.shape
    tr = _tile(R, 256)

    def body(me_ref, w_ref, o_ref):
        o_ref[...] = w_ref[...].astype(BF)

    return ORDER.call(
        body, [w], [pl.BlockSpec((tr, C), lambda i, mr: (i, 0))], prefetch=(me,), name=name, grid=(R // tr,),
        out_specs=pl.BlockSpec((None, tr, C), lambda i, mr: (mr[0], i, 0)),
        out_shape=_sds((N_CHIPS, R, C), BF), compiler_params=_cparams(("parallel",)),
    )


SEM = pl.BlockSpec(memory_space=pltpu.SEMAPHORE)
IN_HBM = pl.BlockSpec(memory_space=pltpu.HBM)
DATAFLOW = pltpu.SideEffectType.DATAFLOW_SIDE_EFFECTING


def _in_hbm(a):
    return pltpu.with_memory_space_constraint(a, pltpu.HBM)


def _copy_start(name, bufs, copies, n_copies, earlier=None):
    n = len(bufs)
    after = None if any(b is ORDER.last for b in bufs) else ORDER.last
    n_extra = (2 if earlier is not None else 0) + (1 if after is not None else 0)

    def body(*refs):
        ins = refs[:n]
        if earlier is not None:
            for k, (src, dst, to) in enumerate(earlier[0](ins)):
                cp = _remote(src, dst, refs[n].at[k], refs[n + 1].at[k], to)
                cp.wait_send()
                cp.wait_recv()
        send_sems, recv_sems = refs[n + n_extra], refs[n + n_extra + 1]
        for k, (src, dst, to) in enumerate(copies(ins)):
            _remote(src, dst, send_sems.at[k], recv_sems.at[k], to).start()
        refs[-1][...] = jnp.zeros((8, 128), F32)

    operands = [_in_hbm(b) for b in bufs]
    in_specs = [IN_HBM] * n
    if earlier is not None:
        operands += [earlier[1], earlier[2]]
        in_specs += [SEM, SEM]
    if after is not None:
        operands.append(after)
        in_specs.append(HBM)
    outs = pl.pallas_call(
        body, name=name,
        out_shape=(pltpu.SemaphoreType.DMA((n_copies,)), pltpu.SemaphoreType.DMA((n_copies,)),
                   *[pltpu.HBM(b.shape, b.dtype) for b in bufs], _sds((8, 128), F32)),
        in_specs=in_specs,
        out_specs=(SEM, SEM, *[IN_HBM] * n, pl.BlockSpec(memory_space=pltpu.VMEM)),
        input_output_aliases={i: 2 + i for i in range(n)},
        compiler_params=pltpu.CompilerParams(has_side_effects=DATAFLOW),
    )(*operands)
    ORDER.last = outs[-1]
    return outs[0], outs[1], list(outs[2:2 + n])


def _copy_wait(name, bufs, copies, send_sems, recv_sems):
    n = len(bufs)
    after = ORDER.last

    def body(*refs):
        ins = refs[:n]
        for k, (src, dst, to) in enumerate(copies(ins)):
            cp = _remote(src, dst, refs[n].at[k], refs[n + 1].at[k], to)
            cp.wait_send()
            cp.wait_recv()

    outs = list(pl.pallas_call(
        body, name=name,
        out_shape=tuple(pltpu.HBM(b.shape, b.dtype) for b in bufs),
        in_specs=[IN_HBM] * n + [SEM, SEM, HBM], out_specs=tuple([IN_HBM] * n),
        input_output_aliases={i: i for i in range(n)},
        compiler_params=pltpu.CompilerParams(has_side_effects=DATAFLOW),
    )(*bufs, send_sems, recv_sems, after))
    ORDER.last = outs[0]
    return outs


def _gather_hop1(bufs):
    x, y, c, chips = _place()
    out = []
    for b in bufs:
        half = b.shape[1] // 2
        mine = b.at[2 * x + y, pl.ds(c * half, half), :]
        out += [(mine, mine, (*chip, c)) for chip in chips]
    return out


def _gather_hop2(bufs):
    x, y, c, chips = _place()
    out = []
    for b in bufs:
        half = b.shape[1] // 2
        for chip in chips:
            landed = b.at[2 * chip[0] + chip[1], pl.ds(c * half, half), :]
            out.append((landed, landed, (x, y, 1 - c)))
    return out


def _swap_copies(bufs):
    x, y, c, _ = _place()
    n = len(bufs) // 2
    out = []
    for p, land in zip(bufs[:n], bufs[n:]):
        half = p.shape[1] // 2
        out.append((p.at[:, pl.ds((1 - c) * half, half), :], land, (x, y, 1 - c)))
    return out


def _scatter_copies(bufs):
    _, _, c, chips = _place()
    n = len(bufs) // 2
    out = []
    for s_, land in zip(bufs[:n], bufs[n:]):
        out += [(s_.at[2 * chip[0] + chip[1]], land.at[j], (*chip, c)) for j, chip in enumerate(chips)]
    return out


def _join_copies(bufs):
    x, y, c, _ = _place()
    out = []
    for b in bufs:
        half = b.shape[0] // 2
        mine = b.at[pl.ds(c * half, half), :]
        out.append((mine, mine, (x, y, 1 - c)))
    return out


def _gather_small(vec):
    m_per, n = vec.shape

    def body(x_ref, out_ref, send_sems, recv_sems, local_sem):
        x, y, c, chips = _place()
        me, sibling = (x, y, c), (x, y, 1 - c)

        def rows(px, py, pc):
            return out_ref.at[pl.ds((4 * px + 2 * py + pc) * m_per, m_per), :]

        def copy(k, block, to, src=None):
            return _remote(rows(*block) if src is None else src, rows(*block), send_sems.at[k], recv_sems.at[k], to)

        mine = pltpu.make_async_copy(x_ref, rows(*me), local_sem)
        mine.start()
        first = [copy(0, me, sibling, src=x_ref)]
        first += [copy(1 + j, me, (*chip, c), src=x_ref) for j, chip in enumerate(chips)]
        for cp in first:
            cp.start()
        passed = [copy(4 + j, (*chip, c), sibling) for j, chip in enumerate(chips)]
        for j, chip in enumerate(chips):
            copy(1 + j, (*chip, c), me).wait_recv()
            passed[j].start()
        copy(0, sibling, me).wait_recv()
        for j, chip in enumerate(chips):
            copy(4 + j, (*chip, 1 - c), me).wait_recv()
        for cp in first + passed:
            cp.wait_send()
        mine.wait()

    return ORDER.call(
        body, [vec], [pl.BlockSpec(memory_space=pltpu.VMEM)], name="gather_small_grads",
        out_shape=_sds((8 * m_per, n), vec.dtype), out_specs=pl.BlockSpec(memory_space=pltpu.VMEM),
        scratch_shapes=[pltpu.SemaphoreType.DMA((7,)), pltpu.SemaphoreType.DMA((7,)), pltpu.SemaphoreType.DMA],
    )


def _add_sibling(name, partial, received, c):
    _, R, C = partial.shape
    half = R // 2
    tr = _tile(half, 256)
    nb = half // tr

    def body(c_ref, p_ref, r_ref, o_ref):
        o_ref[...] = (p_ref[...].astype(F32) + r_ref[...].astype(F32)).astype(BF)

    return ORDER.call(
        body, [partial, received],
        [pl.BlockSpec((None, tr, C), lambda j, i, cr: (j, cr[0] * nb + i, 0)),
         pl.BlockSpec((None, tr, C), lambda j, i, cr: (j, i, 0))],
        prefetch=(c,), name=name, grid=(N_CHIPS, nb),
        out_specs=pl.BlockSpec((None, tr, C), lambda j, i, cr: (j, i, 0)),
        out_shape=_sds((N_CHIPS, half, C), BF), compiler_params=_cparams(("parallel", "parallel")),
    )


def _add_chips(name, sums, received, me_c):
    _, half, C = sums.shape
    tr = _tile(half, 256)
    nb = half // tr

    def body(mc_ref, s_ref, r_ref, o_ref):
        acc = s_ref[...].astype(F32)
        for j in range(3):
            acc = acc + r_ref[j].astype(F32)
        o_ref[...] = acc

    return ORDER.call(
        body, [sums, received],
        [pl.BlockSpec((None, tr, C), lambda i, mc: (mc[0], i, 0)),
         pl.BlockSpec((3, tr, C), lambda i, mc: (0, i, 0))],
        prefetch=(me_c,), name=name, grid=(nb,),
        out_specs=pl.BlockSpec((tr, C), lambda i, mc: (mc[1] * nb + i, 0)),
        out_shape=_sds((2 * half, C), F32), compiler_params=_cparams(("parallel",)),
    )


def _adamw_math(w, g, m, v):
    m = ADAM_B1 * m + (1.0 - ADAM_B1) * g
    v = ADAM_B2 * v + (1.0 - ADAM_B2) * (g * g)
    m_hat = m / (1.0 - ADAM_B1 ** ADAM_STEP)
    v_hat = v / (1.0 - ADAM_B2 ** ADAM_STEP)
    delta = -ADAM_LR * (m_hat / (jnp.sqrt(v_hat) + ADAM_EPS) + ADAM_WD * w)
    return delta, m, v


def _adamw(name, w, g, m, v):
    R, C = w.shape
    tr = _tile(R, 128)

    def body(w_ref, g_ref, m_ref, v_ref, go_ref, d_ref, mo_ref, vo_ref):
        gv = g_ref[...]
        go_ref[...] = gv
        d_ref[...], mo_ref[...], vo_ref[...] = _adamw_math(w_ref[...], gv, m_ref[...], v_ref[...])

    row = pl.BlockSpec((tr, C), lambda i: (i, 0))
    return ORDER.call(
        body, [w, g, m, v], [row] * 4, name=name, grid=(R // tr,), out_specs=[row] * 4,
        out_shape=[_sds((R, C), F32)] * 4, compiler_params=_cparams(("parallel",)), chain_output=1,
    )


def _adamw_small(gathered, w, m, v):
    rows, n = w.shape

    def body(ga_ref, w_ref, m_ref, v_ref, go_ref, d_ref, mo_ref, vo_ref):
        g = ga_ref[pl.ds(0, rows), :]
        for dev in range(1, 8):
            g = g + ga_ref[pl.ds(dev * rows, rows), :]
        go_ref[...] = g
        d_ref[...], mo_ref[...], vo_ref[...] = _adamw_math(w_ref[...], g, m_ref[...], v_ref[...])

    whole = pl.BlockSpec(memory_space=pltpu.VMEM)
    return ORDER.call(
        body, [gathered, w, m, v], [whole] * 4, name="adamw_small", out_specs=[whole] * 4,
        out_shape=[_sds((rows, n), F32)] * 4, compiler_params=_cparams(), chain_output=1,
    )


class _Exchange:
    GATHER = (("qkv",), ("gate", "proj_a", "proj_b", "out"), ("up", "down"))
    REDUCE = {"mlp": ("down", "up"), "mix": ("out", "proj_a", "proj_b"), "in": ("qkv", "gate")}

    def __init__(self, shards, me, c):
        self.me, self.c = me, c
        self.hop1, self.hop2, self.stage, self.grads = {}, {}, {}, {}
        for g, names in enumerate(self.GATHER):
            bufs = [_place_shard(f"place_{n}", shards[n], me) for n in names]
            self.hop1[g] = _copy_start(f"gather{g}_start", bufs, _gather_hop1, 3 * len(names))

    def forward(self, g):
        send, recv, thru = self.hop1.pop(g)
        self.hop2[g] = _copy_start(f"gather{g}_forward", thru, _gather_hop2, len(thru) * 3,
                                   earlier=(_gather_hop1, send, recv))

    def weights(self, g):
        send, recv, thru = self.hop2.pop(g)
        return _copy_wait(f"gather{g}_wait", thru, _gather_hop2, send, recv)

    def reduce(self, key, partials=None):
        names = self.REDUCE[key]
        n = len(names)
        if partials is not None:
            lands = [lax.empty((p.shape[0], p.shape[1] // 2, p.shape[2]), p.dtype) for p in partials]
            self.stage[key] = ("swap",) + _copy_start(f"reduce_{key}_swap", list(partials) + lands, _swap_copies, n)
            return
        kind, send, recv, thru = self.stage.pop(key)
        if kind == "swap":
            thru = _copy_wait(f"reduce_{key}_swap_wait", thru, _swap_copies, send, recv)
            sums = [_add_sibling(f"reduce_{nm}_add_sibling", p, r, self.c)
                    for nm, p, r in zip(names, thru[:n], thru[n:])]
            lands = [lax.empty((3,) + s_.shape[1:], s_.dtype) for s_ in sums]
            self.stage[key] = ("scatter",) + _copy_start(f"reduce_{key}_scatter", sums + lands, _scatter_copies, 3 * n)
        elif kind == "scatter":
            thru = _copy_wait(f"reduce_{key}_scatter_wait", thru, _scatter_copies, send, recv)
            me_c = jnp.concatenate([self.me, self.c])
            halves = [_add_chips(f"reduce_{nm}_add_chips", s_, r, me_c)
                      for nm, s_, r in zip(names, thru[:n], thru[n:])]
            self.stage[key] = ("join",) + _copy_start(f"reduce_{key}_join", halves, _join_copies, n)
        else:
            thru = _copy_wait(f"reduce_{key}_join_wait", thru, _join_copies, send, recv)
            self.grads.update(zip(names, thru))


def _forward_backward(x, target, norm_mix, b_gate, rpb, norm_mlp, norm_final, ex):
    S, D = x.shape

    h1 = _rms_fwd("rms_mix", x, norm_mix)
    ex.forward(0)
    e2 = _rpb_to_table(rpb)
    (gq,) = ex.weights(0)
    nq = QKV_W // 512
    (qkv3,), _ = _mm_nn_cols(
        "qkv", h1, gq, BF, tn=512,
        outs=[(_sds((3, S, QKV_W), BF), pl.BlockSpec((None, _tile(S, 1024), 512), lambda i, j, k: (j // nq, i, j % nq)))])

    ex.forward(1)
    outs_a = [_attn_a_fwd(qkv3, 0, DILATIONS[0])]
    gg, gpa, gpb, gout = ex.weights(1)
    wout = gout.reshape(D, D)

    tg = _tile(gg.shape[2], 1024)
    ng = D // tg

    def gate_epilogue(acc, ex_, outs):
        outs[0][...] = jax.nn.sigmoid(acc + ex_[0][...])

    (g3,), _ = _mm_nn_cols(
        "gate", h1, gg, F32, epilogue=gate_epilogue, tn=tg,
        extras=[(b_gate, pl.BlockSpec((1, tg), lambda i, j, k: (0, j)))],
        outs=[(_sds((2, S, D), F32), pl.BlockSpec((None, _tile(S, 1024), tg), lambda i, j, k: (j // ng, i, j % ng)))])

    outs_a += [_attn_a_fwd(qkv3, grp, d) for grp, d in enumerate(DILATIONS) if grp > 0]
    y_a, lj = _attn_a_combine([o for o, _ in outs_a], [l for _, l in outs_a])
    y_b, lse_b = _attn_b_fwd(qkv3, e2)

    (pa,), (tm, tp, _) = _mm_nn_cols("proj_a", y_a, gpa, F32, tn=512)

    def merge_epilogue(acc, ex_, outs):
        g = ex_[0][...]
        outs[0][...] = acc
        outs[1][...] = (g[0] * ex_[1][...] + g[1] * acc).astype(BF)

    tile = pl.BlockSpec((tm, tp), lambda i, j, k: (i, j))
    gates = pl.BlockSpec((2, tm, tp), lambda i, j, k: (0, i, j))
    (pb, merged), _ = _mm_nn_cols(
        "proj_b_merge", y_b, gpb, F32, epilogue=merge_epilogue, tn=512,
        extras=[(g3, gates), (pa, tile)],
        outs=[(_sds((S, D), F32), tile), (_sds((S, D), BF), tile)])

    def residual_epilogue(acc, ex_, outs):
        outs[0][...] = acc + ex_[0][...]

    def nn_plain(name, a, w, res):
        M, K = a.shape
        N = w.shape[1]
        bm, bn, bk = _tile(M, 1024), _tile(N, 1024), _tile(K, 2048)
        t = pl.BlockSpec((bm, bn), lambda i, j, k: (i, j))
        return _matmul(name, a, w, pl.BlockSpec((bm, bk), lambda i, j, k: (i, k)),
                       pl.BlockSpec((bk, bn), lambda i, j, k: (k, j)), NN, (M // bm, N // bn, K // bk), (bm, bn),
                       [(res, t)], [(_sds((M, N), F32), t)], residual_epilogue)[0]

    ex.forward(2)
    x1 = nn_plain("out_proj", merged, wout, x)
    h2 = _rms_fwd("rms_mlp", x1, norm_mlp)
    gup, gdown = ex.weights(2)
    F = gup.shape[2] * N_CHIPS
    wdown = gdown.reshape(F, D)

    def up_epilogue(acc, ex_, outs):
        ru = jnp.maximum(acc, 0.0)
        outs[0][...] = (ru * ru).astype(BF)
        outs[1][...] = ru.astype(BF)

    tu = _tile(gup.shape[2], 1024)
    ut = pl.BlockSpec((_tile(S, 1024), tu), lambda i, j, k: (i, j))
    (act, ru), _ = _mm_nn_cols("mlp_up", h2, gup, BF, epilogue=up_epilogue, tn=tu,
                               outs=[(_sds((S, F), BF), ut), (_sds((S, F), BF), ut)])
    x2 = nn_plain("mlp_down", act, wdown, x1)

    loss, dx2, dx2b, d_norm_final = _loss_head(x2, target, norm_final.reshape(1, D))

    def nt_rows(name, a, w, epilogue, extras, outs, bn=1024):
        M, N = a.shape
        K = w.shape[0]
        bm, bn, bk = _tile(M, 1024), _tile(K, bn), _tile(N, 2048)
        return _matmul(name, a, w, pl.BlockSpec((bm, bk), lambda i, j, k: (i, k)),
                       pl.BlockSpec((bn, bk), lambda i, j, k: (j, k)), NT, (M // bm, K // bn, N // bk), (bm, bn),
                       extras(bm, bn), outs(bm, bn), epilogue)

    def nt_cols(name, a_spec_fn, a, g, M, epilogue, extras, outs, bk):
        _, K, Nq = g.shape
        bm, bn, bk = _tile(M, 1024), _tile(K, 1024), _tile(Nq, bk)
        q = Nq // bk
        return _matmul(name, a, g, a_spec_fn(bm, bk), pl.BlockSpec((None, bn, bk), lambda i, j, k: (k // q, j, k % q)),
                       NT, (M // bm, K // bn, N_CHIPS * q), (bm, bn), extras(bm, bn), outs(bm, bn), epilogue)

    def tn_grad(name, a, a_spec_fn, b, b_spec_fn, Kin, N, out_shape, out_spec_fn, bn=1024):
        bm, bn, bk = _tile(Kin, 1024), _tile(N, bn), _tile(S, 2048)
        return _matmul(name, a, b, a_spec_fn(bk, bm), b_spec_fn(bk, bn), TN, (Kin // bm, N // bn, S // bk), (bm, bn),
                       [], [(_sds(out_shape, BF), out_spec_fn(bm, bn))], _store(BF))[0]

    plain_a = lambda bk, bm: pl.BlockSpec((bk, bm), lambda i, j, k: (k, i))
    plain_b = lambda bk, bn: pl.BlockSpec((bk, bn), lambda i, j, k: (k, j))
    plain_o = lambda bm, bn: pl.BlockSpec((bm, bn), lambda i, j, k: (i, j))
    a_rows = lambda bm, bk: pl.BlockSpec((bm, bk), lambda i, j, k: (i, k))

    def cols_o(Nq):
        def spec(bm, bn):
            q = Nq // bn
            return pl.BlockSpec((None, bm, bn), lambda i, j, k: (j // q, i, j % q))
        return spec

    def du_epilogue(acc, ex_, outs):
        outs[0][...] = (acc * (2.0 * ex_[0][...].astype(F32))).astype(BF)

    dw_down = tn_grad("mlp_down_dw", act, plain_a, dx2b, plain_b, F, D, (F, D), plain_o)
    (du,) = nt_rows("mlp_down_dx", dx2b, wdown, du_epilogue,
                    lambda bm, bn: [(ru, plain_o(bm, bn))], lambda bm, bn: [(_sds((S, F), BF), plain_o(bm, bn))])

    fq = gup.shape[2]
    dw_up = tn_grad("mlp_up_dw", h2, plain_a, du, plain_b, D, F, (N_CHIPS, D, fq), cols_o(fq), bn=min(fq, 1024))
    ex.reduce("mlp", partials=[dw_down.reshape(N_CHIPS, F // N_CHIPS, D), dw_up])
    (dh2,) = nt_cols("mlp_up_dx", a_rows, du, gup, S, _store(F32), lambda bm, bn: [],
                     lambda bm, bn: [(_sds((S, D), F32), plain_o(bm, bn))], 2048)
    ex.reduce("mlp")
    dx1, dx1b, d_norm_mlp = _rms_bwd("rms_mlp_bwd", dh2, x1, norm_mlp, dx2)

    def merge_bwd_epilogue(acc, ex_, outs):
        g, pav, pbv = ex_[0][...], ex_[1][...], ex_[2][...]
        outs[0][...] = (acc * g[0]).astype(BF)
        outs[1][...] = (acc * g[1]).astype(BF)
        dga = acc * pav * g[0] * (1.0 - g[0])
        dgb = acc * pbv * g[1] * (1.0 - g[1])
        outs[2][0] = dga.astype(BF)
        outs[2][1] = dgb.astype(BF)
        outs[3][...] = jnp.concatenate([jnp.sum(dga, axis=0, keepdims=True), jnp.sum(dgb, axis=0, keepdims=True)], 0)

    def pair(bm, bn):
        return pl.BlockSpec((2, bm, bn), lambda i, j, k: (0, i, j))

    n_row_blocks = S // _tile(S, 1024)
    dpa, dpb, dg3, db_gate = nt_rows(
        "out_proj_dx", dx1b, wout, merge_bwd_epilogue,
        lambda bm, bn: [(g3, pair(bm, bn)), (pa, plain_o(bm, bn)), (pb, plain_o(bm, bn))],
        lambda bm, bn: [(_sds((S, D), BF), plain_o(bm, bn)), (_sds((S, D), BF), plain_o(bm, bn)),
                        (_sds((2, S, D), BF), pair(bm, bn)),
                        (_sds((n_row_blocks, 2, D), F32), pl.BlockSpec((None, 2, bn), lambda i, j, k: (i, 0, j)))],
        bn=512)
    dw_out = tn_grad("out_proj_dw", merged, plain_a, dx1b, plain_b, D, D, (D, D), plain_o)

    pq = gpa.shape[2]
    proj_dx = lambda name, dproj, g: nt_cols(name, a_rows, dproj, g, S, _store(BF), lambda bm, bn: [],
                                             lambda bm, bn: [(_sds((S, 512), BF), plain_o(bm, bn))], 512)[0]
    dw_pa = tn_grad("proj_a_dw", y_a, plain_a, dpa, plain_b, 512, D, (N_CHIPS, 512, pq), cols_o(pq), bn=min(pq, 512))
    dw_pb = tn_grad("proj_b_dw", y_b, plain_a, dpb, plain_b, 512, D, (N_CHIPS, 512, pq), cols_o(pq), bn=min(pq, 512))
    ex.reduce("mix", partials=[dw_out.reshape(N_CHIPS, D // N_CHIPS, D), dw_pa, dw_pb])
    dy_a = proj_dx("proj_a_dx", dpa, gpa)
    dy_b = proj_dx("proj_b_dx", dpb, gpb)

    dqkv3 = lax.empty((3, S, QKV_W), BF)
    dqkv3 = _attn_a_bwd(qkv3, dy_a, y_a, lj, dqkv3, 0, DILATIONS[0])
    ex.reduce("mix")
    for grp, d in enumerate(DILATIONS):
        if grp > 0:
            dqkv3 = _attn_a_bwd(qkv3, dy_a, y_a, lj, dqkv3, grp, d)
    dqkv3, de2 = _attn_b_bwd(qkv3, e2, dy_b, y_b, lse_b, dqkv3)
    d_rpb = _table_grad_to_rpb(de2)

    def stacked_a(width):
        def spec(bm, bk):
            q = width // bk
            return pl.BlockSpec((None, bm, bk), lambda i, j, k: (k // q, i, k % q))
        return spec

    def stacked_b(width):
        def spec(bk, bn):
            q = width // bn
            return pl.BlockSpec((None, bk, bn), lambda i, j, k: (j // q, k, j % q))
        return spec

    dw_qkv = tn_grad("qkv_dw", h1, plain_a, dqkv3, stacked_b(QKV_W), D, 3 * QKV_W, (N_CHIPS,) + gq.shape[1:],
                     cols_o(gq.shape[2]), bn=512)
    dw_gate = tn_grad("gate_dw", h1, plain_a, dg3, stacked_b(D), D, 2 * D, (N_CHIPS,) + gg.shape[1:],
                      cols_o(gg.shape[2]), bn=gg.shape[2])
    ex.reduce("in", partials=[dw_qkv, dw_gate])
    ex.reduce("mlp")
    (dh1_q,) = nt_cols("qkv_dx", stacked_a(QKV_W), dqkv3, gq, S, _store(F32), lambda bm, bn: [],
                       lambda bm, bn: [(_sds((S, D), F32), plain_o(bm, bn))], 512)
    ex.reduce("in")
    ex.reduce("mix")

    def add_epilogue(acc, ex_, outs):
        outs[0][...] = acc + ex_[0][...]

    (dh1,) = nt_cols("gate_dx", stacked_a(D), dg3, gg, S, add_epilogue, lambda bm, bn: [(dh1_q, plain_o(bm, bn))],
                     lambda bm, bn: [(_sds((S, D), F32), plain_o(bm, bn))], gg.shape[2])
    grad_x, _, d_norm_mix = _rms_bwd("rms_mix_bwd", dh1, x, norm_mix, dx1)
    ex.reduce("mlp")
    ex.reduce("mix")

    small = [d_norm_mix, jnp.sum(db_gate, axis=0).reshape(1, 2 * D), d_rpb, d_norm_mlp, d_norm_final]
    return loss, grad_x, small


def _pack_small(parts, width):
    flat = jnp.concatenate([p.reshape(-1) for p in parts])
    return jnp.pad(flat, (0, 8 * width - flat.shape[0])).reshape(8, width)


def kernel(x, norm_mix, w_qkv, w_gate, b_gate, rpb, w_proj_a, w_proj_b, w_out, norm_mlp, w_up, w_down, norm_final, loss_target, m_norm_mix, m_w_qkv, m_w_gate, m_b_gate, m_rpb, m_w_proj_a, m_w_proj_b, m_w_out, m_norm_mlp, m_w_up, m_w_down, m_norm_final, v_norm_mix, v_w_qkv, v_w_gate, v_b_gate, v_rpb, v_w_proj_a, v_w_proj_b, v_w_out, v_norm_mlp, v_w_up, v_w_down, v_norm_final):
    names = ["qkv", "gate", "proj_a", "proj_b", "out", "up", "down"]
    big = dict(zip(names, [w_qkv[0], w_gate[0], w_proj_a[0], w_proj_b[0], w_out[0], w_up[0], w_down[0]]))
    big_m = dict(zip(names, [m_w_qkv[0], m_w_gate[0], m_w_proj_a[0], m_w_proj_b[0], m_w_out[0], m_w_up[0], m_w_down[0]]))
    big_v = dict(zip(names, [v_w_qkv[0], v_w_gate[0], v_w_proj_a[0], v_w_proj_b[0], v_w_out[0], v_w_up[0], v_w_down[0]]))

    c = lax.axis_index("c").astype(jnp.int32).reshape(1)
    me = (2 * lax.axis_index("x") + lax.axis_index("y")).astype(jnp.int32).reshape(1)
    ORDER.last = None
    ex = _Exchange(big, me, c)
    loss, grad_x, small = _forward_backward(x[0], loss_target[0], norm_mix, b_gate, rpb[0], norm_mlp, norm_final, ex)

    def adamw(group):
        return {n: _adamw(f"adamw_{n}", big[n], ex.grads[n], big_m[n], big_v[n]) for n in _Exchange.REDUCE[group]}

    big_out = {**adamw("mlp"), **adamw("mix")}
    ex.reduce("in")

    small_w = [norm_mix, b_gate, rpb, norm_mlp, norm_final]
    count = sum(int(np.prod(p.shape)) for p in small_w)
    width = -(-count // (8 * 128)) * 128
    packed = _adamw_small(_gather_small(_pack_small(small, width)), _pack_small(small_w, width),
                          _pack_small([m_norm_mix, m_b_gate, m_rpb, m_norm_mlp, m_norm_final], width),
                          _pack_small([v_norm_mix, v_b_gate, v_rpb, v_norm_mlp, v_norm_final], width))
    ex.reduce("in")
    big_out.update(adamw("in"))

    def unpack(flat2d):
        flat, out, at = flat2d.reshape(-1), [], 0
        for p in small_w:
            size = int(np.prod(p.shape))
            out.append(flat[at:at + size].reshape(p.shape))
            at += size
        return out

    small_out = [unpack(a) for a in packed]

    def ordered(kind):
        sm = small_out[kind]
        bg = {n: o[kind][None] for n, o in big_out.items()}
        return [sm[0], bg["qkv"], bg["gate"], sm[1], sm[2], bg["proj_a"], bg["proj_b"], bg["out"], sm[3],
                bg["up"], bg["down"], sm[4]]

    total = lax.psum(loss[0, 0], ("x", "y", "c"))
    return (total, grad_x[None], *ordered(0), *ordered(1), *ordered(2), *ordered(3))
```

```python
import functools
import math

import numpy as np
import jax
import jax.numpy as jnp
from jax import lax
from jax.experimental import pallas as pl
from jax.experimental.pallas import tpu as pltpu

BF = jnp.bfloat16
F32 = jnp.float32
MESH = pl.DeviceIdType.MESH

HEAD_DIM = 128
N_HEADS = 16
N_HEADS_A = 12
QKV_W = N_HEADS * HEAD_DIM
DILATIONS = (1, 4, 16)
HALF_WINDOW = 64
GRID_W = 64
NA_ROWS = 8
NA_COLS = 16
RPB_ROWS = 2 * NA_ROWS - 1
RPB_COLS = 2 * NA_COLS - 1
EPS = 1e-6
NEG = -1e30
SCALE = HEAD_DIM ** -0.5

ADAM_LR = 0.001
ADAM_B1 = 0.9
ADAM_B2 = 0.999
ADAM_EPS = 1e-08
ADAM_WD = 0.01
ADAM_STEP = 10

N_CHIPS = 4
VMEM_LIMIT_BYTES = 48 * 1024 * 1024
QB = 256
NBR_SIDE = 4


def _key_rows(L):
    return min(QB + 2 * HALF_WINDOW, L)


def _cparams(sem=None):
    return pltpu.CompilerParams(dimension_semantics=sem, vmem_limit_bytes=VMEM_LIMIT_BYTES)


def _tile(dim, want):
    t = min(dim, want)
    assert dim % t == 0, (dim, want)
    return t


class _ProgramOrder:
    def __init__(self):
        self.last = None

    def call(self, body, operands, in_specs, *, prefetch=(), grid=None, out_specs=None, chain_output=0, **kwargs):
        operands, in_specs = list(operands), list(in_specs)
        lead = len(prefetch) + len(operands)
        if self.last is not None and not any(op is self.last for op in operands):
            operands.append(self.last)
            in_specs.append(pl.BlockSpec(memory_space=pl.ANY))
            inner = body

            def body(*refs):
                return inner(*refs[:lead], *refs[lead + 1:])

        if prefetch:
            kwargs["grid_spec"] = pltpu.PrefetchScalarGridSpec(
                num_scalar_prefetch=len(prefetch), grid=grid, in_specs=in_specs, out_specs=out_specs)
        else:
            kwargs.update(in_specs=in_specs, out_specs=out_specs)
            if grid is not None:
                kwargs["grid"] = grid
        out = pl.pallas_call(body, **kwargs)(*prefetch, *operands)
        self.last = out[chain_output] if isinstance(out, (tuple, list)) else out
        return out


ORDER = _ProgramOrder()


NN = ((1,), (0,))
NT = ((1,), (1,))
TN = ((0,), (0,))


def _matmul(name, a, b, a_spec, b_spec, dims, grid, acc_shape, extras, outs, epilogue, precision=None):
    n_ex, n_out, nk = len(extras), len(outs), grid[2]

    def body(*refs):
        a_ref, b_ref = refs[0], refs[1]
        ex_refs = refs[2:2 + n_ex]
        out_refs = refs[2 + n_ex:2 + n_ex + n_out]

        def dot():
            return lax.dot_general(a_ref[...], b_ref[...], (dims, ((), ())),
                                   preferred_element_type=F32, precision=precision)

        if nk == 1:
            epilogue(dot(), ex_refs, out_refs)
            return
        acc_ref = refs[-1]
        k = pl.program_id(2)

        @pl.when(k == 0)
        def _():
            acc_ref[...] = dot()

        if nk > 2:
            @pl.when((k > 0) & (k < nk - 1))
            def _():
                acc_ref[...] += dot()

        @pl.when(k == nk - 1)
        def _():
            epilogue(acc_ref[...] + dot(), ex_refs, out_refs)

    return ORDER.call(
        body, [a, b] + [e for e, _ in extras], [a_spec, b_spec] + [s for _, s in extras], name=name, grid=grid,
        out_specs=[s for _, s in outs],
        out_shape=[sh for sh, _ in outs],
        scratch_shapes=[pltpu.VMEM(acc_shape, F32)] if nk > 1 else [],
        compiler_params=_cparams(("parallel", "parallel", "arbitrary")),
    )


def _store(dtype):
    def epilogue(acc, ex, outs):
        outs[0][...] = acc.astype(dtype)
    return epilogue


def _sds(shape, dtype):
    return jax.ShapeDtypeStruct(shape, dtype)


def _mm_nn_cols(name, a, g, out_dtype, epilogue=None, extras=(), outs=None, tm=1024, tn=1024, tk=2048):
    M, K = a.shape
    _, _, Nq = g.shape
    tm, tn, tk = _tile(M, tm), _tile(Nq, tn), _tile(K, tk)
    q = Nq // tn
    grid = (M // tm, N_CHIPS * q, K // tk)
    if outs is None:
        outs = [(_sds((M, N_CHIPS * Nq), out_dtype), pl.BlockSpec((tm, tn), lambda i, j, k: (i, j)))]
    return _matmul(name, a, g, pl.BlockSpec((tm, tk), lambda i, j, k: (i, k)),
                   pl.BlockSpec((None, tk, tn), lambda i, j, k: (j // q, k, j % q)), NN, grid, (tm, tn),
                   list(extras), outs, epilogue or _store(out_dtype)), (tm, tn, tk)


def _rms_fwd(name, x, g):
    S, D = x.shape
    tm = _tile(S, 256)

    def body(x_ref, g_ref, h_ref):
        xv = x_ref[...]
        r = lax.rsqrt(jnp.mean(xv * xv, axis=-1, keepdims=True) + EPS)
        h_ref[...] = ((xv * r) * g_ref[...]).astype(BF)

    row = pl.BlockSpec((tm, D), lambda i: (i, 0))
    return ORDER.call(
        body, [x, g], [row, pl.BlockSpec((1, D), lambda i: (0, 0))], name=name, grid=(S // tm,),
        out_specs=row, out_shape=_sds((S, D), BF), compiler_params=_cparams(("parallel",)),
    )


def _rms_bwd(name, dh, x, g, dres):
    S, D = x.shape
    tm = _tile(S, 256)

    def body(dh_ref, x_ref, g_ref, dres_ref, dx_ref, dxb_ref, dg_ref):
        xv = x_ref[...]
        r = lax.rsqrt(jnp.mean(xv * xv, axis=-1, keepdims=True) + EPS)
        n = xv * r
        dhv = dh_ref[...]
        dyg = dhv * g_ref[...]
        dx = dres_ref[...] + r * (dyg - n * jnp.mean(dyg * n, axis=-1, keepdims=True))
        dx_ref[...] = dx
        dxb_ref[...] = dx.astype(BF)

        @pl.when(pl.program_id(0) == 0)
        def _():
            dg_ref[...] = jnp.zeros_like(dg_ref)

        dg_ref[...] += jnp.sum(dhv * n, axis=0, keepdims=True)

    row = pl.BlockSpec((tm, D), lambda i: (i, 0))
    vec = pl.BlockSpec((1, D), lambda i: (0, 0))
    return ORDER.call(
        body, [dh, x, g, dres], [row, row, vec, row], name=name, grid=(S // tm,),
        out_specs=[row, row, vec],
        out_shape=[_sds((S, D), F32), _sds((S, D), BF), _sds((1, D), F32)],
        compiler_params=_cparams(("arbitrary",)),
    )


def _loss_head(x2, target, g):
    S, D = x2.shape
    tm = _tile(S, 256)

    def body(x_ref, t_ref, g_ref, loss_ref, dx_ref, dxb_ref, dg_ref):
        xv = x_ref[...]
        gv = g_ref[...]
        r = lax.rsqrt(jnp.mean(xv * xv, axis=-1, keepdims=True) + EPS)
        n = xv * r
        e = n * gv - t_ref[...]
        dy = e * (1.0 / D)
        dyg = dy * gv
        dx = r * (dyg - n * jnp.mean(dyg * n, axis=-1, keepdims=True))
        dx_ref[...] = dx
        dxb_ref[...] = dx.astype(BF)

        @pl.when(pl.program_id(0) == 0)
        def _():
            dg_ref[...] = jnp.zeros_like(dg_ref)
            loss_ref[...] = jnp.zeros_like(loss_ref)

        dg_ref[...] += jnp.sum(dy * n, axis=0, keepdims=True)
        per_row = jnp.mean(e * e, axis=-1, keepdims=True)
        loss_ref[...] += 0.5 * jnp.sum(per_row, axis=0, keepdims=True)

    row = pl.BlockSpec((tm, D), lambda i: (i, 0))
    vec = pl.BlockSpec((1, D), lambda i: (0, 0))
    return ORDER.call(
        body, [x2, target, g], [row, row, vec], name="loss_head", grid=(S // tm,),
        out_specs=[pl.BlockSpec((1, 1), lambda i: (0, 0)), row, row, vec],
        out_shape=[_sds((1, 1), F32), _sds((S, D), F32), _sds((S, D), BF), _sds((1, D), F32)],
        compiler_params=_cparams(("arbitrary",)), chain_output=1,
    )


def _chains(L):
    side = min(4, L // QB)
    return side, 4 // side


def _band_scores(qkv_ref, i, L, coef, head):
    KB = _key_rows(L)
    lanes = pl.ds(head * HEAD_DIM, HEAD_DIM)
    q0 = pl.multiple_of(i * QB, QB)
    ks = pl.multiple_of(jnp.clip(i * QB - HALF_WINDOW, 0, L - KB), HALF_WINDOW)
    q = qkv_ref[0, pl.ds(q0, QB), lanes]
    k = qkv_ref[1, pl.ds(ks, KB), lanes]
    v = qkv_ref[2, pl.ds(ks, KB), lanes]
    s = lax.dot_general(q, k, (NT, ((), ())), preferred_element_type=F32) * SCALE
    qpos = q0 + lax.broadcasted_iota(jnp.int32, (QB, KB), 0)
    kpos = ks + lax.broadcasted_iota(jnp.int32, (QB, KB), 1)
    rel = jnp.abs(kpos - qpos)
    valid = rel <= HALF_WINDOW
    s = jnp.where(valid, s - coef * rel.astype(F32), NEG)
    return q0, ks, q, k, v, s, valid


def _alibi_coefs(group, d, heads):
    first = 4 * group + 1 + pl.program_id(1) * heads
    scale = jnp.full((1, 1), -(8.0 / N_HEADS_A) * math.log(2.0), F32)
    return [jnp.exp(scale * (first + hh).astype(F32)) * float(d) for hh in range(heads)]


def _dilated_view(qkv3, group, d, heads):
    _, S, _ = qkv3.shape
    L = S // d
    per = 4 // heads
    if d == 1:
        return qkv3, pl.BlockSpec((3, L, heads * HEAD_DIM), lambda r, j: (0, 0, per * group + j))
    cols = qkv3[:, :, 512 * group:512 * (group + 1)].reshape(3, L, d * 512)
    return cols, pl.BlockSpec((3, L, heads * HEAD_DIM), lambda r, j: (0, 0, r * per + j))


def _attn_a_fwd(qkv3, group, d):
    _, S, _ = qkv3.shape
    L = S // d
    assert L % QB == 0
    side, heads = _chains(L)
    view, blocks_spec = _dilated_view(qkv3, group, d, heads)

    def body(qkv_ref, o_ref, lse_ref):
        coefs = _alibi_coefs(group, d, heads)

        def step(i, carry):
            chains = [(hh, _band_scores(qkv_ref, side * i + u, L, coefs[hh], hh))
                      for u in range(side) for hh in range(heads)]
            soft = []
            for hh, (q0, _, _, _, v, s, _) in chains:
                m = jnp.max(s, axis=-1, keepdims=True)
                p = jnp.exp(s - m)
                den = jnp.sum(p, axis=-1, keepdims=True)
                soft.append((hh, q0, (p / den).astype(BF), v, m + jnp.log(den)))
            for hh, q0, pn, v, lse in soft:
                lanes = pl.ds(hh * HEAD_DIM, HEAD_DIM)
                o_ref[pl.ds(q0, QB), lanes] = jnp.dot(pn, v, preferred_element_type=F32)
                lse_ref[pl.ds(q0, QB), lanes] = jnp.broadcast_to(lse, (QB, HEAD_DIM))
            return carry

        lax.fori_loop(0, L // QB // side, step, 0)

    per = 4 // heads
    out = pl.BlockSpec((L, heads * HEAD_DIM), lambda r, j: (0, r * per + j))
    o, lse = ORDER.call(
        body, [view], [blocks_spec],
        name=f"attn_a_fwd_d{d}", grid=(d, per),
        out_specs=[out, out],
        out_shape=[_sds((L, d * 512), F32), _sds((L, d * 512), F32)],
        compiler_params=_cparams(("parallel", "parallel")),
    )
    return o.reshape(S, 512), lse.reshape(S, 512)


def _attn_a_combine(os_, lses):
    S, W = os_[0].shape
    tm = _tile(S, 512)

    def body(o0, o1, o2, l0, l1, l2, y_ref, lj_ref):
        ls = [l0[...], l1[...], l2[...]]
        m = jnp.maximum(jnp.maximum(ls[0], ls[1]), ls[2])
        es = [jnp.exp(l - m) for l in ls]
        den = es[0] + es[1] + es[2]
        y = (es[0] / den) * o0[...] + (es[1] / den) * o1[...] + (es[2] / den) * o2[...]
        y_ref[...] = y.astype(BF)
        lj_ref[...] = m + jnp.log(den)

    row = pl.BlockSpec((tm, W), lambda i: (i, 0))
    return ORDER.call(
        body, [*os_, *lses], [row] * 6, name="attn_a_combine", grid=(S // tm,), out_specs=[row, row],
        out_shape=[_sds((S, W), BF), _sds((S, W), F32)], compiler_params=_cparams(("parallel",)),
    )


def _attn_a_bwd(qkv3, dy, y, lj, dqkv3, group, d):
    _, S, _ = qkv3.shape
    L = S // d
    side, heads = _chains(L)
    view, blocks_spec = _dilated_view(qkv3, group, d, heads)

    def body(qkv_ref, dy_ref, y_ref, lj_ref, *rest):
        out_ref, dk_acc, dv_acc = rest[-3:]
        coefs = _alibi_coefs(group, d, heads)
        dk_acc[...] = jnp.zeros_like(dk_acc)
        dv_acc[...] = jnp.zeros_like(dv_acc)

        def step(i, carry):
            chains = [(pl.ds(hh * HEAD_DIM, HEAD_DIM), _band_scores(qkv_ref, side * i + u, L, coefs[hh], hh))
                      for u in range(side) for hh in range(heads)]
            dys = [dy_ref[pl.ds(c[0], QB), lanes] for lanes, c in chains]
            dps = [lax.dot_general(dyv, c[4], (NT, ((), ())), preferred_element_type=F32)
                   for dyv, (_, c) in zip(dys, chains)]
            grads = []
            for (lanes, (q0, ks, q, k, v, s, valid)), dyv, dp in zip(chains, dys, dps):
                rows = pl.ds(q0, QB)
                delta = jnp.sum(dyv.astype(F32) * y_ref[rows, lanes].astype(F32), axis=-1, keepdims=True)
                p = jnp.where(valid, jnp.exp(s - jnp.tile(lj_ref[rows, lanes], (1, _key_rows(L) // HEAD_DIM))), 0.0)
                grads.append(((p * (dp - delta)).astype(BF), p.astype(BF)))
            for (lanes, (q0, ks, q, k, v, s, valid)), dyv, (ds, pb) in zip(chains, dys, grads):
                out_ref[0, pl.ds(q0, QB), lanes] = (jnp.dot(ds, k, preferred_element_type=F32) * SCALE).astype(BF)
                keys = pl.ds(ks, _key_rows(L))
                dk_acc[keys, lanes] += lax.dot_general(ds, q, (TN, ((), ())), preferred_element_type=F32) * SCALE
                dv_acc[keys, lanes] += lax.dot_general(pb, dyv, (TN, ((), ())), preferred_element_type=F32)
            return carry

        lax.fori_loop(0, L // QB // side, step, 0)
        out_ref[1] = dk_acc[...].astype(BF)
        out_ref[2] = dv_acc[...].astype(BF)

    per = 4 // heads
    width = heads * HEAD_DIM
    row = pl.BlockSpec((L, width), lambda r, j: (0, r * per + j))
    operands = [view, dy.reshape(L, d * 512), y.reshape(L, d * 512), lj.reshape(L, d * 512)]
    scratch = [pltpu.VMEM((L, width), F32), pltpu.VMEM((L, width), F32)]
    if d == 1:
        return ORDER.call(
            body, operands + [dqkv3], [blocks_spec, row, row, row, pl.BlockSpec(memory_space=pl.ANY)],
            name=f"attn_a_bwd_d{d}", grid=(d, per), out_specs=blocks_spec, out_shape=_sds((3, S, QKV_W), BF),
            scratch_shapes=scratch, input_output_aliases={4: 0}, compiler_params=_cparams(("parallel", "parallel")))
    out = ORDER.call(
        body, operands, [blocks_spec, row, row, row], name=f"attn_a_bwd_d{d}", grid=(d, per),
        out_specs=blocks_spec, out_shape=_sds((3, L, d * 512), BF),
        scratch_shapes=scratch, compiler_params=_cparams(("parallel", "parallel")))
    return lax.dynamic_update_slice(dqkv3, out.reshape(3, S, 512), (0, 0, 512 * group))


def _toeplitz_onehot():
    oh = np.zeros((64, GRID_W, 128), np.float32)
    for qc in range(GRID_W):
        for m in range(128):
            kc = m % GRID_W
            dc = int(np.clip(kc - qc, -(NA_COLS - 1), NA_COLS - 1)) + NA_COLS - 1
            oh[(m // GRID_W) * 32 + dc, qc, m] = 1.0
    return oh.reshape(64, GRID_W * 128)


def _nbr_scores(qkv_ref, e2_ref, r, rows, ok):
    rs = jnp.clip(r - NA_ROWS // 2, 0, rows - NA_ROWS)
    q0 = pl.multiple_of(r * GRID_W, GRID_W)
    k0 = pl.multiple_of(rs * GRID_W, GRID_W)
    q = qkv_ref[0, pl.ds(q0, GRID_W), :]
    k = qkv_ref[1, pl.ds(k0, NA_ROWS * GRID_W), :]
    v = qkv_ref[2, pl.ds(k0, NA_ROWS * GRID_W), :]
    s = lax.dot_general(q, k, (NT, ((), ())), preferred_element_type=F32) * SCALE
    first = rs - r + NA_ROWS - 1
    bias = jnp.concatenate([e2_ref[first + 2 * pair] for pair in range(NA_ROWS // 2)], axis=1)
    s = jnp.where(ok, s + bias, NEG)
    return q0, k0, first, q, k, v, s


def _nbr_col_ok():
    qc = lax.broadcasted_iota(jnp.int32, (GRID_W, NA_ROWS * GRID_W), 0)
    kc = lax.broadcasted_iota(jnp.int32, (GRID_W, NA_ROWS * GRID_W), 1) % GRID_W
    cs = jnp.clip(qc - NA_COLS // 2, 0, GRID_W - NA_COLS)
    return (kc >= cs) & (kc < cs + NA_COLS)


def _attn_b_fwd(qkv3, e2):
    _, S, _ = qkv3.shape
    rows = S // GRID_W
    assert rows >= NA_ROWS

    def body(qkv_ref, e2_ref, o_ref, lse_ref):
        ok = _nbr_col_ok()

        def step(i, carry):
            blocks = [_nbr_scores(qkv_ref, e2_ref, NBR_SIDE * i + u, rows, ok) for u in range(NBR_SIDE)]
            soft = []
            for q0, _, _, _, _, v, s in blocks:
                m = jnp.max(s, axis=-1, keepdims=True)
                p = jnp.exp(s - m)
                den = jnp.sum(p, axis=-1, keepdims=True)
                soft.append((q0, (p / den).astype(BF), v, m + jnp.log(den)))
            for q0, pn, v, lse in soft:
                o_ref[pl.ds(q0, GRID_W), :] = jnp.dot(pn, v, preferred_element_type=F32).astype(BF)
                lse_ref[pl.ds(q0, GRID_W), :] = jnp.broadcast_to(lse, (GRID_W, HEAD_DIM))
            return carry

        lax.fori_loop(0, rows // NBR_SIDE, step, 0)

    out = pl.BlockSpec((S, HEAD_DIM), lambda h: (0, h))
    return ORDER.call(
        body, [qkv3, e2],
        [pl.BlockSpec((3, S, HEAD_DIM), lambda h: (0, 0, N_HEADS_A + h)),
         pl.BlockSpec((None, RPB_ROWS - 1, GRID_W, 128), lambda h: (h, 0, 0, 0))],
        name="attn_b_fwd", grid=(4,),
        out_specs=[out, out], out_shape=[_sds((S, 512), BF), _sds((S, 512), F32)],
        compiler_params=_cparams(("parallel",)),
    )


def _attn_b_bwd(qkv3, e2, dy, y, lse, dqkv3):
    _, S, _ = qkv3.shape
    rows = S // GRID_W
    nk = NA_ROWS * GRID_W

    def body(qkv_ref, e2_ref, dy_ref, y_ref, lse_ref, _, out_ref, de2_ref, dk_acc, dv_acc):
        ok = _nbr_col_ok()
        dk_acc[...] = jnp.zeros_like(dk_acc)
        dv_acc[...] = jnp.zeros_like(dv_acc)
        de2_ref[...] = jnp.zeros_like(de2_ref)

        def step(i, carry):
            blocks = [_nbr_scores(qkv_ref, e2_ref, NBR_SIDE * i + u, rows, ok) for u in range(NBR_SIDE)]
            dys = [dy_ref[pl.ds(b[0], GRID_W), :] for b in blocks]
            dps = [lax.dot_general(dyv, b[5], (NT, ((), ())), preferred_element_type=F32) for dyv, b in zip(dys, blocks)]
            grads = []
            for (q0, k0, first, q, k, v, s), dyv, dp in zip(blocks, dys, dps):
                qrows = pl.ds(q0, GRID_W)
                delta = jnp.sum(dyv.astype(F32) * y_ref[qrows, :].astype(F32), axis=-1, keepdims=True)
                p = jnp.where(ok, jnp.exp(s - jnp.tile(lse_ref[qrows, :], (1, nk // HEAD_DIM))), 0.0)
                ds = p * (dp - delta)
                for pair in range(NA_ROWS // 2):
                    de2_ref[first + 2 * pair] += ds[:, pair * 128:(pair + 1) * 128]
                grads.append((ds.astype(BF), p.astype(BF)))
            for (q0, k0, first, q, k, v, s), dyv, (dsb, pb) in zip(blocks, dys, grads):
                out_ref[0, pl.ds(q0, GRID_W), :] = (jnp.dot(dsb, k, preferred_element_type=F32) * SCALE).astype(BF)
                keys = pl.ds(k0, nk)
                dk_acc[keys, :] += lax.dot_general(dsb, q, (TN, ((), ())), preferred_element_type=F32) * SCALE
                dv_acc[keys, :] += lax.dot_general(pb, dyv, (TN, ((), ())), preferred_element_type=F32)
            return carry

        lax.fori_loop(0, rows // NBR_SIDE, step, 0)
        out_ref[1] = dk_acc[...].astype(BF)
        out_ref[2] = dv_acc[...].astype(BF)

    heads = pl.BlockSpec((3, S, HEAD_DIM), lambda h: (0, 0, N_HEADS_A + h))
    row = pl.BlockSpec((S, HEAD_DIM), lambda h: (0, h))
    table = pl.BlockSpec((None, RPB_ROWS - 1, GRID_W, 128), lambda h: (h, 0, 0, 0))
    return ORDER.call(
        body, [qkv3, e2, dy, y, lse, dqkv3],
        [heads, table, row, row, row, pl.BlockSpec(memory_space=pl.ANY)], name="attn_b_bwd", grid=(4,),
        out_specs=[heads, table],
        out_shape=[_sds((3, S, QKV_W), BF), _sds((4, RPB_ROWS - 1, GRID_W, 128), F32)],
        scratch_shapes=[pltpu.VMEM((S, HEAD_DIM), F32), pltpu.VMEM((S, HEAD_DIM), F32)],
        input_output_aliases={5: 0},
        compiler_params=_cparams(("parallel",)), chain_output=1,
    )


def _rpb_to_table(rpb):
    pad = jnp.pad(rpb, ((0, 0), (0, 0), (0, 1)))
    pairs = jnp.concatenate([pad[:, :-1], pad[:, 1:]], axis=-1).reshape(4 * (RPB_ROWS - 1), 64)
    onehot = jnp.asarray(_toeplitz_onehot())
    n = onehot.shape[1]
    tn = 2048
    full = lambda i, j, k: (0, 0)
    (e2,) = _matmul("rpb_table", pairs, onehot, pl.BlockSpec(pairs.shape, full),
                    pl.BlockSpec((64, tn), lambda i, j, k: (0, j)), NN, (1, n // tn, 1), (pairs.shape[0], tn), [],
                    [(_sds((pairs.shape[0], n), F32), pl.BlockSpec((pairs.shape[0], tn), lambda i, j, k: (0, j)))],
                    _store(F32), precision=lax.Precision.HIGHEST)
    return e2.reshape(4, RPB_ROWS - 1, GRID_W, 128)


def _table_grad_to_rpb(de2):
    onehot = jnp.asarray(_toeplitz_onehot())
    n = onehot.shape[1]
    flat = de2.reshape(4 * (RPB_ROWS - 1), n)
    tk = 2048
    (dpairs,) = _matmul("rpb_table_grad", flat, onehot, pl.BlockSpec((flat.shape[0], tk), lambda i, j, k: (0, k)),
                        pl.BlockSpec((64, tk), lambda i, j, k: (0, k)), NT, (1, 1, n // tk), (flat.shape[0], 64), [],
                        [(_sds((flat.shape[0], 64), F32), pl.BlockSpec((flat.shape[0], 64), lambda i, j, k: (0, 0)))],
                        _store(F32), precision=lax.Precision.HIGHEST)
    dpairs = dpairs.reshape(4, RPB_ROWS - 1, 64)
    zero = jnp.zeros((4, 1, RPB_COLS), F32)
    return (jnp.concatenate([dpairs[:, :, :RPB_COLS], zero], axis=1)
            + jnp.concatenate([zero, dpairs[:, :, 32:32 + RPB_COLS]], axis=1))


HBM = pl.BlockSpec(memory_space=pl.ANY)


def _place():
    x, y, c = lax.axis_index("x"), lax.axis_index("y"), lax.axis_index("c")
    chips = [(1 - x, y), (x, 1 - y), (1 - x, 1 - y)]
    return x, y, c, chips


def _remote(src, dst, send_sem, recv_sem, to):
    return pltpu.make_async_remote_copy(src_ref=src, dst_ref=dst, send_sem=send_sem, recv_sem=recv_sem,
                                        device_id=to, device_id_type=MESH)


def _place_shard(name, w, me):
    R, C = w.shape
    tr = _tile(R, 256)

    def body(me_ref, w_ref, o_ref):
        o_ref[...] = w_ref[...].astype(BF)

    return ORDER.call(
        body, [w], [pl.BlockSpec((tr, C), lambda i, mr: (i, 0))], prefetch=(me,), name=name, grid=(R // tr,),
        out_specs=pl.BlockSpec((None, tr, C), lambda i, mr: (mr[0], i, 0)),
        out_shape=_sds((N_CHIPS, R, C), BF), compiler_params=_cparams(("parallel",)),
    )


SEM = pl.BlockSpec(memory_space=pltpu.SEMAPHORE)
IN_HBM = pl.BlockSpec(memory_space=pltpu.HBM)
DATAFLOW = pltpu.SideEffectType.DATAFLOW_SIDE_EFFECTING


def _in_hbm(a):
    return pltpu.with_memory_space_constraint(a, pltpu.HBM)


def _copy_start(name, bufs, copies, n_copies, earlier=None):
    n = len(bufs)
    after = None if any(b is ORDER.last for b in bufs) else ORDER.last
    n_extra = (2 if earlier is not None else 0) + (1 if after is not None else 0)

    def body(*refs):
        ins = refs[:n]
        if earlier is not None:
            for k, (src, dst, to) in enumerate(earlier[0](ins)):
                cp = _remote(src, dst, refs[n].at[k], refs[n + 1].at[k], to)
                cp.wait_send()
                cp.wait_recv()
        send_sems, recv_sems = refs[n + n_extra], refs[n + n_extra + 1]
        for k, (src, dst, to) in enumerate(copies(ins)):
            _remote(src, dst, send_sems.at[k], recv_sems.at[k], to).start()
        refs[-1][...] = jnp.zeros((8, 128), F32)

    operands = [_in_hbm(b) for b in bufs]
    in_specs = [IN_HBM] * n
    if earlier is not None:
        operands += [earlier[1], earlier[2]]
        in_specs += [SEM, SEM]
    if after is not None:
        operands.append(after)
        in_specs.append(HBM)
    outs = pl.pallas_call(
        body, name=name,
        out_shape=(pltpu.SemaphoreType.DMA((n_copies,)), pltpu.SemaphoreType.DMA((n_copies,)),
                   *[pltpu.HBM(b.shape, b.dtype) for b in bufs], _sds((8, 128), F32)),
        in_specs=in_specs,
        out_specs=(SEM, SEM, *[IN_HBM] * n, pl.BlockSpec(memory_space=pltpu.VMEM)),
        input_output_aliases={i: 2 + i for i in range(n)},
        compiler_params=pltpu.CompilerParams(has_side_effects=DATAFLOW),
    )(*operands)
    ORDER.last = outs[-1]
    return outs[0], outs[1], list(outs[2:2 + n])


def _copy_wait(name, bufs, copies, send_sems, recv_sems):
    n = len(bufs)
    after = ORDER.last

    def body(*refs):
        ins = refs[:n]
        for k, (src, dst, to) in enumerate(copies(ins)):
            cp = _remote(src, dst, refs[n].at[k], refs[n + 1].at[k], to)
            cp.wait_send()
            cp.wait_recv()

    outs = list(pl.pallas_call(
        body, name=name,
        out_shape=tuple(pltpu.HBM(b.shape, b.dtype) for b in bufs),
        in_specs=[IN_HBM] * n + [SEM, SEM, HBM], out_specs=tuple([IN_HBM] * n),
        input_output_aliases={i: i for i in range(n)},
        compiler_params=pltpu.CompilerParams(has_side_effects=DATAFLOW),
    )(*bufs, send_sems, recv_sems, after))
    ORDER.last = outs[0]
    return outs


def _gather_hop1(bufs):
    x, y, c, chips = _place()
    out = []
    for b in bufs:
        half = b.shape[1] // 2
        mine = b.at[2 * x + y, pl.ds(c * half, half), :]
        out += [(mine, mine, (*chip, c)) for chip in chips]
    return out


def _gather_hop2(bufs):
    x, y, c, chips = _place()
    out = []
    for b in bufs:
        half = b.shape[1] // 2
        for chip in chips:
            landed = b.at[2 * chip[0] + chip[1], pl.ds(c * half, half), :]
            out.append((landed, landed, (x, y, 1 - c)))
    return out


def _swap_copies(bufs):
    x, y, c, _ = _place()
    n = len(bufs) // 2
    out = []
    for p, land in zip(bufs[:n], bufs[n:]):
        half = p.shape[1] // 2
        out.append((p.at[:, pl.ds((1 - c) * half, half), :], land, (x, y, 1 - c)))
    return out


def _scatter_copies(bufs):
    _, _, c, chips = _place()
    n = len(bufs) // 2
    out = []
    for s_, land in zip(bufs[:n], bufs[n:]):
        out += [(s_.at[2 * chip[0] + chip[1]], land.at[j], (*chip, c)) for j, chip in enumerate(chips)]
    return out


def _join_copies(bufs):
    x, y, c, _ = _place()
    out = []
    for b in bufs:
        half = b.shape[0] // 2
        mine = b.at[pl.ds(c * half, half), :]
        out.append((mine, mine, (x, y, 1 - c)))
    return out


def _gather_small(vec):
    m_per, n = vec.shape

    def body(x_ref, out_ref, send_sems, recv_sems, local_sem):
        x, y, c, chips = _place()
        me, sibling = (x, y, c), (x, y, 1 - c)

        def rows(px, py, pc):
            return out_ref.at[pl.ds((4 * px + 2 * py + pc) * m_per, m_per), :]

        def copy(k, block, to, src=None):
            return _remote(rows(*block) if src is None else src, rows(*block), send_sems.at[k], recv_sems.at[k], to)

        mine = pltpu.make_async_copy(x_ref, rows(*me), local_sem)
        mine.start()
        first = [copy(0, me, sibling, src=x_ref)]
        first += [copy(1 + j, me, (*chip, c), src=x_ref) for j, chip in enumerate(chips)]
        for cp in first:
            cp.start()
        passed = [copy(4 + j, (*chip, c), sibling) for j, chip in enumerate(chips)]
        for j, chip in enumerate(chips):
            copy(1 + j, (*chip, c), me).wait_recv()
            passed[j].start()
        copy(0, sibling, me).wait_recv()
        for j, chip in enumerate(chips):
            copy(4 + j, (*chip, 1 - c), me).wait_recv()
        for cp in first + passed:
            cp.wait_send()
        mine.wait()

    return ORDER.call(
        body, [vec], [pl.BlockSpec(memory_space=pltpu.VMEM)], name="gather_small_grads",
        out_shape=_sds((8 * m_per, n), vec.dtype), out_specs=pl.BlockSpec(memory_space=pltpu.VMEM),
        scratch_shapes=[pltpu.SemaphoreType.DMA((7,)), pltpu.SemaphoreType.DMA((7,)), pltpu.SemaphoreType.DMA],
    )


def _add_sibling(name, partial, received, c):
    _, R, C = partial.shape
    half = R // 2
    tr = _tile(half, 256)
    nb = half // tr

    def body(c_ref, p_ref, r_ref, o_ref):
        o_ref[...] = (p_ref[...].astype(F32) + r_ref[...].astype(F32)).astype(BF)

    return ORDER.call(
        body, [partial, received],
        [pl.BlockSpec((None, tr, C), lambda j, i, cr: (j, cr[0] * nb + i, 0)),
         pl.BlockSpec((None, tr, C), lambda j, i, cr: (j, i, 0))],
        prefetch=(c,), name=name, grid=(N_CHIPS, nb),
        out_specs=pl.BlockSpec((None, tr, C), lambda j, i, cr: (j, i, 0)),
        out_shape=_sds((N_CHIPS, half, C), BF), compiler_params=_cparams(("parallel", "parallel")),
    )


def _add_chips(name, sums, received, me_c):
    _, half, C = sums.shape
    tr = _tile(half, 256)
    nb = half // tr

    def body(mc_ref, s_ref, r_ref, o_ref):
        acc = s_ref[...].astype(F32)
        for j in range(3):
            acc = acc + r_ref[j].astype(F32)
        o_ref[...] = acc

    return ORDER.call(
        body, [sums, received],
        [pl.BlockSpec((None, tr, C), lambda i, mc: (mc[0], i, 0)),
         pl.BlockSpec((3, tr, C), lambda i, mc: (0, i, 0))],
        prefetch=(me_c,), name=name, grid=(nb,),
        out_specs=pl.BlockSpec((tr, C), lambda i, mc: (mc[1] * nb + i, 0)),
        out_shape=_sds((2 * half, C), F32), compiler_params=_cparams(("parallel",)),
    )


def _adamw_math(w, g, m, v):
    m = ADAM_B1 * m + (1.0 - ADAM_B1) * g
    v = ADAM_B2 * v + (1.0 - ADAM_B2) * (g * g)
    m_hat = m / (1.0 - ADAM_B1 ** ADAM_STEP)
    v_hat = v / (1.0 - ADAM_B2 ** ADAM_STEP)
    delta = -ADAM_LR * (m_hat / (jnp.sqrt(v_hat) + ADAM_EPS) + ADAM_WD * w)
    return delta, m, v


def _adamw(name, w, g, m, v):
    R, C = w.shape
    tr = _tile(R, 128)

    def body(w_ref, g_ref, m_ref, v_ref, go_ref, d_ref, mo_ref, vo_ref):
        gv = g_ref[...]
        go_ref[...] = gv
        d_ref[...], mo_ref[...], vo_ref[...] = _adamw_math(w_ref[...], gv, m_ref[...], v_ref[...])

    row = pl.BlockSpec((tr, C), lambda i: (i, 0))
    return ORDER.call(
        body, [w, g, m, v], [row] * 4, name=name, grid=(R // tr,), out_specs=[row] * 4,
        out_shape=[_sds((R, C), F32)] * 4, compiler_params=_cparams(("parallel",)), chain_output=1,
    )


def _adamw_small(gathered, w, m, v):
    rows, n = w.shape

    def body(ga_ref, w_ref, m_ref, v_ref, go_ref, d_ref, mo_ref, vo_ref):
        g = ga_ref[pl.ds(0, rows), :]
        for dev in range(1, 8):
            g = g + ga_ref[pl.ds(dev * rows, rows), :]
        go_ref[...] = g
        d_ref[...], mo_ref[...], vo_ref[...] = _adamw_math(w_ref[...], g, m_ref[...], v_ref[...])

    whole = pl.BlockSpec(memory_space=pltpu.VMEM)
    return ORDER.call(
        body, [gathered, w, m, v], [whole] * 4, name="adamw_small", out_specs=[whole] * 4,
        out_shape=[_sds((rows, n), F32)] * 4, compiler_params=_cparams(), chain_output=1,
    )


class _Exchange:
    GATHER = (("qkv",), ("gate", "proj_a", "proj_b", "out"), ("up", "down"))
    REDUCE = {"mlp": ("down", "up"), "mix": ("out", "proj_a", "proj_b"), "in": ("qkv", "gate")}

    def __init__(self, shards, me, c):
        self.me, self.c = me, c
        self.hop1, self.hop2, self.stage, self.grads = {}, {}, {}, {}
        for g, names in enumerate(self.GATHER):
            bufs = [_place_shard(f"place_{n}", shards[n], me) for n in names]
            self.hop1[g] = _copy_start(f"gather{g}_start", bufs, _gather_hop1, 3 * len(names))

    def forward(self, g):
        send, recv, thru = self.hop1.pop(g)
        self.hop2[g] = _copy_start(f"gather{g}_forward", thru, _gather_hop2, len(thru) * 3,
                                   earlier=(_gather_hop1, send, recv))

    def weights(self, g):
        send, recv, thru = self.hop2.pop(g)
        return _copy_wait(f"gather{g}_wait", thru, _gather_hop2, send, recv)

    def reduce(self, key, partials=None):
        names = self.REDUCE[key]
        n = len(names)
        if partials is not None:
            lands = [lax.empty((p.shape[0], p.shape[1] // 2, p.shape[2]), p.dtype) for p in partials]
            self.stage[key] = ("swap",) + _copy_start(f"reduce_{key}_swap", list(partials) + lands, _swap_copies, n)
            return
        kind, send, recv, thru = self.stage.pop(key)
        if kind == "swap":
            thru = _copy_wait(f"reduce_{key}_swap_wait", thru, _swap_copies, send, recv)
            sums = [_add_sibling(f"reduce_{nm}_add_sibling", p, r, self.c)
                    for nm, p, r in zip(names, thru[:n], thru[n:])]
            lands = [lax.empty((3,) + s_.shape[1:], s_.dtype) for s_ in sums]
            self.stage[key] = ("scatter",) + _copy_start(f"reduce_{key}_scatter", sums + lands, _scatter_copies, 3 * n)
        elif kind == "scatter":
            thru = _copy_wait(f"reduce_{key}_scatter_wait", thru, _scatter_copies, send, recv)
            me_c = jnp.concatenate([self.me, self.c])
            halves = [_add_chips(f"reduce_{nm}_add_chips", s_, r, me_c)
                      for nm, s_, r in zip(names, thru[:n], thru[n:])]
            self.stage[key] = ("join",) + _copy_start(f"reduce_{key}_join", halves, _join_copies, n)
        else:
            thru = _copy_wait(f"reduce_{key}_join_wait", thru, _join_copies, send, recv)
            self.grads.update(zip(names, thru))


def _forward_backward(x, target, norm_mix, b_gate, rpb, norm_mlp, norm_final, ex):
    S, D = x.shape

    h1 = _rms_fwd("rms_mix", x, norm_mix)
    ex.forward(0)
    e2 = _rpb_to_table(rpb)
    (gq,) = ex.weights(0)
    nq = QKV_W // 512
    (qkv3,), _ = _mm_nn_cols(
        "qkv", h1, gq, BF, tn=512,
        outs=[(_sds((3, S, QKV_W), BF), pl.BlockSpec((None, _tile(S, 1024), 512), lambda i, j, k: (j // nq, i, j % nq)))])

    ex.forward(1)
    outs_a = [_attn_a_fwd(qkv3, 0, DILATIONS[0])]
    gg, gpa, gpb, gout = ex.weights(1)
    wout = gout.reshape(D, D)

    tg = _tile(gg.shape[2], 1024)
    ng = D // tg

    def gate_epilogue(acc, ex_, outs):
        outs[0][...] = jax.nn.sigmoid(acc + ex_[0][...])

    (g3,), _ = _mm_nn_cols(
        "gate", h1, gg, F32, epilogue=gate_epilogue, tn=tg,
        extras=[(b_gate, pl.BlockSpec((1, tg), lambda i, j, k: (0, j)))],
        outs=[(_sds((2, S, D), F32), pl.BlockSpec((None, _tile(S, 1024), tg), lambda i, j, k: (j // ng, i, j % ng)))])

    outs_a += [_attn_a_fwd(qkv3, grp, d) for grp, d in enumerate(DILATIONS) if grp > 0]
    y_a, lj = _attn_a_combine([o for o, _ in outs_a], [l for _, l in outs_a])
    y_b, lse_b = _attn_b_fwd(qkv3, e2)

    (pa,), (tm, tp, _) = _mm_nn_cols("proj_a", y_a, gpa, F32, tn=512)

    def merge_epilogue(acc, ex_, outs):
        g = ex_[0][...]
        outs[0][...] = acc
        outs[1][...] = (g[0] * ex_[1][...] + g[1] * acc).astype(BF)

    tile = pl.BlockSpec((tm, tp), lambda i, j, k: (i, j))
    gates = pl.BlockSpec((2, tm, tp), lambda i, j, k: (0, i, j))
    (pb, merged), _ = _mm_nn_cols(
        "proj_b_merge", y_b, gpb, F32, epilogue=merge_epilogue, tn=512,
        extras=[(g3, gates), (pa, tile)],
        outs=[(_sds((S, D), F32), tile), (_sds((S, D), BF), tile)])

    def residual_epilogue(acc, ex_, outs):
        outs[0][...] = acc + ex_[0][...]

    def nn_plain(name, a, w, res):
        M, K = a.shape
        N = w.shape[1]
        bm, bn, bk = _tile(M, 1024), _tile(N, 1024), _tile(K, 2048)
        t = pl.BlockSpec((bm, bn), lambda i, j, k: (i, j))
        return _matmul(name, a, w, pl.BlockSpec((bm, bk), lambda i, j, k: (i, k)),
                       pl.BlockSpec((bk, bn), lambda i, j, k: (k, j)), NN, (M // bm, N // bn, K // bk), (bm, bn),
                       [(res, t)], [(_sds((M, N), F32), t)], residual_epilogue)[0]

    ex.forward(2)
    x1 = nn_plain("out_proj", merged, wout, x)
    h2 = _rms_fwd("rms_mlp", x1, norm_mlp)
    gup, gdown = ex.weights(2)
    F = gup.shape[2] * N_CHIPS
    wdown = gdown.reshape(F, D)

    def up_epilogue(acc, ex_, outs):
        ru = jnp.maximum(acc, 0.0)
        outs[0][...] = (ru * ru).astype(BF)
        outs[1][...] = ru.astype(BF)

    tu = _tile(gup.shape[2], 1024)
    ut = pl.BlockSpec((_tile(S, 1024), tu), lambda i, j, k: (i, j))
    (act, ru), _ = _mm_nn_cols("mlp_up", h2, gup, BF, epilogue=up_epilogue, tn=tu,
                               outs=[(_sds((S, F), BF), ut), (_sds((S, F), BF), ut)])
    x2 = nn_plain("mlp_down", act, wdown, x1)

    loss, dx2, dx2b, d_norm_final = _loss_head(x2, target, norm_final.reshape(1, D))

    def nt_rows(name, a, w, epilogue, extras, outs, bn=1024):
        M, N = a.shape
        K = w.shape[0]
        bm, bn, bk = _tile(M, 1024), _tile(K, bn), _tile(N, 2048)
        return _matmul(name, a, w, pl.BlockSpec((bm, bk), lambda i, j, k: (i, k)),
                       pl.BlockSpec((bn, bk), lambda i, j, k: (j, k)), NT, (M // bm, K // bn, N // bk), (bm, bn),
                       extras(bm, bn), outs(bm, bn), epilogue)

    def nt_cols(name, a_spec_fn, a, g, M, epilogue, extras, outs, bk):
        _, K, Nq = g.shape
        bm, bn, bk = _tile(M, 1024), _tile(K, 1024), _tile(Nq, bk)
        q = Nq // bk
        return _matmul(name, a, g, a_spec_fn(bm, bk), pl.BlockSpec((None, bn, bk), lambda i, j, k: (k // q, j, k % q)),
                       NT, (M // bm, K // bn, N_CHIPS * q), (bm, bn), extras(bm, bn), outs(bm, bn), epilogue)

    def tn_grad(name, a, a_spec_fn, b, b_spec_fn, Kin, N, out_shape, out_spec_fn, bn=1024):
        bm, bn, bk = _tile(Kin, 1024), _tile(N, bn), _tile(S, 2048)
        return _matmul(name, a, b, a_spec_fn(bk, bm), b_spec_fn(bk, bn), TN, (Kin // bm, N // bn, S // bk), (bm, bn),
                       [], [(_sds(out_shape, BF), out_spec_fn(bm, bn))], _store(BF))[0]

    plain_a = lambda bk, bm: pl.BlockSpec((bk, bm), lambda i, j, k: (k, i))
    plain_b = lambda bk, bn: pl.BlockSpec((bk, bn), lambda i, j, k: (k, j))
    plain_o = lambda bm, bn: pl.BlockSpec((bm, bn), lambda i, j, k: (i, j))
    a_rows = lambda bm, bk: pl.BlockSpec((bm, bk), lambda i, j, k: (i, k))

    def cols_o(Nq):
        def spec(bm, bn):
            q = Nq // bn
            return pl.BlockSpec((None, bm, bn), lambda i, j, k: (j // q, i, j % q))
        return spec

    def du_epilogue(acc, ex_, outs):
        outs[0][...] = (acc * (2.0 * ex_[0][...].astype(F32))).astype(BF)

    dw_down = tn_grad("mlp_down_dw", act, plain_a, dx2b, plain_b, F, D, (F, D), plain_o)
    (du,) = nt_rows("mlp_down_dx", dx2b, wdown, du_epilogue,
                    lambda bm, bn: [(ru, plain_o(bm, bn))], lambda bm, bn: [(_sds((S, F), BF), plain_o(bm, bn))])

    fq = gup.shape[2]
    dw_up = tn_grad("mlp_up_dw", h2, plain_a, du, plain_b, D, F, (N_CHIPS, D, fq), cols_o(fq), bn=min(fq, 1024))
    ex.reduce("mlp", partials=[dw_down.reshape(N_CHIPS, F // N_CHIPS, D), dw_up])
    (dh2,) = nt_cols("mlp_up_dx", a_rows, du, gup, S, _store(F32), lambda bm, bn: [],
                     lambda bm, bn: [(_sds((S, D), F32), plain_o(bm, bn))], 2048)
    ex.reduce("mlp")
    dx1, dx1b, d_norm_mlp = _rms_bwd("rms_mlp_bwd", dh2, x1, norm_mlp, dx2)

    def merge_bwd_epilogue(acc, ex_, outs):
        g, pav, pbv = ex_[0][...], ex_[1][...], ex_[2][...]
        outs[0][...] = (acc * g[0]).astype(BF)
        outs[1][...] = (acc * g[1]).astype(BF)
        dga = acc * pav * g[0] * (1.0 - g[0])
        dgb = acc * pbv * g[1] * (1.0 - g[1])
        outs[2][0] = dga.astype(BF)
        outs[2][1] = dgb.astype(BF)
        outs[3][...] = jnp.concatenate([jnp.sum(dga, axis=0, keepdims=True), jnp.sum(dgb, axis=0, keepdims=True)], 0)

    def pair(bm, bn):
        return pl.BlockSpec((2, bm, bn), lambda i, j, k: (0, i, j))

    n_row_blocks = S // _tile(S, 1024)
    dpa, dpb, dg3, db_gate = nt_rows(
        "out_proj_dx", dx1b, wout, merge_bwd_epilogue,
        lambda bm, bn: [(g3, pair(bm, bn)), (pa, plain_o(bm, bn)), (pb, plain_o(bm, bn))],
        lambda bm, bn: [(_sds((S, D), BF), plain_o(bm, bn)), (_sds((S, D), BF), plain_o(bm, bn)),
                        (_sds((2, S, D), BF), pair(bm, bn)),
                        (_sds((n_row_blocks, 2, D), F32), pl.BlockSpec((None, 2, bn), lambda i, j, k: (i, 0, j)))],
        bn=512)
    dw_out = tn_grad("out_proj_dw", merged, plain_a, dx1b, plain_b, D, D, (D, D), plain_o)

    pq = gpa.shape[2]
    proj_dx = lambda name, dproj, g: nt_cols(name, a_rows, dproj, g, S, _store(BF), lambda bm, bn: [],
                                             lambda bm, bn: [(_sds((S, 512), BF), plain_o(bm, bn))], 512)[0]
    dw_pa = tn_grad("proj_a_dw", y_a, plain_a, dpa, plain_b, 512, D, (N_CHIPS, 512, pq), cols_o(pq), bn=min(pq, 512))
    dw_pb = tn_grad("proj_b_dw", y_b, plain_a, dpb, plain_b, 512, D, (N_CHIPS, 512, pq), cols_o(pq), bn=min(pq, 512))
    ex.reduce("mix", partials=[dw_out.reshape(N_CHIPS, D // N_CHIPS, D), dw_pa, dw_pb])
    dy_a = proj_dx("proj_a_dx", dpa, gpa)
    dy_b = proj_dx("proj_b_dx", dpb, gpb)

    dqkv3 = lax.empty((3, S, QKV_W), BF)
    dqkv3 = _attn_a_bwd(qkv3, dy_a, y_a, lj, dqkv3, 0, DILATIONS[0])
    ex.reduce("mix")
    for grp, d in enumerate(DILATIONS):
        if grp > 0:
            dqkv3 = _attn_a_bwd(qkv3, dy_a, y_a, lj, dqkv3, grp, d)
    dqkv3, de2 = _attn_b_bwd(qkv3, e2, dy_b, y_b, lse_b, dqkv3)
    d_rpb = _table_grad_to_rpb(de2)

    def stacked_a(width):
        def spec(bm, bk):
            q = width // bk
            return pl.BlockSpec((None, bm, bk), lambda i, j, k: (k // q, i, k % q))
        return spec

    def stacked_b(width):
        def spec(bk, bn):
            q = width // bn
            return pl.BlockSpec((None, bk, bn), lambda i, j, k: (j // q, k, j % q))
        return spec

    dw_qkv = tn_grad("qkv_dw", h1, plain_a, dqkv3, stacked_b(QKV_W), D, 3 * QKV_W, (N_CHIPS,) + gq.shape[1:],
                     cols_o(gq.shape[2]), bn=512)
    dw_gate = tn_grad("gate_dw", h1, plain_a, dg3, stacked_b(D), D, 2 * D, (N_CHIPS,) + gg.shape[1:],
                      cols_o(gg.shape[2]), bn=gg.shape[2])
    ex.reduce("in", partials=[dw_qkv, dw_gate])
    ex.reduce("mlp")
    (dh1_q,) = nt_cols("qkv_dx", stacked_a(QKV_W), dqkv3, gq, S, _store(F32), lambda bm, bn: [],
                       lambda bm, bn: [(_sds((S, D), F32), plain_o(bm, bn))], 512)
    ex.reduce("in")
    ex.reduce("mix")

    def add_epilogue(acc, ex_, outs):
        outs[0][...] = acc + ex_[0][...]

    (dh1,) = nt_cols("gate_dx", stacked_a(D), dg3, gg, S, add_epilogue, lambda bm, bn: [(dh1_q, plain_o(bm, bn))],
                     lambda bm, bn: [(_sds((S, D), F32), plain_o(bm, bn))], gg.shape[2])
    grad_x, _, d_norm_mix = _rms_bwd("rms_mix_bwd", dh1, x, norm_mix, dx1)
    ex.reduce("mlp")
    ex.reduce("mix")

    small = [d_norm_mix, jnp.sum(db_gate, axis=0).reshape(1, 2 * D), d_rpb, d_norm_mlp, d_norm_final]
    return loss, grad_x, small


def _pack_small(parts, width):
    flat = jnp.concatenate([p.reshape(-1) for p in parts])
    return jnp.pad(flat, (0, 8 * width - flat.shape[0])).reshape(8, width)


def kernel(x, norm_mix, w_qkv, w_gate, b_gate, rpb, w_proj_a, w_proj_b, w_out, norm_mlp, w_up, w_down, norm_final, loss_target, m_norm_mix, m_w_qkv, m_w_gate, m_b_gate, m_rpb, m_w_proj_a, m_w_proj_b, m_w_out, m_norm_mlp, m_w_up, m_w_down, m_norm_final, v_norm_mix, v_w_qkv, v_w_gate, v_b_gate, v_rpb, v_w_proj_a, v_w_proj_b, v_w_out, v_norm_mlp, v_w_up, v_w_down, v_norm_final):
    names = ["qkv", "gate", "proj_a", "proj_b", "out", "up", "down"]
    big = dict(zip(names, [w_qkv[0], w_gate[0], w_proj_a[0], w_proj_b[0], w_out[0], w_up[0], w_down[0]]))
    big_m = dict(zip(names, [m_w_qkv[0], m_w_gate[0], m_w_proj_a[0], m_w_proj_b[0], m_w_out[0], m_w_up[0], m_w_down[0]]))
    big_v = dict(zip(names, [v_w_qkv[0], v_w_gate[0], v_w_proj_a[0], v_w_proj_b[0], v_w_out[0], v_w_up[0], v_w_down[0]]))

    c = lax.axis_index("c").astype(jnp.int32).reshape(1)
    me = (2 * lax.axis_index("x") + lax.axis_index("y")).astype(jnp.int32).reshape(1)
    ORDER.last = None
    ex = _Exchange(big, me, c)
    loss, grad_x, small = _forward_backward(x[0], loss_target[0], norm_mix, b_gate, rpb[0], norm_mlp, norm_final, ex)

    def adamw(group):
        return {n: _adamw(f"adamw_{n}", big[n], ex.grads[n], big_m[n], big_v[n]) for n in _Exchange.REDUCE[group]}

    big_out = {**adamw("mlp"), **adamw("mix")}
    ex.reduce("in")

    small_w = [norm_mix, b_gate, rpb, norm_mlp, norm_final]
    count = sum(int(np.prod(p.shape)) for p in small_w)
    width = -(-count // (8 * 128)) * 128
    packed = _adamw_small(_gather_small(_pack_small(small, width)), _pack_small(small_w, width),
                          _pack_small([m_norm_mix, m_b_gate, m_rpb, m_norm_mlp, m_norm_final], width),
                          _pack_small([v_norm_mix, v_b_gate, v_rpb, v_norm_mlp, v_norm_final], width))
    ex.reduce("in")
    big_out.update(adamw("in"))

    def unpack(flat2d):
        flat, out, at = flat2d.reshape(-1), [], 0
        for p in small_w:
            size = int(np.prod(p.shape))
            out.append(flat[at:at + size].reshape(p.shape))
            at += size
        return out

    small_out = [unpack(a) for a in packed]

    def ordered(kind):
        sm = small_out[kind]
        bg = {n: o[kind][None] for n, o in big_out.items()}
        return [sm[0], bg["qkv"], bg["gate"], sm[1], sm[2], bg["proj_a"], bg["proj_b"], bg["out"], sm[3],
                bg["up"], bg["down"], sm[4]]

    total = lax.psum(loss[0, 0], ("x", "y", "c"))
    return (total, grad_x[None], *ordered(0), *ordered(1), *ordered(2), *ordered(3))
```

```python
import functools
import math

import numpy as np
import jax
import jax.numpy as jnp
from jax import lax
from jax.experimental import pallas as pl
from jax.experimental.pallas import tpu as pltpu

BF = jnp.bfloat16
F32 = jnp.float32
MESH = pl.DeviceIdType.MESH

HEAD_DIM = 128
N_HEADS = 16
N_HEADS_A = 12
QKV_W = N_HEADS * HEAD_DIM
DILATIONS = (1, 4, 16)
HALF_WINDOW = 64
GRID_W = 64
NA_ROWS = 8
NA_COLS = 16
RPB_ROWS = 2 * NA_ROWS - 1
RPB_COLS = 2 * NA_COLS - 1
EPS = 1e-6
NEG = -1e30
SCALE = HEAD_DIM ** -0.5

ADAM_LR = 0.001
ADAM_B1 = 0.9
ADAM_B2 = 0.999
ADAM_EPS = 1e-08
ADAM_WD = 0.01
ADAM_STEP = 10

N_CHIPS = 4
VMEM_LIMIT_BYTES = 48 * 1024 * 1024
QB = 256
NBR_SIDE = 4


def _key_rows(L):
    return min(QB + 2 * HALF_WINDOW, L)


def _cparams(sem=None):
    return pltpu.CompilerParams(dimension_semantics=sem, vmem_limit_bytes=VMEM_LIMIT_BYTES)


def _tile(dim, want):
    t = min(dim, want)
    assert dim % t == 0, (dim, want)
    return t


class _ProgramOrder:
    def __init__(self):
        self.last = None

    def call(self, body, operands, in_specs, *, prefetch=(), grid=None, out_specs=None, chain_output=0, **kwargs):
        operands, in_specs = list(operands), list(in_specs)
        lead = len(prefetch) + len(operands)
        if self.last is not None and not any(op is self.last for op in operands):
            operands.append(self.last)
            in_specs.append(pl.BlockSpec(memory_space=pl.ANY))
            inner = body

            def body(*refs):
                return inner(*refs[:lead], *refs[lead + 1:])

        if prefetch:
            kwargs["grid_spec"] = pltpu.PrefetchScalarGridSpec(
                num_scalar_prefetch=len(prefetch), grid=grid, in_specs=in_specs, out_specs=out_specs)
        else:
            kwargs.update(in_specs=in_specs, out_specs=out_specs)
            if grid is not None:
                kwargs["grid"] = grid
        out = pl.pallas_call(body, **kwargs)(*prefetch, *operands)
        self.last = out[chain_output] if isinstance(out, (tuple, list)) else out
        return out


ORDER = _ProgramOrder()


NN = ((1,), (0,))
NT = ((1,), (1,))
TN = ((0,), (0,))


def _matmul(name, a, b, a_spec, b_spec, dims, grid, acc_shape, extras, outs, epilogue, precision=None):
    n_ex, n_out, nk = len(extras), len(outs), grid[2]

    def body(*refs):
        a_ref, b_ref = refs[0], refs[1]
        ex_refs = refs[2:2 + n_ex]
        out_refs = refs[2 + n_ex:2 + n_ex + n_out]

        def dot():
            return lax.dot_general(a_ref[...], b_ref[...], (dims, ((), ())),
                                   preferred_element_type=F32, precision=precision)

        if nk == 1:
            epilogue(dot(), ex_refs, out_refs)
            return
        acc_ref = refs[-1]
        k = pl.program_id(2)

        @pl.when(k == 0)
        def _():
            acc_ref[...] = dot()

        if nk > 2:
            @pl.when((k > 0) & (k < nk - 1))
            def _():
                acc_ref[...] += dot()

        @pl.when(k == nk - 1)
        def _():
            epilogue(acc_ref[...] + dot(), ex_refs, out_refs)

    return ORDER.call(
        body, [a, b] + [e for e, _ in extras], [a_spec, b_spec] + [s for _, s in extras], name=name, grid=grid,
        out_specs=[s for _, s in outs],
        out_shape=[sh for sh, _ in outs],
        scratch_shapes=[pltpu.VMEM(acc_shape, F32)] if nk > 1 else [],
        compiler_params=_cparams(("parallel", "parallel", "arbitrary")),
    )


def _store(dtype):
    def epilogue(acc, ex, outs):
        outs[0][...] = acc.astype(dtype)
    return epilogue


def _sds(shape, dtype):
    return jax.ShapeDtypeStruct(shape, dtype)


def _mm_nn_cols(name, a, g, out_dtype, epilogue=None, extras=(), outs=None, tm=1024, tn=1024, tk=2048):
    M, K = a.shape
    _, _, Nq = g.shape
    tm, tn, tk = _tile(M, tm), _tile(Nq, tn), _tile(K, tk)
    q = Nq // tn
    grid = (M // tm, N_CHIPS * q, K // tk)
    if outs is None:
        outs = [(_sds((M, N_CHIPS * Nq), out_dtype), pl.BlockSpec((tm, tn), lambda i, j, k: (i, j)))]
    return _matmul(name, a, g, pl.BlockSpec((tm, tk), lambda i, j, k: (i, k)),
                   pl.BlockSpec((None, tk, tn), lambda i, j, k: (j // q, k, j % q)), NN, grid, (tm, tn),
                   list(extras), outs, epilogue or _store(out_dtype)), (tm, tn, tk)


def _rms_fwd(name, x, g):
    S, D = x.shape
    tm = _tile(S, 256)

    def body(x_ref, g_ref, h_ref):
        xv = x_ref[...]
        r = lax.rsqrt(jnp.mean(xv * xv, axis=-1, keepdims=True) + EPS)
        h_ref[...] = ((xv * r) * g_ref[...]).astype(BF)

    row = pl.BlockSpec((tm, D), lambda i: (i, 0))
    return ORDER.call(
        body, [x, g], [row, pl.BlockSpec((1, D), lambda i: (0, 0))], name=name, grid=(S // tm,),
        out_specs=row, out_shape=_sds((S, D), BF), compiler_params=_cparams(("parallel",)),
    )


def _rms_bwd(name, dh, x, g, dres):
    S, D = x.shape
    tm = _tile(S, 256)

    def body(dh_ref, x_ref, g_ref, dres_ref, dx_ref, dxb_ref, dg_ref):
        xv = x_ref[...]
        r = lax.rsqrt(jnp.mean(xv * xv, axis=-1, keepdims=True) + EPS)
        n = xv * r
        dhv = dh_ref[...]
        dyg = dhv * g_ref[...]
        dx = dres_ref[...] + r * (dyg - n * jnp.mean(dyg * n, axis=-1, keepdims=True))
        dx_ref[...] = dx
        dxb_ref[...] = dx.astype(BF)

        @pl.when(pl.program_id(0) == 0)
        def _():
            dg_ref[...] = jnp.zeros_like(dg_ref)

        dg_ref[...] += jnp.sum(dhv * n, axis=0, keepdims=True)

    row = pl.BlockSpec((tm, D), lambda i: (i, 0))
    vec = pl.BlockSpec((1, D), lambda i: (0, 0))
    return ORDER.call(
        body, [dh, x, g, dres], [row, row, vec, row], name=name, grid=(S // tm,),
        out_specs=[row, row, vec],
        out_shape=[_sds((S, D), F32), _sds((S, D), BF), _sds((1, D), F32)],
        compiler_params=_cparams(("arbitrary",)),
    )


def _loss_head(x2, target, g):
    S, D = x2.shape
    tm = _tile(S, 256)

    def body(x_ref, t_ref, g_ref, loss_ref, dx_ref, dxb_ref, dg_ref):
        xv = x_ref[...]
        gv = g_ref[...]
        r = lax.rsqrt(jnp.mean(xv * xv, axis=-1, keepdims=True) + EPS)
        n = xv * r
        e = n * gv - t_ref[...]
        dy = e * (1.0 / D)
        dyg = dy * gv
        dx = r * (dyg - n * jnp.mean(dyg * n, axis=-1, keepdims=True))
        dx_ref[...] = dx
        dxb_ref[...] = dx.astype(BF)

        @pl.when(pl.program_id(0) == 0)
        def _():
            dg_ref[...] = jnp.zeros_like(dg_ref)
            loss_ref[...] = jnp.zeros_like(loss_ref)

        dg_ref[...] += jnp.sum(dy * n, axis=0, keepdims=True)
        per_row = jnp.mean(e * e, axis=-1, keepdims=True)
        loss_ref[...] += 0.5 * jnp.sum(per_row, axis=0, keepdims=True)

    row = pl.BlockSpec((tm, D), lambda i: (i, 0))
    vec = pl.BlockSpec((1, D), lambda i: (0, 0))
    return ORDER.call(
        body, [x2, target, g], [row, row, vec], name="loss_head", grid=(S // tm,),
        out_specs=[pl.BlockSpec((1, 1), lambda i: (0, 0)), row, row, vec],
        out_shape=[_sds((1, 1), F32), _sds((S, D), F32), _sds((S, D), BF), _sds((1, D), F32)],
        compiler_params=_cparams(("arbitrary",)), chain_output=1,
    )


def _chains(L):
    side = min(4, L // QB)
    return side, 4 // side


def _band_scores(qkv_ref, i, L, coef, head):
    KB = _key_rows(L)
    lanes = pl.ds(head * HEAD_DIM, HEAD_DIM)
    q0 = pl.multiple_of(i * QB, QB)
    ks = pl.multiple_of(jnp.clip(i * QB - HALF_WINDOW, 0, L - KB), HALF_WINDOW)
    q = qkv_ref[0, pl.ds(q0, QB), lanes]
    k = qkv_ref[1, pl.ds(ks, KB), lanes]
    v = qkv_ref[2, pl.ds(ks, KB), lanes]
    s = lax.dot_general(q, k, (NT, ((), ())), preferred_element_type=F32) * SCALE
    qpos = q0 + lax.broadcasted_iota(jnp.int32, (QB, KB), 0)
    kpos = ks + lax.broadcasted_iota(jnp.int32, (QB, KB), 1)
    rel = jnp.abs(kpos - qpos)
    valid = rel <= HALF_WINDOW
    s = jnp.where(valid, s - coef * rel.astype(F32), NEG)
    return q0, ks, q, k, v, s, valid


def _alibi_coefs(group, d, heads):
    first = 4 * group + 1 + pl.program_id(1) * heads
    scale = jnp.full((1, 1), -(8.0 / N_HEADS_A) * math.log(2.0), F32)
    return [jnp.exp(scale * (first + hh).astype(F32)) * float(d) for hh in range(heads)]


def _dilated_view(qkv3, group, d, heads):
    _, S, _ = qkv3.shape
    L = S // d
    per = 4 // heads
    if d == 1:
        return qkv3, pl.BlockSpec((3, L, heads * HEAD_DIM), lambda r, j: (0, 0, per * group + j))
    cols = qkv3[:, :, 512 * group:512 * (group + 1)].reshape(3, L, d * 512)
    return cols, pl.BlockSpec((3, L, heads * HEAD_DIM), lambda r, j: (0, 0, r * per + j))


def _attn_a_fwd(qkv3, group, d):
    _, S, _ = qkv3.shape
    L = S // d
    assert L % QB == 0
    side, heads = _chains(L)
    view, blocks_spec = _dilated_view(qkv3, group, d, heads)

    def body(qkv_ref, o_ref, lse_ref):
        coefs = _alibi_coefs(group, d, heads)

        def step(i, carry):
            chains = [(hh, _band_scores(qkv_ref, side * i + u, L, coefs[hh], hh))
                      for u in range(side) for hh in range(heads)]
            soft = []
            for hh, (q0, _, _, _, v, s, _) in chains:
                m = jnp.max(s, axis=-1, keepdims=True)
                p = jnp.exp(s - m)
                den = jnp.sum(p, axis=-1, keepdims=True)
                soft.append((hh, q0, (p / den).astype(BF), v, m + jnp.log(den)))
            for hh, q0, pn, v, lse in soft:
                lanes = pl.ds(hh * HEAD_DIM, HEAD_DIM)
                o_ref[pl.ds(q0, QB), lanes] = jnp.dot(pn, v, preferred_element_type=F32)
                lse_ref[pl.ds(q0, QB), lanes] = jnp.broadcast_to(lse, (QB, HEAD_DIM))
            return carry

        lax.fori_loop(0, L // QB // side, step, 0)

    per = 4 // heads
    out = pl.BlockSpec((L, heads * HEAD_DIM), lambda r, j: (0, r * per + j))
    o, lse = ORDER.call(
        body, [view], [blocks_spec],
        name=f"attn_a_fwd_d{d}", grid=(d, per),
        out_specs=[out, out],
        out_shape=[_sds((L, d * 512), F32), _sds((L, d * 512), F32)],
        compiler_params=_cparams(("parallel", "parallel")),
    )
    return o, lse


def _dilated_rows(name, arrays):
    S, W = arrays[0].shape
    tm = _tile(S, 256)
    ds_ = [d for d in DILATIONS if d > 1]
    n = len(arrays)

    def body(*refs):
        nc = W // 128
        ins, outs, scr = refs[:n], refs[n:-nc], refs[-nc:]
        for a, src in enumerate(ins):
            for c in range(nc):
                scr[c][...] = src[:, c * 128:(c + 1) * 128].astype(F32)
            for k, d in enumerate(ds_):
                dst = outs[a * len(ds_) + k]
                for r in range(d):
                    for c in range(nc):
                        at = r * W + c * 128
                        dst[:, at:at + 128] = scr[c][pl.ds(r, tm // d, stride=d), :].astype(dst.dtype)

    row = pl.BlockSpec((tm, W), lambda i: (i, 0))
    out_specs, out_shape = [], []
    for a in arrays:
        for d in ds_:
            out_specs.append(pl.BlockSpec((tm // d, d * W), lambda i: (i, 0)))
            out_shape.append(_sds((S // d, d * W), a.dtype))
    outs = ORDER.call(body, list(arrays), [row] * n, name=name, grid=(S // tm,), out_specs=out_specs,
                      out_shape=out_shape, scratch_shapes=[pltpu.VMEM((tm, 128), F32)] * (W // 128),
                      compiler_params=_cparams(("parallel",)))
    return [{d: outs[a * len(ds_) + k] for k, d in enumerate(ds_)} for a in range(n)]


def _attn_a_combine(os_, lses):
    W = 512
    S = os_[0].shape[0] * DILATIONS[0]
    tm = _tile(S, 256)
    nc = W // 128
    dilated = [g for g, d in enumerate(DILATIONS) if d > 1]

    def body(o0, o1, o2, l0, l1, l2, y_ref, lj_ref, *scr):
        def token_order(src, g, slot):
            d = DILATIONS[g]
            if d == 1:
                return src[...]
            bufs = scr[slot * nc:(slot + 1) * nc]
            for r in range(d):
                for c in range(nc):
                    at = r * W + c * 128
                    bufs[c][pl.ds(r, tm // d, stride=d), :] = src[:, at:at + 128]
            return jnp.concatenate([buf[...] for buf in bufs], axis=1)

        slots = {g: k for k, g in enumerate(dilated)}
        ls = [token_order(l, g, slots.get(g, 0)) for g, l in enumerate((l0, l1, l2))]
        os_tok = [token_order(o, g, len(dilated) + slots.get(g, 0)) for g, o in enumerate((o0, o1, o2))]
        m = jnp.maximum(jnp.maximum(ls[0], ls[1]), ls[2])
        es = [jnp.exp(l - m) for l in ls]
        den = es[0] + es[1] + es[2]
        y = (es[0] / den) * os_tok[0] + (es[1] / den) * os_tok[1] + (es[2] / den) * os_tok[2]
        y_ref[...] = y.astype(BF)
        lj_ref[...] = m + jnp.log(den)

    row = pl.BlockSpec((tm, W), lambda i: (i, 0))
    views = [pl.BlockSpec((tm // d, d * W), lambda i: (i, 0)) for d in DILATIONS]
    return ORDER.call(
        body, [*os_, *lses], views + views, name="attn_a_combine", grid=(S // tm,), out_specs=[row, row],
        out_shape=[_sds((S, W), BF), _sds((S, W), F32)],
        scratch_shapes=[pltpu.VMEM((tm, 128), F32)] * (2 * len(dilated) * nc),
        compiler_params=_cparams(("parallel",)),
    )


def _attn_a_bwd(qkv3, dy, y, lj, dqkv3, group, d):
    _, S, _ = qkv3.shape
    L = S // d
    side, heads = _chains(L)
    view, blocks_spec = _dilated_view(qkv3, group, d, heads)

    def body(qkv_ref, dy_ref, y_ref, lj_ref, *rest):
        out_ref, dk_acc, dv_acc = rest[-3:]
        coefs = _alibi_coefs(group, d, heads)
        dk_acc[...] = jnp.zeros_like(dk_acc)
        dv_acc[...] = jnp.zeros_like(dv_acc)

        def step(i, carry):
            chains = [(pl.ds(hh * HEAD_DIM, HEAD_DIM), _band_scores(qkv_ref, side * i + u, L, coefs[hh], hh))
                      for u in range(side) for hh in range(heads)]
            dys = [dy_ref[pl.ds(c[0], QB), lanes] for lanes, c in chains]
            dps = [lax.dot_general(dyv, c[4], (NT, ((), ())), preferred_element_type=F32)
                   for dyv, (_, c) in zip(dys, chains)]
            grads = []
            for (lanes, (q0, ks, q, k, v, s, valid)), dyv, dp in zip(chains, dys, dps):
                rows = pl.ds(q0, QB)
                delta = jnp.sum(dyv.astype(F32) * y_ref[rows, lanes].astype(F32), axis=-1, keepdims=True)
                p = jnp.where(valid, jnp.exp(s - jnp.tile(lj_ref[rows, lanes], (1, _key_rows(L) // HEAD_DIM))), 0.0)
                grads.append(((p * (dp - delta)).astype(BF), p.astype(BF)))
            for (lanes, (q0, ks, q, k, v, s, valid)), dyv, (ds, pb) in zip(chains, dys, grads):
                out_ref[0, pl.ds(q0, QB), lanes] = (jnp.dot(ds, k, preferred_element_type=F32) * SCALE).astype(BF)
                keys = pl.ds(ks, _key_rows(L))
                dk_acc[keys, lanes] += lax.dot_general(ds, q, (TN, ((), ())), preferred_element_type=F32) * SCALE
                dv_acc[keys, lanes] += lax.dot_general(pb, dyv, (TN, ((), ())), preferred_element_type=F32)
            return carry

        lax.fori_loop(0, L // QB // side, step, 0)
        out_ref[1] = dk_acc[...].astype(BF)
        out_ref[2] = dv_acc[...].astype(BF)

    per = 4 // heads
    width = heads * HEAD_DIM
    row = pl.BlockSpec((L, width), lambda r, j: (0, r * per + j))
    operands = [view, dy, y, lj]
    scratch = [pltpu.VMEM((L, width), F32), pltpu.VMEM((L, width), F32)]
    if d == 1:
        return ORDER.call(
            body, operands + [dqkv3], [blocks_spec, row, row, row, pl.BlockSpec(memory_space=pl.ANY)],
            name=f"attn_a_bwd_d{d}", grid=(d, per), out_specs=blocks_spec, out_shape=_sds((3, S, QKV_W), BF),
            scratch_shapes=scratch, input_output_aliases={4: 0}, compiler_params=_cparams(("parallel", "parallel")))
    out = ORDER.call(
        body, operands, [blocks_spec, row, row, row], name=f"attn_a_bwd_d{d}", grid=(d, per),
        out_specs=blocks_spec, out_shape=_sds((3, L, d * 512), BF),
        scratch_shapes=scratch, compiler_params=_cparams(("parallel", "parallel")))
    return lax.dynamic_update_slice(dqkv3, out.reshape(3, S, 512), (0, 0, 512 * group))


def _toeplitz_onehot():
    oh = np.zeros((64, GRID_W, 128), np.float32)
    for qc in range(GRID_W):
        for m in range(128):
            kc = m % GRID_W
            dc = int(np.clip(kc - qc, -(NA_COLS - 1), NA_COLS - 1)) + NA_COLS - 1
            oh[(m // GRID_W) * 32 + dc, qc, m] = 1.0
    return oh.reshape(64, GRID_W * 128)


def _nbr_scores(qkv_ref, e2_ref, r, rows, ok):
    rs = jnp.clip(r - NA_ROWS // 2, 0, rows - NA_ROWS)
    q0 = pl.multiple_of(r * GRID_W, GRID_W)
    k0 = pl.multiple_of(rs * GRID_W, GRID_W)
    q = qkv_ref[0, pl.ds(q0, GRID_W), :]
    k = qkv_ref[1, pl.ds(k0, NA_ROWS * GRID_W), :]
    v = qkv_ref[2, pl.ds(k0, NA_ROWS * GRID_W), :]
    s = lax.dot_general(q, k, (NT, ((), ())), preferred_element_type=F32) * SCALE
    first = rs - r + NA_ROWS - 1
    bias = jnp.concatenate([e2_ref[first + 2 * pair] for pair in range(NA_ROWS // 2)], axis=1)
    s = jnp.where(ok, s + bias, NEG)
    return q0, k0, first, q, k, v, s


def _nbr_col_ok():
    qc = lax.broadcasted_iota(jnp.int32, (GRID_W, NA_ROWS * GRID_W), 0)
    kc = lax.broadcasted_iota(jnp.int32, (GRID_W, NA_ROWS * GRID_W), 1) % GRID_W
    cs = jnp.clip(qc - NA_COLS // 2, 0, GRID_W - NA_COLS)
    return (kc >= cs) & (kc < cs + NA_COLS)


def _attn_b_fwd(qkv3, e2):
    _, S, _ = qkv3.shape
    rows = S // GRID_W
    assert rows >= NA_ROWS

    def body(qkv_ref, e2_ref, o_ref, lse_ref):
        ok = _nbr_col_ok()

        def step(i, carry):
            blocks = [_nbr_scores(qkv_ref, e2_ref, NBR_SIDE * i + u, rows, ok) for u in range(NBR_SIDE)]
            soft = []
            for q0, _, _, _, _, v, s in blocks:
                m = jnp.max(s, axis=-1, keepdims=True)
                p = jnp.exp(s - m)
                den = jnp.sum(p, axis=-1, keepdims=True)
                soft.append((q0, (p / den).astype(BF), v, m + jnp.log(den)))
            for q0, pn, v, lse in soft:
                o_ref[pl.ds(q0, GRID_W), :] = jnp.dot(pn, v, preferred_element_type=F32).astype(BF)
                lse_ref[pl.ds(q0, GRID_W), :] = jnp.broadcast_to(lse, (GRID_W, HEAD_DIM))
            return carry

        lax.fori_loop(0, rows // NBR_SIDE, step, 0)

    out = pl.BlockSpec((S, HEAD_DIM), lambda h: (0, h))
    return ORDER.call(
        body, [qkv3, e2],
        [pl.BlockSpec((3, S, HEAD_DIM), lambda h: (0, 0, N_HEADS_A + h)),
         pl.BlockSpec((None, RPB_ROWS - 1, GRID_W, 128), lambda h: (h, 0, 0, 0))],
        name="attn_b_fwd", grid=(4,),
        out_specs=[out, out], out_shape=[_sds((S, 512), BF), _sds((S, 512), F32)],
        compiler_params=_cparams(("parallel",)),
    )


def _attn_b_bwd(qkv3, e2, dy, y, lse, dqkv3):
    _, S, _ = qkv3.shape
    rows = S // GRID_W
    nk = NA_ROWS * GRID_W

    def body(qkv_ref, e2_ref, dy_ref, y_ref, lse_ref, _, out_ref, de2_ref, dk_acc, dv_acc):
        ok = _nbr_col_ok()
        dk_acc[...] = jnp.zeros_like(dk_acc)
        dv_acc[...] = jnp.zeros_like(dv_acc)
        de2_ref[...] = jnp.zeros_like(de2_ref)

        def step(i, carry):
            blocks = [_nbr_scores(qkv_ref, e2_ref, NBR_SIDE * i + u, rows, ok) for u in range(NBR_SIDE)]
            dys = [dy_ref[pl.ds(b[0], GRID_W), :] for b in blocks]
            dps = [lax.dot_general(dyv, b[5], (NT, ((), ())), preferred_element_type=F32) for dyv, b in zip(dys, blocks)]
            grads = []
            for (q0, k0, first, q, k, v, s), dyv, dp in zip(blocks, dys, dps):
                qrows = pl.ds(q0, GRID_W)
                delta = jnp.sum(dyv.astype(F32) * y_ref[qrows, :].astype(F32), axis=-1, keepdims=True)
                p = jnp.where(ok, jnp.exp(s - jnp.tile(lse_ref[qrows, :], (1, nk // HEAD_DIM))), 0.0)
                ds = p * (dp - delta)
                for pair in range(NA_ROWS // 2):
                    de2_ref[first + 2 * pair] += ds[:, pair * 128:(pair + 1) * 128]
                grads.append((ds.astype(BF), p.astype(BF)))
            for (q0, k0, first, q, k, v, s), dyv, (dsb, pb) in zip(blocks, dys, grads):
                out_ref[0, pl.ds(q0, GRID_W), :] = (jnp.dot(dsb, k, preferred_element_type=F32) * SCALE).astype(BF)
                keys = pl.ds(k0, nk)
                dk_acc[keys, :] += lax.dot_general(dsb, q, (TN, ((), ())), preferred_element_type=F32) * SCALE
                dv_acc[keys, :] += lax.dot_general(pb, dyv, (TN, ((), ())), preferred_element_type=F32)
            return carry

        lax.fori_loop(0, rows // NBR_SIDE, step, 0)
        out_ref[1] = dk_acc[...].astype(BF)
        out_ref[2] = dv_acc[...].astype(BF)

    heads = pl.BlockSpec((3, S, HEAD_DIM), lambda h: (0, 0, N_HEADS_A + h))
    row = pl.BlockSpec((S, HEAD_DIM), lambda h: (0, h))
    table = pl.BlockSpec((None, RPB_ROWS - 1, GRID_W, 128), lambda h: (h, 0, 0, 0))
    return ORDER.call(
        body, [qkv3, e2, dy, y, lse, dqkv3],
        [heads, table, row, row, row, pl.BlockSpec(memory_space=pl.ANY)], name="attn_b_bwd", grid=(4,),
        out_specs=[heads, table],
        out_shape=[_sds((3, S, QKV_W), BF), _sds((4, RPB_ROWS - 1, GRID_W, 128), F32)],
        scratch_shapes=[pltpu.VMEM((S, HEAD_DIM), F32), pltpu.VMEM((S, HEAD_DIM), F32)],
        input_output_aliases={5: 0},
        compiler_params=_cparams(("parallel",)), chain_output=1,
    )


def _rpb_to_table(rpb):
    pad = jnp.pad(rpb, ((0, 0), (0, 0), (0, 1)))
    pairs = jnp.concatenate([pad[:, :-1], pad[:, 1:]], axis=-1).reshape(4 * (RPB_ROWS - 1), 64)
    onehot = jnp.asarray(_toeplitz_onehot())
    n = onehot.shape[1]
    tn = 2048
    full = lambda i, j, k: (0, 0)
    (e2,) = _matmul("rpb_table", pairs, onehot, pl.BlockSpec(pairs.shape, full),
                    pl.BlockSpec((64, tn), lambda i, j, k: (0, j)), NN, (1, n // tn, 1), (pairs.shape[0], tn), [],
                    [(_sds((pairs.shape[0], n), F32), pl.BlockSpec((pairs.shape[0], tn), lambda i, j, k: (0, j)))],
                    _store(F32), precision=lax.Precision.HIGHEST)
    return e2.reshape(4, RPB_ROWS - 1, GRID_W, 128)


def _table_grad_to_rpb(de2):
    onehot = jnp.asarray(_toeplitz_onehot())
    n = onehot.shape[1]
    flat = de2.reshape(4 * (RPB_ROWS - 1), n)
    tk = 2048
    (dpairs,) = _matmul("rpb_table_grad", flat, onehot, pl.BlockSpec((flat.shape[0], tk), lambda i, j, k: (0, k)),
                        pl.BlockSpec((64, tk), lambda i, j, k: (0, k)), NT, (1, 1, n // tk), (flat.shape[0], 64), [],
                        [(_sds((flat.shape[0], 64), F32), pl.BlockSpec((flat.shape[0], 64), lambda i, j, k: (0, 0)))],
                        _store(F32), precision=lax.Precision.HIGHEST)
    dpairs = dpairs.reshape(4, RPB_ROWS - 1, 64)
    zero = jnp.zeros((4, 1, RPB_COLS), F32)
    return (jnp.concatenate([dpairs[:, :, :RPB_COLS], zero], axis=1)
            + jnp.concatenate([zero, dpairs[:, :, 32:32 + RPB_COLS]], axis=1))


HBM = pl.BlockSpec(memory_space=pl.ANY)


def _place():
    x, y, c = lax.axis_index("x"), lax.axis_index("y"), lax.axis_index("c")
    chips = [(1 - x, y), (x, 1 - y), (1 - x, 1 - y)]
    return x, y, c, chips


def _remote(src, dst, send_sem, recv_sem, to):
    return pltpu.make_async_remote_copy(src_ref=src, dst_ref=dst, send_sem=send_sem, recv_sem=recv_sem,
                                        device_id=to, device_id_type=MESH)


def _place_shard(name, w, me):
    R, C = w.shape
    tr = _tile(R, 256)

    def body(me_ref, w_ref, o_ref):
        o_ref[...] = w_ref[...].astype(BF)

    return ORDER.call(
        body, [w], [pl.BlockSpec((tr, C), lambda i, mr: (i, 0))], prefetch=(me,), name=name, grid=(R // tr,),
        out_specs=pl.BlockSpec((None, tr, C), lambda i, mr: (mr[0], i, 0)),
        out_shape=_sds((N_CHIPS, R, C), BF), compiler_params=_cparams(("parallel",)),
    )


SEM = pl.BlockSpec(memory_space=pltpu.SEMAPHORE)
IN_HBM = pl.BlockSpec(memory_space=pltpu.HBM)
DATAFLOW = pltpu.SideEffectType.DATAFLOW_SIDE_EFFECTING


def _in_hbm(a):
    return pltpu.with_memory_space_constraint(a, pltpu.HBM)


def _copy_start(name, bufs, copies, n_copies, earlier=None):
    n = len(bufs)
    after = None if any(b is ORDER.last for b in bufs) else ORDER.last
    n_extra = (2 if earlier is not None else 0) + (1 if after is not None else 0)

    def body(*refs):
        ins = refs[:n]
        if earlier is not None:
            for k, (src, dst, to) in enumerate(earlier[0](ins)):
                cp = _remote(src, dst, refs[n].at[k], refs[n + 1].at[k], to)
                cp.wait_send()
                cp.wait_recv()
        send_sems, recv_sems = refs[n + n_extra], refs[n + n_extra + 1]
        for k, (src, dst, to) in enumerate(copies(ins)):
            _remote(src, dst, send_sems.at[k], recv_sems.at[k], to).start()
        refs[-1][...] = jnp.zeros((8, 128), F32)

    operands = [_in_hbm(b) for b in bufs]
    in_specs = [IN_HBM] * n
    if earlier is not None:
        operands += [earlier[1], earlier[2]]
        in_specs += [SEM, SEM]
    if after is not None:
        operands.append(after)
        in_specs.append(HBM)
    outs = pl.pallas_call(
        body, name=name,
        out_shape=(pltpu.SemaphoreType.DMA((n_copies,)), pltpu.SemaphoreType.DMA((n_copies,)),
                   *[pltpu.HBM(b.shape, b.dtype) for b in bufs], _sds((8, 128), F32)),
        in_specs=in_specs,
        out_specs=(SEM, SEM, *[IN_HBM] * n, pl.BlockSpec(memory_space=pltpu.VMEM)),
        input_output_aliases={i: 2 + i for i in range(n)},
        compiler_params=pltpu.CompilerParams(has_side_effects=DATAFLOW),
    )(*operands)
    ORDER.last = outs[-1]
    return outs[0], outs[1], list(outs[2:2 + n])


def _copy_wait(name, bufs, copies, send_sems, recv_sems):
    n = len(bufs)
    after = ORDER.last

    def body(*refs):
        ins = refs[:n]
        for k, (src, dst, to) in enumerate(copies(ins)):
            cp = _remote(src, dst, refs[n].at[k], refs[n + 1].at[k], to)
            cp.wait_send()
            cp.wait_recv()

    outs = list(pl.pallas_call(
        body, name=name,
        out_shape=tuple(pltpu.HBM(b.shape, b.dtype) for b in bufs),
        in_specs=[IN_HBM] * n + [SEM, SEM, HBM], out_specs=tuple([IN_HBM] * n),
        input_output_aliases={i: i for i in range(n)},
        compiler_params=pltpu.CompilerParams(has_side_effects=DATAFLOW),
    )(*bufs, send_sems, recv_sems, after))
    ORDER.last = outs[0]
    return outs


def _gather_hop1(bufs):
    x, y, c, chips = _place()
    out = []
    for b in bufs:
        half = b.shape[1] // 2
        mine = b.at[2 * x + y, pl.ds(c * half, half), :]
        out += [(mine, mine, (*chip, c)) for chip in chips]
    return out


def _gather_hop2(bufs):
    x, y, c, chips = _place()
    out = []
    for b in bufs:
        half = b.shape[1] // 2
        for chip in chips:
            landed = b.at[2 * chip[0] + chip[1], pl.ds(c * half, half), :]
            out.append((landed, landed, (x, y, 1 - c)))
    return out


def _swap_copies(bufs):
    x, y, c, _ = _place()
    n = len(bufs) // 2
    out = []
    for p, land in zip(bufs[:n], bufs[n:]):
        half = p.shape[1] // 2
        out.append((p.at[:, pl.ds((1 - c) * half, half), :], land, (x, y, 1 - c)))
    return out


def _scatter_copies(bufs):
    _, _, c, chips = _place()
    n = len(bufs) // 2
    out = []
    for s_, land in zip(bufs[:n], bufs[n:]):
        out += [(s_.at[2 * chip[0] + chip[1]], land.at[j], (*chip, c)) for j, chip in enumerate(chips)]
    return out


def _join_copies(bufs):
    x, y, c, _ = _place()
    out = []
    for b in bufs:
        half = b.shape[0] // 2
        mine = b.at[pl.ds(c * half, half), :]
        out.append((mine, mine, (x, y, 1 - c)))
    return out


def _gather_small(vec):
    m_per, n = vec.shape

    def body(x_ref, out_ref, send_sems, recv_sems, local_sem):
        x, y, c, chips = _place()
        me, sibling = (x, y, c), (x, y, 1 - c)

        def rows(px, py, pc):
            return out_ref.at[pl.ds((4 * px + 2 * py + pc) * m_per, m_per), :]

        def copy(k, block, to, src=None):
            return _remote(rows(*block) if src is None else src, rows(*block), send_sems.at[k], recv_sems.at[k], to)

        mine = pltpu.make_async_copy(x_ref, rows(*me), local_sem)
        mine.start()
        first = [copy(0, me, sibling, src=x_ref)]
        first += [copy(1 + j, me, (*chip, c), src=x_ref) for j, chip in enumerate(chips)]
        for cp in first:
            cp.start()
        passed = [copy(4 + j, (*chip, c), sibling) for j, chip in enumerate(chips)]
        for j, chip in enumerate(chips):
            copy(1 + j, (*chip, c), me).wait_recv()
            passed[j].start()
        copy(0, sibling, me).wait_recv()
        for j, chip in enumerate(chips):
            copy(4 + j, (*chip, 1 - c), me).wait_recv()
        for cp in first + passed:
            cp.wait_send()
        mine.wait()

    return ORDER.call(
        body, [vec], [pl.BlockSpec(memory_space=pltpu.VMEM)], name="gather_small_grads",
        out_shape=_sds((8 * m_per, n), vec.dtype), out_specs=pl.BlockSpec(memory_space=pltpu.VMEM),
        scratch_shapes=[pltpu.SemaphoreType.DMA((7,)), pltpu.SemaphoreType.DMA((7,)), pltpu.SemaphoreType.DMA],
    )


def _add_sibling(name, partial, received, c):
    _, R, C = partial.shape
    half = R // 2
    tr = _tile(half, 256)
    nb = half // tr

    def body(c_ref, p_ref, r_ref, o_ref):
        o_ref[...] = (p_ref[...].astype(F32) + r_ref[...].astype(F32)).astype(BF)

    return ORDER.call(
        body, [partial, received],
        [pl.BlockSpec((None, tr, C), lambda j, i, cr: (j, cr[0] * nb + i, 0)),
         pl.BlockSpec((None, tr, C), lambda j, i, cr: (j, i, 0))],
        prefetch=(c,), name=name, grid=(N_CHIPS, nb),
        out_specs=pl.BlockSpec((None, tr, C), lambda j, i, cr: (j, i, 0)),
        out_shape=_sds((N_CHIPS, half, C), BF), compiler_params=_cparams(("parallel", "parallel")),
    )


def _add_chips(name, sums, received, me_c):
    _, half, C = sums.shape
    tr = _tile(half, 256)
    nb = half // tr

    def body(mc_ref, s_ref, r_ref, o_ref):
        acc = s_ref[...].astype(F32)
        for j in range(3):
            acc = acc + r_ref[j].astype(F32)
        o_ref[...] = acc

    return ORDER.call(
        body, [sums, received],
        [pl.BlockSpec((None, tr, C), lambda i, mc: (mc[0], i, 0)),
         pl.BlockSpec((3, tr, C), lambda i, mc: (0, i, 0))],
        prefetch=(me_c,), name=name, grid=(nb,),
        out_specs=pl.BlockSpec((tr, C), lambda i, mc: (mc[1] * nb + i, 0)),
        out_shape=_sds((2 * half, C), F32), compiler_params=_cparams(("parallel",)),
    )


def _adamw_math(w, g, m, v):
    m = ADAM_B1 * m + (1.0 - ADAM_B1) * g
    v = ADAM_B2 * v + (1.0 - ADAM_B2) * (g * g)
    m_hat = m / (1.0 - ADAM_B1 ** ADAM_STEP)
    v_hat = v / (1.0 - ADAM_B2 ** ADAM_STEP)
    delta = -ADAM_LR * (m_hat / (jnp.sqrt(v_hat) + ADAM_EPS) + ADAM_WD * w)
    return delta, m, v


def _adamw(name, w, g, m, v):
    R, C = w.shape
    tr = _tile(R, 128)

    def body(w_ref, g_ref, m_ref, v_ref, go_ref, d_ref, mo_ref, vo_ref):
        gv = g_ref[...]
        go_ref[...] = gv
        d_ref[...], mo_ref[...], vo_ref[...] = _adamw_math(w_ref[...], gv, m_ref[...], v_ref[...])

    row = pl.BlockSpec((tr, C), lambda i: (i, 0))
    return ORDER.call(
        body, [w, g, m, v], [row] * 4, name=name, grid=(R // tr,), out_specs=[row] * 4,
        out_shape=[_sds((R, C), F32)] * 4, compiler_params=_cparams(("parallel",)), chain_output=1,
    )


def _adamw_small(gathered, w, m, v):
    rows, n = w.shape

    def body(ga_ref, w_ref, m_ref, v_ref, go_ref, d_ref, mo_ref, vo_ref):
        g = ga_ref[pl.ds(0, rows), :]
        for dev in range(1, 8):
            g = g + ga_ref[pl.ds(dev * rows, rows), :]
        go_ref[...] = g
        d_ref[...], mo_ref[...], vo_ref[...] = _adamw_math(w_ref[...], g, m_ref[...], v_ref[...])

    whole = pl.BlockSpec(memory_space=pltpu.VMEM)
    return ORDER.call(
        body, [gathered, w, m, v], [whole] * 4, name="adamw_small", out_specs=[whole] * 4,
        out_shape=[_sds((rows, n), F32)] * 4, compiler_params=_cparams(), chain_output=1,
    )


def _proj_merge(y_a, y_b, gpa, gpb, g3):
    S, K = y_a.shape
    _, _, Nq = gpa.shape
    D = N_CHIPS * Nq
    tm, tn = _tile(S, 1024), _tile(Nq, 512)
    q = Nq // tn

    def body(ya_ref, yb_ref, wa_ref, wb_ref, g_ref, merged_ref, c_ref):
        pa = jnp.dot(ya_ref[...], wa_ref[...], preferred_element_type=F32)
        pb = jnp.dot(yb_ref[...], wb_ref[...], preferred_element_type=F32)
        g = g_ref[...].astype(F32)
        merged_ref[...] = (g[0] * pa + g[1] * pb).astype(BF)
        c_ref[0] = (pa * g[0] * (1.0 - g[0])).astype(BF)
        c_ref[1] = (pb * g[1] * (1.0 - g[1])).astype(BF)

    rows = pl.BlockSpec((tm, K), lambda i, j: (i, 0))
    weight = pl.BlockSpec((None, K, tn), lambda i, j: (j // q, 0, j % q))
    pair = pl.BlockSpec((2, tm, tn), lambda i, j: (0, i, j))
    return ORDER.call(
        body, [y_a, y_b, gpa, gpb, g3], [rows, rows, weight, weight, pair], name="proj_merge",
        grid=(S // tm, N_CHIPS * q), out_specs=[pl.BlockSpec((tm, tn), lambda i, j: (i, j)), pair],
        out_shape=[_sds((S, D), BF), _sds((2, S, D), BF)], compiler_params=_cparams(("parallel", "parallel")))


class _Exchange:
    GATHER = (("qkv",), ("gate", "proj_a", "proj_b", "out"), ("up", "down"))
    REDUCE = {"mlp": ("down", "up"), "mix": ("out", "proj_a", "proj_b"), "in": ("qkv", "gate")}

    def __init__(self, shards, me, c):
        self.me, self.c = me, c
        self.hop1, self.hop2, self.stage, self.grads = {}, {}, {}, {}
        for g, names in enumerate(self.GATHER):
            bufs = [_place_shard(f"place_{n}", shards[n], me) for n in names]
            self.hop1[g] = _copy_start(f"gather{g}_start", bufs, _gather_hop1, 3 * len(names))

    def forward(self, g):
        send, recv, thru = self.hop1.pop(g)
        self.hop2[g] = _copy_start(f"gather{g}_forward", thru, _gather_hop2, len(thru) * 3,
                                   earlier=(_gather_hop1, send, recv))

    def weights(self, g):
        send, recv, thru = self.hop2.pop(g)
        return _copy_wait(f"gather{g}_wait", thru, _gather_hop2, send, recv)

    def reduce(self, key, partials=None):
        names = self.REDUCE[key]
        n = len(names)
        if partials is not None:
            lands = [lax.empty((p.shape[0], p.shape[1] // 2, p.shape[2]), p.dtype) for p in partials]
            self.stage[key] = ("swap",) + _copy_start(f"reduce_{key}_swap", list(partials) + lands, _swap_copies, n)
            return
        kind, send, recv, thru = self.stage.pop(key)
        if kind == "swap":
            thru = _copy_wait(f"reduce_{key}_swap_wait", thru, _swap_copies, send, recv)
            sums = [_add_sibling(f"reduce_{nm}_add_sibling", p, r, self.c)
                    for nm, p, r in zip(names, thru[:n], thru[n:])]
            lands = [lax.empty((3,) + s_.shape[1:], s_.dtype) for s_ in sums]
            self.stage[key] = ("scatter",) + _copy_start(f"reduce_{key}_scatter", sums + lands, _scatter_copies, 3 * n)
        elif kind == "scatter":
            thru = _copy_wait(f"reduce_{key}_scatter_wait", thru, _scatter_copies, send, recv)
            me_c = jnp.concatenate([self.me, self.c])
            halves = [_add_chips(f"reduce_{nm}_add_chips", s_, r, me_c)
                      for nm, s_, r in zip(names, thru[:n], thru[n:])]
            self.stage[key] = ("join",) + _copy_start(f"reduce_{key}_join", halves, _join_copies, n)
        else:
            thru = _copy_wait(f"reduce_{key}_join_wait", thru, _join_copies, send, recv)
            self.grads.update(zip(names, thru))


def _forward_backward(x, target, norm_mix, b_gate, rpb, norm_mlp, norm_final, ex):
    S, D = x.shape

    h1 = _rms_fwd("rms_mix", x, norm_mix)
    ex.forward(0)
    e2 = _rpb_to_table(rpb)
    (gq,) = ex.weights(0)
    nq = QKV_W // 512
    (qkv3,), _ = _mm_nn_cols(
        "qkv", h1, gq, BF, tn=512,
        outs=[(_sds((3, S, QKV_W), BF), pl.BlockSpec((None, _tile(S, 1024), 512), lambda i, j, k: (j // nq, i, j % nq)))])

    ex.forward(1)
    outs_a = [_attn_a_fwd(qkv3, 0, DILATIONS[0])]
    gg, gpa, gpb, gout = ex.weights(1)
    wout = gout.reshape(D, D)

    tg = _tile(gg.shape[2], 1024)
    ng = D // tg

    def gate_epilogue(acc, ex_, outs):
        outs[0][...] = jax.nn.sigmoid(acc + ex_[0][...]).astype(BF)

    (g3,), _ = _mm_nn_cols(
        "gate", h1, gg, BF, epilogue=gate_epilogue, tn=tg,
        extras=[(b_gate, pl.BlockSpec((1, tg), lambda i, j, k: (0, j)))],
        outs=[(_sds((2, S, D), BF), pl.BlockSpec((None, _tile(S, 1024), tg), lambda i, j, k: (j // ng, i, j % ng)))])

    outs_a += [_attn_a_fwd(qkv3, grp, d) for grp, d in enumerate(DILATIONS) if grp > 0]
    y_a, lj = _attn_a_combine([o for o, _ in outs_a], [l for _, l in outs_a])
    y_b, lse_b = _attn_b_fwd(qkv3, e2)
    merged, c3 = _proj_merge(y_a, y_b, gpa, gpb, g3)

    def residual_epilogue(acc, ex_, outs):
        outs[0][...] = acc + ex_[0][...]

    def nn_plain(name, a, w, res):
        M, K = a.shape
        N = w.shape[1]
        bm, bn, bk = _tile(M, 1024), _tile(N, 1024), _tile(K, 2048)
        t = pl.BlockSpec((bm, bn), lambda i, j, k: (i, j))
        return _matmul(name, a, w, pl.BlockSpec((bm, bk), lambda i, j, k: (i, k)),
                       pl.BlockSpec((bk, bn), lambda i, j, k: (k, j)), NN, (M // bm, N // bn, K // bk), (bm, bn),
                       [(res, t)], [(_sds((M, N), F32), t)], residual_epilogue)[0]

    ex.forward(2)
    x1 = nn_plain("out_proj", merged, wout, x)
    h2 = _rms_fwd("rms_mlp", x1, norm_mlp)
    gup, gdown = ex.weights(2)
    F = gup.shape[2] * N_CHIPS
    wdown = gdown.reshape(F, D)

    def up_epilogue(acc, ex_, outs):
        ru = jnp.maximum(acc, 0.0)
        outs[0][...] = (ru * ru).astype(BF)
        outs[1][...] = ru.astype(BF)

    tu = _tile(gup.shape[2], 1024)
    ut = pl.BlockSpec((_tile(S, 1024), tu), lambda i, j, k: (i, j))
    (act, ru), _ = _mm_nn_cols("mlp_up", h2, gup, BF, epilogue=up_epilogue, tn=tu,
                               outs=[(_sds((S, F), BF), ut), (_sds((S, F), BF), ut)])
    x2 = nn_plain("mlp_down", act, wdown, x1)

    loss, dx2, dx2b, d_norm_final = _loss_head(x2, target, norm_final.reshape(1, D))

    def nt_rows(name, a, w, epilogue, extras, outs, bn=1024):
        M, N = a.shape
        K = w.shape[0]
        bm, bn, bk = _tile(M, 1024), _tile(K, bn), _tile(N, 2048)
        return _matmul(name, a, w, pl.BlockSpec((bm, bk), lambda i, j, k: (i, k)),
                       pl.BlockSpec((bn, bk), lambda i, j, k: (j, k)), NT, (M // bm, K // bn, N // bk), (bm, bn),
                       extras(bm, bn), outs(bm, bn), epilogue)

    def nt_cols(name, a_spec_fn, a, g, M, epilogue, extras, outs, bk):
        _, K, Nq = g.shape
        bm, bn, bk = _tile(M, 1024), _tile(K, 1024), _tile(Nq, bk)
        q = Nq // bk
        return _matmul(name, a, g, a_spec_fn(bm, bk), pl.BlockSpec((None, bn, bk), lambda i, j, k: (k // q, j, k % q)),
                       NT, (M // bm, K // bn, N_CHIPS * q), (bm, bn), extras(bm, bn), outs(bm, bn), epilogue)

    def tn_grad(name, a, a_spec_fn, b, b_spec_fn, Kin, N, out_shape, out_spec_fn, bn=1024):
        bm, bn, bk = _tile(Kin, 1024), _tile(N, bn), _tile(S, 2048)
        return _matmul(name, a, b, a_spec_fn(bk, bm), b_spec_fn(bk, bn), TN, (Kin // bm, N // bn, S // bk), (bm, bn),
                       [], [(_sds(out_shape, BF), out_spec_fn(bm, bn))], _store(BF))[0]

    plain_a = lambda bk, bm: pl.BlockSpec((bk, bm), lambda i, j, k: (k, i))
    plain_b = lambda bk, bn: pl.BlockSpec((bk, bn), lambda i, j, k: (k, j))
    plain_o = lambda bm, bn: pl.BlockSpec((bm, bn), lambda i, j, k: (i, j))
    a_rows = lambda bm, bk: pl.BlockSpec((bm, bk), lambda i, j, k: (i, k))

    def cols_o(Nq):
        def spec(bm, bn):
            q = Nq // bn
            return pl.BlockSpec((None, bm, bn), lambda i, j, k: (j // q, i, j % q))
        return spec

    def du_epilogue(acc, ex_, outs):
        outs[0][...] = (acc * (2.0 * ex_[0][...].astype(F32))).astype(BF)

    dw_down = tn_grad("mlp_down_dw", act, plain_a, dx2b, plain_b, F, D, (F, D), plain_o)
    (du,) = nt_rows("mlp_down_dx", dx2b, wdown, du_epilogue,
                    lambda bm, bn: [(ru, plain_o(bm, bn))], lambda bm, bn: [(_sds((S, F), BF), plain_o(bm, bn))])

    fq = gup.shape[2]
    dw_up = tn_grad("mlp_up_dw", h2, plain_a, du, plain_b, D, F, (N_CHIPS, D, fq), cols_o(fq), bn=min(fq, 1024))
    ex.reduce("mlp", partials=[dw_down.reshape(N_CHIPS, F // N_CHIPS, D), dw_up])
    (dh2,) = nt_cols("mlp_up_dx", a_rows, du, gup, S, _store(F32), lambda bm, bn: [],
                     lambda bm, bn: [(_sds((S, D), F32), plain_o(bm, bn))], 2048)
    ex.reduce("mlp")
    dx1, dx1b, d_norm_mlp = _rms_bwd("rms_mlp_bwd", dh2, x1, norm_mlp, dx2)

    def merge_bwd_epilogue(acc, ex_, outs):
        g, c = ex_[0][...].astype(F32), ex_[1][...].astype(F32)
        outs[0][...] = (acc * g[0]).astype(BF)
        outs[1][...] = (acc * g[1]).astype(BF)
        dga = acc * c[0]
        dgb = acc * c[1]
        outs[2][0] = dga.astype(BF)
        outs[2][1] = dgb.astype(BF)
        outs[3][...] = jnp.concatenate([jnp.sum(dga, axis=0, keepdims=True), jnp.sum(dgb, axis=0, keepdims=True)], 0)

    def pair(bm, bn):
        return pl.BlockSpec((2, bm, bn), lambda i, j, k: (0, i, j))

    n_row_blocks = S // _tile(S, 1024)
    dpa, dpb, dg3, db_gate = nt_rows(
        "out_proj_dx", dx1b, wout, merge_bwd_epilogue,
        lambda bm, bn: [(g3, pair(bm, bn)), (c3, pair(bm, bn))],
        lambda bm, bn: [(_sds((S, D), BF), plain_o(bm, bn)), (_sds((S, D), BF), plain_o(bm, bn)),
                        (_sds((2, S, D), BF), pair(bm, bn)),
                        (_sds((n_row_blocks, 2, D), F32), pl.BlockSpec((None, 2, bn), lambda i, j, k: (i, 0, j)))],
        bn=512)
    dw_out = tn_grad("out_proj_dw", merged, plain_a, dx1b, plain_b, D, D, (D, D), plain_o)

    pq = gpa.shape[2]
    proj_dx = lambda name, dproj, g: nt_cols(name, a_rows, dproj, g, S, _store(BF), lambda bm, bn: [],
                                             lambda bm, bn: [(_sds((S, 512), BF), plain_o(bm, bn))], 512)[0]
    dw_pa = tn_grad("proj_a_dw", y_a, plain_a, dpa, plain_b, 512, D, (N_CHIPS, 512, pq), cols_o(pq), bn=min(pq, 512))
    dw_pb = tn_grad("proj_b_dw", y_b, plain_a, dpb, plain_b, 512, D, (N_CHIPS, 512, pq), cols_o(pq), bn=min(pq, 512))
    ex.reduce("mix", partials=[dw_out.reshape(N_CHIPS, D // N_CHIPS, D), dw_pa, dw_pb])
    dy_a = proj_dx("proj_a_dx", dpa, gpa)
    dy_b = proj_dx("proj_b_dx", dpb, gpb)

    dqkv3 = lax.empty((3, S, QKV_W), BF)
    dqkv3 = _attn_a_bwd(qkv3, dy_a, y_a, lj, dqkv3, 0, DILATIONS[0])
    ex.reduce("mix")
    dy_views, y_views, lj_views = _dilated_rows("attn_a_bwd_rows", [dy_a, y_a, lj])
    for grp, d in enumerate(DILATIONS):
        if grp > 0:
            dqkv3 = _attn_a_bwd(qkv3, dy_views[d], y_views[d], lj_views[d], dqkv3, grp, d)
    dqkv3, de2 = _attn_b_bwd(qkv3, e2, dy_b, y_b, lse_b, dqkv3)
    d_rpb = _table_grad_to_rpb(de2)

    def stacked_a(width):
        def spec(bm, bk):
            q = width // bk
            return pl.BlockSpec((None, bm, bk), lambda i, j, k: (k // q, i, k % q))
        return spec

    def stacked_b(width):
        def spec(bk, bn):
            q = width // bn
            return pl.BlockSpec((None, bk, bn), lambda i, j, k: (j // q, k, j % q))
        return spec

    dw_qkv = tn_grad("qkv_dw", h1, plain_a, dqkv3, stacked_b(QKV_W), D, 3 * QKV_W, (N_CHIPS,) + gq.shape[1:],
                     cols_o(gq.shape[2]), bn=512)
    dw_gate = tn_grad("gate_dw", h1, plain_a, dg3, stacked_b(D), D, 2 * D, (N_CHIPS,) + gg.shape[1:],
                      cols_o(gg.shape[2]), bn=gg.shape[2])
    ex.reduce("in", partials=[dw_qkv, dw_gate])
    ex.reduce("mlp")
    (dh1_q,) = nt_cols("qkv_dx", stacked_a(QKV_W), dqkv3, gq, S, _store(F32), lambda bm, bn: [],
                       lambda bm, bn: [(_sds((S, D), F32), plain_o(bm, bn))], 512)
    ex.reduce("in")
    ex.reduce("mix")

    def add_epilogue(acc, ex_, outs):
        outs[0][...] = acc + ex_[0][...]

    (dh1,) = nt_cols("gate_dx", stacked_a(D), dg3, gg, S, add_epilogue, lambda bm, bn: [(dh1_q, plain_o(bm, bn))],
                     lambda bm, bn: [(_sds((S, D), F32), plain_o(bm, bn))], gg.shape[2])
    grad_x, _, d_norm_mix = _rms_bwd("rms_mix_bwd", dh1, x, norm_mix, dx1)
    ex.reduce("mlp")
    ex.reduce("mix")

    small = [d_norm_mix, jnp.sum(db_gate, axis=0).reshape(1, 2 * D), d_rpb, d_norm_mlp, d_norm_final]
    return loss, grad_x, small


def _pack_small(parts, width):
    flat = jnp.concatenate([p.reshape(-1) for p in parts])
    return jnp.pad(flat, (0, 8 * width - flat.shape[0])).reshape(8, width)


def kernel(x, norm_mix, w_qkv, w_gate, b_gate, rpb, w_proj_a, w_proj_b, w_out, norm_mlp, w_up, w_down, norm_final, loss_target, m_norm_mix, m_w_qkv, m_w_gate, m_b_gate, m_rpb, m_w_proj_a, m_w_proj_b, m_w_out, m_norm_mlp, m_w_up, m_w_down, m_norm_final, v_norm_mix, v_w_qkv, v_w_gate, v_b_gate, v_rpb, v_w_proj_a, v_w_proj_b, v_w_out, v_norm_mlp, v_w_up, v_w_down, v_norm_final):
    names = ["qkv", "gate", "proj_a", "proj_b", "out", "up", "down"]
    big = dict(zip(names, [w_qkv[0], w_gate[0], w_proj_a[0], w_proj_b[0], w_out[0], w_up[0], w_down[0]]))
    big_m = dict(zip(names, [m_w_qkv[0], m_w_gate[0], m_w_proj_a[0], m_w_proj_b[0], m_w_out[0], m_w_up[0], m_w_down[0]]))
    big_v = dict(zip(names, [v_w_qkv[0], v_w_gate[0], v_w_proj_a[0], v_w_proj_b[0], v_w_out[0], v_w_up[0], v_w_down[0]]))

    c = lax.axis_index("c").astype(jnp.int32).reshape(1)
    me = (2 * lax.axis_index("x") + lax.axis_index("y")).astype(jnp.int32).reshape(1)
    ORDER.last = None
    ex = _Exchange(big, me, c)
    loss, grad_x, small = _forward_backward(x[0], loss_target[0], norm_mix, b_gate, rpb[0], norm_mlp, norm_final, ex)

    def adamw(group):
        return {n: _adamw(f"adamw_{n}", big[n], ex.grads[n], big_m[n], big_v[n]) for n in _Exchange.REDUCE[group]}

    big_out = {**adamw("mlp"), **adamw("mix")}
    ex.reduce("in")

    small_w = [norm_mix, b_gate, rpb, norm_mlp, norm_final]
    count = sum(int(np.prod(p.shape)) for p in small_w)
    width = -(-count // (8 * 128)) * 128
    packed = _adamw_small(_gather_small(_pack_small(small, width)), _pack_small(small_w, width),
                          _pack_small([m_norm_mix, m_b_gate, m_rpb, m_norm_mlp, m_norm_final], width),
                          _pack_small([v_norm_mix, v_b_gate, v_rpb, v_norm_mlp, v_norm_final], width))
    ex.reduce("in")
    big_out.update(adamw("in"))

    def unpack(flat2d):
        flat, out, at = flat2d.reshape(-1), [], 0
        for p in small_w:
            size = int(np.prod(p.shape))
            out.append(flat[at:at + size].reshape(p.shape))
            at += size
        return out

    small_out = [unpack(a) for a in packed]

    def ordered(kind):
        sm = small_out[kind]
        bg = {n: o[kind][None] for n, o in big_out.items()}
        return [sm[0], bg["qkv"], bg["gate"], sm[1], sm[2], bg["proj_a"], bg["proj_b"], bg["out"], sm[3],
                bg["up"], bg["down"], sm[4]]

    total = lax.psum(loss[0, 0], ("x", "y", "c"))
    return (total, grad_x[None], *ordered(0), *ordered(1), *ordered(2), *ordered(3))
```

```python
import functools
import math

import numpy as np
import jax
import jax.numpy as jnp
from jax import lax
from jax.experimental import pallas as pl
from jax.experimental.pallas import tpu as pltpu

BF = jnp.bfloat16
F32 = jnp.float32
MESH = pl.DeviceIdType.MESH

HEAD_DIM = 128
N_HEADS = 16
N_HEADS_A = 12
QKV_W = N_HEADS * HEAD_DIM
DILATIONS = (1, 4, 16)
HALF_WINDOW = 64
GRID_W = 64
NA_ROWS = 8
NA_COLS = 16
RPB_ROWS = 2 * NA_ROWS - 1
RPB_COLS = 2 * NA_COLS - 1
EPS = 1e-6
NEG = -1e30
SCALE = HEAD_DIM ** -0.5

ADAM_LR = 0.001
ADAM_B1 = 0.9
ADAM_B2 = 0.999
ADAM_EPS = 1e-08
ADAM_WD = 0.01
ADAM_STEP = 10

N_CHIPS = 4
VMEM_LIMIT_BYTES = 48 * 1024 * 1024
QB = 256
NBR_SIDE = 4
ROW_TILE = 512


def _key_rows(L):
    return min(QB + 2 * HALF_WINDOW, L)


def _cparams(sem=None):
    return pltpu.CompilerParams(dimension_semantics=sem, vmem_limit_bytes=VMEM_LIMIT_BYTES)


def _tile(dim, want):
    t = min(dim, want)
    assert dim % t == 0, (dim, want)
    return t


class _ProgramOrder:
    def __init__(self):
        self.last = None

    def call(self, body, operands, in_specs, *, prefetch=(), grid=None, out_specs=None, chain_output=0, **kwargs):
        operands, in_specs = list(operands), list(in_specs)
        lead = len(prefetch) + len(operands)
        if self.last is not None and not any(op is self.last for op in operands):
            operands.append(self.last)
            in_specs.append(pl.BlockSpec(memory_space=pl.ANY))
            inner = body

            def body(*refs):
                return inner(*refs[:lead], *refs[lead + 1:])

        if prefetch:
            kwargs["grid_spec"] = pltpu.PrefetchScalarGridSpec(
                num_scalar_prefetch=len(prefetch), grid=grid, in_specs=in_specs, out_specs=out_specs)
        else:
            kwargs.update(in_specs=in_specs, out_specs=out_specs)
            if grid is not None:
                kwargs["grid"] = grid
        out = pl.pallas_call(body, **kwargs)(*prefetch, *operands)
        self.last = out[chain_output] if isinstance(out, (tuple, list)) else out
        return out


ORDER = _ProgramOrder()


NN = ((1,), (0,))
NT = ((1,), (1,))
TN = ((0,), (0,))


def _matmul(name, a, b, a_spec, b_spec, dims, grid, acc_shape, extras, outs, epilogue, precision=None,
            prefetch=(), into=None):
    n_ex, n_out, nk = len(extras), len(outs), grid[2]
    n_in = 2 + n_ex + (into is not None)

    def body(*refs):
        refs = refs[len(prefetch):]
        a_ref, b_ref = refs[0], refs[1]
        ex_refs = refs[2:2 + n_ex]
        out_refs = refs[n_in:n_in + n_out]

        def dot():
            return lax.dot_general(a_ref[...], b_ref[...], (dims, ((), ())),
                                   preferred_element_type=F32, precision=precision)

        if nk == 1:
            epilogue(dot(), ex_refs, out_refs)
            return
        acc_ref = refs[-1]
        k = pl.program_id(2)

        @pl.when(k == 0)
        def _():
            acc_ref[...] = dot()

        if nk > 2:
            @pl.when((k > 0) & (k < nk - 1))
            def _():
                acc_ref[...] += dot()

        @pl.when(k == nk - 1)
        def _():
            epilogue(acc_ref[...] + dot(), ex_refs, out_refs)

    operands = [a, b] + [e for e, _ in extras]
    in_specs = [a_spec, b_spec] + [s for _, s in extras]
    kwargs = {}
    if into is not None:
        operands.append(into)
        in_specs.append(pl.BlockSpec(memory_space=pl.ANY))
        kwargs["input_output_aliases"] = {len(prefetch) + n_in - 1: 0}
    return ORDER.call(
        body, operands, in_specs, prefetch=prefetch, name=name, grid=grid,
        out_specs=[s for _, s in outs],
        out_shape=[sh for sh, _ in outs],
        scratch_shapes=[pltpu.VMEM(acc_shape, F32)] if nk > 1 else [],
        compiler_params=_cparams(("parallel", "parallel", "arbitrary")), **kwargs,
    )


def _mm_nn_shards(name, a, w, me, own, out, out_block, epilogue, extras=(), into=None, tn=512):
    M, K = a.shape
    Nq = w.shape[-1]
    tm, tn = _tile(M, 1024), _tile(Nq, tn)
    q = Nq // tn

    def tile(j, me_ref):
        shard = me_ref[0] if own else (me_ref[0] + 1 + j // q) % N_CHIPS
        return shard, j % q, shard * q + j % q

    if own:
        b_spec = pl.BlockSpec((K, tn), lambda i, j, k, me_ref: (0, j))
    else:
        b_spec = pl.BlockSpec((None, K, tn), lambda i, j, k, me_ref: (tile(j, me_ref)[0], 0, tile(j, me_ref)[1]))
    shape, dtype = out
    out_spec = pl.BlockSpec((None, tm, tn), lambda i, j, k, me_ref: out_block(i, tile(j, me_ref)[2]))
    ex = [(e, pl.BlockSpec((1, tn), lambda i, j, k, me_ref: (0, tile(j, me_ref)[2]))) for e in extras]
    return _matmul(name, a, w, pl.BlockSpec((tm, K), lambda i, j, k, me_ref: (i, 0)), b_spec, NN,
                   (M // tm, q if own else (N_CHIPS - 1) * q, 1), (tm, tn), ex, [(_sds(shape, dtype), out_spec)],
                   epilogue, prefetch=(me,), into=into)[0]


def _store(dtype):
    def epilogue(acc, ex, outs):
        outs[0][...] = acc.astype(dtype)
    return epilogue


def _sds(shape, dtype):
    return jax.ShapeDtypeStruct(shape, dtype)


def _mm_nn_cols(name, a, g, out_dtype, epilogue=None, extras=(), outs=None, tm=1024, tn=1024, tk=2048):
    M, K = a.shape
    _, _, Nq = g.shape
    tm, tn, tk = _tile(M, tm), _tile(Nq, tn), _tile(K, tk)
    q = Nq // tn
    grid = (M // tm, N_CHIPS * q, K // tk)
    if outs is None:
        outs = [(_sds((M, N_CHIPS * Nq), out_dtype), pl.BlockSpec((tm, tn), lambda i, j, k: (i, j)))]
    return _matmul(name, a, g, pl.BlockSpec((tm, tk), lambda i, j, k: (i, k)),
                   pl.BlockSpec((None, tk, tn), lambda i, j, k: (j // q, k, j % q)), NN, grid, (tm, tn),
                   list(extras), outs, epilogue or _store(out_dtype)), (tm, tn, tk)


def _rms_fwd(name, x, g):
    S, D = x.shape
    tm = _tile(S, ROW_TILE)

    def body(x_ref, g_ref, h_ref):
        xv = x_ref[...]
        r = lax.rsqrt(jnp.mean(xv * xv, axis=-1, keepdims=True) + EPS)
        h_ref[...] = ((xv * r) * g_ref[...]).astype(BF)

    row = pl.BlockSpec((tm, D), lambda i: (i, 0))
    return ORDER.call(
        body, [x, g], [row, pl.BlockSpec((1, D), lambda i: (0, 0))], name=name, grid=(S // tm,),
        out_specs=row, out_shape=_sds((S, D), BF), compiler_params=_cparams(("parallel",)),
    )


def _rms_bwd(name, dh, x, g, dres):
    S, D = x.shape
    tm = _tile(S, ROW_TILE // 2)

    def body(dh_ref, x_ref, g_ref, dres_ref, dx_ref, dxb_ref, dg_ref):
        xv = x_ref[...]
        r = lax.rsqrt(jnp.mean(xv * xv, axis=-1, keepdims=True) + EPS)
        n = xv * r
        dhv = dh_ref[...]
        dyg = dhv * g_ref[...]
        dx = dres_ref[...] + r * (dyg - n * jnp.mean(dyg * n, axis=-1, keepdims=True))
        dx_ref[...] = dx
        dxb_ref[...] = dx.astype(BF)

        @pl.when(pl.program_id(0) == 0)
        def _():
            dg_ref[...] = jnp.zeros_like(dg_ref)

        dg_ref[...] += jnp.sum(dhv * n, axis=0, keepdims=True)

    row = pl.BlockSpec((tm, D), lambda i: (i, 0))
    vec = pl.BlockSpec((1, D), lambda i: (0, 0))
    return ORDER.call(
        body, [dh, x, g, dres], [row, row, vec, row], name=name, grid=(S // tm,),
        out_specs=[row, row, vec],
        out_shape=[_sds((S, D), F32), _sds((S, D), BF), _sds((1, D), F32)],
        compiler_params=_cparams(("arbitrary",)),
    )


def _loss_head(x2, target, g):
    S, D = x2.shape
    tm = _tile(S, ROW_TILE)

    def body(x_ref, t_ref, g_ref, loss_ref, dx_ref, dxb_ref, dg_ref):
        xv = x_ref[...]
        gv = g_ref[...]
        r = lax.rsqrt(jnp.mean(xv * xv, axis=-1, keepdims=True) + EPS)
        n = xv * r
        e = n * gv - t_ref[...]
        dy = e * (1.0 / D)
        dyg = dy * gv
        dx = r * (dyg - n * jnp.mean(dyg * n, axis=-1, keepdims=True))
        dx_ref[...] = dx
        dxb_ref[...] = dx.astype(BF)

        @pl.when(pl.program_id(0) == 0)
        def _():
            dg_ref[...] = jnp.zeros_like(dg_ref)
            loss_ref[...] = jnp.zeros_like(loss_ref)

        dg_ref[...] += jnp.sum(dy * n, axis=0, keepdims=True)
        per_row = jnp.mean(e * e, axis=-1, keepdims=True)
        loss_ref[...] += 0.5 * jnp.sum(per_row, axis=0, keepdims=True)

    row = pl.BlockSpec((tm, D), lambda i: (i, 0))
    vec = pl.BlockSpec((1, D), lambda i: (0, 0))
    return ORDER.call(
        body, [x2, target, g], [row, row, vec], name="loss_head", grid=(S // tm,),
        out_specs=[pl.BlockSpec((1, 1), lambda i: (0, 0)), row, row, vec],
        out_shape=[_sds((1, 1), F32), _sds((S, D), F32), _sds((S, D), BF), _sds((1, D), F32)],
        compiler_params=_cparams(("arbitrary",)), chain_output=1,
    )


def _chains(L):
    side = min(4, L // QB)
    return side, 4 // side


def _band_scores(qkv_ref, i, L, coef, head):
    KB = _key_rows(L)
    lanes = pl.ds(head * HEAD_DIM, HEAD_DIM)
    q0 = pl.multiple_of(i * QB, QB)
    ks = pl.multiple_of(jnp.clip(i * QB - HALF_WINDOW, 0, L - KB), HALF_WINDOW)
    q = qkv_ref[0, pl.ds(q0, QB), lanes]
    k = qkv_ref[1, pl.ds(ks, KB), lanes]
    v = qkv_ref[2, pl.ds(ks, KB), lanes]
    s = lax.dot_general(q, k, (NT, ((), ())), preferred_element_type=F32) * SCALE
    qpos = q0 + lax.broadcasted_iota(jnp.int32, (QB, KB), 0)
    kpos = ks + lax.broadcasted_iota(jnp.int32, (QB, KB), 1)
    rel = jnp.abs(kpos - qpos)
    valid = rel <= HALF_WINDOW
    s = jnp.where(valid, s - coef * rel.astype(F32), NEG)
    return q0, ks, q, k, v, s, valid


def _alibi_coefs(group, d, heads):
    first = 4 * group + 1 + pl.program_id(1) * heads
    scale = jnp.full((1, 1), -(8.0 / N_HEADS_A) * math.log(2.0), F32)
    return [jnp.exp(scale * (first + hh).astype(F32)) * float(d) for hh in range(heads)]


def _dilated_view(qkv3, group, d, heads):
    _, S, _ = qkv3.shape
    L = S // d
    per = 4 // heads
    if d == 1:
        return qkv3, pl.BlockSpec((3, L, heads * HEAD_DIM), lambda r, j: (0, 0, per * group + j))
    cols = qkv3[:, :, 512 * group:512 * (group + 1)].reshape(3, L, d * 512)
    return cols, pl.BlockSpec((3, L, heads * HEAD_DIM), lambda r, j: (0, 0, r * per + j))


def _attn_a_fwd(qkv3, group, d):
    _, S, _ = qkv3.shape
    L = S // d
    assert L % QB == 0
    side, heads = _chains(L)
    view, blocks_spec = _dilated_view(qkv3, group, d, heads)

    def body(qkv_ref, o_ref, lse_ref):
        coefs = _alibi_coefs(group, d, heads)

        def step(i, carry):
            chains = [(hh, _band_scores(qkv_ref, side * i + u, L, coefs[hh], hh))
                      for u in range(side) for hh in range(heads)]
            soft = []
            for hh, (q0, _, _, _, v, s, _) in chains:
                m = jnp.max(s, axis=-1, keepdims=True)
                p = jnp.exp(s - m)
                den = jnp.sum(p, axis=-1, keepdims=True)
                soft.append((hh, q0, (p / den).astype(BF), v, m + jnp.log(den)))
            for hh, q0, pn, v, lse in soft:
                lanes = pl.ds(hh * HEAD_DIM, HEAD_DIM)
                o_ref[pl.ds(q0, QB), lanes] = jnp.dot(pn, v, preferred_element_type=F32)
                lse_ref[pl.ds(q0, QB), lanes] = jnp.broadcast_to(lse, (QB, HEAD_DIM))
            return carry

        lax.fori_loop(0, L // QB // side, step, 0)

    per = 4 // heads
    out = pl.BlockSpec((L, heads * HEAD_DIM), lambda r, j: (0, r * per + j))
    o, lse = ORDER.call(
        body, [view], [blocks_spec],
        name=f"attn_a_fwd_d{d}", grid=(d, per),
        out_specs=[out, out],
        out_shape=[_sds((L, d * 512), F32), _sds((L, d * 512), F32)],
        compiler_params=_cparams(("parallel", "parallel")),
    )
    return o, lse


def _dilated_rows(name, arrays):
    S, W = arrays[0].shape
    tm = _tile(S, 256)
    ds_ = [d for d in DILATIONS if d > 1]
    n = len(arrays)

    def body(*refs):
        nc = W // 128
        ins, outs, scr = refs[:n], refs[n:-nc], refs[-nc:]
        for a, src in enumerate(ins):
            for c in range(nc):
                scr[c][...] = src[:, c * 128:(c + 1) * 128].astype(F32)
            for k, d in enumerate(ds_):
                dst = outs[a * len(ds_) + k]
                for r in range(d):
                    for c in range(nc):
                        at = r * W + c * 128
                        dst[:, at:at + 128] = scr[c][pl.ds(r, tm // d, stride=d), :].astype(dst.dtype)

    row = pl.BlockSpec((tm, W), lambda i: (i, 0))
    out_specs, out_shape = [], []
    for a in arrays:
        for d in ds_:
            out_specs.append(pl.BlockSpec((tm // d, d * W), lambda i: (i, 0)))
            out_shape.append(_sds((S // d, d * W), a.dtype))
    outs = ORDER.call(body, list(arrays), [row] * n, name=name, grid=(S // tm,), out_specs=out_specs,
                      out_shape=out_shape, scratch_shapes=[pltpu.VMEM((tm, 128), F32)] * (W // 128),
                      compiler_params=_cparams(("parallel",)))
    return [{d: outs[a * len(ds_) + k] for k, d in enumerate(ds_)} for a in range(n)]


def _attn_a_combine(os_, lses):
    W = 512
    S = os_[0].shape[0] * DILATIONS[0]
    tm = _tile(S, 256)
    nc = W // 128
    dilated = [g for g, d in enumerate(DILATIONS) if d > 1]

    def body(o0, o1, o2, l0, l1, l2, y_ref, lj_ref, *scr):
        def token_order(src, g, slot):
            d = DILATIONS[g]
            if d == 1:
                return src[...]
            bufs = scr[slot * nc:(slot + 1) * nc]
            for r in range(d):
                for c in range(nc):
                    at = r * W + c * 128
                    bufs[c][pl.ds(r, tm // d, stride=d), :] = src[:, at:at + 128]
            return jnp.concatenate([buf[...] for buf in bufs], axis=1)

        slots = {g: k for k, g in enumerate(dilated)}
        ls = [token_order(l, g, slots.get(g, 0)) for g, l in enumerate((l0, l1, l2))]
        os_tok = [token_order(o, g, len(dilated) + slots.get(g, 0)) for g, o in enumerate((o0, o1, o2))]
        m = jnp.maximum(jnp.maximum(ls[0], ls[1]), ls[2])
        es = [jnp.exp(l - m) for l in ls]
        den = es[0] + es[1] + es[2]
        y = (es[0] / den) * os_tok[0] + (es[1] / den) * os_tok[1] + (es[2] / den) * os_tok[2]
        y_ref[...] = y.astype(BF)
        lj_ref[...] = m + jnp.log(den)

    row = pl.BlockSpec((tm, W), lambda i: (i, 0))
    views = [pl.BlockSpec((tm // d, d * W), lambda i: (i, 0)) for d in DILATIONS]
    return ORDER.call(
        body, [*os_, *lses], views + views, name="attn_a_combine", grid=(S // tm,), out_specs=[row, row],
        out_shape=[_sds((S, W), BF), _sds((S, W), F32)],
        scratch_shapes=[pltpu.VMEM((tm, 128), F32)] * (2 * len(dilated) * nc),
        compiler_params=_cparams(("parallel",)),
    )


def _attn_a_bwd(qkv3, dy, y, lj, dqkv3, group, d):
    _, S, _ = qkv3.shape
    L = S // d
    side, heads = _chains(L)
    view, blocks_spec = _dilated_view(qkv3, group, d, heads)

    def body(qkv_ref, dy_ref, y_ref, lj_ref, *rest):
        out_ref, dk_acc, dv_acc = rest[-3:]
        coefs = _alibi_coefs(group, d, heads)
        dk_acc[...] = jnp.zeros_like(dk_acc)
        dv_acc[...] = jnp.zeros_like(dv_acc)

        def step(i, carry):
            chains = [(pl.ds(hh * HEAD_DIM, HEAD_DIM), _band_scores(qkv_ref, side * i + u, L, coefs[hh], hh))
                      for u in range(side) for hh in range(heads)]
            dys = [dy_ref[pl.ds(c[0], QB), lanes] for lanes, c in chains]
            dps = [lax.dot_general(dyv, c[4], (NT, ((), ())), preferred_element_type=F32)
                   for dyv, (_, c) in zip(dys, chains)]
            grads = []
            for (lanes, (q0, ks, q, k, v, s, valid)), dyv, dp in zip(chains, dys, dps):
                rows = pl.ds(q0, QB)
                delta = jnp.sum(dyv.astype(F32) * y_ref[rows, lanes].astype(F32), axis=-1, keepdims=True)
                p = jnp.where(valid, jnp.exp(s - jnp.tile(lj_ref[rows, lanes], (1, _key_rows(L) // HEAD_DIM))), 0.0)
                grads.append(((p * (dp - delta)).astype(BF), p.astype(BF)))
            for (lanes, (q0, ks, q, k, v, s, valid)), dyv, (ds, pb) in zip(chains, dys, grads):
                out_ref[0, pl.ds(q0, QB), lanes] = (jnp.dot(ds, k, preferred_element_type=F32) * SCALE).astype(BF)
                keys = pl.ds(ks, _key_rows(L))
                dk_acc[keys, lanes] += lax.dot_general(ds, q, (TN, ((), ())), preferred_element_type=F32) * SCALE
                dv_acc[keys, lanes] += lax.dot_general(pb, dyv, (TN, ((), ())), preferred_element_type=F32)
            return carry

        lax.fori_loop(0, L // QB // side, step, 0)
        out_ref[1] = dk_acc[...].astype(BF)
        out_ref[2] = dv_acc[...].astype(BF)

    per = 4 // heads
    width = heads * HEAD_DIM
    row = pl.BlockSpec((L, width), lambda r, j: (0, r * per + j))
    operands = [view, dy, y, lj]
    scratch = [pltpu.VMEM((L, width), F32), pltpu.VMEM((L, width), F32)]
    if d == 1:
        return ORDER.call(
            body, operands + [dqkv3], [blocks_spec, row, row, row, pl.BlockSpec(memory_space=pl.ANY)],
            name=f"attn_a_bwd_d{d}", grid=(d, per), out_specs=blocks_spec, out_shape=_sds((3, S, QKV_W), BF),
            scratch_shapes=scratch, input_output_aliases={4: 0}, compiler_params=_cparams(("parallel", "parallel")))
    out = ORDER.call(
        body, operands, [blocks_spec, row, row, row], name=f"attn_a_bwd_d{d}", grid=(d, per),
        out_specs=blocks_spec, out_shape=_sds((3, L, d * 512), BF),
        scratch_shapes=scratch, compiler_params=_cparams(("parallel", "parallel")))
    return lax.dynamic_update_slice(dqkv3, out.reshape(3, S, 512), (0, 0, 512 * group))


def _toeplitz_onehot():
    oh = np.zeros((64, GRID_W, 128), np.float32)
    for qc in range(GRID_W):
        for m in range(128):
            kc = m % GRID_W
            dc = int(np.clip(kc - qc, -(NA_COLS - 1), NA_COLS - 1)) + NA_COLS - 1
            oh[(m // GRID_W) * 32 + dc, qc, m] = 1.0
    return oh.reshape(64, GRID_W * 128)


def _nbr_scores(qkv_ref, e2_ref, r, rows, ok):
    rs = jnp.clip(r - NA_ROWS // 2, 0, rows - NA_ROWS)
    q0 = pl.multiple_of(r * GRID_W, GRID_W)
    k0 = pl.multiple_of(rs * GRID_W, GRID_W)
    q = qkv_ref[0, pl.ds(q0, GRID_W), :]
    k = qkv_ref[1, pl.ds(k0, NA_ROWS * GRID_W), :]
    v = qkv_ref[2, pl.ds(k0, NA_ROWS * GRID_W), :]
    s = lax.dot_general(q, k, (NT, ((), ())), preferred_element_type=F32) * SCALE
    first = rs - r + NA_ROWS - 1
    bias = jnp.concatenate([e2_ref[first + 2 * pair] for pair in range(NA_ROWS // 2)], axis=1)
    s = jnp.where(ok, s + bias, NEG)
    return q0, k0, first, q, k, v, s


def _nbr_col_ok():
    qc = lax.broadcasted_iota(jnp.int32, (GRID_W, NA_ROWS * GRID_W), 0)
    kc = lax.broadcasted_iota(jnp.int32, (GRID_W, NA_ROWS * GRID_W), 1) % GRID_W
    cs = jnp.clip(qc - NA_COLS // 2, 0, GRID_W - NA_COLS)
    return (kc >= cs) & (kc < cs + NA_COLS)


def _attn_b_fwd(qkv3, e2):
    _, S, _ = qkv3.shape
    rows = S // GRID_W
    assert rows >= NA_ROWS

    def body(qkv_ref, e2_ref, o_ref, lse_ref):
        ok = _nbr_col_ok()

        def step(i, carry):
            blocks = [_nbr_scores(qkv_ref, e2_ref, NBR_SIDE * i + u, rows, ok) for u in range(NBR_SIDE)]
            soft = []
            for q0, _, _, _, _, v, s in blocks:
                m = jnp.max(s, axis=-1, keepdims=True)
                p = jnp.exp(s - m)
                den = jnp.sum(p, axis=-1, keepdims=True)
                soft.append((q0, (p / den).astype(BF), v, m + jnp.log(den)))
            for q0, pn, v, lse in soft:
                o_ref[pl.ds(q0, GRID_W), :] = jnp.dot(pn, v, preferred_element_type=F32).astype(BF)
                lse_ref[pl.ds(q0, GRID_W), :] = jnp.broadcast_to(lse, (GRID_W, HEAD_DIM))
            return carry

        lax.fori_loop(0, rows // NBR_SIDE, step, 0)

    out = pl.BlockSpec((S, HEAD_DIM), lambda h: (0, h))
    return ORDER.call(
        body, [qkv3, e2],
        [pl.BlockSpec((3, S, HEAD_DIM), lambda h: (0, 0, N_HEADS_A + h)),
         pl.BlockSpec((None, RPB_ROWS - 1, GRID_W, 128), lambda h: (h, 0, 0, 0))],
        name="attn_b_fwd", grid=(4,),
        out_specs=[out, out], out_shape=[_sds((S, 512), BF), _sds((S, 512), F32)],
        compiler_params=_cparams(("parallel",)),
    )


def _attn_b_bwd(qkv3, e2, dy, y, lse, dqkv3):
    _, S, _ = qkv3.shape
    rows = S // GRID_W
    nk = NA_ROWS * GRID_W

    def body(qkv_ref, e2_ref, dy_ref, y_ref, lse_ref, _, out_ref, de2_ref, dk_acc, dv_acc):
        ok = _nbr_col_ok()
        dk_acc[...] = jnp.zeros_like(dk_acc)
        dv_acc[...] = jnp.zeros_like(dv_acc)
        de2_ref[...] = jnp.zeros_like(de2_ref)

        def step(i, carry):
            blocks = [_nbr_scores(qkv_ref, e2_ref, NBR_SIDE * i + u, rows, ok) for u in range(NBR_SIDE)]
            dys = [dy_ref[pl.ds(b[0], GRID_W), :] for b in blocks]
            dps = [lax.dot_general(dyv, b[5], (NT, ((), ())), preferred_element_type=F32) for dyv, b in zip(dys, blocks)]
            grads = []
            for (q0, k0, first, q, k, v, s), dyv, dp in zip(blocks, dys, dps):
                qrows = pl.ds(q0, GRID_W)
                delta = jnp.sum(dyv.astype(F32) * y_ref[qrows, :].astype(F32), axis=-1, keepdims=True)
                p = jnp.where(ok, jnp.exp(s - jnp.tile(lse_ref[qrows, :], (1, nk // HEAD_DIM))), 0.0)
                ds = p * (dp - delta)
                for pair in range(NA_ROWS // 2):
                    de2_ref[first + 2 * pair] += ds[:, pair * 128:(pair + 1) * 128]
                grads.append((ds.astype(BF), p.astype(BF)))
            for (q0, k0, first, q, k, v, s), dyv, (dsb, pb) in zip(blocks, dys, grads):
                out_ref[0, pl.ds(q0, GRID_W), :] = (jnp.dot(dsb, k, preferred_element_type=F32) * SCALE).astype(BF)
                keys = pl.ds(k0, nk)
                dk_acc[keys, :] += lax.dot_general(dsb, q, (TN, ((), ())), preferred_element_type=F32) * SCALE
                dv_acc[keys, :] += lax.dot_general(pb, dyv, (TN, ((), ())), preferred_element_type=F32)
            return carry

        lax.fori_loop(0, rows // NBR_SIDE, step, 0)
        out_ref[1] = dk_acc[...].astype(BF)
        out_ref[2] = dv_acc[...].astype(BF)

    heads = pl.BlockSpec((3, S, HEAD_DIM), lambda h: (0, 0, N_HEADS_A + h))
    row = pl.BlockSpec((S, HEAD_DIM), lambda h: (0, h))
    table = pl.BlockSpec((None, RPB_ROWS - 1, GRID_W, 128), lambda h: (h, 0, 0, 0))
    return ORDER.call(
        body, [qkv3, e2, dy, y, lse, dqkv3],
        [heads, table, row, row, row, pl.BlockSpec(memory_space=pl.ANY)], name="attn_b_bwd", grid=(4,),
        out_specs=[heads, table],
        out_shape=[_sds((3, S, QKV_W), BF), _sds((4, RPB_ROWS - 1, GRID_W, 128), F32)],
        scratch_shapes=[pltpu.VMEM((S, HEAD_DIM), F32), pltpu.VMEM((S, HEAD_DIM), F32)],
        input_output_aliases={5: 0},
        compiler_params=_cparams(("parallel",)), chain_output=1,
    )


def _rpb_to_table(rpb):
    pad = jnp.pad(rpb, ((0, 0), (0, 0), (0, 1)))
    pairs = jnp.concatenate([pad[:, :-1], pad[:, 1:]], axis=-1).reshape(4 * (RPB_ROWS - 1), 64)
    onehot = jnp.asarray(_toeplitz_onehot())
    n = onehot.shape[1]
    tn = 2048
    full = lambda i, j, k: (0, 0)
    (e2,) = _matmul("rpb_table", pairs, onehot, pl.BlockSpec(pairs.shape, full),
                    pl.BlockSpec((64, tn), lambda i, j, k: (0, j)), NN, (1, n // tn, 1), (pairs.shape[0], tn), [],
                    [(_sds((pairs.shape[0], n), F32), pl.BlockSpec((pairs.shape[0], tn), lambda i, j, k: (0, j)))],
                    _store(F32), precision=lax.Precision.HIGHEST)
    return e2.reshape(4, RPB_ROWS - 1, GRID_W, 128)


def _table_grad_to_rpb(de2):
    onehot = jnp.asarray(_toeplitz_onehot())
    n = onehot.shape[1]
    flat = de2.reshape(4 * (RPB_ROWS - 1), n)
    tk = 2048
    (dpairs,) = _matmul("rpb_table_grad", flat, onehot, pl.BlockSpec((flat.shape[0], tk), lambda i, j, k: (0, k)),
                        pl.BlockSpec((64, tk), lambda i, j, k: (0, k)), NT, (1, 1, n // tk), (flat.shape[0], 64), [],
                        [(_sds((flat.shape[0], 64), F32), pl.BlockSpec((flat.shape[0], 64), lambda i, j, k: (0, 0)))],
                        _store(F32), precision=lax.Precision.HIGHEST)
    dpairs = dpairs.reshape(4, RPB_ROWS - 1, 64)
    zero = jnp.zeros((4, 1, RPB_COLS), F32)
    return (jnp.concatenate([dpairs[:, :, :RPB_COLS], zero], axis=1)
            + jnp.concatenate([zero, dpairs[:, :, 32:32 + RPB_COLS]], axis=1))


HBM = pl.BlockSpec(memory_space=pl.ANY)


def _place():
    x, y, c = lax.axis_index("x"), lax.axis_index("y"), lax.axis_index("c")
    chips = [(1 - x, y), (x, 1 - y), (1 - x, 1 - y)]
    return x, y, c, chips


def _remote(src, dst, send_sem, recv_sem, to):
    return pltpu.make_async_remote_copy(src_ref=src, dst_ref=dst, send_sem=send_sem, recv_sem=recv_sem,
                                        device_id=to, device_id_type=MESH)


def _place_shard(name, w, me, plain=False):
    R, C = w.shape
    tr = _tile(R, 256)

    def body(me_ref, w_ref, *o_refs):
        for o_ref in o_refs:
            o_ref[...] = w_ref[...].astype(BF)

    row = pl.BlockSpec((tr, C), lambda i, mr: (i, 0))
    placed = pl.BlockSpec((None, tr, C), lambda i, mr: (mr[0], i, 0))
    return ORDER.call(
        body, [w], [row], prefetch=(me,), name=name, grid=(R // tr,),
        out_specs=[placed, row] if plain else [placed],
        out_shape=[_sds((N_CHIPS, R, C), BF)] + ([_sds((R, C), BF)] if plain else []),
        compiler_params=_cparams(("parallel",)),
    )


SEM = pl.BlockSpec(memory_space=pltpu.SEMAPHORE)
IN_HBM = pl.BlockSpec(memory_space=pltpu.HBM)
DATAFLOW = pltpu.SideEffectType.DATAFLOW_SIDE_EFFECTING


def _in_hbm(a):
    return pltpu.with_memory_space_constraint(a, pltpu.HBM)


def _copy_start(name, bufs, copies, n_copies, earlier=None):
    n = len(bufs)
    after = None if any(b is ORDER.last for b in bufs) else ORDER.last
    n_extra = (2 if earlier is not None else 0) + (1 if after is not None else 0)

    def body(*refs):
        ins = refs[:n]
        if earlier is not None:
            for k, (src, dst, to) in enumerate(earlier[0](ins)):
                cp = _remote(src, dst, refs[n].at[k], refs[n + 1].at[k], to)
                cp.wait_send()
                cp.wait_recv()
        send_sems, recv_sems = refs[n + n_extra], refs[n + n_extra + 1]
        for k, (src, dst, to) in enumerate(copies(ins)):
            _remote(src, dst, send_sems.at[k], recv_sems.at[k], to).start()
        refs[-1][...] = jnp.zeros((8, 128), F32)

    operands = [_in_hbm(b) for b in bufs]
    in_specs = [IN_HBM] * n
    if earlier is not None:
        operands += [earlier[1], earlier[2]]
        in_specs += [SEM, SEM]
    if after is not None:
        operands.append(after)
        in_specs.append(HBM)
    outs = pl.pallas_call(
        body, name=name,
        out_shape=(pltpu.SemaphoreType.DMA((n_copies,)), pltpu.SemaphoreType.DMA((n_copies,)),
                   *[pltpu.HBM(b.shape, b.dtype) for b in bufs], _sds((8, 128), F32)),
        in_specs=in_specs,
        out_specs=(SEM, SEM, *[IN_HBM] * n, pl.BlockSpec(memory_space=pltpu.VMEM)),
        input_output_aliases={i: 2 + i for i in range(n)},
        compiler_params=pltpu.CompilerParams(has_side_effects=DATAFLOW),
    )(*operands)
    ORDER.last = outs[-1]
    return outs[0], outs[1], list(outs[2:2 + n])


def _copy_wait(name, bufs, copies, send_sems, recv_sems):
    n = len(bufs)
    after = ORDER.last

    def body(*refs):
        ins = refs[:n]
        for k, (src, dst, to) in enumerate(copies(ins)):
            cp = _remote(src, dst, refs[n].at[k], refs[n + 1].at[k], to)
            cp.wait_send()
            cp.wait_recv()

    outs = list(pl.pallas_call(
        body, name=name,
        out_shape=tuple(pltpu.HBM(b.shape, b.dtype) for b in bufs),
        in_specs=[IN_HBM] * n + [SEM, SEM, HBM], out_specs=tuple([IN_HBM] * n),
        input_output_aliases={i: i for i in range(n)},
        compiler_params=pltpu.CompilerParams(has_side_effects=DATAFLOW),
    )(*bufs, send_sems, recv_sems, after))
    ORDER.last = outs[0]
    return outs


def _gather_hop1(bufs):
    x, y, c, chips = _place()
    out = []
    for b in bufs:
        half = b.shape[1] // 2
        mine = b.at[2 * x + y, pl.ds(c * half, half), :]
        out += [(mine, mine, (*chip, c)) for chip in chips]
    return out


def _gather_hop2(bufs):
    x, y, c, chips = _place()
    out = []
    for b in bufs:
        half = b.shape[1] // 2
        for chip in chips:
            landed = b.at[2 * chip[0] + chip[1], pl.ds(c * half, half), :]
            out.append((landed, landed, (x, y, 1 - c)))
    return out


def _swap_copies(bufs):
    x, y, c, _ = _place()
    n = len(bufs) // 2
    out = []
    for p, land in zip(bufs[:n], bufs[n:]):
        half = p.shape[1] // 2
        out.append((p.at[:, pl.ds((1 - c) * half, half), :], land, (x, y, 1 - c)))
    return out


def _scatter_copies(bufs):
    _, _, c, chips = _place()
    n = len(bufs) // 2
    out = []
    for s_, land in zip(bufs[:n], bufs[n:]):
        out += [(s_.at[2 * chip[0] + chip[1]], land.at[j], (*chip, c)) for j, chip in enumerate(chips)]
    return out


def _join_copies(bufs):
    x, y, c, _ = _place()
    out = []
    for b in bufs:
        half = b.shape[0] // 2
        mine = b.at[pl.ds(c * half, half), :]
        out.append((mine, mine, (x, y, 1 - c)))
    return out


def _gather_small(vec):
    m_per, n = vec.shape

    def body(x_ref, out_ref, send_sems, recv_sems, local_sem):
        x, y, c, chips = _place()
        me, sibling = (x, y, c), (x, y, 1 - c)

        def rows(px, py, pc):
            return out_ref.at[pl.ds((4 * px + 2 * py + pc) * m_per, m_per), :]

        def copy(k, block, to, src=None):
            return _remote(rows(*block) if src is None else src, rows(*block), send_sems.at[k], recv_sems.at[k], to)

        mine = pltpu.make_async_copy(x_ref, rows(*me), local_sem)
        mine.start()
        first = [copy(0, me, sibling, src=x_ref)]
        first += [copy(1 + j, me, (*chip, c), src=x_ref) for j, chip in enumerate(chips)]
        for cp in first:
            cp.start()
        passed = [copy(4 + j, (*chip, c), sibling) for j, chip in enumerate(chips)]
        for j, chip in enumerate(chips):
            copy(1 + j, (*chip, c), me).wait_recv()
            passed[j].start()
        copy(0, sibling, me).wait_recv()
        for j, chip in enumerate(chips):
            copy(4 + j, (*chip, 1 - c), me).wait_recv()
        for cp in first + passed:
            cp.wait_send()
        mine.wait()

    return ORDER.call(
        body, [vec], [pl.BlockSpec(memory_space=pltpu.VMEM)], name="gather_small_grads",
        out_shape=_sds((8 * m_per, n), vec.dtype), out_specs=pl.BlockSpec(memory_space=pltpu.VMEM),
        scratch_shapes=[pltpu.SemaphoreType.DMA((7,)), pltpu.SemaphoreType.DMA((7,)), pltpu.SemaphoreType.DMA],
    )


def _add_sibling(name, partial, received, c):
    _, R, C = partial.shape
    half = R // 2
    tr = _tile(half, 256)
    nb = half // tr

    def body(c_ref, p_ref, r_ref, o_ref):
        o_ref[...] = (p_ref[...].astype(F32) + r_ref[...].astype(F32)).astype(BF)

    return ORDER.call(
        body, [partial, received],
        [pl.BlockSpec((None, tr, C), lambda j, i, cr: (j, cr[0] * nb + i, 0)),
         pl.BlockSpec((None, tr, C), lambda j, i, cr: (j, i, 0))],
        prefetch=(c,), name=name, grid=(N_CHIPS, nb),
        out_specs=pl.BlockSpec((None, tr, C), lambda j, i, cr: (j, i, 0)),
        out_shape=_sds((N_CHIPS, half, C), BF), compiler_params=_cparams(("parallel", "parallel")),
    )


def _add_chips(name, sums, received, me_c):
    _, half, C = sums.shape
    tr = _tile(half, 256)
    nb = half // tr

    def body(mc_ref, s_ref, r_ref, o_ref):
        acc = s_ref[...].astype(F32)
        for j in range(3):
            acc = acc + r_ref[j].astype(F32)
        o_ref[...] = acc

    return ORDER.call(
        body, [sums, received],
        [pl.BlockSpec((None, tr, C), lambda i, mc: (mc[0], i, 0)),
         pl.BlockSpec((3, tr, C), lambda i, mc: (0, i, 0))],
        prefetch=(me_c,), name=name, grid=(nb,),
        out_specs=pl.BlockSpec((tr, C), lambda i, mc: (mc[1] * nb + i, 0)),
        out_shape=_sds((2 * half, C), F32), compiler_params=_cparams(("parallel",)),
    )


def _adamw_math(w, g, m, v):
    m = ADAM_B1 * m + (1.0 - ADAM_B1) * g
    v = ADAM_B2 * v + (1.0 - ADAM_B2) * (g * g)
    m_hat = m / (1.0 - ADAM_B1 ** ADAM_STEP)
    v_hat = v / (1.0 - ADAM_B2 ** ADAM_STEP)
    delta = -ADAM_LR * (m_hat / (jnp.sqrt(v_hat) + ADAM_EPS) + ADAM_WD * w)
    return delta, m, v


def _adamw(name, w, g, m, v):
    R, C = w.shape
    tr = _tile(R, 256)

    def body(w_ref, g_ref, m_ref, v_ref, go_ref, d_ref, mo_ref, vo_ref):
        gv = g_ref[...]
        go_ref[...] = gv
        d_ref[...], mo_ref[...], vo_ref[...] = _adamw_math(w_ref[...], gv, m_ref[...], v_ref[...])

    row = pl.BlockSpec((tr, C), lambda i: (i, 0))
    return ORDER.call(
        body, [w, g, m, v], [row] * 4, name=name, grid=(R // tr,), out_specs=[row] * 4,
        out_shape=[_sds((R, C), F32)] * 4, compiler_params=_cparams(("parallel",)), chain_output=1,
    )


def _adamw_small(gathered, w, m, v):
    rows, n = w.shape

    def body(ga_ref, w_ref, m_ref, v_ref, go_ref, d_ref, mo_ref, vo_ref):
        g = ga_ref[pl.ds(0, rows), :]
        for dev in range(1, 8):
            g = g + ga_ref[pl.ds(dev * rows, rows), :]
        go_ref[...] = g
        d_ref[...], mo_ref[...], vo_ref[...] = _adamw_math(w_ref[...], g, m_ref[...], v_ref[...])

    whole = pl.BlockSpec(memory_space=pltpu.VMEM)
    return ORDER.call(
        body, [gathered, w, m, v], [whole] * 4, name="adamw_small", out_specs=[whole] * 4,
        out_shape=[_sds((rows, n), F32)] * 4, compiler_params=_cparams(), chain_output=1,
    )


def _proj_merge(y_a, y_b, gpa, gpb, g3):
    S, K = y_a.shape
    _, _, Nq = gpa.shape
    D = N_CHIPS * Nq
    tm, tn = _tile(S, 1024), _tile(Nq, 512)
    q = Nq // tn

    def body(ya_ref, yb_ref, wa_ref, wb_ref, g_ref, merged_ref, c_ref):
        pa = jnp.dot(ya_ref[...], wa_ref[...], preferred_element_type=F32)
        pb = jnp.dot(yb_ref[...], wb_ref[...], preferred_element_type=F32)
        g = g_ref[...].astype(F32)
        merged_ref[...] = (g[0] * pa + g[1] * pb).astype(BF)
        c_ref[0] = (pa * g[0] * (1.0 - g[0])).astype(BF)
        c_ref[1] = (pb * g[1] * (1.0 - g[1])).astype(BF)

    rows = pl.BlockSpec((tm, K), lambda i, j: (i, 0))
    weight = pl.BlockSpec((None, K, tn), lambda i, j: (j // q, 0, j % q))
    pair = pl.BlockSpec((2, tm, tn), lambda i, j: (0, i, j))
    return ORDER.call(
        body, [y_a, y_b, gpa, gpb, g3], [rows, rows, weight, weight, pair], name="proj_merge",
        grid=(S // tm, N_CHIPS * q), out_specs=[pl.BlockSpec((tm, tn), lambda i, j: (i, j)), pair],
        out_shape=[_sds((S, D), BF), _sds((2, S, D), BF)], compiler_params=_cparams(("parallel", "parallel")))


class _Exchange:
    GATHER = (("qkv",), ("gate", "proj_a", "proj_b", "out"), ("up", "down"))
    REDUCE = {"mlp": ("down", "up"), "mix": ("out", "proj_a", "proj_b"), "in": ("qkv", "gate")}

    OWN_FIRST = ("qkv", "gate")

    def __init__(self, shards, me, c):
        self.me, self.c = me, c
        self.hop1, self.hop2, self.stage, self.grads, self.own = {}, {}, {}, {}, {}
        for g, names in enumerate(self.GATHER):
            bufs = []
            for n in names:
                placed = _place_shard(f"place_{n}", shards[n], me, plain=n in self.OWN_FIRST)
                bufs.append(placed[0])
                if n in self.OWN_FIRST:
                    self.own[n] = placed[1]
            self.hop1[g] = _copy_start(f"gather{g}_start", bufs, _gather_hop1, 3 * len(names))

    def forward(self, g):
        send, recv, thru = self.hop1.pop(g)
        self.hop2[g] = _copy_start(f"gather{g}_forward", thru, _gather_hop2, len(thru) * 3,
                                   earlier=(_gather_hop1, send, recv))

    def weights(self, g):
        send, recv, thru = self.hop2.pop(g)
        return _copy_wait(f"gather{g}_wait", thru, _gather_hop2, send, recv)

    def reduce(self, key, partials=None):
        names = self.REDUCE[key]
        n = len(names)
        if partials is not None:
            lands = [lax.empty((p.shape[0], p.shape[1] // 2, p.shape[2]), p.dtype) for p in partials]
            self.stage[key] = ("swap",) + _copy_start(f"reduce_{key}_swap", list(partials) + lands, _swap_copies, n)
            return
        kind, send, recv, thru = self.stage.pop(key)
        if kind == "swap":
            thru = _copy_wait(f"reduce_{key}_swap_wait", thru, _swap_copies, send, recv)
            sums = [_add_sibling(f"reduce_{nm}_add_sibling", p, r, self.c)
                    for nm, p, r in zip(names, thru[:n], thru[n:])]
            lands = [lax.empty((3,) + s_.shape[1:], s_.dtype) for s_ in sums]
            self.stage[key] = ("scatter",) + _copy_start(f"reduce_{key}_scatter", sums + lands, _scatter_copies, 3 * n)
        elif kind == "scatter":
            thru = _copy_wait(f"reduce_{key}_scatter_wait", thru, _scatter_copies, send, recv)
            me_c = jnp.concatenate([self.me, self.c])
            halves = [_add_chips(f"reduce_{nm}_add_chips", s_, r, me_c)
                      for nm, s_, r in zip(names, thru[:n], thru[n:])]
            self.stage[key] = ("join",) + _copy_start(f"reduce_{key}_join", halves, _join_copies, n)
        else:
            thru = _copy_wait(f"reduce_{key}_join_wait", thru, _join_copies, send, recv)
            self.grads.update(zip(names, thru))


def _forward_backward(x, target, norm_mix, b_gate, rpb, norm_mlp, norm_final, ex):
    S, D = x.shape

    h1 = _rms_fwd("rms_mix", x, norm_mix)
    nq = QKV_W // 512
    qkv_out = (((3, S, QKV_W), BF), lambda i, T: (T // nq, i, T % nq))
    tg = _tile(ex.own["gate"].shape[1], 1024)
    ng = D // tg
    gate_out = (((2, S, D), BF), lambda i, T: (T // ng, i, T % ng))

    def gate_epilogue(acc, ex_, outs):
        outs[0][...] = jax.nn.sigmoid(acc + ex_[0][...]).astype(BF)

    qkv3 = _mm_nn_shards("qkv_own", h1, ex.own["qkv"], ex.me, True, *qkv_out, _store(BF))
    g3 = _mm_nn_shards("gate_own", h1, ex.own["gate"], ex.me, True, *gate_out, gate_epilogue, extras=[b_gate], tn=tg)
    ex.forward(0)
    e2 = _rpb_to_table(rpb)
    (gq,) = ex.weights(0)
    qkv3 = _mm_nn_shards("qkv", h1, gq, ex.me, False, *qkv_out, _store(BF), into=qkv3)

    ex.forward(1)
    outs_a = [_attn_a_fwd(qkv3, 0, DILATIONS[0])]
    gg, gpa, gpb, gout = ex.weights(1)
    wout = gout.reshape(D, D)

    g3 = _mm_nn_shards("gate", h1, gg, ex.me, False, *gate_out, gate_epilogue, extras=[b_gate], into=g3, tn=tg)

    outs_a += [_attn_a_fwd(qkv3, grp, d) for grp, d in enumerate(DILATIONS) if grp > 0]
    y_a, lj = _attn_a_combine([o for o, _ in outs_a], [l for _, l in outs_a])
    y_b, lse_b = _attn_b_fwd(qkv3, e2)
    merged, c3 = _proj_merge(y_a, y_b, gpa, gpb, g3)

    def residual_epilogue(acc, ex_, outs):
        outs[0][...] = acc + ex_[0][...]

    def nn_plain(name, a, w, res):
        M, K = a.shape
        N = w.shape[1]
        bm, bn, bk = _tile(M, 1024), _tile(N, 1024), _tile(K, 2048)
        t = pl.BlockSpec((bm, bn), lambda i, j, k: (i, j))
        return _matmul(name, a, w, pl.BlockSpec((bm, bk), lambda i, j, k: (i, k)),
                       pl.BlockSpec((bk, bn), lambda i, j, k: (k, j)), NN, (M // bm, N // bn, K // bk), (bm, bn),
                       [(res, t)], [(_sds((M, N), F32), t)], residual_epilogue)[0]

    ex.forward(2)
    x1 = nn_plain("out_proj", merged, wout, x)
    h2 = _rms_fwd("rms_mlp", x1, norm_mlp)
    gup, gdown = ex.weights(2)
    F = gup.shape[2] * N_CHIPS
    wdown = gdown.reshape(F, D)

    def up_epilogue(acc, ex_, outs):
        ru = jnp.maximum(acc, 0.0)
        outs[0][...] = (ru * ru).astype(BF)
        outs[1][...] = ru.astype(BF)

    tu = _tile(gup.shape[2], 1024)
    ut = pl.BlockSpec((_tile(S, 1024), tu), lambda i, j, k: (i, j))
    (act, ru), _ = _mm_nn_cols("mlp_up", h2, gup, BF, epilogue=up_epilogue, tn=tu,
                               outs=[(_sds((S, F), BF), ut), (_sds((S, F), BF), ut)])
    x2 = nn_plain("mlp_down", act, wdown, x1)

    loss, dx2, dx2b, d_norm_final = _loss_head(x2, target, norm_final.reshape(1, D))

    def nt_rows(name, a, w, epilogue, extras, outs, bn=1024):
        M, N = a.shape
        K = w.shape[0]
        bm, bn, bk = _tile(M, 1024), _tile(K, bn), _tile(N, 2048)
        return _matmul(name, a, w, pl.BlockSpec((bm, bk), lambda i, j, k: (i, k)),
                       pl.BlockSpec((bn, bk), lambda i, j, k: (j, k)), NT, (M // bm, K // bn, N // bk), (bm, bn),
                       extras(bm, bn), outs(bm, bn), epilogue)

    def nt_cols(name, a_spec_fn, a, g, M, epilogue, extras, outs, bk):
        _, K, Nq = g.shape
        bm, bn, bk = _tile(M, 1024), _tile(K, 1024), _tile(Nq, bk)
        q = Nq // bk
        return _matmul(name, a, g, a_spec_fn(bm, bk), pl.BlockSpec((None, bn, bk), lambda i, j, k: (k // q, j, k % q)),
                       NT, (M // bm, K // bn, N_CHIPS * q), (bm, bn), extras(bm, bn), outs(bm, bn), epilogue)

    def tn_grad(name, a, a_spec_fn, b, b_spec_fn, Kin, N, out_shape, out_spec_fn, bn=1024):
        bm, bn, bk = _tile(Kin, 1024), _tile(N, bn), _tile(S, 2048)
        return _matmul(name, a, b, a_spec_fn(bk, bm), b_spec_fn(bk, bn), TN, (Kin // bm, N // bn, S // bk), (bm, bn),
                       [], [(_sds(out_shape, BF), out_spec_fn(bm, bn))], _store(BF))[0]

    plain_a = lambda bk, bm: pl.BlockSpec((bk, bm), lambda i, j, k: (k, i))
    plain_b = lambda bk, bn: pl.BlockSpec((bk, bn), lambda i, j, k: (k, j))
    plain_o = lambda bm, bn: pl.BlockSpec((bm, bn), lambda i, j, k: (i, j))
    a_rows = lambda bm, bk: pl.BlockSpec((bm, bk), lambda i, j, k: (i, k))

    def cols_o(Nq):
        def spec(bm, bn):
            q = Nq // bn
            return pl.BlockSpec((None, bm, bn), lambda i, j, k: (j // q, i, j % q))
        return spec

    def du_epilogue(acc, ex_, outs):
        outs[0][...] = (acc * (2.0 * ex_[0][...].astype(F32))).astype(BF)

    dw_down = tn_grad("mlp_down_dw", act, plain_a, dx2b, plain_b, F, D, (F, D), plain_o)
    (du,) = nt_rows("mlp_down_dx", dx2b, wdown, du_epilogue,
                    lambda bm, bn: [(ru, plain_o(bm, bn))], lambda bm, bn: [(_sds((S, F), BF), plain_o(bm, bn))])

    fq = gup.shape[2]
    dw_up = tn_grad("mlp_up_dw", h2, plain_a, du, plain_b, D, F, (N_CHIPS, D, fq), cols_o(fq), bn=min(fq, 1024))
    ex.reduce("mlp", partials=[dw_down.reshape(N_CHIPS, F // N_CHIPS, D), dw_up])
    (dh2,) = nt_cols("mlp_up_dx", a_rows, du, gup, S, _store(F32), lambda bm, bn: [],
                     lambda bm, bn: [(_sds((S, D), F32), plain_o(bm, bn))], 2048)
    ex.reduce("mlp")
    dx1, dx1b, d_norm_mlp = _rms_bwd("rms_mlp_bwd", dh2, x1, norm_mlp, dx2)

    def merge_bwd_epilogue(acc, ex_, outs):
        g, c = ex_[0][...].astype(F32), ex_[1][...].astype(F32)
        outs[0][...] = (acc * g[0]).astype(BF)
        outs[1][...] = (acc * g[1]).astype(BF)
        dga = acc * c[0]
        dgb = acc * c[1]
        outs[2][0] = dga.astype(BF)
        outs[2][1] = dgb.astype(BF)
        outs[3][...] = jnp.concatenate([jnp.sum(dga, axis=0, keepdims=True), jnp.sum(dgb, axis=0, keepdims=True)], 0)

    def pair(bm, bn):
        return pl.BlockSpec((2, bm, bn), lambda i, j, k: (0, i, j))

    n_row_blocks = S // _tile(S, 1024)
    dpa, dpb, dg3, db_gate = nt_rows(
        "out_proj_dx", dx1b, wout, merge_bwd_epilogue,
        lambda bm, bn: [(g3, pair(bm, bn)), (c3, pair(bm, bn))],
        lambda bm, bn: [(_sds((S, D), BF), plain_o(bm, bn)), (_sds((S, D), BF), plain_o(bm, bn)),
                        (_sds((2, S, D), BF), pair(bm, bn)),
                        (_sds((n_row_blocks, 2, D), F32), pl.BlockSpec((None, 2, bn), lambda i, j, k: (i, 0, j)))],
        bn=512)
    dw_out = tn_grad("out_proj_dw", merged, plain_a, dx1b, plain_b, D, D, (D, D), plain_o)

    pq = gpa.shape[2]
    proj_dx = lambda name, dproj, g: nt_cols(name, a_rows, dproj, g, S, _store(BF), lambda bm, bn: [],
                                             lambda bm, bn: [(_sds((S, 512), BF), plain_o(bm, bn))], 512)[0]
    dw_pa = tn_grad("proj_a_dw", y_a, plain_a, dpa, plain_b, 512, D, (N_CHIPS, 512, pq), cols_o(pq), bn=min(pq, 512))
    dw_pb = tn_grad("proj_b_dw", y_b, plain_a, dpb, plain_b, 512, D, (N_CHIPS, 512, pq), cols_o(pq), bn=min(pq, 512))
    ex.reduce("mix", partials=[dw_out.reshape(N_CHIPS, D // N_CHIPS, D), dw_pa, dw_pb])
    dy_a = proj_dx("proj_a_dx", dpa, gpa)
    dy_b = proj_dx("proj_b_dx", dpb, gpb)

    dqkv3 = lax.empty((3, S, QKV_W), BF)
    dqkv3 = _attn_a_bwd(qkv3, dy_a, y_a, lj, dqkv3, 0, DILATIONS[0])
    ex.reduce("mix")
    dy_views, y_views, lj_views = _dilated_rows("attn_a_bwd_rows", [dy_a, y_a, lj])
    for grp, d in enumerate(DILATIONS):
        if grp > 0:
            dqkv3 = _attn_a_bwd(qkv3, dy_views[d], y_views[d], lj_views[d], dqkv3, grp, d)
    dqkv3, de2 = _attn_b_bwd(qkv3, e2, dy_b, y_b, lse_b, dqkv3)
    d_rpb = _table_grad_to_rpb(de2)

    def stacked_a(width):
        def spec(bm, bk):
            q = width // bk
            return pl.BlockSpec((None, bm, bk), lambda i, j, k: (k // q, i, k % q))
        return spec

    def stacked_b(width):
        def spec(bk, bn):
            q = width // bn
            return pl.BlockSpec((None, bk, bn), lambda i, j, k: (j // q, k, j % q))
        return spec

    dw_qkv = tn_grad("qkv_dw", h1, plain_a, dqkv3, stacked_b(QKV_W), D, 3 * QKV_W, (N_CHIPS,) + gq.shape[1:],
                     cols_o(gq.shape[2]), bn=512)
    dw_gate = tn_grad("gate_dw", h1, plain_a, dg3, stacked_b(D), D, 2 * D, (N_CHIPS,) + gg.shape[1:],
                      cols_o(gg.shape[2]), bn=gg.shape[2])
    ex.reduce("in", partials=[dw_qkv, dw_gate])
    ex.reduce("mlp")
    (dh1_q,) = nt_cols("qkv_dx", stacked_a(QKV_W), dqkv3, gq, S, _store(F32), lambda bm, bn: [],
                       lambda bm, bn: [(_sds((S, D), F32), plain_o(bm, bn))], 512)
    ex.reduce("in")
    ex.reduce("mix")

    def add_epilogue(acc, ex_, outs):
        outs[0][...] = acc + ex_[0][...]

    (dh1,) = nt_cols("gate_dx", stacked_a(D), dg3, gg, S, add_epilogue, lambda bm, bn: [(dh1_q, plain_o(bm, bn))],
                     lambda bm, bn: [(_sds((S, D), F32), plain_o(bm, bn))], gg.shape[2])
    grad_x, _, d_norm_mix = _rms_bwd("rms_mix_bwd", dh1, x, norm_mix, dx1)
    ex.reduce("mlp")
    ex.reduce("mix")

    small = [d_norm_mix, jnp.sum(db_gate, axis=0).reshape(1, 2 * D), d_rpb, d_norm_mlp, d_norm_final]
    return loss, grad_x, small


def _pack_small(parts, width):
    flat = jnp.concatenate([p.reshape(-1) for p in parts])
    return jnp.pad(flat, (0, 8 * width - flat.shape[0])).reshape(8, width)


def kernel(x, norm_mix, w_qkv, w_gate, b_gate, rpb, w_proj_a, w_proj_b, w_out, norm_mlp, w_up, w_down, norm_final, loss_target, m_norm_mix, m_w_qkv, m_w_gate, m_b_gate, m_rpb, m_w_proj_a, m_w_proj_b, m_w_out, m_norm_mlp, m_w_up, m_w_down, m_norm_final, v_norm_mix, v_w_qkv, v_w_gate, v_b_gate, v_rpb, v_w_proj_a, v_w_proj_b, v_w_out, v_norm_mlp, v_w_up, v_w_down, v_norm_final):
    names = ["qkv", "gate", "proj_a", "proj_b", "out", "up", "down"]
    big = dict(zip(names, [w_qkv[0], w_gate[0], w_proj_a[0], w_proj_b[0], w_out[0], w_up[0], w_down[0]]))
    big_m = dict(zip(names, [m_w_qkv[0], m_w_gate[0], m_w_proj_a[0], m_w_proj_b[0], m_w_out[0], m_w_up[0], m_w_down[0]]))
    big_v = dict(zip(names, [v_w_qkv[0], v_w_gate[0], v_w_proj_a[0], v_w_proj_b[0], v_w_out[0], v_w_up[0], v_w_down[0]]))

    c = lax.axis_index("c").astype(jnp.int32).reshape(1)
    me = (2 * lax.axis_index("x") + lax.axis_index("y")).astype(jnp.int32).reshape(1)
    ORDER.last = None
    ex = _Exchange(big, me, c)
    loss, grad_x, small = _forward_backward(x[0], loss_target[0], norm_mix, b_gate, rpb[0], norm_mlp, norm_final, ex)

    def adamw(group):
        return {n: _adamw(f"adamw_{n}", big[n], ex.grads[n], big_m[n], big_v[n]) for n in _Exchange.REDUCE[group]}

    big_out = {**adamw("mlp"), **adamw("mix")}
    ex.reduce("in")

    small_w = [norm_mix, b_gate, rpb, norm_mlp, norm_final]
    count = sum(int(np.prod(p.shape)) for p in small_w)
    width = -(-count // (8 * 128)) * 128
    packed = _adamw_small(_gather_small(_pack_small(small, width)), _pack_small(small_w, width),
                          _pack_small([m_norm_mix, m_b_gate, m_rpb, m_norm_mlp, m_norm_final], width),
                          _pack_small([v_norm_mix, v_b_gate, v_rpb, v_norm_mlp, v_norm_final], width))
    ex.reduce("in")
    big_out.update(adamw("in"))

    def unpack(flat2d):
        flat, out, at = flat2d.reshape(-1), [], 0
        for p in small_w:
            size = int(np.prod(p.shape))
            out.append(flat[at:at + size].reshape(p.shape))
            at += size
        return out

    small_out = [unpack(a) for a in packed]

    def ordered(kind):
        sm = small_out[kind]
        bg = {n: o[kind][None] for n, o in big_out.items()}
        return [sm[0], bg["qkv"], bg["gate"], sm[1], sm[2], bg["proj_a"], bg["proj_b"], bg["out"], sm[3],
                bg["up"], bg["down"], sm[4]]

    total = lax.psum(loss[0, 0], ("x", "y", "c"))
    return (total, grad_x[None], *ordered(0), *ordered(1), *ordered(2), *ordered(3))
```

```python
import functools
import math

import numpy as np
import jax
import jax.numpy as jnp
from jax import lax
from jax.experimental import pallas as pl
from jax.experimental.pallas import tpu as pltpu

BF = jnp.bfloat16
F32 = jnp.float32
MESH = pl.DeviceIdType.MESH

HEAD_DIM = 128
N_HEADS = 16
N_HEADS_A = 12
QKV_W = N_HEADS * HEAD_DIM
DILATIONS = (1, 4, 16)
HALF_WINDOW = 64
GRID_W = 64
NA_ROWS = 8
NA_COLS = 16
RPB_ROWS = 2 * NA_ROWS - 1
RPB_COLS = 2 * NA_COLS - 1
EPS = 1e-6
NEG = -1e30
SCALE = HEAD_DIM ** -0.5

ADAM_LR = 0.001
ADAM_B1 = 0.9
ADAM_B2 = 0.999
ADAM_EPS = 1e-08
ADAM_WD = 0.01
ADAM_STEP = 10

N_CHIPS = 4
VMEM_LIMIT_BYTES = 48 * 1024 * 1024
QB = 256
NBR_SIDE = 4
ROW_TILE = 512


def _key_rows(L):
    return min(QB + 2 * HALF_WINDOW, L)


def _cparams(sem=None):
    return pltpu.CompilerParams(dimension_semantics=sem, vmem_limit_bytes=VMEM_LIMIT_BYTES)


def _tile(dim, want):
    t = min(dim, want)
    assert dim % t == 0, (dim, want)
    return t


class _ProgramOrder:
    def __init__(self):
        self.last = None

    def call(self, body, operands, in_specs, *, prefetch=(), grid=None, out_specs=None, chain_output=0, **kwargs):
        operands, in_specs = list(operands), list(in_specs)
        lead = len(prefetch) + len(operands)
        if self.last is not None and not any(op is self.last for op in operands):
            operands.append(self.last)
            in_specs.append(pl.BlockSpec(memory_space=pl.ANY))
            inner = body

            def body(*refs):
                return inner(*refs[:lead], *refs[lead + 1:])

        if prefetch:
            kwargs["grid_spec"] = pltpu.PrefetchScalarGridSpec(
                num_scalar_prefetch=len(prefetch), grid=grid, in_specs=in_specs, out_specs=out_specs)
        else:
            kwargs.update(in_specs=in_specs, out_specs=out_specs)
            if grid is not None:
                kwargs["grid"] = grid
        out = pl.pallas_call(body, **kwargs)(*prefetch, *operands)
        self.last = out[chain_output] if isinstance(out, (tuple, list)) else out
        return out


ORDER = _ProgramOrder()


NN = ((1,), (0,))
NT = ((1,), (1,))
TN = ((0,), (0,))


def _matmul(name, a, b, a_spec, b_spec, dims, grid, acc_shape, extras, outs, epilogue, precision=None,
            prefetch=(), into=None):
    n_ex, n_out, nk = len(extras), len(outs), grid[2]
    n_in = 2 + n_ex + (into is not None)

    def body(*refs):
        refs = refs[len(prefetch):]
        a_ref, b_ref = refs[0], refs[1]
        ex_refs = refs[2:2 + n_ex]
        out_refs = refs[n_in:n_in + n_out]

        def dot():
            return lax.dot_general(a_ref[...], b_ref[...], (dims, ((), ())),
                                   preferred_element_type=F32, precision=precision)

        if nk == 1:
            epilogue(dot(), ex_refs, out_refs)
            return
        acc_ref = refs[-1]
        k = pl.program_id(2)

        @pl.when(k == 0)
        def _():
            acc_ref[...] = dot()

        if nk > 2:
            @pl.when((k > 0) & (k < nk - 1))
            def _():
                acc_ref[...] += dot()

        @pl.when(k == nk - 1)
        def _():
            epilogue(acc_ref[...] + dot(), ex_refs, out_refs)

    operands = [a, b] + [e for e, _ in extras]
    in_specs = [a_spec, b_spec] + [s for _, s in extras]
    kwargs = {}
    if into is not None:
        operands.append(into)
        in_specs.append(pl.BlockSpec(memory_space=pl.ANY))
        kwargs["input_output_aliases"] = {len(prefetch) + n_in - 1: 0}
    return ORDER.call(
        body, operands, in_specs, prefetch=prefetch, name=name, grid=grid,
        out_specs=[s for _, s in outs],
        out_shape=[sh for sh, _ in outs],
        scratch_shapes=[pltpu.VMEM(acc_shape, F32)] if nk > 1 else [],
        compiler_params=_cparams(("parallel", "parallel", "arbitrary")), **kwargs,
    )


def _mm_nn_shards(name, a, w, me, own, out, out_block, epilogue, extras=(), into=None, tn=512):
    M, K = a.shape
    Nq = w.shape[-1]
    tm, tn = _tile(M, 1024), _tile(Nq, tn)
    q = Nq // tn

    def tile(j, me_ref):
        shard = me_ref[0] if own else (me_ref[0] + 1 + j // q) % N_CHIPS
        return shard, j % q, shard * q + j % q

    if own:
        b_spec = pl.BlockSpec((K, tn), lambda i, j, k, me_ref: (0, j))
    else:
        b_spec = pl.BlockSpec((None, K, tn), lambda i, j, k, me_ref: (tile(j, me_ref)[0], 0, tile(j, me_ref)[1]))
    shape, dtype = out
    out_spec = pl.BlockSpec((None, tm, tn), lambda i, j, k, me_ref: out_block(i, tile(j, me_ref)[2]))
    ex = [(e, pl.BlockSpec((1, tn), lambda i, j, k, me_ref: (0, tile(j, me_ref)[2]))) for e in extras]
    return _matmul(name, a, w, pl.BlockSpec((tm, K), lambda i, j, k, me_ref: (i, 0)), b_spec, NN,
                   (M // tm, q if own else (N_CHIPS - 1) * q, 1), (tm, tn), ex, [(_sds(shape, dtype), out_spec)],
                   epilogue, prefetch=(me,), into=into)[0]


def _store(dtype):
    def epilogue(acc, ex, outs):
        outs[0][...] = acc.astype(dtype)
    return epilogue


def _sds(shape, dtype):
    return jax.ShapeDtypeStruct(shape, dtype)


def _mm_nn_cols(name, a, g, out_dtype, epilogue=None, extras=(), outs=None, tm=1024, tn=1024, tk=2048):
    M, K = a.shape
    _, _, Nq = g.shape
    tm, tn, tk = _tile(M, tm), _tile(Nq, tn), _tile(K, tk)
    q = Nq // tn
    grid = (M // tm, N_CHIPS * q, K // tk)
    if outs is None:
        outs = [(_sds((M, N_CHIPS * Nq), out_dtype), pl.BlockSpec((tm, tn), lambda i, j, k: (i, j)))]
    return _matmul(name, a, g, pl.BlockSpec((tm, tk), lambda i, j, k: (i, k)),
                   pl.BlockSpec((None, tk, tn), lambda i, j, k: (j // q, k, j % q)), NN, grid, (tm, tn),
                   list(extras), outs, epilogue or _store(out_dtype)), (tm, tn, tk)


def _rms_fwd(name, x, g):
    S, D = x.shape
    tm = _tile(S, ROW_TILE)

    def body(x_ref, g_ref, h_ref):
        xv = x_ref[...]
        r = lax.rsqrt(jnp.mean(xv * xv, axis=-1, keepdims=True) + EPS)
        h_ref[...] = ((xv * r) * g_ref[...]).astype(BF)

    row = pl.BlockSpec((tm, D), lambda i: (i, 0))
    return ORDER.call(
        body, [x, g], [row, pl.BlockSpec((1, D), lambda i: (0, 0))], name=name, grid=(S // tm,),
        out_specs=row, out_shape=_sds((S, D), BF), compiler_params=_cparams(("parallel",)),
    )


def _rms_bwd(name, dh, x, g, dres):
    S, D = x.shape
    tm = _tile(S, ROW_TILE // 2)

    def body(dh_ref, x_ref, g_ref, dres_ref, dx_ref, dxb_ref, dg_ref):
        xv = x_ref[...]
        r = lax.rsqrt(jnp.mean(xv * xv, axis=-1, keepdims=True) + EPS)
        n = xv * r
        dhv = dh_ref[...]
        dyg = dhv * g_ref[...]
        dx = dres_ref[...] + r * (dyg - n * jnp.mean(dyg * n, axis=-1, keepdims=True))
        dx_ref[...] = dx
        dxb_ref[...] = dx.astype(BF)

        @pl.when(pl.program_id(0) == 0)
        def _():
            dg_ref[...] = jnp.zeros_like(dg_ref)

        dg_ref[...] += jnp.sum(dhv * n, axis=0, keepdims=True)

    row = pl.BlockSpec((tm, D), lambda i: (i, 0))
    vec = pl.BlockSpec((1, D), lambda i: (0, 0))
    return ORDER.call(
        body, [dh, x, g, dres], [row, row, vec, row], name=name, grid=(S // tm,),
        out_specs=[row, row, vec],
        out_shape=[_sds((S, D), F32), _sds((S, D), BF), _sds((1, D), F32)],
        compiler_params=_cparams(("arbitrary",)),
    )


def _loss_head(x2, target, g):
    S, D = x2.shape
    tm = _tile(S, ROW_TILE)

    def body(x_ref, t_ref, g_ref, loss_ref, dx_ref, dxb_ref, dg_ref):
        xv = x_ref[...]
        gv = g_ref[...]
        r = lax.rsqrt(jnp.mean(xv * xv, axis=-1, keepdims=True) + EPS)
        n = xv * r
        e = n * gv - t_ref[...]
        dy = e * (1.0 / D)
        dyg = dy * gv
        dx = r * (dyg - n * jnp.mean(dyg * n, axis=-1, keepdims=True))
        dx_ref[...] = dx
        dxb_ref[...] = dx.astype(BF)

        @pl.when(pl.program_id(0) == 0)
        def _():
            dg_ref[...] = jnp.zeros_like(dg_ref)
            loss_ref[...] = jnp.zeros_like(loss_ref)

        dg_ref[...] += jnp.sum(dy * n, axis=0, keepdims=True)
        per_row = jnp.mean(e * e, axis=-1, keepdims=True)
        loss_ref[...] += 0.5 * jnp.sum(per_row, axis=0, keepdims=True)

    row = pl.BlockSpec((tm, D), lambda i: (i, 0))
    vec = pl.BlockSpec((1, D), lambda i: (0, 0))
    return ORDER.call(
        body, [x2, target, g], [row, row, vec], name="loss_head", grid=(S // tm,),
        out_specs=[pl.BlockSpec((1, 1), lambda i: (0, 0)), row, row, vec],
        out_shape=[_sds((1, 1), F32), _sds((S, D), F32), _sds((S, D), BF), _sds((1, D), F32)],
        compiler_params=_cparams(("arbitrary",)), chain_output=1,
    )


def _chains(L):
    side = min(4, L // QB)
    return side, 4 // side


def _band_scores(qkv_ref, i, L, coef, head):
    KB = _key_rows(L)
    lanes = pl.ds(head * HEAD_DIM, HEAD_DIM)
    q0 = pl.multiple_of(i * QB, QB)
    ks = pl.multiple_of(jnp.clip(i * QB - HALF_WINDOW, 0, L - KB), HALF_WINDOW)
    q = qkv_ref[0, pl.ds(q0, QB), lanes]
    k = qkv_ref[1, pl.ds(ks, KB), lanes]
    v = qkv_ref[2, pl.ds(ks, KB), lanes]
    s = lax.dot_general(q, k, (NT, ((), ())), preferred_element_type=F32) * SCALE
    qpos = q0 + lax.broadcasted_iota(jnp.int32, (QB, KB), 0)
    kpos = ks + lax.broadcasted_iota(jnp.int32, (QB, KB), 1)
    rel = jnp.abs(kpos - qpos)
    valid = rel <= HALF_WINDOW
    s = jnp.where(valid, s - coef * rel.astype(F32), NEG)
    return q0, ks, q, k, v, s, valid


def _alibi_coefs(group, d, heads):
    first = 4 * group + 1 + pl.program_id(1) * heads
    scale = jnp.full((1, 1), -(8.0 / N_HEADS_A) * math.log(2.0), F32)
    return [jnp.exp(scale * (first + hh).astype(F32)) * float(d) for hh in range(heads)]


def _dilated_view(qkv3, group, d, heads):
    _, S, _ = qkv3.shape
    L = S // d
    per = 4 // heads
    if d == 1:
        return qkv3, pl.BlockSpec((3, L, heads * HEAD_DIM), lambda r, j: (0, 0, per * group + j))
    cols = qkv3[:, :, 512 * group:512 * (group + 1)].reshape(3, L, d * 512)
    return cols, pl.BlockSpec((3, L, heads * HEAD_DIM), lambda r, j: (0, 0, r * per + j))


def _attn_a_fwd(qkv3, group, d):
    _, S, _ = qkv3.shape
    L = S // d
    assert L % QB == 0
    side, heads = _chains(L)
    view, blocks_spec = _dilated_view(qkv3, group, d, heads)

    def body(qkv_ref, o_ref, lse_ref):
        coefs = _alibi_coefs(group, d, heads)

        def step(i, carry):
            chains = [(hh, _band_scores(qkv_ref, side * i + u, L, coefs[hh], hh))
                      for u in range(side) for hh in range(heads)]
            soft = []
            for hh, (q0, _, _, _, v, s, _) in chains:
                m = jnp.max(s, axis=-1, keepdims=True)
                p = jnp.exp(s - m)
                den = jnp.sum(p, axis=-1, keepdims=True)
                soft.append((hh, q0, (p / den).astype(BF), v, m + jnp.log(den)))
            for hh, q0, pn, v, lse in soft:
                lanes = pl.ds(hh * HEAD_DIM, HEAD_DIM)
                o_ref[pl.ds(q0, QB), lanes] = jnp.dot(pn, v, preferred_element_type=F32)
                lse_ref[pl.ds(q0, QB), lanes] = jnp.broadcast_to(lse, (QB, HEAD_DIM))
            return carry

        lax.fori_loop(0, L // QB // side, step, 0)

    per = 4 // heads
    out = pl.BlockSpec((L, heads * HEAD_DIM), lambda r, j: (0, r * per + j))
    o, lse = ORDER.call(
        body, [view], [blocks_spec],
        name=f"attn_a_fwd_d{d}", grid=(d, per),
        out_specs=[out, out],
        out_shape=[_sds((L, d * 512), F32), _sds((L, d * 512), F32)],
        compiler_params=_cparams(("parallel", "parallel")),
    )
    return o, lse


def _dilated_rows(name, arrays):
    S, W = arrays[0].shape
    tm = _tile(S, 256)
    ds_ = [d for d in DILATIONS if d > 1]
    n = len(arrays)

    def body(*refs):
        nc = W // 128
        ins, outs, scr = refs[:n], refs[n:-nc], refs[-nc:]
        for a, src in enumerate(ins):
            for c in range(nc):
                scr[c][...] = src[:, c * 128:(c + 1) * 128].astype(F32)
            for k, d in enumerate(ds_):
                dst = outs[a * len(ds_) + k]
                for r in range(d):
                    for c in range(nc):
                        at = r * W + c * 128
                        dst[:, at:at + 128] = scr[c][pl.ds(r, tm // d, stride=d), :].astype(dst.dtype)

    row = pl.BlockSpec((tm, W), lambda i: (i, 0))
    out_specs, out_shape = [], []
    for a in arrays:
        for d in ds_:
            out_specs.append(pl.BlockSpec((tm // d, d * W), lambda i: (i, 0)))
            out_shape.append(_sds((S // d, d * W), a.dtype))
    outs = ORDER.call(body, list(arrays), [row] * n, name=name, grid=(S // tm,), out_specs=out_specs,
                      out_shape=out_shape, scratch_shapes=[pltpu.VMEM((tm, 128), F32)] * (W // 128),
                      compiler_params=_cparams(("parallel",)))
    return [{d: outs[a * len(ds_) + k] for k, d in enumerate(ds_)} for a in range(n)]


def _attn_a_combine(os_, lses):
    W = 512
    S = os_[0].shape[0] * DILATIONS[0]
    tm = _tile(S, 256)
    nc = W // 128
    dilated = [g for g, d in enumerate(DILATIONS) if d > 1]

    def body(o0, o1, o2, l0, l1, l2, y_ref, lj_ref, *scr):
        def token_order(src, g, slot):
            d = DILATIONS[g]
            if d == 1:
                return src[...]
            bufs = scr[slot * nc:(slot + 1) * nc]
            for r in range(d):
                for c in range(nc):
                    at = r * W + c * 128
                    bufs[c][pl.ds(r, tm // d, stride=d), :] = src[:, at:at + 128]
            return jnp.concatenate([buf[...] for buf in bufs], axis=1)

        slots = {g: k for k, g in enumerate(dilated)}
        ls = [token_order(l, g, slots.get(g, 0)) for g, l in enumerate((l0, l1, l2))]
        os_tok = [token_order(o, g, len(dilated) + slots.get(g, 0)) for g, o in enumerate((o0, o1, o2))]
        m = jnp.maximum(jnp.maximum(ls[0], ls[1]), ls[2])
        es = [jnp.exp(l - m) for l in ls]
        den = es[0] + es[1] + es[2]
        y = (es[0] / den) * os_tok[0] + (es[1] / den) * os_tok[1] + (es[2] / den) * os_tok[2]
        y_ref[...] = y.astype(BF)
        lj_ref[...] = m + jnp.log(den)

    row = pl.BlockSpec((tm, W), lambda i: (i, 0))
    views = [pl.BlockSpec((tm // d, d * W), lambda i: (i, 0)) for d in DILATIONS]
    return ORDER.call(
        body, [*os_, *lses], views + views, name="attn_a_combine", grid=(S // tm,), out_specs=[row, row],
        out_shape=[_sds((S, W), BF), _sds((S, W), F32)],
        scratch_shapes=[pltpu.VMEM((tm, 128), F32)] * (2 * len(dilated) * nc),
        compiler_params=_cparams(("parallel",)),
    )


def _attn_a_bwd(qkv3, dy, y, lj, dqkv3, group, d):
    _, S, _ = qkv3.shape
    L = S // d
    side, heads = _chains(L)
    view, blocks_spec = _dilated_view(qkv3, group, d, heads)

    def body(qkv_ref, dy_ref, y_ref, lj_ref, *rest):
        out_ref, dk_acc, dv_acc = rest[-3:]
        coefs = _alibi_coefs(group, d, heads)
        dk_acc[...] = jnp.zeros_like(dk_acc)
        dv_acc[...] = jnp.zeros_like(dv_acc)

        def step(i, carry):
            chains = [(pl.ds(hh * HEAD_DIM, HEAD_DIM), _band_scores(qkv_ref, side * i + u, L, coefs[hh], hh))
                      for u in range(side) for hh in range(heads)]
            dys = [dy_ref[pl.ds(c[0], QB), lanes] for lanes, c in chains]
            dps = [lax.dot_general(dyv, c[4], (NT, ((), ())), preferred_element_type=F32)
                   for dyv, (_, c) in zip(dys, chains)]
            grads = []
            for (lanes, (q0, ks, q, k, v, s, valid)), dyv, dp in zip(chains, dys, dps):
                rows = pl.ds(q0, QB)
                delta = jnp.sum(dyv.astype(F32) * y_ref[rows, lanes].astype(F32), axis=-1, keepdims=True)
                p = jnp.where(valid, jnp.exp(s - jnp.tile(lj_ref[rows, lanes], (1, _key_rows(L) // HEAD_DIM))), 0.0)
                grads.append(((p * (dp - delta)).astype(BF), p.astype(BF)))
            for (lanes, (q0, ks, q, k, v, s, valid)), dyv, (ds, pb) in zip(chains, dys, grads):
                out_ref[0, pl.ds(q0, QB), lanes] = (jnp.dot(ds, k, preferred_element_type=F32) * SCALE).astype(BF)
                keys = pl.ds(ks, _key_rows(L))
                dk_acc[keys, lanes] += lax.dot_general(ds, q, (TN, ((), ())), preferred_element_type=F32) * SCALE
                dv_acc[keys, lanes] += lax.dot_general(pb, dyv, (TN, ((), ())), preferred_element_type=F32)
            return carry

        lax.fori_loop(0, L // QB // side, step, 0)
        out_ref[1] = dk_acc[...].astype(BF)
        out_ref[2] = dv_acc[...].astype(BF)

    per = 4 // heads
    width = heads * HEAD_DIM
    row = pl.BlockSpec((L, width), lambda r, j: (0, r * per + j))
    operands = [view, dy, y, lj]
    scratch = [pltpu.VMEM((L, width), F32), pltpu.VMEM((L, width), F32)]
    if d == 1:
        return ORDER.call(
            body, operands + [dqkv3], [blocks_spec, row, row, row, pl.BlockSpec(memory_space=pl.ANY)],
            name=f"attn_a_bwd_d{d}", grid=(d, per), out_specs=blocks_spec, out_shape=_sds((3, S, QKV_W), BF),
            scratch_shapes=scratch, input_output_aliases={4: 0}, compiler_params=_cparams(("parallel", "parallel")))
    out = ORDER.call(
        body, operands, [blocks_spec, row, row, row], name=f"attn_a_bwd_d{d}", grid=(d, per),
        out_specs=blocks_spec, out_shape=_sds((3, L, d * 512), BF),
        scratch_shapes=scratch, compiler_params=_cparams(("parallel", "parallel")))
    return lax.dynamic_update_slice(dqkv3, out.reshape(3, S, 512), (0, 0, 512 * group))


def _toeplitz_onehot():
    oh = np.zeros((64, GRID_W, 128), np.float32)
    for qc in range(GRID_W):
        for m in range(128):
            kc = m % GRID_W
            dc = int(np.clip(kc - qc, -(NA_COLS - 1), NA_COLS - 1)) + NA_COLS - 1
            oh[(m // GRID_W) * 32 + dc, qc, m] = 1.0
    return oh.reshape(64, GRID_W * 128)


def _nbr_scores(qkv_ref, e2_ref, r, rows, ok):
    rs = jnp.clip(r - NA_ROWS // 2, 0, rows - NA_ROWS)
    q0 = pl.multiple_of(r * GRID_W, GRID_W)
    k0 = pl.multiple_of(rs * GRID_W, GRID_W)
    q = qkv_ref[0, pl.ds(q0, GRID_W), :]
    k = qkv_ref[1, pl.ds(k0, NA_ROWS * GRID_W), :]
    v = qkv_ref[2, pl.ds(k0, NA_ROWS * GRID_W), :]
    s = lax.dot_general(q, k, (NT, ((), ())), preferred_element_type=F32) * SCALE
    first = rs - r + NA_ROWS - 1
    bias = jnp.concatenate([e2_ref[first + 2 * pair] for pair in range(NA_ROWS // 2)], axis=1)
    s = jnp.where(ok, s + bias, NEG)
    return q0, k0, first, q, k, v, s


def _nbr_col_ok():
    qc = lax.broadcasted_iota(jnp.int32, (GRID_W, NA_ROWS * GRID_W), 0)
    kc = lax.broadcasted_iota(jnp.int32, (GRID_W, NA_ROWS * GRID_W), 1) % GRID_W
    cs = jnp.clip(qc - NA_COLS // 2, 0, GRID_W - NA_COLS)
    return (kc >= cs) & (kc < cs + NA_COLS)


def _attn_b_fwd(qkv3, e2):
    _, S, _ = qkv3.shape
    rows = S // GRID_W
    assert rows >= NA_ROWS

    def body(qkv_ref, e2_ref, o_ref, lse_ref):
        ok = _nbr_col_ok()

        def step(i, carry):
            blocks = [_nbr_scores(qkv_ref, e2_ref, NBR_SIDE * i + u, rows, ok) for u in range(NBR_SIDE)]
            soft = []
            for q0, _, _, _, _, v, s in blocks:
                m = jnp.max(s, axis=-1, keepdims=True)
                p = jnp.exp(s - m)
                den = jnp.sum(p, axis=-1, keepdims=True)
                soft.append((q0, (p / den).astype(BF), v, m + jnp.log(den)))
            for q0, pn, v, lse in soft:
                o_ref[pl.ds(q0, GRID_W), :] = jnp.dot(pn, v, preferred_element_type=F32).astype(BF)
                lse_ref[pl.ds(q0, GRID_W), :] = jnp.broadcast_to(lse, (GRID_W, HEAD_DIM))
            return carry

        lax.fori_loop(0, rows // NBR_SIDE, step, 0)

    out = pl.BlockSpec((S, HEAD_DIM), lambda h: (0, h))
    return ORDER.call(
        body, [qkv3, e2],
        [pl.BlockSpec((3, S, HEAD_DIM), lambda h: (0, 0, N_HEADS_A + h)),
         pl.BlockSpec((None, RPB_ROWS - 1, GRID_W, 128), lambda h: (h, 0, 0, 0))],
        name="attn_b_fwd", grid=(4,),
        out_specs=[out, out], out_shape=[_sds((S, 512), BF), _sds((S, 512), F32)],
        compiler_params=_cparams(("parallel",)),
    )


def _attn_b_bwd(qkv3, e2, dy, y, lse, dqkv3):
    _, S, _ = qkv3.shape
    rows = S // GRID_W
    nk = NA_ROWS * GRID_W

    def body(qkv_ref, e2_ref, dy_ref, y_ref, lse_ref, _, out_ref, de2_ref, dk_acc, dv_acc):
        ok = _nbr_col_ok()
        dk_acc[...] = jnp.zeros_like(dk_acc)
        dv_acc[...] = jnp.zeros_like(dv_acc)
        de2_ref[...] = jnp.zeros_like(de2_ref)

        def step(i, carry):
            blocks = [_nbr_scores(qkv_ref, e2_ref, NBR_SIDE * i + u, rows, ok) for u in range(NBR_SIDE)]
            dys = [dy_ref[pl.ds(b[0], GRID_W), :] for b in blocks]
            dps = [lax.dot_general(dyv, b[5], (NT, ((), ())), preferred_element_type=F32) for dyv, b in zip(dys, blocks)]
            grads = []
            for (q0, k0, first, q, k, v, s), dyv, dp in zip(blocks, dys, dps):
                qrows = pl.ds(q0, GRID_W)
                delta = jnp.sum(dyv.astype(F32) * y_ref[qrows, :].astype(F32), axis=-1, keepdims=True)
                p = jnp.where(ok, jnp.exp(s - jnp.tile(lse_ref[qrows, :], (1, nk // HEAD_DIM))), 0.0)
                ds = p * (dp - delta)
                for pair in range(NA_ROWS // 2):
                    de2_ref[first + 2 * pair] += ds[:, pair * 128:(pair + 1) * 128]
                grads.append((ds.astype(BF), p.astype(BF)))
            for (q0, k0, first, q, k, v, s), dyv, (dsb, pb) in zip(blocks, dys, grads):
                out_ref[0, pl.ds(q0, GRID_W), :] = (jnp.dot(dsb, k, preferred_element_type=F32) * SCALE).astype(BF)
                keys = pl.ds(k0, nk)
                dk_acc[keys, :] += lax.dot_general(dsb, q, (TN, ((), ())), preferred_element_type=F32) * SCALE
                dv_acc[keys, :] += lax.dot_general(pb, dyv, (TN, ((), ())), preferred_element_type=F32)
            return carry

        lax.fori_loop(0, rows // NBR_SIDE, step, 0)
        out_ref[1] = dk_acc[...].astype(BF)
        out_ref[2] = dv_acc[...].astype(BF)

    heads = pl.BlockSpec((3, S, HEAD_DIM), lambda h: (0, 0, N_HEADS_A + h))
    row = pl.BlockSpec((S, HEAD_DIM), lambda h: (0, h))
    table = pl.BlockSpec((None, RPB_ROWS - 1, GRID_W, 128), lambda h: (h, 0, 0, 0))
    return ORDER.call(
        body, [qkv3, e2, dy, y, lse, dqkv3],
        [heads, table, row, row, row, pl.BlockSpec(memory_space=pl.ANY)], name="attn_b_bwd", grid=(4,),
        out_specs=[heads, table],
        out_shape=[_sds((3, S, QKV_W), BF), _sds((4, RPB_ROWS - 1, GRID_W, 128), F32)],
        scratch_shapes=[pltpu.VMEM((S, HEAD_DIM), F32), pltpu.VMEM((S, HEAD_DIM), F32)],
        input_output_aliases={5: 0},
        compiler_params=_cparams(("parallel",)), chain_output=1,
    )


def _rpb_to_table(rpb):
    pad = jnp.pad(rpb, ((0, 0), (0, 0), (0, 1)))
    pairs = jnp.concatenate([pad[:, :-1], pad[:, 1:]], axis=-1).reshape(4 * (RPB_ROWS - 1), 64)
    onehot = jnp.asarray(_toeplitz_onehot())
    n = onehot.shape[1]
    tn = 2048
    full = lambda i, j, k: (0, 0)
    (e2,) = _matmul("rpb_table", pairs, onehot, pl.BlockSpec(pairs.shape, full),
                    pl.BlockSpec((64, tn), lambda i, j, k: (0, j)), NN, (1, n // tn, 1), (pairs.shape[0], tn), [],
                    [(_sds((pairs.shape[0], n), F32), pl.BlockSpec((pairs.shape[0], tn), lambda i, j, k: (0, j)))],
                    _store(F32), precision=lax.Precision.HIGHEST)
    return e2.reshape(4, RPB_ROWS - 1, GRID_W, 128)


def _table_grad_to_rpb(de2):
    onehot = jnp.asarray(_toeplitz_onehot())
    n = onehot.shape[1]
    flat = de2.reshape(4 * (RPB_ROWS - 1), n)
    tk = 2048
    (dpairs,) = _matmul("rpb_table_grad", flat, onehot, pl.BlockSpec((flat.shape[0], tk), lambda i, j, k: (0, k)),
                        pl.BlockSpec((64, tk), lambda i, j, k: (0, k)), NT, (1, 1, n // tk), (flat.shape[0], 64), [],
                        [(_sds((flat.shape[0], 64), F32), pl.BlockSpec((flat.shape[0], 64), lambda i, j, k: (0, 0)))],
                        _store(F32), precision=lax.Precision.HIGHEST)
    dpairs = dpairs.reshape(4, RPB_ROWS - 1, 64)
    zero = jnp.zeros((4, 1, RPB_COLS), F32)
    return (jnp.concatenate([dpairs[:, :, :RPB_COLS], zero], axis=1)
            + jnp.concatenate([zero, dpairs[:, :, 32:32 + RPB_COLS]], axis=1))


HBM = pl.BlockSpec(memory_space=pl.ANY)


def _place():
    x, y, c = lax.axis_index("x"), lax.axis_index("y"), lax.axis_index("c")
    chips = [(1 - x, y), (x, 1 - y), (1 - x, 1 - y)]
    return x, y, c, chips


def _remote(src, dst, send_sem, recv_sem, to):
    return pltpu.make_async_remote_copy(src_ref=src, dst_ref=dst, send_sem=send_sem, recv_sem=recv_sem,
                                        device_id=to, device_id_type=MESH)


def _place_shard(name, w, me, plain=False):
    R, C = w.shape
    tr = _tile(R, 256)

    def body(me_ref, w_ref, *o_refs):
        for o_ref in o_refs:
            o_ref[...] = w_ref[...].astype(BF)

    row = pl.BlockSpec((tr, C), lambda i, mr: (i, 0))
    placed = pl.BlockSpec((None, tr, C), lambda i, mr: (mr[0], i, 0))
    return ORDER.call(
        body, [w], [row], prefetch=(me,), name=name, grid=(R // tr,),
        out_specs=[placed, row] if plain else [placed],
        out_shape=[_sds((N_CHIPS, R, C), BF)] + ([_sds((R, C), BF)] if plain else []),
        compiler_params=_cparams(("parallel",)),
    )


SEM = pl.BlockSpec(memory_space=pltpu.SEMAPHORE)
IN_HBM = pl.BlockSpec(memory_space=pltpu.HBM)
DATAFLOW = pltpu.SideEffectType.DATAFLOW_SIDE_EFFECTING


def _in_hbm(a):
    return pltpu.with_memory_space_constraint(a, pltpu.HBM)


def _copy_start(name, bufs, copies, n_copies, earlier=None):
    n = len(bufs)
    after = None if any(b is ORDER.last for b in bufs) else ORDER.last
    n_extra = (2 if earlier is not None else 0) + (1 if after is not None else 0)

    def body(*refs):
        ins = refs[:n]
        if earlier is not None:
            for k, (src, dst, to) in enumerate(earlier[0](ins)):
                cp = _remote(src, dst, refs[n].at[k], refs[n + 1].at[k], to)
                cp.wait_send()
                cp.wait_recv()
        send_sems, recv_sems = refs[n + n_extra], refs[n + n_extra + 1]
        for k, (src, dst, to) in enumerate(copies(ins)):
            _remote(src, dst, send_sems.at[k], recv_sems.at[k], to).start()
        refs[-1][...] = jnp.zeros((8, 128), F32)

    operands = [_in_hbm(b) for b in bufs]
    in_specs = [IN_HBM] * n
    if earlier is not None:
        operands += [earlier[1], earlier[2]]
        in_specs += [SEM, SEM]
    if after is not None:
        operands.append(after)
        in_specs.append(HBM)
    outs = pl.pallas_call(
        body, name=name,
        out_shape=(pltpu.SemaphoreType.DMA((n_copies,)), pltpu.SemaphoreType.DMA((n_copies,)),
                   *[pltpu.HBM(b.shape, b.dtype) for b in bufs], _sds((8, 128), F32)),
        in_specs=in_specs,
        out_specs=(SEM, SEM, *[IN_HBM] * n, pl.BlockSpec(memory_space=pltpu.VMEM)),
        input_output_aliases={i: 2 + i for i in range(n)},
        compiler_params=pltpu.CompilerParams(has_side_effects=DATAFLOW),
    )(*operands)
    ORDER.last = outs[-1]
    return outs[0], outs[1], list(outs[2:2 + n])


def _copy_wait(name, bufs, copies, send_sems, recv_sems):
    n = len(bufs)
    after = ORDER.last

    def body(*refs):
        ins = refs[:n]
        for k, (src, dst, to) in enumerate(copies(ins)):
            cp = _remote(src, dst, refs[n].at[k], refs[n + 1].at[k], to)
            cp.wait_send()
            cp.wait_recv()

    outs = list(pl.pallas_call(
        body, name=name,
        out_shape=tuple(pltpu.HBM(b.shape, b.dtype) for b in bufs),
        in_specs=[IN_HBM] * n + [SEM, SEM, HBM], out_specs=tuple([IN_HBM] * n),
        input_output_aliases={i: i for i in range(n)},
        compiler_params=pltpu.CompilerParams(has_side_effects=DATAFLOW),
    )(*bufs, send_sems, recv_sems, after))
    ORDER.last = outs[0]
    return outs


def _gather_hop1(bufs):
    x, y, c, chips = _place()
    out = []
    for b in bufs:
        half = b.shape[1] // 2
        mine = b.at[2 * x + y, pl.ds(c * half, half), :]
        out += [(mine, mine, (*chip, c)) for chip in chips]
    return out


def _gather_hop2(bufs):
    x, y, c, chips = _place()
    out = []
    for b in bufs:
        half = b.shape[1] // 2
        for chip in chips:
            landed = b.at[2 * chip[0] + chip[1], pl.ds(c * half, half), :]
            out.append((landed, landed, (x, y, 1 - c)))
    return out


def _swap_copies(bufs):
    x, y, c, _ = _place()
    n = len(bufs) // 2
    out = []
    for p, land in zip(bufs[:n], bufs[n:]):
        half = p.shape[1] // 2
        out.append((p.at[:, pl.ds((1 - c) * half, half), :], land, (x, y, 1 - c)))
    return out


def _scatter_copies(bufs):
    _, _, c, chips = _place()
    n = len(bufs) // 2
    out = []
    for s_, land in zip(bufs[:n], bufs[n:]):
        out += [(s_.at[2 * chip[0] + chip[1]], land.at[j], (*chip, c)) for j, chip in enumerate(chips)]
    return out


def _join_copies(bufs):
    x, y, c, _ = _place()
    out = []
    for b in bufs:
        half = b.shape[0] // 2
        mine = b.at[pl.ds(c * half, half), :]
        out.append((mine, mine, (x, y, 1 - c)))
    return out


def _gather_small(vec):
    m_per, n = vec.shape

    def body(x_ref, out_ref, send_sems, recv_sems, local_sem):
        x, y, c, chips = _place()
        me, sibling = (x, y, c), (x, y, 1 - c)

        def rows(px, py, pc):
            return out_ref.at[pl.ds((4 * px + 2 * py + pc) * m_per, m_per), :]

        def copy(k, block, to, src=None):
            return _remote(rows(*block) if src is None else src, rows(*block), send_sems.at[k], recv_sems.at[k], to)

        mine = pltpu.make_async_copy(x_ref, rows(*me), local_sem)
        mine.start()
        first = [copy(0, me, sibling, src=x_ref)]
        first += [copy(1 + j, me, (*chip, c), src=x_ref) for j, chip in enumerate(chips)]
        for cp in first:
            cp.start()
        passed = [copy(4 + j, (*chip, c), sibling) for j, chip in enumerate(chips)]
        for j, chip in enumerate(chips):
            copy(1 + j, (*chip, c), me).wait_recv()
            passed[j].start()
        copy(0, sibling, me).wait_recv()
        for j, chip in enumerate(chips):
            copy(4 + j, (*chip, 1 - c), me).wait_recv()
        for cp in first + passed:
            cp.wait_send()
        mine.wait()

    return ORDER.call(
        body, [vec], [pl.BlockSpec(memory_space=pltpu.VMEM)], name="gather_small_grads",
        out_shape=_sds((8 * m_per, n), vec.dtype), out_specs=pl.BlockSpec(memory_space=pltpu.VMEM),
        scratch_shapes=[pltpu.SemaphoreType.DMA((7,)), pltpu.SemaphoreType.DMA((7,)), pltpu.SemaphoreType.DMA],
    )


def _add_sibling(name, partial, received, c):
    _, R, C = partial.shape
    half = R // 2
    tr = _tile(half, 256)
    nb = half // tr

    def body(c_ref, p_ref, r_ref, o_ref):
        o_ref[...] = (p_ref[...].astype(F32) + r_ref[...].astype(F32)).astype(BF)

    return ORDER.call(
        body, [partial, received],
        [pl.BlockSpec((None, tr, C), lambda j, i, cr: (j, cr[0] * nb + i, 0)),
         pl.BlockSpec((None, tr, C), lambda j, i, cr: (j, i, 0))],
        prefetch=(c,), name=name, grid=(N_CHIPS, nb),
        out_specs=pl.BlockSpec((None, tr, C), lambda j, i, cr: (j, i, 0)),
        out_shape=_sds((N_CHIPS, half, C), BF), compiler_params=_cparams(("parallel", "parallel")),
    )


def _add_chips(name, sums, received, me_c):
    _, half, C = sums.shape
    tr = _tile(half, 256)
    nb = half // tr

    def body(mc_ref, s_ref, r_ref, o_ref):
        acc = s_ref[...].astype(F32)
        for j in range(3):
            acc = acc + r_ref[j].astype(F32)
        o_ref[...] = acc

    return ORDER.call(
        body, [sums, received],
        [pl.BlockSpec((None, tr, C), lambda i, mc: (mc[0], i, 0)),
         pl.BlockSpec((3, tr, C), lambda i, mc: (0, i, 0))],
        prefetch=(me_c,), name=name, grid=(nb,),
        out_specs=pl.BlockSpec((tr, C), lambda i, mc: (mc[1] * nb + i, 0)),
        out_shape=_sds((2 * half, C), F32), compiler_params=_cparams(("parallel",)),
    )


def _adamw_math(w, g, m, v):
    m = ADAM_B1 * m + (1.0 - ADAM_B1) * g
    v = ADAM_B2 * v + (1.0 - ADAM_B2) * (g * g)
    m_hat = m / (1.0 - ADAM_B1 ** ADAM_STEP)
    v_hat = v / (1.0 - ADAM_B2 ** ADAM_STEP)
    delta = -ADAM_LR * (m_hat / (jnp.sqrt(v_hat) + ADAM_EPS) + ADAM_WD * w)
    return delta, m, v


def _adamw(name, w, g, m, v):
    R, C = w.shape
    tr = _tile(R, 256)

    def body(w_ref, g_ref, m_ref, v_ref, go_ref, d_ref, mo_ref, vo_ref):
        gv = g_ref[...]
        go_ref[...] = gv
        d_ref[...], mo_ref[...], vo_ref[...] = _adamw_math(w_ref[...], gv, m_ref[...], v_ref[...])

    row = pl.BlockSpec((tr, C), lambda i: (i, 0))
    return ORDER.call(
        body, [w, g, m, v], [row] * 4, name=name, grid=(R // tr,), out_specs=[row] * 4,
        out_shape=[_sds((R, C), F32)] * 4, compiler_params=_cparams(("parallel",)), chain_output=1,
    )


def _adamw_small(gathered, w, m, v):
    rows, n = w.shape

    def body(ga_ref, w_ref, m_ref, v_ref, go_ref, d_ref, mo_ref, vo_ref):
        g = ga_ref[pl.ds(0, rows), :]
        for dev in range(1, 8):
            g = g + ga_ref[pl.ds(dev * rows, rows), :]
        go_ref[...] = g
        d_ref[...], mo_ref[...], vo_ref[...] = _adamw_math(w_ref[...], g, m_ref[...], v_ref[...])

    whole = pl.BlockSpec(memory_space=pltpu.VMEM)
    return ORDER.call(
        body, [gathered, w, m, v], [whole] * 4, name="adamw_small", out_specs=[whole] * 4,
        out_shape=[_sds((rows, n), F32)] * 4, compiler_params=_cparams(), chain_output=1,
    )


def _proj_merge(y_a, y_b, gpa, gpb, g3):
    S, K = y_a.shape
    _, _, Nq = gpa.shape
    D = N_CHIPS * Nq
    tm, tn = _tile(S, 1024), _tile(Nq, 512)
    q = Nq // tn

    def body(ya_ref, yb_ref, wa_ref, wb_ref, g_ref, merged_ref, c_ref):
        pa = jnp.dot(ya_ref[...], wa_ref[...], preferred_element_type=F32)
        pb = jnp.dot(yb_ref[...], wb_ref[...], preferred_element_type=F32)
        g = g_ref[...].astype(F32)
        merged_ref[...] = (g[0] * pa + g[1] * pb).astype(BF)
        c_ref[0] = (pa * g[0] * (1.0 - g[0])).astype(BF)
        c_ref[1] = (pb * g[1] * (1.0 - g[1])).astype(BF)

    rows = pl.BlockSpec((tm, K), lambda i, j: (i, 0))
    weight = pl.BlockSpec((None, K, tn), lambda i, j: (j // q, 0, j % q))
    pair = pl.BlockSpec((2, tm, tn), lambda i, j: (0, i, j))
    return ORDER.call(
        body, [y_a, y_b, gpa, gpb, g3], [rows, rows, weight, weight, pair], name="proj_merge",
        grid=(S // tm, N_CHIPS * q), out_specs=[pl.BlockSpec((tm, tn), lambda i, j: (i, j)), pair],
        out_shape=[_sds((S, D), BF), _sds((2, S, D), BF)], compiler_params=_cparams(("parallel", "parallel")))


class _Exchange:
    GATHER = (("qkv",), ("gate",), ("proj_a", "proj_b", "out"), ("up",), ("down",))
    REDUCE = {"mlp": ("down", "up"), "mix": ("out", "proj_a", "proj_b"), "in": ("qkv", "gate")}

    OWN_FIRST = ("qkv", "gate")

    def __init__(self, shards, me, c):
        self.me, self.c = me, c
        self.hop1, self.hop2, self.stage, self.grads, self.own = {}, {}, {}, {}, {}
        for g, names in enumerate(self.GATHER):
            bufs = []
            for n in names:
                placed = _place_shard(f"place_{n}", shards[n], me, plain=n in self.OWN_FIRST)
                bufs.append(placed[0])
                if n in self.OWN_FIRST:
                    self.own[n] = placed[1]
            self.hop1[g] = _copy_start(f"gather{g}_start", bufs, _gather_hop1, 3 * len(names))

    def forward(self, g):
        send, recv, thru = self.hop1.pop(g)
        self.hop2[g] = _copy_start(f"gather{g}_forward", thru, _gather_hop2, len(thru) * 3,
                                   earlier=(_gather_hop1, send, recv))

    def weights(self, g):
        send, recv, thru = self.hop2.pop(g)
        return _copy_wait(f"gather{g}_wait", thru, _gather_hop2, send, recv)

    def reduce(self, key, partials=None):
        names = self.REDUCE[key]
        n = len(names)
        if partials is not None:
            lands = [lax.empty((p.shape[0], p.shape[1] // 2, p.shape[2]), p.dtype) for p in partials]
            self.stage[key] = ("swap",) + _copy_start(f"reduce_{key}_swap", list(partials) + lands, _swap_copies, n)
            return
        kind, send, recv, thru = self.stage.pop(key)
        if kind == "swap":
            thru = _copy_wait(f"reduce_{key}_swap_wait", thru, _swap_copies, send, recv)
            sums = [_add_sibling(f"reduce_{nm}_add_sibling", p, r, self.c)
                    for nm, p, r in zip(names, thru[:n], thru[n:])]
            lands = [lax.empty((3,) + s_.shape[1:], s_.dtype) for s_ in sums]
            self.stage[key] = ("scatter",) + _copy_start(f"reduce_{key}_scatter", sums + lands, _scatter_copies, 3 * n)
        elif kind == "scatter":
            thru = _copy_wait(f"reduce_{key}_scatter_wait", thru, _scatter_copies, send, recv)
            me_c = jnp.concatenate([self.me, self.c])
            halves = [_add_chips(f"reduce_{nm}_add_chips", s_, r, me_c)
                      for nm, s_, r in zip(names, thru[:n], thru[n:])]
            self.stage[key] = ("join",) + _copy_start(f"reduce_{key}_join", halves, _join_copies, n)
        else:
            thru = _copy_wait(f"reduce_{key}_join_wait", thru, _join_copies, send, recv)
            self.grads.update(zip(names, thru))


def _forward_backward(x, target, norm_mix, b_gate, rpb, norm_mlp, norm_final, ex):
    S, D = x.shape

    h1 = _rms_fwd("rms_mix", x, norm_mix)
    nq = QKV_W // 512
    qkv_out = (((3, S, QKV_W), BF), lambda i, T: (T // nq, i, T % nq))
    tg = _tile(ex.own["gate"].shape[1], 1024)
    ng = D // tg
    gate_out = (((2, S, D), BF), lambda i, T: (T // ng, i, T % ng))

    def gate_epilogue(acc, ex_, outs):
        outs[0][...] = jax.nn.sigmoid(acc + ex_[0][...]).astype(BF)

    qkv3 = _mm_nn_shards("qkv_own", h1, ex.own["qkv"], ex.me, True, *qkv_out, _store(BF))
    g3 = _mm_nn_shards("gate_own", h1, ex.own["gate"], ex.me, True, *gate_out, gate_epilogue, extras=[b_gate], tn=tg)
    ex.forward(0)
    e2 = _rpb_to_table(rpb)
    (gq,) = ex.weights(0)
    qkv3 = _mm_nn_shards("qkv", h1, gq, ex.me, False, *qkv_out, _store(BF), into=qkv3)

    ex.forward(1)
    outs_a = [_attn_a_fwd(qkv3, 0, DILATIONS[0])]
    (gg,) = ex.weights(1)
    g3 = _mm_nn_shards("gate", h1, gg, ex.me, False, *gate_out, gate_epilogue, extras=[b_gate], into=g3, tn=tg)

    ex.forward(2)
    outs_a += [_attn_a_fwd(qkv3, grp, d) for grp, d in enumerate(DILATIONS) if grp > 0]
    y_a, lj = _attn_a_combine([o for o, _ in outs_a], [l for _, l in outs_a])
    y_b, lse_b = _attn_b_fwd(qkv3, e2)
    gpa, gpb, gout = ex.weights(2)
    wout = gout.reshape(D, D)
    merged, c3 = _proj_merge(y_a, y_b, gpa, gpb, g3)

    def residual_epilogue(acc, ex_, outs):
        outs[0][...] = acc + ex_[0][...]

    def nn_plain(name, a, w, res):
        M, K = a.shape
        N = w.shape[1]
        bm, bn, bk = _tile(M, 1024), _tile(N, 1024), _tile(K, 2048)
        t = pl.BlockSpec((bm, bn), lambda i, j, k: (i, j))
        return _matmul(name, a, w, pl.BlockSpec((bm, bk), lambda i, j, k: (i, k)),
                       pl.BlockSpec((bk, bn), lambda i, j, k: (k, j)), NN, (M // bm, N // bn, K // bk), (bm, bn),
                       [(res, t)], [(_sds((M, N), F32), t)], residual_epilogue)[0]

    ex.forward(3)
    x1 = nn_plain("out_proj", merged, wout, x)
    h2 = _rms_fwd("rms_mlp", x1, norm_mlp)
    (gup,) = ex.weights(3)
    F = gup.shape[2] * N_CHIPS
    ex.forward(4)

    def up_epilogue(acc, ex_, outs):
        ru = jnp.maximum(acc, 0.0)
        outs[0][...] = (ru * ru).astype(BF)
        outs[1][...] = ru.astype(BF)

    tu = _tile(gup.shape[2], 1024)
    ut = pl.BlockSpec((_tile(S, 1024), tu), lambda i, j, k: (i, j))
    (act, ru), _ = _mm_nn_cols("mlp_up", h2, gup, BF, epilogue=up_epilogue, tn=tu,
                               outs=[(_sds((S, F), BF), ut), (_sds((S, F), BF), ut)])
    (gdown,) = ex.weights(4)
    wdown = gdown.reshape(F, D)
    x2 = nn_plain("mlp_down", act, wdown, x1)

    loss, dx2, dx2b, d_norm_final = _loss_head(x2, target, norm_final.reshape(1, D))

    def nt_rows(name, a, w, epilogue, extras, outs, bn=1024):
        M, N = a.shape
        K = w.shape[0]
        bm, bn, bk = _tile(M, 1024), _tile(K, bn), _tile(N, 2048)
        return _matmul(name, a, w, pl.BlockSpec((bm, bk), lambda i, j, k: (i, k)),
                       pl.BlockSpec((bn, bk), lambda i, j, k: (j, k)), NT, (M // bm, K // bn, N // bk), (bm, bn),
                       extras(bm, bn), outs(bm, bn), epilogue)

    def nt_cols(name, a_spec_fn, a, g, M, epilogue, extras, outs, bk):
        _, K, Nq = g.shape
        bm, bn, bk = _tile(M, 1024), _tile(K, 1024), _tile(Nq, bk)
        q = Nq // bk
        return _matmul(name, a, g, a_spec_fn(bm, bk), pl.BlockSpec((None, bn, bk), lambda i, j, k: (k // q, j, k % q)),
                       NT, (M // bm, K // bn, N_CHIPS * q), (bm, bn), extras(bm, bn), outs(bm, bn), epilogue)

    def tn_grad(name, a, a_spec_fn, b, b_spec_fn, Kin, N, out_shape, out_spec_fn, bn=1024):
        bm, bn, bk = _tile(Kin, 1024), _tile(N, bn), _tile(S, 2048)
        return _matmul(name, a, b, a_spec_fn(bk, bm), b_spec_fn(bk, bn), TN, (Kin // bm, N // bn, S // bk), (bm, bn),
                       [], [(_sds(out_shape, BF), out_spec_fn(bm, bn))], _store(BF))[0]

    plain_a = lambda bk, bm: pl.BlockSpec((bk, bm), lambda i, j, k: (k, i))
    plain_b = lambda bk, bn: pl.BlockSpec((bk, bn), lambda i, j, k: (k, j))
    plain_o = lambda bm, bn: pl.BlockSpec((bm, bn), lambda i, j, k: (i, j))
    a_rows = lambda bm, bk: pl.BlockSpec((bm, bk), lambda i, j, k: (i, k))

    def cols_o(Nq):
        def spec(bm, bn):
            q = Nq // bn
            return pl.BlockSpec((None, bm, bn), lambda i, j, k: (j // q, i, j % q))
        return spec

    def du_epilogue(acc, ex_, outs):
        outs[0][...] = (acc * (2.0 * ex_[0][...].astype(F32))).astype(BF)

    dw_down = tn_grad("mlp_down_dw", act, plain_a, dx2b, plain_b, F, D, (F, D), plain_o)
    (du,) = nt_rows("mlp_down_dx", dx2b, wdown, du_epilogue,
                    lambda bm, bn: [(ru, plain_o(bm, bn))], lambda bm, bn: [(_sds((S, F), BF), plain_o(bm, bn))])

    fq = gup.shape[2]
    dw_up = tn_grad("mlp_up_dw", h2, plain_a, du, plain_b, D, F, (N_CHIPS, D, fq), cols_o(fq), bn=min(fq, 1024))
    ex.reduce("mlp", partials=[dw_down.reshape(N_CHIPS, F // N_CHIPS, D), dw_up])
    (dh2,) = nt_cols("mlp_up_dx", a_rows, du, gup, S, _store(F32), lambda bm, bn: [],
                     lambda bm, bn: [(_sds((S, D), F32), plain_o(bm, bn))], 2048)
    ex.reduce("mlp")
    dx1, dx1b, d_norm_mlp = _rms_bwd("rms_mlp_bwd", dh2, x1, norm_mlp, dx2)

    def merge_bwd_epilogue(acc, ex_, outs):
        g, c = ex_[0][...].astype(F32), ex_[1][...].astype(F32)
        outs[0][...] = (acc * g[0]).astype(BF)
        outs[1][...] = (acc * g[1]).astype(BF)
        dga = acc * c[0]
        dgb = acc * c[1]
        outs[2][0] = dga.astype(BF)
        outs[2][1] = dgb.astype(BF)
        outs[3][...] = jnp.concatenate([jnp.sum(dga, axis=0, keepdims=True), jnp.sum(dgb, axis=0, keepdims=True)], 0)

    def pair(bm, bn):
        return pl.BlockSpec((2, bm, bn), lambda i, j, k: (0, i, j))

    n_row_blocks = S // _tile(S, 1024)
    dpa, dpb, dg3, db_gate = nt_rows(
        "out_proj_dx", dx1b, wout, merge_bwd_epilogue,
        lambda bm, bn: [(g3, pair(bm, bn)), (c3, pair(bm, bn))],
        lambda bm, bn: [(_sds((S, D), BF), plain_o(bm, bn)), (_sds((S, D), BF), plain_o(bm, bn)),
                        (_sds((2, S, D), BF), pair(bm, bn)),
                        (_sds((n_row_blocks, 2, D), F32), pl.BlockSpec((None, 2, bn), lambda i, j, k: (i, 0, j)))],
        bn=512)
    dw_out = tn_grad("out_proj_dw", merged, plain_a, dx1b, plain_b, D, D, (D, D), plain_o)

    pq = gpa.shape[2]
    proj_dx = lambda name, dproj, g: nt_cols(name, a_rows, dproj, g, S, _store(BF), lambda bm, bn: [],
                                             lambda bm, bn: [(_sds((S, 512), BF), plain_o(bm, bn))], 512)[0]
    dw_pa = tn_grad("proj_a_dw", y_a, plain_a, dpa, plain_b, 512, D, (N_CHIPS, 512, pq), cols_o(pq), bn=min(pq, 512))
    dw_pb = tn_grad("proj_b_dw", y_b, plain_a, dpb, plain_b, 512, D, (N_CHIPS, 512, pq), cols_o(pq), bn=min(pq, 512))
    ex.reduce("mix", partials=[dw_out.reshape(N_CHIPS, D // N_CHIPS, D), dw_pa, dw_pb])
    dy_a = proj_dx("proj_a_dx", dpa, gpa)
    dy_b = proj_dx("proj_b_dx", dpb, gpb)

    dqkv3 = lax.empty((3, S, QKV_W), BF)
    dqkv3 = _attn_a_bwd(qkv3, dy_a, y_a, lj, dqkv3, 0, DILATIONS[0])
    ex.reduce("mix")
    dy_views, y_views, lj_views = _dilated_rows("attn_a_bwd_rows", [dy_a, y_a, lj])
    for grp, d in enumerate(DILATIONS):
        if grp > 0:
            dqkv3 = _attn_a_bwd(qkv3, dy_views[d], y_views[d], lj_views[d], dqkv3, grp, d)
    dqkv3, de2 = _attn_b_bwd(qkv3, e2, dy_b, y_b, lse_b, dqkv3)
    d_rpb = _table_grad_to_rpb(de2)

    def stacked_a(width):
        def spec(bm, bk):
            q = width // bk
            return pl.BlockSpec((None, bm, bk), lambda i, j, k: (k // q, i, k % q))
        return spec

    def stacked_b(width):
        def spec(bk, bn):
            q = width // bn
            return pl.BlockSpec((None, bk, bn), lambda i, j, k: (j // q, k, j % q))
        return spec

    dw_qkv = tn_grad("qkv_dw", h1, plain_a, dqkv3, stacked_b(QKV_W), D, 3 * QKV_W, (N_CHIPS,) + gq.shape[1:],
                     cols_o(gq.shape[2]), bn=512)
    dw_gate = tn_grad("gate_dw", h1, plain_a, dg3, stacked_b(D), D, 2 * D, (N_CHIPS,) + gg.shape[1:],
                      cols_o(gg.shape[2]), bn=gg.shape[2])
    ex.reduce("in", partials=[dw_qkv, dw_gate])
    ex.reduce("mlp")
    (dh1_q,) = nt_cols("qkv_dx", stacked_a(QKV_W), dqkv3, gq, S, _store(F32), lambda bm, bn: [],
                       lambda bm, bn: [(_sds((S, D), F32), plain_o(bm, bn))], 512)
    ex.reduce("in")
    ex.reduce("mix")

    def add_epilogue(acc, ex_, outs):
        outs[0][...] = acc + ex_[0][...]

    (dh1,) = nt_cols("gate_dx", stacked_a(D), dg3, gg, S, add_epilogue, lambda bm, bn: [(dh1_q, plain_o(bm, bn))],
                     lambda bm, bn: [(_sds((S, D), F32), plain_o(bm, bn))], gg.shape[2])
    grad_x, _, d_norm_mix = _rms_bwd("rms_mix_bwd", dh1, x, norm_mix, dx1)
    ex.reduce("mlp")
    ex.reduce("mix")

    small = [d_norm_mix, jnp.sum(db_gate, axis=0).reshape(1, 2 * D), d_rpb, d_norm_mlp, d_norm_final]
    return loss, grad_x, small


def _pack_small(parts, width):
    flat = jnp.concatenate([p.reshape(-1) for p in parts])
    return jnp.pad(flat, (0, 8 * width - flat.shape[0])).reshape(8, width)


def kernel(x, norm_mix, w_qkv, w_gate, b_gate, rpb, w_proj_a, w_proj_b, w_out, norm_mlp, w_up, w_down, norm_final, loss_target, m_norm_mix, m_w_qkv, m_w_gate, m_b_gate, m_rpb, m_w_proj_a, m_w_proj_b, m_w_out, m_norm_mlp, m_w_up, m_w_down, m_norm_final, v_norm_mix, v_w_qkv, v_w_gate, v_b_gate, v_rpb, v_w_proj_a, v_w_proj_b, v_w_out, v_norm_mlp, v_w_up, v_w_down, v_norm_final):
    names = ["qkv", "gate", "proj_a", "proj_b", "out", "up", "down"]
    big = dict(zip(names, [w_qkv[0], w_gate[0], w_proj_a[0], w_proj_b[0], w_out[0], w_up[0], w_down[0]]))
    big_m = dict(zip(names, [m_w_qkv[0], m_w_gate[0], m_w_proj_a[0], m_w_proj_b[0], m_w_out[0], m_w_up[0], m_w_down[0]]))
    big_v = dict(zip(names, [v_w_qkv[0], v_w_gate[0], v_w_proj_a[0], v_w_proj_b[0], v_w_out[0], v_w_up[0], v_w_down[0]]))

    c = lax.axis_index("c").astype(jnp.int32).reshape(1)
    me = (2 * lax.axis_index("x") + lax.axis_index("y")).astype(jnp.int32).reshape(1)
    ORDER.last = None
    ex = _Exchange(big, me, c)
    loss, grad_x, small = _forward_backward(x[0], loss_target[0], norm_mix, b_gate, rpb[0], norm_mlp, norm_final, ex)

    def adamw(group):
        return {n: _adamw(f"adamw_{n}", big[n], ex.grads[n], big_m[n], big_v[n]) for n in _Exchange.REDUCE[group]}

    big_out = {**adamw("mlp"), **adamw("mix")}
    ex.reduce("in")

    small_w = [norm_mix, b_gate, rpb, norm_mlp, norm_final]
    count = sum(int(np.prod(p.shape)) for p in small_w)
    width = -(-count // (8 * 128)) * 128
    packed = _adamw_small(_gather_small(_pack_small(small, width)), _pack_small(small_w, width),
                          _pack_small([m_norm_mix, m_b_gate, m_rpb, m_norm_mlp, m_norm_final], width),
                          _pack_small([v_norm_mix, v_b_gate, v_rpb, v_norm_mlp, v_norm_final], width))
    ex.reduce("in")
    big_out.update(adamw("in"))

    def unpack(flat2d):
        flat, out, at = flat2d.reshape(-1), [], 0
        for p in small_w:
            size = int(np.prod(p.shape))
            out.append(flat[at:at + size].reshape(p.shape))
            at += size
        return out

    small_out = [unpack(a) for a in packed]

    def ordered(kind):
        sm = small_out[kind]
        bg = {n: o[kind][None] for n, o in big_out.items()}
        return [sm[0], bg["qkv"], bg["gate"], sm[1], sm[2], bg["proj_a"], bg["proj_b"], bg["out"], sm[3],
                bg["up"], bg["down"], sm[4]]

    total = lax.psum(loss[0, 0], ("x", "y", "c"))
    return (total, grad_x[None], *ordered(0), *ordered(1), *ordered(2), *ordered(3))
```

```python
import functools
import math

import numpy as np
import jax
import jax.numpy as jnp
from jax import lax
from jax.experimental import pallas as pl
from jax.experimental.pallas import tpu as pltpu

BF = jnp.bfloat16
F32 = jnp.float32
MESH = pl.DeviceIdType.MESH

HEAD_DIM = 128
N_HEADS = 16
N_HEADS_A = 12
QKV_W = N_HEADS * HEAD_DIM
DILATIONS = (1, 4, 16)
HALF_WINDOW = 64
GRID_W = 64
NA_ROWS = 8
NA_COLS = 16
RPB_ROWS = 2 * NA_ROWS - 1
RPB_COLS = 2 * NA_COLS - 1
EPS = 1e-6
NEG = -1e30
SCALE = HEAD_DIM ** -0.5

ADAM_LR = 0.001
ADAM_B1 = 0.9
ADAM_B2 = 0.999
ADAM_EPS = 1e-08
ADAM_WD = 0.01
ADAM_STEP = 10

N_CHIPS = 4
VMEM_LIMIT_BYTES = 48 * 1024 * 1024
QB = 256
NBR_SIDE = 4
ROW_TILE = 512


def _key_rows(L):
    return min(QB + 2 * HALF_WINDOW, L)


def _cparams(sem=None):
    return pltpu.CompilerParams(dimension_semantics=sem, vmem_limit_bytes=VMEM_LIMIT_BYTES)


def _tile(dim, want):
    t = min(dim, want)
    assert dim % t == 0, (dim, want)
    return t


class _ProgramOrder:
    def __init__(self):
        self.last = None

    def call(self, body, operands, in_specs, *, prefetch=(), grid=None, out_specs=None, chain_output=0, **kwargs):
        operands, in_specs = list(operands), list(in_specs)
        lead = len(prefetch) + len(operands)
        if self.last is not None and not any(op is self.last for op in operands):
            operands.append(self.last)
            in_specs.append(pl.BlockSpec(memory_space=pl.ANY))
            inner = body

            def body(*refs):
                return inner(*refs[:lead], *refs[lead + 1:])

        if prefetch:
            kwargs["grid_spec"] = pltpu.PrefetchScalarGridSpec(
                num_scalar_prefetch=len(prefetch), grid=grid, in_specs=in_specs, out_specs=out_specs)
        else:
            kwargs.update(in_specs=in_specs, out_specs=out_specs)
            if grid is not None:
                kwargs["grid"] = grid
        out = pl.pallas_call(body, **kwargs)(*prefetch, *operands)
        self.last = out[chain_output] if isinstance(out, (tuple, list)) else out
        return out


ORDER = _ProgramOrder()


NN = ((1,), (0,))
NT = ((1,), (1,))
TN = ((0,), (0,))


def _matmul(name, a, b, a_spec, b_spec, dims, grid, acc_shape, extras, outs, epilogue, precision=None,
            prefetch=(), into=None):
    n_ex, n_out, nk = len(extras), len(outs), grid[2]
    n_in = 2 + n_ex + (into is not None)

    def body(*refs):
        refs = refs[len(prefetch):]
        a_ref, b_ref = refs[0], refs[1]
        ex_refs = refs[2:2 + n_ex]
        out_refs = refs[n_in:n_in + n_out]

        def dot():
            return lax.dot_general(a_ref[...], b_ref[...], (dims, ((), ())),
                                   preferred_element_type=F32, precision=precision)

        if nk == 1:
            epilogue(dot(), ex_refs, out_refs)
            return
        acc_ref = refs[-1]
        k = pl.program_id(2)

        @pl.when(k == 0)
        def _():
            acc_ref[...] = dot()

        if nk > 2:
            @pl.when((k > 0) & (k < nk - 1))
            def _():
                acc_ref[...] += dot()

        @pl.when(k == nk - 1)
        def _():
            epilogue(acc_ref[...] + dot(), ex_refs, out_refs)

    operands = [a, b] + [e for e, _ in extras]
    in_specs = [a_spec, b_spec] + [s for _, s in extras]
    kwargs = {}
    if into is not None:
        operands.append(into)
        in_specs.append(pl.BlockSpec(memory_space=pl.ANY))
        kwargs["input_output_aliases"] = {len(prefetch) + n_in - 1: 0}
    return ORDER.call(
        body, operands, in_specs, prefetch=prefetch, name=name, grid=grid,
        out_specs=[s for _, s in outs],
        out_shape=[sh for sh, _ in outs],
        scratch_shapes=[pltpu.VMEM(acc_shape, F32)] if nk > 1 else [],
        compiler_params=_cparams(("parallel", "parallel", "arbitrary")), **kwargs,
    )


def _mm_nn_shards(name, a, w, me, own, out, out_block, epilogue, extras=(), into=None, tn=512):
    M, K = a.shape
    Nq = w.shape[-1]
    tm, tn = _tile(M, 1024), _tile(Nq, tn)
    q = Nq // tn

    def tile(j, me_ref):
        shard = me_ref[0] if own else (me_ref[0] + 1 + j // q) % N_CHIPS
        return shard, j % q, shard * q + j % q

    if own:
        b_spec = pl.BlockSpec((K, tn), lambda i, j, k, me_ref: (0, j))
    else:
        b_spec = pl.BlockSpec((None, K, tn), lambda i, j, k, me_ref: (tile(j, me_ref)[0], 0, tile(j, me_ref)[1]))
    shape, dtype = out
    out_spec = pl.BlockSpec((None, tm, tn), lambda i, j, k, me_ref: out_block(i, tile(j, me_ref)[2]))
    ex = [(e, pl.BlockSpec((1, tn), lambda i, j, k, me_ref: (0, tile(j, me_ref)[2]))) for e in extras]
    return _matmul(name, a, w, pl.BlockSpec((tm, K), lambda i, j, k, me_ref: (i, 0)), b_spec, NN,
                   (M // tm, q if own else (N_CHIPS - 1) * q, 1), (tm, tn), ex, [(_sds(shape, dtype), out_spec)],
                   epilogue, prefetch=(me,), into=into)[0]


def _store(dtype):
    def epilogue(acc, ex, outs):
        outs[0][...] = acc.astype(dtype)
    return epilogue


def _sds(shape, dtype):
    return jax.ShapeDtypeStruct(shape, dtype)


def _mm_nn_cols(name, a, g, out_dtype, epilogue=None, extras=(), outs=None, tm=1024, tn=1024, tk=2048):
    M, K = a.shape
    _, _, Nq = g.shape
    tm, tn, tk = _tile(M, tm), _tile(Nq, tn), _tile(K, tk)
    q = Nq // tn
    grid = (M // tm, N_CHIPS * q, K // tk)
    if outs is None:
        outs = [(_sds((M, N_CHIPS * Nq), out_dtype), pl.BlockSpec((tm, tn), lambda i, j, k: (i, j)))]
    return _matmul(name, a, g, pl.BlockSpec((tm, tk), lambda i, j, k: (i, k)),
                   pl.BlockSpec((None, tk, tn), lambda i, j, k: (j // q, k, j % q)), NN, grid, (tm, tn),
                   list(extras), outs, epilogue or _store(out_dtype)), (tm, tn, tk)


def _rms_fwd(name, x, g):
    S, D = x.shape
    tm = _tile(S, ROW_TILE)

    def body(x_ref, g_ref, h_ref):
        xv = x_ref[...]
        r = lax.rsqrt(jnp.mean(xv * xv, axis=-1, keepdims=True) + EPS)
        h_ref[...] = ((xv * r) * g_ref[...]).astype(BF)

    row = pl.BlockSpec((tm, D), lambda i: (i, 0))
    return ORDER.call(
        body, [x, g], [row, pl.BlockSpec((1, D), lambda i: (0, 0))], name=name, grid=(S // tm,),
        out_specs=row, out_shape=_sds((S, D), BF), compiler_params=_cparams(("parallel",)),
    )


def _rms_bwd(name, dh, x, g, dres):
    S, D = x.shape
    tm = _tile(S, ROW_TILE // 2)

    def body(dh_ref, x_ref, g_ref, dres_ref, dx_ref, dxb_ref, dg_ref):
        xv = x_ref[...]
        r = lax.rsqrt(jnp.mean(xv * xv, axis=-1, keepdims=True) + EPS)
        n = xv * r
        dhv = dh_ref[...]
        dyg = dhv * g_ref[...]
        dx = dres_ref[...] + r * (dyg - n * jnp.mean(dyg * n, axis=-1, keepdims=True))
        dx_ref[...] = dx
        dxb_ref[...] = dx.astype(BF)

        @pl.when(pl.program_id(0) == 0)
        def _():
            dg_ref[...] = jnp.zeros_like(dg_ref)

        dg_ref[...] += jnp.sum(dhv * n, axis=0, keepdims=True)

    row = pl.BlockSpec((tm, D), lambda i: (i, 0))
    vec = pl.BlockSpec((1, D), lambda i: (0, 0))
    return ORDER.call(
        body, [dh, x, g, dres], [row, row, vec, row], name=name, grid=(S // tm,),
        out_specs=[row, row, vec],
        out_shape=[_sds((S, D), F32), _sds((S, D), BF), _sds((1, D), F32)],
        compiler_params=_cparams(("arbitrary",)),
    )


def _loss_head(x2, target, g):
    S, D = x2.shape
    tm = _tile(S, ROW_TILE)

    def body(x_ref, t_ref, g_ref, loss_ref, dx_ref, dxb_ref, dg_ref):
        xv = x_ref[...]
        gv = g_ref[...]
        r = lax.rsqrt(jnp.mean(xv * xv, axis=-1, keepdims=True) + EPS)
        n = xv * r
        e = n * gv - t_ref[...]
        dy = e * (1.0 / D)
        dyg = dy * gv
        dx = r * (dyg - n * jnp.mean(dyg * n, axis=-1, keepdims=True))
        dx_ref[...] = dx
        dxb_ref[...] = dx.astype(BF)

        @pl.when(pl.program_id(0) == 0)
        def _():
            dg_ref[...] = jnp.zeros_like(dg_ref)
            loss_ref[...] = jnp.zeros_like(loss_ref)

        dg_ref[...] += jnp.sum(dy * n, axis=0, keepdims=True)
        per_row = jnp.mean(e * e, axis=-1, keepdims=True)
        loss_ref[...] += 0.5 * jnp.sum(per_row, axis=0, keepdims=True)

    row = pl.BlockSpec((tm, D), lambda i: (i, 0))
    vec = pl.BlockSpec((1, D), lambda i: (0, 0))
    return ORDER.call(
        body, [x2, target, g], [row, row, vec], name="loss_head", grid=(S // tm,),
        out_specs=[pl.BlockSpec((1, 1), lambda i: (0, 0)), row, row, vec],
        out_shape=[_sds((1, 1), F32), _sds((S, D), F32), _sds((S, D), BF), _sds((1, D), F32)],
        compiler_params=_cparams(("arbitrary",)), chain_output=1,
    )


def _chains(L):
    side = min(4, L // QB)
    return side, 4 // side


def _band_scores(qkv_ref, i, L, coef, head):
    KB = _key_rows(L)
    lanes = pl.ds(head * HEAD_DIM, HEAD_DIM)
    q0 = pl.multiple_of(i * QB, QB)
    ks = pl.multiple_of(jnp.clip(i * QB - HALF_WINDOW, 0, L - KB), HALF_WINDOW)
    q = qkv_ref[0, pl.ds(q0, QB), lanes]
    k = qkv_ref[1, pl.ds(ks, KB), lanes]
    v = qkv_ref[2, pl.ds(ks, KB), lanes]
    s = lax.dot_general(q, k, (NT, ((), ())), preferred_element_type=F32) * SCALE
    qpos = q0 + lax.broadcasted_iota(jnp.int32, (QB, KB), 0)
    kpos = ks + lax.broadcasted_iota(jnp.int32, (QB, KB), 1)
    rel = jnp.abs(kpos - qpos)
    valid = rel <= HALF_WINDOW
    s = jnp.where(valid, s - coef * rel.astype(F32), NEG)
    return q0, ks, q, k, v, s, valid


def _alibi_coefs(group, d, heads):
    first = 4 * group + 1 + pl.program_id(1) * heads
    scale = jnp.full((1, 1), -(8.0 / N_HEADS_A) * math.log(2.0), F32)
    return [jnp.exp(scale * (first + hh).astype(F32)) * float(d) for hh in range(heads)]


def _dilated_view(qkv3, group, d, heads):
    per = 4 // heads
    L = qkv3.shape[1]
    if d == 1:
        return qkv3, pl.BlockSpec((3, L, heads * HEAD_DIM), lambda r, j: (0, 0, per * group + j))
    return qkv3, pl.BlockSpec((3, L, heads * HEAD_DIM), lambda r, j: (0, 0, r * per + j))


def _qkv_views(name, qkv3, views=None):
    _, S, _ = qkv3.shape
    W = 512
    tm = _tile(S, 256)
    dilated = [(g, d) for g, d in enumerate(DILATIONS) if d > 1]
    first = dilated[0][0]
    assert [g for g, _ in dilated] == list(range(first, first + len(dilated)))
    nc = W // 128
    to_views = views is None

    def body(*refs):
        scr = refs[-nc:]
        if to_views:
            src, outs = refs[0], refs[1:1 + len(dilated)]
        else:
            ins, dst = refs[:len(dilated)], refs[len(dilated) + 1]
        for k, (_, d) in enumerate(dilated):
            @pl.when(pl.program_id(1) == k)
            def _():
                for w in range(3):
                    for c in range(nc):
                        if to_views:
                            scr[c][...] = src[w, :, c * 128:(c + 1) * 128].astype(F32)
                    for r in range(d):
                        for c in range(nc):
                            at = r * W + c * 128
                            if to_views:
                                outs[k][w, :, at:at + 128] = scr[c][pl.ds(r, tm // d, stride=d), :].astype(BF)
                            else:
                                scr[c][pl.ds(r, tm // d, stride=d), :] = ins[k][w, :, at:at + 128].astype(F32)
                    for c in range(nc):
                        if not to_views:
                            dst[w, :, c * 128:(c + 1) * 128] = scr[c][...].astype(BF)

    cols = pl.BlockSpec((3, tm, W), lambda i, k: (0, i, first + k))
    rows = [pl.BlockSpec((3, tm // d, d * W), lambda i, k: (0, i, 0)) for _, d in dilated]
    shapes = [_sds((3, S // d, d * W), BF) for _, d in dilated]
    common = dict(name=name, grid=(S // tm, len(dilated)), scratch_shapes=[pltpu.VMEM((tm, 128), F32)] * nc,
                  compiler_params=_cparams(("parallel", "arbitrary")))
    if to_views:
        outs = ORDER.call(body, [qkv3], [cols], out_specs=rows, out_shape=shapes, **common)
        return {d: o for (_, d), o in zip(dilated, outs)}
    return ORDER.call(body, [views[d] for _, d in dilated] + [qkv3], rows + [pl.BlockSpec(memory_space=pl.ANY)],
                      out_specs=cols, out_shape=_sds(qkv3.shape, BF), input_output_aliases={len(dilated): 0}, **common)


def _attn_a_fwd(qkv3, group, d):
    L = qkv3.shape[1]
    S = L * d
    assert L % QB == 0
    side, heads = _chains(L)
    view, blocks_spec = _dilated_view(qkv3, group, d, heads)

    def body(qkv_ref, o_ref, lse_ref):
        coefs = _alibi_coefs(group, d, heads)

        def step(i, carry):
            chains = [(hh, _band_scores(qkv_ref, side * i + u, L, coefs[hh], hh))
                      for u in range(side) for hh in range(heads)]
            soft = []
            for hh, (q0, _, _, _, v, s, _) in chains:
                m = jnp.max(s, axis=-1, keepdims=True)
                p = jnp.exp(s - m)
                den = jnp.sum(p, axis=-1, keepdims=True)
                soft.append((hh, q0, (p / den).astype(BF), v, m + jnp.log(den)))
            for hh, q0, pn, v, lse in soft:
                lanes = pl.ds(hh * HEAD_DIM, HEAD_DIM)
                o_ref[pl.ds(q0, QB), lanes] = jnp.dot(pn, v, preferred_element_type=F32)
                lse_ref[pl.ds(q0, QB), lanes] = jnp.broadcast_to(lse, (QB, HEAD_DIM))
            return carry

        lax.fori_loop(0, L // QB // side, step, 0)

    per = 4 // heads
    out = pl.BlockSpec((L, heads * HEAD_DIM), lambda r, j: (0, r * per + j))
    o, lse = ORDER.call(
        body, [view], [blocks_spec],
        name=f"attn_a_fwd_d{d}", grid=(d, per),
        out_specs=[out, out],
        out_shape=[_sds((L, d * 512), F32), _sds((L, d * 512), F32)],
        compiler_params=_cparams(("parallel", "parallel")),
    )
    return o, lse


def _dilated_rows(name, arrays):
    S, W = arrays[0].shape
    tm = _tile(S, 256)
    ds_ = [d for d in DILATIONS if d > 1]
    n = len(arrays)

    def body(*refs):
        nc = W // 128
        ins, outs, scr = refs[:n], refs[n:-nc], refs[-nc:]
        for a, src in enumerate(ins):
            for c in range(nc):
                scr[c][...] = src[:, c * 128:(c + 1) * 128].astype(F32)
            for k, d in enumerate(ds_):
                dst = outs[a * len(ds_) + k]
                for r in range(d):
                    for c in range(nc):
                        at = r * W + c * 128
                        dst[:, at:at + 128] = scr[c][pl.ds(r, tm // d, stride=d), :].astype(dst.dtype)

    row = pl.BlockSpec((tm, W), lambda i: (i, 0))
    out_specs, out_shape = [], []
    for a in arrays:
        for d in ds_:
            out_specs.append(pl.BlockSpec((tm // d, d * W), lambda i: (i, 0)))
            out_shape.append(_sds((S // d, d * W), a.dtype))
    outs = ORDER.call(body, list(arrays), [row] * n, name=name, grid=(S // tm,), out_specs=out_specs,
                      out_shape=out_shape, scratch_shapes=[pltpu.VMEM((tm, 128), F32)] * (W // 128),
                      compiler_params=_cparams(("parallel",)))
    return [{d: outs[a * len(ds_) + k] for k, d in enumerate(ds_)} for a in range(n)]


def _attn_a_combine(os_, lses):
    W = 512
    S = os_[0].shape[0] * DILATIONS[0]
    tm = _tile(S, 256)
    nc = W // 128
    dilated = [g for g, d in enumerate(DILATIONS) if d > 1]

    def body(o0, o1, o2, l0, l1, l2, y_ref, lj_ref, *scr):
        def token_order(src, g, slot):
            d = DILATIONS[g]
            if d == 1:
                return src[...]
            bufs = scr[slot * nc:(slot + 1) * nc]
            for r in range(d):
                for c in range(nc):
                    at = r * W + c * 128
                    bufs[c][pl.ds(r, tm // d, stride=d), :] = src[:, at:at + 128]
            return jnp.concatenate([buf[...] for buf in bufs], axis=1)

        slots = {g: k for k, g in enumerate(dilated)}
        ls = [token_order(l, g, slots.get(g, 0)) for g, l in enumerate((l0, l1, l2))]
        os_tok = [token_order(o, g, len(dilated) + slots.get(g, 0)) for g, o in enumerate((o0, o1, o2))]
        m = jnp.maximum(jnp.maximum(ls[0], ls[1]), ls[2])
        es = [jnp.exp(l - m) for l in ls]
        den = es[0] + es[1] + es[2]
        y = (es[0] / den) * os_tok[0] + (es[1] / den) * os_tok[1] + (es[2] / den) * os_tok[2]
        y_ref[...] = y.astype(BF)
        lj_ref[...] = m + jnp.log(den)

    row = pl.BlockSpec((tm, W), lambda i: (i, 0))
    views = [pl.BlockSpec((tm // d, d * W), lambda i: (i, 0)) for d in DILATIONS]
    return ORDER.call(
        body, [*os_, *lses], views + views, name="attn_a_combine", grid=(S // tm,), out_specs=[row, row],
        out_shape=[_sds((S, W), BF), _sds((S, W), F32)],
        scratch_shapes=[pltpu.VMEM((tm, 128), F32)] * (2 * len(dilated) * nc),
        compiler_params=_cparams(("parallel",)),
    )


def _attn_a_bwd(qkv3, dy, y, lj, dqkv3, group, d):
    L = qkv3.shape[1]
    S = L * d
    side, heads = _chains(L)
    view, blocks_spec = _dilated_view(qkv3, group, d, heads)

    def body(qkv_ref, dy_ref, y_ref, lj_ref, *rest):
        out_ref, dk_acc, dv_acc = rest[-3:]
        coefs = _alibi_coefs(group, d, heads)
        dk_acc[...] = jnp.zeros_like(dk_acc)
        dv_acc[...] = jnp.zeros_like(dv_acc)

        def step(i, carry):
            chains = [(pl.ds(hh * HEAD_DIM, HEAD_DIM), _band_scores(qkv_ref, side * i + u, L, coefs[hh], hh))
                      for u in range(side) for hh in range(heads)]
            dys = [dy_ref[pl.ds(c[0], QB), lanes] for lanes, c in chains]
            dps = [lax.dot_general(dyv, c[4], (NT, ((), ())), preferred_element_type=F32)
                   for dyv, (_, c) in zip(dys, chains)]
            grads = []
            for (lanes, (q0, ks, q, k, v, s, valid)), dyv, dp in zip(chains, dys, dps):
                rows = pl.ds(q0, QB)
                delta = jnp.sum(dyv.astype(F32) * y_ref[rows, lanes].astype(F32), axis=-1, keepdims=True)
                p = jnp.where(valid, jnp.exp(s - jnp.tile(lj_ref[rows, lanes], (1, _key_rows(L) // HEAD_DIM))), 0.0)
                grads.append(((p * (dp - delta)).astype(BF), p.astype(BF)))
            for (lanes, (q0, ks, q, k, v, s, valid)), dyv, (ds, pb) in zip(chains, dys, grads):
                out_ref[0, pl.ds(q0, QB), lanes] = (jnp.dot(ds, k, preferred_element_type=F32) * SCALE).astype(BF)
                keys = pl.ds(ks, _key_rows(L))
                dk_acc[keys, lanes] += lax.dot_general(ds, q, (TN, ((), ())), preferred_element_type=F32) * SCALE
                dv_acc[keys, lanes] += lax.dot_general(pb, dyv, (TN, ((), ())), preferred_element_type=F32)
            return carry

        lax.fori_loop(0, L // QB // side, step, 0)
        out_ref[1] = dk_acc[...].astype(BF)
        out_ref[2] = dv_acc[...].astype(BF)

    per = 4 // heads
    width = heads * HEAD_DIM
    row = pl.BlockSpec((L, width), lambda r, j: (0, r * per + j))
    operands = [view, dy, y, lj]
    scratch = [pltpu.VMEM((L, width), F32), pltpu.VMEM((L, width), F32)]
    if d == 1:
        return ORDER.call(
            body, operands + [dqkv3], [blocks_spec, row, row, row, pl.BlockSpec(memory_space=pl.ANY)],
            name=f"attn_a_bwd_d{d}", grid=(d, per), out_specs=blocks_spec, out_shape=_sds((3, S, QKV_W), BF),
            scratch_shapes=scratch, input_output_aliases={4: 0}, compiler_params=_cparams(("parallel", "parallel")))
    return ORDER.call(
        body, operands, [blocks_spec, row, row, row], name=f"attn_a_bwd_d{d}", grid=(d, per),
        out_specs=blocks_spec, out_shape=_sds((3, L, d * 512), BF),
        scratch_shapes=scratch, compiler_params=_cparams(("parallel", "parallel")))


def _toeplitz_onehot():
    oh = np.zeros((64, GRID_W, 128), np.float32)
    for qc in range(GRID_W):
        for m in range(128):
            kc = m % GRID_W
            dc = int(np.clip(kc - qc, -(NA_COLS - 1), NA_COLS - 1)) + NA_COLS - 1
            oh[(m // GRID_W) * 32 + dc, qc, m] = 1.0
    return oh.reshape(64, GRID_W * 128)


def _nbr_scores(qkv_ref, e2_ref, r, rows, ok):
    rs = jnp.clip(r - NA_ROWS // 2, 0, rows - NA_ROWS)
    q0 = pl.multiple_of(r * GRID_W, GRID_W)
    k0 = pl.multiple_of(rs * GRID_W, GRID_W)
    q = qkv_ref[0, pl.ds(q0, GRID_W), :]
    k = qkv_ref[1, pl.ds(k0, NA_ROWS * GRID_W), :]
    v = qkv_ref[2, pl.ds(k0, NA_ROWS * GRID_W), :]
    s = lax.dot_general(q, k, (NT, ((), ())), preferred_element_type=F32) * SCALE
    first = rs - r + NA_ROWS - 1
    bias = jnp.concatenate([e2_ref[first + 2 * pair] for pair in range(NA_ROWS // 2)], axis=1)
    s = jnp.where(ok, s + bias, NEG)
    return q0, k0, first, q, k, v, s


def _nbr_col_ok():
    qc = lax.broadcasted_iota(jnp.int32, (GRID_W, NA_ROWS * GRID_W), 0)
    kc = lax.broadcasted_iota(jnp.int32, (GRID_W, NA_ROWS * GRID_W), 1) % GRID_W
    cs = jnp.clip(qc - NA_COLS // 2, 0, GRID_W - NA_COLS)
    return (kc >= cs) & (kc < cs + NA_COLS)


def _attn_b_fwd(qkv3, e2):
    _, S, _ = qkv3.shape
    rows = S // GRID_W
    assert rows >= NA_ROWS

    def body(qkv_ref, e2_ref, o_ref, lse_ref):
        ok = _nbr_col_ok()

        def step(i, carry):
            blocks = [_nbr_scores(qkv_ref, e2_ref, NBR_SIDE * i + u, rows, ok) for u in range(NBR_SIDE)]
            soft = []
            for q0, _, _, _, _, v, s in blocks:
                m = jnp.max(s, axis=-1, keepdims=True)
                p = jnp.exp(s - m)
                den = jnp.sum(p, axis=-1, keepdims=True)
                soft.append((q0, (p / den).astype(BF), v, m + jnp.log(den)))
            for q0, pn, v, lse in soft:
                o_ref[pl.ds(q0, GRID_W), :] = jnp.dot(pn, v, preferred_element_type=F32).astype(BF)
                lse_ref[pl.ds(q0, GRID_W), :] = jnp.broadcast_to(lse, (GRID_W, HEAD_DIM))
            return carry

        lax.fori_loop(0, rows // NBR_SIDE, step, 0)

    out = pl.BlockSpec((S, HEAD_DIM), lambda h: (0, h))
    return ORDER.call(
        body, [qkv3, e2],
        [pl.BlockSpec((3, S, HEAD_DIM), lambda h: (0, 0, N_HEADS_A + h)),
         pl.BlockSpec((None, RPB_ROWS - 1, GRID_W, 128), lambda h: (h, 0, 0, 0))],
        name="attn_b_fwd", grid=(4,),
        out_specs=[out, out], out_shape=[_sds((S, 512), BF), _sds((S, 512), F32)],
        compiler_params=_cparams(("parallel",)),
    )


def _attn_b_bwd(qkv3, e2, dy, y, lse, dqkv3):
    _, S, _ = qkv3.shape
    rows = S // GRID_W
    nk = NA_ROWS * GRID_W

    def body(qkv_ref, e2_ref, dy_ref, y_ref, lse_ref, _, out_ref, de2_ref, dk_acc, dv_acc):
        ok = _nbr_col_ok()
        dk_acc[...] = jnp.zeros_like(dk_acc)
        dv_acc[...] = jnp.zeros_like(dv_acc)
        de2_ref[...] = jnp.zeros_like(de2_ref)

        def step(i, carry):
            blocks = [_nbr_scores(qkv_ref, e2_ref, NBR_SIDE * i + u, rows, ok) for u in range(NBR_SIDE)]
            dys = [dy_ref[pl.ds(b[0], GRID_W), :] for b in blocks]
            dps = [lax.dot_general(dyv, b[5], (NT, ((), ())), preferred_element_type=F32) for dyv, b in zip(dys, blocks)]
            grads = []
            for (q0, k0, first, q, k, v, s), dyv, dp in zip(blocks, dys, dps):
                qrows = pl.ds(q0, GRID_W)
                delta = jnp.sum(dyv.astype(F32) * y_ref[qrows, :].astype(F32), axis=-1, keepdims=True)
                p = jnp.where(ok, jnp.exp(s - jnp.tile(lse_ref[qrows, :], (1, nk // HEAD_DIM))), 0.0)
                ds = p * (dp - delta)
                for pair in range(NA_ROWS // 2):
                    de2_ref[first + 2 * pair] += ds[:, pair * 128:(pair + 1) * 128]
                grads.append((ds.astype(BF), p.astype(BF)))
            for (q0, k0, first, q, k, v, s), dyv, (dsb, pb) in zip(blocks, dys, grads):
                out_ref[0, pl.ds(q0, GRID_W), :] = (jnp.dot(dsb, k, preferred_element_type=F32) * SCALE).astype(BF)
                keys = pl.ds(k0, nk)
                dk_acc[keys, :] += lax.dot_general(dsb, q, (TN, ((), ())), preferred_element_type=F32) * SCALE
                dv_acc[keys, :] += lax.dot_general(pb, dyv, (TN, ((), ())), preferred_element_type=F32)
            return carry

        lax.fori_loop(0, rows // NBR_SIDE, step, 0)
        out_ref[1] = dk_acc[...].astype(BF)
        out_ref[2] = dv_acc[...].astype(BF)

    heads = pl.BlockSpec((3, S, HEAD_DIM), lambda h: (0, 0, N_HEADS_A + h))
    row = pl.BlockSpec((S, HEAD_DIM), lambda h: (0, h))
    table = pl.BlockSpec((None, RPB_ROWS - 1, GRID_W, 128), lambda h: (h, 0, 0, 0))
    return ORDER.call(
        body, [qkv3, e2, dy, y, lse, dqkv3],
        [heads, table, row, row, row, pl.BlockSpec(memory_space=pl.ANY)], name="attn_b_bwd", grid=(4,),
        out_specs=[heads, table],
        out_shape=[_sds((3, S, QKV_W), BF), _sds((4, RPB_ROWS - 1, GRID_W, 128), F32)],
        scratch_shapes=[pltpu.VMEM((S, HEAD_DIM), F32), pltpu.VMEM((S, HEAD_DIM), F32)],
        input_output_aliases={5: 0},
        compiler_params=_cparams(("parallel",)), chain_output=1,
    )


def _rpb_to_table(rpb):
    pad = jnp.pad(rpb, ((0, 0), (0, 0), (0, 1)))
    pairs = jnp.concatenate([pad[:, :-1], pad[:, 1:]], axis=-1).reshape(4 * (RPB_ROWS - 1), 64)
    onehot = jnp.asarray(_toeplitz_onehot())
    n = onehot.shape[1]
    tn = 2048
    full = lambda i, j, k: (0, 0)
    (e2,) = _matmul("rpb_table", pairs, onehot, pl.BlockSpec(pairs.shape, full),
                    pl.BlockSpec((64, tn), lambda i, j, k: (0, j)), NN, (1, n // tn, 1), (pairs.shape[0], tn), [],
                    [(_sds((pairs.shape[0], n), F32), pl.BlockSpec((pairs.shape[0], tn), lambda i, j, k: (0, j)))],
                    _store(F32), precision=lax.Precision.HIGHEST)
    return e2.reshape(4, RPB_ROWS - 1, GRID_W, 128)


def _table_grad_to_rpb(de2):
    onehot = jnp.asarray(_toeplitz_onehot())
    n = onehot.shape[1]
    flat = de2.reshape(4 * (RPB_ROWS - 1), n)
    tk = 2048
    (dpairs,) = _matmul("rpb_table_grad", flat, onehot, pl.BlockSpec((flat.shape[0], tk), lambda i, j, k: (0, k)),
                        pl.BlockSpec((64, tk), lambda i, j, k: (0, k)), NT, (1, 1, n // tk), (flat.shape[0], 64), [],
                        [(_sds((flat.shape[0], 64), F32), pl.BlockSpec((flat.shape[0], 64), lambda i, j, k: (0, 0)))],
                        _store(F32), precision=lax.Precision.HIGHEST)
    dpairs = dpairs.reshape(4, RPB_ROWS - 1, 64)
    zero = jnp.zeros((4, 1, RPB_COLS), F32)
    return (jnp.concatenate([dpairs[:, :, :RPB_COLS], zero], axis=1)
            + jnp.concatenate([zero, dpairs[:, :, 32:32 + RPB_COLS]], axis=1))


HBM = pl.BlockSpec(memory_space=pl.ANY)


def _place():
    x, y, c = lax.axis_index("x"), lax.axis_index("y"), lax.axis_index("c")
    chips = [(1 - x, y), (x, 1 - y), (1 - x, 1 - y)]
    return x, y, c, chips


def _remote(src, dst, send_sem, recv_sem, to):
    return pltpu.make_async_remote_copy(src_ref=src, dst_ref=dst, send_sem=send_sem, recv_sem=recv_sem,
                                        device_id=to, device_id_type=MESH)


def _place_shard(name, w, me, plain=False):
    R, C = w.shape
    tr = _tile(R, 256)

    def body(me_ref, w_ref, *o_refs):
        for o_ref in o_refs:
            o_ref[...] = w_ref[...].astype(BF)

    row = pl.BlockSpec((tr, C), lambda i, mr: (i, 0))
    placed = pl.BlockSpec((None, tr, C), lambda i, mr: (mr[0], i, 0))
    return ORDER.call(
        body, [w], [row], prefetch=(me,), name=name, grid=(R // tr,),
        out_specs=[placed, row] if plain else [placed],
        out_shape=[_sds((N_CHIPS, R, C), BF)] + ([_sds((R, C), BF)] if plain else []),
        compiler_params=_cparams(("parallel",)),
    )


SEM = pl.BlockSpec(memory_space=pltpu.SEMAPHORE)
IN_HBM = pl.BlockSpec(memory_space=pltpu.HBM)
DATAFLOW = pltpu.SideEffectType.DATAFLOW_SIDE_EFFECTING


def _in_hbm(a):
    return pltpu.with_memory_space_constraint(a, pltpu.HBM)


def _copy_start(name, bufs, copies, n_copies, earlier=None):
    n = len(bufs)
    after = None if any(b is ORDER.last for b in bufs) else ORDER.last
    n_extra = (2 if earlier is not None else 0) + (1 if after is not None else 0)

    def body(*refs):
        ins = refs[:n]
        if earlier is not None:
            for k, (src, dst, to) in enumerate(earlier[0](ins)):
                cp = _remote(src, dst, refs[n].at[k], refs[n + 1].at[k], to)
                cp.wait_send()
                cp.wait_recv()
        send_sems, recv_sems = refs[n + n_extra], refs[n + n_extra + 1]
        for k, (src, dst, to) in enumerate(copies(ins)):
            _remote(src, dst, send_sems.at[k], recv_sems.at[k], to).start()
        refs[-1][...] = jnp.zeros((8, 128), F32)

    operands = [_in_hbm(b) for b in bufs]
    in_specs = [IN_HBM] * n
    if earlier is not None:
        operands += [earlier[1], earlier[2]]
        in_specs += [SEM, SEM]
    if after is not None:
        operands.append(after)
        in_specs.append(HBM)
    outs = pl.pallas_call(
        body, name=name,
        out_shape=(pltpu.SemaphoreType.DMA((n_copies,)), pltpu.SemaphoreType.DMA((n_copies,)),
                   *[pltpu.HBM(b.shape, b.dtype) for b in bufs], _sds((8, 128), F32)),
        in_specs=in_specs,
        out_specs=(SEM, SEM, *[IN_HBM] * n, pl.BlockSpec(memory_space=pltpu.VMEM)),
        input_output_aliases={i: 2 + i for i in range(n)},
        compiler_params=pltpu.CompilerParams(has_side_effects=DATAFLOW),
    )(*operands)
    ORDER.last = outs[-1]
    return outs[0], outs[1], list(outs[2:2 + n])


def _copy_wait(name, bufs, copies, send_sems, recv_sems):
    n = len(bufs)
    after = ORDER.last

    def body(*refs):
        ins = refs[:n]
        for k, (src, dst, to) in enumerate(copies(ins)):
            cp = _remote(src, dst, refs[n].at[k], refs[n + 1].at[k], to)
            cp.wait_send()
            cp.wait_recv()

    outs = list(pl.pallas_call(
        body, name=name,
        out_shape=tuple(pltpu.HBM(b.shape, b.dtype) for b in bufs),
        in_specs=[IN_HBM] * n + [SEM, SEM, HBM], out_specs=tuple([IN_HBM] * n),
        input_output_aliases={i: i for i in range(n)},
        compiler_params=pltpu.CompilerParams(has_side_effects=DATAFLOW),
    )(*bufs, send_sems, recv_sems, after))
    ORDER.last = outs[0]
    return outs


def _gather_hop1(bufs):
    x, y, c, chips = _place()
    out = []
    for b in bufs:
        half = b.shape[1] // 2
        mine = b.at[2 * x + y, pl.ds(c * half, half), :]
        out += [(mine, mine, (*chip, c)) for chip in chips]
    return out


def _gather_hop2(bufs):
    x, y, c, chips = _place()
    out = []
    for b in bufs:
        half = b.shape[1] // 2
        for chip in chips:
            landed = b.at[2 * chip[0] + chip[1], pl.ds(c * half, half), :]
            out.append((landed, landed, (x, y, 1 - c)))
    return out


def _swap_copies(bufs):
    x, y, c, _ = _place()
    n = len(bufs) // 2
    out = []
    for p, land in zip(bufs[:n], bufs[n:]):
        half = p.shape[1] // 2
        out.append((p.at[:, pl.ds((1 - c) * half, half), :], land, (x, y, 1 - c)))
    return out


def _scatter_copies(bufs):
    _, _, c, chips = _place()
    n = len(bufs) // 2
    out = []
    for s_, land in zip(bufs[:n], bufs[n:]):
        out += [(s_.at[2 * chip[0] + chip[1]], land.at[j], (*chip, c)) for j, chip in enumerate(chips)]
    return out


def _join_copies(bufs):
    x, y, c, _ = _place()
    out = []
    for b in bufs:
        half = b.shape[0] // 2
        mine = b.at[pl.ds(c * half, half), :]
        out.append((mine, mine, (x, y, 1 - c)))
    return out


def _gather_small(vec):
    m_per, n = vec.shape

    def body(x_ref, out_ref, send_sems, recv_sems, local_sem):
        x, y, c, chips = _place()
        me, sibling = (x, y, c), (x, y, 1 - c)

        def rows(px, py, pc):
            return out_ref.at[pl.ds((4 * px + 2 * py + pc) * m_per, m_per), :]

        def copy(k, block, to, src=None):
            return _remote(rows(*block) if src is None else src, rows(*block), send_sems.at[k], recv_sems.at[k], to)

        mine = pltpu.make_async_copy(x_ref, rows(*me), local_sem)
        mine.start()
        first = [copy(0, me, sibling, src=x_ref)]
        first += [copy(1 + j, me, (*chip, c), src=x_ref) for j, chip in enumerate(chips)]
        for cp in first:
            cp.start()
        passed = [copy(4 + j, (*chip, c), sibling) for j, chip in enumerate(chips)]
        for j, chip in enumerate(chips):
            copy(1 + j, (*chip, c), me).wait_recv()
            passed[j].start()
        copy(0, sibling, me).wait_recv()
        for j, chip in enumerate(chips):
            copy(4 + j, (*chip, 1 - c), me).wait_recv()
        for cp in first + passed:
            cp.wait_send()
        mine.wait()

    return ORDER.call(
        body, [vec], [pl.BlockSpec(memory_space=pltpu.VMEM)], name="gather_small_grads",
        out_shape=_sds((8 * m_per, n), vec.dtype), out_specs=pl.BlockSpec(memory_space=pltpu.VMEM),
        scratch_shapes=[pltpu.SemaphoreType.DMA((7,)), pltpu.SemaphoreType.DMA((7,)), pltpu.SemaphoreType.DMA],
    )


def _add_sibling(name, partial, received, c):
    _, R, C = partial.shape
    half = R // 2
    tr = _tile(half, 256)
    nb = half // tr

    def body(c_ref, p_ref, r_ref, o_ref):
        o_ref[...] = (p_ref[...].astype(F32) + r_ref[...].astype(F32)).astype(BF)

    return ORDER.call(
        body, [partial, received],
        [pl.BlockSpec((None, tr, C), lambda j, i, cr: (j, cr[0] * nb + i, 0)),
         pl.BlockSpec((None, tr, C), lambda j, i, cr: (j, i, 0))],
        prefetch=(c,), name=name, grid=(N_CHIPS, nb),
        out_specs=pl.BlockSpec((None, tr, C), lambda j, i, cr: (j, i, 0)),
        out_shape=_sds((N_CHIPS, half, C), BF), compiler_params=_cparams(("parallel", "parallel")),
    )


def _add_chips(name, sums, received, me_c):
    _, half, C = sums.shape
    tr = _tile(half, 256)
    nb = half // tr

    def body(mc_ref, s_ref, r_ref, o_ref):
        acc = s_ref[...].astype(F32)
        for j in range(3):
            acc = acc + r_ref[j].astype(F32)
        o_ref[...] = acc

    return ORDER.call(
        body, [sums, received],
        [pl.BlockSpec((None, tr, C), lambda i, mc: (mc[0], i, 0)),
         pl.BlockSpec((3, tr, C), lambda i, mc: (0, i, 0))],
        prefetch=(me_c,), name=name, grid=(nb,),
        out_specs=pl.BlockSpec((tr, C), lambda i, mc: (mc[1] * nb + i, 0)),
        out_shape=_sds((2 * half, C), F32), compiler_params=_cparams(("parallel",)),
    )


def _adamw_math(w, g, m, v):
    m = ADAM_B1 * m + (1.0 - ADAM_B1) * g
    v = ADAM_B2 * v + (1.0 - ADAM_B2) * (g * g)
    m_hat = m / (1.0 - ADAM_B1 ** ADAM_STEP)
    v_hat = v / (1.0 - ADAM_B2 ** ADAM_STEP)
    delta = -ADAM_LR * (m_hat / (jnp.sqrt(v_hat) + ADAM_EPS) + ADAM_WD * w)
    return delta, m, v


def _adamw(name, w, g, m, v):
    R, C = w.shape
    tr = _tile(R, 256)

    def body(w_ref, g_ref, m_ref, v_ref, go_ref, d_ref, mo_ref, vo_ref):
        gv = g_ref[...]
        go_ref[...] = gv
        d_ref[...], mo_ref[...], vo_ref[...] = _adamw_math(w_ref[...], gv, m_ref[...], v_ref[...])

    row = pl.BlockSpec((tr, C), lambda i: (i, 0))
    return ORDER.call(
        body, [w, g, m, v], [row] * 4, name=name, grid=(R // tr,), out_specs=[row] * 4,
        out_shape=[_sds((R, C), F32)] * 4, compiler_params=_cparams(("parallel",)), chain_output=1,
    )


def _adamw_small(gathered, w, m, v):
    rows, n = w.shape

    def body(ga_ref, w_ref, m_ref, v_ref, go_ref, d_ref, mo_ref, vo_ref):
        g = ga_ref[pl.ds(0, rows), :]
        for dev in range(1, 8):
            g = g + ga_ref[pl.ds(dev * rows, rows), :]
        go_ref[...] = g
        d_ref[...], mo_ref[...], vo_ref[...] = _adamw_math(w_ref[...], g, m_ref[...], v_ref[...])

    whole = pl.BlockSpec(memory_space=pltpu.VMEM)
    return ORDER.call(
        body, [gathered, w, m, v], [whole] * 4, name="adamw_small", out_specs=[whole] * 4,
        out_shape=[_sds((rows, n), F32)] * 4, compiler_params=_cparams(), chain_output=1,
    )


def _proj_merge(y_a, y_b, gpa, gpb, g3):
    S, K = y_a.shape
    _, _, Nq = gpa.shape
    D = N_CHIPS * Nq
    tm, tn = _tile(S, 1024), _tile(Nq, 512)
    q = Nq // tn

    def body(ya_ref, yb_ref, wa_ref, wb_ref, g_ref, merged_ref, c_ref):
        pa = jnp.dot(ya_ref[...], wa_ref[...], preferred_element_type=F32)
        pb = jnp.dot(yb_ref[...], wb_ref[...], preferred_element_type=F32)
        g = g_ref[...].astype(F32)
        merged_ref[...] = (g[0] * pa + g[1] * pb).astype(BF)
        c_ref[0] = (pa * g[0] * (1.0 - g[0])).astype(BF)
        c_ref[1] = (pb * g[1] * (1.0 - g[1])).astype(BF)

    rows = pl.BlockSpec((tm, K), lambda i, j: (i, 0))
    weight = pl.BlockSpec((None, K, tn), lambda i, j: (j // q, 0, j % q))
    pair = pl.BlockSpec((2, tm, tn), lambda i, j: (0, i, j))
    return ORDER.call(
        body, [y_a, y_b, gpa, gpb, g3], [rows, rows, weight, weight, pair], name="proj_merge",
        grid=(S // tm, N_CHIPS * q), out_specs=[pl.BlockSpec((tm, tn), lambda i, j: (i, j)), pair],
        out_shape=[_sds((S, D), BF), _sds((2, S, D), BF)], compiler_params=_cparams(("parallel", "parallel")))


class _Exchange:
    GATHER = (("qkv",), ("gate",), ("proj_a", "proj_b", "out"), ("up",), ("down",))
    REDUCE = {"mlp": ("down", "up"), "mix": ("out", "proj_a", "proj_b"), "in": ("qkv", "gate")}

    OWN_FIRST = ("qkv", "gate")

    def __init__(self, shards, me, c):
        self.me, self.c = me, c
        self.hop1, self.hop2, self.stage, self.grads, self.own = {}, {}, {}, {}, {}
        for g, names in enumerate(self.GATHER):
            bufs = []
            for n in names:
                placed = _place_shard(f"place_{n}", shards[n], me, plain=n in self.OWN_FIRST)
                bufs.append(placed[0])
                if n in self.OWN_FIRST:
                    self.own[n] = placed[1]
            self.hop1[g] = _copy_start(f"gather{g}_start", bufs, _gather_hop1, 3 * len(names))

    def forward(self, g):
        send, recv, thru = self.hop1.pop(g)
        self.hop2[g] = _copy_start(f"gather{g}_forward", thru, _gather_hop2, len(thru) * 3,
                                   earlier=(_gather_hop1, send, recv))

    def weights(self, g):
        send, recv, thru = self.hop2.pop(g)
        return _copy_wait(f"gather{g}_wait", thru, _gather_hop2, send, recv)

    def reduce(self, key, partials=None):
        names = self.REDUCE[key]
        n = len(names)
        if partials is not None:
            lands = [lax.empty((p.shape[0], p.shape[1] // 2, p.shape[2]), p.dtype) for p in partials]
            self.stage[key] = ("swap",) + _copy_start(f"reduce_{key}_swap", list(partials) + lands, _swap_copies, n)
            return
        kind, send, recv, thru = self.stage.pop(key)
        if kind == "swap":
            thru = _copy_wait(f"reduce_{key}_swap_wait", thru, _swap_copies, send, recv)
            sums = [_add_sibling(f"reduce_{nm}_add_sibling", p, r, self.c)
                    for nm, p, r in zip(names, thru[:n], thru[n:])]
            lands = [lax.empty((3,) + s_.shape[1:], s_.dtype) for s_ in sums]
            self.stage[key] = ("scatter",) + _copy_start(f"reduce_{key}_scatter", sums + lands, _scatter_copies, 3 * n)
        elif kind == "scatter":
            thru = _copy_wait(f"reduce_{key}_scatter_wait", thru, _scatter_copies, send, recv)
            me_c = jnp.concatenate([self.me, self.c])
            halves = [_add_chips(f"reduce_{nm}_add_chips", s_, r, me_c)
                      for nm, s_, r in zip(names, thru[:n], thru[n:])]
            self.stage[key] = ("join",) + _copy_start(f"reduce_{key}_join", halves, _join_copies, n)
        else:
            thru = _copy_wait(f"reduce_{key}_join_wait", thru, _join_copies, send, recv)
            self.grads.update(zip(names, thru))


def _forward_backward(x, target, norm_mix, b_gate, rpb, norm_mlp, norm_final, ex):
    S, D = x.shape

    h1 = _rms_fwd("rms_mix", x, norm_mix)
    nq = QKV_W // 512
    qkv_out = (((3, S, QKV_W), BF), lambda i, T: (T // nq, i, T % nq))
    tg = _tile(ex.own["gate"].shape[1], 1024)
    ng = D // tg
    gate_out = (((2, S, D), BF), lambda i, T: (T // ng, i, T % ng))

    def gate_epilogue(acc, ex_, outs):
        outs[0][...] = jax.nn.sigmoid(acc + ex_[0][...]).astype(BF)

    qkv3 = _mm_nn_shards("qkv_own", h1, ex.own["qkv"], ex.me, True, *qkv_out, _store(BF))
    g3 = _mm_nn_shards("gate_own", h1, ex.own["gate"], ex.me, True, *gate_out, gate_epilogue, extras=[b_gate], tn=tg)
    ex.forward(0)
    e2 = _rpb_to_table(rpb)
    (gq,) = ex.weights(0)
    qkv3 = _mm_nn_shards("qkv", h1, gq, ex.me, False, *qkv_out, _store(BF), into=qkv3)

    ex.forward(1)
    outs_a = [_attn_a_fwd(qkv3, 0, DILATIONS[0])]
    (gg,) = ex.weights(1)
    g3 = _mm_nn_shards("gate", h1, gg, ex.me, False, *gate_out, gate_epilogue, extras=[b_gate], into=g3, tn=tg)

    ex.forward(2)
    qkv_views = _qkv_views("qkv_views", qkv3)
    outs_a += [_attn_a_fwd(qkv_views[d], grp, d) for grp, d in enumerate(DILATIONS) if grp > 0]
    y_a, lj = _attn_a_combine([o for o, _ in outs_a], [l for _, l in outs_a])
    y_b, lse_b = _attn_b_fwd(qkv3, e2)
    gpa, gpb, gout = ex.weights(2)
    wout = gout.reshape(D, D)
    merged, c3 = _proj_merge(y_a, y_b, gpa, gpb, g3)

    def residual_epilogue(acc, ex_, outs):
        outs[0][...] = acc + ex_[0][...]

    def nn_plain(name, a, w, res):
        M, K = a.shape
        N = w.shape[1]
        bm, bn, bk = _tile(M, 1024), _tile(N, 1024), _tile(K, 2048)
        t = pl.BlockSpec((bm, bn), lambda i, j, k: (i, j))
        return _matmul(name, a, w, pl.BlockSpec((bm, bk), lambda i, j, k: (i, k)),
                       pl.BlockSpec((bk, bn), lambda i, j, k: (k, j)), NN, (M // bm, N // bn, K // bk), (bm, bn),
                       [(res, t)], [(_sds((M, N), F32), t)], residual_epilogue)[0]

    ex.forward(3)
    x1 = nn_plain("out_proj", merged, wout, x)
    h2 = _rms_fwd("rms_mlp", x1, norm_mlp)
    (gup,) = ex.weights(3)
    F = gup.shape[2] * N_CHIPS
    ex.forward(4)

    def up_epilogue(acc, ex_, outs):
        ru = jnp.maximum(acc, 0.0)
        outs[0][...] = (ru * ru).astype(BF)
        outs[1][...] = ru.astype(BF)

    tu = _tile(gup.shape[2], 1024)
    ut = pl.BlockSpec((_tile(S, 1024), tu), lambda i, j, k: (i, j))
    (act, ru), _ = _mm_nn_cols("mlp_up", h2, gup, BF, epilogue=up_epilogue, tn=tu,
                               outs=[(_sds((S, F), BF), ut), (_sds((S, F), BF), ut)])
    (gdown,) = ex.weights(4)
    wdown = gdown.reshape(F, D)
    x2 = nn_plain("mlp_down", act, wdown, x1)

    loss, dx2, dx2b, d_norm_final = _loss_head(x2, target, norm_final.reshape(1, D))

    def nt_rows(name, a, w, epilogue, extras, outs, bn=1024):
        M, N = a.shape
        K = w.shape[0]
        bm, bn, bk = _tile(M, 1024), _tile(K, bn), _tile(N, 2048)
        return _matmul(name, a, w, pl.BlockSpec((bm, bk), lambda i, j, k: (i, k)),
                       pl.BlockSpec((bn, bk), lambda i, j, k: (j, k)), NT, (M // bm, K // bn, N // bk), (bm, bn),
                       extras(bm, bn), outs(bm, bn), epilogue)

    def nt_cols(name, a_spec_fn, a, g, M, epilogue, extras, outs, bk, bn=1024):
        _, K, Nq = g.shape
        bm, bn, bk = _tile(M, 1024), _tile(K, bn), _tile(Nq, bk)
        q = Nq // bk
        return _matmul(name, a, g, a_spec_fn(bm, bk), pl.BlockSpec((None, bn, bk), lambda i, j, k: (k // q, j, k % q)),
                       NT, (M // bm, K // bn, N_CHIPS * q), (bm, bn), extras(bm, bn), outs(bm, bn), epilogue)

    def tn_grad(name, a, a_spec_fn, b, b_spec_fn, Kin, N, out_shape, out_spec_fn, bn=1024):
        bm, bn, bk = _tile(Kin, 1024), _tile(N, bn), _tile(S, 2048)
        return _matmul(name, a, b, a_spec_fn(bk, bm), b_spec_fn(bk, bn), TN, (Kin // bm, N // bn, S // bk), (bm, bn),
                       [], [(_sds(out_shape, BF), out_spec_fn(bm, bn))], _store(BF))[0]

    plain_a = lambda bk, bm: pl.BlockSpec((bk, bm), lambda i, j, k: (k, i))
    plain_b = lambda bk, bn: pl.BlockSpec((bk, bn), lambda i, j, k: (k, j))
    plain_o = lambda bm, bn: pl.BlockSpec((bm, bn), lambda i, j, k: (i, j))
    a_rows = lambda bm, bk: pl.BlockSpec((bm, bk), lambda i, j, k: (i, k))

    def cols_o(Nq):
        def spec(bm, bn):
            q = Nq // bn
            return pl.BlockSpec((None, bm, bn), lambda i, j, k: (j // q, i, j % q))
        return spec

    def du_epilogue(acc, ex_, outs):
        outs[0][...] = (acc * (2.0 * ex_[0][...].astype(F32))).astype(BF)

    dw_down = tn_grad("mlp_down_dw", act, plain_a, dx2b, plain_b, F, D, (F, D), plain_o)
    (du,) = nt_rows("mlp_down_dx", dx2b, wdown, du_epilogue,
                    lambda bm, bn: [(ru, plain_o(bm, bn))], lambda bm, bn: [(_sds((S, F), BF), plain_o(bm, bn))])

    fq = gup.shape[2]
    dw_up = tn_grad("mlp_up_dw", h2, plain_a, du, plain_b, D, F, (N_CHIPS, D, fq), cols_o(fq), bn=min(fq, 1024))
    ex.reduce("mlp", partials=[dw_down.reshape(N_CHIPS, F // N_CHIPS, D), dw_up])
    (dh2,) = nt_cols("mlp_up_dx", a_rows, du, gup, S, _store(F32), lambda bm, bn: [],
                     lambda bm, bn: [(_sds((S, D), F32), plain_o(bm, bn))], 2048)
    ex.reduce("mlp")
    dx1, dx1b, d_norm_mlp = _rms_bwd("rms_mlp_bwd", dh2, x1, norm_mlp, dx2)

    def merge_bwd_epilogue(acc, ex_, outs):
        g, c = ex_[0][...].astype(F32), ex_[1][...].astype(F32)
        outs[0][...] = (acc * g[0]).astype(BF)
        outs[1][...] = (acc * g[1]).astype(BF)
        dga = acc * c[0]
        dgb = acc * c[1]
        outs[2][0] = dga.astype(BF)
        outs[2][1] = dgb.astype(BF)
        outs[3][...] = jnp.concatenate([jnp.sum(dga, axis=0, keepdims=True), jnp.sum(dgb, axis=0, keepdims=True)], 0)

    def pair(bm, bn):
        return pl.BlockSpec((2, bm, bn), lambda i, j, k: (0, i, j))

    n_row_blocks = S // _tile(S, 1024)
    dpa, dpb, dg3, db_gate = nt_rows(
        "out_proj_dx", dx1b, wout, merge_bwd_epilogue,
        lambda bm, bn: [(g3, pair(bm, bn)), (c3, pair(bm, bn))],
        lambda bm, bn: [(_sds((S, D), BF), plain_o(bm, bn)), (_sds((S, D), BF), plain_o(bm, bn)),
                        (_sds((2, S, D), BF), pair(bm, bn)),
                        (_sds((n_row_blocks, 2, D), F32), pl.BlockSpec((None, 2, bn), lambda i, j, k: (i, 0, j)))],
        bn=512)
    dw_out = tn_grad("out_proj_dw", merged, plain_a, dx1b, plain_b, D, D, (D, D), plain_o)

    pq = gpa.shape[2]
    proj_dx = lambda name, dproj, g: nt_cols(name, a_rows, dproj, g, S, _store(BF), lambda bm, bn: [],
                                             lambda bm, bn: [(_sds((S, 512), BF), plain_o(bm, bn))], 512)[0]
    dw_pa = tn_grad("proj_a_dw", y_a, plain_a, dpa, plain_b, 512, D, (N_CHIPS, 512, pq), cols_o(pq), bn=min(pq, 512))
    dw_pb = tn_grad("proj_b_dw", y_b, plain_a, dpb, plain_b, 512, D, (N_CHIPS, 512, pq), cols_o(pq), bn=min(pq, 512))
    ex.reduce("mix", partials=[dw_out.reshape(N_CHIPS, D // N_CHIPS, D), dw_pa, dw_pb])
    dy_a = proj_dx("proj_a_dx", dpa, gpa)
    dy_b = proj_dx("proj_b_dx", dpb, gpb)

    dqkv3 = lax.empty((3, S, QKV_W), BF)
    dqkv3 = _attn_a_bwd(qkv3, dy_a, y_a, lj, dqkv3, 0, DILATIONS[0])
    ex.reduce("mix")
    dy_views, y_views, lj_views = _dilated_rows("attn_a_bwd_rows", [dy_a, y_a, lj])
    dqkv_views = {d: _attn_a_bwd(qkv_views[d], dy_views[d], y_views[d], lj_views[d], None, grp, d)
                  for grp, d in enumerate(DILATIONS) if grp > 0}
    dqkv3 = _qkv_views("dqkv_from_views", dqkv3, dqkv_views)
    dqkv3, de2 = _attn_b_bwd(qkv3, e2, dy_b, y_b, lse_b, dqkv3)
    d_rpb = _table_grad_to_rpb(de2)

    def stacked_a(width):
        def spec(bm, bk):
            q = width // bk
            return pl.BlockSpec((None, bm, bk), lambda i, j, k: (k // q, i, k % q))
        return spec

    def stacked_b(width):
        def spec(bk, bn):
            q = width // bn
            return pl.BlockSpec((None, bk, bn), lambda i, j, k: (j // q, k, j % q))
        return spec

    dw_qkv = tn_grad("qkv_dw", h1, plain_a, dqkv3, stacked_b(QKV_W), D, 3 * QKV_W, (N_CHIPS,) + gq.shape[1:],
                     cols_o(gq.shape[2]), bn=512)
    dw_gate = tn_grad("gate_dw", h1, plain_a, dg3, stacked_b(D), D, 2 * D, (N_CHIPS,) + gg.shape[1:],
                      cols_o(gg.shape[2]), bn=gg.shape[2])
    ex.reduce("in", partials=[dw_qkv, dw_gate])
    ex.reduce("mlp")
    (dh1_q,) = nt_cols("qkv_dx", stacked_a(QKV_W), dqkv3, gq, S, _store(F32), lambda bm, bn: [],
                       lambda bm, bn: [(_sds((S, D), F32), plain_o(bm, bn))], 512, bn=2048)
    ex.reduce("in")
    ex.reduce("mix")

    def add_epilogue(acc, ex_, outs):
        outs[0][...] = acc + ex_[0][...]

    (dh1,) = nt_cols("gate_dx", stacked_a(D), dg3, gg, S, add_epilogue, lambda bm, bn: [(dh1_q, plain_o(bm, bn))],
                     lambda bm, bn: [(_sds((S, D), F32), plain_o(bm, bn))], gg.shape[2])
    grad_x, _, d_norm_mix = _rms_bwd("rms_mix_bwd", dh1, x, norm_mix, dx1)
    ex.reduce("mlp")
    ex.reduce("mix")

    small = [d_norm_mix, jnp.sum(db_gate, axis=0).reshape(1, 2 * D), d_rpb, d_norm_mlp, d_norm_final]
    return loss, grad_x, small


def _pack_small(parts, width):
    flat = jnp.concatenate([p.reshape(-1) for p in parts])
    return jnp.pad(flat, (0, 8 * width - flat.shape[0])).reshape(8, width)


def kernel(x, norm_mix, w_qkv, w_gate, b_gate, rpb, w_proj_a, w_proj_b, w_out, norm_mlp, w_up, w_down, norm_final, loss_target, m_norm_mix, m_w_qkv, m_w_gate, m_b_gate, m_rpb, m_w_proj_a, m_w_proj_b, m_w_out, m_norm_mlp, m_w_up, m_w_down, m_norm_final, v_norm_mix, v_w_qkv, v_w_gate, v_b_gate, v_rpb, v_w_proj_a, v_w_proj_b, v_w_out, v_norm_mlp, v_w_up, v_w_down, v_norm_final):
    names = ["qkv", "gate", "proj_a", "proj_b", "out", "up", "down"]
    big = dict(zip(names, [w_qkv[0], w_gate[0], w_proj_a[0], w_proj_b[0], w_out[0], w_up[0], w_down[0]]))
    big_m = dict(zip(names, [m_w_qkv[0], m_w_gate[0], m_w_proj_a[0], m_w_proj_b[0], m_w_out[0], m_w_up[0], m_w_down[0]]))
    big_v = dict(zip(names, [v_w_qkv[0], v_w_gate[0], v_w_proj_a[0], v_w_proj_b[0], v_w_out[0], v_w_up[0], v_w_down[0]]))

    c = lax.axis_index("c").astype(jnp.int32).reshape(1)
    me = (2 * lax.axis_index("x") + lax.axis_index("y")).astype(jnp.int32).reshape(1)
    ORDER.last = None
    ex = _Exchange(big, me, c)
    loss, grad_x, small = _forward_backward(x[0], loss_target[0], norm_mix, b_gate, rpb[0], norm_mlp, norm_final, ex)

    def adamw(group):
        return {n: _adamw(f"adamw_{n}", big[n], ex.grads[n], big_m[n], big_v[n]) for n in _Exchange.REDUCE[group]}

    big_out = {**adamw("mlp"), **adamw("mix")}
    ex.reduce("in")

    small_w = [norm_mix, b_gate, rpb, norm_mlp, norm_final]
    count = sum(int(np.prod(p.shape)) for p in small_w)
    width = -(-count // (8 * 128)) * 128
    packed = _adamw_small(_gather_small(_pack_small(small, width)), _pack_small(small_w, width),
                          _pack_small([m_norm_mix, m_b_gate, m_rpb, m_norm_mlp, m_norm_final], width),
                          _pack_small([v_norm_mix, v_b_gate, v_rpb, v_norm_mlp, v_norm_final], width))
    ex.reduce("in")
    big_out.update(adamw("in"))

    def unpack(flat2d):
        flat, out, at = flat2d.reshape(-1), [], 0
        for p in small_w:
            size = int(np.prod(p.shape))
            out.append(flat[at:at + size].reshape(p.shape))
            at += size
        return out

    small_out = [unpack(a) for a in packed]

    def ordered(kind):
        sm = small_out[kind]
        bg = {n: o[kind][None] for n, o in big_out.items()}
        return [sm[0], bg["qkv"], bg["gate"], sm[1], sm[2], bg["proj_a"], bg["proj_b"], bg["out"], sm[3],
                bg["up"], bg["down"], sm[4]]

    total = lax.psum(loss[0, 0], ("x", "y", "c"))
    return (total, grad_x[None], *ordered(0), *ordered(1), *ordered(2), *ordered(3))
```

```python
import functools
import math

import numpy as np
import jax
import jax.numpy as jnp
from jax import lax
from jax.experimental import pallas as pl
from jax.experimental.pallas import tpu as pltpu

BF = jnp.bfloat16
F32 = jnp.float32
MESH = pl.DeviceIdType.MESH

HEAD_DIM = 128
N_HEADS = 16
N_HEADS_A = 12
QKV_W = N_HEADS * HEAD_DIM
DILATIONS = (1, 4, 16)
HALF_WINDOW = 64
GRID_W = 64
NA_ROWS = 8
NA_COLS = 16
RPB_ROWS = 2 * NA_ROWS - 1
RPB_COLS = 2 * NA_COLS - 1
EPS = 1e-6
NEG = -1e30
SCALE = HEAD_DIM ** -0.5

ADAM_LR = 0.001
ADAM_B1 = 0.9
ADAM_B2 = 0.999
ADAM_EPS = 1e-08
ADAM_WD = 0.01
ADAM_STEP = 10

N_CHIPS = 4
VMEM_LIMIT_BYTES = 48 * 1024 * 1024
QB = 256
NBR_SIDE = 4
ROW_TILE = 512


def _key_rows(L):
    return min(QB + 2 * HALF_WINDOW, L)


def _cparams(sem=None):
    return pltpu.CompilerParams(dimension_semantics=sem, vmem_limit_bytes=VMEM_LIMIT_BYTES)


def _tile(dim, want):
    t = min(dim, want)
    assert dim % t == 0, (dim, want)
    return t


class _ProgramOrder:
    def __init__(self):
        self.last = None

    def call(self, body, operands, in_specs, *, prefetch=(), grid=None, out_specs=None, chain_output=0, **kwargs):
        operands, in_specs = list(operands), list(in_specs)
        lead = len(prefetch) + len(operands)
        if self.last is not None and not any(op is self.last for op in operands):
            operands.append(self.last)
            in_specs.append(pl.BlockSpec(memory_space=pl.ANY))
            inner = body

            def body(*refs):
                return inner(*refs[:lead], *refs[lead + 1:])

        if prefetch:
            kwargs["grid_spec"] = pltpu.PrefetchScalarGridSpec(
                num_scalar_prefetch=len(prefetch), grid=grid, in_specs=in_specs, out_specs=out_specs)
        else:
            kwargs.update(in_specs=in_specs, out_specs=out_specs)
            if grid is not None:
                kwargs["grid"] = grid
        out = pl.pallas_call(body, **kwargs)(*prefetch, *operands)
        self.last = out[chain_output] if isinstance(out, (tuple, list)) else out
        return out


ORDER = _ProgramOrder()


NN = ((1,), (0,))
NT = ((1,), (1,))
TN = ((0,), (0,))


def _matmul(name, a, b, a_spec, b_spec, dims, grid, acc_shape, extras, outs, epilogue, precision=None,
            prefetch=(), into=None):
    n_ex, n_out, nk = len(extras), len(outs), grid[2]
    n_in = 2 + n_ex + (into is not None)

    def body(*refs):
        refs = refs[len(prefetch):]
        a_ref, b_ref = refs[0], refs[1]
        ex_refs = refs[2:2 + n_ex]
        out_refs = refs[n_in:n_in + n_out]

        def dot():
            return lax.dot_general(a_ref[...], b_ref[...], (dims, ((), ())),
                                   preferred_element_type=F32, precision=precision)

        if nk == 1:
            epilogue(dot(), ex_refs, out_refs)
            return
        acc_ref = refs[-1]
        k = pl.program_id(2)

        @pl.when(k == 0)
        def _():
            acc_ref[...] = dot()

        if nk > 2:
            @pl.when((k > 0) & (k < nk - 1))
            def _():
                acc_ref[...] += dot()

        @pl.when(k == nk - 1)
        def _():
            epilogue(acc_ref[...] + dot(), ex_refs, out_refs)

    operands = [a, b] + [e for e, _ in extras]
    in_specs = [a_spec, b_spec] + [s for _, s in extras]
    kwargs = {}
    if into is not None:
        operands.append(into)
        in_specs.append(pl.BlockSpec(memory_space=pl.ANY))
        kwargs["input_output_aliases"] = {len(prefetch) + n_in - 1: 0}
    return ORDER.call(
        body, operands, in_specs, prefetch=prefetch, name=name, grid=grid,
        out_specs=[s for _, s in outs],
        out_shape=[sh for sh, _ in outs],
        scratch_shapes=[pltpu.VMEM(acc_shape, F32)] if nk > 1 else [],
        compiler_params=_cparams(("parallel", "parallel", "arbitrary")), **kwargs,
    )


def _mm_nn_shards(name, a, w, me, own, out, out_block, epilogue, extras=(), into=None, tn=512):
    M, K = a.shape
    Nq = w.shape[-1]
    tm, tn = _tile(M, 1024), _tile(Nq, tn)
    q = Nq // tn

    def tile(j, me_ref):
        shard = me_ref[0] if own else (me_ref[0] + 1 + j // q) % N_CHIPS
        return shard, j % q, shard * q + j % q

    if own:
        b_spec = pl.BlockSpec((K, tn), lambda i, j, k, me_ref: (0, j))
    else:
        b_spec = pl.BlockSpec((None, K, tn), lambda i, j, k, me_ref: (tile(j, me_ref)[0], 0, tile(j, me_ref)[1]))
    shape, dtype = out
    out_spec = pl.BlockSpec((None, tm, tn), lambda i, j, k, me_ref: out_block(i, tile(j, me_ref)[2]))
    ex = [(e, pl.BlockSpec((1, tn), lambda i, j, k, me_ref: (0, tile(j, me_ref)[2]))) for e in extras]
    return _matmul(name, a, w, pl.BlockSpec((tm, K), lambda i, j, k, me_ref: (i, 0)), b_spec, NN,
                   (M // tm, q if own else (N_CHIPS - 1) * q, 1), (tm, tn), ex, [(_sds(shape, dtype), out_spec)],
                   epilogue, prefetch=(me,), into=into)[0]


def _store(dtype):
    def epilogue(acc, ex, outs):
        outs[0][...] = acc.astype(dtype)
    return epilogue


def _sds(shape, dtype):
    return jax.ShapeDtypeStruct(shape, dtype)


def _mm_nn_cols(name, a, g, out_dtype, epilogue=None, extras=(), outs=None, tm=1024, tn=1024, tk=2048):
    M, K = a.shape
    _, _, Nq = g.shape
    tm, tn, tk = _tile(M, tm), _tile(Nq, tn), _tile(K, tk)
    q = Nq // tn
    grid = (M // tm, N_CHIPS * q, K // tk)
    if outs is None:
        outs = [(_sds((M, N_CHIPS * Nq), out_dtype), pl.BlockSpec((tm, tn), lambda i, j, k: (i, j)))]
    return _matmul(name, a, g, pl.BlockSpec((tm, tk), lambda i, j, k: (i, k)),
                   pl.BlockSpec((None, tk, tn), lambda i, j, k: (j // q, k, j % q)), NN, grid, (tm, tn),
                   list(extras), outs, epilogue or _store(out_dtype)), (tm, tn, tk)


def _rms_fwd(name, x, g):
    S, D = x.shape
    tm = _tile(S, ROW_TILE)

    def body(x_ref, g_ref, h_ref):
        xv = x_ref[...]
        r = lax.rsqrt(jnp.mean(xv * xv, axis=-1, keepdims=True) + EPS)
        h_ref[...] = ((xv * r) * g_ref[...]).astype(BF)

    row = pl.BlockSpec((tm, D), lambda i: (i, 0))
    return ORDER.call(
        body, [x, g], [row, pl.BlockSpec((1, D), lambda i: (0, 0))], name=name, grid=(S // tm,),
        out_specs=row, out_shape=_sds((S, D), BF), compiler_params=_cparams(("parallel",)),
    )


def _rms_bwd(name, dh, x, g, dres):
    S, D = x.shape
    tm = _tile(S, ROW_TILE // 2)

    def body(dh_ref, x_ref, g_ref, dres_ref, dx_ref, dxb_ref, dg_ref):
        xv = x_ref[...]
        r = lax.rsqrt(jnp.mean(xv * xv, axis=-1, keepdims=True) + EPS)
        n = xv * r
        dhv = dh_ref[...]
        dyg = dhv * g_ref[...]
        dx = dres_ref[...] + r * (dyg - n * jnp.mean(dyg * n, axis=-1, keepdims=True))
        dx_ref[...] = dx
        dxb_ref[...] = dx.astype(BF)

        @pl.when(pl.program_id(0) == 0)
        def _():
            dg_ref[...] = jnp.zeros_like(dg_ref)

        dg_ref[...] += jnp.sum(dhv * n, axis=0, keepdims=True)

    row = pl.BlockSpec((tm, D), lambda i: (i, 0))
    vec = pl.BlockSpec((1, D), lambda i: (0, 0))
    return ORDER.call(
        body, [dh, x, g, dres], [row, row, vec, row], name=name, grid=(S // tm,),
        out_specs=[row, row, vec],
        out_shape=[_sds((S, D), F32), _sds((S, D), BF), _sds((1, D), F32)],
        compiler_params=_cparams(("arbitrary",)),
    )


def _loss_head(x2, target, g):
    S, D = x2.shape
    tm = _tile(S, ROW_TILE)

    def body(x_ref, t_ref, g_ref, loss_ref, dx_ref, dxb_ref, dg_ref):
        xv = x_ref[...]
        gv = g_ref[...]
        r = lax.rsqrt(jnp.mean(xv * xv, axis=-1, keepdims=True) + EPS)
        n = xv * r
        e = n * gv - t_ref[...]
        dy = e * (1.0 / D)
        dyg = dy * gv
        dx = r * (dyg - n * jnp.mean(dyg * n, axis=-1, keepdims=True))
        dx_ref[...] = dx
        dxb_ref[...] = dx.astype(BF)

        @pl.when(pl.program_id(0) == 0)
        def _():
            dg_ref[...] = jnp.zeros_like(dg_ref)
            loss_ref[...] = jnp.zeros_like(loss_ref)

        dg_ref[...] += jnp.sum(dy * n, axis=0, keepdims=True)
        per_row = jnp.mean(e * e, axis=-1, keepdims=True)
        loss_ref[...] += 0.5 * jnp.sum(per_row, axis=0, keepdims=True)

    row = pl.BlockSpec((tm, D), lambda i: (i, 0))
    vec = pl.BlockSpec((1, D), lambda i: (0, 0))
    return ORDER.call(
        body, [x2, target, g], [row, row, vec], name="loss_head", grid=(S // tm,),
        out_specs=[pl.BlockSpec((1, 1), lambda i: (0, 0)), row, row, vec],
        out_shape=[_sds((1, 1), F32), _sds((S, D), F32), _sds((S, D), BF), _sds((1, D), F32)],
        compiler_params=_cparams(("arbitrary",)), chain_output=1,
    )


def _chains(L):
    side = min(4, L // QB)
    return side, 4 // side


def _band_scores(qkv_ref, i, L, coef, head):
    KB = _key_rows(L)
    lanes = pl.ds(head * HEAD_DIM, HEAD_DIM)
    q0 = pl.multiple_of(i * QB, QB)
    ks = pl.multiple_of(jnp.clip(i * QB - HALF_WINDOW, 0, L - KB), HALF_WINDOW)
    q = qkv_ref[0, pl.ds(q0, QB), lanes]
    k = qkv_ref[1, pl.ds(ks, KB), lanes]
    v = qkv_ref[2, pl.ds(ks, KB), lanes]
    s = lax.dot_general(q, k, (NT, ((), ())), preferred_element_type=F32) * SCALE
    qpos = q0 + lax.broadcasted_iota(jnp.int32, (QB, KB), 0)
    kpos = ks + lax.broadcasted_iota(jnp.int32, (QB, KB), 1)
    rel = jnp.abs(kpos - qpos)
    valid = rel <= HALF_WINDOW
    s = jnp.where(valid, s - coef * rel.astype(F32), NEG)
    return q0, ks, q, k, v, s, valid


def _alibi_coefs(group, d, heads):
    first = 4 * group + 1 + pl.program_id(1) * heads
    scale = jnp.full((1, 1), -(8.0 / N_HEADS_A) * math.log(2.0), F32)
    return [jnp.exp(scale * (first + hh).astype(F32)) * float(d) for hh in range(heads)]


def _dilated_view(qkv3, group, d, heads):
    per = 4 // heads
    L = qkv3.shape[1]
    if d == 1:
        return qkv3, pl.BlockSpec((3, L, heads * HEAD_DIM), lambda r, j: (0, 0, per * group + j))
    return qkv3, pl.BlockSpec((3, L, heads * HEAD_DIM), lambda r, j: (0, 0, r * per + j))


def _qkv_views(name, qkv3, views=None):
    _, S, _ = qkv3.shape
    W = 512
    tm = _tile(S, ROW_TILE)
    dilated = [(g, d) for g, d in enumerate(DILATIONS) if d > 1]
    first = dilated[0][0]
    assert [g for g, _ in dilated] == list(range(first, first + len(dilated)))
    nc = W // 128
    to_views = views is None

    def body(*refs):
        scr = refs[-nc:]
        if to_views:
            src, outs = refs[0], refs[1:1 + len(dilated)]
        else:
            ins, dst = refs[:len(dilated)], refs[len(dilated) + 1]
        for k, (_, d) in enumerate(dilated):
            @pl.when(pl.program_id(1) == k)
            def _():
                for w in range(3):
                    for c in range(nc):
                        if to_views:
                            scr[c][...] = src[w, :, c * 128:(c + 1) * 128].astype(F32)
                    for r in range(d):
                        for c in range(nc):
                            at = r * W + c * 128
                            if to_views:
                                outs[k][w, :, at:at + 128] = scr[c][pl.ds(r, tm // d, stride=d), :].astype(BF)
                            else:
                                scr[c][pl.ds(r, tm // d, stride=d), :] = ins[k][w, :, at:at + 128].astype(F32)
                    for c in range(nc):
                        if not to_views:
                            dst[w, :, c * 128:(c + 1) * 128] = scr[c][...].astype(BF)

    cols = pl.BlockSpec((3, tm, W), lambda i, k: (0, i, first + k))
    rows = [pl.BlockSpec((3, tm // d, d * W), lambda i, k: (0, i, 0)) for _, d in dilated]
    shapes = [_sds((3, S // d, d * W), BF) for _, d in dilated]
    common = dict(name=name, grid=(S // tm, len(dilated)), scratch_shapes=[pltpu.VMEM((tm, 128), F32)] * nc,
                  compiler_params=_cparams(("parallel", "arbitrary")))
    if to_views:
        outs = ORDER.call(body, [qkv3], [cols], out_specs=rows, out_shape=shapes, **common)
        return {d: o for (_, d), o in zip(dilated, outs)}
    return ORDER.call(body, [views[d] for _, d in dilated] + [qkv3], rows + [pl.BlockSpec(memory_space=pl.ANY)],
                      out_specs=cols, out_shape=_sds(qkv3.shape, BF), input_output_aliases={len(dilated): 0}, **common)


def _attn_a_fwd(qkv3, group, d):
    L = qkv3.shape[1]
    S = L * d
    assert L % QB == 0
    side, heads = _chains(L)
    view, blocks_spec = _dilated_view(qkv3, group, d, heads)

    def body(qkv_ref, o_ref, lse_ref):
        coefs = _alibi_coefs(group, d, heads)

        def step(i, carry):
            chains = [(hh, _band_scores(qkv_ref, side * i + u, L, coefs[hh], hh))
                      for u in range(side) for hh in range(heads)]
            soft = []
            for hh, (q0, _, _, _, v, s, _) in chains:
                m = jnp.max(s, axis=-1, keepdims=True)
                p = jnp.exp(s - m)
                den = jnp.sum(p, axis=-1, keepdims=True)
                soft.append((hh, q0, (p / den).astype(BF), v, m + jnp.log(den)))
            for hh, q0, pn, v, lse in soft:
                lanes = pl.ds(hh * HEAD_DIM, HEAD_DIM)
                o_ref[pl.ds(q0, QB), lanes] = jnp.dot(pn, v, preferred_element_type=F32)
                lse_ref[pl.ds(q0, QB), lanes] = jnp.broadcast_to(lse, (QB, HEAD_DIM))
            return carry

        lax.fori_loop(0, L // QB // side, step, 0)

    per = 4 // heads
    out = pl.BlockSpec((L, heads * HEAD_DIM), lambda r, j: (0, r * per + j))
    o, lse = ORDER.call(
        body, [view], [blocks_spec],
        name=f"attn_a_fwd_d{d}", grid=(d, per),
        out_specs=[out, out],
        out_shape=[_sds((L, d * 512), F32), _sds((L, d * 512), F32)],
        compiler_params=_cparams(("parallel", "parallel")),
    )
    return o, lse


def _dilated_rows(name, arrays):
    S, W = arrays[0].shape
    tm = _tile(S, ROW_TILE)
    ds_ = [d for d in DILATIONS if d > 1]
    n = len(arrays)

    def body(*refs):
        nc = W // 128
        ins, outs, scr = refs[:n], refs[n:-nc], refs[-nc:]
        for a, src in enumerate(ins):
            for c in range(nc):
                scr[c][...] = src[:, c * 128:(c + 1) * 128].astype(F32)
            for k, d in enumerate(ds_):
                dst = outs[a * len(ds_) + k]
                for r in range(d):
                    for c in range(nc):
                        at = r * W + c * 128
                        dst[:, at:at + 128] = scr[c][pl.ds(r, tm // d, stride=d), :].astype(dst.dtype)

    row = pl.BlockSpec((tm, W), lambda i: (i, 0))
    out_specs, out_shape = [], []
    for a in arrays:
        for d in ds_:
            out_specs.append(pl.BlockSpec((tm // d, d * W), lambda i: (i, 0)))
            out_shape.append(_sds((S // d, d * W), a.dtype))
    outs = ORDER.call(body, list(arrays), [row] * n, name=name, grid=(S // tm,), out_specs=out_specs,
                      out_shape=out_shape, scratch_shapes=[pltpu.VMEM((tm, 128), F32)] * (W // 128),
                      compiler_params=_cparams(("parallel",)))
    return [{d: outs[a * len(ds_) + k] for k, d in enumerate(ds_)} for a in range(n)]


def _attn_a_combine(os_, lses):
    W = 512
    S = os_[0].shape[0] * DILATIONS[0]
    tm = _tile(S, ROW_TILE)
    nc = W // 128
    dilated = [g for g, d in enumerate(DILATIONS) if d > 1]

    def body(o0, o1, o2, l0, l1, l2, y_ref, lj_ref, *scr):
        def token_order(src, g, slot):
            d = DILATIONS[g]
            if d == 1:
                return src[...]
            bufs = scr[slot * nc:(slot + 1) * nc]
            for r in range(d):
                for c in range(nc):
                    at = r * W + c * 128
                    bufs[c][pl.ds(r, tm // d, stride=d), :] = src[:, at:at + 128]
            return jnp.concatenate([buf[...] for buf in bufs], axis=1)

        slots = {g: k for k, g in enumerate(dilated)}
        ls = [token_order(l, g, slots.get(g, 0)) for g, l in enumerate((l0, l1, l2))]
        os_tok = [token_order(o, g, len(dilated) + slots.get(g, 0)) for g, o in enumerate((o0, o1, o2))]
        m = jnp.maximum(jnp.maximum(ls[0], ls[1]), ls[2])
        es = [jnp.exp(l - m) for l in ls]
        den = es[0] + es[1] + es[2]
        y = (es[0] / den) * os_tok[0] + (es[1] / den) * os_tok[1] + (es[2] / den) * os_tok[2]
        y_ref[...] = y.astype(BF)
        lj_ref[...] = m + jnp.log(den)

    row = pl.BlockSpec((tm, W), lambda i: (i, 0))
    views = [pl.BlockSpec((tm // d, d * W), lambda i: (i, 0)) for d in DILATIONS]
    return ORDER.call(
        body, [*os_, *lses], views + views, name="attn_a_combine", grid=(S // tm,), out_specs=[row, row],
        out_shape=[_sds((S, W), BF), _sds((S, W), F32)],
        scratch_shapes=[pltpu.VMEM((tm, 128), F32)] * (2 * len(dilated) * nc),
        compiler_params=_cparams(("parallel",)),
    )


def _attn_a_bwd(qkv3, dy, y, lj, dqkv3, group, d):
    L = qkv3.shape[1]
    S = L * d
    side, heads = _chains(L)
    view, blocks_spec = _dilated_view(qkv3, group, d, heads)

    def body(qkv_ref, dy_ref, y_ref, lj_ref, *rest):
        out_ref, dk_acc, dv_acc = rest[-3:]
        coefs = _alibi_coefs(group, d, heads)
        dk_acc[...] = jnp.zeros_like(dk_acc)
        dv_acc[...] = jnp.zeros_like(dv_acc)

        def step(i, carry):
            chains = [(pl.ds(hh * HEAD_DIM, HEAD_DIM), _band_scores(qkv_ref, side * i + u, L, coefs[hh], hh))
                      for u in range(side) for hh in range(heads)]
            dys = [dy_ref[pl.ds(c[0], QB), lanes] for lanes, c in chains]
            dps = [lax.dot_general(dyv, c[4], (NT, ((), ())), preferred_element_type=F32)
                   for dyv, (_, c) in zip(dys, chains)]
            grads = []
            for (lanes, (q0, ks, q, k, v, s, valid)), dyv, dp in zip(chains, dys, dps):
                rows = pl.ds(q0, QB)
                delta = jnp.sum(dyv.astype(F32) * y_ref[rows, lanes].astype(F32), axis=-1, keepdims=True)
                p = jnp.where(valid, jnp.exp(s - jnp.tile(lj_ref[rows, lanes], (1, _key_rows(L) // HEAD_DIM))), 0.0)
                grads.append(((p * (dp - delta)).astype(BF), p.astype(BF)))
            for (lanes, (q0, ks, q, k, v, s, valid)), dyv, (ds, pb) in zip(chains, dys, grads):
                out_ref[0, pl.ds(q0, QB), lanes] = (jnp.dot(ds, k, preferred_element_type=F32) * SCALE).astype(BF)
                keys = pl.ds(ks, _key_rows(L))
                dk_acc[keys, lanes] += lax.dot_general(ds, q, (TN, ((), ())), preferred_element_type=F32) * SCALE
                dv_acc[keys, lanes] += lax.dot_general(pb, dyv, (TN, ((), ())), preferred_element_type=F32)
            return carry

        lax.fori_loop(0, L // QB // side, step, 0)
        out_ref[1] = dk_acc[...].astype(BF)
        out_ref[2] = dv_acc[...].astype(BF)

    per = 4 // heads
    width = heads * HEAD_DIM
    row = pl.BlockSpec((L, width), lambda r, j: (0, r * per + j))
    operands = [view, dy, y, lj]
    scratch = [pltpu.VMEM((L, width), F32), pltpu.VMEM((L, width), F32)]
    if d == 1:
        return ORDER.call(
            body, operands + [dqkv3], [blocks_spec, row, row, row, pl.BlockSpec(memory_space=pl.ANY)],
            name=f"attn_a_bwd_d{d}", grid=(d, per), out_specs=blocks_spec, out_shape=_sds((3, S, QKV_W), BF),
            scratch_shapes=scratch, input_output_aliases={4: 0}, compiler_params=_cparams(("parallel", "parallel")))
    return ORDER.call(
        body, operands, [blocks_spec, row, row, row], name=f"attn_a_bwd_d{d}", grid=(d, per),
        out_specs=blocks_spec, out_shape=_sds((3, L, d * 512), BF),
        scratch_shapes=scratch, compiler_params=_cparams(("parallel", "parallel")))


def _toeplitz_onehot():
    oh = np.zeros((64, GRID_W, 128), np.float32)
    for qc in range(GRID_W):
        for m in range(128):
            kc = m % GRID_W
            dc = int(np.clip(kc - qc, -(NA_COLS - 1), NA_COLS - 1)) + NA_COLS - 1
            oh[(m // GRID_W) * 32 + dc, qc, m] = 1.0
    return oh.reshape(64, GRID_W * 128)


def _nbr_scores(qkv_ref, e2_ref, r, rows, ok):
    rs = jnp.clip(r - NA_ROWS // 2, 0, rows - NA_ROWS)
    q0 = pl.multiple_of(r * GRID_W, GRID_W)
    k0 = pl.multiple_of(rs * GRID_W, GRID_W)
    q = qkv_ref[0, pl.ds(q0, GRID_W), :]
    k = qkv_ref[1, pl.ds(k0, NA_ROWS * GRID_W), :]
    v = qkv_ref[2, pl.ds(k0, NA_ROWS * GRID_W), :]
    s = lax.dot_general(q, k, (NT, ((), ())), preferred_element_type=F32) * SCALE
    first = rs - r + NA_ROWS - 1
    bias = jnp.concatenate([e2_ref[first + 2 * pair] for pair in range(NA_ROWS // 2)], axis=1)
    s = jnp.where(ok, s + bias, NEG)
    return q0, k0, first, q, k, v, s


def _nbr_col_ok():
    qc = lax.broadcasted_iota(jnp.int32, (GRID_W, NA_ROWS * GRID_W), 0)
    kc = lax.broadcasted_iota(jnp.int32, (GRID_W, NA_ROWS * GRID_W), 1) % GRID_W
    cs = jnp.clip(qc - NA_COLS // 2, 0, GRID_W - NA_COLS)
    return (kc >= cs) & (kc < cs + NA_COLS)


def _attn_b_fwd(qkv3, e2):
    _, S, _ = qkv3.shape
    rows = S // GRID_W
    assert rows >= NA_ROWS

    def body(qkv_ref, e2_ref, o_ref, lse_ref):
        ok = _nbr_col_ok()

        def step(i, carry):
            blocks = [_nbr_scores(qkv_ref, e2_ref, NBR_SIDE * i + u, rows, ok) for u in range(NBR_SIDE)]
            soft = []
            for q0, _, _, _, _, v, s in blocks:
                m = jnp.max(s, axis=-1, keepdims=True)
                p = jnp.exp(s - m)
                den = jnp.sum(p, axis=-1, keepdims=True)
                soft.append((q0, (p / den).astype(BF), v, m + jnp.log(den)))
            for q0, pn, v, lse in soft:
                o_ref[pl.ds(q0, GRID_W), :] = jnp.dot(pn, v, preferred_element_type=F32).astype(BF)
                lse_ref[pl.ds(q0, GRID_W), :] = jnp.broadcast_to(lse, (GRID_W, HEAD_DIM))
            return carry

        lax.fori_loop(0, rows // NBR_SIDE, step, 0)

    out = pl.BlockSpec((S, HEAD_DIM), lambda h: (0, h))
    return ORDER.call(
        body, [qkv3, e2],
        [pl.BlockSpec((3, S, HEAD_DIM), lambda h: (0, 0, N_HEADS_A + h)),
         pl.BlockSpec((None, RPB_ROWS - 1, GRID_W, 128), lambda h: (h, 0, 0, 0))],
        name="attn_b_fwd", grid=(4,),
        out_specs=[out, out], out_shape=[_sds((S, 512), BF), _sds((S, 512), F32)],
        compiler_params=_cparams(("parallel",)),
    )


def _attn_b_bwd(qkv3, e2, dy, y, lse, dqkv3):
    _, S, _ = qkv3.shape
    rows = S // GRID_W
    nk = NA_ROWS * GRID_W

    def body(qkv_ref, e2_ref, dy_ref, y_ref, lse_ref, _, out_ref, de2_ref, dk_acc, dv_acc):
        ok = _nbr_col_ok()
        dk_acc[...] = jnp.zeros_like(dk_acc)
        dv_acc[...] = jnp.zeros_like(dv_acc)
        de2_ref[...] = jnp.zeros_like(de2_ref)

        def step(i, carry):
            blocks = [_nbr_scores(qkv_ref, e2_ref, NBR_SIDE * i + u, rows, ok) for u in range(NBR_SIDE)]
            dys = [dy_ref[pl.ds(b[0], GRID_W), :] for b in blocks]
            dps = [lax.dot_general(dyv, b[5], (NT, ((), ())), preferred_element_type=F32) for dyv, b in zip(dys, blocks)]
            grads = []
            for (q0, k0, first, q, k, v, s), dyv, dp in zip(blocks, dys, dps):
                qrows = pl.ds(q0, GRID_W)
                delta = jnp.sum(dyv.astype(F32) * y_ref[qrows, :].astype(F32), axis=-1, keepdims=True)
                p = jnp.where(ok, jnp.exp(s - jnp.tile(lse_ref[qrows, :], (1, nk // HEAD_DIM))), 0.0)
                ds = p * (dp - delta)
                for pair in range(NA_ROWS // 2):
                    de2_ref[first + 2 * pair] += ds[:, pair * 128:(pair + 1) * 128]
                grads.append((ds.astype(BF), p.astype(BF)))
            for (q0, k0, first, q, k, v, s), dyv, (dsb, pb) in zip(blocks, dys, grads):
                out_ref[0, pl.ds(q0, GRID_W), :] = (jnp.dot(dsb, k, preferred_element_type=F32) * SCALE).astype(BF)
                keys = pl.ds(k0, nk)
                dk_acc[keys, :] += lax.dot_general(dsb, q, (TN, ((), ())), preferred_element_type=F32) * SCALE
                dv_acc[keys, :] += lax.dot_general(pb, dyv, (TN, ((), ())), preferred_element_type=F32)
            return carry

        lax.fori_loop(0, rows // NBR_SIDE, step, 0)
        out_ref[1] = dk_acc[...].astype(BF)
        out_ref[2] = dv_acc[...].astype(BF)

    heads = pl.BlockSpec((3, S, HEAD_DIM), lambda h: (0, 0, N_HEADS_A + h))
    row = pl.BlockSpec((S, HEAD_DIM), lambda h: (0, h))
    table = pl.BlockSpec((None, RPB_ROWS - 1, GRID_W, 128), lambda h: (h, 0, 0, 0))
    return ORDER.call(
        body, [qkv3, e2, dy, y, lse, dqkv3],
        [heads, table, row, row, row, pl.BlockSpec(memory_space=pl.ANY)], name="attn_b_bwd", grid=(4,),
        out_specs=[heads, table],
        out_shape=[_sds((3, S, QKV_W), BF), _sds((4, RPB_ROWS - 1, GRID_W, 128), F32)],
        scratch_shapes=[pltpu.VMEM((S, HEAD_DIM), F32), pltpu.VMEM((S, HEAD_DIM), F32)],
        input_output_aliases={5: 0},
        compiler_params=_cparams(("parallel",)), chain_output=1,
    )


def _rpb_to_table(rpb):
    pad = jnp.pad(rpb, ((0, 0), (0, 0), (0, 1)))
    pairs = jnp.concatenate([pad[:, :-1], pad[:, 1:]], axis=-1).reshape(4 * (RPB_ROWS - 1), 64)
    onehot = jnp.asarray(_toeplitz_onehot())
    n = onehot.shape[1]
    tn = 2048
    full = lambda i, j, k: (0, 0)
    (e2,) = _matmul("rpb_table", pairs, onehot, pl.BlockSpec(pairs.shape, full),
                    pl.BlockSpec((64, tn), lambda i, j, k: (0, j)), NN, (1, n // tn, 1), (pairs.shape[0], tn), [],
                    [(_sds((pairs.shape[0], n), F32), pl.BlockSpec((pairs.shape[0], tn), lambda i, j, k: (0, j)))],
                    _store(F32), precision=lax.Precision.HIGHEST)
    return e2.reshape(4, RPB_ROWS - 1, GRID_W, 128)


def _table_grad_to_rpb(de2):
    onehot = jnp.asarray(_toeplitz_onehot())
    n = onehot.shape[1]
    flat = de2.reshape(4 * (RPB_ROWS - 1), n)
    tk = 2048
    (dpairs,) = _matmul("rpb_table_grad", flat, onehot, pl.BlockSpec((flat.shape[0], tk), lambda i, j, k: (0, k)),
                        pl.BlockSpec((64, tk), lambda i, j, k: (0, k)), NT, (1, 1, n // tk), (flat.shape[0], 64), [],
                        [(_sds((flat.shape[0], 64), F32), pl.BlockSpec((flat.shape[0], 64), lambda i, j, k: (0, 0)))],
                        _store(F32), precision=lax.Precision.HIGHEST)
    dpairs = dpairs.reshape(4, RPB_ROWS - 1, 64)
    zero = jnp.zeros((4, 1, RPB_COLS), F32)
    return (jnp.concatenate([dpairs[:, :, :RPB_COLS], zero], axis=1)
            + jnp.concatenate([zero, dpairs[:, :, 32:32 + RPB_COLS]], axis=1))


HBM = pl.BlockSpec(memory_space=pl.ANY)


def _place():
    x, y, c = lax.axis_index("x"), lax.axis_index("y"), lax.axis_index("c")
    chips = [(1 - x, y), (x, 1 - y), (1 - x, 1 - y)]
    return x, y, c, chips


def _remote(src, dst, send_sem, recv_sem, to):
    return pltpu.make_async_remote_copy(src_ref=src, dst_ref=dst, send_sem=send_sem, recv_sem=recv_sem,
                                        device_id=to, device_id_type=MESH)


def _place_shard(name, w, me, plain=False):
    R, C = w.shape
    tr = _tile(R, 256)

    def body(me_ref, w_ref, *o_refs):
        for o_ref in o_refs:
            o_ref[...] = w_ref[...].astype(BF)

    row = pl.BlockSpec((tr, C), lambda i, mr: (i, 0))
    placed = pl.BlockSpec((None, tr, C), lambda i, mr: (mr[0], i, 0))
    return ORDER.call(
        body, [w], [row], prefetch=(me,), name=name, grid=(R // tr,),
        out_specs=[placed, row] if plain else [placed],
        out_shape=[_sds((N_CHIPS, R, C), BF)] + ([_sds((R, C), BF)] if plain else []),
        compiler_params=_cparams(("parallel",)),
    )


SEM = pl.BlockSpec(memory_space=pltpu.SEMAPHORE)
IN_HBM = pl.BlockSpec(memory_space=pltpu.HBM)
DATAFLOW = pltpu.SideEffectType.DATAFLOW_SIDE_EFFECTING


def _in_hbm(a):
    return pltpu.with_memory_space_constraint(a, pltpu.HBM)


def _copy_start(name, bufs, copies, n_copies, earlier=None):
    n = len(bufs)
    after = None if any(b is ORDER.last for b in bufs) else ORDER.last
    n_extra = (2 if earlier is not None else 0) + (1 if after is not None else 0)

    def body(*refs):
        ins = refs[:n]
        if earlier is not None:
            for k, (src, dst, to) in enumerate(earlier[0](ins)):
                cp = _remote(src, dst, refs[n].at[k], refs[n + 1].at[k], to)
                cp.wait_send()
                cp.wait_recv()
        send_sems, recv_sems = refs[n + n_extra], refs[n + n_extra + 1]
        for k, (src, dst, to) in enumerate(copies(ins)):
            _remote(src, dst, send_sems.at[k], recv_sems.at[k], to).start()
        refs[-1][...] = jnp.zeros((8, 128), F32)

    operands = [_in_hbm(b) for b in bufs]
    in_specs = [IN_HBM] * n
    if earlier is not None:
        operands += [earlier[1], earlier[2]]
        in_specs += [SEM, SEM]
    if after is not None:
        operands.append(after)
        in_specs.append(HBM)
    outs = pl.pallas_call(
        body, name=name,
        out_shape=(pltpu.SemaphoreType.DMA((n_copies,)), pltpu.SemaphoreType.DMA((n_copies,)),
                   *[pltpu.HBM(b.shape, b.dtype) for b in bufs], _sds((8, 128), F32)),
        in_specs=in_specs,
        out_specs=(SEM, SEM, *[IN_HBM] * n, pl.BlockSpec(memory_space=pltpu.VMEM)),
        input_output_aliases={i: 2 + i for i in range(n)},
        compiler_params=pltpu.CompilerParams(has_side_effects=DATAFLOW),
    )(*operands)
    ORDER.last = outs[-1]
    return outs[0], outs[1], list(outs[2:2 + n])


def _copy_wait(name, bufs, copies, send_sems, recv_sems):
    n = len(bufs)
    after = ORDER.last

    def body(*refs):
        ins = refs[:n]
        for k, (src, dst, to) in enumerate(copies(ins)):
            cp = _remote(src, dst, refs[n].at[k], refs[n + 1].at[k], to)
            cp.wait_send()
            cp.wait_recv()

    outs = list(pl.pallas_call(
        body, name=name,
        out_shape=tuple(pltpu.HBM(b.shape, b.dtype) for b in bufs),
        in_specs=[IN_HBM] * n + [SEM, SEM, HBM], out_specs=tuple([IN_HBM] * n),
        input_output_aliases={i: i for i in range(n)},
        compiler_params=pltpu.CompilerParams(has_side_effects=DATAFLOW),
    )(*bufs, send_sems, recv_sems, after))
    ORDER.last = outs[0]
    return outs


def _gather_hop1(bufs):
    x, y, c, chips = _place()
    out = []
    for b in bufs:
        half = b.shape[1] // 2
        mine = b.at[2 * x + y, pl.ds(c * half, half), :]
        out += [(mine, mine, (*chip, c)) for chip in chips]
    return out


def _gather_hop2(bufs):
    x, y, c, chips = _place()
    out = []
    for b in bufs:
        half = b.shape[1] // 2
        for chip in chips:
            landed = b.at[2 * chip[0] + chip[1], pl.ds(c * half, half), :]
            out.append((landed, landed, (x, y, 1 - c)))
    return out


def _swap_copies(bufs):
    x, y, c, _ = _place()
    n = len(bufs) // 2
    out = []
    for p, land in zip(bufs[:n], bufs[n:]):
        half = p.shape[1] // 2
        out.append((p.at[:, pl.ds((1 - c) * half, half), :], land, (x, y, 1 - c)))
    return out


def _scatter_copies(bufs):
    _, _, c, chips = _place()
    n = len(bufs) // 2
    out = []
    for s_, land in zip(bufs[:n], bufs[n:]):
        out += [(s_.at[2 * chip[0] + chip[1]], land.at[j], (*chip, c)) for j, chip in enumerate(chips)]
    return out


def _join_copies(bufs):
    x, y, c, _ = _place()
    out = []
    for b in bufs:
        half = b.shape[0] // 2
        mine = b.at[pl.ds(c * half, half), :]
        out.append((mine, mine, (x, y, 1 - c)))
    return out


def _gather_small(vec):
    m_per, n = vec.shape

    def body(x_ref, out_ref, send_sems, recv_sems, local_sem):
        x, y, c, chips = _place()
        me, sibling = (x, y, c), (x, y, 1 - c)

        def rows(px, py, pc):
            return out_ref.at[pl.ds((4 * px + 2 * py + pc) * m_per, m_per), :]

        def copy(k, block, to, src=None):
            return _remote(rows(*block) if src is None else src, rows(*block), send_sems.at[k], recv_sems.at[k], to)

        mine = pltpu.make_async_copy(x_ref, rows(*me), local_sem)
        mine.start()
        first = [copy(0, me, sibling, src=x_ref)]
        first += [copy(1 + j, me, (*chip, c), src=x_ref) for j, chip in enumerate(chips)]
        for cp in first:
            cp.start()
        passed = [copy(4 + j, (*chip, c), sibling) for j, chip in enumerate(chips)]
        for j, chip in enumerate(chips):
            copy(1 + j, (*chip, c), me).wait_recv()
            passed[j].start()
        copy(0, sibling, me).wait_recv()
        for j, chip in enumerate(chips):
            copy(4 + j, (*chip, 1 - c), me).wait_recv()
        for cp in first + passed:
            cp.wait_send()
        mine.wait()

    return ORDER.call(
        body, [vec], [pl.BlockSpec(memory_space=pltpu.VMEM)], name="gather_small_grads",
        out_shape=_sds((8 * m_per, n), vec.dtype), out_specs=pl.BlockSpec(memory_space=pltpu.VMEM),
        scratch_shapes=[pltpu.SemaphoreType.DMA((7,)), pltpu.SemaphoreType.DMA((7,)), pltpu.SemaphoreType.DMA],
    )


def _add_sibling(name, partial, received, c):
    _, R, C = partial.shape
    half = R // 2
    tr = _tile(half, 256)
    nb = half // tr

    def body(c_ref, p_ref, r_ref, o_ref):
        o_ref[...] = (p_ref[...].astype(F32) + r_ref[...].astype(F32)).astype(BF)

    return ORDER.call(
        body, [partial, received],
        [pl.BlockSpec((None, tr, C), lambda j, i, cr: (j, cr[0] * nb + i, 0)),
         pl.BlockSpec((None, tr, C), lambda j, i, cr: (j, i, 0))],
        prefetch=(c,), name=name, grid=(N_CHIPS, nb),
        out_specs=pl.BlockSpec((None, tr, C), lambda j, i, cr: (j, i, 0)),
        out_shape=_sds((N_CHIPS, half, C), BF), compiler_params=_cparams(("parallel", "parallel")),
    )


def _add_chips(name, sums, received, me_c):
    _, half, C = sums.shape
    tr = _tile(half, 256)
    nb = half // tr

    def body(mc_ref, s_ref, r_ref, o_ref):
        acc = s_ref[...].astype(F32)
        for j in range(3):
            acc = acc + r_ref[j].astype(F32)
        o_ref[...] = acc

    return ORDER.call(
        body, [sums, received],
        [pl.BlockSpec((None, tr, C), lambda i, mc: (mc[0], i, 0)),
         pl.BlockSpec((3, tr, C), lambda i, mc: (0, i, 0))],
        prefetch=(me_c,), name=name, grid=(nb,),
        out_specs=pl.BlockSpec((tr, C), lambda i, mc: (mc[1] * nb + i, 0)),
        out_shape=_sds((2 * half, C), F32), compiler_params=_cparams(("parallel",)),
    )


def _adamw_math(w, g, m, v):
    m = ADAM_B1 * m + (1.0 - ADAM_B1) * g
    v = ADAM_B2 * v + (1.0 - ADAM_B2) * (g * g)
    m_hat = m / (1.0 - ADAM_B1 ** ADAM_STEP)
    v_hat = v / (1.0 - ADAM_B2 ** ADAM_STEP)
    delta = -ADAM_LR * (m_hat / (jnp.sqrt(v_hat) + ADAM_EPS) + ADAM_WD * w)
    return delta, m, v


def _adamw(name, w, g, m, v):
    R, C = w.shape
    tr = _tile(R, 256)

    def body(w_ref, g_ref, m_ref, v_ref, go_ref, d_ref, mo_ref, vo_ref):
        gv = g_ref[...]
        go_ref[...] = gv
        d_ref[...], mo_ref[...], vo_ref[...] = _adamw_math(w_ref[...], gv, m_ref[...], v_ref[...])

    row = pl.BlockSpec((tr, C), lambda i: (i, 0))
    return ORDER.call(
        body, [w, g, m, v], [row] * 4, name=name, grid=(R // tr,), out_specs=[row] * 4,
        out_shape=[_sds((R, C), F32)] * 4, compiler_params=_cparams(("parallel",)), chain_output=1,
    )


def _adamw_small(gathered, w, m, v):
    rows, n = w.shape

    def body(ga_ref, w_ref, m_ref, v_ref, go_ref, d_ref, mo_ref, vo_ref):
        g = ga_ref[pl.ds(0, rows), :]
        for dev in range(1, 8):
            g = g + ga_ref[pl.ds(dev * rows, rows), :]
        go_ref[...] = g
        d_ref[...], mo_ref[...], vo_ref[...] = _adamw_math(w_ref[...], g, m_ref[...], v_ref[...])

    whole = pl.BlockSpec(memory_space=pltpu.VMEM)
    return ORDER.call(
        body, [gathered, w, m, v], [whole] * 4, name="adamw_small", out_specs=[whole] * 4,
        out_shape=[_sds((rows, n), F32)] * 4, compiler_params=_cparams(), chain_output=1,
    )


def _proj_merge(y_a, y_b, gpa, gpb, g3):
    S, K = y_a.shape
    _, _, Nq = gpa.shape
    D = N_CHIPS * Nq
    tm, tn = _tile(S, 1024), _tile(Nq, 512)
    q = Nq // tn

    def body(ya_ref, yb_ref, wa_ref, wb_ref, g_ref, merged_ref, c_ref):
        pa = jnp.dot(ya_ref[...], wa_ref[...], preferred_element_type=F32)
        pb = jnp.dot(yb_ref[...], wb_ref[...], preferred_element_type=F32)
        g = g_ref[...].astype(F32)
        merged_ref[...] = (g[0] * pa + g[1] * pb).astype(BF)
        c_ref[0] = (pa * g[0] * (1.0 - g[0])).astype(BF)
        c_ref[1] = (pb * g[1] * (1.0 - g[1])).astype(BF)

    rows = pl.BlockSpec((tm, K), lambda i, j: (i, 0))
    weight = pl.BlockSpec((None, K, tn), lambda i, j: (j // q, 0, j % q))
    pair = pl.BlockSpec((2, tm, tn), lambda i, j: (0, i, j))
    return ORDER.call(
        body, [y_a, y_b, gpa, gpb, g3], [rows, rows, weight, weight, pair], name="proj_merge",
        grid=(S // tm, N_CHIPS * q), out_specs=[pl.BlockSpec((tm, tn), lambda i, j: (i, j)), pair],
        out_shape=[_sds((S, D), BF), _sds((2, S, D), BF)], compiler_params=_cparams(("parallel", "parallel")))


class _Exchange:
    GATHER = (("qkv",), ("gate",), ("proj_a", "proj_b", "out"), ("up",), ("down",))
    REDUCE = {"mlp": ("down", "up"), "mix": ("out", "proj_a", "proj_b"), "in": ("qkv", "gate")}

    OWN_FIRST = ("qkv", "gate")

    def __init__(self, shards, me, c):
        self.me, self.c = me, c
        self.hop1, self.hop2, self.stage, self.grads, self.own = {}, {}, {}, {}, {}
        for g, names in enumerate(self.GATHER):
            bufs = []
            for n in names:
                placed = _place_shard(f"place_{n}", shards[n], me, plain=n in self.OWN_FIRST)
                bufs.append(placed[0])
                if n in self.OWN_FIRST:
                    self.own[n] = placed[1]
            self.hop1[g] = _copy_start(f"gather{g}_start", bufs, _gather_hop1, 3 * len(names))

    def forward(self, g):
        send, recv, thru = self.hop1.pop(g)
        self.hop2[g] = _copy_start(f"gather{g}_forward", thru, _gather_hop2, len(thru) * 3,
                                   earlier=(_gather_hop1, send, recv))

    def weights(self, g):
        send, recv, thru = self.hop2.pop(g)
        return _copy_wait(f"gather{g}_wait", thru, _gather_hop2, send, recv)

    def reduce(self, key, partials=None):
        names = self.REDUCE[key]
        n = len(names)
        if partials is not None:
            lands = [lax.empty((p.shape[0], p.shape[1] // 2, p.shape[2]), p.dtype) for p in partials]
            self.stage[key] = ("swap",) + _copy_start(f"reduce_{key}_swap", list(partials) + lands, _swap_copies, n)
            return
        kind, send, recv, thru = self.stage.pop(key)
        if kind == "swap":
            thru = _copy_wait(f"reduce_{key}_swap_wait", thru, _swap_copies, send, recv)
            sums = [_add_sibling(f"reduce_{nm}_add_sibling", p, r, self.c)
                    for nm, p, r in zip(names, thru[:n], thru[n:])]
            lands = [lax.empty((3,) + s_.shape[1:], s_.dtype) for s_ in sums]
            self.stage[key] = ("scatter",) + _copy_start(f"reduce_{key}_scatter", sums + lands, _scatter_copies, 3 * n)
        elif kind == "scatter":
            thru = _copy_wait(f"reduce_{key}_scatter_wait", thru, _scatter_copies, send, recv)
            me_c = jnp.concatenate([self.me, self.c])
            halves = [_add_chips(f"reduce_{nm}_add_chips", s_, r, me_c)
                      for nm, s_, r in zip(names, thru[:n], thru[n:])]
            self.stage[key] = ("join",) + _copy_start(f"reduce_{key}_join", halves, _join_copies, n)
        else:
            thru = _copy_wait(f"reduce_{key}_join_wait", thru, _join_copies, send, recv)
            self.grads.update(zip(names, thru))


def _forward_backward(x, target, norm_mix, b_gate, rpb, norm_mlp, norm_final, ex):
    S, D = x.shape

    h1 = _rms_fwd("rms_mix", x, norm_mix)
    nq = QKV_W // 512
    qkv_out = (((3, S, QKV_W), BF), lambda i, T: (T // nq, i, T % nq))
    tg = _tile(ex.own["gate"].shape[1], 1024)
    ng = D // tg
    gate_out = (((2, S, D), BF), lambda i, T: (T // ng, i, T % ng))

    def gate_epilogue(acc, ex_, outs):
        outs[0][...] = jax.nn.sigmoid(acc + ex_[0][...]).astype(BF)

    qkv3 = _mm_nn_shards("qkv_own", h1, ex.own["qkv"], ex.me, True, *qkv_out, _store(BF))
    g3 = _mm_nn_shards("gate_own", h1, ex.own["gate"], ex.me, True, *gate_out, gate_epilogue, extras=[b_gate], tn=tg)
    ex.forward(0)
    e2 = _rpb_to_table(rpb)
    (gq,) = ex.weights(0)
    qkv3 = _mm_nn_shards("qkv", h1, gq, ex.me, False, *qkv_out, _store(BF), into=qkv3)

    ex.forward(1)
    outs_a = [_attn_a_fwd(qkv3, 0, DILATIONS[0])]
    (gg,) = ex.weights(1)
    g3 = _mm_nn_shards("gate", h1, gg, ex.me, False, *gate_out, gate_epilogue, extras=[b_gate], into=g3, tn=tg)

    ex.forward(2)
    qkv_views = _qkv_views("qkv_views", qkv3)
    outs_a += [_attn_a_fwd(qkv_views[d], grp, d) for grp, d in enumerate(DILATIONS) if grp > 0]
    y_a, lj = _attn_a_combine([o for o, _ in outs_a], [l for _, l in outs_a])
    y_b, lse_b = _attn_b_fwd(qkv3, e2)
    gpa, gpb, gout = ex.weights(2)
    wout = gout.reshape(D, D)
    merged, c3 = _proj_merge(y_a, y_b, gpa, gpb, g3)

    def residual_epilogue(acc, ex_, outs):
        outs[0][...] = acc + ex_[0][...]

    def nn_plain(name, a, w, res):
        M, K = a.shape
        N = w.shape[1]
        bm, bn, bk = _tile(M, 1024), _tile(N, 1024), _tile(K, 2048)
        t = pl.BlockSpec((bm, bn), lambda i, j, k: (i, j))
        return _matmul(name, a, w, pl.BlockSpec((bm, bk), lambda i, j, k: (i, k)),
                       pl.BlockSpec((bk, bn), lambda i, j, k: (k, j)), NN, (M // bm, N // bn, K // bk), (bm, bn),
                       [(res, t)], [(_sds((M, N), F32), t)], residual_epilogue)[0]

    ex.forward(3)
    x1 = nn_plain("out_proj", merged, wout, x)
    h2 = _rms_fwd("rms_mlp", x1, norm_mlp)
    (gup,) = ex.weights(3)
    F = gup.shape[2] * N_CHIPS
    ex.forward(4)

    def up_epilogue(acc, ex_, outs):
        ru = jnp.maximum(acc, 0.0)
        outs[0][...] = (ru * ru).astype(BF)
        outs[1][...] = ru.astype(BF)

    tu = _tile(gup.shape[2], 1024)
    ut = pl.BlockSpec((_tile(S, 1024), tu), lambda i, j, k: (i, j))
    (act, ru), _ = _mm_nn_cols("mlp_up", h2, gup, BF, epilogue=up_epilogue, tn=tu,
                               outs=[(_sds((S, F), BF), ut), (_sds((S, F), BF), ut)])
    (gdown,) = ex.weights(4)
    wdown = gdown.reshape(F, D)
    x2 = nn_plain("mlp_down", act, wdown, x1)

    loss, dx2, dx2b, d_norm_final = _loss_head(x2, target, norm_final.reshape(1, D))

    def nt_rows(name, a, w, epilogue, extras, outs, bn=1024):
        M, N = a.shape
        K = w.shape[0]
        bm, bn, bk = _tile(M, 1024), _tile(K, bn), _tile(N, 2048)
        return _matmul(name, a, w, pl.BlockSpec((bm, bk), lambda i, j, k: (i, k)),
                       pl.BlockSpec((bn, bk), lambda i, j, k: (j, k)), NT, (M // bm, K // bn, N // bk), (bm, bn),
                       extras(bm, bn), outs(bm, bn), epilogue)

    def nt_cols(name, a_spec_fn, a, g, M, epilogue, extras, outs, bk, bn=1024):
        _, K, Nq = g.shape
        bm, bn, bk = _tile(M, 1024), _tile(K, bn), _tile(Nq, bk)
        q = Nq // bk
        return _matmul(name, a, g, a_spec_fn(bm, bk), pl.BlockSpec((None, bn, bk), lambda i, j, k: (k // q, j, k % q)),
                       NT, (M // bm, K // bn, N_CHIPS * q), (bm, bn), extras(bm, bn), outs(bm, bn), epilogue)

    def tn_grad(name, a, a_spec_fn, b, b_spec_fn, Kin, N, out_shape, out_spec_fn, bn=1024):
        bm, bn, bk = _tile(Kin, 1024), _tile(N, bn), _tile(S, 4096)
        return _matmul(name, a, b, a_spec_fn(bk, bm), b_spec_fn(bk, bn), TN, (Kin // bm, N // bn, S // bk), (bm, bn),
                       [], [(_sds(out_shape, BF), out_spec_fn(bm, bn))], _store(BF))[0]

    plain_a = lambda bk, bm: pl.BlockSpec((bk, bm), lambda i, j, k: (k, i))
    plain_b = lambda bk, bn: pl.BlockSpec((bk, bn), lambda i, j, k: (k, j))
    plain_o = lambda bm, bn: pl.BlockSpec((bm, bn), lambda i, j, k: (i, j))
    a_rows = lambda bm, bk: pl.BlockSpec((bm, bk), lambda i, j, k: (i, k))

    def cols_o(Nq):
        def spec(bm, bn):
            q = Nq // bn
            return pl.BlockSpec((None, bm, bn), lambda i, j, k: (j // q, i, j % q))
        return spec

    def du_epilogue(acc, ex_, outs):
        outs[0][...] = (acc * (2.0 * ex_[0][...].astype(F32))).astype(BF)

    dw_down = tn_grad("mlp_down_dw", act, plain_a, dx2b, plain_b, F, D, (F, D), plain_o)
    (du,) = nt_rows("mlp_down_dx", dx2b, wdown, du_epilogue,
                    lambda bm, bn: [(ru, plain_o(bm, bn))], lambda bm, bn: [(_sds((S, F), BF), plain_o(bm, bn))])

    fq = gup.shape[2]
    dw_up = tn_grad("mlp_up_dw", h2, plain_a, du, plain_b, D, F, (N_CHIPS, D, fq), cols_o(fq), bn=min(fq, 1024))
    ex.reduce("mlp", partials=[dw_down.reshape(N_CHIPS, F // N_CHIPS, D), dw_up])
    (dh2,) = nt_cols("mlp_up_dx", a_rows, du, gup, S, _store(F32), lambda bm, bn: [],
                     lambda bm, bn: [(_sds((S, D), F32), plain_o(bm, bn))], 2048)
    ex.reduce("mlp")
    dx1, dx1b, d_norm_mlp = _rms_bwd("rms_mlp_bwd", dh2, x1, norm_mlp, dx2)

    def merge_bwd_epilogue(acc, ex_, outs):
        g, c = ex_[0][...].astype(F32), ex_[1][...].astype(F32)
        outs[0][...] = (acc * g[0]).astype(BF)
        outs[1][...] = (acc * g[1]).astype(BF)
        dga = acc * c[0]
        dgb = acc * c[1]
        outs[2][0] = dga.astype(BF)
        outs[2][1] = dgb.astype(BF)
        outs[3][...] = jnp.concatenate([jnp.sum(dga, axis=0, keepdims=True), jnp.sum(dgb, axis=0, keepdims=True)], 0)

    def pair(bm, bn):
        return pl.BlockSpec((2, bm, bn), lambda i, j, k: (0, i, j))

    n_row_blocks = S // _tile(S, 1024)
    dpa, dpb, dg3, db_gate = nt_rows(
        "out_proj_dx", dx1b, wout, merge_bwd_epilogue,
        lambda bm, bn: [(g3, pair(bm, bn)), (c3, pair(bm, bn))],
        lambda bm, bn: [(_sds((S, D), BF), plain_o(bm, bn)), (_sds((S, D), BF), plain_o(bm, bn)),
                        (_sds((2, S, D), BF), pair(bm, bn)),
                        (_sds((n_row_blocks, 2, D), F32), pl.BlockSpec((None, 2, bn), lambda i, j, k: (i, 0, j)))],
        bn=512)
    dw_out = tn_grad("out_proj_dw", merged, plain_a, dx1b, plain_b, D, D, (D, D), plain_o)

    pq = gpa.shape[2]
    proj_dx = lambda name, dproj, g: nt_cols(name, a_rows, dproj, g, S, _store(BF), lambda bm, bn: [],
                                             lambda bm, bn: [(_sds((S, 512), BF), plain_o(bm, bn))], 512)[0]
    dw_pa = tn_grad("proj_a_dw", y_a, plain_a, dpa, plain_b, 512, D, (N_CHIPS, 512, pq), cols_o(pq), bn=min(pq, 512))
    dw_pb = tn_grad("proj_b_dw", y_b, plain_a, dpb, plain_b, 512, D, (N_CHIPS, 512, pq), cols_o(pq), bn=min(pq, 512))
    ex.reduce("mix", partials=[dw_out.reshape(N_CHIPS, D // N_CHIPS, D), dw_pa, dw_pb])
    dy_a = proj_dx("proj_a_dx", dpa, gpa)
    dy_b = proj_dx("proj_b_dx", dpb, gpb)

    dqkv3 = lax.empty((3, S, QKV_W), BF)
    dqkv3 = _attn_a_bwd(qkv3, dy_a, y_a, lj, dqkv3, 0, DILATIONS[0])
    ex.reduce("mix")
    dy_views, y_views, lj_views = _dilated_rows("attn_a_bwd_rows", [dy_a, y_a, lj])
    dqkv_views = {d: _attn_a_bwd(qkv_views[d], dy_views[d], y_views[d], lj_views[d], None, grp, d)
                  for grp, d in enumerate(DILATIONS) if grp > 0}
    dqkv3 = _qkv_views("dqkv_from_views", dqkv3, dqkv_views)
    dqkv3, de2 = _attn_b_bwd(qkv3, e2, dy_b, y_b, lse_b, dqkv3)
    d_rpb = _table_grad_to_rpb(de2)

    def stacked_a(width):
        def spec(bm, bk):
            q = width // bk
            return pl.BlockSpec((None, bm, bk), lambda i, j, k: (k // q, i, k % q))
        return spec

    def stacked_b(width):
        def spec(bk, bn):
            q = width // bn
            return pl.BlockSpec((None, bk, bn), lambda i, j, k: (j // q, k, j % q))
        return spec

    dw_qkv = tn_grad("qkv_dw", h1, plain_a, dqkv3, stacked_b(QKV_W), D, 3 * QKV_W, (N_CHIPS,) + gq.shape[1:],
                     cols_o(gq.shape[2]), bn=512)
    dw_gate = tn_grad("gate_dw", h1, plain_a, dg3, stacked_b(D), D, 2 * D, (N_CHIPS,) + gg.shape[1:],
                      cols_o(gg.shape[2]), bn=gg.shape[2])
    ex.reduce("in", partials=[dw_qkv, dw_gate])
    ex.reduce("mlp")
    (dh1_q,) = nt_cols("qkv_dx", stacked_a(QKV_W), dqkv3, gq, S, _store(F32), lambda bm, bn: [],
                       lambda bm, bn: [(_sds((S, D), F32), plain_o(bm, bn))], 512, bn=2048)
    ex.reduce("in")
    ex.reduce("mix")

    def add_epilogue(acc, ex_, outs):
        outs[0][...] = acc + ex_[0][...]

    (dh1,) = nt_cols("gate_dx", stacked_a(D), dg3, gg, S, add_epilogue, lambda bm, bn: [(dh1_q, plain_o(bm, bn))],
                     lambda bm, bn: [(_sds((S, D), F32), plain_o(bm, bn))], gg.shape[2])
    grad_x, _, d_norm_mix = _rms_bwd("rms_mix_bwd", dh1, x, norm_mix, dx1)
    ex.reduce("mlp")
    ex.reduce("mix")

    small = [d_norm_mix, jnp.sum(db_gate, axis=0).reshape(1, 2 * D), d_rpb, d_norm_mlp, d_norm_final]
    return loss, grad_x, small


def _pack_small(parts, width):
    flat = jnp.concatenate([p.reshape(-1) for p in parts])
    return jnp.pad(flat, (0, 8 * width - flat.shape[0])).reshape(8, width)


def kernel(x, norm_mix, w_qkv, w_gate, b_gate, rpb, w_proj_a, w_proj_b, w_out, norm_mlp, w_up, w_down, norm_final, loss_target, m_norm_mix, m_w_qkv, m_w_gate, m_b_gate, m_rpb, m_w_proj_a, m_w_proj_b, m_w_out, m_norm_mlp, m_w_up, m_w_down, m_norm_final, v_norm_mix, v_w_qkv, v_w_gate, v_b_gate, v_rpb, v_w_proj_a, v_w_proj_b, v_w_out, v_norm_mlp, v_w_up, v_w_down, v_norm_final):
    names = ["qkv", "gate", "proj_a", "proj_b", "out", "up", "down"]
    big = dict(zip(names, [w_qkv[0], w_gate[0], w_proj_a[0], w_proj_b[0], w_out[0], w_up[0], w_down[0]]))
    big_m = dict(zip(names, [m_w_qkv[0], m_w_gate[0], m_w_proj_a[0], m_w_proj_b[0], m_w_out[0], m_w_up[0], m_w_down[0]]))
    big_v = dict(zip(names, [v_w_qkv[0], v_w_gate[0], v_w_proj_a[0], v_w_proj_b[0], v_w_out[0], v_w_up[0], v_w_down[0]]))

    c = lax.axis_index("c").astype(jnp.int32).reshape(1)
    me = (2 * lax.axis_index("x") + lax.axis_index("y")).astype(jnp.int32).reshape(1)
    ORDER.last = None
    ex = _Exchange(big, me, c)
    loss, grad_x, small = _forward_backward(x[0], loss_target[0], norm_mix, b_gate, rpb[0], norm_mlp, norm_final, ex)

    def adamw(group):
        return {n: _adamw(f"adamw_{n}", big[n], ex.grads[n], big_m[n], big_v[n]) for n in _Exchange.REDUCE[group]}

    big_out = {**adamw("mlp"), **adamw("mix")}
    ex.reduce("in")

    small_w = [norm_mix, b_gate, rpb, norm_mlp, norm_final]
    count = sum(int(np.prod(p.shape)) for p in small_w)
    width = -(-count // (8 * 128)) * 128
    packed = _adamw_small(_gather_small(_pack_small(small, width)), _pack_small(small_w, width),
                          _pack_small([m_norm_mix, m_b_gate, m_rpb, m_norm_mlp, m_norm_final], width),
                          _pack_small([v_norm_mix, v_b_gate, v_rpb, v_norm_mlp, v_norm_final], width))
    ex.reduce("in")
    big_out.update(adamw("in"))

    def unpack(flat2d):
        flat, out, at = flat2d.reshape(-1), [], 0
        for p in small_w:
            size = int(np.prod(p.shape))
            out.append(flat[at:at + size].reshape(p.shape))
            at += size
        return out

    small_out = [unpack(a) for a in packed]

    def ordered(kind):
        sm = small_out[kind]
        bg = {n: o[kind][None] for n, o in big_out.items()}
        return [sm[0], bg["qkv"], bg["gate"], sm[1], sm[2], bg["proj_a"], bg["proj_b"], bg["out"], sm[3],
                bg["up"], bg["down"], sm[4]]

    total = lax.psum(loss[0, 0], ("x", "y", "c"))
    return (total, grad_x[None], *ordered(0), *ordered(1), *ordered(2), *ordered(3))
```

```python
import functools
import math

import numpy as np
import jax
import jax.numpy as jnp
from jax import lax
from jax.experimental import pallas as pl
from jax.experimental.pallas import tpu as pltpu

BF = jnp.bfloat16
F32 = jnp.float32
MESH = pl.DeviceIdType.MESH

HEAD_DIM = 128
N_HEADS = 16
N_HEADS_A = 12
QKV_W = N_HEADS * HEAD_DIM
DILATIONS = (1, 4, 16)
HALF_WINDOW = 64
GRID_W = 64
NA_ROWS = 8
NA_COLS = 16
RPB_ROWS = 2 * NA_ROWS - 1
RPB_COLS = 2 * NA_COLS - 1
EPS = 1e-6
NEG = -1e30
SCALE = HEAD_DIM ** -0.5

ADAM_LR = 0.001
ADAM_B1 = 0.9
ADAM_B2 = 0.999
ADAM_EPS = 1e-08
ADAM_WD = 0.01
ADAM_STEP = 10

N_CHIPS = 4
VMEM_LIMIT_BYTES = 48 * 1024 * 1024
QB = 256
NBR_SIDE = 8
ROW_TILE = 512


def _key_rows(L):
    return min(QB + 2 * HALF_WINDOW, L)


def _cparams(sem=None):
    return pltpu.CompilerParams(dimension_semantics=sem, vmem_limit_bytes=VMEM_LIMIT_BYTES)


def _tile(dim, want):
    t = min(dim, want)
    assert dim % t == 0, (dim, want)
    return t


class _ProgramOrder:
    def __init__(self):
        self.last = None

    def call(self, body, operands, in_specs, *, prefetch=(), grid=None, out_specs=None, chain_output=0, **kwargs):
        operands, in_specs = list(operands), list(in_specs)
        lead = len(prefetch) + len(operands)
        if self.last is not None and not any(op is self.last for op in operands):
            operands.append(self.last)
            in_specs.append(pl.BlockSpec(memory_space=pl.ANY))
            inner = body

            def body(*refs):
                return inner(*refs[:lead], *refs[lead + 1:])

        if prefetch:
            kwargs["grid_spec"] = pltpu.PrefetchScalarGridSpec(
                num_scalar_prefetch=len(prefetch), grid=grid, in_specs=in_specs, out_specs=out_specs)
        else:
            kwargs.update(in_specs=in_specs, out_specs=out_specs)
            if grid is not None:
                kwargs["grid"] = grid
        out = pl.pallas_call(body, **kwargs)(*prefetch, *operands)
        self.last = out[chain_output] if isinstance(out, (tuple, list)) else out
        return out


ORDER = _ProgramOrder()


NN = ((1,), (0,))
NT = ((1,), (1,))
TN = ((0,), (0,))


def _matmul(name, a, b, a_spec, b_spec, dims, grid, acc_shape, extras, outs, epilogue, precision=None,
            prefetch=(), into=None):
    n_ex, n_out, nk = len(extras), len(outs), grid[2]
    n_in = 2 + n_ex + (into is not None)

    def body(*refs):
        refs = refs[len(prefetch):]
        a_ref, b_ref = refs[0], refs[1]
        ex_refs = refs[2:2 + n_ex]
        out_refs = refs[n_in:n_in + n_out]

        def dot():
            return lax.dot_general(a_ref[...], b_ref[...], (dims, ((), ())),
                                   preferred_element_type=F32, precision=precision)

        if nk == 1:
            epilogue(dot(), ex_refs, out_refs)
            return
        acc_ref = refs[-1]
        k = pl.program_id(2)

        @pl.when(k == 0)
        def _():
            acc_ref[...] = dot()

        if nk > 2:
            @pl.when((k > 0) & (k < nk - 1))
            def _():
                acc_ref[...] += dot()

        @pl.when(k == nk - 1)
        def _():
            epilogue(acc_ref[...] + dot(), ex_refs, out_refs)

    operands = [a, b] + [e for e, _ in extras]
    in_specs = [a_spec, b_spec] + [s for _, s in extras]
    kwargs = {}
    if into is not None:
        operands.append(into)
        in_specs.append(pl.BlockSpec(memory_space=pl.ANY))
        kwargs["input_output_aliases"] = {len(prefetch) + n_in - 1: 0}
    return ORDER.call(
        body, operands, in_specs, prefetch=prefetch, name=name, grid=grid,
        out_specs=[s for _, s in outs],
        out_shape=[sh for sh, _ in outs],
        scratch_shapes=[pltpu.VMEM(acc_shape, F32)] if nk > 1 else [],
        compiler_params=_cparams(("parallel", "parallel", "arbitrary")), **kwargs,
    )


def _mm_nn_shards(name, a, w, me, own, out, out_block, epilogue, extras=(), into=None, tn=512):
    M, K = a.shape
    Nq = w.shape[-1]
    tm, tn = _tile(M, 1024), _tile(Nq, tn)
    q = Nq // tn

    def tile(j, me_ref):
        shard = me_ref[0] if own else (me_ref[0] + 1 + j // q) % N_CHIPS
        return shard, j % q, shard * q + j % q

    if own:
        b_spec = pl.BlockSpec((K, tn), lambda i, j, k, me_ref: (0, j))
    else:
        b_spec = pl.BlockSpec((None, K, tn), lambda i, j, k, me_ref: (tile(j, me_ref)[0], 0, tile(j, me_ref)[1]))
    shape, dtype = out
    out_spec = pl.BlockSpec((None, tm, tn), lambda i, j, k, me_ref: out_block(i, tile(j, me_ref)[2]))
    ex = [(e, pl.BlockSpec((1, tn), lambda i, j, k, me_ref: (0, tile(j, me_ref)[2]))) for e in extras]
    return _matmul(name, a, w, pl.BlockSpec((tm, K), lambda i, j, k, me_ref: (i, 0)), b_spec, NN,
                   (M // tm, q if own else (N_CHIPS - 1) * q, 1), (tm, tn), ex, [(_sds(shape, dtype), out_spec)],
                   epilogue, prefetch=(me,), into=into)[0]


def _store(dtype):
    def epilogue(acc, ex, outs):
        outs[0][...] = acc.astype(dtype)
    return epilogue


def _sds(shape, dtype):
    return jax.ShapeDtypeStruct(shape, dtype)


def _mm_nn_cols(name, a, g, out_dtype, epilogue=None, extras=(), outs=None, tm=1024, tn=1024, tk=2048):
    M, K = a.shape
    _, _, Nq = g.shape
    tm, tn, tk = _tile(M, tm), _tile(Nq, tn), _tile(K, tk)
    q = Nq // tn
    grid = (M // tm, N_CHIPS * q, K // tk)
    if outs is None:
        outs = [(_sds((M, N_CHIPS * Nq), out_dtype), pl.BlockSpec((tm, tn), lambda i, j, k: (i, j)))]
    return _matmul(name, a, g, pl.BlockSpec((tm, tk), lambda i, j, k: (i, k)),
                   pl.BlockSpec((None, tk, tn), lambda i, j, k: (j // q, k, j % q)), NN, grid, (tm, tn),
                   list(extras), outs, epilogue or _store(out_dtype)), (tm, tn, tk)


def _rms_fwd(name, x, g):
    S, D = x.shape
    tm = _tile(S, ROW_TILE)

    def body(x_ref, g_ref, h_ref):
        xv = x_ref[...]
        r = lax.rsqrt(jnp.mean(xv * xv, axis=-1, keepdims=True) + EPS)
        h_ref[...] = ((xv * r) * g_ref[...]).astype(BF)

    row = pl.BlockSpec((tm, D), lambda i: (i, 0))
    return ORDER.call(
        body, [x, g], [row, pl.BlockSpec((1, D), lambda i: (0, 0))], name=name, grid=(S // tm,),
        out_specs=row, out_shape=_sds((S, D), BF), compiler_params=_cparams(("parallel",)),
    )


def _rms_bwd(name, dh, x, g, dres):
    S, D = x.shape
    tm = _tile(S, ROW_TILE // 2)

    def body(dh_ref, x_ref, g_ref, dres_ref, dx_ref, dxb_ref, dg_ref):
        xv = x_ref[...]
        r = lax.rsqrt(jnp.mean(xv * xv, axis=-1, keepdims=True) + EPS)
        n = xv * r
        dhv = dh_ref[...]
        dyg = dhv * g_ref[...]
        dx = dres_ref[...] + r * (dyg - n * jnp.mean(dyg * n, axis=-1, keepdims=True))
        dx_ref[...] = dx
        dxb_ref[...] = dx.astype(BF)

        @pl.when(pl.program_id(0) == 0)
        def _():
            dg_ref[...] = jnp.zeros_like(dg_ref)

        dg_ref[...] += jnp.sum(dhv * n, axis=0, keepdims=True)

    row = pl.BlockSpec((tm, D), lambda i: (i, 0))
    vec = pl.BlockSpec((1, D), lambda i: (0, 0))
    return ORDER.call(
        body, [dh, x, g, dres], [row, row, vec, row], name=name, grid=(S // tm,),
        out_specs=[row, row, vec],
        out_shape=[_sds((S, D), F32), _sds((S, D), BF), _sds((1, D), F32)],
        compiler_params=_cparams(("arbitrary",)),
    )


def _loss_head(x2, target, g):
    S, D = x2.shape
    tm = _tile(S, ROW_TILE)

    def body(x_ref, t_ref, g_ref, loss_ref, dx_ref, dxb_ref, dg_ref):
        xv = x_ref[...]
        gv = g_ref[...]
        r = lax.rsqrt(jnp.mean(xv * xv, axis=-1, keepdims=True) + EPS)
        n = xv * r
        e = n * gv - t_ref[...]
        dy = e * (1.0 / D)
        dyg = dy * gv
        dx = r * (dyg - n * jnp.mean(dyg * n, axis=-1, keepdims=True))
        dx_ref[...] = dx
        dxb_ref[...] = dx.astype(BF)

        @pl.when(pl.program_id(0) == 0)
        def _():
            dg_ref[...] = jnp.zeros_like(dg_ref)
            loss_ref[...] = jnp.zeros_like(loss_ref)

        dg_ref[...] += jnp.sum(dy * n, axis=0, keepdims=True)
        per_row = jnp.mean(e * e, axis=-1, keepdims=True)
        loss_ref[...] += 0.5 * jnp.sum(per_row, axis=0, keepdims=True)

    row = pl.BlockSpec((tm, D), lambda i: (i, 0))
    vec = pl.BlockSpec((1, D), lambda i: (0, 0))
    return ORDER.call(
        body, [x2, target, g], [row, row, vec], name="loss_head", grid=(S // tm,),
        out_specs=[pl.BlockSpec((1, 1), lambda i: (0, 0)), row, row, vec],
        out_shape=[_sds((1, 1), F32), _sds((S, D), F32), _sds((S, D), BF), _sds((1, D), F32)],
        compiler_params=_cparams(("arbitrary",)), chain_output=1,
    )


def _chains(L):
    side = min(4, L // QB)
    return side, 4 // side


def _band_scores(qkv_ref, i, L, coef, head):
    KB = _key_rows(L)
    lanes = pl.ds(head * HEAD_DIM, HEAD_DIM)
    q0 = pl.multiple_of(i * QB, QB)
    ks = pl.multiple_of(jnp.clip(i * QB - HALF_WINDOW, 0, L - KB), HALF_WINDOW)
    q = qkv_ref[0, pl.ds(q0, QB), lanes]
    k = qkv_ref[1, pl.ds(ks, KB), lanes]
    v = qkv_ref[2, pl.ds(ks, KB), lanes]
    s = lax.dot_general(q, k, (NT, ((), ())), preferred_element_type=F32) * SCALE
    qpos = q0 + lax.broadcasted_iota(jnp.int32, (QB, KB), 0)
    kpos = ks + lax.broadcasted_iota(jnp.int32, (QB, KB), 1)
    rel = jnp.abs(kpos - qpos)
    valid = rel <= HALF_WINDOW
    s = jnp.where(valid, s - coef * rel.astype(F32), NEG)
    return q0, ks, q, k, v, s, valid


def _alibi_coefs(group, d, heads):
    first = 4 * group + 1 + pl.program_id(1) * heads
    scale = jnp.full((1, 1), -(8.0 / N_HEADS_A) * math.log(2.0), F32)
    return [jnp.exp(scale * (first + hh).astype(F32)) * float(d) for hh in range(heads)]


def _dilated_view(qkv3, group, d, heads):
    per = 4 // heads
    L = qkv3.shape[1]
    if d == 1:
        return qkv3, pl.BlockSpec((3, L, heads * HEAD_DIM), lambda r, j: (0, 0, per * group + j))
    return qkv3, pl.BlockSpec((3, L, heads * HEAD_DIM), lambda r, j: (0, 0, r * per + j))


def _qkv_views(name, qkv3, views=None):
    _, S, _ = qkv3.shape
    W = 512
    tm = _tile(S, ROW_TILE)
    dilated = [(g, d) for g, d in enumerate(DILATIONS) if d > 1]
    first = dilated[0][0]
    assert [g for g, _ in dilated] == list(range(first, first + len(dilated)))
    nc = W // 128
    to_views = views is None

    def body(*refs):
        scr = refs[-nc:]
        if to_views:
            src, outs = refs[0], refs[1:1 + len(dilated)]
        else:
            ins, dst = refs[:len(dilated)], refs[len(dilated) + 1]
        for k, (_, d) in enumerate(dilated):
            @pl.when(pl.program_id(1) == k)
            def _():
                for w in range(3):
                    for c in range(nc):
                        if to_views:
                            scr[c][...] = src[w, :, c * 128:(c + 1) * 128].astype(F32)
                    for r in range(d):
                        for c in range(nc):
                            at = r * W + c * 128
                            if to_views:
                                outs[k][w, :, at:at + 128] = scr[c][pl.ds(r, tm // d, stride=d), :].astype(BF)
                            else:
                                scr[c][pl.ds(r, tm // d, stride=d), :] = ins[k][w, :, at:at + 128].astype(F32)
                    for c in range(nc):
                        if not to_views:
                            dst[w, :, c * 128:(c + 1) * 128] = scr[c][...].astype(BF)

    cols = pl.BlockSpec((3, tm, W), lambda i, k: (0, i, first + k))
    rows = [pl.BlockSpec((3, tm // d, d * W), lambda i, k: (0, i, 0)) for _, d in dilated]
    shapes = [_sds((3, S // d, d * W), BF) for _, d in dilated]
    common = dict(name=name, grid=(S // tm, len(dilated)), scratch_shapes=[pltpu.VMEM((tm, 128), F32)] * nc,
                  compiler_params=_cparams(("parallel", "arbitrary")))
    if to_views:
        outs = ORDER.call(body, [qkv3], [cols], out_specs=rows, out_shape=shapes, **common)
        return {d: o for (_, d), o in zip(dilated, outs)}
    return ORDER.call(body, [views[d] for _, d in dilated] + [qkv3], rows + [pl.BlockSpec(memory_space=pl.ANY)],
                      out_specs=cols, out_shape=_sds(qkv3.shape, BF), input_output_aliases={len(dilated): 0}, **common)


def _attn_a_fwd(qkv3, group, d):
    L = qkv3.shape[1]
    S = L * d
    assert L % QB == 0
    side, heads = _chains(L)
    view, blocks_spec = _dilated_view(qkv3, group, d, heads)

    def body(qkv_ref, o_ref, lse_ref):
        coefs = _alibi_coefs(group, d, heads)

        def step(i, carry):
            chains = [(hh, _band_scores(qkv_ref, side * i + u, L, coefs[hh], hh))
                      for u in range(side) for hh in range(heads)]
            soft = []
            for hh, (q0, _, _, _, v, s, _) in chains:
                m = jnp.max(s, axis=-1, keepdims=True)
                p = jnp.exp(s - m)
                den = jnp.sum(p, axis=-1, keepdims=True)
                soft.append((hh, q0, (p / den).astype(BF), v, m + jnp.log(den)))
            for hh, q0, pn, v, lse in soft:
                lanes = pl.ds(hh * HEAD_DIM, HEAD_DIM)
                o_ref[pl.ds(q0, QB), lanes] = jnp.dot(pn, v, preferred_element_type=F32)
                lse_ref[pl.ds(q0, QB), lanes] = jnp.broadcast_to(lse, (QB, HEAD_DIM))
            return carry

        lax.fori_loop(0, L // QB // side, step, 0)

    per = 4 // heads
    out = pl.BlockSpec((L, heads * HEAD_DIM), lambda r, j: (0, r * per + j))
    o, lse = ORDER.call(
        body, [view], [blocks_spec],
        name=f"attn_a_fwd_d{d}", grid=(d, per),
        out_specs=[out, out],
        out_shape=[_sds((L, d * 512), F32), _sds((L, d * 512), F32)],
        compiler_params=_cparams(("parallel", "parallel")),
    )
    return o, lse


def _dilated_rows(name, arrays):
    S, W = arrays[0].shape
    tm = _tile(S, ROW_TILE)
    ds_ = [d for d in DILATIONS if d > 1]
    n = len(arrays)

    def body(*refs):
        nc = W // 128
        ins, outs, scr = refs[:n], refs[n:-nc], refs[-nc:]
        for a, src in enumerate(ins):
            for c in range(nc):
                scr[c][...] = src[:, c * 128:(c + 1) * 128].astype(F32)
            for k, d in enumerate(ds_):
                dst = outs[a * len(ds_) + k]
                for r in range(d):
                    for c in range(nc):
                        at = r * W + c * 128
                        dst[:, at:at + 128] = scr[c][pl.ds(r, tm // d, stride=d), :].astype(dst.dtype)

    row = pl.BlockSpec((tm, W), lambda i: (i, 0))
    out_specs, out_shape = [], []
    for a in arrays:
        for d in ds_:
            out_specs.append(pl.BlockSpec((tm // d, d * W), lambda i: (i, 0)))
            out_shape.append(_sds((S // d, d * W), a.dtype))
    outs = ORDER.call(body, list(arrays), [row] * n, name=name, grid=(S // tm,), out_specs=out_specs,
                      out_shape=out_shape, scratch_shapes=[pltpu.VMEM((tm, 128), F32)] * (W // 128),
                      compiler_params=_cparams(("parallel",)))
    return [{d: outs[a * len(ds_) + k] for k, d in enumerate(ds_)} for a in range(n)]


def _attn_a_combine(os_, lses):
    W = 512
    S = os_[0].shape[0] * DILATIONS[0]
    tm = _tile(S, ROW_TILE)
    nc = W // 128
    dilated = [g for g, d in enumerate(DILATIONS) if d > 1]

    def body(o0, o1, o2, l0, l1, l2, y_ref, lj_ref, *scr):
        def token_order(src, g, slot):
            d = DILATIONS[g]
            if d == 1:
                return src[...]
            bufs = scr[slot * nc:(slot + 1) * nc]
            for r in range(d):
                for c in range(nc):
                    at = r * W + c * 128
                    bufs[c][pl.ds(r, tm // d, stride=d), :] = src[:, at:at + 128]
            return jnp.concatenate([buf[...] for buf in bufs], axis=1)

        slots = {g: k for k, g in enumerate(dilated)}
        ls = [token_order(l, g, slots.get(g, 0)) for g, l in enumerate((l0, l1, l2))]
        os_tok = [token_order(o, g, len(dilated) + slots.get(g, 0)) for g, o in enumerate((o0, o1, o2))]
        m = jnp.maximum(jnp.maximum(ls[0], ls[1]), ls[2])
        es = [jnp.exp(l - m) for l in ls]
        den = es[0] + es[1] + es[2]
        y = (es[0] / den) * os_tok[0] + (es[1] / den) * os_tok[1] + (es[2] / den) * os_tok[2]
        y_ref[...] = y.astype(BF)
        lj_ref[...] = m + jnp.log(den)

    row = pl.BlockSpec((tm, W), lambda i: (i, 0))
    views = [pl.BlockSpec((tm // d, d * W), lambda i: (i, 0)) for d in DILATIONS]
    return ORDER.call(
        body, [*os_, *lses], views + views, name="attn_a_combine", grid=(S // tm,), out_specs=[row, row],
        out_shape=[_sds((S, W), BF), _sds((S, W), F32)],
        scratch_shapes=[pltpu.VMEM((tm, 128), F32)] * (2 * len(dilated) * nc),
        compiler_params=_cparams(("parallel",)),
    )


def _attn_a_bwd(qkv3, dy, y, lj, dqkv3, group, d):
    L = qkv3.shape[1]
    S = L * d
    side, heads = _chains(L)
    view, blocks_spec = _dilated_view(qkv3, group, d, heads)

    def body(qkv_ref, dy_ref, y_ref, lj_ref, *rest):
        out_ref, dk_acc, dv_acc = rest[-3:]
        coefs = _alibi_coefs(group, d, heads)
        dk_acc[...] = jnp.zeros_like(dk_acc)
        dv_acc[...] = jnp.zeros_like(dv_acc)

        def step(i, carry):
            chains = [(pl.ds(hh * HEAD_DIM, HEAD_DIM), _band_scores(qkv_ref, side * i + u, L, coefs[hh], hh))
                      for u in range(side) for hh in range(heads)]
            dys = [dy_ref[pl.ds(c[0], QB), lanes] for lanes, c in chains]
            dps = [lax.dot_general(dyv, c[4], (NT, ((), ())), preferred_element_type=F32)
                   for dyv, (_, c) in zip(dys, chains)]
            grads = []
            for (lanes, (q0, ks, q, k, v, s, valid)), dyv, dp in zip(chains, dys, dps):
                rows = pl.ds(q0, QB)
                delta = jnp.sum(dyv.astype(F32) * y_ref[rows, lanes].astype(F32), axis=-1, keepdims=True)
                p = jnp.where(valid, jnp.exp(s - jnp.tile(lj_ref[rows, lanes], (1, _key_rows(L) // HEAD_DIM))), 0.0)
                grads.append(((p * (dp - delta)).astype(BF), p.astype(BF)))
            for (lanes, (q0, ks, q, k, v, s, valid)), dyv, (ds, pb) in zip(chains, dys, grads):
                out_ref[0, pl.ds(q0, QB), lanes] = (jnp.dot(ds, k, preferred_element_type=F32) * SCALE).astype(BF)
                keys = pl.ds(ks, _key_rows(L))
                dk_acc[keys, lanes] += lax.dot_general(ds, q, (TN, ((), ())), preferred_element_type=F32) * SCALE
                dv_acc[keys, lanes] += lax.dot_general(pb, dyv, (TN, ((), ())), preferred_element_type=F32)
            return carry

        lax.fori_loop(0, L // QB // side, step, 0)
        out_ref[1] = dk_acc[...].astype(BF)
        out_ref[2] = dv_acc[...].astype(BF)

    per = 4 // heads
    width = heads * HEAD_DIM
    row = pl.BlockSpec((L, width), lambda r, j: (0, r * per + j))
    operands = [view, dy, y, lj]
    scratch = [pltpu.VMEM((L, width), F32), pltpu.VMEM((L, width), F32)]
    if d == 1:
        return ORDER.call(
            body, operands + [dqkv3], [blocks_spec, row, row, row, pl.BlockSpec(memory_space=pl.ANY)],
            name=f"attn_a_bwd_d{d}", grid=(d, per), out_specs=blocks_spec, out_shape=_sds((3, S, QKV_W), BF),
            scratch_shapes=scratch, input_output_aliases={4: 0}, compiler_params=_cparams(("parallel", "parallel")))
    return ORDER.call(
        body, operands, [blocks_spec, row, row, row], name=f"attn_a_bwd_d{d}", grid=(d, per),
        out_specs=blocks_spec, out_shape=_sds((3, L, d * 512), BF),
        scratch_shapes=scratch, compiler_params=_cparams(("parallel", "parallel")))


def _toeplitz_onehot():
    oh = np.zeros((64, GRID_W, 128), np.float32)
    for qc in range(GRID_W):
        for m in range(128):
            kc = m % GRID_W
            dc = int(np.clip(kc - qc, -(NA_COLS - 1), NA_COLS - 1)) + NA_COLS - 1
            oh[(m // GRID_W) * 32 + dc, qc, m] = 1.0
    return oh.reshape(64, GRID_W * 128)


def _nbr_scores(qkv_ref, e2_ref, r, rows, ok):
    rs = jnp.clip(r - NA_ROWS // 2, 0, rows - NA_ROWS)
    q0 = pl.multiple_of(r * GRID_W, GRID_W)
    k0 = pl.multiple_of(rs * GRID_W, GRID_W)
    q = qkv_ref[0, pl.ds(q0, GRID_W), :]
    k = qkv_ref[1, pl.ds(k0, NA_ROWS * GRID_W), :]
    v = qkv_ref[2, pl.ds(k0, NA_ROWS * GRID_W), :]
    s = lax.dot_general(q, k, (NT, ((), ())), preferred_element_type=F32) * SCALE
    first = rs - r + NA_ROWS - 1
    bias = jnp.concatenate([e2_ref[first + 2 * pair] for pair in range(NA_ROWS // 2)], axis=1)
    s = jnp.where(ok, s + bias, NEG)
    return q0, k0, first, q, k, v, s


def _nbr_col_ok():
    qc = lax.broadcasted_iota(jnp.int32, (GRID_W, NA_ROWS * GRID_W), 0)
    kc = lax.broadcasted_iota(jnp.int32, (GRID_W, NA_ROWS * GRID_W), 1) % GRID_W
    cs = jnp.clip(qc - NA_COLS // 2, 0, GRID_W - NA_COLS)
    return (kc >= cs) & (kc < cs + NA_COLS)


def _attn_b_fwd(qkv3, e2):
    _, S, _ = qkv3.shape
    rows = S // GRID_W
    assert rows >= NA_ROWS

    def body(qkv_ref, e2_ref, o_ref, lse_ref):
        ok = _nbr_col_ok()

        def step(i, carry):
            blocks = [_nbr_scores(qkv_ref, e2_ref, NBR_SIDE * i + u, rows, ok) for u in range(NBR_SIDE)]
            soft = []
            for q0, _, _, _, _, v, s in blocks:
                m = jnp.max(s, axis=-1, keepdims=True)
                p = jnp.exp(s - m)
                den = jnp.sum(p, axis=-1, keepdims=True)
                soft.append((q0, (p / den).astype(BF), v, m + jnp.log(den)))
            for q0, pn, v, lse in soft:
                o_ref[pl.ds(q0, GRID_W), :] = jnp.dot(pn, v, preferred_element_type=F32).astype(BF)
                lse_ref[pl.ds(q0, GRID_W), :] = jnp.broadcast_to(lse, (GRID_W, HEAD_DIM))
            return carry

        lax.fori_loop(0, rows // NBR_SIDE, step, 0)

    out = pl.BlockSpec((S, HEAD_DIM), lambda h: (0, h))
    return ORDER.call(
        body, [qkv3, e2],
        [pl.BlockSpec((3, S, HEAD_DIM), lambda h: (0, 0, N_HEADS_A + h)),
         pl.BlockSpec((None, RPB_ROWS - 1, GRID_W, 128), lambda h: (h, 0, 0, 0))],
        name="attn_b_fwd", grid=(4,),
        out_specs=[out, out], out_shape=[_sds((S, 512), BF), _sds((S, 512), F32)],
        compiler_params=_cparams(("parallel",)),
    )


def _attn_b_bwd(qkv3, e2, dy, y, lse, dqkv3):
    _, S, _ = qkv3.shape
    rows = S // GRID_W
    nk = NA_ROWS * GRID_W

    def body(qkv_ref, e2_ref, dy_ref, y_ref, lse_ref, _, out_ref, de2_ref, dk_acc, dv_acc):
        ok = _nbr_col_ok()
        dk_acc[...] = jnp.zeros_like(dk_acc)
        dv_acc[...] = jnp.zeros_like(dv_acc)
        de2_ref[...] = jnp.zeros_like(de2_ref)

        def step(i, carry):
            blocks = [_nbr_scores(qkv_ref, e2_ref, NBR_SIDE * i + u, rows, ok) for u in range(NBR_SIDE)]
            dys = [dy_ref[pl.ds(b[0], GRID_W), :] for b in blocks]
            dps = [lax.dot_general(dyv, b[5], (NT, ((), ())), preferred_element_type=F32) for dyv, b in zip(dys, blocks)]
            grads = []
            for (q0, k0, first, q, k, v, s), dyv, dp in zip(blocks, dys, dps):
                qrows = pl.ds(q0, GRID_W)
                delta = jnp.sum(dyv.astype(F32) * y_ref[qrows, :].astype(F32), axis=-1, keepdims=True)
                p = jnp.where(ok, jnp.exp(s - jnp.tile(lse_ref[qrows, :], (1, nk // HEAD_DIM))), 0.0)
                ds = p * (dp - delta)
                for pair in range(NA_ROWS // 2):
                    de2_ref[first + 2 * pair] += ds[:, pair * 128:(pair + 1) * 128]
                grads.append((ds.astype(BF), p.astype(BF)))
            for (q0, k0, first, q, k, v, s), dyv, (dsb, pb) in zip(blocks, dys, grads):
                out_ref[0, pl.ds(q0, GRID_W), :] = (jnp.dot(dsb, k, preferred_element_type=F32) * SCALE).astype(BF)
                keys = pl.ds(k0, nk)
                dk_acc[keys, :] += lax.dot_general(dsb, q, (TN, ((), ())), preferred_element_type=F32) * SCALE
                dv_acc[keys, :] += lax.dot_general(pb, dyv, (TN, ((), ())), preferred_element_type=F32)
            return carry

        lax.fori_loop(0, rows // NBR_SIDE, step, 0)
        out_ref[1] = dk_acc[...].astype(BF)
        out_ref[2] = dv_acc[...].astype(BF)

    heads = pl.BlockSpec((3, S, HEAD_DIM), lambda h: (0, 0, N_HEADS_A + h))
    row = pl.BlockSpec((S, HEAD_DIM), lambda h: (0, h))
    table = pl.BlockSpec((None, RPB_ROWS - 1, GRID_W, 128), lambda h: (h, 0, 0, 0))
    return ORDER.call(
        body, [qkv3, e2, dy, y, lse, dqkv3],
        [heads, table, row, row, row, pl.BlockSpec(memory_space=pl.ANY)], name="attn_b_bwd", grid=(4,),
        out_specs=[heads, table],
        out_shape=[_sds((3, S, QKV_W), BF), _sds((4, RPB_ROWS - 1, GRID_W, 128), F32)],
        scratch_shapes=[pltpu.VMEM((S, HEAD_DIM), F32), pltpu.VMEM((S, HEAD_DIM), F32)],
        input_output_aliases={5: 0},
        compiler_params=_cparams(("parallel",)), chain_output=1,
    )


def _rpb_to_table(rpb):
    pad = jnp.pad(rpb, ((0, 0), (0, 0), (0, 1)))
    pairs = jnp.concatenate([pad[:, :-1], pad[:, 1:]], axis=-1).reshape(4 * (RPB_ROWS - 1), 64)
    onehot = jnp.asarray(_toeplitz_onehot())
    n = onehot.shape[1]
    tn = 2048
    full = lambda i, j, k: (0, 0)
    (e2,) = _matmul("rpb_table", pairs, onehot, pl.BlockSpec(pairs.shape, full),
                    pl.BlockSpec((64, tn), lambda i, j, k: (0, j)), NN, (1, n // tn, 1), (pairs.shape[0], tn), [],
                    [(_sds((pairs.shape[0], n), F32), pl.BlockSpec((pairs.shape[0], tn), lambda i, j, k: (0, j)))],
                    _store(F32), precision=lax.Precision.HIGHEST)
    return e2.reshape(4, RPB_ROWS - 1, GRID_W, 128)


def _table_grad_to_rpb(de2):
    onehot = jnp.asarray(_toeplitz_onehot())
    n = onehot.shape[1]
    flat = de2.reshape(4 * (RPB_ROWS - 1), n)
    tk = 2048
    (dpairs,) = _matmul("rpb_table_grad", flat, onehot, pl.BlockSpec((flat.shape[0], tk), lambda i, j, k: (0, k)),
                        pl.BlockSpec((64, tk), lambda i, j, k: (0, k)), NT, (1, 1, n // tk), (flat.shape[0], 64), [],
                        [(_sds((flat.shape[0], 64), F32), pl.BlockSpec((flat.shape[0], 64), lambda i, j, k: (0, 0)))],
                        _store(F32), precision=lax.Precision.HIGHEST)
    dpairs = dpairs.reshape(4, RPB_ROWS - 1, 64)
    zero = jnp.zeros((4, 1, RPB_COLS), F32)
    return (jnp.concatenate([dpairs[:, :, :RPB_COLS], zero], axis=1)
            + jnp.concatenate([zero, dpairs[:, :, 32:32 + RPB_COLS]], axis=1))


HBM = pl.BlockSpec(memory_space=pl.ANY)


def _place():
    x, y, c = lax.axis_index("x"), lax.axis_index("y"), lax.axis_index("c")
    chips = [(1 - x, y), (x, 1 - y), (1 - x, 1 - y)]
    return x, y, c, chips


def _remote(src, dst, send_sem, recv_sem, to):
    return pltpu.make_async_remote_copy(src_ref=src, dst_ref=dst, send_sem=send_sem, recv_sem=recv_sem,
                                        device_id=to, device_id_type=MESH)


def _place_shard(name, w, me, plain=False):
    R, C = w.shape
    tr = _tile(R, 256)

    def body(me_ref, w_ref, *o_refs):
        for o_ref in o_refs:
            o_ref[...] = w_ref[...].astype(BF)

    row = pl.BlockSpec((tr, C), lambda i, mr: (i, 0))
    placed = pl.BlockSpec((None, tr, C), lambda i, mr: (mr[0], i, 0))
    return ORDER.call(
        body, [w], [row], prefetch=(me,), name=name, grid=(R // tr,),
        out_specs=[placed, row] if plain else [placed],
        out_shape=[_sds((N_CHIPS, R, C), BF)] + ([_sds((R, C), BF)] if plain else []),
        compiler_params=_cparams(("parallel",)),
    )


SEM = pl.BlockSpec(memory_space=pltpu.SEMAPHORE)
IN_HBM = pl.BlockSpec(memory_space=pltpu.HBM)
DATAFLOW = pltpu.SideEffectType.DATAFLOW_SIDE_EFFECTING


def _in_hbm(a):
    return pltpu.with_memory_space_constraint(a, pltpu.HBM)


def _copy_start(name, bufs, copies, n_copies, earlier=None):
    n = len(bufs)
    after = None if any(b is ORDER.last for b in bufs) else ORDER.last
    n_extra = (2 if earlier is not None else 0) + (1 if after is not None else 0)

    def body(*refs):
        ins = refs[:n]
        if earlier is not None:
            for k, (src, dst, to) in enumerate(earlier[0](ins)):
                cp = _remote(src, dst, refs[n].at[k], refs[n + 1].at[k], to)
                cp.wait_send()
                cp.wait_recv()
        send_sems, recv_sems = refs[n + n_extra], refs[n + n_extra + 1]
        for k, (src, dst, to) in enumerate(copies(ins)):
            _remote(src, dst, send_sems.at[k], recv_sems.at[k], to).start()
        refs[-1][...] = jnp.zeros((8, 128), F32)

    operands = [_in_hbm(b) for b in bufs]
    in_specs = [IN_HBM] * n
    if earlier is not None:
        operands += [earlier[1], earlier[2]]
        in_specs += [SEM, SEM]
    if after is not None:
        operands.append(after)
        in_specs.append(HBM)
    outs = pl.pallas_call(
        body, name=name,
        out_shape=(pltpu.SemaphoreType.DMA((n_copies,)), pltpu.SemaphoreType.DMA((n_copies,)),
                   *[pltpu.HBM(b.shape, b.dtype) for b in bufs], _sds((8, 128), F32)),
        in_specs=in_specs,
        out_specs=(SEM, SEM, *[IN_HBM] * n, pl.BlockSpec(memory_space=pltpu.VMEM)),
        input_output_aliases={i: 2 + i for i in range(n)},
        compiler_params=pltpu.CompilerParams(has_side_effects=DATAFLOW),
    )(*operands)
    ORDER.last = outs[-1]
    return outs[0], outs[1], list(outs[2:2 + n])


def _copy_wait(name, bufs, copies, send_sems, recv_sems):
    n = len(bufs)
    after = ORDER.last

    def body(*refs):
        ins = refs[:n]
        for k, (src, dst, to) in enumerate(copies(ins)):
            cp = _remote(src, dst, refs[n].at[k], refs[n + 1].at[k], to)
            cp.wait_send()
            cp.wait_recv()

    outs = list(pl.pallas_call(
        body, name=name,
        out_shape=tuple(pltpu.HBM(b.shape, b.dtype) for b in bufs),
        in_specs=[IN_HBM] * n + [SEM, SEM, HBM], out_specs=tuple([IN_HBM] * n),
        input_output_aliases={i: i for i in range(n)},
        compiler_params=pltpu.CompilerParams(has_side_effects=DATAFLOW),
    )(*bufs, send_sems, recv_sems, after))
    ORDER.last = outs[0]
    return outs


def _gather_hop1(bufs):
    x, y, c, chips = _place()
    out = []
    for b in bufs:
        half = b.shape[1] // 2
        mine = b.at[2 * x + y, pl.ds(c * half, half), :]
        out += [(mine, mine, (*chip, c)) for chip in chips]
    return out


def _gather_hop2(bufs):
    x, y, c, chips = _place()
    out = []
    for b in bufs:
        half = b.shape[1] // 2
        for chip in chips:
            landed = b.at[2 * chip[0] + chip[1], pl.ds(c * half, half), :]
            out.append((landed, landed, (x, y, 1 - c)))
    return out


def _swap_copies(bufs):
    x, y, c, _ = _place()
    n = len(bufs) // 2
    out = []
    for p, land in zip(bufs[:n], bufs[n:]):
        half = p.shape[1] // 2
        out.append((p.at[:, pl.ds((1 - c) * half, half), :], land, (x, y, 1 - c)))
    return out


def _scatter_copies(bufs):
    _, _, c, chips = _place()
    n = len(bufs) // 2
    out = []
    for s_, land in zip(bufs[:n], bufs[n:]):
        out += [(s_.at[2 * chip[0] + chip[1]], land.at[j], (*chip, c)) for j, chip in enumerate(chips)]
    return out


def _join_copies(bufs):
    x, y, c, _ = _place()
    out = []
    for b in bufs:
        half = b.shape[0] // 2
        mine = b.at[pl.ds(c * half, half), :]
        out.append((mine, mine, (x, y, 1 - c)))
    return out


def _gather_small(vec):
    m_per, n = vec.shape

    def body(x_ref, out_ref, send_sems, recv_sems, local_sem):
        x, y, c, chips = _place()
        me, sibling = (x, y, c), (x, y, 1 - c)

        def rows(px, py, pc):
            return out_ref.at[pl.ds((4 * px + 2 * py + pc) * m_per, m_per), :]

        def copy(k, block, to, src=None):
            return _remote(rows(*block) if src is None else src, rows(*block), send_sems.at[k], recv_sems.at[k], to)

        mine = pltpu.make_async_copy(x_ref, rows(*me), local_sem)
        mine.start()
        first = [copy(0, me, sibling, src=x_ref)]
        first += [copy(1 + j, me, (*chip, c), src=x_ref) for j, chip in enumerate(chips)]
        for cp in first:
            cp.start()
        passed = [copy(4 + j, (*chip, c), sibling) for j, chip in enumerate(chips)]
        for j, chip in enumerate(chips):
            copy(1 + j, (*chip, c), me).wait_recv()
            passed[j].start()
        copy(0, sibling, me).wait_recv()
        for j, chip in enumerate(chips):
            copy(4 + j, (*chip, 1 - c), me).wait_recv()
        for cp in first + passed:
            cp.wait_send()
        mine.wait()

    return ORDER.call(
        body, [vec], [pl.BlockSpec(memory_space=pltpu.VMEM)], name="gather_small_grads",
        out_shape=_sds((8 * m_per, n), vec.dtype), out_specs=pl.BlockSpec(memory_space=pltpu.VMEM),
        scratch_shapes=[pltpu.SemaphoreType.DMA((7,)), pltpu.SemaphoreType.DMA((7,)), pltpu.SemaphoreType.DMA],
    )


def _add_sibling(name, partial, received, c):
    _, R, C = partial.shape
    half = R // 2
    tr = _tile(half, 256)
    nb = half // tr

    def body(c_ref, p_ref, r_ref, o_ref):
        o_ref[...] = (p_ref[...].astype(F32) + r_ref[...].astype(F32)).astype(BF)

    return ORDER.call(
        body, [partial, received],
        [pl.BlockSpec((None, tr, C), lambda j, i, cr: (j, cr[0] * nb + i, 0)),
         pl.BlockSpec((None, tr, C), lambda j, i, cr: (j, i, 0))],
        prefetch=(c,), name=name, grid=(N_CHIPS, nb),
        out_specs=pl.BlockSpec((None, tr, C), lambda j, i, cr: (j, i, 0)),
        out_shape=_sds((N_CHIPS, half, C), BF), compiler_params=_cparams(("parallel", "parallel")),
    )


def _add_chips(name, sums, received, me_c):
    _, half, C = sums.shape
    tr = _tile(half, 256)
    nb = half // tr

    def body(mc_ref, s_ref, r_ref, o_ref):
        acc = s_ref[...].astype(F32)
        for j in range(3):
            acc = acc + r_ref[j].astype(F32)
        o_ref[...] = acc

    return ORDER.call(
        body, [sums, received],
        [pl.BlockSpec((None, tr, C), lambda i, mc: (mc[0], i, 0)),
         pl.BlockSpec((3, tr, C), lambda i, mc: (0, i, 0))],
        prefetch=(me_c,), name=name, grid=(nb,),
        out_specs=pl.BlockSpec((tr, C), lambda i, mc: (mc[1] * nb + i, 0)),
        out_shape=_sds((2 * half, C), F32), compiler_params=_cparams(("parallel",)),
    )


def _adamw_math(w, g, m, v):
    m = ADAM_B1 * m + (1.0 - ADAM_B1) * g
    v = ADAM_B2 * v + (1.0 - ADAM_B2) * (g * g)
    m_hat = m / (1.0 - ADAM_B1 ** ADAM_STEP)
    v_hat = v / (1.0 - ADAM_B2 ** ADAM_STEP)
    delta = -ADAM_LR * (m_hat / (jnp.sqrt(v_hat) + ADAM_EPS) + ADAM_WD * w)
    return delta, m, v


def _adamw(name, w, g, m, v):
    R, C = w.shape
    tr = _tile(R, 256)

    def body(w_ref, g_ref, m_ref, v_ref, go_ref, d_ref, mo_ref, vo_ref):
        gv = g_ref[...]
        go_ref[...] = gv
        d_ref[...], mo_ref[...], vo_ref[...] = _adamw_math(w_ref[...], gv, m_ref[...], v_ref[...])

    row = pl.BlockSpec((tr, C), lambda i: (i, 0))
    return ORDER.call(
        body, [w, g, m, v], [row] * 4, name=name, grid=(R // tr,), out_specs=[row] * 4,
        out_shape=[_sds((R, C), F32)] * 4, compiler_params=_cparams(("parallel",)), chain_output=1,
    )


def _adamw_small(gathered, w, m, v):
    rows, n = w.shape

    def body(ga_ref, w_ref, m_ref, v_ref, go_ref, d_ref, mo_ref, vo_ref):
        g = ga_ref[pl.ds(0, rows), :]
        for dev in range(1, 8):
            g = g + ga_ref[pl.ds(dev * rows, rows), :]
        go_ref[...] = g
        d_ref[...], mo_ref[...], vo_ref[...] = _adamw_math(w_ref[...], g, m_ref[...], v_ref[...])

    whole = pl.BlockSpec(memory_space=pltpu.VMEM)
    return ORDER.call(
        body, [gathered, w, m, v], [whole] * 4, name="adamw_small", out_specs=[whole] * 4,
        out_shape=[_sds((rows, n), F32)] * 4, compiler_params=_cparams(), chain_output=1,
    )


def _proj_merge(y_a, y_b, gpa, gpb, g3):
    S, K = y_a.shape
    _, _, Nq = gpa.shape
    D = N_CHIPS * Nq
    tm, tn = _tile(S, 1024), _tile(Nq, 512)
    q = Nq // tn

    def body(ya_ref, yb_ref, wa_ref, wb_ref, g_ref, merged_ref, c_ref):
        pa = jnp.dot(ya_ref[...], wa_ref[...], preferred_element_type=F32)
        pb = jnp.dot(yb_ref[...], wb_ref[...], preferred_element_type=F32)
        g = g_ref[...].astype(F32)
        merged_ref[...] = (g[0] * pa + g[1] * pb).astype(BF)
        c_ref[0] = (pa * g[0] * (1.0 - g[0])).astype(BF)
        c_ref[1] = (pb * g[1] * (1.0 - g[1])).astype(BF)

    rows = pl.BlockSpec((tm, K), lambda i, j: (i, 0))
    weight = pl.BlockSpec((None, K, tn), lambda i, j: (j // q, 0, j % q))
    pair = pl.BlockSpec((2, tm, tn), lambda i, j: (0, i, j))
    return ORDER.call(
        body, [y_a, y_b, gpa, gpb, g3], [rows, rows, weight, weight, pair], name="proj_merge",
        grid=(S // tm, N_CHIPS * q), out_specs=[pl.BlockSpec((tm, tn), lambda i, j: (i, j)), pair],
        out_shape=[_sds((S, D), BF), _sds((2, S, D), BF)], compiler_params=_cparams(("parallel", "parallel")))


class _Exchange:
    GATHER = (("qkv",), ("gate",), ("proj_a", "proj_b", "out"), ("up",), ("down",))
    REDUCE = {"mlp": ("down", "up"), "mix": ("out", "proj_a", "proj_b"), "in": ("qkv", "gate")}

    OWN_FIRST = ("qkv", "gate")

    def __init__(self, shards, me, c):
        self.me, self.c = me, c
        self.hop1, self.hop2, self.stage, self.grads, self.own = {}, {}, {}, {}, {}
        for g, names in enumerate(self.GATHER):
            bufs = []
            for n in names:
                placed = _place_shard(f"place_{n}", shards[n], me, plain=n in self.OWN_FIRST)
                bufs.append(placed[0])
                if n in self.OWN_FIRST:
                    self.own[n] = placed[1]
            self.hop1[g] = _copy_start(f"gather{g}_start", bufs, _gather_hop1, 3 * len(names))

    def forward(self, g):
        send, recv, thru = self.hop1.pop(g)
        self.hop2[g] = _copy_start(f"gather{g}_forward", thru, _gather_hop2, len(thru) * 3,
                                   earlier=(_gather_hop1, send, recv))

    def weights(self, g):
        send, recv, thru = self.hop2.pop(g)
        return _copy_wait(f"gather{g}_wait", thru, _gather_hop2, send, recv)

    def reduce(self, key, partials=None):
        names = self.REDUCE[key]
        n = len(names)
        if partials is not None:
            lands = [lax.empty((p.shape[0], p.shape[1] // 2, p.shape[2]), p.dtype) for p in partials]
            self.stage[key] = ("swap",) + _copy_start(f"reduce_{key}_swap", list(partials) + lands, _swap_copies, n)
            return
        kind, send, recv, thru = self.stage.pop(key)
        if kind == "swap":
            thru = _copy_wait(f"reduce_{key}_swap_wait", thru, _swap_copies, send, recv)
            sums = [_add_sibling(f"reduce_{nm}_add_sibling", p, r, self.c)
                    for nm, p, r in zip(names, thru[:n], thru[n:])]
            lands = [lax.empty((3,) + s_.shape[1:], s_.dtype) for s_ in sums]
            self.stage[key] = ("scatter",) + _copy_start(f"reduce_{key}_scatter", sums + lands, _scatter_copies, 3 * n)
        elif kind == "scatter":
            thru = _copy_wait(f"reduce_{key}_scatter_wait", thru, _scatter_copies, send, recv)
            me_c = jnp.concatenate([self.me, self.c])
            halves = [_add_chips(f"reduce_{nm}_add_chips", s_, r, me_c)
                      for nm, s_, r in zip(names, thru[:n], thru[n:])]
            self.stage[key] = ("join",) + _copy_start(f"reduce_{key}_join", halves, _join_copies, n)
        else:
            thru = _copy_wait(f"reduce_{key}_join_wait", thru, _join_copies, send, recv)
            self.grads.update(zip(names, thru))


def _forward_backward(x, target, norm_mix, b_gate, rpb, norm_mlp, norm_final, ex):
    S, D = x.shape

    h1 = _rms_fwd("rms_mix", x, norm_mix)
    nq = QKV_W // 512
    qkv_out = (((3, S, QKV_W), BF), lambda i, T: (T // nq, i, T % nq))
    tg = _tile(ex.own["gate"].shape[1], 1024)
    ng = D // tg
    gate_out = (((2, S, D), BF), lambda i, T: (T // ng, i, T % ng))

    def gate_epilogue(acc, ex_, outs):
        outs[0][...] = jax.nn.sigmoid(acc + ex_[0][...]).astype(BF)

    qkv3 = _mm_nn_shards("qkv_own", h1, ex.own["qkv"], ex.me, True, *qkv_out, _store(BF))
    g3 = _mm_nn_shards("gate_own", h1, ex.own["gate"], ex.me, True, *gate_out, gate_epilogue, extras=[b_gate], tn=tg)
    ex.forward(0)
    e2 = _rpb_to_table(rpb)
    (gq,) = ex.weights(0)
    qkv3 = _mm_nn_shards("qkv", h1, gq, ex.me, False, *qkv_out, _store(BF), into=qkv3)

    ex.forward(1)
    outs_a = [_attn_a_fwd(qkv3, 0, DILATIONS[0])]
    (gg,) = ex.weights(1)
    g3 = _mm_nn_shards("gate", h1, gg, ex.me, False, *gate_out, gate_epilogue, extras=[b_gate], into=g3, tn=tg)

    ex.forward(2)
    qkv_views = _qkv_views("qkv_views", qkv3)
    outs_a += [_attn_a_fwd(qkv_views[d], grp, d) for grp, d in enumerate(DILATIONS) if grp > 0]
    y_a, lj = _attn_a_combine([o for o, _ in outs_a], [l for _, l in outs_a])
    y_b, lse_b = _attn_b_fwd(qkv3, e2)
    gpa, gpb, gout = ex.weights(2)
    wout = gout.reshape(D, D)
    merged, c3 = _proj_merge(y_a, y_b, gpa, gpb, g3)

    def residual_epilogue(acc, ex_, outs):
        outs[0][...] = acc + ex_[0][...]

    def nn_plain(name, a, w, res, bm=1024, bn=1024):
        M, K = a.shape
        N = w.shape[1]
        bm, bn, bk = _tile(M, bm), _tile(N, bn), _tile(K, 2048)
        t = pl.BlockSpec((bm, bn), lambda i, j, k: (i, j))
        return _matmul(name, a, w, pl.BlockSpec((bm, bk), lambda i, j, k: (i, k)),
                       pl.BlockSpec((bk, bn), lambda i, j, k: (k, j)), NN, (M // bm, N // bn, K // bk), (bm, bn),
                       [(res, t)], [(_sds((M, N), F32), t)], residual_epilogue)[0]

    ex.forward(3)
    x1 = nn_plain("out_proj", merged, wout, x, bm=512, bn=2048)
    h2 = _rms_fwd("rms_mlp", x1, norm_mlp)
    (gup,) = ex.weights(3)
    F = gup.shape[2] * N_CHIPS
    ex.forward(4)

    def up_epilogue(acc, ex_, outs):
        ru = jnp.maximum(acc, 0.0)
        outs[0][...] = (ru * ru).astype(BF)
        outs[1][...] = ru.astype(BF)

    tu = _tile(gup.shape[2], 1024)
    ut = pl.BlockSpec((_tile(S, 1024), tu), lambda i, j, k: (i, j))
    (act, ru), _ = _mm_nn_cols("mlp_up", h2, gup, BF, epilogue=up_epilogue, tn=tu,
                               outs=[(_sds((S, F), BF), ut), (_sds((S, F), BF), ut)])
    (gdown,) = ex.weights(4)
    wdown = gdown.reshape(F, D)
    x2 = nn_plain("mlp_down", act, wdown, x1)

    loss, dx2, dx2b, d_norm_final = _loss_head(x2, target, norm_final.reshape(1, D))

    def nt_rows(name, a, w, epilogue, extras, outs, bn=1024):
        M, N = a.shape
        K = w.shape[0]
        bm, bn, bk = _tile(M, 1024), _tile(K, bn), _tile(N, 2048)
        return _matmul(name, a, w, pl.BlockSpec((bm, bk), lambda i, j, k: (i, k)),
                       pl.BlockSpec((bn, bk), lambda i, j, k: (j, k)), NT, (M // bm, K // bn, N // bk), (bm, bn),
                       extras(bm, bn), outs(bm, bn), epilogue)

    def nt_cols(name, a_spec_fn, a, g, M, epilogue, extras, outs, bk, bn=1024):
        _, K, Nq = g.shape
        bm, bn, bk = _tile(M, 1024), _tile(K, bn), _tile(Nq, bk)
        q = Nq // bk
        return _matmul(name, a, g, a_spec_fn(bm, bk), pl.BlockSpec((None, bn, bk), lambda i, j, k: (k // q, j, k % q)),
                       NT, (M // bm, K // bn, N_CHIPS * q), (bm, bn), extras(bm, bn), outs(bm, bn), epilogue)

    def tn_grad(name, a, a_spec_fn, b, b_spec_fn, Kin, N, out_shape, out_spec_fn, bn=1024):
        bm, bn, bk = _tile(Kin, 1024), _tile(N, bn), _tile(S, 4096)
        return _matmul(name, a, b, a_spec_fn(bk, bm), b_spec_fn(bk, bn), TN, (Kin // bm, N // bn, S // bk), (bm, bn),
                       [], [(_sds(out_shape, BF), out_spec_fn(bm, bn))], _store(BF))[0]

    plain_a = lambda bk, bm: pl.BlockSpec((bk, bm), lambda i, j, k: (k, i))
    plain_b = lambda bk, bn: pl.BlockSpec((bk, bn), lambda i, j, k: (k, j))
    plain_o = lambda bm, bn: pl.BlockSpec((bm, bn), lambda i, j, k: (i, j))
    a_rows = lambda bm, bk: pl.BlockSpec((bm, bk), lambda i, j, k: (i, k))

    def cols_o(Nq):
        def spec(bm, bn):
            q = Nq // bn
            return pl.BlockSpec((None, bm, bn), lambda i, j, k: (j // q, i, j % q))
        return spec

    def du_epilogue(acc, ex_, outs):
        outs[0][...] = (acc * (2.0 * ex_[0][...].astype(F32))).astype(BF)

    dw_down = tn_grad("mlp_down_dw", act, plain_a, dx2b, plain_b, F, D, (F, D), plain_o)
    (du,) = nt_rows("mlp_down_dx", dx2b, wdown, du_epilogue,
                    lambda bm, bn: [(ru, plain_o(bm, bn))], lambda bm, bn: [(_sds((S, F), BF), plain_o(bm, bn))])

    fq = gup.shape[2]
    dw_up = tn_grad("mlp_up_dw", h2, plain_a, du, plain_b, D, F, (N_CHIPS, D, fq), cols_o(fq), bn=min(fq, 1024))
    ex.reduce("mlp", partials=[dw_down.reshape(N_CHIPS, F // N_CHIPS, D), dw_up])
    (dh2,) = nt_cols("mlp_up_dx", a_rows, du, gup, S, _store(F32), lambda bm, bn: [],
                     lambda bm, bn: [(_sds((S, D), F32), plain_o(bm, bn))], 1024, bn=2048)
    ex.reduce("mlp")
    dx1, dx1b, d_norm_mlp = _rms_bwd("rms_mlp_bwd", dh2, x1, norm_mlp, dx2)

    def merge_bwd_epilogue(acc, ex_, outs):
        g, c = ex_[0][...].astype(F32), ex_[1][...].astype(F32)
        outs[0][...] = (acc * g[0]).astype(BF)
        outs[1][...] = (acc * g[1]).astype(BF)
        dga = acc * c[0]
        dgb = acc * c[1]
        outs[2][0] = dga.astype(BF)
        outs[2][1] = dgb.astype(BF)
        outs[3][...] = jnp.concatenate([jnp.sum(dga, axis=0, keepdims=True), jnp.sum(dgb, axis=0, keepdims=True)], 0)

    def pair(bm, bn):
        return pl.BlockSpec((2, bm, bn), lambda i, j, k: (0, i, j))

    n_row_blocks = S // _tile(S, 1024)
    dpa, dpb, dg3, db_gate = nt_rows(
        "out_proj_dx", dx1b, wout, merge_bwd_epilogue,
        lambda bm, bn: [(g3, pair(bm, bn)), (c3, pair(bm, bn))],
        lambda bm, bn: [(_sds((S, D), BF), plain_o(bm, bn)), (_sds((S, D), BF), plain_o(bm, bn)),
                        (_sds((2, S, D), BF), pair(bm, bn)),
                        (_sds((n_row_blocks, 2, D), F32), pl.BlockSpec((None, 2, bn), lambda i, j, k: (i, 0, j)))],
        bn=512)
    dw_out = tn_grad("out_proj_dw", merged, plain_a, dx1b, plain_b, D, D, (D, D), plain_o)

    pq = gpa.shape[2]
    proj_dx = lambda name, dproj, g: nt_cols(name, a_rows, dproj, g, S, _store(BF), lambda bm, bn: [],
                                             lambda bm, bn: [(_sds((S, 512), BF), plain_o(bm, bn))], 512)[0]
    dw_pa = tn_grad("proj_a_dw", y_a, plain_a, dpa, plain_b, 512, D, (N_CHIPS, 512, pq), cols_o(pq), bn=min(pq, 512))
    dw_pb = tn_grad("proj_b_dw", y_b, plain_a, dpb, plain_b, 512, D, (N_CHIPS, 512, pq), cols_o(pq), bn=min(pq, 512))
    ex.reduce("mix", partials=[dw_out.reshape(N_CHIPS, D // N_CHIPS, D), dw_pa, dw_pb])
    dy_a = proj_dx("proj_a_dx", dpa, gpa)
    dy_b = proj_dx("proj_b_dx", dpb, gpb)

    dqkv3 = lax.empty((3, S, QKV_W), BF)
    dqkv3 = _attn_a_bwd(qkv3, dy_a, y_a, lj, dqkv3, 0, DILATIONS[0])
    ex.reduce("mix")
    dy_views, y_views, lj_views = _dilated_rows("attn_a_bwd_rows", [dy_a, y_a, lj])
    dqkv_views = {d: _attn_a_bwd(qkv_views[d], dy_views[d], y_views[d], lj_views[d], None, grp, d)
                  for grp, d in enumerate(DILATIONS) if grp > 0}
    dqkv3 = _qkv_views("dqkv_from_views", dqkv3, dqkv_views)
    dqkv3, de2 = _attn_b_bwd(qkv3, e2, dy_b, y_b, lse_b, dqkv3)
    d_rpb = _table_grad_to_rpb(de2)

    def stacked_a(width):
        def spec(bm, bk):
            q = width // bk
            return pl.BlockSpec((None, bm, bk), lambda i, j, k: (k // q, i, k % q))
        return spec

    def stacked_b(width):
        def spec(bk, bn):
            q = width // bn
            return pl.BlockSpec((None, bk, bn), lambda i, j, k: (j // q, k, j % q))
        return spec

    dw_qkv = tn_grad("qkv_dw", h1, plain_a, dqkv3, stacked_b(QKV_W), D, 3 * QKV_W, (N_CHIPS,) + gq.shape[1:],
                     cols_o(gq.shape[2]), bn=512)
    dw_gate = tn_grad("gate_dw", h1, plain_a, dg3, stacked_b(D), D, 2 * D, (N_CHIPS,) + gg.shape[1:],
                      cols_o(gg.shape[2]), bn=gg.shape[2])
    ex.reduce("in", partials=[dw_qkv, dw_gate])
    ex.reduce("mlp")
    (dh1_q,) = nt_cols("qkv_dx", stacked_a(QKV_W), dqkv3, gq, S, _store(F32), lambda bm, bn: [],
                       lambda bm, bn: [(_sds((S, D), F32), plain_o(bm, bn))], 512, bn=2048)
    ex.reduce("in")
    ex.reduce("mix")

    def add_epilogue(acc, ex_, outs):
        outs[0][...] = acc + ex_[0][...]

    (dh1,) = nt_cols("gate_dx", stacked_a(D), dg3, gg, S, add_epilogue, lambda bm, bn: [(dh1_q, plain_o(bm, bn))],
                     lambda bm, bn: [(_sds((S, D), F32), plain_o(bm, bn))], gg.shape[2])
    grad_x, _, d_norm_mix = _rms_bwd("rms_mix_bwd", dh1, x, norm_mix, dx1)
    ex.reduce("mlp")
    ex.reduce("mix")

    small = [d_norm_mix, jnp.sum(db_gate, axis=0).reshape(1, 2 * D), d_rpb, d_norm_mlp, d_norm_final]
    return loss, grad_x, small


def _pack_small(parts, width):
    flat = jnp.concatenate([p.reshape(-1) for p in parts])
    return jnp.pad(flat, (0, 8 * width - flat.shape[0])).reshape(8, width)


def kernel(x, norm_mix, w_qkv, w_gate, b_gate, rpb, w_proj_a, w_proj_b, w_out, norm_mlp, w_up, w_down, norm_final, loss_target, m_norm_mix, m_w_qkv, m_w_gate, m_b_gate, m_rpb, m_w_proj_a, m_w_proj_b, m_w_out, m_norm_mlp, m_w_up, m_w_down, m_norm_final, v_norm_mix, v_w_qkv, v_w_gate, v_b_gate, v_rpb, v_w_proj_a, v_w_proj_b, v_w_out, v_norm_mlp, v_w_up, v_w_down, v_norm_final):
    names = ["qkv", "gate", "proj_a", "proj_b", "out", "up", "down"]
    big = dict(zip(names, [w_qkv[0], w_gate[0], w_proj_a[0], w_proj_b[0], w_out[0], w_up[0], w_down[0]]))
    big_m = dict(zip(names, [m_w_qkv[0], m_w_gate[0], m_w_proj_a[0], m_w_proj_b[0], m_w_out[0], m_w_up[0], m_w_down[0]]))
    big_v = dict(zip(names, [v_w_qkv[0], v_w_gate[0], v_w_proj_a[0], v_w_proj_b[0], v_w_out[0], v_w_up[0], v_w_down[0]]))

    c = lax.axis_index("c").astype(jnp.int32).reshape(1)
    me = (2 * lax.axis_index("x") + lax.axis_index("y")).astype(jnp.int32).reshape(1)
    ORDER.last = None
    ex = _Exchange(big, me, c)
    loss, grad_x, small = _forward_backward(x[0], loss_target[0], norm_mix, b_gate, rpb[0], norm_mlp, norm_final, ex)

    def adamw(group):
        return {n: _adamw(f"adamw_{n}", big[n], ex.grads[n], big_m[n], big_v[n]) for n in _Exchange.REDUCE[group]}

    big_out = {**adamw("mlp"), **adamw("mix")}
    ex.reduce("in")

    small_w = [norm_mix, b_gate, rpb, norm_mlp, norm_final]
    count = sum(int(np.prod(p.shape)) for p in small_w)
    width = -(-count // (8 * 128)) * 128
    packed = _adamw_small(_gather_small(_pack_small(small, width)), _pack_small(small_w, width),
                          _pack_small([m_norm_mix, m_b_gate, m_rpb, m_norm_mlp, m_norm_final], width),
                          _pack_small([v_norm_mix, v_b_gate, v_rpb, v_norm_mlp, v_norm_final], width))
    ex.reduce("in")
    big_out.update(adamw("in"))

    def unpack(flat2d):
        flat, out, at = flat2d.reshape(-1), [], 0
        for p in small_w:
            size = int(np.prod(p.shape))
            out.append(flat[at:at + size].reshape(p.shape))
            at += size
        return out

    small_out = [unpack(a) for a in packed]

    def ordered(kind):
        sm = small_out[kind]
        bg = {n: o[kind][None] for n, o in big_out.items()}
        return [sm[0], bg["qkv"], bg["gate"], sm[1], sm[2], bg["proj_a"], bg["proj_b"], bg["out"], sm[3],
                bg["up"], bg["down"], sm[4]]

    total = lax.psum(loss[0, 0], ("x", "y", "c"))
    return (total, grad_x[None], *ordered(0), *ordered(1), *ordered(2), *ordered(3))
```

```python
import functools
import math

import numpy as np
import jax
import jax.numpy as jnp
from jax import lax
from jax.experimental import pallas as pl
from jax.experimental.pallas import tpu as pltpu

BF = jnp.bfloat16
F32 = jnp.float32
MESH = pl.DeviceIdType.MESH

HEAD_DIM = 128
N_HEADS = 16
N_HEADS_A = 12
QKV_W = N_HEADS * HEAD_DIM
DILATIONS = (1, 4, 16)
HALF_WINDOW = 64
GRID_W = 64
NA_ROWS = 8
NA_COLS = 16
RPB_ROWS = 2 * NA_ROWS - 1
RPB_COLS = 2 * NA_COLS - 1
EPS = 1e-6
NEG = -1e30
SCALE = HEAD_DIM ** -0.5

ADAM_LR = 0.001
ADAM_B1 = 0.9
ADAM_B2 = 0.999
ADAM_EPS = 1e-08
ADAM_WD = 0.01
ADAM_STEP = 10

N_CHIPS = 4
VMEM_LIMIT_BYTES = 48 * 1024 * 1024
QB = 256
NBR_SIDE = 8
ROW_TILE = 512


def _key_rows(L):
    return min(QB + 2 * HALF_WINDOW, L)


def _cparams(sem=None):
    return pltpu.CompilerParams(dimension_semantics=sem, vmem_limit_bytes=VMEM_LIMIT_BYTES)


def _tile(dim, want):
    t = min(dim, want)
    assert dim % t == 0, (dim, want)
    return t


class _ProgramOrder:
    def __init__(self):
        self.last = None

    def call(self, body, operands, in_specs, *, prefetch=(), grid=None, out_specs=None, chain_output=0, **kwargs):
        operands, in_specs = list(operands), list(in_specs)
        lead = len(prefetch) + len(operands)
        if self.last is not None and not any(op is self.last for op in operands):
            operands.append(self.last)
            in_specs.append(pl.BlockSpec(memory_space=pl.ANY))
            inner = body

            def body(*refs):
                return inner(*refs[:lead], *refs[lead + 1:])

        if prefetch:
            kwargs["grid_spec"] = pltpu.PrefetchScalarGridSpec(
                num_scalar_prefetch=len(prefetch), grid=grid, in_specs=in_specs, out_specs=out_specs)
        else:
            kwargs.update(in_specs=in_specs, out_specs=out_specs)
            if grid is not None:
                kwargs["grid"] = grid
        out = pl.pallas_call(body, **kwargs)(*prefetch, *operands)
        self.last = out[chain_output] if isinstance(out, (tuple, list)) else out
        return out


ORDER = _ProgramOrder()


NN = ((1,), (0,))
NT = ((1,), (1,))
TN = ((0,), (0,))


def _matmul(name, a, b, a_spec, b_spec, dims, grid, acc_shape, extras, outs, epilogue, precision=None,
            prefetch=(), into=None):
    n_ex, n_out, nk = len(extras), len(outs), grid[2]
    n_in = 2 + n_ex + (into is not None)

    def body(*refs):
        refs = refs[len(prefetch):]
        a_ref, b_ref = refs[0], refs[1]
        ex_refs = refs[2:2 + n_ex]
        out_refs = refs[n_in:n_in + n_out]

        def dot():
            return lax.dot_general(a_ref[...], b_ref[...], (dims, ((), ())),
                                   preferred_element_type=F32, precision=precision)

        if nk == 1:
            epilogue(dot(), ex_refs, out_refs)
            return
        acc_ref = refs[-1]
        k = pl.program_id(2)

        @pl.when(k == 0)
        def _():
            acc_ref[...] = dot()

        if nk > 2:
            @pl.when((k > 0) & (k < nk - 1))
            def _():
                acc_ref[...] += dot()

        @pl.when(k == nk - 1)
        def _():
            epilogue(acc_ref[...] + dot(), ex_refs, out_refs)

    operands = [a, b] + [e for e, _ in extras]
    in_specs = [a_spec, b_spec] + [s for _, s in extras]
    kwargs = {}
    if into is not None:
        operands.append(into)
        in_specs.append(pl.BlockSpec(memory_space=pl.ANY))
        kwargs["input_output_aliases"] = {len(prefetch) + n_in - 1: 0}
    return ORDER.call(
        body, operands, in_specs, prefetch=prefetch, name=name, grid=grid,
        out_specs=[s for _, s in outs],
        out_shape=[sh for sh, _ in outs],
        scratch_shapes=[pltpu.VMEM(acc_shape, F32)] if nk > 1 else [],
        compiler_params=_cparams(("parallel", "parallel", "arbitrary")), **kwargs,
    )


def _mm_nn_shards(name, a, w, me, own, out, out_block, epilogue, extras=(), into=None, tn=512):
    M, K = a.shape
    Nq = w.shape[-1]
    tm, tn = _tile(M, 1024), _tile(Nq, tn)
    q = Nq // tn

    def tile(j, me_ref):
        shard = me_ref[0] if own else (me_ref[0] + 1 + j // q) % N_CHIPS
        return shard, j % q, shard * q + j % q

    if own:
        b_spec = pl.BlockSpec((K, tn), lambda i, j, k, me_ref: (0, j))
    else:
        b_spec = pl.BlockSpec((None, K, tn), lambda i, j, k, me_ref: (tile(j, me_ref)[0], 0, tile(j, me_ref)[1]))
    shape, dtype = out
    out_spec = pl.BlockSpec((None, tm, tn), lambda i, j, k, me_ref: out_block(i, tile(j, me_ref)[2]))
    ex = [(e, pl.BlockSpec((1, tn), lambda i, j, k, me_ref: (0, tile(j, me_ref)[2]))) for e in extras]
    return _matmul(name, a, w, pl.BlockSpec((tm, K), lambda i, j, k, me_ref: (i, 0)), b_spec, NN,
                   (M // tm, q if own else (N_CHIPS - 1) * q, 1), (tm, tn), ex, [(_sds(shape, dtype), out_spec)],
                   epilogue, prefetch=(me,), into=into)[0]


def _store(dtype):
    def epilogue(acc, ex, outs):
        outs[0][...] = acc.astype(dtype)
    return epilogue


def _sds(shape, dtype):
    return jax.ShapeDtypeStruct(shape, dtype)


def _mm_nn_cols(name, a, g, out_dtype, epilogue=None, extras=(), outs=None, tm=1024, tn=1024, tk=2048):
    M, K = a.shape
    _, _, Nq = g.shape
    tm, tn, tk = _tile(M, tm), _tile(Nq, tn), _tile(K, tk)
    q = Nq // tn
    grid = (M // tm, N_CHIPS * q, K // tk)
    if outs is None:
        outs = [(_sds((M, N_CHIPS * Nq), out_dtype), pl.BlockSpec((tm, tn), lambda i, j, k: (i, j)))]
    return _matmul(name, a, g, pl.BlockSpec((tm, tk), lambda i, j, k: (i, k)),
                   pl.BlockSpec((None, tk, tn), lambda i, j, k: (j // q, k, j % q)), NN, grid, (tm, tn),
                   list(extras), outs, epilogue or _store(out_dtype)), (tm, tn, tk)


def _rms_fwd(name, x, g):
    S, D = x.shape
    tm = _tile(S, ROW_TILE)

    def body(x_ref, g_ref, h_ref):
        xv = x_ref[...]
        r = lax.rsqrt(jnp.mean(xv * xv, axis=-1, keepdims=True) + EPS)
        h_ref[...] = ((xv * r) * g_ref[...]).astype(BF)

    row = pl.BlockSpec((tm, D), lambda i: (i, 0))
    return ORDER.call(
        body, [x, g], [row, pl.BlockSpec((1, D), lambda i: (0, 0))], name=name, grid=(S // tm,),
        out_specs=row, out_shape=_sds((S, D), BF), compiler_params=_cparams(("parallel",)),
    )


def _rms_bwd(name, dh, x, g, dres):
    S, D = x.shape
    tm = _tile(S, ROW_TILE // 2)

    def body(dh_ref, x_ref, g_ref, dres_ref, dx_ref, dxb_ref, dg_ref):
        xv = x_ref[...]
        r = lax.rsqrt(jnp.mean(xv * xv, axis=-1, keepdims=True) + EPS)
        n = xv * r
        dhv = dh_ref[...]
        dyg = dhv * g_ref[...]
        dx = dres_ref[...] + r * (dyg - n * jnp.mean(dyg * n, axis=-1, keepdims=True))
        dx_ref[...] = dx
        dxb_ref[...] = dx.astype(BF)

        @pl.when(pl.program_id(0) == 0)
        def _():
            dg_ref[...] = jnp.zeros_like(dg_ref)

        dg_ref[...] += jnp.sum(dhv * n, axis=0, keepdims=True)

    row = pl.BlockSpec((tm, D), lambda i: (i, 0))
    vec = pl.BlockSpec((1, D), lambda i: (0, 0))
    return ORDER.call(
        body, [dh, x, g, dres], [row, row, vec, row], name=name, grid=(S // tm,),
        out_specs=[row, row, vec],
        out_shape=[_sds((S, D), F32), _sds((S, D), BF), _sds((1, D), F32)],
        compiler_params=_cparams(("arbitrary",)),
    )


def _loss_head(x2, target, g):
    S, D = x2.shape
    tm = _tile(S, ROW_TILE)

    def body(x_ref, t_ref, g_ref, loss_ref, dx_ref, dxb_ref, dg_ref):
        xv = x_ref[...]
        gv = g_ref[...]
        r = lax.rsqrt(jnp.mean(xv * xv, axis=-1, keepdims=True) + EPS)
        n = xv * r
        e = n * gv - t_ref[...]
        dy = e * (1.0 / D)
        dyg = dy * gv
        dx = r * (dyg - n * jnp.mean(dyg * n, axis=-1, keepdims=True))
        dx_ref[...] = dx
        dxb_ref[...] = dx.astype(BF)

        @pl.when(pl.program_id(0) == 0)
        def _():
            dg_ref[...] = jnp.zeros_like(dg_ref)
            loss_ref[...] = jnp.zeros_like(loss_ref)

        dg_ref[...] += jnp.sum(dy * n, axis=0, keepdims=True)
        per_row = jnp.mean(e * e, axis=-1, keepdims=True)
        loss_ref[...] += 0.5 * jnp.sum(per_row, axis=0, keepdims=True)

    row = pl.BlockSpec((tm, D), lambda i: (i, 0))
    vec = pl.BlockSpec((1, D), lambda i: (0, 0))
    return ORDER.call(
        body, [x2, target, g], [row, row, vec], name="loss_head", grid=(S // tm,),
        out_specs=[pl.BlockSpec((1, 1), lambda i: (0, 0)), row, row, vec],
        out_shape=[_sds((1, 1), F32), _sds((S, D), F32), _sds((S, D), BF), _sds((1, D), F32)],
        compiler_params=_cparams(("arbitrary",)), chain_output=1,
    )


def _chains(L):
    side = min(4, L // QB)
    return side, 4 // side


def _band_scores(qkv_ref, i, L, coef, head):
    KB = _key_rows(L)
    lanes = pl.ds(head * HEAD_DIM, HEAD_DIM)
    q0 = pl.multiple_of(i * QB, QB)
    ks = pl.multiple_of(jnp.clip(i * QB - HALF_WINDOW, 0, L - KB), HALF_WINDOW)
    q = qkv_ref[0, pl.ds(q0, QB), lanes]
    k = qkv_ref[1, pl.ds(ks, KB), lanes]
    v = qkv_ref[2, pl.ds(ks, KB), lanes]
    s = lax.dot_general(q, k, (NT, ((), ())), preferred_element_type=F32) * SCALE
    qpos = q0 + lax.broadcasted_iota(jnp.int32, (QB, KB), 0)
    kpos = ks + lax.broadcasted_iota(jnp.int32, (QB, KB), 1)
    rel = jnp.abs(kpos - qpos)
    valid = rel <= HALF_WINDOW
    s = jnp.where(valid, s - coef * rel.astype(F32), NEG)
    return q0, ks, q, k, v, s, valid


def _alibi_coefs(group, d, heads):
    first = 4 * group + 1 + pl.program_id(1) * heads
    scale = jnp.full((1, 1), -(8.0 / N_HEADS_A) * math.log(2.0), F32)
    return [jnp.exp(scale * (first + hh).astype(F32)) * float(d) for hh in range(heads)]


def _dilated_view(qkv3, group, d, heads):
    per = 4 // heads
    L = qkv3.shape[1]
    if d == 1:
        return qkv3, pl.BlockSpec((3, L, heads * HEAD_DIM), lambda r, j: (0, 0, per * group + j))
    return qkv3, pl.BlockSpec((3, L, heads * HEAD_DIM), lambda r, j: (0, 0, r * per + j))


def _qkv_views(name, qkv3, views=None):
    _, S, _ = qkv3.shape
    W = 512
    tm = _tile(S, ROW_TILE)
    dilated = [(g, d) for g, d in enumerate(DILATIONS) if d > 1]
    first = dilated[0][0]
    assert [g for g, _ in dilated] == list(range(first, first + len(dilated)))
    nc = W // 128
    to_views = views is None

    def body(*refs):
        scr = refs[-nc:]
        if to_views:
            src, outs = refs[0], refs[1:1 + len(dilated)]
        else:
            ins, dst = refs[:len(dilated)], refs[len(dilated) + 1]
        for k, (_, d) in enumerate(dilated):
            @pl.when(pl.program_id(1) == k)
            def _():
                for w in range(3):
                    for c in range(nc):
                        if to_views:
                            scr[c][...] = src[w, :, c * 128:(c + 1) * 128].astype(F32)
                    for r in range(d):
                        for c in range(nc):
                            at = r * W + c * 128
                            if to_views:
                                outs[k][w, :, at:at + 128] = scr[c][pl.ds(r, tm // d, stride=d), :].astype(BF)
                            else:
                                scr[c][pl.ds(r, tm // d, stride=d), :] = ins[k][w, :, at:at + 128].astype(F32)
                    for c in range(nc):
                        if not to_views:
                            dst[w, :, c * 128:(c + 1) * 128] = scr[c][...].astype(BF)

    cols = pl.BlockSpec((3, tm, W), lambda i, k: (0, i, first + k))
    rows = [pl.BlockSpec((3, tm // d, d * W), lambda i, k: (0, i, 0)) for _, d in dilated]
    shapes = [_sds((3, S // d, d * W), BF) for _, d in dilated]
    common = dict(name=name, grid=(S // tm, len(dilated)), scratch_shapes=[pltpu.VMEM((tm, 128), F32)] * nc,
                  compiler_params=_cparams(("parallel", "arbitrary")))
    if to_views:
        outs = ORDER.call(body, [qkv3], [cols], out_specs=rows, out_shape=shapes, **common)
        return {d: o for (_, d), o in zip(dilated, outs)}
    return ORDER.call(body, [views[d] for _, d in dilated] + [qkv3], rows + [pl.BlockSpec(memory_space=pl.ANY)],
                      out_specs=cols, out_shape=_sds(qkv3.shape, BF), input_output_aliases={len(dilated): 0}, **common)


def _attn_a_fwd(qkv3, group, d):
    L = qkv3.shape[1]
    S = L * d
    assert L % QB == 0
    side, heads = _chains(L)
    view, blocks_spec = _dilated_view(qkv3, group, d, heads)

    def body(qkv_ref, o_ref, lse_ref):
        coefs = _alibi_coefs(group, d, heads)

        def step(i, carry):
            chains = [(hh, _band_scores(qkv_ref, side * i + u, L, coefs[hh], hh))
                      for u in range(side) for hh in range(heads)]
            soft = []
            for hh, (q0, _, _, _, v, s, _) in chains:
                m = jnp.max(s, axis=-1, keepdims=True)
                p = jnp.exp(s - m)
                den = jnp.sum(p, axis=-1, keepdims=True)
                soft.append((hh, q0, (p / den).astype(BF), v, m + jnp.log(den)))
            for hh, q0, pn, v, lse in soft:
                lanes = pl.ds(hh * HEAD_DIM, HEAD_DIM)
                o_ref[pl.ds(q0, QB), lanes] = jnp.dot(pn, v, preferred_element_type=F32)
                lse_ref[pl.ds(q0, QB), lanes] = jnp.broadcast_to(lse, (QB, HEAD_DIM))
            return carry

        lax.fori_loop(0, L // QB // side, step, 0)

    per = 4 // heads
    out = pl.BlockSpec((L, heads * HEAD_DIM), lambda r, j: (0, r * per + j))
    o, lse = ORDER.call(
        body, [view], [blocks_spec],
        name=f"attn_a_fwd_d{d}", grid=(d, per),
        out_specs=[out, out],
        out_shape=[_sds((L, d * 512), F32), _sds((L, d * 512), F32)],
        compiler_params=_cparams(("parallel", "parallel")),
    )
    return o, lse


def _dilated_rows(name, arrays):
    S, W = arrays[0].shape
    tm = _tile(S, ROW_TILE)
    ds_ = [d for d in DILATIONS if d > 1]
    n = len(arrays)

    def body(*refs):
        nc = W // 128
        ins, outs, scr = refs[:n], refs[n:-nc], refs[-nc:]
        for a, src in enumerate(ins):
            for c in range(nc):
                scr[c][...] = src[:, c * 128:(c + 1) * 128].astype(F32)
            for k, d in enumerate(ds_):
                dst = outs[a * len(ds_) + k]
                for r in range(d):
                    for c in range(nc):
                        at = r * W + c * 128
                        dst[:, at:at + 128] = scr[c][pl.ds(r, tm // d, stride=d), :].astype(dst.dtype)

    row = pl.BlockSpec((tm, W), lambda i: (i, 0))
    out_specs, out_shape = [], []
    for a in arrays:
        for d in ds_:
            out_specs.append(pl.BlockSpec((tm // d, d * W), lambda i: (i, 0)))
            out_shape.append(_sds((S // d, d * W), a.dtype))
    outs = ORDER.call(body, list(arrays), [row] * n, name=name, grid=(S // tm,), out_specs=out_specs,
                      out_shape=out_shape, scratch_shapes=[pltpu.VMEM((tm, 128), F32)] * (W // 128),
                      compiler_params=_cparams(("parallel",)))
    return [{d: outs[a * len(ds_) + k] for k, d in enumerate(ds_)} for a in range(n)]


def _attn_a_combine(os_, lses):
    W = 512
    S = os_[0].shape[0] * DILATIONS[0]
    tm = _tile(S, ROW_TILE)
    nc = W // 128
    dilated = [g for g, d in enumerate(DILATIONS) if d > 1]

    def body(o0, o1, o2, l0, l1, l2, y_ref, lj_ref, *scr):
        def token_order(src, g, slot):
            d = DILATIONS[g]
            if d == 1:
                return src[...]
            bufs = scr[slot * nc:(slot + 1) * nc]
            for r in range(d):
                for c in range(nc):
                    at = r * W + c * 128
                    bufs[c][pl.ds(r, tm // d, stride=d), :] = src[:, at:at + 128]
            return jnp.concatenate([buf[...] for buf in bufs], axis=1)

        slots = {g: k for k, g in enumerate(dilated)}
        ls = [token_order(l, g, slots.get(g, 0)) for g, l in enumerate((l0, l1, l2))]
        os_tok = [token_order(o, g, len(dilated) + slots.get(g, 0)) for g, o in enumerate((o0, o1, o2))]
        m = jnp.maximum(jnp.maximum(ls[0], ls[1]), ls[2])
        es = [jnp.exp(l - m) for l in ls]
        den = es[0] + es[1] + es[2]
        y = (es[0] / den) * os_tok[0] + (es[1] / den) * os_tok[1] + (es[2] / den) * os_tok[2]
        y_ref[...] = y.astype(BF)
        lj_ref[...] = m + jnp.log(den)

    row = pl.BlockSpec((tm, W), lambda i: (i, 0))
    views = [pl.BlockSpec((tm // d, d * W), lambda i: (i, 0)) for d in DILATIONS]
    return ORDER.call(
        body, [*os_, *lses], views + views, name="attn_a_combine", grid=(S // tm,), out_specs=[row, row],
        out_shape=[_sds((S, W), BF), _sds((S, W), F32)],
        scratch_shapes=[pltpu.VMEM((tm, 128), F32)] * (2 * len(dilated) * nc),
        compiler_params=_cparams(("parallel",)),
    )


def _attn_a_bwd(qkv3, dy, y, lj, dqkv3, group, d):
    L = qkv3.shape[1]
    S = L * d
    side, heads = _chains(L)
    view, blocks_spec = _dilated_view(qkv3, group, d, heads)

    def body(qkv_ref, dy_ref, y_ref, lj_ref, *rest):
        out_ref, dk_acc, dv_acc = rest[-3:]
        coefs = _alibi_coefs(group, d, heads)
        dk_acc[...] = jnp.zeros_like(dk_acc)
        dv_acc[...] = jnp.zeros_like(dv_acc)

        def step(i, carry):
            chains = [(pl.ds(hh * HEAD_DIM, HEAD_DIM), _band_scores(qkv_ref, side * i + u, L, coefs[hh], hh))
                      for u in range(side) for hh in range(heads)]
            dys = [dy_ref[pl.ds(c[0], QB), lanes] for lanes, c in chains]
            dps = [lax.dot_general(dyv, c[4], (NT, ((), ())), preferred_element_type=F32)
                   for dyv, (_, c) in zip(dys, chains)]
            grads = []
            for (lanes, (q0, ks, q, k, v, s, valid)), dyv, dp in zip(chains, dys, dps):
                rows = pl.ds(q0, QB)
                delta = jnp.sum(dyv.astype(F32) * y_ref[rows, lanes].astype(F32), axis=-1, keepdims=True)
                p = jnp.where(valid, jnp.exp(s - jnp.tile(lj_ref[rows, lanes], (1, _key_rows(L) // HEAD_DIM))), 0.0)
                grads.append(((p * (dp - delta)).astype(BF), p.astype(BF)))
            for (lanes, (q0, ks, q, k, v, s, valid)), dyv, (ds, pb) in zip(chains, dys, grads):
                out_ref[0, pl.ds(q0, QB), lanes] = (jnp.dot(ds, k, preferred_element_type=F32) * SCALE).astype(BF)
                keys = pl.ds(ks, _key_rows(L))
                dk_acc[keys, lanes] += lax.dot_general(ds, q, (TN, ((), ())), preferred_element_type=F32) * SCALE
                dv_acc[keys, lanes] += lax.dot_general(pb, dyv, (TN, ((), ())), preferred_element_type=F32)
            return carry

        lax.fori_loop(0, L // QB // side, step, 0)
        out_ref[1] = dk_acc[...].astype(BF)
        out_ref[2] = dv_acc[...].astype(BF)

    per = 4 // heads
    width = heads * HEAD_DIM
    row = pl.BlockSpec((L, width), lambda r, j: (0, r * per + j))
    operands = [view, dy, y, lj]
    scratch = [pltpu.VMEM((L, width), F32), pltpu.VMEM((L, width), F32)]
    if d == 1:
        return ORDER.call(
            body, operands + [dqkv3], [blocks_spec, row, row, row, pl.BlockSpec(memory_space=pl.ANY)],
            name=f"attn_a_bwd_d{d}", grid=(d, per), out_specs=blocks_spec, out_shape=_sds((3, S, QKV_W), BF),
            scratch_shapes=scratch, input_output_aliases={4: 0}, compiler_params=_cparams(("parallel", "parallel")))
    return ORDER.call(
        body, operands, [blocks_spec, row, row, row], name=f"attn_a_bwd_d{d}", grid=(d, per),
        out_specs=blocks_spec, out_shape=_sds((3, L, d * 512), BF),
        scratch_shapes=scratch, compiler_params=_cparams(("parallel", "parallel")))


def _toeplitz_onehot():
    oh = np.zeros((64, GRID_W, 128), np.float32)
    for qc in range(GRID_W):
        for m in range(128):
            kc = m % GRID_W
            dc = int(np.clip(kc - qc, -(NA_COLS - 1), NA_COLS - 1)) + NA_COLS - 1
            oh[(m // GRID_W) * 32 + dc, qc, m] = 1.0
    return oh.reshape(64, GRID_W * 128)


def _nbr_scores(qkv_ref, e2_ref, r, rows, ok):
    rs = jnp.clip(r - NA_ROWS // 2, 0, rows - NA_ROWS)
    q0 = pl.multiple_of(r * GRID_W, GRID_W)
    k0 = pl.multiple_of(rs * GRID_W, GRID_W)
    q = qkv_ref[0, pl.ds(q0, GRID_W), :]
    k = qkv_ref[1, pl.ds(k0, NA_ROWS * GRID_W), :]
    v = qkv_ref[2, pl.ds(k0, NA_ROWS * GRID_W), :]
    s = lax.dot_general(q, k, (NT, ((), ())), preferred_element_type=F32) * SCALE
    first = rs - r + NA_ROWS - 1
    bias = jnp.concatenate([e2_ref[first + 2 * pair] for pair in range(NA_ROWS // 2)], axis=1)
    s = jnp.where(ok, s + bias, NEG)
    return q0, k0, first, q, k, v, s


def _nbr_col_ok():
    qc = lax.broadcasted_iota(jnp.int32, (GRID_W, NA_ROWS * GRID_W), 0)
    kc = lax.broadcasted_iota(jnp.int32, (GRID_W, NA_ROWS * GRID_W), 1) % GRID_W
    cs = jnp.clip(qc - NA_COLS // 2, 0, GRID_W - NA_COLS)
    return (kc >= cs) & (kc < cs + NA_COLS)


def _attn_b_fwd(qkv3, e2):
    _, S, _ = qkv3.shape
    rows = S // GRID_W
    assert rows >= NA_ROWS

    def body(qkv_ref, e2_ref, o_ref, lse_ref):
        ok = _nbr_col_ok()

        def step(i, carry):
            blocks = [_nbr_scores(qkv_ref, e2_ref, NBR_SIDE * i + u, rows, ok) for u in range(NBR_SIDE)]
            soft = []
            for q0, _, _, _, _, v, s in blocks:
                m = jnp.max(s, axis=-1, keepdims=True)
                p = jnp.exp(s - m)
                den = jnp.sum(p, axis=-1, keepdims=True)
                soft.append((q0, (p / den).astype(BF), v, m + jnp.log(den)))
            for q0, pn, v, lse in soft:
                o_ref[pl.ds(q0, GRID_W), :] = jnp.dot(pn, v, preferred_element_type=F32).astype(BF)
                lse_ref[pl.ds(q0, GRID_W), :] = jnp.broadcast_to(lse, (GRID_W, HEAD_DIM))
            return carry

        lax.fori_loop(0, rows // NBR_SIDE, step, 0)

    out = pl.BlockSpec((S, HEAD_DIM), lambda h: (0, h))
    return ORDER.call(
        body, [qkv3, e2],
        [pl.BlockSpec((3, S, HEAD_DIM), lambda h: (0, 0, N_HEADS_A + h)),
         pl.BlockSpec((None, RPB_ROWS - 1, GRID_W, 128), lambda h: (h, 0, 0, 0))],
        name="attn_b_fwd", grid=(4,),
        out_specs=[out, out], out_shape=[_sds((S, 512), BF), _sds((S, 512), F32)],
        compiler_params=_cparams(("parallel",)),
    )


def _attn_b_bwd(qkv3, e2, dy, y, lse, dqkv3):
    _, S, _ = qkv3.shape
    rows = S // GRID_W
    nk = NA_ROWS * GRID_W

    def body(qkv_ref, e2_ref, dy_ref, y_ref, lse_ref, _, out_ref, de2_ref, dk_acc, dv_acc):
        ok = _nbr_col_ok()
        dk_acc[...] = jnp.zeros_like(dk_acc)
        dv_acc[...] = jnp.zeros_like(dv_acc)
        de2_ref[...] = jnp.zeros_like(de2_ref)

        def step(i, carry):
            blocks = [_nbr_scores(qkv_ref, e2_ref, NBR_SIDE * i + u, rows, ok) for u in range(NBR_SIDE)]
            dys = [dy_ref[pl.ds(b[0], GRID_W), :] for b in blocks]
            dps = [lax.dot_general(dyv, b[5], (NT, ((), ())), preferred_element_type=F32) for dyv, b in zip(dys, blocks)]
            grads = []
            for (q0, k0, first, q, k, v, s), dyv, dp in zip(blocks, dys, dps):
                qrows = pl.ds(q0, GRID_W)
                delta = jnp.sum(dyv.astype(F32) * y_ref[qrows, :].astype(F32), axis=-1, keepdims=True)
                p = jnp.where(ok, jnp.exp(s - jnp.tile(lse_ref[qrows, :], (1, nk // HEAD_DIM))), 0.0)
                ds = p * (dp - delta)
                for pair in range(NA_ROWS // 2):
                    de2_ref[first + 2 * pair] += ds[:, pair * 128:(pair + 1) * 128]
                grads.append((ds.astype(BF), p.astype(BF)))
            for (q0, k0, first, q, k, v, s), dyv, (dsb, pb) in zip(blocks, dys, grads):
                out_ref[0, pl.ds(q0, GRID_W), :] = (jnp.dot(dsb, k, preferred_element_type=F32) * SCALE).astype(BF)
                keys = pl.ds(k0, nk)
                dk_acc[keys, :] += lax.dot_general(dsb, q, (TN, ((), ())), preferred_element_type=F32) * SCALE
                dv_acc[keys, :] += lax.dot_general(pb, dyv, (TN, ((), ())), preferred_element_type=F32)
            return carry

        lax.fori_loop(0, rows // NBR_SIDE, step, 0)
        out_ref[1] = dk_acc[...].astype(BF)
        out_ref[2] = dv_acc[...].astype(BF)

    heads = pl.BlockSpec((3, S, HEAD_DIM), lambda h: (0, 0, N_HEADS_A + h))
    row = pl.BlockSpec((S, HEAD_DIM), lambda h: (0, h))
    table = pl.BlockSpec((None, RPB_ROWS - 1, GRID_W, 128), lambda h: (h, 0, 0, 0))
    return ORDER.call(
        body, [qkv3, e2, dy, y, lse, dqkv3],
        [heads, table, row, row, row, pl.BlockSpec(memory_space=pl.ANY)], name="attn_b_bwd", grid=(4,),
        out_specs=[heads, table],
        out_shape=[_sds((3, S, QKV_W), BF), _sds((4, RPB_ROWS - 1, GRID_W, 128), F32)],
        scratch_shapes=[pltpu.VMEM((S, HEAD_DIM), F32), pltpu.VMEM((S, HEAD_DIM), F32)],
        input_output_aliases={5: 0},
        compiler_params=_cparams(("parallel",)), chain_output=1,
    )


def _rpb_to_table(rpb):
    pad = jnp.pad(rpb, ((0, 0), (0, 0), (0, 1)))
    pairs = jnp.concatenate([pad[:, :-1], pad[:, 1:]], axis=-1).reshape(4 * (RPB_ROWS - 1), 64)
    onehot = jnp.asarray(_toeplitz_onehot())
    n = onehot.shape[1]
    tn = 2048
    full = lambda i, j, k: (0, 0)
    (e2,) = _matmul("rpb_table", pairs, onehot, pl.BlockSpec(pairs.shape, full),
                    pl.BlockSpec((64, tn), lambda i, j, k: (0, j)), NN, (1, n // tn, 1), (pairs.shape[0], tn), [],
                    [(_sds((pairs.shape[0], n), F32), pl.BlockSpec((pairs.shape[0], tn), lambda i, j, k: (0, j)))],
                    _store(F32), precision=lax.Precision.HIGHEST)
    return e2.reshape(4, RPB_ROWS - 1, GRID_W, 128)


def _table_grad_to_rpb(de2):
    onehot = jnp.asarray(_toeplitz_onehot())
    n = onehot.shape[1]
    flat = de2.reshape(4 * (RPB_ROWS - 1), n)
    tk = 2048
    (dpairs,) = _matmul("rpb_table_grad", flat, onehot, pl.BlockSpec((flat.shape[0], tk), lambda i, j, k: (0, k)),
                        pl.BlockSpec((64, tk), lambda i, j, k: (0, k)), NT, (1, 1, n // tk), (flat.shape[0], 64), [],
                        [(_sds((flat.shape[0], 64), F32), pl.BlockSpec((flat.shape[0], 64), lambda i, j, k: (0, 0)))],
                        _store(F32), precision=lax.Precision.HIGHEST)
    dpairs = dpairs.reshape(4, RPB_ROWS - 1, 64)
    zero = jnp.zeros((4, 1, RPB_COLS), F32)
    return (jnp.concatenate([dpairs[:, :, :RPB_COLS], zero], axis=1)
            + jnp.concatenate([zero, dpairs[:, :, 32:32 + RPB_COLS]], axis=1))


HBM = pl.BlockSpec(memory_space=pl.ANY)


def _place():
    x, y, c = lax.axis_index("x"), lax.axis_index("y"), lax.axis_index("c")
    chips = [(1 - x, y), (x, 1 - y), (1 - x, 1 - y)]
    return x, y, c, chips


def _remote(src, dst, send_sem, recv_sem, to):
    return pltpu.make_async_remote_copy(src_ref=src, dst_ref=dst, send_sem=send_sem, recv_sem=recv_sem,
                                        device_id=to, device_id_type=MESH)


def _place_shard(name, w, me, plain=False):
    R, C = w.shape
    tr = _tile(R, 256)

    def body(me_ref, w_ref, *o_refs):
        for o_ref in o_refs:
            o_ref[...] = w_ref[...].astype(BF)

    row = pl.BlockSpec((tr, C), lambda i, mr: (i, 0))
    placed = pl.BlockSpec((None, tr, C), lambda i, mr: (mr[0], i, 0))
    return ORDER.call(
        body, [w], [row], prefetch=(me,), name=name, grid=(R // tr,),
        out_specs=[placed, row] if plain else [placed],
        out_shape=[_sds((N_CHIPS, R, C), BF)] + ([_sds((R, C), BF)] if plain else []),
        compiler_params=_cparams(("parallel",)),
    )


SEM = pl.BlockSpec(memory_space=pltpu.SEMAPHORE)
IN_HBM = pl.BlockSpec(memory_space=pltpu.HBM)
DATAFLOW = pltpu.SideEffectType.DATAFLOW_SIDE_EFFECTING


def _in_hbm(a):
    return pltpu.with_memory_space_constraint(a, pltpu.HBM)


def _copy_start(name, bufs, copies, n_copies, earlier=None):
    n = len(bufs)
    after = None if any(b is ORDER.last for b in bufs) else ORDER.last
    n_extra = (2 if earlier is not None else 0) + (1 if after is not None else 0)

    def body(*refs):
        ins = refs[:n]
        if earlier is not None:
            for k, (src, dst, to) in enumerate(earlier[0](ins)):
                cp = _remote(src, dst, refs[n].at[k], refs[n + 1].at[k], to)
                cp.wait_send()
                cp.wait_recv()
        send_sems, recv_sems = refs[n + n_extra], refs[n + n_extra + 1]
        for k, (src, dst, to) in enumerate(copies(ins)):
            _remote(src, dst, send_sems.at[k], recv_sems.at[k], to).start()
        refs[-1][...] = jnp.zeros((8, 128), F32)

    operands = [_in_hbm(b) for b in bufs]
    in_specs = [IN_HBM] * n
    if earlier is not None:
        operands += [earlier[1], earlier[2]]
        in_specs += [SEM, SEM]
    if after is not None:
        operands.append(after)
        in_specs.append(HBM)
    outs = pl.pallas_call(
        body, name=name,
        out_shape=(pltpu.SemaphoreType.DMA((n_copies,)), pltpu.SemaphoreType.DMA((n_copies,)),
                   *[pltpu.HBM(b.shape, b.dtype) for b in bufs], _sds((8, 128), F32)),
        in_specs=in_specs,
        out_specs=(SEM, SEM, *[IN_HBM] * n, pl.BlockSpec(memory_space=pltpu.VMEM)),
        input_output_aliases={i: 2 + i for i in range(n)},
        compiler_params=pltpu.CompilerParams(has_side_effects=DATAFLOW),
    )(*operands)
    ORDER.last = outs[-1]
    return outs[0], outs[1], list(outs[2:2 + n])


def _copy_wait(name, bufs, copies, send_sems, recv_sems):
    n = len(bufs)
    after = ORDER.last

    def body(*refs):
        ins = refs[:n]
        for k, (src, dst, to) in enumerate(copies(ins)):
            cp = _remote(src, dst, refs[n].at[k], refs[n + 1].at[k], to)
            cp.wait_send()
            cp.wait_recv()

    outs = list(pl.pallas_call(
        body, name=name,
        out_shape=tuple(pltpu.HBM(b.shape, b.dtype) for b in bufs),
        in_specs=[IN_HBM] * n + [SEM, SEM, HBM], out_specs=tuple([IN_HBM] * n),
        input_output_aliases={i: i for i in range(n)},
        compiler_params=pltpu.CompilerParams(has_side_effects=DATAFLOW),
    )(*bufs, send_sems, recv_sems, after))
    ORDER.last = outs[0]
    return outs


def _gather_hop1(bufs):
    x, y, c, chips = _place()
    out = []
    for b in bufs:
        half = b.shape[1] // 2
        mine = b.at[2 * x + y, pl.ds(c * half, half), :]
        out += [(mine, mine, (*chip, c)) for chip in chips]
    return out


def _gather_hop2(bufs):
    x, y, c, chips = _place()
    out = []
    for b in bufs:
        half = b.shape[1] // 2
        for chip in chips:
            landed = b.at[2 * chip[0] + chip[1], pl.ds(c * half, half), :]
            out.append((landed, landed, (x, y, 1 - c)))
    return out


def _swap_copies(bufs):
    x, y, c, _ = _place()
    n = len(bufs) // 2
    out = []
    for p, land in zip(bufs[:n], bufs[n:]):
        half = p.shape[1] // 2
        out.append((p.at[:, pl.ds((1 - c) * half, half), :], land, (x, y, 1 - c)))
    return out


def _scatter_copies(bufs):
    _, _, c, chips = _place()
    n = len(bufs) // 2
    out = []
    for s_, land in zip(bufs[:n], bufs[n:]):
        out += [(s_.at[2 * chip[0] + chip[1]], land.at[j], (*chip, c)) for j, chip in enumerate(chips)]
    return out


def _join_copies(bufs):
    x, y, c, _ = _place()
    out = []
    for b in bufs:
        half = b.shape[0] // 2
        mine = b.at[pl.ds(c * half, half), :]
        out.append((mine, mine, (x, y, 1 - c)))
    return out


def _gather_small(vec):
    m_per, n = vec.shape

    def body(x_ref, out_ref, send_sems, recv_sems, local_sem):
        x, y, c, chips = _place()
        me, sibling = (x, y, c), (x, y, 1 - c)

        def rows(px, py, pc):
            return out_ref.at[pl.ds((4 * px + 2 * py + pc) * m_per, m_per), :]

        def copy(k, block, to, src=None):
            return _remote(rows(*block) if src is None else src, rows(*block), send_sems.at[k], recv_sems.at[k], to)

        mine = pltpu.make_async_copy(x_ref, rows(*me), local_sem)
        mine.start()
        first = [copy(0, me, sibling, src=x_ref)]
        first += [copy(1 + j, me, (*chip, c), src=x_ref) for j, chip in enumerate(chips)]
        for cp in first:
            cp.start()
        passed = [copy(4 + j, (*chip, c), sibling) for j, chip in enumerate(chips)]
        for j, chip in enumerate(chips):
            copy(1 + j, (*chip, c), me).wait_recv()
            passed[j].start()
        copy(0, sibling, me).wait_recv()
        for j, chip in enumerate(chips):
            copy(4 + j, (*chip, 1 - c), me).wait_recv()
        for cp in first + passed:
            cp.wait_send()
        mine.wait()

    return ORDER.call(
        body, [vec], [pl.BlockSpec(memory_space=pltpu.VMEM)], name="gather_small_grads",
        out_shape=_sds((8 * m_per, n), vec.dtype), out_specs=pl.BlockSpec(memory_space=pltpu.VMEM),
        scratch_shapes=[pltpu.SemaphoreType.DMA((7,)), pltpu.SemaphoreType.DMA((7,)), pltpu.SemaphoreType.DMA],
    )


def _add_sibling(name, partial, received, c):
    _, R, C = partial.shape
    half = R // 2
    tr = _tile(half, 256)
    nb = half // tr

    def body(c_ref, p_ref, r_ref, o_ref):
        o_ref[...] = (p_ref[...].astype(F32) + r_ref[...].astype(F32)).astype(BF)

    return ORDER.call(
        body, [partial, received],
        [pl.BlockSpec((None, tr, C), lambda j, i, cr: (j, cr[0] * nb + i, 0)),
         pl.BlockSpec((None, tr, C), lambda j, i, cr: (j, i, 0))],
        prefetch=(c,), name=name, grid=(N_CHIPS, nb),
        out_specs=pl.BlockSpec((None, tr, C), lambda j, i, cr: (j, i, 0)),
        out_shape=_sds((N_CHIPS, half, C), BF), compiler_params=_cparams(("parallel", "parallel")),
    )


def _add_chips(name, sums, received, me_c):
    _, half, C = sums.shape
    tr = _tile(half, 256)
    nb = half // tr

    def body(mc_ref, s_ref, r_ref, o_ref):
        acc = s_ref[...].astype(F32)
        for j in range(3):
            acc = acc + r_ref[j].astype(F32)
        o_ref[...] = acc

    return ORDER.call(
        body, [sums, received],
        [pl.BlockSpec((None, tr, C), lambda i, mc: (mc[0], i, 0)),
         pl.BlockSpec((3, tr, C), lambda i, mc: (0, i, 0))],
        prefetch=(me_c,), name=name, grid=(nb,),
        out_specs=pl.BlockSpec((tr, C), lambda i, mc: (mc[1] * nb + i, 0)),
        out_shape=_sds((2 * half, C), F32), compiler_params=_cparams(("parallel",)),
    )


def _adamw_math(w, g, m, v):
    m = ADAM_B1 * m + (1.0 - ADAM_B1) * g
    v = ADAM_B2 * v + (1.0 - ADAM_B2) * (g * g)
    m_hat = m / (1.0 - ADAM_B1 ** ADAM_STEP)
    v_hat = v / (1.0 - ADAM_B2 ** ADAM_STEP)
    delta = -ADAM_LR * (m_hat / (jnp.sqrt(v_hat) + ADAM_EPS) + ADAM_WD * w)
    return delta, m, v


def _adamw(name, w, g, m, v):
    R, C = w.shape
    tr = _tile(R, 256)

    def body(w_ref, g_ref, m_ref, v_ref, go_ref, d_ref, mo_ref, vo_ref):
        gv = g_ref[...]
        go_ref[...] = gv
        d_ref[...], mo_ref[...], vo_ref[...] = _adamw_math(w_ref[...], gv, m_ref[...], v_ref[...])

    row = pl.BlockSpec((tr, C), lambda i: (i, 0))
    return ORDER.call(
        body, [w, g, m, v], [row] * 4, name=name, grid=(R // tr,), out_specs=[row] * 4,
        out_shape=[_sds((R, C), F32)] * 4, compiler_params=_cparams(("parallel",)), chain_output=1,
    )


def _adamw_small(gathered, w, m, v):
    rows, n = w.shape

    def body(ga_ref, w_ref, m_ref, v_ref, go_ref, d_ref, mo_ref, vo_ref):
        g = ga_ref[pl.ds(0, rows), :]
        for dev in range(1, 8):
            g = g + ga_ref[pl.ds(dev * rows, rows), :]
        go_ref[...] = g
        d_ref[...], mo_ref[...], vo_ref[...] = _adamw_math(w_ref[...], g, m_ref[...], v_ref[...])

    whole = pl.BlockSpec(memory_space=pltpu.VMEM)
    return ORDER.call(
        body, [gathered, w, m, v], [whole] * 4, name="adamw_small", out_specs=[whole] * 4,
        out_shape=[_sds((rows, n), F32)] * 4, compiler_params=_cparams(), chain_output=1,
    )


def _proj_merge(y_a, y_b, gpa, gpb, g3):
    S, K = y_a.shape
    _, _, Nq = gpa.shape
    D = N_CHIPS * Nq
    tm, tn = _tile(S, 1024), _tile(Nq, 512)
    q = Nq // tn

    def body(ya_ref, yb_ref, wa_ref, wb_ref, g_ref, merged_ref, c_ref):
        pa = jnp.dot(ya_ref[...], wa_ref[...], preferred_element_type=F32)
        pb = jnp.dot(yb_ref[...], wb_ref[...], preferred_element_type=F32)
        g = g_ref[...].astype(F32)
        merged_ref[...] = (g[0] * pa + g[1] * pb).astype(BF)
        c_ref[0] = (pa * g[0] * (1.0 - g[0])).astype(BF)
        c_ref[1] = (pb * g[1] * (1.0 - g[1])).astype(BF)

    rows = pl.BlockSpec((tm, K), lambda i, j: (i, 0))
    weight = pl.BlockSpec((None, K, tn), lambda i, j: (j // q, 0, j % q))
    pair = pl.BlockSpec((2, tm, tn), lambda i, j: (0, i, j))
    return ORDER.call(
        body, [y_a, y_b, gpa, gpb, g3], [rows, rows, weight, weight, pair], name="proj_merge",
        grid=(S // tm, N_CHIPS * q), out_specs=[pl.BlockSpec((tm, tn), lambda i, j: (i, j)), pair],
        out_shape=[_sds((S, D), BF), _sds((2, S, D), BF)], compiler_params=_cparams(("parallel", "parallel")))


class _Exchange:
    GATHER = (("qkv",), ("gate",), ("proj_a", "proj_b", "out"), ("up",), ("down",))
    REDUCE = {"mlp": ("down", "up"), "in": ("out", "proj_a", "proj_b", "qkv", "gate")}

    OWN_FIRST = ("qkv", "gate")

    def __init__(self, shards, me, c):
        self.me, self.c = me, c
        self.hop1, self.hop2, self.stage, self.grads, self.own = {}, {}, {}, {}, {}
        for g, names in enumerate(self.GATHER):
            bufs = []
            for n in names:
                placed = _place_shard(f"place_{n}", shards[n], me, plain=n in self.OWN_FIRST)
                bufs.append(placed[0])
                if n in self.OWN_FIRST:
                    self.own[n] = placed[1]
            self.hop1[g] = _copy_start(f"gather{g}_start", bufs, _gather_hop1, 3 * len(names))

    def forward(self, g):
        send, recv, thru = self.hop1.pop(g)
        self.hop2[g] = _copy_start(f"gather{g}_forward", thru, _gather_hop2, len(thru) * 3,
                                   earlier=(_gather_hop1, send, recv))

    def weights(self, g):
        send, recv, thru = self.hop2.pop(g)
        return _copy_wait(f"gather{g}_wait", thru, _gather_hop2, send, recv)

    def reduce(self, key, partials=None):
        names = self.REDUCE[key]
        n = len(names)
        if partials is not None:
            lands = [lax.empty((p.shape[0], p.shape[1] // 2, p.shape[2]), p.dtype) for p in partials]
            self.stage[key] = ("swap",) + _copy_start(f"reduce_{key}_swap", list(partials) + lands, _swap_copies, n)
            return
        kind, send, recv, thru = self.stage.pop(key)
        if kind == "swap":
            thru = _copy_wait(f"reduce_{key}_swap_wait", thru, _swap_copies, send, recv)
            sums = [_add_sibling(f"reduce_{nm}_add_sibling", p, r, self.c)
                    for nm, p, r in zip(names, thru[:n], thru[n:])]
            lands = [lax.empty((3,) + s_.shape[1:], s_.dtype) for s_ in sums]
            self.stage[key] = ("scatter",) + _copy_start(f"reduce_{key}_scatter", sums + lands, _scatter_copies, 3 * n)
        elif kind == "scatter":
            thru = _copy_wait(f"reduce_{key}_scatter_wait", thru, _scatter_copies, send, recv)
            me_c = jnp.concatenate([self.me, self.c])
            halves = [_add_chips(f"reduce_{nm}_add_chips", s_, r, me_c)
                      for nm, s_, r in zip(names, thru[:n], thru[n:])]
            self.stage[key] = ("join",) + _copy_start(f"reduce_{key}_join", halves, _join_copies, n)
        else:
            thru = _copy_wait(f"reduce_{key}_join_wait", thru, _join_copies, send, recv)
            self.grads.update(zip(names, thru))


def _forward_backward(x, target, norm_mix, b_gate, rpb, norm_mlp, norm_final, ex):
    S, D = x.shape

    h1 = _rms_fwd("rms_mix", x, norm_mix)
    nq = QKV_W // 512
    qkv_out = (((3, S, QKV_W), BF), lambda i, T: (T // nq, i, T % nq))
    tg = _tile(ex.own["gate"].shape[1], 1024)
    ng = D // tg
    gate_out = (((2, S, D), BF), lambda i, T: (T // ng, i, T % ng))

    def gate_epilogue(acc, ex_, outs):
        outs[0][...] = jax.nn.sigmoid(acc + ex_[0][...]).astype(BF)

    qkv3 = _mm_nn_shards("qkv_own", h1, ex.own["qkv"], ex.me, True, *qkv_out, _store(BF))
    g3 = _mm_nn_shards("gate_own", h1, ex.own["gate"], ex.me, True, *gate_out, gate_epilogue, extras=[b_gate], tn=tg)
    ex.forward(0)
    e2 = _rpb_to_table(rpb)
    (gq,) = ex.weights(0)
    qkv3 = _mm_nn_shards("qkv", h1, gq, ex.me, False, *qkv_out, _store(BF), into=qkv3)

    ex.forward(1)
    outs_a = [_attn_a_fwd(qkv3, 0, DILATIONS[0])]
    (gg,) = ex.weights(1)
    g3 = _mm_nn_shards("gate", h1, gg, ex.me, False, *gate_out, gate_epilogue, extras=[b_gate], into=g3, tn=tg)

    ex.forward(2)
    qkv_views = _qkv_views("qkv_views", qkv3)
    outs_a += [_attn_a_fwd(qkv_views[d], grp, d) for grp, d in enumerate(DILATIONS) if grp > 0]
    y_a, lj = _attn_a_combine([o for o, _ in outs_a], [l for _, l in outs_a])
    y_b, lse_b = _attn_b_fwd(qkv3, e2)
    gpa, gpb, gout = ex.weights(2)
    wout = gout.reshape(D, D)
    merged, c3 = _proj_merge(y_a, y_b, gpa, gpb, g3)

    def residual_epilogue(acc, ex_, outs):
        outs[0][...] = acc + ex_[0][...]

    def nn_plain(name, a, w, res, bm=1024, bn=1024):
        M, K = a.shape
        N = w.shape[1]
        bm, bn, bk = _tile(M, bm), _tile(N, bn), _tile(K, 2048)
        t = pl.BlockSpec((bm, bn), lambda i, j, k: (i, j))
        return _matmul(name, a, w, pl.BlockSpec((bm, bk), lambda i, j, k: (i, k)),
                       pl.BlockSpec((bk, bn), lambda i, j, k: (k, j)), NN, (M // bm, N // bn, K // bk), (bm, bn),
                       [(res, t)], [(_sds((M, N), F32), t)], residual_epilogue)[0]

    ex.forward(3)
    x1 = nn_plain("out_proj", merged, wout, x, bm=512, bn=2048)
    h2 = _rms_fwd("rms_mlp", x1, norm_mlp)
    (gup,) = ex.weights(3)
    F = gup.shape[2] * N_CHIPS

    def up_epilogue(acc, ex_, outs):
        ru = jnp.maximum(acc, 0.0)
        outs[0][...] = (ru * ru).astype(BF)
        outs[1][...] = ru.astype(BF)

    tu = _tile(gup.shape[2], 2048)
    ut = pl.BlockSpec((_tile(S, 1024), tu), lambda i, j, k: (i, j))
    (act, ru), _ = _mm_nn_cols("mlp_up", h2, gup, BF, epilogue=up_epilogue, tn=tu,
                               outs=[(_sds((S, F), BF), ut), (_sds((S, F), BF), ut)])
    ex.forward(4)
    (gdown,) = ex.weights(4)
    wdown = gdown.reshape(F, D)
    x2 = nn_plain("mlp_down", act, wdown, x1)

    loss, dx2, dx2b, d_norm_final = _loss_head(x2, target, norm_final.reshape(1, D))

    def nt_rows(name, a, w, epilogue, extras, outs, bn=1024):
        M, N = a.shape
        K = w.shape[0]
        bm, bn, bk = _tile(M, 1024), _tile(K, bn), _tile(N, 2048)
        return _matmul(name, a, w, pl.BlockSpec((bm, bk), lambda i, j, k: (i, k)),
                       pl.BlockSpec((bn, bk), lambda i, j, k: (j, k)), NT, (M // bm, K // bn, N // bk), (bm, bn),
                       extras(bm, bn), outs(bm, bn), epilogue)

    def nt_cols(name, a_spec_fn, a, g, M, epilogue, extras, outs, bk, bn=1024):
        _, K, Nq = g.shape
        bm, bn, bk = _tile(M, 1024), _tile(K, bn), _tile(Nq, bk)
        q = Nq // bk
        return _matmul(name, a, g, a_spec_fn(bm, bk), pl.BlockSpec((None, bn, bk), lambda i, j, k: (k // q, j, k % q)),
                       NT, (M // bm, K // bn, N_CHIPS * q), (bm, bn), extras(bm, bn), outs(bm, bn), epilogue)

    def tn_grad(name, a, a_spec_fn, b, b_spec_fn, Kin, N, out_shape, out_spec_fn, bn=1024):
        bm, bn, bk = _tile(Kin, 1024), _tile(N, bn), _tile(S, 4096)
        return _matmul(name, a, b, a_spec_fn(bk, bm), b_spec_fn(bk, bn), TN, (Kin // bm, N // bn, S // bk), (bm, bn),
                       [], [(_sds(out_shape, BF), out_spec_fn(bm, bn))], _store(BF))[0]

    plain_a = lambda bk, bm: pl.BlockSpec((bk, bm), lambda i, j, k: (k, i))
    plain_b = lambda bk, bn: pl.BlockSpec((bk, bn), lambda i, j, k: (k, j))
    plain_o = lambda bm, bn: pl.BlockSpec((bm, bn), lambda i, j, k: (i, j))
    a_rows = lambda bm, bk: pl.BlockSpec((bm, bk), lambda i, j, k: (i, k))

    def cols_o(Nq):
        def spec(bm, bn):
            q = Nq // bn
            return pl.BlockSpec((None, bm, bn), lambda i, j, k: (j // q, i, j % q))
        return spec

    def du_epilogue(acc, ex_, outs):
        outs[0][...] = (acc * (2.0 * ex_[0][...].astype(F32))).astype(BF)

    dw_down = tn_grad("mlp_down_dw", act, plain_a, dx2b, plain_b, F, D, (F, D), plain_o)
    (du,) = nt_rows("mlp_down_dx", dx2b, wdown, du_epilogue,
                    lambda bm, bn: [(ru, plain_o(bm, bn))], lambda bm, bn: [(_sds((S, F), BF), plain_o(bm, bn))],
                    bn=2048)

    fq = gup.shape[2]
    dw_up = tn_grad("mlp_up_dw", h2, plain_a, du, plain_b, D, F, (N_CHIPS, D, fq), cols_o(fq), bn=min(fq, 1024))
    ex.reduce("mlp", partials=[dw_down.reshape(N_CHIPS, F // N_CHIPS, D), dw_up])
    (dh2,) = nt_cols("mlp_up_dx", a_rows, du, gup, S, _store(F32), lambda bm, bn: [],
                     lambda bm, bn: [(_sds((S, D), F32), plain_o(bm, bn))], 1024, bn=2048)
    ex.reduce("mlp")
    dx1, dx1b, d_norm_mlp = _rms_bwd("rms_mlp_bwd", dh2, x1, norm_mlp, dx2)

    def merge_bwd_epilogue(acc, ex_, outs):
        g, c = ex_[0][...].astype(F32), ex_[1][...].astype(F32)
        outs[0][...] = (acc * g[0]).astype(BF)
        outs[1][...] = (acc * g[1]).astype(BF)
        dga = acc * c[0]
        dgb = acc * c[1]
        outs[2][0] = dga.astype(BF)
        outs[2][1] = dgb.astype(BF)
        outs[3][...] = jnp.concatenate([jnp.sum(dga, axis=0, keepdims=True), jnp.sum(dgb, axis=0, keepdims=True)], 0)

    def pair(bm, bn):
        return pl.BlockSpec((2, bm, bn), lambda i, j, k: (0, i, j))

    n_row_blocks = S // _tile(S, 1024)
    dpa, dpb, dg3, db_gate = nt_rows(
        "out_proj_dx", dx1b, wout, merge_bwd_epilogue,
        lambda bm, bn: [(g3, pair(bm, bn)), (c3, pair(bm, bn))],
        lambda bm, bn: [(_sds((S, D), BF), plain_o(bm, bn)), (_sds((S, D), BF), plain_o(bm, bn)),
                        (_sds((2, S, D), BF), pair(bm, bn)),
                        (_sds((n_row_blocks, 2, D), F32), pl.BlockSpec((None, 2, bn), lambda i, j, k: (i, 0, j)))],
        bn=512)
    dw_out = tn_grad("out_proj_dw", merged, plain_a, dx1b, plain_b, D, D, (D, D), plain_o)

    pq = gpa.shape[2]
    proj_dx = lambda name, dproj, g: nt_cols(name, a_rows, dproj, g, S, _store(BF), lambda bm, bn: [],
                                             lambda bm, bn: [(_sds((S, 512), BF), plain_o(bm, bn))], 512)[0]
    dw_pa = tn_grad("proj_a_dw", y_a, plain_a, dpa, plain_b, 512, D, (N_CHIPS, 512, pq), cols_o(pq), bn=min(pq, 512))
    dw_pb = tn_grad("proj_b_dw", y_b, plain_a, dpb, plain_b, 512, D, (N_CHIPS, 512, pq), cols_o(pq), bn=min(pq, 512))
    dy_a = proj_dx("proj_a_dx", dpa, gpa)
    dy_b = proj_dx("proj_b_dx", dpb, gpb)

    dqkv3 = lax.empty((3, S, QKV_W), BF)
    dqkv3 = _attn_a_bwd(qkv3, dy_a, y_a, lj, dqkv3, 0, DILATIONS[0])
    dy_views, y_views, lj_views = _dilated_rows("attn_a_bwd_rows", [dy_a, y_a, lj])
    dqkv_views = {d: _attn_a_bwd(qkv_views[d], dy_views[d], y_views[d], lj_views[d], None, grp, d)
                  for grp, d in enumerate(DILATIONS) if grp > 0}
    dqkv3 = _qkv_views("dqkv_from_views", dqkv3, dqkv_views)
    dqkv3, de2 = _attn_b_bwd(qkv3, e2, dy_b, y_b, lse_b, dqkv3)
    d_rpb = _table_grad_to_rpb(de2)

    def stacked_a(width):
        def spec(bm, bk):
            q = width // bk
            return pl.BlockSpec((None, bm, bk), lambda i, j, k: (k // q, i, k % q))
        return spec

    def stacked_b(width):
        def spec(bk, bn):
            q = width // bn
            return pl.BlockSpec((None, bk, bn), lambda i, j, k: (j // q, k, j % q))
        return spec

    dw_qkv = tn_grad("qkv_dw", h1, plain_a, dqkv3, stacked_b(QKV_W), D, 3 * QKV_W, (N_CHIPS,) + gq.shape[1:],
                     cols_o(gq.shape[2]), bn=512)
    dw_gate = tn_grad("gate_dw", h1, plain_a, dg3, stacked_b(D), D, 2 * D, (N_CHIPS,) + gg.shape[1:],
                      cols_o(gg.shape[2]), bn=gg.shape[2])
    ex.reduce("in", partials=[dw_out.reshape(N_CHIPS, D // N_CHIPS, D), dw_pa, dw_pb, dw_qkv, dw_gate])
    ex.reduce("mlp")
    (dh1_q,) = nt_cols("qkv_dx", stacked_a(QKV_W), dqkv3, gq, S, _store(F32), lambda bm, bn: [],
                       lambda bm, bn: [(_sds((S, D), F32), plain_o(bm, bn))], 512, bn=2048)
    ex.reduce("in")

    def add_epilogue(acc, ex_, outs):
        outs[0][...] = acc + ex_[0][...]

    (dh1,) = nt_cols("gate_dx", stacked_a(D), dg3, gg, S, add_epilogue, lambda bm, bn: [(dh1_q, plain_o(bm, bn))],
                     lambda bm, bn: [(_sds((S, D), F32), plain_o(bm, bn))], gg.shape[2])
    grad_x, _, d_norm_mix = _rms_bwd("rms_mix_bwd", dh1, x, norm_mix, dx1)
    ex.reduce("mlp")

    small = [d_norm_mix, jnp.sum(db_gate, axis=0).reshape(1, 2 * D), d_rpb, d_norm_mlp, d_norm_final]
    return loss, grad_x, small


def _pack_small(parts, width):
    flat = jnp.concatenate([p.reshape(-1) for p in parts])
    return jnp.pad(flat, (0, 8 * width - flat.shape[0])).reshape(8, width)


def kernel(x, norm_mix, w_qkv, w_gate, b_gate, rpb, w_proj_a, w_proj_b, w_out, norm_mlp, w_up, w_down, norm_final, loss_target, m_norm_mix, m_w_qkv, m_w_gate, m_b_gate, m_rpb, m_w_proj_a, m_w_proj_b, m_w_out, m_norm_mlp, m_w_up, m_w_down, m_norm_final, v_norm_mix, v_w_qkv, v_w_gate, v_b_gate, v_rpb, v_w_proj_a, v_w_proj_b, v_w_out, v_norm_mlp, v_w_up, v_w_down, v_norm_final):
    names = ["qkv", "gate", "proj_a", "proj_b", "out", "up", "down"]
    big = dict(zip(names, [w_qkv[0], w_gate[0], w_proj_a[0], w_proj_b[0], w_out[0], w_up[0], w_down[0]]))
    big_m = dict(zip(names, [m_w_qkv[0], m_w_gate[0], m_w_proj_a[0], m_w_proj_b[0], m_w_out[0], m_w_up[0], m_w_down[0]]))
    big_v = dict(zip(names, [v_w_qkv[0], v_w_gate[0], v_w_proj_a[0], v_w_proj_b[0], v_w_out[0], v_w_up[0], v_w_down[0]]))

    c = lax.axis_index("c").astype(jnp.int32).reshape(1)
    me = (2 * lax.axis_index("x") + lax.axis_index("y")).astype(jnp.int32).reshape(1)
    ORDER.last = None
    ex = _Exchange(big, me, c)
    loss, grad_x, small = _forward_backward(x[0], loss_target[0], norm_mix, b_gate, rpb[0], norm_mlp, norm_final, ex)

    def adamw(group):
        return {n: _adamw(f"adamw_{n}", big[n], ex.grads[n], big_m[n], big_v[n]) for n in _Exchange.REDUCE[group]}

    big_out = adamw("mlp")
    ex.reduce("in")

    small_w = [norm_mix, b_gate, rpb, norm_mlp, norm_final]
    count = sum(int(np.prod(p.shape)) for p in small_w)
    width = -(-count // (8 * 128)) * 128
    packed = _adamw_small(_gather_small(_pack_small(small, width)), _pack_small(small_w, width),
                          _pack_small([m_norm_mix, m_b_gate, m_rpb, m_norm_mlp, m_norm_final], width),
                          _pack_small([v_norm_mix, v_b_gate, v_rpb, v_norm_mlp, v_norm_final], width))
    ex.reduce("in")
    big_out.update(adamw("in"))

    def unpack(flat2d):
        flat, out, at = flat2d.reshape(-1), [], 0
        for p in small_w:
            size = int(np.prod(p.shape))
            out.append(flat[at:at + size].reshape(p.shape))
            at += size
        return out

    small_out = [unpack(a) for a in packed]

    def ordered(kind):
        sm = small_out[kind]
        bg = {n: o[kind][None] for n, o in big_out.items()}
        return [sm[0], bg["qkv"], bg["gate"], sm[1], sm[2], bg["proj_a"], bg["proj_b"], bg["out"], sm[3],
                bg["up"], bg["down"], sm[4]]

    total = lax.psum(loss[0, 0], ("x", "y", "c"))
    return (total, grad_x[None], *ordered(0), *ordered(1), *ordered(2), *ordered(3))
```

```python
import functools
import math

import numpy as np
import jax
import jax.numpy as jnp
from jax import lax
from jax.experimental import pallas as pl
from jax.experimental.pallas import tpu as pltpu

BF = jnp.bfloat16
F32 = jnp.float32
MESH = pl.DeviceIdType.MESH

HEAD_DIM = 128
N_HEADS = 16
N_HEADS_A = 12
QKV_W = N_HEADS * HEAD_DIM
DILATIONS = (1, 4, 16)
HALF_WINDOW = 64
GRID_W = 64
NA_ROWS = 8
NA_COLS = 16
RPB_ROWS = 2 * NA_ROWS - 1
RPB_COLS = 2 * NA_COLS - 1
EPS = 1e-6
NEG = -1e30
SCALE = HEAD_DIM ** -0.5

ADAM_LR = 0.001
ADAM_B1 = 0.9
ADAM_B2 = 0.999
ADAM_EPS = 1e-08
ADAM_WD = 0.01
ADAM_STEP = 10

N_CHIPS = 4
VMEM_LIMIT_BYTES = 48 * 1024 * 1024
QB = 256
NBR_SIDE = 8
ROW_TILE = 512


def _key_rows(L):
    return min(QB + 2 * HALF_WINDOW, L)


def _cparams(sem=None):
    return pltpu.CompilerParams(dimension_semantics=sem, vmem_limit_bytes=VMEM_LIMIT_BYTES)


def _tile(dim, want):
    t = min(dim, want)
    assert dim % t == 0, (dim, want)
    return t


class _ProgramOrder:
    def __init__(self):
        self.last = None

    def call(self, body, operands, in_specs, *, prefetch=(), grid=None, out_specs=None, chain_output=0, **kwargs):
        operands, in_specs = list(operands), list(in_specs)
        lead = len(prefetch) + len(operands)
        if self.last is not None and not any(op is self.last for op in operands):
            operands.append(self.last)
            in_specs.append(pl.BlockSpec(memory_space=pl.ANY))
            inner = body

            def body(*refs):
                return inner(*refs[:lead], *refs[lead + 1:])

        if prefetch:
            kwargs["grid_spec"] = pltpu.PrefetchScalarGridSpec(
                num_scalar_prefetch=len(prefetch), grid=grid, in_specs=in_specs, out_specs=out_specs)
        else:
            kwargs.update(in_specs=in_specs, out_specs=out_specs)
            if grid is not None:
                kwargs["grid"] = grid
        out = pl.pallas_call(body, **kwargs)(*prefetch, *operands)
        self.last = out[chain_output] if isinstance(out, (tuple, list)) else out
        return out


ORDER = _ProgramOrder()


NN = ((1,), (0,))
NT = ((1,), (1,))
TN = ((0,), (0,))


def _matmul(name, a, b, a_spec, b_spec, dims, grid, acc_shape, extras, outs, epilogue, precision=None,
            prefetch=(), into=None):
    n_ex, n_out, nk = len(extras), len(outs), grid[2]
    n_in = 2 + n_ex + (into is not None)

    def body(*refs):
        refs = refs[len(prefetch):]
        a_ref, b_ref = refs[0], refs[1]
        ex_refs = refs[2:2 + n_ex]
        out_refs = refs[n_in:n_in + n_out]

        def dot():
            return lax.dot_general(a_ref[...], b_ref[...], (dims, ((), ())),
                                   preferred_element_type=F32, precision=precision)

        if nk == 1:
            epilogue(dot(), ex_refs, out_refs)
            return
        acc_ref = refs[-1]
        k = pl.program_id(2)

        @pl.when(k == 0)
        def _():
            acc_ref[...] = dot()

        if nk > 2:
            @pl.when((k > 0) & (k < nk - 1))
            def _():
                acc_ref[...] += dot()

        @pl.when(k == nk - 1)
        def _():
            epilogue(acc_ref[...] + dot(), ex_refs, out_refs)

    operands = [a, b] + [e for e, _ in extras]
    in_specs = [a_spec, b_spec] + [s for _, s in extras]
    kwargs = {}
    if into is not None:
        operands.append(into)
        in_specs.append(pl.BlockSpec(memory_space=pl.ANY))
        kwargs["input_output_aliases"] = {len(prefetch) + n_in - 1: 0}
    return ORDER.call(
        body, operands, in_specs, prefetch=prefetch, name=name, grid=grid,
        out_specs=[s for _, s in outs],
        out_shape=[sh for sh, _ in outs],
        scratch_shapes=[pltpu.VMEM(acc_shape, F32)] if nk > 1 else [],
        compiler_params=_cparams(("parallel", "parallel", "arbitrary")), **kwargs,
    )


def _mm_nn_shards(name, a, w, me, own, out, out_block, epilogue, extras=(), into=None, tn=512):
    M, K = a.shape
    Nq = w.shape[-1]
    tm, tn = _tile(M, 1024), _tile(Nq, tn)
    q = Nq // tn

    def tile(j, me_ref):
        shard = me_ref[0] if own else (me_ref[0] + 1 + j // q) % N_CHIPS
        return shard, j % q, shard * q + j % q

    if own:
        b_spec = pl.BlockSpec((K, tn), lambda i, j, k, me_ref: (0, j))
    else:
        b_spec = pl.BlockSpec((None, K, tn), lambda i, j, k, me_ref: (tile(j, me_ref)[0], 0, tile(j, me_ref)[1]))
    shape, dtype = out
    out_spec = pl.BlockSpec((None, tm, tn), lambda i, j, k, me_ref: out_block(i, tile(j, me_ref)[2]))
    ex = [(e, pl.BlockSpec((1, tn), lambda i, j, k, me_ref: (0, tile(j, me_ref)[2]))) for e in extras]
    return _matmul(name, a, w, pl.BlockSpec((tm, K), lambda i, j, k, me_ref: (i, 0)), b_spec, NN,
                   (M // tm, q if own else (N_CHIPS - 1) * q, 1), (tm, tn), ex, [(_sds(shape, dtype), out_spec)],
                   epilogue, prefetch=(me,), into=into)[0]


def _store(dtype):
    def epilogue(acc, ex, outs):
        outs[0][...] = acc.astype(dtype)
    return epilogue


def _sds(shape, dtype):
    return jax.ShapeDtypeStruct(shape, dtype)


def _mm_nn_cols(name, a, g, out_dtype, epilogue=None, extras=(), outs=None, tm=1024, tn=1024, tk=2048):
    M, K = a.shape
    _, _, Nq = g.shape
    tm, tn, tk = _tile(M, tm), _tile(Nq, tn), _tile(K, tk)
    q = Nq // tn
    grid = (M // tm, N_CHIPS * q, K // tk)
    if outs is None:
        outs = [(_sds((M, N_CHIPS * Nq), out_dtype), pl.BlockSpec((tm, tn), lambda i, j, k: (i, j)))]
    return _matmul(name, a, g, pl.BlockSpec((tm, tk), lambda i, j, k: (i, k)),
                   pl.BlockSpec((None, tk, tn), lambda i, j, k: (j // q, k, j % q)), NN, grid, (tm, tn),
                   list(extras), outs, epilogue or _store(out_dtype)), (tm, tn, tk)


def _rms_fwd(name, x, g):
    S, D = x.shape
    tm = _tile(S, ROW_TILE)

    def body(x_ref, g_ref, h_ref):
        xv = x_ref[...]
        r = lax.rsqrt(jnp.mean(xv * xv, axis=-1, keepdims=True) + EPS)
        h_ref[...] = ((xv * r) * g_ref[...]).astype(BF)

    row = pl.BlockSpec((tm, D), lambda i: (i, 0))
    return ORDER.call(
        body, [x, g], [row, pl.BlockSpec((1, D), lambda i: (0, 0))], name=name, grid=(S // tm,),
        out_specs=row, out_shape=_sds((S, D), BF), compiler_params=_cparams(("parallel",)),
    )


def _rms_bwd(name, dh, x, g, dres):
    S, D = x.shape
    tm = _tile(S, ROW_TILE // 2)

    def body(dh_ref, x_ref, g_ref, dres_ref, dx_ref, dxb_ref, dg_ref):
        xv = x_ref[...]
        r = lax.rsqrt(jnp.mean(xv * xv, axis=-1, keepdims=True) + EPS)
        n = xv * r
        dhv = dh_ref[...]
        dyg = dhv * g_ref[...]
        dx = dres_ref[...] + r * (dyg - n * jnp.mean(dyg * n, axis=-1, keepdims=True))
        dx_ref[...] = dx
        dxb_ref[...] = dx.astype(BF)

        @pl.when(pl.program_id(0) == 0)
        def _():
            dg_ref[...] = jnp.zeros_like(dg_ref)

        dg_ref[...] += jnp.sum(dhv * n, axis=0, keepdims=True)

    row = pl.BlockSpec((tm, D), lambda i: (i, 0))
    vec = pl.BlockSpec((1, D), lambda i: (0, 0))
    return ORDER.call(
        body, [dh, x, g, dres], [row, row, vec, row], name=name, grid=(S // tm,),
        out_specs=[row, row, vec],
        out_shape=[_sds((S, D), F32), _sds((S, D), BF), _sds((1, D), F32)],
        compiler_params=_cparams(("arbitrary",)),
    )


def _loss_head(x2, target, g):
    S, D = x2.shape
    tm = _tile(S, ROW_TILE)

    def body(x_ref, t_ref, g_ref, loss_ref, dx_ref, dxb_ref, dg_ref):
        xv = x_ref[...]
        gv = g_ref[...]
        r = lax.rsqrt(jnp.mean(xv * xv, axis=-1, keepdims=True) + EPS)
        n = xv * r
        e = n * gv - t_ref[...]
        dy = e * (1.0 / D)
        dyg = dy * gv
        dx = r * (dyg - n * jnp.mean(dyg * n, axis=-1, keepdims=True))
        dx_ref[...] = dx
        dxb_ref[...] = dx.astype(BF)

        @pl.when(pl.program_id(0) == 0)
        def _():
            dg_ref[...] = jnp.zeros_like(dg_ref)
            loss_ref[...] = jnp.zeros_like(loss_ref)

        dg_ref[...] += jnp.sum(dy * n, axis=0, keepdims=True)
        per_row = jnp.mean(e * e, axis=-1, keepdims=True)
        loss_ref[...] += 0.5 * jnp.sum(per_row, axis=0, keepdims=True)

    row = pl.BlockSpec((tm, D), lambda i: (i, 0))
    vec = pl.BlockSpec((1, D), lambda i: (0, 0))
    return ORDER.call(
        body, [x2, target, g], [row, row, vec], name="loss_head", grid=(S // tm,),
        out_specs=[pl.BlockSpec((1, 1), lambda i: (0, 0)), row, row, vec],
        out_shape=[_sds((1, 1), F32), _sds((S, D), F32), _sds((S, D), BF), _sds((1, D), F32)],
        compiler_params=_cparams(("arbitrary",)), chain_output=1,
    )


def _chains(L):
    side = min(4, L // QB)
    return side, 4 // side


def _band_scores(qkv_ref, i, L, coef, head):
    KB = _key_rows(L)
    lanes = pl.ds(head * HEAD_DIM, HEAD_DIM)
    q0 = pl.multiple_of(i * QB, QB)
    ks = pl.multiple_of(jnp.clip(i * QB - HALF_WINDOW, 0, L - KB), HALF_WINDOW)
    q = qkv_ref[0, pl.ds(q0, QB), lanes]
    k = qkv_ref[1, pl.ds(ks, KB), lanes]
    v = qkv_ref[2, pl.ds(ks, KB), lanes]
    s = lax.dot_general(q, k, (NT, ((), ())), preferred_element_type=F32) * SCALE
    qpos = q0 + lax.broadcasted_iota(jnp.int32, (QB, KB), 0)
    kpos = ks + lax.broadcasted_iota(jnp.int32, (QB, KB), 1)
    rel = jnp.abs(kpos - qpos)
    valid = rel <= HALF_WINDOW
    s = jnp.where(valid, s - coef * rel.astype(F32), NEG)
    return q0, ks, q, k, v, s, valid


def _alibi_coefs(group, d, heads):
    first = 4 * group + 1 + pl.program_id(1) * heads
    scale = jnp.full((1, 1), -(8.0 / N_HEADS_A) * math.log(2.0), F32)
    return [jnp.exp(scale * (first + hh).astype(F32)) * float(d) for hh in range(heads)]


def _dilated_view(qkv3, group, d, heads):
    per = 4 // heads
    L = qkv3.shape[1]
    if d == 1:
        return qkv3, pl.BlockSpec((3, L, heads * HEAD_DIM), lambda r, j: (0, 0, per * group + j))
    return qkv3, pl.BlockSpec((3, L, heads * HEAD_DIM), lambda r, j: (0, 0, r * per + j))


def _qkv_views(name, qkv3, views=None):
    _, S, _ = qkv3.shape
    W = 512
    tm = _tile(S, ROW_TILE)
    dilated = [(g, d) for g, d in enumerate(DILATIONS) if d > 1]
    first = dilated[0][0]
    assert [g for g, _ in dilated] == list(range(first, first + len(dilated)))
    nc = W // 128
    to_views = views is None

    def body(*refs):
        scr = refs[-nc:]
        if to_views:
            src, outs = refs[0], refs[1:1 + len(dilated)]
        else:
            ins, dst = refs[:len(dilated)], refs[len(dilated) + 1]
        for k, (_, d) in enumerate(dilated):
            @pl.when(pl.program_id(1) == k)
            def _():
                for w in range(3):
                    for c in range(nc):
                        if to_views:
                            scr[c][...] = src[w, :, c * 128:(c + 1) * 128].astype(F32)
                    for r in range(d):
                        for c in range(nc):
                            at = r * W + c * 128
                            if to_views:
                                outs[k][w, :, at:at + 128] = scr[c][pl.ds(r, tm // d, stride=d), :].astype(BF)
                            else:
                                scr[c][pl.ds(r, tm // d, stride=d), :] = ins[k][w, :, at:at + 128].astype(F32)
                    for c in range(nc):
                        if not to_views:
                            dst[w, :, c * 128:(c + 1) * 128] = scr[c][...].astype(BF)

    cols = pl.BlockSpec((3, tm, W), lambda i, k: (0, i, first + k))
    rows = [pl.BlockSpec((3, tm // d, d * W), lambda i, k: (0, i, 0)) for _, d in dilated]
    shapes = [_sds((3, S // d, d * W), BF) for _, d in dilated]
    common = dict(name=name, grid=(S // tm, len(dilated)), scratch_shapes=[pltpu.VMEM((tm, 128), F32)] * nc,
                  compiler_params=_cparams(("parallel", "arbitrary")))
    if to_views:
        outs = ORDER.call(body, [qkv3], [cols], out_specs=rows, out_shape=shapes, **common)
        return {d: o for (_, d), o in zip(dilated, outs)}
    return ORDER.call(body, [views[d] for _, d in dilated] + [qkv3], rows + [pl.BlockSpec(memory_space=pl.ANY)],
                      out_specs=cols, out_shape=_sds(qkv3.shape, BF), input_output_aliases={len(dilated): 0}, **common)


def _attn_a_fwd(qkv3, group, d):
    L = qkv3.shape[1]
    S = L * d
    assert L % QB == 0
    side, heads = _chains(L)
    view, blocks_spec = _dilated_view(qkv3, group, d, heads)

    def body(qkv_ref, o_ref, lse_ref):
        coefs = _alibi_coefs(group, d, heads)

        def step(i, carry):
            chains = [(hh, _band_scores(qkv_ref, side * i + u, L, coefs[hh], hh))
                      for u in range(side) for hh in range(heads)]
            soft = []
            for hh, (q0, _, _, _, v, s, _) in chains:
                m = jnp.max(s, axis=-1, keepdims=True)
                p = jnp.exp(s - m)
                den = jnp.sum(p, axis=-1, keepdims=True)
                soft.append((hh, q0, (p / den).astype(BF), v, m + jnp.log(den)))
            for hh, q0, pn, v, lse in soft:
                lanes = pl.ds(hh * HEAD_DIM, HEAD_DIM)
                o_ref[pl.ds(q0, QB), lanes] = jnp.dot(pn, v, preferred_element_type=F32)
                lse_ref[pl.ds(q0, QB), lanes] = jnp.broadcast_to(lse, (QB, HEAD_DIM))
            return carry

        lax.fori_loop(0, L // QB // side, step, 0)

    per = 4 // heads
    out = pl.BlockSpec((L, heads * HEAD_DIM), lambda r, j: (0, r * per + j))
    o, lse = ORDER.call(
        body, [view], [blocks_spec],
        name=f"attn_a_fwd_d{d}", grid=(d, per),
        out_specs=[out, out],
        out_shape=[_sds((L, d * 512), F32), _sds((L, d * 512), F32)],
        compiler_params=_cparams(("parallel", "parallel")),
    )
    return o, lse


def _dilated_rows(name, arrays):
    S, W = arrays[0].shape
    tm = _tile(S, ROW_TILE)
    ds_ = [d for d in DILATIONS if d > 1]
    n = len(arrays)

    def body(*refs):
        nc = W // 128
        ins, outs, scr = refs[:n], refs[n:-nc], refs[-nc:]
        for a, src in enumerate(ins):
            for c in range(nc):
                scr[c][...] = src[:, c * 128:(c + 1) * 128].astype(F32)
            for k, d in enumerate(ds_):
                dst = outs[a * len(ds_) + k]
                for r in range(d):
                    for c in range(nc):
                        at = r * W + c * 128
                        dst[:, at:at + 128] = scr[c][pl.ds(r, tm // d, stride=d), :].astype(dst.dtype)

    row = pl.BlockSpec((tm, W), lambda i: (i, 0))
    out_specs, out_shape = [], []
    for a in arrays:
        for d in ds_:
            out_specs.append(pl.BlockSpec((tm // d, d * W), lambda i: (i, 0)))
            out_shape.append(_sds((S // d, d * W), a.dtype))
    outs = ORDER.call(body, list(arrays), [row] * n, name=name, grid=(S // tm,), out_specs=out_specs,
                      out_shape=out_shape, scratch_shapes=[pltpu.VMEM((tm, 128), F32)] * (W // 128),
                      compiler_params=_cparams(("parallel",)))
    return [{d: outs[a * len(ds_) + k] for k, d in enumerate(ds_)} for a in range(n)]


def _attn_a_combine(os_, lses):
    W = 512
    S = os_[0].shape[0] * DILATIONS[0]
    tm = _tile(S, ROW_TILE)
    nc = W // 128
    dilated = [g for g, d in enumerate(DILATIONS) if d > 1]

    def body(o0, o1, o2, l0, l1, l2, y_ref, lj_ref, *scr):
        def token_order(src, g, slot):
            d = DILATIONS[g]
            if d == 1:
                return src[...]
            bufs = scr[slot * nc:(slot + 1) * nc]
            for r in range(d):
                for c in range(nc):
                    at = r * W + c * 128
                    bufs[c][pl.ds(r, tm // d, stride=d), :] = src[:, at:at + 128]
            return jnp.concatenate([buf[...] for buf in bufs], axis=1)

        slots = {g: k for k, g in enumerate(dilated)}
        ls = [token_order(l, g, slots.get(g, 0)) for g, l in enumerate((l0, l1, l2))]
        os_tok = [token_order(o, g, len(dilated) + slots.get(g, 0)) for g, o in enumerate((o0, o1, o2))]
        m = jnp.maximum(jnp.maximum(ls[0], ls[1]), ls[2])
        es = [jnp.exp(l - m) for l in ls]
        den = es[0] + es[1] + es[2]
        y = (es[0] / den) * os_tok[0] + (es[1] / den) * os_tok[1] + (es[2] / den) * os_tok[2]
        y_ref[...] = y.astype(BF)
        lj_ref[...] = m + jnp.log(den)

    row = pl.BlockSpec((tm, W), lambda i: (i, 0))
    views = [pl.BlockSpec((tm // d, d * W), lambda i: (i, 0)) for d in DILATIONS]
    return ORDER.call(
        body, [*os_, *lses], views + views, name="attn_a_combine", grid=(S // tm,), out_specs=[row, row],
        out_shape=[_sds((S, W), BF), _sds((S, W), F32)],
        scratch_shapes=[pltpu.VMEM((tm, 128), F32)] * (2 * len(dilated) * nc),
        compiler_params=_cparams(("parallel",)),
    )


def _attn_a_bwd(qkv3, dy, y, lj, dqkv3, group, d):
    L = qkv3.shape[1]
    S = L * d
    side, heads = _chains(L)
    view, blocks_spec = _dilated_view(qkv3, group, d, heads)

    def body(qkv_ref, dy_ref, y_ref, lj_ref, *rest):
        out_ref, dk_acc, dv_acc = rest[-3:]
        coefs = _alibi_coefs(group, d, heads)
        dk_acc[...] = jnp.zeros_like(dk_acc)
        dv_acc[...] = jnp.zeros_like(dv_acc)

        def step(i, carry):
            chains = [(pl.ds(hh * HEAD_DIM, HEAD_DIM), _band_scores(qkv_ref, side * i + u, L, coefs[hh], hh))
                      for u in range(side) for hh in range(heads)]
            dys = [dy_ref[pl.ds(c[0], QB), lanes] for lanes, c in chains]
            dps = [lax.dot_general(dyv, c[4], (NT, ((), ())), preferred_element_type=F32)
                   for dyv, (_, c) in zip(dys, chains)]
            grads = []
            for (lanes, (q0, ks, q, k, v, s, valid)), dyv, dp in zip(chains, dys, dps):
                rows = pl.ds(q0, QB)
                delta = jnp.sum(dyv.astype(F32) * y_ref[rows, lanes].astype(F32), axis=-1, keepdims=True)
                p = jnp.where(valid, jnp.exp(s - jnp.tile(lj_ref[rows, lanes], (1, _key_rows(L) // HEAD_DIM))), 0.0)
                grads.append(((p * (dp - delta)).astype(BF), p.astype(BF)))
            for (lanes, (q0, ks, q, k, v, s, valid)), dyv, (ds, pb) in zip(chains, dys, grads):
                out_ref[0, pl.ds(q0, QB), lanes] = (jnp.dot(ds, k, preferred_element_type=F32) * SCALE).astype(BF)
                keys = pl.ds(ks, _key_rows(L))
                dk_acc[keys, lanes] += lax.dot_general(ds, q, (TN, ((), ())), preferred_element_type=F32) * SCALE
                dv_acc[keys, lanes] += lax.dot_general(pb, dyv, (TN, ((), ())), preferred_element_type=F32)
            return carry

        lax.fori_loop(0, L // QB // side, step, 0)
        out_ref[1] = dk_acc[...].astype(BF)
        out_ref[2] = dv_acc[...].astype(BF)

    per = 4 // heads
    width = heads * HEAD_DIM
    row = pl.BlockSpec((L, width), lambda r, j: (0, r * per + j))
    operands = [view, dy, y, lj]
    scratch = [pltpu.VMEM((L, width), F32), pltpu.VMEM((L, width), F32)]
    if d == 1:
        return ORDER.call(
            body, operands + [dqkv3], [blocks_spec, row, row, row, pl.BlockSpec(memory_space=pl.ANY)],
            name=f"attn_a_bwd_d{d}", grid=(d, per), out_specs=blocks_spec, out_shape=_sds((3, S, QKV_W), BF),
            scratch_shapes=scratch, input_output_aliases={4: 0}, compiler_params=_cparams(("parallel", "parallel")))
    return ORDER.call(
        body, operands, [blocks_spec, row, row, row], name=f"attn_a_bwd_d{d}", grid=(d, per),
        out_specs=blocks_spec, out_shape=_sds((3, L, d * 512), BF),
        scratch_shapes=scratch, compiler_params=_cparams(("parallel", "parallel")))


def _toeplitz_onehot():
    oh = np.zeros((64, GRID_W, 128), np.float32)
    for qc in range(GRID_W):
        for m in range(128):
            kc = m % GRID_W
            dc = int(np.clip(kc - qc, -(NA_COLS - 1), NA_COLS - 1)) + NA_COLS - 1
            oh[(m // GRID_W) * 32 + dc, qc, m] = 1.0
    return oh.reshape(64, GRID_W * 128)


def _nbr_scores(qkv_ref, e2_ref, r, rows, ok):
    rs = jnp.clip(r - NA_ROWS // 2, 0, rows - NA_ROWS)
    q0 = pl.multiple_of(r * GRID_W, GRID_W)
    k0 = pl.multiple_of(rs * GRID_W, GRID_W)
    q = qkv_ref[0, pl.ds(q0, GRID_W), :]
    k = qkv_ref[1, pl.ds(k0, NA_ROWS * GRID_W), :]
    v = qkv_ref[2, pl.ds(k0, NA_ROWS * GRID_W), :]
    s = lax.dot_general(q, k, (NT, ((), ())), preferred_element_type=F32) * SCALE
    first = rs - r + NA_ROWS - 1
    bias = jnp.concatenate([e2_ref[first + 2 * pair] for pair in range(NA_ROWS // 2)], axis=1)
    s = jnp.where(ok, s + bias, NEG)
    return q0, k0, first, q, k, v, s


def _nbr_col_ok():
    qc = lax.broadcasted_iota(jnp.int32, (GRID_W, NA_ROWS * GRID_W), 0)
    kc = lax.broadcasted_iota(jnp.int32, (GRID_W, NA_ROWS * GRID_W), 1) % GRID_W
    cs = jnp.clip(qc - NA_COLS // 2, 0, GRID_W - NA_COLS)
    return (kc >= cs) & (kc < cs + NA_COLS)


def _attn_b_fwd(qkv3, e2):
    _, S, _ = qkv3.shape
    rows = S // GRID_W
    assert rows >= NA_ROWS

    def body(qkv_ref, e2_ref, o_ref, lse_ref):
        ok = _nbr_col_ok()

        def step(i, carry):
            blocks = [_nbr_scores(qkv_ref, e2_ref, NBR_SIDE * i + u, rows, ok) for u in range(NBR_SIDE)]
            soft = []
            for q0, _, _, _, _, v, s in blocks:
                m = jnp.max(s, axis=-1, keepdims=True)
                p = jnp.exp(s - m)
                den = jnp.sum(p, axis=-1, keepdims=True)
                soft.append((q0, (p / den).astype(BF), v, m + jnp.log(den)))
            for q0, pn, v, lse in soft:
                o_ref[pl.ds(q0, GRID_W), :] = jnp.dot(pn, v, preferred_element_type=F32).astype(BF)
                lse_ref[pl.ds(q0, GRID_W), :] = jnp.broadcast_to(lse, (GRID_W, HEAD_DIM))
            return carry

        lax.fori_loop(0, rows // NBR_SIDE, step, 0)

    out = pl.BlockSpec((S, HEAD_DIM), lambda h: (0, h))
    return ORDER.call(
        body, [qkv3, e2],
        [pl.BlockSpec((3, S, HEAD_DIM), lambda h: (0, 0, N_HEADS_A + h)),
         pl.BlockSpec((None, RPB_ROWS - 1, GRID_W, 128), lambda h: (h, 0, 0, 0))],
        name="attn_b_fwd", grid=(4,),
        out_specs=[out, out], out_shape=[_sds((S, 512), BF), _sds((S, 512), F32)],
        compiler_params=_cparams(("parallel",)),
    )


def _attn_b_bwd(qkv3, e2, dy, y, lse, dqkv3):
    _, S, _ = qkv3.shape
    rows = S // GRID_W
    nk = NA_ROWS * GRID_W

    def body(qkv_ref, e2_ref, dy_ref, y_ref, lse_ref, _, out_ref, de2_ref, dk_acc, dv_acc):
        ok = _nbr_col_ok()
        dk_acc[...] = jnp.zeros_like(dk_acc)
        dv_acc[...] = jnp.zeros_like(dv_acc)
        de2_ref[...] = jnp.zeros_like(de2_ref)

        def step(i, carry):
            blocks = [_nbr_scores(qkv_ref, e2_ref, NBR_SIDE * i + u, rows, ok) for u in range(NBR_SIDE)]
            dys = [dy_ref[pl.ds(b[0], GRID_W), :] for b in blocks]
            dps = [lax.dot_general(dyv, b[5], (NT, ((), ())), preferred_element_type=F32) for dyv, b in zip(dys, blocks)]
            grads = []
            for (q0, k0, first, q, k, v, s), dyv, dp in zip(blocks, dys, dps):
                qrows = pl.ds(q0, GRID_W)
                delta = jnp.sum(dyv.astype(F32) * y_ref[qrows, :].astype(F32), axis=-1, keepdims=True)
                p = jnp.where(ok, jnp.exp(s - jnp.tile(lse_ref[qrows, :], (1, nk // HEAD_DIM))), 0.0)
                ds = p * (dp - delta)
                for pair in range(NA_ROWS // 2):
                    de2_ref[first + 2 * pair] += ds[:, pair * 128:(pair + 1) * 128]
                grads.append((ds.astype(BF), p.astype(BF)))
            for (q0, k0, first, q, k, v, s), dyv, (dsb, pb) in zip(blocks, dys, grads):
                out_ref[0, pl.ds(q0, GRID_W), :] = (jnp.dot(dsb, k, preferred_element_type=F32) * SCALE).astype(BF)
                keys = pl.ds(k0, nk)
                dk_acc[keys, :] += lax.dot_general(dsb, q, (TN, ((), ())), preferred_element_type=F32) * SCALE
                dv_acc[keys, :] += lax.dot_general(pb, dyv, (TN, ((), ())), preferred_element_type=F32)
            return carry

        lax.fori_loop(0, rows // NBR_SIDE, step, 0)
        out_ref[1] = dk_acc[...].astype(BF)
        out_ref[2] = dv_acc[...].astype(BF)

    heads = pl.BlockSpec((3, S, HEAD_DIM), lambda h: (0, 0, N_HEADS_A + h))
    row = pl.BlockSpec((S, HEAD_DIM), lambda h: (0, h))
    table = pl.BlockSpec((None, RPB_ROWS - 1, GRID_W, 128), lambda h: (h, 0, 0, 0))
    return ORDER.call(
        body, [qkv3, e2, dy, y, lse, dqkv3],
        [heads, table, row, row, row, pl.BlockSpec(memory_space=pl.ANY)], name="attn_b_bwd", grid=(4,),
        out_specs=[heads, table],
        out_shape=[_sds((3, S, QKV_W), BF), _sds((4, RPB_ROWS - 1, GRID_W, 128), F32)],
        scratch_shapes=[pltpu.VMEM((S, HEAD_DIM), F32), pltpu.VMEM((S, HEAD_DIM), F32)],
        input_output_aliases={5: 0},
        compiler_params=_cparams(("parallel",)), chain_output=1,
    )


def _rpb_to_table(rpb):
    pad = jnp.pad(rpb, ((0, 0), (0, 0), (0, 1)))
    pairs = jnp.concatenate([pad[:, :-1], pad[:, 1:]], axis=-1).reshape(4 * (RPB_ROWS - 1), 64)
    onehot = jnp.asarray(_toeplitz_onehot())
    n = onehot.shape[1]
    tn = 2048
    full = lambda i, j, k: (0, 0)
    (e2,) = _matmul("rpb_table", pairs, onehot, pl.BlockSpec(pairs.shape, full),
                    pl.BlockSpec((64, tn), lambda i, j, k: (0, j)), NN, (1, n // tn, 1), (pairs.shape[0], tn), [],
                    [(_sds((pairs.shape[0], n), F32), pl.BlockSpec((pairs.shape[0], tn), lambda i, j, k: (0, j)))],
                    _store(F32), precision=lax.Precision.HIGHEST)
    return e2.reshape(4, RPB_ROWS - 1, GRID_W, 128)


def _table_grad_to_rpb(de2):
    onehot = jnp.asarray(_toeplitz_onehot())
    n = onehot.shape[1]
    flat = de2.reshape(4 * (RPB_ROWS - 1), n)
    tk = 2048
    (dpairs,) = _matmul("rpb_table_grad", flat, onehot, pl.BlockSpec((flat.shape[0], tk), lambda i, j, k: (0, k)),
                        pl.BlockSpec((64, tk), lambda i, j, k: (0, k)), NT, (1, 1, n // tk), (flat.shape[0], 64), [],
                        [(_sds((flat.shape[0], 64), F32), pl.BlockSpec((flat.shape[0], 64), lambda i, j, k: (0, 0)))],
                        _store(F32), precision=lax.Precision.HIGHEST)
    dpairs = dpairs.reshape(4, RPB_ROWS - 1, 64)
    zero = jnp.zeros((4, 1, RPB_COLS), F32)
    return (jnp.concatenate([dpairs[:, :, :RPB_COLS], zero], axis=1)
            + jnp.concatenate([zero, dpairs[:, :, 32:32 + RPB_COLS]], axis=1))


HBM = pl.BlockSpec(memory_space=pl.ANY)


def _place():
    x, y, c = lax.axis_index("x"), lax.axis_index("y"), lax.axis_index("c")
    chips = [(1 - x, y), (x, 1 - y), (1 - x, 1 - y)]
    return x, y, c, chips


def _remote(src, dst, send_sem, recv_sem, to):
    return pltpu.make_async_remote_copy(src_ref=src, dst_ref=dst, send_sem=send_sem, recv_sem=recv_sem,
                                        device_id=to, device_id_type=MESH)


def _place_shard(name, w, me, plain=False):
    R, C = w.shape
    tr = _tile(R, 256)

    def body(me_ref, w_ref, *o_refs):
        for o_ref in o_refs:
            o_ref[...] = w_ref[...].astype(BF)

    row = pl.BlockSpec((tr, C), lambda i, mr: (i, 0))
    placed = pl.BlockSpec((None, tr, C), lambda i, mr: (mr[0], i, 0))
    return ORDER.call(
        body, [w], [row], prefetch=(me,), name=name, grid=(R // tr,),
        out_specs=[placed, row] if plain else [placed],
        out_shape=[_sds((N_CHIPS, R, C), BF)] + ([_sds((R, C), BF)] if plain else []),
        compiler_params=_cparams(("parallel",)),
    )


SEM = pl.BlockSpec(memory_space=pltpu.SEMAPHORE)
IN_HBM = pl.BlockSpec(memory_space=pltpu.HBM)
DATAFLOW = pltpu.SideEffectType.DATAFLOW_SIDE_EFFECTING


def _in_hbm(a):
    return pltpu.with_memory_space_constraint(a, pltpu.HBM)


def _copy_start(name, bufs, copies, n_copies, earlier=None):
    n = len(bufs)
    after = None if any(b is ORDER.last for b in bufs) else ORDER.last
    n_extra = (2 if earlier is not None else 0) + (1 if after is not None else 0)

    def body(*refs):
        ins = refs[:n]
        if earlier is not None:
            for k, (src, dst, to) in enumerate(earlier[0](ins)):
                cp = _remote(src, dst, refs[n].at[k], refs[n + 1].at[k], to)
                cp.wait_send()
                cp.wait_recv()
        send_sems, recv_sems = refs[n + n_extra], refs[n + n_extra + 1]
        for k, (src, dst, to) in enumerate(copies(ins)):
            _remote(src, dst, send_sems.at[k], recv_sems.at[k], to).start()
        refs[-1][...] = jnp.zeros((8, 128), F32)

    operands = [_in_hbm(b) for b in bufs]
    in_specs = [IN_HBM] * n
    if earlier is not None:
        operands += [earlier[1], earlier[2]]
        in_specs += [SEM, SEM]
    if after is not None:
        operands.append(after)
        in_specs.append(HBM)
    outs = pl.pallas_call(
        body, name=name,
        out_shape=(pltpu.SemaphoreType.DMA((n_copies,)), pltpu.SemaphoreType.DMA((n_copies,)),
                   *[pltpu.HBM(b.shape, b.dtype) for b in bufs], _sds((8, 128), F32)),
        in_specs=in_specs,
        out_specs=(SEM, SEM, *[IN_HBM] * n, pl.BlockSpec(memory_space=pltpu.VMEM)),
        input_output_aliases={i: 2 + i for i in range(n)},
        compiler_params=pltpu.CompilerParams(has_side_effects=DATAFLOW),
    )(*operands)
    ORDER.last = outs[-1]
    return outs[0], outs[1], list(outs[2:2 + n])


def _copy_wait(name, bufs, copies, send_sems, recv_sems):
    n = len(bufs)
    after = ORDER.last

    def body(*refs):
        ins = refs[:n]
        for k, (src, dst, to) in enumerate(copies(ins)):
            cp = _remote(src, dst, refs[n].at[k], refs[n + 1].at[k], to)
            cp.wait_send()
            cp.wait_recv()

    outs = list(pl.pallas_call(
        body, name=name,
        out_shape=tuple(pltpu.HBM(b.shape, b.dtype) for b in bufs),
        in_specs=[IN_HBM] * n + [SEM, SEM, HBM], out_specs=tuple([IN_HBM] * n),
        input_output_aliases={i: i for i in range(n)},
        compiler_params=pltpu.CompilerParams(has_side_effects=DATAFLOW),
    )(*bufs, send_sems, recv_sems, after))
    ORDER.last = outs[0]
    return outs


def _gather_hop1(bufs):
    x, y, c, chips = _place()
    out = []
    for b in bufs:
        half = b.shape[1] // 2
        mine = b.at[2 * x + y, pl.ds(c * half, half), :]
        out += [(mine, mine, (*chip, c)) for chip in chips]
    return out


def _gather_hop2(bufs):
    x, y, c, chips = _place()
    out = []
    for b in bufs:
        half = b.shape[1] // 2
        for chip in chips:
            landed = b.at[2 * chip[0] + chip[1], pl.ds(c * half, half), :]
            out.append((landed, landed, (x, y, 1 - c)))
    return out


def _swap_copies(bufs):
    x, y, c, _ = _place()
    n = len(bufs) // 2
    out = []
    for p, land in zip(bufs[:n], bufs[n:]):
        half = p.shape[1] // 2
        out.append((p.at[:, pl.ds((1 - c) * half, half), :], land, (x, y, 1 - c)))
    return out


def _scatter_copies(bufs):
    _, _, c, chips = _place()
    n = len(bufs) // 2
    out = []
    for s_, land in zip(bufs[:n], bufs[n:]):
        out += [(s_.at[2 * chip[0] + chip[1]], land.at[j], (*chip, c)) for j, chip in enumerate(chips)]
    return out


def _join_copies(bufs):
    x, y, c, _ = _place()
    out = []
    for b in bufs:
        half = b.shape[0] // 2
        mine = b.at[pl.ds(c * half, half), :]
        out.append((mine, mine, (x, y, 1 - c)))
    return out


def _gather_small(vec):
    m_per, n = vec.shape

    def body(x_ref, out_ref, send_sems, recv_sems, local_sem):
        x, y, c, chips = _place()
        me, sibling = (x, y, c), (x, y, 1 - c)

        def rows(px, py, pc):
            return out_ref.at[pl.ds((4 * px + 2 * py + pc) * m_per, m_per), :]

        def copy(k, block, to, src=None):
            return _remote(rows(*block) if src is None else src, rows(*block), send_sems.at[k], recv_sems.at[k], to)

        mine = pltpu.make_async_copy(x_ref, rows(*me), local_sem)
        mine.start()
        first = [copy(0, me, sibling, src=x_ref)]
        first += [copy(1 + j, me, (*chip, c), src=x_ref) for j, chip in enumerate(chips)]
        for cp in first:
            cp.start()
        passed = [copy(4 + j, (*chip, c), sibling) for j, chip in enumerate(chips)]
        for j, chip in enumerate(chips):
            copy(1 + j, (*chip, c), me).wait_recv()
            passed[j].start()
        copy(0, sibling, me).wait_recv()
        for j, chip in enumerate(chips):
            copy(4 + j, (*chip, 1 - c), me).wait_recv()
        for cp in first + passed:
            cp.wait_send()
        mine.wait()

    return ORDER.call(
        body, [vec], [pl.BlockSpec(memory_space=pltpu.VMEM)], name="gather_small_grads",
        out_shape=_sds((8 * m_per, n), vec.dtype), out_specs=pl.BlockSpec(memory_space=pltpu.VMEM),
        scratch_shapes=[pltpu.SemaphoreType.DMA((7,)), pltpu.SemaphoreType.DMA((7,)), pltpu.SemaphoreType.DMA],
    )


def _add_sibling(name, partial, received, c):
    _, R, C = partial.shape
    half = R // 2
    tr = _tile(half, 256)
    nb = half // tr

    def body(c_ref, p_ref, r_ref, o_ref):
        o_ref[...] = (p_ref[...].astype(F32) + r_ref[...].astype(F32)).astype(BF)

    return ORDER.call(
        body, [partial, received],
        [pl.BlockSpec((None, tr, C), lambda j, i, cr: (j, cr[0] * nb + i, 0)),
         pl.BlockSpec((None, tr, C), lambda j, i, cr: (j, i, 0))],
        prefetch=(c,), name=name, grid=(N_CHIPS, nb),
        out_specs=pl.BlockSpec((None, tr, C), lambda j, i, cr: (j, i, 0)),
        out_shape=_sds((N_CHIPS, half, C), BF), compiler_params=_cparams(("parallel", "parallel")),
    )


def _add_chips(name, sums, received, me_c):
    _, half, C = sums.shape
    tr = _tile(half, 256)
    nb = half // tr

    def body(mc_ref, s_ref, r_ref, o_ref):
        acc = s_ref[...].astype(F32)
        for j in range(3):
            acc = acc + r_ref[j].astype(F32)
        o_ref[...] = acc

    return ORDER.call(
        body, [sums, received],
        [pl.BlockSpec((None, tr, C), lambda i, mc: (mc[0], i, 0)),
         pl.BlockSpec((3, tr, C), lambda i, mc: (0, i, 0))],
        prefetch=(me_c,), name=name, grid=(nb,),
        out_specs=pl.BlockSpec((tr, C), lambda i, mc: (mc[1] * nb + i, 0)),
        out_shape=_sds((2 * half, C), F32), compiler_params=_cparams(("parallel",)),
    )


def _adamw_math(w, g, m, v):
    m = ADAM_B1 * m + (1.0 - ADAM_B1) * g
    v = ADAM_B2 * v + (1.0 - ADAM_B2) * (g * g)
    m_hat = m / (1.0 - ADAM_B1 ** ADAM_STEP)
    v_hat = v / (1.0 - ADAM_B2 ** ADAM_STEP)
    delta = -ADAM_LR * (m_hat / (jnp.sqrt(v_hat) + ADAM_EPS) + ADAM_WD * w)
    return delta, m, v


def _adamw(name, w, g, m, v):
    R, C = w.shape
    tr = _tile(R, 256)

    def body(w_ref, g_ref, m_ref, v_ref, go_ref, d_ref, mo_ref, vo_ref):
        gv = g_ref[...]
        go_ref[...] = gv
        d_ref[...], mo_ref[...], vo_ref[...] = _adamw_math(w_ref[...], gv, m_ref[...], v_ref[...])

    row = pl.BlockSpec((tr, C), lambda i: (i, 0))
    return ORDER.call(
        body, [w, g, m, v], [row] * 4, name=name, grid=(R // tr,), out_specs=[row] * 4,
        out_shape=[_sds((R, C), F32)] * 4, compiler_params=_cparams(("parallel",)), chain_output=1,
    )


def _adamw_small(gathered, w, m, v):
    rows, n = w.shape

    def body(ga_ref, w_ref, m_ref, v_ref, go_ref, d_ref, mo_ref, vo_ref):
        g = ga_ref[pl.ds(0, rows), :]
        for dev in range(1, 8):
            g = g + ga_ref[pl.ds(dev * rows, rows), :]
        go_ref[...] = g
        d_ref[...], mo_ref[...], vo_ref[...] = _adamw_math(w_ref[...], g, m_ref[...], v_ref[...])

    whole = pl.BlockSpec(memory_space=pltpu.VMEM)
    return ORDER.call(
        body, [gathered, w, m, v], [whole] * 4, name="adamw_small", out_specs=[whole] * 4,
        out_shape=[_sds((rows, n), F32)] * 4, compiler_params=_cparams(), chain_output=1,
    )


def _proj_merge(y_a, y_b, gpa, gpb, g3):
    S, K = y_a.shape
    _, _, Nq = gpa.shape
    D = N_CHIPS * Nq
    tm, tn = _tile(S, 1024), _tile(Nq, 512)
    q = Nq // tn

    def body(ya_ref, yb_ref, wa_ref, wb_ref, g_ref, merged_ref, c_ref):
        pa = jnp.dot(ya_ref[...], wa_ref[...], preferred_element_type=F32)
        pb = jnp.dot(yb_ref[...], wb_ref[...], preferred_element_type=F32)
        g = g_ref[...].astype(F32)
        merged_ref[...] = (g[0] * pa + g[1] * pb).astype(BF)
        c_ref[0] = (pa * g[0] * (1.0 - g[0])).astype(BF)
        c_ref[1] = (pb * g[1] * (1.0 - g[1])).astype(BF)

    rows = pl.BlockSpec((tm, K), lambda i, j: (i, 0))
    weight = pl.BlockSpec((None, K, tn), lambda i, j: (j // q, 0, j % q))
    pair = pl.BlockSpec((2, tm, tn), lambda i, j: (0, i, j))
    return ORDER.call(
        body, [y_a, y_b, gpa, gpb, g3], [rows, rows, weight, weight, pair], name="proj_merge",
        grid=(S // tm, N_CHIPS * q), out_specs=[pl.BlockSpec((tm, tn), lambda i, j: (i, j)), pair],
        out_shape=[_sds((S, D), BF), _sds((2, S, D), BF)], compiler_params=_cparams(("parallel", "parallel")))


class _Exchange:
    GATHER = (("qkv",), ("gate",), ("proj_a", "proj_b", "out"), ("up",), ("down",))
    REDUCE = {"mlp": ("down", "up"), "mix": ("out", "proj_a", "proj_b"), "in": ("qkv", "gate")}

    OWN_FIRST = ("qkv", "gate")

    def __init__(self, shards, me, c):
        self.me, self.c = me, c
        self.hop1, self.hop2, self.stage, self.grads, self.own = {}, {}, {}, {}, {}
        for g, names in enumerate(self.GATHER):
            bufs = []
            for n in names:
                placed = _place_shard(f"place_{n}", shards[n], me, plain=n in self.OWN_FIRST)
                bufs.append(placed[0])
                if n in self.OWN_FIRST:
                    self.own[n] = placed[1]
            self.hop1[g] = _copy_start(f"gather{g}_start", bufs, _gather_hop1, 3 * len(names))

    def forward(self, g):
        send, recv, thru = self.hop1.pop(g)
        self.hop2[g] = _copy_start(f"gather{g}_forward", thru, _gather_hop2, len(thru) * 3,
                                   earlier=(_gather_hop1, send, recv))

    def weights(self, g):
        send, recv, thru = self.hop2.pop(g)
        return _copy_wait(f"gather{g}_wait", thru, _gather_hop2, send, recv)

    def reduce(self, key, partials=None):
        names = self.REDUCE[key]
        n = len(names)
        if partials is not None:
            lands = [lax.empty((p.shape[0], p.shape[1] // 2, p.shape[2]), p.dtype) for p in partials]
            self.stage[key] = ("swap",) + _copy_start(f"reduce_{key}_swap", list(partials) + lands, _swap_copies, n)
            return
        kind, send, recv, thru = self.stage.pop(key)
        if kind == "swap":
            thru = _copy_wait(f"reduce_{key}_swap_wait", thru, _swap_copies, send, recv)
            sums = [_add_sibling(f"reduce_{nm}_add_sibling", p, r, self.c)
                    for nm, p, r in zip(names, thru[:n], thru[n:])]
            lands = [lax.empty((3,) + s_.shape[1:], s_.dtype) for s_ in sums]
            self.stage[key] = ("scatter",) + _copy_start(f"reduce_{key}_scatter", sums + lands, _scatter_copies, 3 * n)
        elif kind == "scatter":
            thru = _copy_wait(f"reduce_{key}_scatter_wait", thru, _scatter_copies, send, recv)
            me_c = jnp.concatenate([self.me, self.c])
            halves = [_add_chips(f"reduce_{nm}_add_chips", s_, r, me_c)
                      for nm, s_, r in zip(names, thru[:n], thru[n:])]
            self.stage[key] = ("join",) + _copy_start(f"reduce_{key}_join", halves, _join_copies, n)
        else:
            thru = _copy_wait(f"reduce_{key}_join_wait", thru, _join_copies, send, recv)
            self.grads.update(zip(names, thru))


def _forward_backward(x, target, norm_mix, b_gate, rpb, norm_mlp, norm_final, ex):
    S, D = x.shape

    h1 = _rms_fwd("rms_mix", x, norm_mix)
    nq = QKV_W // 512
    qkv_out = (((3, S, QKV_W), BF), lambda i, T: (T // nq, i, T % nq))
    tg = _tile(ex.own["gate"].shape[1], 1024)
    ng = D // tg
    gate_out = (((2, S, D), BF), lambda i, T: (T // ng, i, T % ng))

    def gate_epilogue(acc, ex_, outs):
        outs[0][...] = jax.nn.sigmoid(acc + ex_[0][...]).astype(BF)

    qkv3 = _mm_nn_shards("qkv_own", h1, ex.own["qkv"], ex.me, True, *qkv_out, _store(BF))
    g3 = _mm_nn_shards("gate_own", h1, ex.own["gate"], ex.me, True, *gate_out, gate_epilogue, extras=[b_gate], tn=tg)
    ex.forward(0)
    e2 = _rpb_to_table(rpb)
    (gq,) = ex.weights(0)
    qkv3 = _mm_nn_shards("qkv", h1, gq, ex.me, False, *qkv_out, _store(BF), into=qkv3)

    ex.forward(1)
    outs_a = [_attn_a_fwd(qkv3, 0, DILATIONS[0])]
    (gg,) = ex.weights(1)
    g3 = _mm_nn_shards("gate", h1, gg, ex.me, False, *gate_out, gate_epilogue, extras=[b_gate], into=g3, tn=tg)

    ex.forward(2)
    qkv_views = _qkv_views("qkv_views", qkv3)
    outs_a += [_attn_a_fwd(qkv_views[d], grp, d) for grp, d in enumerate(DILATIONS) if grp > 0]
    y_a, lj = _attn_a_combine([o for o, _ in outs_a], [l for _, l in outs_a])
    y_b, lse_b = _attn_b_fwd(qkv3, e2)
    gpa, gpb, gout = ex.weights(2)
    wout = gout.reshape(D, D)
    merged, c3 = _proj_merge(y_a, y_b, gpa, gpb, g3)

    def residual_epilogue(acc, ex_, outs):
        outs[0][...] = acc + ex_[0][...]

    def nn_plain(name, a, w, res, bm=1024, bn=1024):
        M, K = a.shape
        N = w.shape[1]
        bm, bn, bk = _tile(M, bm), _tile(N, bn), _tile(K, 2048)
        t = pl.BlockSpec((bm, bn), lambda i, j, k: (i, j))
        return _matmul(name, a, w, pl.BlockSpec((bm, bk), lambda i, j, k: (i, k)),
                       pl.BlockSpec((bk, bn), lambda i, j, k: (k, j)), NN, (M // bm, N // bn, K // bk), (bm, bn),
                       [(res, t)], [(_sds((M, N), F32), t)], residual_epilogue)[0]

    ex.forward(3)
    x1 = nn_plain("out_proj", merged, wout, x, bm=512, bn=2048)
    h2 = _rms_fwd("rms_mlp", x1, norm_mlp)
    (gup,) = ex.weights(3)
    F = gup.shape[2] * N_CHIPS

    def up_epilogue(acc, ex_, outs):
        ru = jnp.maximum(acc, 0.0)
        outs[0][...] = (ru * ru).astype(BF)
        outs[1][...] = ru.astype(BF)

    tu = _tile(gup.shape[2], 2048)
    ut = pl.BlockSpec((_tile(S, 1024), tu), lambda i, j, k: (i, j))
    (act, ru), _ = _mm_nn_cols("mlp_up", h2, gup, BF, epilogue=up_epilogue, tn=tu,
                               outs=[(_sds((S, F), BF), ut), (_sds((S, F), BF), ut)])
    ex.forward(4)
    (gdown,) = ex.weights(4)
    wdown = gdown.reshape(F, D)
    x2 = nn_plain("mlp_down", act, wdown, x1)

    loss, dx2, dx2b, d_norm_final = _loss_head(x2, target, norm_final.reshape(1, D))

    def nt_rows(name, a, w, epilogue, extras, outs, bn=1024):
        M, N = a.shape
        K = w.shape[0]
        bm, bn, bk = _tile(M, 1024), _tile(K, bn), _tile(N, 2048)
        return _matmul(name, a, w, pl.BlockSpec((bm, bk), lambda i, j, k: (i, k)),
                       pl.BlockSpec((bn, bk), lambda i, j, k: (j, k)), NT, (M // bm, K // bn, N // bk), (bm, bn),
                       extras(bm, bn), outs(bm, bn), epilogue)

    def nt_cols(name, a_spec_fn, a, g, M, epilogue, extras, outs, bk, bn=1024):
        _, K, Nq = g.shape
        bm, bn, bk = _tile(M, 1024), _tile(K, bn), _tile(Nq, bk)
        q = Nq // bk
        return _matmul(name, a, g, a_spec_fn(bm, bk), pl.BlockSpec((None, bn, bk), lambda i, j, k: (k // q, j, k % q)),
                       NT, (M // bm, K // bn, N_CHIPS * q), (bm, bn), extras(bm, bn), outs(bm, bn), epilogue)

    def tn_grad(name, a, a_spec_fn, b, b_spec_fn, Kin, N, out_shape, out_spec_fn, bn=1024):
        bm, bn, bk = _tile(Kin, 1024), _tile(N, bn), _tile(S, 4096)
        return _matmul(name, a, b, a_spec_fn(bk, bm), b_spec_fn(bk, bn), TN, (Kin // bm, N // bn, S // bk), (bm, bn),
                       [], [(_sds(out_shape, BF), out_spec_fn(bm, bn))], _store(BF))[0]

    plain_a = lambda bk, bm: pl.BlockSpec((bk, bm), lambda i, j, k: (k, i))
    plain_b = lambda bk, bn: pl.BlockSpec((bk, bn), lambda i, j, k: (k, j))
    plain_o = lambda bm, bn: pl.BlockSpec((bm, bn), lambda i, j, k: (i, j))
    a_rows = lambda bm, bk: pl.BlockSpec((bm, bk), lambda i, j, k: (i, k))

    def cols_o(Nq):
        def spec(bm, bn):
            q = Nq // bn
            return pl.BlockSpec((None, bm, bn), lambda i, j, k: (j // q, i, j % q))
        return spec

    def du_epilogue(acc, ex_, outs):
        outs[0][...] = (acc * (2.0 * ex_[0][...].astype(F32))).astype(BF)

    dw_down = tn_grad("mlp_down_dw", act, plain_a, dx2b, plain_b, F, D, (F, D), plain_o)
    (du,) = nt_rows("mlp_down_dx", dx2b, wdown, du_epilogue,
                    lambda bm, bn: [(ru, plain_o(bm, bn))], lambda bm, bn: [(_sds((S, F), BF), plain_o(bm, bn))],
                    bn=2048)

    fq = gup.shape[2]
    dw_up = tn_grad("mlp_up_dw", h2, plain_a, du, plain_b, D, F, (N_CHIPS, D, fq), cols_o(fq), bn=min(fq, 1024))
    ex.reduce("mlp", partials=[dw_down.reshape(N_CHIPS, F // N_CHIPS, D), dw_up])
    (dh2,) = nt_cols("mlp_up_dx", a_rows, du, gup, S, _store(F32), lambda bm, bn: [],
                     lambda bm, bn: [(_sds((S, D), F32), plain_o(bm, bn))], 1024, bn=2048)
    ex.reduce("mlp")
    dx1, dx1b, d_norm_mlp = _rms_bwd("rms_mlp_bwd", dh2, x1, norm_mlp, dx2)

    def merge_bwd_epilogue(acc, ex_, outs):
        g, c = ex_[0][...].astype(F32), ex_[1][...].astype(F32)
        outs[0][...] = (acc * g[0]).astype(BF)
        outs[1][...] = (acc * g[1]).astype(BF)
        dga = acc * c[0]
        dgb = acc * c[1]
        outs[2][0] = dga.astype(BF)
        outs[2][1] = dgb.astype(BF)
        outs[3][...] = jnp.concatenate([jnp.sum(dga, axis=0, keepdims=True), jnp.sum(dgb, axis=0, keepdims=True)], 0)

    def pair(bm, bn):
        return pl.BlockSpec((2, bm, bn), lambda i, j, k: (0, i, j))

    n_row_blocks = S // _tile(S, 1024)
    dpa, dpb, dg3, db_gate = nt_rows(
        "out_proj_dx", dx1b, wout, merge_bwd_epilogue,
        lambda bm, bn: [(g3, pair(bm, bn)), (c3, pair(bm, bn))],
        lambda bm, bn: [(_sds((S, D), BF), plain_o(bm, bn)), (_sds((S, D), BF), plain_o(bm, bn)),
                        (_sds((2, S, D), BF), pair(bm, bn)),
                        (_sds((n_row_blocks, 2, D), F32), pl.BlockSpec((None, 2, bn), lambda i, j, k: (i, 0, j)))],
        bn=512)
    dw_out = tn_grad("out_proj_dw", merged, plain_a, dx1b, plain_b, D, D, (D, D), plain_o)

    pq = gpa.shape[2]
    proj_dx = lambda name, dproj, g: nt_cols(name, a_rows, dproj, g, S, _store(BF), lambda bm, bn: [],
                                             lambda bm, bn: [(_sds((S, 512), BF), plain_o(bm, bn))], 512)[0]
    dw_pa = tn_grad("proj_a_dw", y_a, plain_a, dpa, plain_b, 512, D, (N_CHIPS, 512, pq), cols_o(pq), bn=min(pq, 512))
    dw_pb = tn_grad("proj_b_dw", y_b, plain_a, dpb, plain_b, 512, D, (N_CHIPS, 512, pq), cols_o(pq), bn=min(pq, 512))
    ex.reduce("mix", partials=[dw_out.reshape(N_CHIPS, D // N_CHIPS, D), dw_pa, dw_pb])
    dy_a = proj_dx("proj_a_dx", dpa, gpa)
    dy_b = proj_dx("proj_b_dx", dpb, gpb)

    dqkv3 = lax.empty((3, S, QKV_W), BF)
    dqkv3 = _attn_a_bwd(qkv3, dy_a, y_a, lj, dqkv3, 0, DILATIONS[0])
    ex.reduce("mix")
    dy_views, y_views, lj_views = _dilated_rows("attn_a_bwd_rows", [dy_a, y_a, lj])
    dqkv_views = {d: _attn_a_bwd(qkv_views[d], dy_views[d], y_views[d], lj_views[d], None, grp, d)
                  for grp, d in enumerate(DILATIONS) if grp > 0}
    dqkv3 = _qkv_views("dqkv_from_views", dqkv3, dqkv_views)
    dqkv3, de2 = _attn_b_bwd(qkv3, e2, dy_b, y_b, lse_b, dqkv3)
    d_rpb = _table_grad_to_rpb(de2)

    def stacked_a(width):
        def spec(bm, bk):
            q = width // bk
            return pl.BlockSpec((None, bm, bk), lambda i, j, k: (k // q, i, k % q))
        return spec

    def stacked_b(width):
        def spec(bk, bn):
            q = width // bn
            return pl.BlockSpec((None, bk, bn), lambda i, j, k: (j // q, k, j % q))
        return spec

    dw_qkv = tn_grad("qkv_dw", h1, plain_a, dqkv3, stacked_b(QKV_W), D, 3 * QKV_W, (N_CHIPS,) + gq.shape[1:],
                     cols_o(gq.shape[2]), bn=512)
    dw_gate = tn_grad("gate_dw", h1, plain_a, dg3, stacked_b(D), D, 2 * D, (N_CHIPS,) + gg.shape[1:],
                      cols_o(gg.shape[2]), bn=gg.shape[2])
    ex.reduce("in", partials=[dw_qkv, dw_gate])
    ex.reduce("mlp")
    (dh1_q,) = nt_cols("qkv_dx", stacked_a(QKV_W), dqkv3, gq, S, _store(F32), lambda bm, bn: [],
                       lambda bm, bn: [(_sds((S, D), F32), plain_o(bm, bn))], 512, bn=2048)
    ex.reduce("in")
    ex.reduce("mix")

    def add_epilogue(acc, ex_, outs):
        outs[0][...] = acc + ex_[0][...]

    (dh1,) = nt_cols("gate_dx", stacked_a(D), dg3, gg, S, add_epilogue, lambda bm, bn: [(dh1_q, plain_o(bm, bn))],
                     lambda bm, bn: [(_sds((S, D), F32), plain_o(bm, bn))], gg.shape[2])
    grad_x, _, d_norm_mix = _rms_bwd("rms_mix_bwd", dh1, x, norm_mix, dx1)
    ex.reduce("mlp")
    ex.reduce("mix")

    small = [d_norm_mix, jnp.sum(db_gate, axis=0).reshape(1, 2 * D), d_rpb, d_norm_mlp, d_norm_final]
    return loss, grad_x, small


def _pack_small(parts, width):
    flat = jnp.concatenate([p.reshape(-1) for p in parts])
    return jnp.pad(flat, (0, 8 * width - flat.shape[0])).reshape(8, width)


def kernel(x, norm_mix, w_qkv, w_gate, b_gate, rpb, w_proj_a, w_proj_b, w_out, norm_mlp, w_up, w_down, norm_final, loss_target, m_norm_mix, m_w_qkv, m_w_gate, m_b_gate, m_rpb, m_w_proj_a, m_w_proj_b, m_w_out, m_norm_mlp, m_w_up, m_w_down, m_norm_final, v_norm_mix, v_w_qkv, v_w_gate, v_b_gate, v_rpb, v_w_proj_a, v_w_proj_b, v_w_out, v_norm_mlp, v_w_up, v_w_down, v_norm_final):
    names = ["qkv", "gate", "proj_a", "proj_b", "out", "up", "down"]
    big = dict(zip(names, [w_qkv[0], w_gate[0], w_proj_a[0], w_proj_b[0], w_out[0], w_up[0], w_down[0]]))
    big_m = dict(zip(names, [m_w_qkv[0], m_w_gate[0], m_w_proj_a[0], m_w_proj_b[0], m_w_out[0], m_w_up[0], m_w_down[0]]))
    big_v = dict(zip(names, [v_w_qkv[0], v_w_gate[0], v_w_proj_a[0], v_w_proj_b[0], v_w_out[0], v_w_up[0], v_w_down[0]]))

    c = lax.axis_index("c").astype(jnp.int32).reshape(1)
    me = (2 * lax.axis_index("x") + lax.axis_index("y")).astype(jnp.int32).reshape(1)
    ORDER.last = None
    ex = _Exchange(big, me, c)
    loss, grad_x, small = _forward_backward(x[0], loss_target[0], norm_mix, b_gate, rpb[0], norm_mlp, norm_final, ex)

    def adamw(group):
        return {n: _adamw(f"adamw_{n}", big[n], ex.grads[n], big_m[n], big_v[n]) for n in _Exchange.REDUCE[group]}

    big_out = {**adamw("mlp"), **adamw("mix")}
    ex.reduce("in")

    small_w = [norm_mix, b_gate, rpb, norm_mlp, norm_final]
    count = sum(int(np.prod(p.shape)) for p in small_w)
    width = -(-count // (8 * 128)) * 128
    packed = _adamw_small(_gather_small(_pack_small(small, width)), _pack_small(small_w, width),
                          _pack_small([m_norm_mix, m_b_gate, m_rpb, m_norm_mlp, m_norm_final], width),
                          _pack_small([v_norm_mix, v_b_gate, v_rpb, v_norm_mlp, v_norm_final], width))
    ex.reduce("in")
    big_out.update(adamw("in"))

    def unpack(flat2d):
        flat, out, at = flat2d.reshape(-1), [], 0
        for p in small_w:
            size = int(np.prod(p.shape))
            out.append(flat[at:at + size].reshape(p.shape))
            at += size
        return out

    small_out = [unpack(a) for a in packed]

    def ordered(kind):
        sm = small_out[kind]
        bg = {n: o[kind][None] for n, o in big_out.items()}
        return [sm[0], bg["qkv"], bg["gate"], sm[1], sm[2], bg["proj_a"], bg["proj_b"], bg["out"], sm[3],
                bg["up"], bg["down"], sm[4]]

    total = lax.psum(loss[0, 0], ("x", "y", "c"))
    return (total, grad_x[None], *ordered(0), *ordered(1), *ordered(2), *ordered(3))
```

```python
import functools
import math

import numpy as np
import jax
import jax.numpy as jnp
from jax import lax
from jax.experimental import pallas as pl
from jax.experimental.pallas import tpu as pltpu

BF = jnp.bfloat16
F32 = jnp.float32
MESH = pl.DeviceIdType.MESH

HEAD_DIM = 128
N_HEADS = 16
N_HEADS_A = 12
QKV_W = N_HEADS * HEAD_DIM
DILATIONS = (1, 4, 16)
HALF_WINDOW = 64
GRID_W = 64
NA_ROWS = 8
NA_COLS = 16
RPB_ROWS = 2 * NA_ROWS - 1
RPB_COLS = 2 * NA_COLS - 1
EPS = 1e-6
NEG = -1e30
SCALE = HEAD_DIM ** -0.5

ADAM_LR = 0.001
ADAM_B1 = 0.9
ADAM_B2 = 0.999
ADAM_EPS = 1e-08
ADAM_WD = 0.01
ADAM_STEP = 10

N_CHIPS = 4
VMEM_LIMIT_BYTES = 48 * 1024 * 1024
QB = 256
NBR_SIDE = 8
ROW_TILE = 512


def _key_rows(L):
    return min(QB + 2 * HALF_WINDOW, L)


def _cparams(sem=None):
    return pltpu.CompilerParams(dimension_semantics=sem, vmem_limit_bytes=VMEM_LIMIT_BYTES)


def _tile(dim, want):
    t = min(dim, want)
    assert dim % t == 0, (dim, want)
    return t


class _ProgramOrder:
    def __init__(self):
        self.last = None

    def call(self, body, operands, in_specs, *, prefetch=(), grid=None, out_specs=None, chain_output=0, **kwargs):
        operands, in_specs = list(operands), list(in_specs)
        lead = len(prefetch) + len(operands)
        if self.last is not None and not any(op is self.last for op in operands):
            operands.append(self.last)
            in_specs.append(pl.BlockSpec(memory_space=pl.ANY))
            inner = body

            def body(*refs):
                return inner(*refs[:lead], *refs[lead + 1:])

        if prefetch:
            kwargs["grid_spec"] = pltpu.PrefetchScalarGridSpec(
                num_scalar_prefetch=len(prefetch), grid=grid, in_specs=in_specs, out_specs=out_specs)
        else:
            kwargs.update(in_specs=in_specs, out_specs=out_specs)
            if grid is not None:
                kwargs["grid"] = grid
        out = pl.pallas_call(body, **kwargs)(*prefetch, *operands)
        self.last = out[chain_output] if isinstance(out, (tuple, list)) else out
        return out


ORDER = _ProgramOrder()


NN = ((1,), (0,))
NT = ((1,), (1,))
TN = ((0,), (0,))


def _matmul(name, a, b, a_spec, b_spec, dims, grid, acc_shape, extras, outs, epilogue, precision=None,
            prefetch=(), into=None, side=None):
    n_ex, n_out, nk = len(extras), len(outs), grid[2]
    side_fn, side_in, n_side_out = side if side is not None else (None, [], 0)
    side_spec = None
    n_in = 2 + n_ex + len(side_in) + (into is not None)
    if side is not None:
        R, C = side_in[0].shape
        steps = grid[0] * grid[1] * grid[2]
        side_blocks = max(n for n in range(1, steps + 1) if R % n == 0 and (R // n) % 8 == 0)

        def side_step(*ids):
            return (ids[0] * grid[1] + ids[1]) * grid[2] + ids[2]

        side_spec = pl.BlockSpec((R // side_blocks, C),
                                 lambda *ids: (jnp.minimum(side_step(*ids), side_blocks - 1), 0))

    def body(*refs):
        refs = refs[len(prefetch):]
        a_ref, b_ref = refs[0], refs[1]
        ex_refs = refs[2:2 + n_ex]
        out_refs = refs[n_in:n_in + n_out]
        if side is not None:
            @pl.when(side_step(pl.program_id(0), pl.program_id(1), pl.program_id(2)) < side_blocks)
            def _():
                results = side_fn(*[r[...] for r in refs[2 + n_ex:2 + n_ex + len(side_in)]])
                for o_ref, value in zip(refs[n_in + n_out:n_in + n_out + n_side_out], results):
                    o_ref[...] = value

        def dot():
            return lax.dot_general(a_ref[...], b_ref[...], (dims, ((), ())),
                                   preferred_element_type=F32, precision=precision)

        if nk == 1:
            epilogue(dot(), ex_refs, out_refs)
            return
        acc_ref = refs[-1]
        k = pl.program_id(2)

        @pl.when(k == 0)
        def _():
            acc_ref[...] = dot()

        if nk > 2:
            @pl.when((k > 0) & (k < nk - 1))
            def _():
                acc_ref[...] += dot()

        @pl.when(k == nk - 1)
        def _():
            epilogue(acc_ref[...] + dot(), ex_refs, out_refs)

    operands = [a, b] + [e for e, _ in extras] + list(side_in)
    in_specs = [a_spec, b_spec] + [s for _, s in extras] + [side_spec] * len(side_in)
    kwargs = {}
    if into is not None:
        operands.append(into)
        in_specs.append(pl.BlockSpec(memory_space=pl.ANY))
        kwargs["input_output_aliases"] = {len(prefetch) + n_in - 1: 0}
    return ORDER.call(
        body, operands, in_specs, prefetch=prefetch, name=name, grid=grid,
        out_specs=[s for _, s in outs] + [side_spec] * n_side_out,
        out_shape=[sh for sh, _ in outs] + [_sds(s_.shape, F32) for s_ in side_in[:1]] * n_side_out,
        scratch_shapes=[pltpu.VMEM(acc_shape, F32)] if nk > 1 else [],
        compiler_params=_cparams(("parallel", "parallel", "arbitrary")), **kwargs,
    )


def _mm_nn_shards(name, a, w, me, own, out, out_block, epilogue, extras=(), into=None, tn=512):
    M, K = a.shape
    Nq = w.shape[-1]
    tm, tn = _tile(M, 1024), _tile(Nq, tn)
    q = Nq // tn

    def tile(j, me_ref):
        shard = me_ref[0] if own else (me_ref[0] + 1 + j // q) % N_CHIPS
        return shard, j % q, shard * q + j % q

    if own:
        b_spec = pl.BlockSpec((K, tn), lambda i, j, k, me_ref: (0, j))
    else:
        b_spec = pl.BlockSpec((None, K, tn), lambda i, j, k, me_ref: (tile(j, me_ref)[0], 0, tile(j, me_ref)[1]))
    shape, dtype = out
    out_spec = pl.BlockSpec((None, tm, tn), lambda i, j, k, me_ref: out_block(i, tile(j, me_ref)[2]))
    ex = [(e, pl.BlockSpec((1, tn), lambda i, j, k, me_ref: (0, tile(j, me_ref)[2]))) for e in extras]
    return _matmul(name, a, w, pl.BlockSpec((tm, K), lambda i, j, k, me_ref: (i, 0)), b_spec, NN,
                   (M // tm, q if own else (N_CHIPS - 1) * q, 1), (tm, tn), ex, [(_sds(shape, dtype), out_spec)],
                   epilogue, prefetch=(me,), into=into)[0]


def _store(dtype):
    def epilogue(acc, ex, outs):
        outs[0][...] = acc.astype(dtype)
    return epilogue


def _sds(shape, dtype):
    return jax.ShapeDtypeStruct(shape, dtype)


def _mm_nn_cols(name, a, g, out_dtype, epilogue=None, extras=(), outs=None, tm=1024, tn=1024, tk=2048):
    M, K = a.shape
    _, _, Nq = g.shape
    tm, tn, tk = _tile(M, tm), _tile(Nq, tn), _tile(K, tk)
    q = Nq // tn
    grid = (M // tm, N_CHIPS * q, K // tk)
    if outs is None:
        outs = [(_sds((M, N_CHIPS * Nq), out_dtype), pl.BlockSpec((tm, tn), lambda i, j, k: (i, j)))]
    return _matmul(name, a, g, pl.BlockSpec((tm, tk), lambda i, j, k: (i, k)),
                   pl.BlockSpec((None, tk, tn), lambda i, j, k: (j // q, k, j % q)), NN, grid, (tm, tn),
                   list(extras), outs, epilogue or _store(out_dtype)), (tm, tn, tk)


def _rms_fwd(name, x, g):
    S, D = x.shape
    tm = _tile(S, ROW_TILE)

    def body(x_ref, g_ref, h_ref):
        xv = x_ref[...]
        r = lax.rsqrt(jnp.mean(xv * xv, axis=-1, keepdims=True) + EPS)
        h_ref[...] = ((xv * r) * g_ref[...]).astype(BF)

    row = pl.BlockSpec((tm, D), lambda i: (i, 0))
    return ORDER.call(
        body, [x, g], [row, pl.BlockSpec((1, D), lambda i: (0, 0))], name=name, grid=(S // tm,),
        out_specs=row, out_shape=_sds((S, D), BF), compiler_params=_cparams(("parallel",)),
    )


def _rms_bwd(name, dh, x, g, dres):
    S, D = x.shape
    tm = _tile(S, ROW_TILE // 2)

    def body(dh_ref, x_ref, g_ref, dres_ref, dx_ref, dxb_ref, dg_ref):
        xv = x_ref[...]
        r = lax.rsqrt(jnp.mean(xv * xv, axis=-1, keepdims=True) + EPS)
        n = xv * r
        dhv = dh_ref[...]
        dyg = dhv * g_ref[...]
        dx = dres_ref[...] + r * (dyg - n * jnp.mean(dyg * n, axis=-1, keepdims=True))
        dx_ref[...] = dx
        dxb_ref[...] = dx.astype(BF)

        @pl.when(pl.program_id(0) == 0)
        def _():
            dg_ref[...] = jnp.zeros_like(dg_ref)

        dg_ref[...] += jnp.sum(dhv * n, axis=0, keepdims=True)

    row = pl.BlockSpec((tm, D), lambda i: (i, 0))
    vec = pl.BlockSpec((1, D), lambda i: (0, 0))
    return ORDER.call(
        body, [dh, x, g, dres], [row, row, vec, row], name=name, grid=(S // tm,),
        out_specs=[row, row, vec],
        out_shape=[_sds((S, D), F32), _sds((S, D), BF), _sds((1, D), F32)],
        compiler_params=_cparams(("arbitrary",)),
    )


def _loss_head(x2, target, g):
    S, D = x2.shape
    tm = _tile(S, ROW_TILE)

    def body(x_ref, t_ref, g_ref, loss_ref, dx_ref, dxb_ref, dg_ref):
        xv = x_ref[...]
        gv = g_ref[...]
        r = lax.rsqrt(jnp.mean(xv * xv, axis=-1, keepdims=True) + EPS)
        n = xv * r
        e = n * gv - t_ref[...]
        dy = e * (1.0 / D)
        dyg = dy * gv
        dx = r * (dyg - n * jnp.mean(dyg * n, axis=-1, keepdims=True))
        dx_ref[...] = dx
        dxb_ref[...] = dx.astype(BF)

        @pl.when(pl.program_id(0) == 0)
        def _():
            dg_ref[...] = jnp.zeros_like(dg_ref)
            loss_ref[...] = jnp.zeros_like(loss_ref)

        dg_ref[...] += jnp.sum(dy * n, axis=0, keepdims=True)
        per_row = jnp.mean(e * e, axis=-1, keepdims=True)
        loss_ref[...] += 0.5 * jnp.sum(per_row, axis=0, keepdims=True)

    row = pl.BlockSpec((tm, D), lambda i: (i, 0))
    vec = pl.BlockSpec((1, D), lambda i: (0, 0))
    return ORDER.call(
        body, [x2, target, g], [row, row, vec], name="loss_head", grid=(S // tm,),
        out_specs=[pl.BlockSpec((1, 1), lambda i: (0, 0)), row, row, vec],
        out_shape=[_sds((1, 1), F32), _sds((S, D), F32), _sds((S, D), BF), _sds((1, D), F32)],
        compiler_params=_cparams(("arbitrary",)), chain_output=1,
    )


def _chains(L):
    side = min(4, L // QB)
    return side, 4 // side


def _band_scores(qkv_ref, i, L, coef, head):
    KB = _key_rows(L)
    lanes = pl.ds(head * HEAD_DIM, HEAD_DIM)
    q0 = pl.multiple_of(i * QB, QB)
    ks = pl.multiple_of(jnp.clip(i * QB - HALF_WINDOW, 0, L - KB), HALF_WINDOW)
    q = qkv_ref[0, pl.ds(q0, QB), lanes]
    k = qkv_ref[1, pl.ds(ks, KB), lanes]
    v = qkv_ref[2, pl.ds(ks, KB), lanes]
    s = lax.dot_general(q, k, (NT, ((), ())), preferred_element_type=F32) * SCALE
    qpos = q0 + lax.broadcasted_iota(jnp.int32, (QB, KB), 0)
    kpos = ks + lax.broadcasted_iota(jnp.int32, (QB, KB), 1)
    rel = jnp.abs(kpos - qpos)
    valid = rel <= HALF_WINDOW
    s = jnp.where(valid, s - coef * rel.astype(F32), NEG)
    return q0, ks, q, k, v, s, valid


def _alibi_coefs(group, d, heads):
    first = 4 * group + 1 + pl.program_id(1) * heads
    scale = jnp.full((1, 1), -(8.0 / N_HEADS_A) * math.log(2.0), F32)
    return [jnp.exp(scale * (first + hh).astype(F32)) * float(d) for hh in range(heads)]


def _dilated_view(qkv3, group, d, heads):
    per = 4 // heads
    L = qkv3.shape[1]
    if d == 1:
        return qkv3, pl.BlockSpec((3, L, heads * HEAD_DIM), lambda r, j: (0, 0, per * group + j))
    return qkv3, pl.BlockSpec((3, L, heads * HEAD_DIM), lambda r, j: (0, 0, r * per + j))


def _qkv_views(name, qkv3, views=None):
    _, S, _ = qkv3.shape
    W = 512
    tm = _tile(S, ROW_TILE)
    dilated = [(g, d) for g, d in enumerate(DILATIONS) if d > 1]
    first = dilated[0][0]
    assert [g for g, _ in dilated] == list(range(first, first + len(dilated)))
    nc = W // 128
    to_views = views is None

    def body(*refs):
        scr = refs[-nc:]
        if to_views:
            src, outs = refs[0], refs[1:1 + len(dilated)]
        else:
            ins, dst = refs[:len(dilated)], refs[len(dilated) + 1]
        for k, (_, d) in enumerate(dilated):
            @pl.when(pl.program_id(1) == k)
            def _():
                for w in range(3):
                    for c in range(nc):
                        if to_views:
                            scr[c][...] = src[w, :, c * 128:(c + 1) * 128].astype(F32)
                    for r in range(d):
                        for c in range(nc):
                            at = r * W + c * 128
                            if to_views:
                                outs[k][w, :, at:at + 128] = scr[c][pl.ds(r, tm // d, stride=d), :].astype(BF)
                            else:
                                scr[c][pl.ds(r, tm // d, stride=d), :] = ins[k][w, :, at:at + 128].astype(F32)
                    for c in range(nc):
                        if not to_views:
                            dst[w, :, c * 128:(c + 1) * 128] = scr[c][...].astype(BF)

    cols = pl.BlockSpec((3, tm, W), lambda i, k: (0, i, first + k))
    rows = [pl.BlockSpec((3, tm // d, d * W), lambda i, k: (0, i, 0)) for _, d in dilated]
    shapes = [_sds((3, S // d, d * W), BF) for _, d in dilated]
    common = dict(name=name, grid=(S // tm, len(dilated)), scratch_shapes=[pltpu.VMEM((tm, 128), F32)] * nc,
                  compiler_params=_cparams(("parallel", "arbitrary")))
    if to_views:
        outs = ORDER.call(body, [qkv3], [cols], out_specs=rows, out_shape=shapes, **common)
        return {d: o for (_, d), o in zip(dilated, outs)}
    return ORDER.call(body, [views[d] for _, d in dilated] + [qkv3], rows + [pl.BlockSpec(memory_space=pl.ANY)],
                      out_specs=cols, out_shape=_sds(qkv3.shape, BF), input_output_aliases={len(dilated): 0}, **common)


def _attn_a_fwd(qkv3, group, d):
    L = qkv3.shape[1]
    S = L * d
    assert L % QB == 0
    side, heads = _chains(L)
    view, blocks_spec = _dilated_view(qkv3, group, d, heads)

    def body(qkv_ref, o_ref, lse_ref):
        coefs = _alibi_coefs(group, d, heads)

        def step(i, carry):
            chains = [(hh, _band_scores(qkv_ref, side * i + u, L, coefs[hh], hh))
                      for u in range(side) for hh in range(heads)]
            soft = []
            for hh, (q0, _, _, _, v, s, _) in chains:
                m = jnp.max(s, axis=-1, keepdims=True)
                p = jnp.exp(s - m)
                den = jnp.sum(p, axis=-1, keepdims=True)
                soft.append((hh, q0, (p / den).astype(BF), v, m + jnp.log(den)))
            for hh, q0, pn, v, lse in soft:
                lanes = pl.ds(hh * HEAD_DIM, HEAD_DIM)
                o_ref[pl.ds(q0, QB), lanes] = jnp.dot(pn, v, preferred_element_type=F32)
                lse_ref[pl.ds(q0, QB), lanes] = jnp.broadcast_to(lse, (QB, HEAD_DIM))
            return carry

        lax.fori_loop(0, L // QB // side, step, 0)

    per = 4 // heads
    out = pl.BlockSpec((L, heads * HEAD_DIM), lambda r, j: (0, r * per + j))
    o, lse = ORDER.call(
        body, [view], [blocks_spec],
        name=f"attn_a_fwd_d{d}", grid=(d, per),
        out_specs=[out, out],
        out_shape=[_sds((L, d * 512), F32), _sds((L, d * 512), F32)],
        compiler_params=_cparams(("parallel", "parallel")),
    )
    return o, lse


def _dilated_rows(name, arrays):
    S, W = arrays[0].shape
    tm = _tile(S, ROW_TILE)
    ds_ = [d for d in DILATIONS if d > 1]
    n = len(arrays)

    def body(*refs):
        nc = W // 128
        ins, outs, scr = refs[:n], refs[n:-nc], refs[-nc:]
        for a, src in enumerate(ins):
            for c in range(nc):
                scr[c][...] = src[:, c * 128:(c + 1) * 128].astype(F32)
            for k, d in enumerate(ds_):
                dst = outs[a * len(ds_) + k]
                for r in range(d):
                    for c in range(nc):
                        at = r * W + c * 128
                        dst[:, at:at + 128] = scr[c][pl.ds(r, tm // d, stride=d), :].astype(dst.dtype)

    row = pl.BlockSpec((tm, W), lambda i: (i, 0))
    out_specs, out_shape = [], []
    for a in arrays:
        for d in ds_:
            out_specs.append(pl.BlockSpec((tm // d, d * W), lambda i: (i, 0)))
            out_shape.append(_sds((S // d, d * W), a.dtype))
    outs = ORDER.call(body, list(arrays), [row] * n, name=name, grid=(S // tm,), out_specs=out_specs,
                      out_shape=out_shape, scratch_shapes=[pltpu.VMEM((tm, 128), F32)] * (W // 128),
                      compiler_params=_cparams(("parallel",)))
    return [{d: outs[a * len(ds_) + k] for k, d in enumerate(ds_)} for a in range(n)]


def _attn_a_combine(os_, lses):
    W = 512
    S = os_[0].shape[0] * DILATIONS[0]
    tm = _tile(S, ROW_TILE)
    nc = W // 128
    dilated = [g for g, d in enumerate(DILATIONS) if d > 1]

    def body(o0, o1, o2, l0, l1, l2, y_ref, lj_ref, *scr):
        def token_order(src, g, slot):
            d = DILATIONS[g]
            if d == 1:
                return src[...]
            bufs = scr[slot * nc:(slot + 1) * nc]
            for r in range(d):
                for c in range(nc):
                    at = r * W + c * 128
                    bufs[c][pl.ds(r, tm // d, stride=d), :] = src[:, at:at + 128]
            return jnp.concatenate([buf[...] for buf in bufs], axis=1)

        slots = {g: k for k, g in enumerate(dilated)}
        ls = [token_order(l, g, slots.get(g, 0)) for g, l in enumerate((l0, l1, l2))]
        os_tok = [token_order(o, g, len(dilated) + slots.get(g, 0)) for g, o in enumerate((o0, o1, o2))]
        m = jnp.maximum(jnp.maximum(ls[0], ls[1]), ls[2])
        es = [jnp.exp(l - m) for l in ls]
        den = es[0] + es[1] + es[2]
        y = (es[0] / den) * os_tok[0] + (es[1] / den) * os_tok[1] + (es[2] / den) * os_tok[2]
        y_ref[...] = y.astype(BF)
        lj_ref[...] = m + jnp.log(den)

    row = pl.BlockSpec((tm, W), lambda i: (i, 0))
    views = [pl.BlockSpec((tm // d, d * W), lambda i: (i, 0)) for d in DILATIONS]
    return ORDER.call(
        body, [*os_, *lses], views + views, name="attn_a_combine", grid=(S // tm,), out_specs=[row, row],
        out_shape=[_sds((S, W), BF), _sds((S, W), F32)],
        scratch_shapes=[pltpu.VMEM((tm, 128), F32)] * (2 * len(dilated) * nc),
        compiler_params=_cparams(("parallel",)),
    )


def _attn_a_bwd(qkv3, dy, y, lj, dqkv3, group, d):
    L = qkv3.shape[1]
    S = L * d
    side, heads = _chains(L)
    view, blocks_spec = _dilated_view(qkv3, group, d, heads)

    def body(qkv_ref, dy_ref, y_ref, lj_ref, *rest):
        out_ref, dk_acc, dv_acc = rest[-3:]
        coefs = _alibi_coefs(group, d, heads)
        dk_acc[...] = jnp.zeros_like(dk_acc)
        dv_acc[...] = jnp.zeros_like(dv_acc)

        def step(i, carry):
            chains = [(pl.ds(hh * HEAD_DIM, HEAD_DIM), _band_scores(qkv_ref, side * i + u, L, coefs[hh], hh))
                      for u in range(side) for hh in range(heads)]
            dys = [dy_ref[pl.ds(c[0], QB), lanes] for lanes, c in chains]
            dps = [lax.dot_general(dyv, c[4], (NT, ((), ())), preferred_element_type=F32)
                   for dyv, (_, c) in zip(dys, chains)]
            grads = []
            for (lanes, (q0, ks, q, k, v, s, valid)), dyv, dp in zip(chains, dys, dps):
                rows = pl.ds(q0, QB)
                delta = jnp.sum(dyv.astype(F32) * y_ref[rows, lanes].astype(F32), axis=-1, keepdims=True)
                p = jnp.where(valid, jnp.exp(s - jnp.tile(lj_ref[rows, lanes], (1, _key_rows(L) // HEAD_DIM))), 0.0)
                grads.append(((p * (dp - delta)).astype(BF), p.astype(BF)))
            for (lanes, (q0, ks, q, k, v, s, valid)), dyv, (ds, pb) in zip(chains, dys, grads):
                out_ref[0, pl.ds(q0, QB), lanes] = (jnp.dot(ds, k, preferred_element_type=F32) * SCALE).astype(BF)
                keys = pl.ds(ks, _key_rows(L))
                dk_acc[keys, lanes] += lax.dot_general(ds, q, (TN, ((), ())), preferred_element_type=F32) * SCALE
                dv_acc[keys, lanes] += lax.dot_general(pb, dyv, (TN, ((), ())), preferred_element_type=F32)
            return carry

        lax.fori_loop(0, L // QB // side, step, 0)
        out_ref[1] = dk_acc[...].astype(BF)
        out_ref[2] = dv_acc[...].astype(BF)

    per = 4 // heads
    width = heads * HEAD_DIM
    row = pl.BlockSpec((L, width), lambda r, j: (0, r * per + j))
    operands = [view, dy, y, lj]
    scratch = [pltpu.VMEM((L, width), F32), pltpu.VMEM((L, width), F32)]
    if d == 1:
        return ORDER.call(
            body, operands + [dqkv3], [blocks_spec, row, row, row, pl.BlockSpec(memory_space=pl.ANY)],
            name=f"attn_a_bwd_d{d}", grid=(d, per), out_specs=blocks_spec, out_shape=_sds((3, S, QKV_W), BF),
            scratch_shapes=scratch, input_output_aliases={4: 0}, compiler_params=_cparams(("parallel", "parallel")))
    return ORDER.call(
        body, operands, [blocks_spec, row, row, row], name=f"attn_a_bwd_d{d}", grid=(d, per),
        out_specs=blocks_spec, out_shape=_sds((3, L, d * 512), BF),
        scratch_shapes=scratch, compiler_params=_cparams(("parallel", "parallel")))


def _toeplitz_onehot():
    oh = np.zeros((64, GRID_W, 128), np.float32)
    for qc in range(GRID_W):
        for m in range(128):
            kc = m % GRID_W
            dc = int(np.clip(kc - qc, -(NA_COLS - 1), NA_COLS - 1)) + NA_COLS - 1
            oh[(m // GRID_W) * 32 + dc, qc, m] = 1.0
    return oh.reshape(64, GRID_W * 128)


def _nbr_scores(qkv_ref, e2_ref, r, rows, ok):
    rs = jnp.clip(r - NA_ROWS // 2, 0, rows - NA_ROWS)
    q0 = pl.multiple_of(r * GRID_W, GRID_W)
    k0 = pl.multiple_of(rs * GRID_W, GRID_W)
    q = qkv_ref[0, pl.ds(q0, GRID_W), :]
    k = qkv_ref[1, pl.ds(k0, NA_ROWS * GRID_W), :]
    v = qkv_ref[2, pl.ds(k0, NA_ROWS * GRID_W), :]
    s = lax.dot_general(q, k, (NT, ((), ())), preferred_element_type=F32) * SCALE
    first = rs - r + NA_ROWS - 1
    bias = jnp.concatenate([e2_ref[first + 2 * pair] for pair in range(NA_ROWS // 2)], axis=1)
    s = jnp.where(ok, s + bias, NEG)
    return q0, k0, first, q, k, v, s


def _nbr_col_ok():
    qc = lax.broadcasted_iota(jnp.int32, (GRID_W, NA_ROWS * GRID_W), 0)
    kc = lax.broadcasted_iota(jnp.int32, (GRID_W, NA_ROWS * GRID_W), 1) % GRID_W
    cs = jnp.clip(qc - NA_COLS // 2, 0, GRID_W - NA_COLS)
    return (kc >= cs) & (kc < cs + NA_COLS)


def _attn_b_fwd(qkv3, e2):
    _, S, _ = qkv3.shape
    rows = S // GRID_W
    assert rows >= NA_ROWS

    def body(qkv_ref, e2_ref, o_ref, lse_ref):
        ok = _nbr_col_ok()

        def step(i, carry):
            blocks = [_nbr_scores(qkv_ref, e2_ref, NBR_SIDE * i + u, rows, ok) for u in range(NBR_SIDE)]
            soft = []
            for q0, _, _, _, _, v, s in blocks:
                m = jnp.max(s, axis=-1, keepdims=True)
                p = jnp.exp(s - m)
                den = jnp.sum(p, axis=-1, keepdims=True)
                soft.append((q0, (p / den).astype(BF), v, m + jnp.log(den)))
            for q0, pn, v, lse in soft:
                o_ref[pl.ds(q0, GRID_W), :] = jnp.dot(pn, v, preferred_element_type=F32).astype(BF)
                lse_ref[pl.ds(q0, GRID_W), :] = jnp.broadcast_to(lse, (GRID_W, HEAD_DIM))
            return carry

        lax.fori_loop(0, rows // NBR_SIDE, step, 0)

    out = pl.BlockSpec((S, HEAD_DIM), lambda h: (0, h))
    return ORDER.call(
        body, [qkv3, e2],
        [pl.BlockSpec((3, S, HEAD_DIM), lambda h: (0, 0, N_HEADS_A + h)),
         pl.BlockSpec((None, RPB_ROWS - 1, GRID_W, 128), lambda h: (h, 0, 0, 0))],
        name="attn_b_fwd", grid=(4,),
        out_specs=[out, out], out_shape=[_sds((S, 512), BF), _sds((S, 512), F32)],
        compiler_params=_cparams(("parallel",)),
    )


def _attn_b_bwd(qkv3, e2, dy, y, lse, dqkv3):
    _, S, _ = qkv3.shape
    rows = S // GRID_W
    nk = NA_ROWS * GRID_W

    def body(qkv_ref, e2_ref, dy_ref, y_ref, lse_ref, _, out_ref, de2_ref, dk_acc, dv_acc):
        ok = _nbr_col_ok()
        dk_acc[...] = jnp.zeros_like(dk_acc)
        dv_acc[...] = jnp.zeros_like(dv_acc)
        de2_ref[...] = jnp.zeros_like(de2_ref)

        def step(i, carry):
            blocks = [_nbr_scores(qkv_ref, e2_ref, NBR_SIDE * i + u, rows, ok) for u in range(NBR_SIDE)]
            dys = [dy_ref[pl.ds(b[0], GRID_W), :] for b in blocks]
            dps = [lax.dot_general(dyv, b[5], (NT, ((), ())), preferred_element_type=F32) for dyv, b in zip(dys, blocks)]
            grads = []
            for (q0, k0, first, q, k, v, s), dyv, dp in zip(blocks, dys, dps):
                qrows = pl.ds(q0, GRID_W)
                delta = jnp.sum(dyv.astype(F32) * y_ref[qrows, :].astype(F32), axis=-1, keepdims=True)
                p = jnp.where(ok, jnp.exp(s - jnp.tile(lse_ref[qrows, :], (1, nk // HEAD_DIM))), 0.0)
                ds = p * (dp - delta)
                for pair in range(NA_ROWS // 2):
                    de2_ref[first + 2 * pair] += ds[:, pair * 128:(pair + 1) * 128]
                grads.append((ds.astype(BF), p.astype(BF)))
            for (q0, k0, first, q, k, v, s), dyv, (dsb, pb) in zip(blocks, dys, grads):
                out_ref[0, pl.ds(q0, GRID_W), :] = (jnp.dot(dsb, k, preferred_element_type=F32) * SCALE).astype(BF)
                keys = pl.ds(k0, nk)
                dk_acc[keys, :] += lax.dot_general(dsb, q, (TN, ((), ())), preferred_element_type=F32) * SCALE
                dv_acc[keys, :] += lax.dot_general(pb, dyv, (TN, ((), ())), preferred_element_type=F32)
            return carry

        lax.fori_loop(0, rows // NBR_SIDE, step, 0)
        out_ref[1] = dk_acc[...].astype(BF)
        out_ref[2] = dv_acc[...].astype(BF)

    heads = pl.BlockSpec((3, S, HEAD_DIM), lambda h: (0, 0, N_HEADS_A + h))
    row = pl.BlockSpec((S, HEAD_DIM), lambda h: (0, h))
    table = pl.BlockSpec((None, RPB_ROWS - 1, GRID_W, 128), lambda h: (h, 0, 0, 0))
    return ORDER.call(
        body, [qkv3, e2, dy, y, lse, dqkv3],
        [heads, table, row, row, row, pl.BlockSpec(memory_space=pl.ANY)], name="attn_b_bwd", grid=(4,),
        out_specs=[heads, table],
        out_shape=[_sds((3, S, QKV_W), BF), _sds((4, RPB_ROWS - 1, GRID_W, 128), F32)],
        scratch_shapes=[pltpu.VMEM((S, HEAD_DIM), F32), pltpu.VMEM((S, HEAD_DIM), F32)],
        input_output_aliases={5: 0},
        compiler_params=_cparams(("parallel",)), chain_output=1,
    )


def _rpb_to_table(rpb):
    pad = jnp.pad(rpb, ((0, 0), (0, 0), (0, 1)))
    pairs = jnp.concatenate([pad[:, :-1], pad[:, 1:]], axis=-1).reshape(4 * (RPB_ROWS - 1), 64)
    onehot = jnp.asarray(_toeplitz_onehot())
    n = onehot.shape[1]
    tn = 2048
    full = lambda i, j, k: (0, 0)
    (e2,) = _matmul("rpb_table", pairs, onehot, pl.BlockSpec(pairs.shape, full),
                    pl.BlockSpec((64, tn), lambda i, j, k: (0, j)), NN, (1, n // tn, 1), (pairs.shape[0], tn), [],
                    [(_sds((pairs.shape[0], n), F32), pl.BlockSpec((pairs.shape[0], tn), lambda i, j, k: (0, j)))],
                    _store(F32), precision=lax.Precision.HIGHEST)
    return e2.reshape(4, RPB_ROWS - 1, GRID_W, 128)


def _table_grad_to_rpb(de2):
    onehot = jnp.asarray(_toeplitz_onehot())
    n = onehot.shape[1]
    flat = de2.reshape(4 * (RPB_ROWS - 1), n)
    tk = 2048
    (dpairs,) = _matmul("rpb_table_grad", flat, onehot, pl.BlockSpec((flat.shape[0], tk), lambda i, j, k: (0, k)),
                        pl.BlockSpec((64, tk), lambda i, j, k: (0, k)), NT, (1, 1, n // tk), (flat.shape[0], 64), [],
                        [(_sds((flat.shape[0], 64), F32), pl.BlockSpec((flat.shape[0], 64), lambda i, j, k: (0, 0)))],
                        _store(F32), precision=lax.Precision.HIGHEST)
    dpairs = dpairs.reshape(4, RPB_ROWS - 1, 64)
    zero = jnp.zeros((4, 1, RPB_COLS), F32)
    return (jnp.concatenate([dpairs[:, :, :RPB_COLS], zero], axis=1)
            + jnp.concatenate([zero, dpairs[:, :, 32:32 + RPB_COLS]], axis=1))


HBM = pl.BlockSpec(memory_space=pl.ANY)


def _place():
    x, y, c = lax.axis_index("x"), lax.axis_index("y"), lax.axis_index("c")
    chips = [(1 - x, y), (x, 1 - y), (1 - x, 1 - y)]
    return x, y, c, chips


def _remote(src, dst, send_sem, recv_sem, to):
    return pltpu.make_async_remote_copy(src_ref=src, dst_ref=dst, send_sem=send_sem, recv_sem=recv_sem,
                                        device_id=to, device_id_type=MESH)


def _place_shard(name, w, me, plain=False):
    R, C = w.shape
    tr = _tile(R, 256)

    def body(me_ref, w_ref, *o_refs):
        for o_ref in o_refs:
            o_ref[...] = w_ref[...].astype(BF)

    row = pl.BlockSpec((tr, C), lambda i, mr: (i, 0))
    placed = pl.BlockSpec((None, tr, C), lambda i, mr: (mr[0], i, 0))
    return ORDER.call(
        body, [w], [row], prefetch=(me,), name=name, grid=(R // tr,),
        out_specs=[placed, row] if plain else [placed],
        out_shape=[_sds((N_CHIPS, R, C), BF)] + ([_sds((R, C), BF)] if plain else []),
        compiler_params=_cparams(("parallel",)),
    )


SEM = pl.BlockSpec(memory_space=pltpu.SEMAPHORE)
IN_HBM = pl.BlockSpec(memory_space=pltpu.HBM)
DATAFLOW = pltpu.SideEffectType.DATAFLOW_SIDE_EFFECTING


def _in_hbm(a):
    return pltpu.with_memory_space_constraint(a, pltpu.HBM)


def _copy_start(name, bufs, copies, n_copies, earlier=None):
    n = len(bufs)
    after = None if any(b is ORDER.last for b in bufs) else ORDER.last
    n_extra = (2 if earlier is not None else 0) + (1 if after is not None else 0)

    def body(*refs):
        ins = refs[:n]
        if earlier is not None:
            for k, (src, dst, to) in enumerate(earlier[0](ins)):
                cp = _remote(src, dst, refs[n].at[k], refs[n + 1].at[k], to)
                cp.wait_send()
                cp.wait_recv()
        send_sems, recv_sems = refs[n + n_extra], refs[n + n_extra + 1]
        for k, (src, dst, to) in enumerate(copies(ins)):
            _remote(src, dst, send_sems.at[k], recv_sems.at[k], to).start()
        refs[-1][...] = jnp.zeros((8, 128), F32)

    operands = [_in_hbm(b) for b in bufs]
    in_specs = [IN_HBM] * n
    if earlier is not None:
        operands += [earlier[1], earlier[2]]
        in_specs += [SEM, SEM]
    if after is not None:
        operands.append(after)
        in_specs.append(HBM)
    outs = pl.pallas_call(
        body, name=name,
        out_shape=(pltpu.SemaphoreType.DMA((n_copies,)), pltpu.SemaphoreType.DMA((n_copies,)),
                   *[pltpu.HBM(b.shape, b.dtype) for b in bufs], _sds((8, 128), F32)),
        in_specs=in_specs,
        out_specs=(SEM, SEM, *[IN_HBM] * n, pl.BlockSpec(memory_space=pltpu.VMEM)),
        input_output_aliases={i: 2 + i for i in range(n)},
        compiler_params=pltpu.CompilerParams(has_side_effects=DATAFLOW),
    )(*operands)
    ORDER.last = outs[-1]
    return outs[0], outs[1], list(outs[2:2 + n])


def _copy_wait(name, bufs, copies, send_sems, recv_sems):
    n = len(bufs)
    after = ORDER.last

    def body(*refs):
        ins = refs[:n]
        for k, (src, dst, to) in enumerate(copies(ins)):
            cp = _remote(src, dst, refs[n].at[k], refs[n + 1].at[k], to)
            cp.wait_send()
            cp.wait_recv()

    outs = list(pl.pallas_call(
        body, name=name,
        out_shape=tuple(pltpu.HBM(b.shape, b.dtype) for b in bufs),
        in_specs=[IN_HBM] * n + [SEM, SEM, HBM], out_specs=tuple([IN_HBM] * n),
        input_output_aliases={i: i for i in range(n)},
        compiler_params=pltpu.CompilerParams(has_side_effects=DATAFLOW),
    )(*bufs, send_sems, recv_sems, after))
    ORDER.last = outs[0]
    return outs


def _gather_hop1(bufs):
    x, y, c, chips = _place()
    out = []
    for b in bufs:
        half = b.shape[1] // 2
        mine = b.at[2 * x + y, pl.ds(c * half, half), :]
        out += [(mine, mine, (*chip, c)) for chip in chips]
    return out


def _gather_hop2(bufs):
    x, y, c, chips = _place()
    out = []
    for b in bufs:
        half = b.shape[1] // 2
        for chip in chips:
            landed = b.at[2 * chip[0] + chip[1], pl.ds(c * half, half), :]
            out.append((landed, landed, (x, y, 1 - c)))
    return out


def _swap_copies(bufs):
    x, y, c, _ = _place()
    n = len(bufs) // 2
    out = []
    for p, land in zip(bufs[:n], bufs[n:]):
        half = p.shape[1] // 2
        out.append((p.at[:, pl.ds((1 - c) * half, half), :], land, (x, y, 1 - c)))
    return out


def _scatter_copies(bufs):
    _, _, c, chips = _place()
    n = len(bufs) // 2
    out = []
    for s_, land in zip(bufs[:n], bufs[n:]):
        out += [(s_.at[2 * chip[0] + chip[1]], land.at[j], (*chip, c)) for j, chip in enumerate(chips)]
    return out


def _join_copies(bufs):
    x, y, c, _ = _place()
    out = []
    for b in bufs:
        half = b.shape[0] // 2
        mine = b.at[pl.ds(c * half, half), :]
        out.append((mine, mine, (x, y, 1 - c)))
    return out


def _gather_small(vec):
    m_per, n = vec.shape

    def body(x_ref, out_ref, send_sems, recv_sems, local_sem):
        x, y, c, chips = _place()
        me, sibling = (x, y, c), (x, y, 1 - c)

        def rows(px, py, pc):
            return out_ref.at[pl.ds((4 * px + 2 * py + pc) * m_per, m_per), :]

        def copy(k, block, to, src=None):
            return _remote(rows(*block) if src is None else src, rows(*block), send_sems.at[k], recv_sems.at[k], to)

        mine = pltpu.make_async_copy(x_ref, rows(*me), local_sem)
        mine.start()
        first = [copy(0, me, sibling, src=x_ref)]
        first += [copy(1 + j, me, (*chip, c), src=x_ref) for j, chip in enumerate(chips)]
        for cp in first:
            cp.start()
        passed = [copy(4 + j, (*chip, c), sibling) for j, chip in enumerate(chips)]
        for j, chip in enumerate(chips):
            copy(1 + j, (*chip, c), me).wait_recv()
            passed[j].start()
        copy(0, sibling, me).wait_recv()
        for j, chip in enumerate(chips):
            copy(4 + j, (*chip, 1 - c), me).wait_recv()
        for cp in first + passed:
            cp.wait_send()
        mine.wait()

    return ORDER.call(
        body, [vec], [pl.BlockSpec(memory_space=pltpu.VMEM)], name="gather_small_grads",
        out_shape=_sds((8 * m_per, n), vec.dtype), out_specs=pl.BlockSpec(memory_space=pltpu.VMEM),
        scratch_shapes=[pltpu.SemaphoreType.DMA((7,)), pltpu.SemaphoreType.DMA((7,)), pltpu.SemaphoreType.DMA],
    )


def _add_sibling(name, partial, received, c):
    _, R, C = partial.shape
    half = R // 2
    tr = _tile(half, 256)
    nb = half // tr

    def body(c_ref, p_ref, r_ref, o_ref):
        o_ref[...] = (p_ref[...].astype(F32) + r_ref[...].astype(F32)).astype(BF)

    return ORDER.call(
        body, [partial, received],
        [pl.BlockSpec((None, tr, C), lambda j, i, cr: (j, cr[0] * nb + i, 0)),
         pl.BlockSpec((None, tr, C), lambda j, i, cr: (j, i, 0))],
        prefetch=(c,), name=name, grid=(N_CHIPS, nb),
        out_specs=pl.BlockSpec((None, tr, C), lambda j, i, cr: (j, i, 0)),
        out_shape=_sds((N_CHIPS, half, C), BF), compiler_params=_cparams(("parallel", "parallel")),
    )


def _add_chips(name, sums, received, me_c):
    _, half, C = sums.shape
    tr = _tile(half, 256)
    nb = half // tr

    def body(mc_ref, s_ref, r_ref, o_ref):
        acc = s_ref[...].astype(F32)
        for j in range(3):
            acc = acc + r_ref[j].astype(F32)
        o_ref[...] = acc

    return ORDER.call(
        body, [sums, received],
        [pl.BlockSpec((None, tr, C), lambda i, mc: (mc[0], i, 0)),
         pl.BlockSpec((3, tr, C), lambda i, mc: (0, i, 0))],
        prefetch=(me_c,), name=name, grid=(nb,),
        out_specs=pl.BlockSpec((tr, C), lambda i, mc: (mc[1] * nb + i, 0)),
        out_shape=_sds((2 * half, C), F32), compiler_params=_cparams(("parallel",)),
    )


def _adamw_math(w, g, m, v):
    m = ADAM_B1 * m + (1.0 - ADAM_B1) * g
    v = ADAM_B2 * v + (1.0 - ADAM_B2) * (g * g)
    m_hat = m / (1.0 - ADAM_B1 ** ADAM_STEP)
    v_hat = v / (1.0 - ADAM_B2 ** ADAM_STEP)
    delta = -ADAM_LR * (m_hat / (jnp.sqrt(v_hat) + ADAM_EPS) + ADAM_WD * w)
    return delta, m, v


def _adamw(name, w, g, m, v):
    R, C = w.shape
    tr = _tile(R, 256)

    def body(w_ref, g_ref, m_ref, v_ref, go_ref, d_ref, mo_ref, vo_ref):
        gv = g_ref[...]
        go_ref[...] = gv
        d_ref[...], mo_ref[...], vo_ref[...] = _adamw_math(w_ref[...], gv, m_ref[...], v_ref[...])

    row = pl.BlockSpec((tr, C), lambda i: (i, 0))
    return ORDER.call(
        body, [w, g, m, v], [row] * 4, name=name, grid=(R // tr,), out_specs=[row] * 4,
        out_shape=[_sds((R, C), F32)] * 4, compiler_params=_cparams(("parallel",)), chain_output=1,
    )


def _adamw_small(gathered, w, m, v):
    rows, n = w.shape

    def body(ga_ref, w_ref, m_ref, v_ref, go_ref, d_ref, mo_ref, vo_ref):
        g = ga_ref[pl.ds(0, rows), :]
        for dev in range(1, 8):
            g = g + ga_ref[pl.ds(dev * rows, rows), :]
        go_ref[...] = g
        d_ref[...], mo_ref[...], vo_ref[...] = _adamw_math(w_ref[...], g, m_ref[...], v_ref[...])

    whole = pl.BlockSpec(memory_space=pltpu.VMEM)
    return ORDER.call(
        body, [gathered, w, m, v], [whole] * 4, name="adamw_small", out_specs=[whole] * 4,
        out_shape=[_sds((rows, n), F32)] * 4, compiler_params=_cparams(), chain_output=1,
    )


def _proj_merge(y_a, y_b, gpa, gpb, g3):
    S, K = y_a.shape
    _, _, Nq = gpa.shape
    D = N_CHIPS * Nq
    tm, tn = _tile(S, 1024), _tile(Nq, 512)
    q = Nq // tn

    def body(ya_ref, yb_ref, wa_ref, wb_ref, g_ref, merged_ref, c_ref):
        pa = jnp.dot(ya_ref[...], wa_ref[...], preferred_element_type=F32)
        pb = jnp.dot(yb_ref[...], wb_ref[...], preferred_element_type=F32)
        g = g_ref[...].astype(F32)
        merged_ref[...] = (g[0] * pa + g[1] * pb).astype(BF)
        c_ref[0] = (pa * g[0] * (1.0 - g[0])).astype(BF)
        c_ref[1] = (pb * g[1] * (1.0 - g[1])).astype(BF)

    rows = pl.BlockSpec((tm, K), lambda i, j: (i, 0))
    weight = pl.BlockSpec((None, K, tn), lambda i, j: (j // q, 0, j % q))
    pair = pl.BlockSpec((2, tm, tn), lambda i, j: (0, i, j))
    return ORDER.call(
        body, [y_a, y_b, gpa, gpb, g3], [rows, rows, weight, weight, pair], name="proj_merge",
        grid=(S // tm, N_CHIPS * q), out_specs=[pl.BlockSpec((tm, tn), lambda i, j: (i, j)), pair],
        out_shape=[_sds((S, D), BF), _sds((2, S, D), BF)], compiler_params=_cparams(("parallel", "parallel")))


class _Exchange:
    GATHER = (("qkv",), ("gate",), ("proj_a", "proj_b", "out"), ("up",), ("down",))
    REDUCE = {"mlp": ("down", "up"), "mix": ("out", "proj_a", "proj_b"), "in": ("qkv", "gate")}

    OWN_FIRST = ("qkv", "gate")

    def __init__(self, shards, me, c, moments):
        self.me, self.c = me, c
        self.shards, self.moments = shards, moments
        self.hop1, self.hop2, self.stage, self.grads, self.own, self.updates = {}, {}, {}, {}, {}, {}
        for g, names in enumerate(self.GATHER):
            bufs = []
            for n in names:
                placed = _place_shard(f"place_{n}", shards[n], me, plain=n in self.OWN_FIRST)
                bufs.append(placed[0])
                if n in self.OWN_FIRST:
                    self.own[n] = placed[1]
            self.hop1[g] = _copy_start(f"gather{g}_start", bufs, _gather_hop1, 3 * len(names))

    def forward(self, g):
        send, recv, thru = self.hop1.pop(g)
        self.hop2[g] = _copy_start(f"gather{g}_forward", thru, _gather_hop2, len(thru) * 3,
                                   earlier=(_gather_hop1, send, recv))

    def weights(self, g):
        send, recv, thru = self.hop2.pop(g)
        return _copy_wait(f"gather{g}_wait", thru, _gather_hop2, send, recv)

    def adamw_beside(self, name):
        def update(w, g, m, v):
            return (g,) + _adamw_math(w, g, m, v)
        return update, [self.shards[name], self.grads[name], *self.moments[name]], 4

    def reduce(self, key, partials=None):
        names = self.REDUCE[key]
        n = len(names)
        if partials is not None:
            lands = [lax.empty((p.shape[0], p.shape[1] // 2, p.shape[2]), p.dtype) for p in partials]
            self.stage[key] = ("swap",) + _copy_start(f"reduce_{key}_swap", list(partials) + lands, _swap_copies, n)
            return
        kind, send, recv, thru = self.stage.pop(key)
        if kind == "swap":
            thru = _copy_wait(f"reduce_{key}_swap_wait", thru, _swap_copies, send, recv)
            sums = [_add_sibling(f"reduce_{nm}_add_sibling", p, r, self.c)
                    for nm, p, r in zip(names, thru[:n], thru[n:])]
            lands = [lax.empty((3,) + s_.shape[1:], s_.dtype) for s_ in sums]
            self.stage[key] = ("scatter",) + _copy_start(f"reduce_{key}_scatter", sums + lands, _scatter_copies, 3 * n)
        elif kind == "scatter":
            thru = _copy_wait(f"reduce_{key}_scatter_wait", thru, _scatter_copies, send, recv)
            me_c = jnp.concatenate([self.me, self.c])
            halves = [_add_chips(f"reduce_{nm}_add_chips", s_, r, me_c)
                      for nm, s_, r in zip(names, thru[:n], thru[n:])]
            self.stage[key] = ("join",) + _copy_start(f"reduce_{key}_join", halves, _join_copies, n)
        else:
            thru = _copy_wait(f"reduce_{key}_join_wait", thru, _join_copies, send, recv)
            self.grads.update(zip(names, thru))


def _forward_backward(x, target, norm_mix, b_gate, rpb, norm_mlp, norm_final, ex):
    S, D = x.shape

    h1 = _rms_fwd("rms_mix", x, norm_mix)
    nq = QKV_W // 512
    qkv_out = (((3, S, QKV_W), BF), lambda i, T: (T // nq, i, T % nq))
    tg = _tile(ex.own["gate"].shape[1], 1024)
    ng = D // tg
    gate_out = (((2, S, D), BF), lambda i, T: (T // ng, i, T % ng))

    def gate_epilogue(acc, ex_, outs):
        outs[0][...] = jax.nn.sigmoid(acc + ex_[0][...]).astype(BF)

    qkv3 = _mm_nn_shards("qkv_own", h1, ex.own["qkv"], ex.me, True, *qkv_out, _store(BF))
    g3 = _mm_nn_shards("gate_own", h1, ex.own["gate"], ex.me, True, *gate_out, gate_epilogue, extras=[b_gate], tn=tg)
    ex.forward(0)
    e2 = _rpb_to_table(rpb)
    (gq,) = ex.weights(0)
    qkv3 = _mm_nn_shards("qkv", h1, gq, ex.me, False, *qkv_out, _store(BF), into=qkv3)

    ex.forward(1)
    outs_a = [_attn_a_fwd(qkv3, 0, DILATIONS[0])]
    (gg,) = ex.weights(1)
    g3 = _mm_nn_shards("gate", h1, gg, ex.me, False, *gate_out, gate_epilogue, extras=[b_gate], into=g3, tn=tg)

    ex.forward(2)
    qkv_views = _qkv_views("qkv_views", qkv3)
    outs_a += [_attn_a_fwd(qkv_views[d], grp, d) for grp, d in enumerate(DILATIONS) if grp > 0]
    y_a, lj = _attn_a_combine([o for o, _ in outs_a], [l for _, l in outs_a])
    y_b, lse_b = _attn_b_fwd(qkv3, e2)
    gpa, gpb, gout = ex.weights(2)
    wout = gout.reshape(D, D)
    merged, c3 = _proj_merge(y_a, y_b, gpa, gpb, g3)

    def residual_epilogue(acc, ex_, outs):
        outs[0][...] = acc + ex_[0][...]

    def nn_plain(name, a, w, res, bm=1024, bn=1024):
        M, K = a.shape
        N = w.shape[1]
        bm, bn, bk = _tile(M, bm), _tile(N, bn), _tile(K, 2048)
        t = pl.BlockSpec((bm, bn), lambda i, j, k: (i, j))
        return _matmul(name, a, w, pl.BlockSpec((bm, bk), lambda i, j, k: (i, k)),
                       pl.BlockSpec((bk, bn), lambda i, j, k: (k, j)), NN, (M // bm, N // bn, K // bk), (bm, bn),
                       [(res, t)], [(_sds((M, N), F32), t)], residual_epilogue)[0]

    ex.forward(3)
    x1 = nn_plain("out_proj", merged, wout, x, bm=512, bn=2048)
    h2 = _rms_fwd("rms_mlp", x1, norm_mlp)
    (gup,) = ex.weights(3)
    F = gup.shape[2] * N_CHIPS

    def up_epilogue(acc, ex_, outs):
        ru = jnp.maximum(acc, 0.0)
        outs[0][...] = (ru * ru).astype(BF)
        outs[1][...] = ru.astype(BF)

    tu = _tile(gup.shape[2], 2048)
    ut = pl.BlockSpec((_tile(S, 1024), tu), lambda i, j, k: (i, j))
    (act, ru), _ = _mm_nn_cols("mlp_up", h2, gup, BF, epilogue=up_epilogue, tn=tu,
                               outs=[(_sds((S, F), BF), ut), (_sds((S, F), BF), ut)])
    ex.forward(4)
    (gdown,) = ex.weights(4)
    wdown = gdown.reshape(F, D)
    x2 = nn_plain("mlp_down", act, wdown, x1)

    loss, dx2, dx2b, d_norm_final = _loss_head(x2, target, norm_final.reshape(1, D))

    def nt_rows(name, a, w, epilogue, extras, outs, bn=1024):
        M, N = a.shape
        K = w.shape[0]
        bm, bn, bk = _tile(M, 1024), _tile(K, bn), _tile(N, 2048)
        return _matmul(name, a, w, pl.BlockSpec((bm, bk), lambda i, j, k: (i, k)),
                       pl.BlockSpec((bn, bk), lambda i, j, k: (j, k)), NT, (M // bm, K // bn, N // bk), (bm, bn),
                       extras(bm, bn), outs(bm, bn), epilogue)

    def nt_cols(name, a_spec_fn, a, g, M, epilogue, extras, outs, bk, bn=1024, side=None):
        _, K, Nq = g.shape
        bm, bn, bk = _tile(M, 1024), _tile(K, bn), _tile(Nq, bk)
        q = Nq // bk
        return _matmul(name, a, g, a_spec_fn(bm, bk), pl.BlockSpec((None, bn, bk), lambda i, j, k: (k // q, j, k % q)),
                       NT, (M // bm, K // bn, N_CHIPS * q), (bm, bn), extras(bm, bn), outs(bm, bn), epilogue,
                       side=side)

    def tn_grad(name, a, a_spec_fn, b, b_spec_fn, Kin, N, out_shape, out_spec_fn, bn=1024):
        bm, bn, bk = _tile(Kin, 1024), _tile(N, bn), _tile(S, 4096)
        return _matmul(name, a, b, a_spec_fn(bk, bm), b_spec_fn(bk, bn), TN, (Kin // bm, N // bn, S // bk), (bm, bn),
                       [], [(_sds(out_shape, BF), out_spec_fn(bm, bn))], _store(BF))[0]

    plain_a = lambda bk, bm: pl.BlockSpec((bk, bm), lambda i, j, k: (k, i))
    plain_b = lambda bk, bn: pl.BlockSpec((bk, bn), lambda i, j, k: (k, j))
    plain_o = lambda bm, bn: pl.BlockSpec((bm, bn), lambda i, j, k: (i, j))
    a_rows = lambda bm, bk: pl.BlockSpec((bm, bk), lambda i, j, k: (i, k))

    def cols_o(Nq):
        def spec(bm, bn):
            q = Nq // bn
            return pl.BlockSpec((None, bm, bn), lambda i, j, k: (j // q, i, j % q))
        return spec

    def du_epilogue(acc, ex_, outs):
        outs[0][...] = (acc * (2.0 * ex_[0][...].astype(F32))).astype(BF)

    dw_down = tn_grad("mlp_down_dw", act, plain_a, dx2b, plain_b, F, D, (F, D), plain_o)
    (du,) = nt_rows("mlp_down_dx", dx2b, wdown, du_epilogue,
                    lambda bm, bn: [(ru, plain_o(bm, bn))], lambda bm, bn: [(_sds((S, F), BF), plain_o(bm, bn))],
                    bn=2048)

    fq = gup.shape[2]
    dw_up = tn_grad("mlp_up_dw", h2, plain_a, du, plain_b, D, F, (N_CHIPS, D, fq), cols_o(fq), bn=min(fq, 1024))
    ex.reduce("mlp", partials=[dw_down.reshape(N_CHIPS, F // N_CHIPS, D), dw_up])
    (dh2,) = nt_cols("mlp_up_dx", a_rows, du, gup, S, _store(F32), lambda bm, bn: [],
                     lambda bm, bn: [(_sds((S, D), F32), plain_o(bm, bn))], 1024, bn=2048)
    ex.reduce("mlp")
    dx1, dx1b, d_norm_mlp = _rms_bwd("rms_mlp_bwd", dh2, x1, norm_mlp, dx2)

    def merge_bwd_epilogue(acc, ex_, outs):
        g, c = ex_[0][...].astype(F32), ex_[1][...].astype(F32)
        outs[0][...] = (acc * g[0]).astype(BF)
        outs[1][...] = (acc * g[1]).astype(BF)
        dga = acc * c[0]
        dgb = acc * c[1]
        outs[2][0] = dga.astype(BF)
        outs[2][1] = dgb.astype(BF)
        outs[3][...] = jnp.concatenate([jnp.sum(dga, axis=0, keepdims=True), jnp.sum(dgb, axis=0, keepdims=True)], 0)

    def pair(bm, bn):
        return pl.BlockSpec((2, bm, bn), lambda i, j, k: (0, i, j))

    n_row_blocks = S // _tile(S, 1024)
    dpa, dpb, dg3, db_gate = nt_rows(
        "out_proj_dx", dx1b, wout, merge_bwd_epilogue,
        lambda bm, bn: [(g3, pair(bm, bn)), (c3, pair(bm, bn))],
        lambda bm, bn: [(_sds((S, D), BF), plain_o(bm, bn)), (_sds((S, D), BF), plain_o(bm, bn)),
                        (_sds((2, S, D), BF), pair(bm, bn)),
                        (_sds((n_row_blocks, 2, D), F32), pl.BlockSpec((None, 2, bn), lambda i, j, k: (i, 0, j)))],
        bn=512)
    dw_out = tn_grad("out_proj_dw", merged, plain_a, dx1b, plain_b, D, D, (D, D), plain_o)

    pq = gpa.shape[2]
    proj_dx = lambda name, dproj, g: nt_cols(name, a_rows, dproj, g, S, _store(BF), lambda bm, bn: [],
                                             lambda bm, bn: [(_sds((S, 512), BF), plain_o(bm, bn))], 512)[0]
    dw_pa = tn_grad("proj_a_dw", y_a, plain_a, dpa, plain_b, 512, D, (N_CHIPS, 512, pq), cols_o(pq), bn=min(pq, 512))
    dw_pb = tn_grad("proj_b_dw", y_b, plain_a, dpb, plain_b, 512, D, (N_CHIPS, 512, pq), cols_o(pq), bn=min(pq, 512))
    ex.reduce("mix", partials=[dw_out.reshape(N_CHIPS, D // N_CHIPS, D), dw_pa, dw_pb])
    dy_a = proj_dx("proj_a_dx", dpa, gpa)
    dy_b = proj_dx("proj_b_dx", dpb, gpb)

    dqkv3 = lax.empty((3, S, QKV_W), BF)
    dqkv3 = _attn_a_bwd(qkv3, dy_a, y_a, lj, dqkv3, 0, DILATIONS[0])
    ex.reduce("mix")
    dy_views, y_views, lj_views = _dilated_rows("attn_a_bwd_rows", [dy_a, y_a, lj])
    dqkv_views = {d: _attn_a_bwd(qkv_views[d], dy_views[d], y_views[d], lj_views[d], None, grp, d)
                  for grp, d in enumerate(DILATIONS) if grp > 0}
    dqkv3 = _qkv_views("dqkv_from_views", dqkv3, dqkv_views)
    dqkv3, de2 = _attn_b_bwd(qkv3, e2, dy_b, y_b, lse_b, dqkv3)
    d_rpb = _table_grad_to_rpb(de2)

    def stacked_a(width):
        def spec(bm, bk):
            q = width // bk
            return pl.BlockSpec((None, bm, bk), lambda i, j, k: (k // q, i, k % q))
        return spec

    def stacked_b(width):
        def spec(bk, bn):
            q = width // bn
            return pl.BlockSpec((None, bk, bn), lambda i, j, k: (j // q, k, j % q))
        return spec

    ex.reduce("mlp")
    dw_qkv = tn_grad("qkv_dw", h1, plain_a, dqkv3, stacked_b(QKV_W), D, 3 * QKV_W, (N_CHIPS,) + gq.shape[1:],
                     cols_o(gq.shape[2]), bn=512)
    dw_gate = tn_grad("gate_dw", h1, plain_a, dg3, stacked_b(D), D, 2 * D, (N_CHIPS,) + gg.shape[1:],
                      cols_o(gg.shape[2]), bn=gg.shape[2])
    ex.reduce("in", partials=[dw_qkv, dw_gate])
    ex.reduce("mlp")
    dh1_q, *ex.updates["down"] = nt_cols(
        "qkv_dx", stacked_a(QKV_W), dqkv3, gq, S, _store(F32), lambda bm, bn: [],
        lambda bm, bn: [(_sds((S, D), F32), plain_o(bm, bn))], 512, bn=2048, side=ex.adamw_beside("down"))
    ex.reduce("in")
    ex.reduce("mix")

    def add_epilogue(acc, ex_, outs):
        outs[0][...] = acc + ex_[0][...]

    dh1, *ex.updates["up"] = nt_cols(
        "gate_dx", stacked_a(D), dg3, gg, S, add_epilogue, lambda bm, bn: [(dh1_q, plain_o(bm, bn))],
        lambda bm, bn: [(_sds((S, D), F32), plain_o(bm, bn))], gg.shape[2], side=ex.adamw_beside("up"))
    grad_x, _, d_norm_mix = _rms_bwd("rms_mix_bwd", dh1, x, norm_mix, dx1)
    ex.reduce("mix")

    small = [d_norm_mix, jnp.sum(db_gate, axis=0).reshape(1, 2 * D), d_rpb, d_norm_mlp, d_norm_final]
    return loss, grad_x, small


def _pack_small(parts, width):
    flat = jnp.concatenate([p.reshape(-1) for p in parts])
    return jnp.pad(flat, (0, 8 * width - flat.shape[0])).reshape(8, width)


def kernel(x, norm_mix, w_qkv, w_gate, b_gate, rpb, w_proj_a, w_proj_b, w_out, norm_mlp, w_up, w_down, norm_final, loss_target, m_norm_mix, m_w_qkv, m_w_gate, m_b_gate, m_rpb, m_w_proj_a, m_w_proj_b, m_w_out, m_norm_mlp, m_w_up, m_w_down, m_norm_final, v_norm_mix, v_w_qkv, v_w_gate, v_b_gate, v_rpb, v_w_proj_a, v_w_proj_b, v_w_out, v_norm_mlp, v_w_up, v_w_down, v_norm_final):
    names = ["qkv", "gate", "proj_a", "proj_b", "out", "up", "down"]
    big = dict(zip(names, [w_qkv[0], w_gate[0], w_proj_a[0], w_proj_b[0], w_out[0], w_up[0], w_down[0]]))
    big_m = dict(zip(names, [m_w_qkv[0], m_w_gate[0], m_w_proj_a[0], m_w_proj_b[0], m_w_out[0], m_w_up[0], m_w_down[0]]))
    big_v = dict(zip(names, [v_w_qkv[0], v_w_gate[0], v_w_proj_a[0], v_w_proj_b[0], v_w_out[0], v_w_up[0], v_w_down[0]]))

    c = lax.axis_index("c").astype(jnp.int32).reshape(1)
    me = (2 * lax.axis_index("x") + lax.axis_index("y")).astype(jnp.int32).reshape(1)
    ORDER.last = None
    ex = _Exchange(big, me, c, {n: (big_m[n], big_v[n]) for n in names})
    loss, grad_x, small = _forward_backward(x[0], loss_target[0], norm_mix, b_gate, rpb[0], norm_mlp, norm_final, ex)

    def adamw(group):
        return {n: ex.updates[n] if ex.updates.get(n) else _adamw(f"adamw_{n}", big[n], ex.grads[n], big_m[n], big_v[n])
                for n in _Exchange.REDUCE[group]}

    big_out = {**adamw("mlp"), **adamw("mix")}
    ex.reduce("in")

    small_w = [norm_mix, b_gate, rpb, norm_mlp, norm_final]
    count = sum(int(np.prod(p.shape)) for p in small_w)
    width = -(-count // (8 * 128)) * 128
    packed = _adamw_small(_gather_small(_pack_small(small, width)), _pack_small(small_w, width),
                          _pack_small([m_norm_mix, m_b_gate, m_rpb, m_norm_mlp, m_norm_final], width),
                          _pack_small([v_norm_mix, v_b_gate, v_rpb, v_norm_mlp, v_norm_final], width))
    ex.reduce("in")
    big_out.update(adamw("in"))

    def unpack(flat2d):
        flat, out, at = flat2d.reshape(-1), [], 0
        for p in small_w:
            size = int(np.prod(p.shape))
            out.append(flat[at:at + size].reshape(p.shape))
            at += size
        return out

    small_out = [unpack(a) for a in packed]

    def ordered(kind):
        sm = small_out[kind]
        bg = {n: o[kind][None] for n, o in big_out.items()}
        return [sm[0], bg["qkv"], bg["gate"], sm[1], sm[2], bg["proj_a"], bg["proj_b"], bg["out"], sm[3],
                bg["up"], bg["down"], sm[4]]

    total = lax.psum(loss[0, 0], ("x", "y", "c"))
    return (total, grad_x[None], *ordered(0), *ordered(1), *ordered(2), *ordered(3))
```

```python
import math

import numpy as np
import jax
import jax.numpy as jnp
from jax import lax
from jax.experimental import pallas as pl
from jax.experimental.pallas import tpu as pltpu

BF = jnp.bfloat16
F32 = jnp.float32
MESH = pl.DeviceIdType.MESH

HEAD_DIM = 128
N_HEADS = 16
N_HEADS_A = 12
QKV_W = N_HEADS * HEAD_DIM
DILATIONS = (1, 4, 16)
HALF_WINDOW = 64
GRID_W = 64
NA_ROWS = 8
NA_COLS = 16
RPB_ROWS = 2 * NA_ROWS - 1
RPB_COLS = 2 * NA_COLS - 1
EPS = 1e-6
NEG = -1e30
SCALE = HEAD_DIM ** -0.5

ADAM_LR = 0.001
ADAM_B1 = 0.9
ADAM_B2 = 0.999
ADAM_EPS = 1e-08
ADAM_WD = 0.01
ADAM_STEP = 10

N_CHIPS = 4
VMEM_LIMIT_BYTES = 48 * 1024 * 1024
QB = 256
NBR_SIDE = 16
ROW_TILE = 512


def _key_rows(L):
    return min(QB + 2 * HALF_WINDOW, L)


def _cparams(sem=None):
    return pltpu.CompilerParams(dimension_semantics=sem, vmem_limit_bytes=VMEM_LIMIT_BYTES)


def _tile(dim, want):
    t = min(dim, want)
    assert dim % t == 0, (dim, want)
    return t


class _ProgramOrder:
    def __init__(self):
        self.last = None

    def call(self, body, operands, in_specs, *, prefetch=(), grid=None, out_specs=None, chain_output=0, **kwargs):
        operands, in_specs = list(operands), list(in_specs)
        lead = len(prefetch) + len(operands)
        if self.last is not None and not any(op is self.last for op in operands):
            operands.append(self.last)
            in_specs.append(pl.BlockSpec(memory_space=pl.ANY))
            inner = body

            def body(*refs):
                return inner(*refs[:lead], *refs[lead + 1:])

        if prefetch:
            kwargs["grid_spec"] = pltpu.PrefetchScalarGridSpec(
                num_scalar_prefetch=len(prefetch), grid=grid, in_specs=in_specs, out_specs=out_specs)
        else:
            kwargs.update(in_specs=in_specs, out_specs=out_specs)
            if grid is not None:
                kwargs["grid"] = grid
        out = pl.pallas_call(body, **kwargs)(*prefetch, *operands)
        self.last = out[chain_output] if isinstance(out, (tuple, list)) else out
        return out


ORDER = _ProgramOrder()


NN = ((1,), (0,))
NT = ((1,), (1,))
TN = ((0,), (0,))


def _matmul(name, a, b, a_spec, b_spec, dims, grid, acc_shape, extras, outs, epilogue, precision=None,
            prefetch=(), into=None, side=None):
    n_ex, n_out, nk = len(extras), len(outs), grid[2]
    side_fn, side_in, n_side_out = side if side is not None else (None, [], 0)
    side_spec = None
    n_in = 2 + n_ex + len(side_in) + (into is not None)
    if side is not None:
        R, C = side_in[0].shape
        steps = grid[0] * grid[1] * grid[2]
        side_blocks = max(n for n in range(1, steps + 1) if R % n == 0 and (R // n) % 8 == 0)

        def side_step(*ids):
            return (ids[0] * grid[1] + ids[1]) * grid[2] + ids[2]

        side_spec = pl.BlockSpec((R // side_blocks, C),
                                 lambda *ids: (jnp.minimum(side_step(*ids), side_blocks - 1), 0))

    def body(*refs):
        refs = refs[len(prefetch):]
        a_ref, b_ref = refs[0], refs[1]
        ex_refs = refs[2:2 + n_ex]
        out_refs = refs[n_in:n_in + n_out]
        if side is not None:
            @pl.when(side_step(pl.program_id(0), pl.program_id(1), pl.program_id(2)) < side_blocks)
            def _():
                results = side_fn(*[r[...] for r in refs[2 + n_ex:2 + n_ex + len(side_in)]])
                for o_ref, value in zip(refs[n_in + n_out:n_in + n_out + n_side_out], results):
                    o_ref[...] = value

        def dot():
            return lax.dot_general(a_ref[...], b_ref[...], (dims, ((), ())),
                                   preferred_element_type=F32, precision=precision)

        if nk == 1:
            epilogue(dot(), ex_refs, out_refs)
            return
        acc_ref = refs[-1]
        k = pl.program_id(2)

        @pl.when(k == 0)
        def _():
            acc_ref[...] = dot()

        if nk > 2:
            @pl.when((k > 0) & (k < nk - 1))
            def _():
                acc_ref[...] += dot()

        @pl.when(k == nk - 1)
        def _():
            epilogue(acc_ref[...] + dot(), ex_refs, out_refs)

    operands = [a, b] + [e for e, _ in extras] + list(side_in)
    in_specs = [a_spec, b_spec] + [s for _, s in extras] + [side_spec] * len(side_in)
    kwargs = {}
    if into is not None:
        operands.append(into)
        in_specs.append(pl.BlockSpec(memory_space=pl.ANY))
        kwargs["input_output_aliases"] = {len(prefetch) + n_in - 1: 0}
    return ORDER.call(
        body, operands, in_specs, prefetch=prefetch, name=name, grid=grid,
        out_specs=[s for _, s in outs] + [side_spec] * n_side_out,
        out_shape=[sh for sh, _ in outs] + [_sds(s_.shape, F32) for s_ in side_in[:1]] * n_side_out,
        scratch_shapes=[pltpu.VMEM(acc_shape, F32)] if nk > 1 else [],
        compiler_params=_cparams(("parallel", "parallel", "arbitrary")), **kwargs,
    )


def _mm_nn_shards(name, a, w, me, own, out, out_block, epilogue, extras=(), into=None, tn=512):
    M, K = a.shape
    Nq = w.shape[-1]
    tm, tn = _tile(M, 1024), _tile(Nq, tn)
    q = Nq // tn

    def tile(j, me_ref):
        shard = me_ref[0] if own else (me_ref[0] + 1 + j // q) % N_CHIPS
        return shard, j % q, shard * q + j % q

    if own:
        b_spec = pl.BlockSpec((K, tn), lambda i, j, k, me_ref: (0, j))
    else:
        b_spec = pl.BlockSpec((None, K, tn), lambda i, j, k, me_ref: (tile(j, me_ref)[0], 0, tile(j, me_ref)[1]))
    shape, dtype = out
    out_spec = pl.BlockSpec((None, tm, tn), lambda i, j, k, me_ref: out_block(i, tile(j, me_ref)[2]))
    ex = [(e, pl.BlockSpec((1, tn), lambda i, j, k, me_ref: (0, tile(j, me_ref)[2]))) for e in extras]
    return _matmul(name, a, w, pl.BlockSpec((tm, K), lambda i, j, k, me_ref: (i, 0)), b_spec, NN,
                   (M // tm, q if own else (N_CHIPS - 1) * q, 1), (tm, tn), ex, [(_sds(shape, dtype), out_spec)],
                   epilogue, prefetch=(me,), into=into)[0]


def _store(dtype):
    def epilogue(acc, ex, outs):
        outs[0][...] = acc.astype(dtype)
    return epilogue


def _sds(shape, dtype):
    return jax.ShapeDtypeStruct(shape, dtype)


def _mm_nn_cols(name, a, g, out_dtype, epilogue=None, extras=(), outs=None, tm=1024, tn=1024, tk=2048):
    M, K = a.shape
    _, _, Nq = g.shape
    tm, tn, tk = _tile(M, tm), _tile(Nq, tn), _tile(K, tk)
    q = Nq // tn
    grid = (M // tm, N_CHIPS * q, K // tk)
    if outs is None:
        outs = [(_sds((M, N_CHIPS * Nq), out_dtype), pl.BlockSpec((tm, tn), lambda i, j, k: (i, j)))]
    return _matmul(name, a, g, pl.BlockSpec((tm, tk), lambda i, j, k: (i, k)),
                   pl.BlockSpec((None, tk, tn), lambda i, j, k: (j // q, k, j % q)), NN, grid, (tm, tn),
                   list(extras), outs, epilogue or _store(out_dtype)), (tm, tn, tk)


def _rms_fwd(name, x, g):
    S, D = x.shape
    tm = _tile(S, ROW_TILE)

    def body(x_ref, g_ref, h_ref):
        xv = x_ref[...]
        r = lax.rsqrt(jnp.mean(xv * xv, axis=-1, keepdims=True) + EPS)
        h_ref[...] = ((xv * r) * g_ref[...]).astype(BF)

    row = pl.BlockSpec((tm, D), lambda i: (i, 0))
    return ORDER.call(
        body, [x, g], [row, pl.BlockSpec((1, D), lambda i: (0, 0))], name=name, grid=(S // tm,),
        out_specs=row, out_shape=_sds((S, D), BF), compiler_params=_cparams(("parallel",)),
    )


def _rms_bwd(name, dh, x, g, dres):
    S, D = x.shape
    tm = _tile(S, ROW_TILE // 2)

    def body(dh_ref, x_ref, g_ref, dres_ref, dx_ref, dxb_ref, dg_ref):
        xv = x_ref[...]
        r = lax.rsqrt(jnp.mean(xv * xv, axis=-1, keepdims=True) + EPS)
        n = xv * r
        dhv = dh_ref[...]
        dyg = dhv * g_ref[...]
        dx = dres_ref[...] + r * (dyg - n * jnp.mean(dyg * n, axis=-1, keepdims=True))
        dx_ref[...] = dx
        dxb_ref[...] = dx.astype(BF)

        @pl.when(pl.program_id(0) == 0)
        def _():
            dg_ref[...] = jnp.zeros_like(dg_ref)

        dg_ref[...] += jnp.sum(dhv * n, axis=0, keepdims=True)

    row = pl.BlockSpec((tm, D), lambda i: (i, 0))
    vec = pl.BlockSpec((1, D), lambda i: (0, 0))
    return ORDER.call(
        body, [dh, x, g, dres], [row, row, vec, row], name=name, grid=(S // tm,),
        out_specs=[row, row, vec],
        out_shape=[_sds((S, D), F32), _sds((S, D), BF), _sds((1, D), F32)],
        compiler_params=_cparams(("arbitrary",)),
    )


def _loss_head(x2, target, g):
    S, D = x2.shape
    tm = _tile(S, ROW_TILE)

    def body(x_ref, t_ref, g_ref, loss_ref, dx_ref, dxb_ref, dg_ref):
        xv = x_ref[...]
        gv = g_ref[...]
        r = lax.rsqrt(jnp.mean(xv * xv, axis=-1, keepdims=True) + EPS)
        n = xv * r
        e = n * gv - t_ref[...]
        dy = e * (1.0 / D)
        dyg = dy * gv
        dx = r * (dyg - n * jnp.mean(dyg * n, axis=-1, keepdims=True))
        dx_ref[...] = dx
        dxb_ref[...] = dx.astype(BF)

        @pl.when(pl.program_id(0) == 0)
        def _():
            dg_ref[...] = jnp.zeros_like(dg_ref)
            loss_ref[...] = jnp.zeros_like(loss_ref)

        dg_ref[...] += jnp.sum(dy * n, axis=0, keepdims=True)
        per_row = jnp.mean(e * e, axis=-1, keepdims=True)
        loss_ref[...] += 0.5 * jnp.sum(per_row, axis=0, keepdims=True)

    row = pl.BlockSpec((tm, D), lambda i: (i, 0))
    vec = pl.BlockSpec((1, D), lambda i: (0, 0))
    return ORDER.call(
        body, [x2, target, g], [row, row, vec], name="loss_head", grid=(S // tm,),
        out_specs=[pl.BlockSpec((1, 1), lambda i: (0, 0)), row, row, vec],
        out_shape=[_sds((1, 1), F32), _sds((S, D), F32), _sds((S, D), BF), _sds((1, D), F32)],
        compiler_params=_cparams(("arbitrary",)), chain_output=1,
    )


def _chains(L):
    side = min(8, L // QB)
    return side, max(1, 4 // side)


def _band_scores(qkv_ref, i, L, coef, head):
    KB = _key_rows(L)
    lanes = pl.ds(head * HEAD_DIM, HEAD_DIM)
    q0 = pl.multiple_of(i * QB, QB)
    ks = pl.multiple_of(jnp.clip(i * QB - HALF_WINDOW, 0, L - KB), HALF_WINDOW)
    q = qkv_ref[0, pl.ds(q0, QB), lanes]
    k = qkv_ref[1, pl.ds(ks, KB), lanes]
    v = qkv_ref[2, pl.ds(ks, KB), lanes]
    s = lax.dot_general(q, k, (NT, ((), ())), preferred_element_type=F32) * SCALE
    qpos = q0 + lax.broadcasted_iota(jnp.int32, (QB, KB), 0)
    kpos = ks + lax.broadcasted_iota(jnp.int32, (QB, KB), 1)
    rel = jnp.abs(kpos - qpos)
    valid = rel <= HALF_WINDOW
    s = jnp.where(valid, s - coef * rel.astype(F32), NEG)
    return q0, ks, q, k, v, s, valid


def _alibi_coefs(group, d, heads):
    first = 4 * group + 1 + pl.program_id(1) * heads
    scale = jnp.full((1, 1), -(8.0 / N_HEADS_A) * math.log(2.0), F32)
    return [jnp.exp(scale * (first + hh).astype(F32)) * float(d) for hh in range(heads)]


def _dilated_view(qkv3, group, d, heads):
    per = 4 // heads
    L = qkv3.shape[1]
    if d == 1:
        return qkv3, pl.BlockSpec((3, L, heads * HEAD_DIM), lambda r, j: (0, 0, per * group + j))
    return qkv3, pl.BlockSpec((3, L, heads * HEAD_DIM), lambda r, j: (0, 0, r * per + j))


def _qkv_views(name, qkv3, views=None):
    _, S, _ = qkv3.shape
    W = 512
    tm = _tile(S, ROW_TILE)
    dilated = [(g, d) for g, d in enumerate(DILATIONS) if d > 1]
    first = dilated[0][0]
    assert [g for g, _ in dilated] == list(range(first, first + len(dilated)))
    nc = W // 128
    to_views = views is None

    def body(*refs):
        scr = refs[-nc:]
        if to_views:
            src, outs = refs[0], refs[1:1 + len(dilated)]
        else:
            ins, dst = refs[:len(dilated)], refs[len(dilated) + 1]
        for k, (_, d) in enumerate(dilated):
            @pl.when(pl.program_id(1) == k)
            def _():
                for w in range(3):
                    for c in range(nc):
                        if to_views:
                            scr[c][...] = src[w, :, c * 128:(c + 1) * 128].astype(F32)
                    for r in range(d):
                        for c in range(nc):
                            at = r * W + c * 128
                            if to_views:
                                outs[k][w, :, at:at + 128] = scr[c][pl.ds(r, tm // d, stride=d), :].astype(BF)
                            else:
                                scr[c][pl.ds(r, tm // d, stride=d), :] = ins[k][w, :, at:at + 128].astype(F32)
                    for c in range(nc):
                        if not to_views:
                            dst[w, :, c * 128:(c + 1) * 128] = scr[c][...].astype(BF)

    cols = pl.BlockSpec((3, tm, W), lambda i, k: (0, i, first + k))
    rows = [pl.BlockSpec((3, tm // d, d * W), lambda i, k: (0, i, 0)) for _, d in dilated]
    shapes = [_sds((3, S // d, d * W), BF) for _, d in dilated]
    common = dict(name=name, grid=(S // tm, len(dilated)), scratch_shapes=[pltpu.VMEM((tm, 128), F32)] * nc,
                  compiler_params=_cparams(("parallel", "arbitrary")))
    if to_views:
        outs = ORDER.call(body, [qkv3], [cols], out_specs=rows, out_shape=shapes, **common)
        return {d: o for (_, d), o in zip(dilated, outs)}
    return ORDER.call(body, [views[d] for _, d in dilated] + [qkv3], rows + [pl.BlockSpec(memory_space=pl.ANY)],
                      out_specs=cols, out_shape=_sds(qkv3.shape, BF), input_output_aliases={len(dilated): 0}, **common)


def _attn_a_fwd(qkv3, group, d):
    L = qkv3.shape[1]
    S = L * d
    assert L % QB == 0
    side, heads = _chains(L)
    view, blocks_spec = _dilated_view(qkv3, group, d, heads)

    def body(qkv_ref, o_ref, lse_ref):
        coefs = _alibi_coefs(group, d, heads)

        def step(i, carry):
            chains = [(hh, _band_scores(qkv_ref, side * i + u, L, coefs[hh], hh))
                      for u in range(side) for hh in range(heads)]
            soft = []
            for hh, (q0, _, _, _, v, s, _) in chains:
                m = jnp.max(s, axis=-1, keepdims=True)
                p = jnp.exp(s - m)
                den = jnp.sum(p, axis=-1, keepdims=True)
                soft.append((hh, q0, (p / den).astype(BF), v, m + jnp.log(den)))
            for hh, q0, pn, v, lse in soft:
                lanes = pl.ds(hh * HEAD_DIM, HEAD_DIM)
                o_ref[pl.ds(q0, QB), lanes] = jnp.dot(pn, v, preferred_element_type=F32)
                lse_ref[pl.ds(q0, QB), lanes] = jnp.broadcast_to(lse, (QB, HEAD_DIM))
            return carry

        lax.fori_loop(0, L // QB // side, step, 0)

    per = 4 // heads
    out = pl.BlockSpec((L, heads * HEAD_DIM), lambda r, j: (0, r * per + j))
    o, lse = ORDER.call(
        body, [view], [blocks_spec],
        name=f"attn_a_fwd_d{d}", grid=(d, per),
        out_specs=[out, out],
        out_shape=[_sds((L, d * 512), F32), _sds((L, d * 512), F32)],
        compiler_params=_cparams(("parallel", "parallel")),
    )
    return o, lse


def _dilated_rows(name, arrays):
    S, W = arrays[0].shape
    tm = _tile(S, ROW_TILE)
    ds_ = [d for d in DILATIONS if d > 1]
    n = len(arrays)

    def body(*refs):
        nc = W // 128
        ins, outs, scr = refs[:n], refs[n:-nc], refs[-nc:]
        for a, src in enumerate(ins):
            for c in range(nc):
                scr[c][...] = src[:, c * 128:(c + 1) * 128].astype(F32)
            for k, d in enumerate(ds_):
                dst = outs[a * len(ds_) + k]
                for r in range(d):
                    for c in range(nc):
                        at = r * W + c * 128
                        dst[:, at:at + 128] = scr[c][pl.ds(r, tm // d, stride=d), :].astype(dst.dtype)

    row = pl.BlockSpec((tm, W), lambda i: (i, 0))
    out_specs, out_shape = [], []
    for a in arrays:
        for d in ds_:
            out_specs.append(pl.BlockSpec((tm // d, d * W), lambda i: (i, 0)))
            out_shape.append(_sds((S // d, d * W), a.dtype))
    outs = ORDER.call(body, list(arrays), [row] * n, name=name, grid=(S // tm,), out_specs=out_specs,
                      out_shape=out_shape, scratch_shapes=[pltpu.VMEM((tm, 128), F32)] * (W // 128),
                      compiler_params=_cparams(("parallel",)))
    return [{d: outs[a * len(ds_) + k] for k, d in enumerate(ds_)} for a in range(n)]


def _attn_a_combine(os_, lses):
    W = 512
    S = os_[0].shape[0] * DILATIONS[0]
    tm = _tile(S, ROW_TILE)
    nc = W // 128
    dilated = [g for g, d in enumerate(DILATIONS) if d > 1]

    def body(o0, o1, o2, l0, l1, l2, y_ref, lj_ref, *scr):
        def token_order(src, g, slot):
            d = DILATIONS[g]
            if d == 1:
                return src[...]
            bufs = scr[slot * nc:(slot + 1) * nc]
            for r in range(d):
                for c in range(nc):
                    at = r * W + c * 128
                    bufs[c][pl.ds(r, tm // d, stride=d), :] = src[:, at:at + 128]
            return jnp.concatenate([buf[...] for buf in bufs], axis=1)

        slots = {g: k for k, g in enumerate(dilated)}
        ls = [token_order(l, g, slots.get(g, 0)) for g, l in enumerate((l0, l1, l2))]
        os_tok = [token_order(o, g, len(dilated) + slots.get(g, 0)) for g, o in enumerate((o0, o1, o2))]
        m = jnp.maximum(jnp.maximum(ls[0], ls[1]), ls[2])
        es = [jnp.exp(l - m) for l in ls]
        den = es[0] + es[1] + es[2]
        y = (es[0] / den) * os_tok[0] + (es[1] / den) * os_tok[1] + (es[2] / den) * os_tok[2]
        y_ref[...] = y.astype(BF)
        lj_ref[...] = m + jnp.log(den)

    row = pl.BlockSpec((tm, W), lambda i: (i, 0))
    views = [pl.BlockSpec((tm // d, d * W), lambda i: (i, 0)) for d in DILATIONS]
    return ORDER.call(
        body, [*os_, *lses], views + views, name="attn_a_combine", grid=(S // tm,), out_specs=[row, row],
        out_shape=[_sds((S, W), BF), _sds((S, W), F32)],
        scratch_shapes=[pltpu.VMEM((tm, 128), F32)] * (2 * len(dilated) * nc),
        compiler_params=_cparams(("parallel",)),
    )


def _attn_a_bwd(qkv3, dy, y, lj, dqkv3, group, d):
    L = qkv3.shape[1]
    S = L * d
    side, heads = _chains(L)
    view, blocks_spec = _dilated_view(qkv3, group, d, heads)

    def body(qkv_ref, dy_ref, y_ref, lj_ref, *rest):
        out_ref, dk_acc, dv_acc = rest[-3:]
        coefs = _alibi_coefs(group, d, heads)
        dk_acc[...] = jnp.zeros_like(dk_acc)
        dv_acc[...] = jnp.zeros_like(dv_acc)

        def step(i, carry):
            chains = [(pl.ds(hh * HEAD_DIM, HEAD_DIM), _band_scores(qkv_ref, side * i + u, L, coefs[hh], hh))
                      for u in range(side) for hh in range(heads)]
            dys = [dy_ref[pl.ds(c[0], QB), lanes] for lanes, c in chains]
            dps = [lax.dot_general(dyv, c[4], (NT, ((), ())), preferred_element_type=F32)
                   for dyv, (_, c) in zip(dys, chains)]
            grads = []
            for (lanes, (q0, ks, q, k, v, s, valid)), dyv, dp in zip(chains, dys, dps):
                rows = pl.ds(q0, QB)
                delta = jnp.sum(dyv.astype(F32) * y_ref[rows, lanes].astype(F32), axis=-1, keepdims=True)
                p = jnp.where(valid, jnp.exp(s - jnp.tile(lj_ref[rows, lanes], (1, _key_rows(L) // HEAD_DIM))), 0.0)
                grads.append(((p * (dp - delta)).astype(BF), p.astype(BF)))
            for (lanes, (q0, ks, q, k, v, s, valid)), dyv, (ds, pb) in zip(chains, dys, grads):
                out_ref[0, pl.ds(q0, QB), lanes] = (jnp.dot(ds, k, preferred_element_type=F32) * SCALE).astype(BF)
                keys = pl.ds(ks, _key_rows(L))
                dk_acc[keys, lanes] += lax.dot_general(ds, q, (TN, ((), ())), preferred_element_type=F32) * SCALE
                dv_acc[keys, lanes] += lax.dot_general(pb, dyv, (TN, ((), ())), preferred_element_type=F32)
            return carry

        lax.fori_loop(0, L // QB // side, step, 0)
        out_ref[1] = dk_acc[...].astype(BF)
        out_ref[2] = dv_acc[...].astype(BF)

    per = 4 // heads
    width = heads * HEAD_DIM
    row = pl.BlockSpec((L, width), lambda r, j: (0, r * per + j))
    operands = [view, dy, y, lj]
    scratch = [pltpu.VMEM((L, width), F32), pltpu.VMEM((L, width), F32)]
    if d == 1:
        return ORDER.call(
            body, operands + [dqkv3], [blocks_spec, row, row, row, pl.BlockSpec(memory_space=pl.ANY)],
            name=f"attn_a_bwd_d{d}", grid=(d, per), out_specs=blocks_spec, out_shape=_sds((3, S, QKV_W), BF),
            scratch_shapes=scratch, input_output_aliases={4: 0}, compiler_params=_cparams(("parallel", "parallel")))
    return ORDER.call(
        body, operands, [blocks_spec, row, row, row], name=f"attn_a_bwd_d{d}", grid=(d, per),
        out_specs=blocks_spec, out_shape=_sds((3, L, d * 512), BF),
        scratch_shapes=scratch, compiler_params=_cparams(("parallel", "parallel")))


def _toeplitz_onehot():
    oh = np.zeros((64, GRID_W, 128), np.float32)
    for qc in range(GRID_W):
        for m in range(128):
            kc = m % GRID_W
            dc = int(np.clip(kc - qc, -(NA_COLS - 1), NA_COLS - 1)) + NA_COLS - 1
            oh[(m // GRID_W) * 32 + dc, qc, m] = 1.0
    return oh.reshape(64, GRID_W * 128)


def _nbr_scores(qkv_ref, e2_ref, r, rows, ok):
    rs = jnp.clip(r - NA_ROWS // 2, 0, rows - NA_ROWS)
    q0 = pl.multiple_of(r * GRID_W, GRID_W)
    k0 = pl.multiple_of(rs * GRID_W, GRID_W)
    q = qkv_ref[0, pl.ds(q0, GRID_W), :]
    k = qkv_ref[1, pl.ds(k0, NA_ROWS * GRID_W), :]
    v = qkv_ref[2, pl.ds(k0, NA_ROWS * GRID_W), :]
    s = lax.dot_general(q, k, (NT, ((), ())), preferred_element_type=F32) * SCALE
    first = rs - r + NA_ROWS - 1
    bias = jnp.concatenate([e2_ref[first + 2 * pair] for pair in range(NA_ROWS // 2)], axis=1)
    s = jnp.where(ok, s + bias, NEG)
    return q0, k0, first, q, k, v, s


def _nbr_col_ok():
    qc = lax.broadcasted_iota(jnp.int32, (GRID_W, NA_ROWS * GRID_W), 0)
    kc = lax.broadcasted_iota(jnp.int32, (GRID_W, NA_ROWS * GRID_W), 1) % GRID_W
    cs = jnp.clip(qc - NA_COLS // 2, 0, GRID_W - NA_COLS)
    return (kc >= cs) & (kc < cs + NA_COLS)


def _attn_b_fwd(qkv3, e2):
    _, S, _ = qkv3.shape
    rows = S // GRID_W
    assert rows >= NA_ROWS

    def body(qkv_ref, e2_ref, o_ref, lse_ref):
        ok = _nbr_col_ok()

        def step(i, carry):
            blocks = [_nbr_scores(qkv_ref, e2_ref, NBR_SIDE * i + u, rows, ok) for u in range(NBR_SIDE)]
            soft = []
            for q0, _, _, _, _, v, s in blocks:
                m = jnp.max(s, axis=-1, keepdims=True)
                p = jnp.exp(s - m)
                den = jnp.sum(p, axis=-1, keepdims=True)
                soft.append((q0, (p / den).astype(BF), v, m + jnp.log(den)))
            for q0, pn, v, lse in soft:
                o_ref[pl.ds(q0, GRID_W), :] = jnp.dot(pn, v, preferred_element_type=F32).astype(BF)
                lse_ref[pl.ds(q0, GRID_W), :] = jnp.broadcast_to(lse, (GRID_W, HEAD_DIM))
            return carry

        lax.fori_loop(0, rows // NBR_SIDE, step, 0)

    out = pl.BlockSpec((S, HEAD_DIM), lambda h: (0, h))
    return ORDER.call(
        body, [qkv3, e2],
        [pl.BlockSpec((3, S, HEAD_DIM), lambda h: (0, 0, N_HEADS_A + h)),
         pl.BlockSpec((None, RPB_ROWS - 1, GRID_W, 128), lambda h: (h, 0, 0, 0))],
        name="attn_b_fwd", grid=(4,),
        out_specs=[out, out], out_shape=[_sds((S, 512), BF), _sds((S, 512), F32)],
        compiler_params=_cparams(("parallel",)),
    )


def _attn_b_bwd(qkv3, e2, dy, y, lse, dqkv3):
    _, S, _ = qkv3.shape
    rows = S // GRID_W
    nk = NA_ROWS * GRID_W

    def body(qkv_ref, e2_ref, dy_ref, y_ref, lse_ref, _, out_ref, de2_ref, dk_acc, dv_acc):
        ok = _nbr_col_ok()
        dk_acc[...] = jnp.zeros_like(dk_acc)
        dv_acc[...] = jnp.zeros_like(dv_acc)
        de2_ref[...] = jnp.zeros_like(de2_ref)

        def step(i, carry):
            blocks = [_nbr_scores(qkv_ref, e2_ref, NBR_SIDE * i + u, rows, ok) for u in range(NBR_SIDE)]
            dys = [dy_ref[pl.ds(b[0], GRID_W), :] for b in blocks]
            dps = [lax.dot_general(dyv, b[5], (NT, ((), ())), preferred_element_type=F32) for dyv, b in zip(dys, blocks)]
            grads = []
            for (q0, k0, first, q, k, v, s), dyv, dp in zip(blocks, dys, dps):
                qrows = pl.ds(q0, GRID_W)
                delta = jnp.sum(dyv.astype(F32) * y_ref[qrows, :].astype(F32), axis=-1, keepdims=True)
                p = jnp.where(ok, jnp.exp(s - jnp.tile(lse_ref[qrows, :], (1, nk // HEAD_DIM))), 0.0)
                ds = p * (dp - delta)
                for pair in range(NA_ROWS // 2):
                    de2_ref[first + 2 * pair] += ds[:, pair * 128:(pair + 1) * 128]
                grads.append((ds.astype(BF), p.astype(BF)))
            for (q0, k0, first, q, k, v, s), dyv, (dsb, pb) in zip(blocks, dys, grads):
                out_ref[0, pl.ds(q0, GRID_W), :] = (jnp.dot(dsb, k, preferred_element_type=F32) * SCALE).astype(BF)
                keys = pl.ds(k0, nk)
                dk_acc[keys, :] += lax.dot_general(dsb, q, (TN, ((), ())), preferred_element_type=F32) * SCALE
                dv_acc[keys, :] += lax.dot_general(pb, dyv, (TN, ((), ())), preferred_element_type=F32)
            return carry

        lax.fori_loop(0, rows // NBR_SIDE, step, 0)
        out_ref[1] = dk_acc[...].astype(BF)
        out_ref[2] = dv_acc[...].astype(BF)

    heads = pl.BlockSpec((3, S, HEAD_DIM), lambda h: (0, 0, N_HEADS_A + h))
    row = pl.BlockSpec((S, HEAD_DIM), lambda h: (0, h))
    table = pl.BlockSpec((None, RPB_ROWS - 1, GRID_W, 128), lambda h: (h, 0, 0, 0))
    return ORDER.call(
        body, [qkv3, e2, dy, y, lse, dqkv3],
        [heads, table, row, row, row, pl.BlockSpec(memory_space=pl.ANY)], name="attn_b_bwd", grid=(4,),
        out_specs=[heads, table],
        out_shape=[_sds((3, S, QKV_W), BF), _sds((4, RPB_ROWS - 1, GRID_W, 128), F32)],
        scratch_shapes=[pltpu.VMEM((S, HEAD_DIM), F32), pltpu.VMEM((S, HEAD_DIM), F32)],
        input_output_aliases={5: 0},
        compiler_params=_cparams(("parallel",)), chain_output=1,
    )


def _rpb_to_table(rpb):
    pad = jnp.pad(rpb, ((0, 0), (0, 0), (0, 1)))
    pairs = jnp.concatenate([pad[:, :-1], pad[:, 1:]], axis=-1).reshape(4 * (RPB_ROWS - 1), 64)
    onehot = jnp.asarray(_toeplitz_onehot())
    n = onehot.shape[1]
    tn = 2048
    full = lambda i, j, k: (0, 0)
    (e2,) = _matmul("rpb_table", pairs, onehot, pl.BlockSpec(pairs.shape, full),
                    pl.BlockSpec((64, tn), lambda i, j, k: (0, j)), NN, (1, n // tn, 1), (pairs.shape[0], tn), [],
                    [(_sds((pairs.shape[0], n), F32), pl.BlockSpec((pairs.shape[0], tn), lambda i, j, k: (0, j)))],
                    _store(F32), precision=lax.Precision.HIGHEST)
    return e2.reshape(4, RPB_ROWS - 1, GRID_W, 128)


def _table_grad_to_rpb(de2):
    onehot = jnp.asarray(_toeplitz_onehot())
    n = onehot.shape[1]
    flat = de2.reshape(4 * (RPB_ROWS - 1), n)
    tk = 2048
    (dpairs,) = _matmul("rpb_table_grad", flat, onehot, pl.BlockSpec((flat.shape[0], tk), lambda i, j, k: (0, k)),
                        pl.BlockSpec((64, tk), lambda i, j, k: (0, k)), NT, (1, 1, n // tk), (flat.shape[0], 64), [],
                        [(_sds((flat.shape[0], 64), F32), pl.BlockSpec((flat.shape[0], 64), lambda i, j, k: (0, 0)))],
                        _store(F32), precision=lax.Precision.HIGHEST)
    dpairs = dpairs.reshape(4, RPB_ROWS - 1, 64)
    zero = jnp.zeros((4, 1, RPB_COLS), F32)
    return (jnp.concatenate([dpairs[:, :, :RPB_COLS], zero], axis=1)
            + jnp.concatenate([zero, dpairs[:, :, 32:32 + RPB_COLS]], axis=1))


HBM = pl.BlockSpec(memory_space=pl.ANY)


def _place():
    x, y, c = lax.axis_index("x"), lax.axis_index("y"), lax.axis_index("c")
    chips = [(1 - x, y), (x, 1 - y), (1 - x, 1 - y)]
    return x, y, c, chips


def _remote(src, dst, send_sem, recv_sem, to):
    return pltpu.make_async_remote_copy(src_ref=src, dst_ref=dst, send_sem=send_sem, recv_sem=recv_sem,
                                        device_id=to, device_id_type=MESH)


def _place_shard(name, w, me, plain=False):
    R, C = w.shape
    tr = _tile(R, 256)

    def body(me_ref, w_ref, *o_refs):
        for o_ref in o_refs:
            o_ref[...] = w_ref[...].astype(BF)

    row = pl.BlockSpec((tr, C), lambda i, mr: (i, 0))
    placed = pl.BlockSpec((None, tr, C), lambda i, mr: (mr[0], i, 0))
    return ORDER.call(
        body, [w], [row], prefetch=(me,), name=name, grid=(R // tr,),
        out_specs=[placed, row] if plain else [placed],
        out_shape=[_sds((N_CHIPS, R, C), BF)] + ([_sds((R, C), BF)] if plain else []),
        compiler_params=_cparams(("parallel",)),
    )


SEM = pl.BlockSpec(memory_space=pltpu.SEMAPHORE)
IN_HBM = pl.BlockSpec(memory_space=pltpu.HBM)
DATAFLOW = pltpu.SideEffectType.DATAFLOW_SIDE_EFFECTING


def _in_hbm(a):
    return pltpu.with_memory_space_constraint(a, pltpu.HBM)


def _copy_start(name, bufs, copies, n_copies, earlier=None):
    n = len(bufs)
    after = None if any(b is ORDER.last for b in bufs) else ORDER.last
    n_extra = (2 if earlier is not None else 0) + (1 if after is not None else 0)

    def body(*refs):
        ins = refs[:n]
        if earlier is not None:
            for k, (src, dst, to) in enumerate(earlier[0](ins)):
                cp = _remote(src, dst, refs[n].at[k], refs[n + 1].at[k], to)
                cp.wait_send()
                cp.wait_recv()
        send_sems, recv_sems = refs[n + n_extra], refs[n + n_extra + 1]
        for k, (src, dst, to) in enumerate(copies(ins)):
            _remote(src, dst, send_sems.at[k], recv_sems.at[k], to).start()
        refs[-1][...] = jnp.zeros((8, 128), F32)

    operands = [_in_hbm(b) for b in bufs]
    in_specs = [IN_HBM] * n
    if earlier is not None:
        operands += [earlier[1], earlier[2]]
        in_specs += [SEM, SEM]
    if after is not None:
        operands.append(after)
        in_specs.append(HBM)
    outs = pl.pallas_call(
        body, name=name,
        out_shape=(pltpu.SemaphoreType.DMA((n_copies,)), pltpu.SemaphoreType.DMA((n_copies,)),
                   *[pltpu.HBM(b.shape, b.dtype) for b in bufs], _sds((8, 128), F32)),
        in_specs=in_specs,
        out_specs=(SEM, SEM, *[IN_HBM] * n, pl.BlockSpec(memory_space=pltpu.VMEM)),
        input_output_aliases={i: 2 + i for i in range(n)},
        compiler_params=pltpu.CompilerParams(has_side_effects=DATAFLOW),
    )(*operands)
    ORDER.last = outs[-1]
    return outs[0], outs[1], list(outs[2:2 + n])


def _copy_wait(name, bufs, copies, send_sems, recv_sems):
    n = len(bufs)
    after = ORDER.last

    def body(*refs):
        ins = refs[:n]
        for k, (src, dst, to) in enumerate(copies(ins)):
            cp = _remote(src, dst, refs[n].at[k], refs[n + 1].at[k], to)
            cp.wait_send()
            cp.wait_recv()

    outs = list(pl.pallas_call(
        body, name=name,
        out_shape=tuple(pltpu.HBM(b.shape, b.dtype) for b in bufs),
        in_specs=[IN_HBM] * n + [SEM, SEM, HBM], out_specs=tuple([IN_HBM] * n),
        input_output_aliases={i: i for i in range(n)},
        compiler_params=pltpu.CompilerParams(has_side_effects=DATAFLOW),
    )(*bufs, send_sems, recv_sems, after))
    ORDER.last = outs[0]
    return outs


def _gather_hop1(bufs):
    x, y, c, chips = _place()
    out = []
    for b in bufs:
        half = b.shape[1] // 2
        mine = b.at[2 * x + y, pl.ds(c * half, half), :]
        out += [(mine, mine, (*chip, c)) for chip in chips]
    return out


def _gather_hop2(bufs):
    x, y, c, chips = _place()
    out = []
    for b in bufs:
        half = b.shape[1] // 2
        for chip in chips:
            landed = b.at[2 * chip[0] + chip[1], pl.ds(c * half, half), :]
            out.append((landed, landed, (x, y, 1 - c)))
    return out


def _swap_copies(bufs):
    x, y, c, _ = _place()
    n = len(bufs) // 2
    out = []
    for p, land in zip(bufs[:n], bufs[n:]):
        half = p.shape[1] // 2
        out.append((p.at[:, pl.ds((1 - c) * half, half), :], land, (x, y, 1 - c)))
    return out


def _scatter_copies(bufs):
    _, _, c, chips = _place()
    n = len(bufs) // 2
    out = []
    for s_, land in zip(bufs[:n], bufs[n:]):
        out += [(s_.at[2 * chip[0] + chip[1]], land.at[j], (*chip, c)) for j, chip in enumerate(chips)]
    return out


def _join_copies(bufs):
    x, y, c, _ = _place()
    out = []
    for b in bufs:
        half = b.shape[0] // 2
        mine = b.at[pl.ds(c * half, half), :]
        out.append((mine, mine, (x, y, 1 - c)))
    return out


def _gather_small(vec):
    m_per, n = vec.shape

    def body(x_ref, out_ref, send_sems, recv_sems, local_sem):
        x, y, c, chips = _place()
        me, sibling = (x, y, c), (x, y, 1 - c)

        def rows(px, py, pc):
            return out_ref.at[pl.ds((4 * px + 2 * py + pc) * m_per, m_per), :]

        def copy(k, block, to, src=None):
            return _remote(rows(*block) if src is None else src, rows(*block), send_sems.at[k], recv_sems.at[k], to)

        mine = pltpu.make_async_copy(x_ref, rows(*me), local_sem)
        mine.start()
        first = [copy(0, me, sibling, src=x_ref)]
        first += [copy(1 + j, me, (*chip, c), src=x_ref) for j, chip in enumerate(chips)]
        for cp in first:
            cp.start()
        passed = [copy(4 + j, (*chip, c), sibling) for j, chip in enumerate(chips)]
        for j, chip in enumerate(chips):
            copy(1 + j, (*chip, c), me).wait_recv()
            passed[j].start()
        copy(0, sibling, me).wait_recv()
        for j, chip in enumerate(chips):
            copy(4 + j, (*chip, 1 - c), me).wait_recv()
        for cp in first + passed:
            cp.wait_send()
        mine.wait()

    return ORDER.call(
        body, [vec], [pl.BlockSpec(memory_space=pltpu.VMEM)], name="gather_small_grads",
        out_shape=_sds((8 * m_per, n), vec.dtype), out_specs=pl.BlockSpec(memory_space=pltpu.VMEM),
        scratch_shapes=[pltpu.SemaphoreType.DMA((7,)), pltpu.SemaphoreType.DMA((7,)), pltpu.SemaphoreType.DMA],
    )


def _add_sibling(name, partial, received, c):
    _, R, C = partial.shape
    half = R // 2
    tr = _tile(half, 256)
    nb = half // tr

    def body(c_ref, p_ref, r_ref, o_ref):
        o_ref[...] = (p_ref[...].astype(F32) + r_ref[...].astype(F32)).astype(BF)

    return ORDER.call(
        body, [partial, received],
        [pl.BlockSpec((None, tr, C), lambda j, i, cr: (j, cr[0] * nb + i, 0)),
         pl.BlockSpec((None, tr, C), lambda j, i, cr: (j, i, 0))],
        prefetch=(c,), name=name, grid=(N_CHIPS, nb),
        out_specs=pl.BlockSpec((None, tr, C), lambda j, i, cr: (j, i, 0)),
        out_shape=_sds((N_CHIPS, half, C), BF), compiler_params=_cparams(("parallel", "parallel")),
    )


def _add_chips(name, sums, received, me_c):
    _, half, C = sums.shape
    tr = _tile(half, 256)
    nb = half // tr

    def body(mc_ref, s_ref, r_ref, o_ref):
        acc = s_ref[...].astype(F32)
        for j in range(3):
            acc = acc + r_ref[j].astype(F32)
        o_ref[...] = acc

    return ORDER.call(
        body, [sums, received],
        [pl.BlockSpec((None, tr, C), lambda i, mc: (mc[0], i, 0)),
         pl.BlockSpec((3, tr, C), lambda i, mc: (0, i, 0))],
        prefetch=(me_c,), name=name, grid=(nb,),
        out_specs=pl.BlockSpec((tr, C), lambda i, mc: (mc[1] * nb + i, 0)),
        out_shape=_sds((2 * half, C), F32), compiler_params=_cparams(("parallel",)),
    )


def _adamw_math(w, g, m, v):
    m = ADAM_B1 * m + (1.0 - ADAM_B1) * g
    v = ADAM_B2 * v + (1.0 - ADAM_B2) * (g * g)
    m_hat = m / (1.0 - ADAM_B1 ** ADAM_STEP)
    v_hat = v / (1.0 - ADAM_B2 ** ADAM_STEP)
    delta = -ADAM_LR * (m_hat / (jnp.sqrt(v_hat) + ADAM_EPS) + ADAM_WD * w)
    return delta, m, v


def _adamw(name, w, g, m, v):
    R, C = w.shape
    tr = _tile(R, 256)

    def body(w_ref, g_ref, m_ref, v_ref, go_ref, d_ref, mo_ref, vo_ref):
        gv = g_ref[...]
        go_ref[...] = gv
        d_ref[...], mo_ref[...], vo_ref[...] = _adamw_math(w_ref[...], gv, m_ref[...], v_ref[...])

    row = pl.BlockSpec((tr, C), lambda i: (i, 0))
    return ORDER.call(
        body, [w, g, m, v], [row] * 4, name=name, grid=(R // tr,), out_specs=[row] * 4,
        out_shape=[_sds((R, C), F32)] * 4, compiler_params=_cparams(("parallel",)), chain_output=1,
    )


def _adamw_small(gathered, w, m, v):
    rows, n = w.shape

    def body(ga_ref, w_ref, m_ref, v_ref, go_ref, d_ref, mo_ref, vo_ref):
        g = ga_ref[pl.ds(0, rows), :]
        for dev in range(1, 8):
            g = g + ga_ref[pl.ds(dev * rows, rows), :]
        go_ref[...] = g
        d_ref[...], mo_ref[...], vo_ref[...] = _adamw_math(w_ref[...], g, m_ref[...], v_ref[...])

    whole = pl.BlockSpec(memory_space=pltpu.VMEM)
    return ORDER.call(
        body, [gathered, w, m, v], [whole] * 4, name="adamw_small", out_specs=[whole] * 4,
        out_shape=[_sds((rows, n), F32)] * 4, compiler_params=_cparams(), chain_output=1,
    )


def _proj_merge(y_a, y_b, gpa, gpb, g3):
    S, K = y_a.shape
    _, _, Nq = gpa.shape
    D = N_CHIPS * Nq
    tm, tn = _tile(S, 1024), _tile(Nq, 512)
    q = Nq // tn

    def body(ya_ref, yb_ref, wa_ref, wb_ref, g_ref, merged_ref, c_ref):
        pa = jnp.dot(ya_ref[...], wa_ref[...], preferred_element_type=F32)
        pb = jnp.dot(yb_ref[...], wb_ref[...], preferred_element_type=F32)
        g = g_ref[...].astype(F32)
        merged_ref[...] = (g[0] * pa + g[1] * pb).astype(BF)
        c_ref[0] = (pa * g[0] * (1.0 - g[0])).astype(BF)
        c_ref[1] = (pb * g[1] * (1.0 - g[1])).astype(BF)

    rows = pl.BlockSpec((tm, K), lambda i, j: (i, 0))
    weight = pl.BlockSpec((None, K, tn), lambda i, j: (j // q, 0, j % q))
    pair = pl.BlockSpec((2, tm, tn), lambda i, j: (0, i, j))
    return ORDER.call(
        body, [y_a, y_b, gpa, gpb, g3], [rows, rows, weight, weight, pair], name="proj_merge",
        grid=(S // tm, N_CHIPS * q), out_specs=[pl.BlockSpec((tm, tn), lambda i, j: (i, j)), pair],
        out_shape=[_sds((S, D), BF), _sds((2, S, D), BF)], compiler_params=_cparams(("parallel", "parallel")))


class _Exchange:
    GATHER = (("qkv",), ("gate",), ("proj_a", "proj_b", "out"), ("up",), ("down",))
    REDUCE = {"mlp": ("down", "up"), "mix": ("out", "proj_a", "proj_b"), "in": ("qkv", "gate")}

    OWN_FIRST = ("qkv", "gate")

    def __init__(self, shards, me, c, moments):
        self.me, self.c = me, c
        self.shards, self.moments = shards, moments
        self.hop1, self.hop2, self.stage, self.grads, self.own, self.updates = {}, {}, {}, {}, {}, {}
        for g, names in enumerate(self.GATHER):
            bufs = []
            for n in names:
                placed = _place_shard(f"place_{n}", shards[n], me, plain=n in self.OWN_FIRST)
                bufs.append(placed[0])
                if n in self.OWN_FIRST:
                    self.own[n] = placed[1]
            self.hop1[g] = _copy_start(f"gather{g}_start", bufs, _gather_hop1, 3 * len(names))

    def forward(self, g):
        send, recv, thru = self.hop1.pop(g)
        self.hop2[g] = _copy_start(f"gather{g}_forward", thru, _gather_hop2, len(thru) * 3,
                                   earlier=(_gather_hop1, send, recv))

    def weights(self, g):
        send, recv, thru = self.hop2.pop(g)
        return _copy_wait(f"gather{g}_wait", thru, _gather_hop2, send, recv)

    def adamw_beside(self, name):
        def update(w, g, m, v):
            return (g,) + _adamw_math(w, g, m, v)
        return update, [self.shards[name], self.grads[name], *self.moments[name]], 4

    def reduce(self, key, partials=None):
        names = self.REDUCE[key]
        n = len(names)
        if partials is not None:
            lands = [lax.empty((p.shape[0], p.shape[1] // 2, p.shape[2]), p.dtype) for p in partials]
            self.stage[key] = ("swap",) + _copy_start(f"reduce_{key}_swap", list(partials) + lands, _swap_copies, n)
            return
        kind, send, recv, thru = self.stage.pop(key)
        if kind == "swap":
            thru = _copy_wait(f"reduce_{key}_swap_wait", thru, _swap_copies, send, recv)
            sums = [_add_sibling(f"reduce_{nm}_add_sibling", p, r, self.c)
                    for nm, p, r in zip(names, thru[:n], thru[n:])]
            lands = [lax.empty((3,) + s_.shape[1:], s_.dtype) for s_ in sums]
            self.stage[key] = ("scatter",) + _copy_start(f"reduce_{key}_scatter", sums + lands, _scatter_copies, 3 * n)
        elif kind == "scatter":
            thru = _copy_wait(f"reduce_{key}_scatter_wait", thru, _scatter_copies, send, recv)
            me_c = jnp.concatenate([self.me, self.c])
            halves = [_add_chips(f"reduce_{nm}_add_chips", s_, r, me_c)
                      for nm, s_, r in zip(names, thru[:n], thru[n:])]
            self.stage[key] = ("join",) + _copy_start(f"reduce_{key}_join", halves, _join_copies, n)
        else:
            thru = _copy_wait(f"reduce_{key}_join_wait", thru, _join_copies, send, recv)
            self.grads.update(zip(names, thru))


def _forward_backward(x, target, norm_mix, b_gate, rpb, norm_mlp, norm_final, ex):
    S, D = x.shape

    h1 = _rms_fwd("rms_mix", x, norm_mix)
    nq = QKV_W // 512
    qkv_out = (((3, S, QKV_W), BF), lambda i, T: (T // nq, i, T % nq))
    tg = _tile(ex.own["gate"].shape[1], 1024)
    ng = D // tg
    gate_out = (((2, S, D), BF), lambda i, T: (T // ng, i, T % ng))

    def gate_epilogue(acc, ex_, outs):
        outs[0][...] = jax.nn.sigmoid(acc + ex_[0][...]).astype(BF)

    qkv3 = _mm_nn_shards("qkv_own", h1, ex.own["qkv"], ex.me, True, *qkv_out, _store(BF))
    g3 = _mm_nn_shards("gate_own", h1, ex.own["gate"], ex.me, True, *gate_out, gate_epilogue, extras=[b_gate], tn=tg)
    ex.forward(0)
    e2 = _rpb_to_table(rpb)
    (gq,) = ex.weights(0)
    qkv3 = _mm_nn_shards("qkv", h1, gq, ex.me, False, *qkv_out, _store(BF), into=qkv3)

    ex.forward(1)
    outs_a = [_attn_a_fwd(qkv3, 0, DILATIONS[0])]
    (gg,) = ex.weights(1)
    g3 = _mm_nn_shards("gate", h1, gg, ex.me, False, *gate_out, gate_epilogue, extras=[b_gate], into=g3, tn=tg)

    ex.forward(2)
    qkv_views = _qkv_views("qkv_views", qkv3)
    outs_a += [_attn_a_fwd(qkv_views[d], grp, d) for grp, d in enumerate(DILATIONS) if grp > 0]
    y_a, lj = _attn_a_combine([o for o, _ in outs_a], [l for _, l in outs_a])
    y_b, lse_b = _attn_b_fwd(qkv3, e2)
    gpa, gpb, gout = ex.weights(2)
    wout = gout.reshape(D, D)
    merged, c3 = _proj_merge(y_a, y_b, gpa, gpb, g3)

    def residual_epilogue(acc, ex_, outs):
        outs[0][...] = acc + ex_[0][...]

    def nn_plain(name, a, w, res, bm=1024, bn=1024):
        M, K = a.shape
        N = w.shape[1]
        bm, bn, bk = _tile(M, bm), _tile(N, bn), _tile(K, 2048)
        t = pl.BlockSpec((bm, bn), lambda i, j, k: (i, j))
        return _matmul(name, a, w, pl.BlockSpec((bm, bk), lambda i, j, k: (i, k)),
                       pl.BlockSpec((bk, bn), lambda i, j, k: (k, j)), NN, (M // bm, N // bn, K // bk), (bm, bn),
                       [(res, t)], [(_sds((M, N), F32), t)], residual_epilogue)[0]

    ex.forward(3)
    x1 = nn_plain("out_proj", merged, wout, x, bm=512, bn=2048)
    h2 = _rms_fwd("rms_mlp", x1, norm_mlp)
    (gup,) = ex.weights(3)
    F = gup.shape[2] * N_CHIPS

    def up_epilogue(acc, ex_, outs):
        ru = jnp.maximum(acc, 0.0)
        outs[0][...] = (ru * ru).astype(BF)
        outs[1][...] = ru.astype(BF)

    tu = _tile(gup.shape[2], 2048)
    ut = pl.BlockSpec((_tile(S, 1024), tu), lambda i, j, k: (i, j))
    (act, ru), _ = _mm_nn_cols("mlp_up", h2, gup, BF, epilogue=up_epilogue, tn=tu,
                               outs=[(_sds((S, F), BF), ut), (_sds((S, F), BF), ut)])
    ex.forward(4)
    (gdown,) = ex.weights(4)
    wdown = gdown.reshape(F, D)
    x2 = nn_plain("mlp_down", act, wdown, x1)

    loss, dx2, dx2b, d_norm_final = _loss_head(x2, target, norm_final.reshape(1, D))

    def nt_rows(name, a, w, epilogue, extras, outs, bn=1024):
        M, N = a.shape
        K = w.shape[0]
        bm, bn, bk = _tile(M, 1024), _tile(K, bn), _tile(N, 2048)
        return _matmul(name, a, w, pl.BlockSpec((bm, bk), lambda i, j, k: (i, k)),
                       pl.BlockSpec((bn, bk), lambda i, j, k: (j, k)), NT, (M // bm, K // bn, N // bk), (bm, bn),
                       extras(bm, bn), outs(bm, bn), epilogue)

    def nt_cols(name, a_spec_fn, a, g, M, epilogue, extras, outs, bk, bn=1024, side=None):
        _, K, Nq = g.shape
        bm, bn, bk = _tile(M, 1024), _tile(K, bn), _tile(Nq, bk)
        q = Nq // bk
        return _matmul(name, a, g, a_spec_fn(bm, bk), pl.BlockSpec((None, bn, bk), lambda i, j, k: (k // q, j, k % q)),
                       NT, (M // bm, K // bn, N_CHIPS * q), (bm, bn), extras(bm, bn), outs(bm, bn), epilogue,
                       side=side)

    def tn_grad(name, a, a_spec_fn, b, b_spec_fn, Kin, N, out_shape, out_spec_fn, bn=1024):
        bm, bn, bk = _tile(Kin, 1024), _tile(N, bn), _tile(S, 4096)
        return _matmul(name, a, b, a_spec_fn(bk, bm), b_spec_fn(bk, bn), TN, (Kin // bm, N // bn, S // bk), (bm, bn),
                       [], [(_sds(out_shape, BF), out_spec_fn(bm, bn))], _store(BF))[0]

    plain_a = lambda bk, bm: pl.BlockSpec((bk, bm), lambda i, j, k: (k, i))
    plain_b = lambda bk, bn: pl.BlockSpec((bk, bn), lambda i, j, k: (k, j))
    plain_o = lambda bm, bn: pl.BlockSpec((bm, bn), lambda i, j, k: (i, j))
    a_rows = lambda bm, bk: pl.BlockSpec((bm, bk), lambda i, j, k: (i, k))

    def cols_o(Nq):
        def spec(bm, bn):
            q = Nq // bn
            return pl.BlockSpec((None, bm, bn), lambda i, j, k: (j // q, i, j % q))
        return spec

    def du_epilogue(acc, ex_, outs):
        outs[0][...] = (acc * (2.0 * ex_[0][...].astype(F32))).astype(BF)

    dw_down = tn_grad("mlp_down_dw", act, plain_a, dx2b, plain_b, F, D, (F, D), plain_o)
    (du,) = nt_rows("mlp_down_dx", dx2b, wdown, du_epilogue,
                    lambda bm, bn: [(ru, plain_o(bm, bn))], lambda bm, bn: [(_sds((S, F), BF), plain_o(bm, bn))],
                    bn=2048)

    fq = gup.shape[2]
    dw_up = tn_grad("mlp_up_dw", h2, plain_a, du, plain_b, D, F, (N_CHIPS, D, fq), cols_o(fq), bn=min(fq, 1024))
    ex.reduce("mlp", partials=[dw_down.reshape(N_CHIPS, F // N_CHIPS, D), dw_up])
    (dh2,) = nt_cols("mlp_up_dx", a_rows, du, gup, S, _store(F32), lambda bm, bn: [],
                     lambda bm, bn: [(_sds((S, D), F32), plain_o(bm, bn))], 1024, bn=2048)
    ex.reduce("mlp")
    dx1, dx1b, d_norm_mlp = _rms_bwd("rms_mlp_bwd", dh2, x1, norm_mlp, dx2)

    def merge_bwd_epilogue(acc, ex_, outs):
        g, c = ex_[0][...].astype(F32), ex_[1][...].astype(F32)
        outs[0][...] = (acc * g[0]).astype(BF)
        outs[1][...] = (acc * g[1]).astype(BF)
        dga = acc * c[0]
        dgb = acc * c[1]
        outs[2][0] = dga.astype(BF)
        outs[2][1] = dgb.astype(BF)
        outs[3][...] = jnp.concatenate([jnp.sum(dga, axis=0, keepdims=True), jnp.sum(dgb, axis=0, keepdims=True)], 0)

    def pair(bm, bn):
        return pl.BlockSpec((2, bm, bn), lambda i, j, k: (0, i, j))

    n_row_blocks = S // _tile(S, 1024)
    dpa, dpb, dg3, db_gate = nt_rows(
        "out_proj_dx", dx1b, wout, merge_bwd_epilogue,
        lambda bm, bn: [(g3, pair(bm, bn)), (c3, pair(bm, bn))],
        lambda bm, bn: [(_sds((S, D), BF), plain_o(bm, bn)), (_sds((S, D), BF), plain_o(bm, bn)),
                        (_sds((2, S, D), BF), pair(bm, bn)),
                        (_sds((n_row_blocks, 2, D), F32), pl.BlockSpec((None, 2, bn), lambda i, j, k: (i, 0, j)))],
        bn=512)
    dw_out = tn_grad("out_proj_dw", merged, plain_a, dx1b, plain_b, D, D, (D, D), plain_o)

    pq = gpa.shape[2]
    proj_dx = lambda name, dproj, g: nt_cols(name, a_rows, dproj, g, S, _store(BF), lambda bm, bn: [],
                                             lambda bm, bn: [(_sds((S, 512), BF), plain_o(bm, bn))], 512)[0]
    dw_pa = tn_grad("proj_a_dw", y_a, plain_a, dpa, plain_b, 512, D, (N_CHIPS, 512, pq), cols_o(pq), bn=min(pq, 512))
    dw_pb = tn_grad("proj_b_dw", y_b, plain_a, dpb, plain_b, 512, D, (N_CHIPS, 512, pq), cols_o(pq), bn=min(pq, 512))
    ex.reduce("mix", partials=[dw_out.reshape(N_CHIPS, D // N_CHIPS, D), dw_pa, dw_pb])
    dy_a = proj_dx("proj_a_dx", dpa, gpa)
    dy_b = proj_dx("proj_b_dx", dpb, gpb)

    dqkv3 = lax.empty((3, S, QKV_W), BF)
    dqkv3 = _attn_a_bwd(qkv3, dy_a, y_a, lj, dqkv3, 0, DILATIONS[0])
    ex.reduce("mix")
    dy_views, y_views, lj_views = _dilated_rows("attn_a_bwd_rows", [dy_a, y_a, lj])
    dqkv_views = {d: _attn_a_bwd(qkv_views[d], dy_views[d], y_views[d], lj_views[d], None, grp, d)
                  for grp, d in enumerate(DILATIONS) if grp > 0}
    dqkv3 = _qkv_views("dqkv_from_views", dqkv3, dqkv_views)
    dqkv3, de2 = _attn_b_bwd(qkv3, e2, dy_b, y_b, lse_b, dqkv3)
    d_rpb = _table_grad_to_rpb(de2)

    def stacked_a(width):
        def spec(bm, bk):
            q = width // bk
            return pl.BlockSpec((None, bm, bk), lambda i, j, k: (k // q, i, k % q))
        return spec

    def stacked_b(width):
        def spec(bk, bn):
            q = width // bn
            return pl.BlockSpec((None, bk, bn), lambda i, j, k: (j // q, k, j % q))
        return spec

    ex.reduce("mlp")
    dw_qkv = tn_grad("qkv_dw", h1, plain_a, dqkv3, stacked_b(QKV_W), D, 3 * QKV_W, (N_CHIPS,) + gq.shape[1:],
                     cols_o(gq.shape[2]), bn=512)
    dw_gate = tn_grad("gate_dw", h1, plain_a, dg3, stacked_b(D), D, 2 * D, (N_CHIPS,) + gg.shape[1:],
                      cols_o(gg.shape[2]), bn=gg.shape[2])
    ex.reduce("in", partials=[dw_qkv, dw_gate])
    ex.reduce("mlp")
    dh1_q, *ex.updates["down"] = nt_cols(
        "qkv_dx", stacked_a(QKV_W), dqkv3, gq, S, _store(F32), lambda bm, bn: [],
        lambda bm, bn: [(_sds((S, D), F32), plain_o(bm, bn))], 512, bn=2048, side=ex.adamw_beside("down"))
    ex.reduce("in")
    ex.reduce("mix")

    def add_epilogue(acc, ex_, outs):
        outs[0][...] = acc + ex_[0][...]

    dh1, *ex.updates["up"] = nt_cols(
        "gate_dx", stacked_a(D), dg3, gg, S, add_epilogue, lambda bm, bn: [(dh1_q, plain_o(bm, bn))],
        lambda bm, bn: [(_sds((S, D), F32), plain_o(bm, bn))], gg.shape[2], side=ex.adamw_beside("up"))
    grad_x, _, d_norm_mix = _rms_bwd("rms_mix_bwd", dh1, x, norm_mix, dx1)
    ex.reduce("mix")

    small = [d_norm_mix, jnp.sum(db_gate, axis=0).reshape(1, 2 * D), d_rpb, d_norm_mlp, d_norm_final]
    return loss, grad_x, small


def _pack_small(parts, width):
    flat = jnp.concatenate([p.reshape(-1) for p in parts])
    return jnp.pad(flat, (0, 8 * width - flat.shape[0])).reshape(8, width)


def kernel(x, norm_mix, w_qkv, w_gate, b_gate, rpb, w_proj_a, w_proj_b, w_out, norm_mlp, w_up, w_down, norm_final, loss_target, m_norm_mix, m_w_qkv, m_w_gate, m_b_gate, m_rpb, m_w_proj_a, m_w_proj_b, m_w_out, m_norm_mlp, m_w_up, m_w_down, m_norm_final, v_norm_mix, v_w_qkv, v_w_gate, v_b_gate, v_rpb, v_w_proj_a, v_w_proj_b, v_w_out, v_norm_mlp, v_w_up, v_w_down, v_norm_final):
    names = ["qkv", "gate", "proj_a", "proj_b", "out", "up", "down"]
    big = dict(zip(names, [w_qkv[0], w_gate[0], w_proj_a[0], w_proj_b[0], w_out[0], w_up[0], w_down[0]]))
    big_m = dict(zip(names, [m_w_qkv[0], m_w_gate[0], m_w_proj_a[0], m_w_proj_b[0], m_w_out[0], m_w_up[0], m_w_down[0]]))
    big_v = dict(zip(names, [v_w_qkv[0], v_w_gate[0], v_w_proj_a[0], v_w_proj_b[0], v_w_out[0], v_w_up[0], v_w_down[0]]))

    c = lax.axis_index("c").astype(jnp.int32).reshape(1)
    me = (2 * lax.axis_index("x") + lax.axis_index("y")).astype(jnp.int32).reshape(1)
    ORDER.last = None
    ex = _Exchange(big, me, c, {n: (big_m[n], big_v[n]) for n in names})
    loss, grad_x, small = _forward_backward(x[0], loss_target[0], norm_mix, b_gate, rpb[0], norm_mlp, norm_final, ex)

    def adamw(group):
        return {n: ex.updates[n] if ex.updates.get(n) else _adamw(f"adamw_{n}", big[n], ex.grads[n], big_m[n], big_v[n])
                for n in _Exchange.REDUCE[group]}

    big_out = {**adamw("mlp"), **adamw("mix")}
    ex.reduce("in")

    small_w = [norm_mix, b_gate, rpb, norm_mlp, norm_final]
    count = sum(int(np.prod(p.shape)) for p in small_w)
    width = -(-count // (8 * 128)) * 128
    packed = _adamw_small(_gather_small(_pack_small(small, width)), _pack_small(small_w, width),
                          _pack_small([m_norm_mix, m_b_gate, m_rpb, m_norm_mlp, m_norm_final], width),
                          _pack_small([v_norm_mix, v_b_gate, v_rpb, v_norm_mlp, v_norm_final], width))
    ex.reduce("in")
    big_out.update(adamw("in"))

    def unpack(flat2d):
        flat, out, at = flat2d.reshape(-1), [], 0
        for p in small_w:
            size = int(np.prod(p.shape))
            out.append(flat[at:at + size].reshape(p.shape))
            at += size
        return out

    small_out = [unpack(a) for a in packed]

    def ordered(kind):
        sm = small_out[kind]
        bg = {n: o[kind][None] for n, o in big_out.items()}
        return [sm[0], bg["qkv"], bg["gate"], sm[1], sm[2], bg["proj_a"], bg["proj_b"], bg["out"], sm[3],
                bg["up"], bg["down"], sm[4]]

    total = lax.psum(loss[0, 0], ("x", "y", "c"))
    return (total, grad_x[None], *ordered(0), *ordered(1), *ordered(2), *ordered(3))
```

```python
import math

import numpy as np
import jax
import jax.numpy as jnp
from jax import lax
from jax.experimental import pallas as pl
from jax.experimental.pallas import tpu as pltpu

BF = jnp.bfloat16
F32 = jnp.float32
MESH = pl.DeviceIdType.MESH

HEAD_DIM = 128
N_HEADS = 16
N_HEADS_A = 12
QKV_W = N_HEADS * HEAD_DIM
DILATIONS = (1, 4, 16)
HALF_WINDOW = 64
GRID_W = 64
NA_ROWS = 8
NA_COLS = 16
RPB_ROWS = 2 * NA_ROWS - 1
RPB_COLS = 2 * NA_COLS - 1
EPS = 1e-6
NEG = -1e30
SCALE = HEAD_DIM ** -0.5

ADAM_LR = 0.001
ADAM_B1 = 0.9
ADAM_B2 = 0.999
ADAM_EPS = 1e-08
ADAM_WD = 0.01
ADAM_STEP = 10

N_CHIPS = 4
VMEM_LIMIT_BYTES = 48 * 1024 * 1024
QB = 256
NBR_SIDE = 16
ROW_TILE = 512


def _key_rows(L):
    return min(QB + 2 * HALF_WINDOW, L)


def _cparams(sem=None):
    return pltpu.CompilerParams(dimension_semantics=sem, vmem_limit_bytes=VMEM_LIMIT_BYTES)


def _tile(dim, want):
    t = min(dim, want)
    assert dim % t == 0, (dim, want)
    return t


class _ProgramOrder:
    def __init__(self):
        self.last = None

    def call(self, body, operands, in_specs, *, prefetch=(), grid=None, out_specs=None, chain_output=0, **kwargs):
        operands, in_specs = list(operands), list(in_specs)
        lead = len(prefetch) + len(operands)
        if self.last is not None and not any(op is self.last for op in operands):
            operands.append(self.last)
            in_specs.append(pl.BlockSpec(memory_space=pl.ANY))
            inner = body

            def body(*refs):
                return inner(*refs[:lead], *refs[lead + 1:])

        if prefetch:
            kwargs["grid_spec"] = pltpu.PrefetchScalarGridSpec(
                num_scalar_prefetch=len(prefetch), grid=grid, in_specs=in_specs, out_specs=out_specs)
        else:
            kwargs.update(in_specs=in_specs, out_specs=out_specs)
            if grid is not None:
                kwargs["grid"] = grid
        out = pl.pallas_call(body, **kwargs)(*prefetch, *operands)
        self.last = out[chain_output] if isinstance(out, (tuple, list)) else out
        return out


ORDER = _ProgramOrder()


NN = ((1,), (0,))
NT = ((1,), (1,))
TN = ((0,), (0,))


def _matmul(name, a, b, a_spec, b_spec, dims, grid, acc_shape, extras, outs, epilogue, precision=None,
            prefetch=(), into=None, side=None):
    n_ex, n_out, nk = len(extras), len(outs), grid[2]
    side_fn, side_in, n_side_out = side if side is not None else (None, [], 0)
    side_spec = None
    n_in = 2 + n_ex + len(side_in) + (into is not None)
    if side is not None:
        R, C = side_in[0].shape
        steps = grid[0] * grid[1] * grid[2]
        side_blocks = max(n for n in range(1, steps + 1) if R % n == 0 and (R // n) % 8 == 0)

        def side_step(*ids):
            return (ids[0] * grid[1] + ids[1]) * grid[2] + ids[2]

        side_spec = pl.BlockSpec((R // side_blocks, C),
                                 lambda *ids: (jnp.minimum(side_step(*ids), side_blocks - 1), 0))

    def body(*refs):
        refs = refs[len(prefetch):]
        a_ref, b_ref = refs[0], refs[1]
        ex_refs = refs[2:2 + n_ex]
        out_refs = refs[n_in:n_in + n_out]
        if side is not None:
            @pl.when(side_step(pl.program_id(0), pl.program_id(1), pl.program_id(2)) < side_blocks)
            def _():
                results = side_fn(*[r[...] for r in refs[2 + n_ex:2 + n_ex + len(side_in)]])
                for o_ref, value in zip(refs[n_in + n_out:n_in + n_out + n_side_out], results):
                    o_ref[...] = value

        def dot():
            return lax.dot_general(a_ref[...], b_ref[...], (dims, ((), ())),
                                   preferred_element_type=F32, precision=precision)

        if nk == 1:
            epilogue(dot(), ex_refs, out_refs)
            return
        acc_ref = refs[-1]
        k = pl.program_id(2)

        @pl.when(k == 0)
        def _():
            acc_ref[...] = dot()

        if nk > 2:
            @pl.when((k > 0) & (k < nk - 1))
            def _():
                acc_ref[...] += dot()

        @pl.when(k == nk - 1)
        def _():
            epilogue(acc_ref[...] + dot(), ex_refs, out_refs)

    operands = [a, b] + [e for e, _ in extras] + list(side_in)
    in_specs = [a_spec, b_spec] + [s for _, s in extras] + [side_spec] * len(side_in)
    kwargs = {}
    if into is not None:
        operands.append(into)
        in_specs.append(pl.BlockSpec(memory_space=pl.ANY))
        kwargs["input_output_aliases"] = {len(prefetch) + n_in - 1: 0}
    return ORDER.call(
        body, operands, in_specs, prefetch=prefetch, name=name, grid=grid,
        out_specs=[s for _, s in outs] + [side_spec] * n_side_out,
        out_shape=[sh for sh, _ in outs] + [_sds(s_.shape, F32) for s_ in side_in[:1]] * n_side_out,
        scratch_shapes=[pltpu.VMEM(acc_shape, F32)] if nk > 1 else [],
        compiler_params=_cparams(("parallel", "parallel", "arbitrary")), **kwargs,
    )


def _mm_nn_shards(name, a, w, me, own, out, out_block, epilogue, extras=(), into=None, tn=512):
    M, K = a.shape
    Nq = w.shape[-1]
    tm, tn = _tile(M, 1024), _tile(Nq, tn)
    q = Nq // tn

    def tile(j, me_ref):
        shard = me_ref[0] if own else (me_ref[0] + 1 + j // q) % N_CHIPS
        return shard, j % q, shard * q + j % q

    if own:
        b_spec = pl.BlockSpec((K, tn), lambda i, j, k, me_ref: (0, j))
    else:
        b_spec = pl.BlockSpec((None, K, tn), lambda i, j, k, me_ref: (tile(j, me_ref)[0], 0, tile(j, me_ref)[1]))
    shape, dtype = out
    out_spec = pl.BlockSpec((None, tm, tn), lambda i, j, k, me_ref: out_block(i, tile(j, me_ref)[2]))
    ex = [(e, pl.BlockSpec((1, tn), lambda i, j, k, me_ref: (0, tile(j, me_ref)[2]))) for e in extras]
    return _matmul(name, a, w, pl.BlockSpec((tm, K), lambda i, j, k, me_ref: (i, 0)), b_spec, NN,
                   (M // tm, q if own else (N_CHIPS - 1) * q, 1), (tm, tn), ex, [(_sds(shape, dtype), out_spec)],
                   epilogue, prefetch=(me,), into=into)[0]


def _store(dtype):
    def epilogue(acc, ex, outs):
        outs[0][...] = acc.astype(dtype)
    return epilogue


def _sds(shape, dtype):
    return jax.ShapeDtypeStruct(shape, dtype)


def _mm_nn_cols(name, a, g, out_dtype, epilogue=None, extras=(), outs=None, tm=1024, tn=1024, tk=2048):
    M, K = a.shape
    _, _, Nq = g.shape
    tm, tn, tk = _tile(M, tm), _tile(Nq, tn), _tile(K, tk)
    q = Nq // tn
    grid = (M // tm, N_CHIPS * q, K // tk)
    if outs is None:
        outs = [(_sds((M, N_CHIPS * Nq), out_dtype), pl.BlockSpec((tm, tn), lambda i, j, k: (i, j)))]
    return _matmul(name, a, g, pl.BlockSpec((tm, tk), lambda i, j, k: (i, k)),
                   pl.BlockSpec((None, tk, tn), lambda i, j, k: (j // q, k, j % q)), NN, grid, (tm, tn),
                   list(extras), outs, epilogue or _store(out_dtype)), (tm, tn, tk)


def _rms_fwd(name, x, g):
    S, D = x.shape
    tm = _tile(S, ROW_TILE)

    def body(x_ref, g_ref, h_ref):
        xv = x_ref[...]
        r = lax.rsqrt(jnp.mean(xv * xv, axis=-1, keepdims=True) + EPS)
        h_ref[...] = ((xv * r) * g_ref[...]).astype(BF)

    row = pl.BlockSpec((tm, D), lambda i: (i, 0))
    return ORDER.call(
        body, [x, g], [row, pl.BlockSpec((1, D), lambda i: (0, 0))], name=name, grid=(S // tm,),
        out_specs=row, out_shape=_sds((S, D), BF), compiler_params=_cparams(("parallel",)),
    )


def _rms_bwd(name, dh, x, g, dres):
    S, D = x.shape
    tm = _tile(S, ROW_TILE // 2)

    def body(dh_ref, x_ref, g_ref, dres_ref, dx_ref, dxb_ref, dg_ref):
        xv = x_ref[...]
        r = lax.rsqrt(jnp.mean(xv * xv, axis=-1, keepdims=True) + EPS)
        n = xv * r
        dhv = dh_ref[...]
        dyg = dhv * g_ref[...]
        dx = dres_ref[...] + r * (dyg - n * jnp.mean(dyg * n, axis=-1, keepdims=True))
        dx_ref[...] = dx
        dxb_ref[...] = dx.astype(BF)

        @pl.when(pl.program_id(0) == 0)
        def _():
            dg_ref[...] = jnp.zeros_like(dg_ref)

        dg_ref[...] += jnp.sum(dhv * n, axis=0, keepdims=True)

    row = pl.BlockSpec((tm, D), lambda i: (i, 0))
    vec = pl.BlockSpec((1, D), lambda i: (0, 0))
    return ORDER.call(
        body, [dh, x, g, dres], [row, row, vec, row], name=name, grid=(S // tm,),
        out_specs=[row, row, vec],
        out_shape=[_sds((S, D), F32), _sds((S, D), BF), _sds((1, D), F32)],
        compiler_params=_cparams(("arbitrary",)),
    )


def _loss_head(x2, target, g):
    S, D = x2.shape
    tm = _tile(S, ROW_TILE)

    def body(x_ref, t_ref, g_ref, loss_ref, dx_ref, dxb_ref, dg_ref):
        xv = x_ref[...]
        gv = g_ref[...]
        r = lax.rsqrt(jnp.mean(xv * xv, axis=-1, keepdims=True) + EPS)
        n = xv * r
        e = n * gv - t_ref[...]
        dy = e * (1.0 / D)
        dyg = dy * gv
        dx = r * (dyg - n * jnp.mean(dyg * n, axis=-1, keepdims=True))
        dx_ref[...] = dx
        dxb_ref[...] = dx.astype(BF)

        @pl.when(pl.program_id(0) == 0)
        def _():
            dg_ref[...] = jnp.zeros_like(dg_ref)
            loss_ref[...] = jnp.zeros_like(loss_ref)

        dg_ref[...] += jnp.sum(dy * n, axis=0, keepdims=True)
        per_row = jnp.mean(e * e, axis=-1, keepdims=True)
        loss_ref[...] += 0.5 * jnp.sum(per_row, axis=0, keepdims=True)

    row = pl.BlockSpec((tm, D), lambda i: (i, 0))
    vec = pl.BlockSpec((1, D), lambda i: (0, 0))
    return ORDER.call(
        body, [x2, target, g], [row, row, vec], name="loss_head", grid=(S // tm,),
        out_specs=[pl.BlockSpec((1, 1), lambda i: (0, 0)), row, row, vec],
        out_shape=[_sds((1, 1), F32), _sds((S, D), F32), _sds((S, D), BF), _sds((1, D), F32)],
        compiler_params=_cparams(("arbitrary",)), chain_output=1,
    )


def _chains(L):
    side = min(8, L // QB)
    return side, max(1, 4 // side)


def _band_scores(qkv_ref, i, L, coef, head):
    KB = _key_rows(L)
    lanes = pl.ds(head * HEAD_DIM, HEAD_DIM)
    q0 = pl.multiple_of(i * QB, QB)
    ks = pl.multiple_of(jnp.clip(i * QB - HALF_WINDOW, 0, L - KB), HALF_WINDOW)
    q = qkv_ref[0, pl.ds(q0, QB), lanes]
    k = qkv_ref[1, pl.ds(ks, KB), lanes]
    v = qkv_ref[2, pl.ds(ks, KB), lanes]
    s = lax.dot_general(q, k, (NT, ((), ())), preferred_element_type=F32) * SCALE
    qpos = q0 + lax.broadcasted_iota(jnp.int32, (QB, KB), 0)
    kpos = ks + lax.broadcasted_iota(jnp.int32, (QB, KB), 1)
    rel = jnp.abs(kpos - qpos)
    valid = rel <= HALF_WINDOW
    s = jnp.where(valid, s - coef * rel.astype(F32), NEG)
    return q0, ks, q, k, v, s, valid


def _alibi_coefs(group, d, heads):
    first = 4 * group + 1 + pl.program_id(1) * heads
    scale = jnp.full((1, 1), -(8.0 / N_HEADS_A) * math.log(2.0), F32)
    return [jnp.exp(scale * (first + hh).astype(F32)) * float(d) for hh in range(heads)]


def _dilated_view(qkv3, group, d, heads):
    per = 4 // heads
    L = qkv3.shape[1]
    if d == 1:
        return qkv3, pl.BlockSpec((3, L, heads * HEAD_DIM), lambda r, j: (0, 0, per * group + j))
    return qkv3, pl.BlockSpec((3, L, heads * HEAD_DIM), lambda r, j: (0, 0, r * per + j))


def _qkv_views(name, qkv3, views=None):
    _, S, _ = qkv3.shape
    W = 512
    tm = _tile(S, ROW_TILE)
    dilated = [(g, d) for g, d in enumerate(DILATIONS) if d > 1]
    first = dilated[0][0]
    assert [g for g, _ in dilated] == list(range(first, first + len(dilated)))
    nc = W // 128
    to_views = views is None

    def body(*refs):
        scr = refs[-nc:]
        if to_views:
            src, outs = refs[0], refs[1:1 + len(dilated)]
        else:
            ins, dst = refs[:len(dilated)], refs[len(dilated) + 1]
        for k, (_, d) in enumerate(dilated):
            @pl.when(pl.program_id(1) == k)
            def _():
                for w in range(3):
                    for c in range(nc):
                        if to_views:
                            scr[c][...] = src[w, :, c * 128:(c + 1) * 128].astype(F32)
                    for r in range(d):
                        for c in range(nc):
                            at = r * W + c * 128
                            if to_views:
                                outs[k][w, :, at:at + 128] = scr[c][pl.ds(r, tm // d, stride=d), :].astype(BF)
                            else:
                                scr[c][pl.ds(r, tm // d, stride=d), :] = ins[k][w, :, at:at + 128].astype(F32)
                    for c in range(nc):
                        if not to_views:
                            dst[w, :, c * 128:(c + 1) * 128] = scr[c][...].astype(BF)

    cols = pl.BlockSpec((3, tm, W), lambda i, k: (0, i, first + k))
    rows = [pl.BlockSpec((3, tm // d, d * W), lambda i, k: (0, i, 0)) for _, d in dilated]
    shapes = [_sds((3, S // d, d * W), BF) for _, d in dilated]
    common = dict(name=name, grid=(S // tm, len(dilated)), scratch_shapes=[pltpu.VMEM((tm, 128), F32)] * nc,
                  compiler_params=_cparams(("parallel", "arbitrary")))
    if to_views:
        outs = ORDER.call(body, [qkv3], [cols], out_specs=rows, out_shape=shapes, **common)
        return {d: o for (_, d), o in zip(dilated, outs)}
    return ORDER.call(body, [views[d] for _, d in dilated] + [qkv3], rows + [pl.BlockSpec(memory_space=pl.ANY)],
                      out_specs=cols, out_shape=_sds(qkv3.shape, BF), input_output_aliases={len(dilated): 0}, **common)


def _attn_a_fwd(qkv3, group, d):
    L = qkv3.shape[1]
    S = L * d
    assert L % QB == 0
    side, heads = _chains(L)
    view, blocks_spec = _dilated_view(qkv3, group, d, heads)

    def body(qkv_ref, o_ref, lse_ref):
        coefs = _alibi_coefs(group, d, heads)

        def step(i, carry):
            chains = [(hh, _band_scores(qkv_ref, side * i + u, L, coefs[hh], hh))
                      for u in range(side) for hh in range(heads)]
            soft = []
            for hh, (q0, _, _, _, v, s, _) in chains:
                m = jnp.max(s, axis=-1, keepdims=True)
                p = jnp.exp(s - m)
                den = jnp.sum(p, axis=-1, keepdims=True)
                soft.append((hh, q0, (p / den).astype(BF), v, m + jnp.log(den)))
            for hh, q0, pn, v, lse in soft:
                lanes = pl.ds(hh * HEAD_DIM, HEAD_DIM)
                o_ref[pl.ds(q0, QB), lanes] = jnp.dot(pn, v, preferred_element_type=F32)
                lse_ref[pl.ds(q0, QB), lanes] = jnp.broadcast_to(lse, (QB, HEAD_DIM))
            return carry

        lax.fori_loop(0, L // QB // side, step, 0)

    per = 4 // heads
    out = pl.BlockSpec((L, heads * HEAD_DIM), lambda r, j: (0, r * per + j))
    o, lse = ORDER.call(
        body, [view], [blocks_spec],
        name=f"attn_a_fwd_d{d}", grid=(d, per),
        out_specs=[out, out],
        out_shape=[_sds((L, d * 512), F32), _sds((L, d * 512), F32)],
        compiler_params=_cparams(("parallel", "parallel")),
    )
    return o, lse


def _dilated_rows(name, arrays):
    S, W = arrays[0].shape
    tm = _tile(S, ROW_TILE)
    ds_ = [d for d in DILATIONS if d > 1]
    n = len(arrays)

    def body(*refs):
        nc = W // 128
        ins, outs, scr = refs[:n], refs[n:-nc], refs[-nc:]
        for a, src in enumerate(ins):
            for c in range(nc):
                scr[c][...] = src[:, c * 128:(c + 1) * 128].astype(F32)
            for k, d in enumerate(ds_):
                dst = outs[a * len(ds_) + k]
                for r in range(d):
                    for c in range(nc):
                        at = r * W + c * 128
                        dst[:, at:at + 128] = scr[c][pl.ds(r, tm // d, stride=d), :].astype(dst.dtype)

    row = pl.BlockSpec((tm, W), lambda i: (i, 0))
    out_specs, out_shape = [], []
    for a in arrays:
        for d in ds_:
            out_specs.append(pl.BlockSpec((tm // d, d * W), lambda i: (i, 0)))
            out_shape.append(_sds((S // d, d * W), a.dtype))
    outs = ORDER.call(body, list(arrays), [row] * n, name=name, grid=(S // tm,), out_specs=out_specs,
                      out_shape=out_shape, scratch_shapes=[pltpu.VMEM((tm, 128), F32)] * (W // 128),
                      compiler_params=_cparams(("parallel",)))
    return [{d: outs[a * len(ds_) + k] for k, d in enumerate(ds_)} for a in range(n)]


def _attn_a_combine(os_, lses):
    W = 512
    S = os_[0].shape[0] * DILATIONS[0]
    tm = _tile(S, ROW_TILE)
    nc = W // 128
    dilated = [g for g, d in enumerate(DILATIONS) if d > 1]

    def body(o0, o1, o2, l0, l1, l2, y_ref, lj_ref, *scr):
        def token_order(src, g, slot):
            d = DILATIONS[g]
            if d == 1:
                return src[...]
            bufs = scr[slot * nc:(slot + 1) * nc]
            for r in range(d):
                for c in range(nc):
                    at = r * W + c * 128
                    bufs[c][pl.ds(r, tm // d, stride=d), :] = src[:, at:at + 128]
            return jnp.concatenate([buf[...] for buf in bufs], axis=1)

        slots = {g: k for k, g in enumerate(dilated)}
        ls = [token_order(l, g, slots.get(g, 0)) for g, l in enumerate((l0, l1, l2))]
        os_tok = [token_order(o, g, len(dilated) + slots.get(g, 0)) for g, o in enumerate((o0, o1, o2))]
        m = jnp.maximum(jnp.maximum(ls[0], ls[1]), ls[2])
        es = [jnp.exp(l - m) for l in ls]
        den = es[0] + es[1] + es[2]
        y = (es[0] / den) * os_tok[0] + (es[1] / den) * os_tok[1] + (es[2] / den) * os_tok[2]
        y_ref[...] = y.astype(BF)
        lj_ref[...] = m + jnp.log(den)

    row = pl.BlockSpec((tm, W), lambda i: (i, 0))
    views = [pl.BlockSpec((tm // d, d * W), lambda i: (i, 0)) for d in DILATIONS]
    return ORDER.call(
        body, [*os_, *lses], views + views, name="attn_a_combine", grid=(S // tm,), out_specs=[row, row],
        out_shape=[_sds((S, W), BF), _sds((S, W), F32)],
        scratch_shapes=[pltpu.VMEM((tm, 128), F32)] * (2 * len(dilated) * nc),
        compiler_params=_cparams(("parallel",)),
    )


def _attn_a_bwd(qkv3, dy, y, lj, dqkv3, group, d):
    L = qkv3.shape[1]
    S = L * d
    side, heads = _chains(L)
    view, blocks_spec = _dilated_view(qkv3, group, d, heads)

    def body(qkv_ref, dy_ref, y_ref, lj_ref, *rest):
        out_ref, dk_acc, dv_acc = rest[-3:]
        coefs = _alibi_coefs(group, d, heads)
        dk_acc[...] = jnp.zeros_like(dk_acc)
        dv_acc[...] = jnp.zeros_like(dv_acc)

        def step(i, carry):
            chains = [(pl.ds(hh * HEAD_DIM, HEAD_DIM), _band_scores(qkv_ref, side * i + u, L, coefs[hh], hh))
                      for u in range(side) for hh in range(heads)]
            dys = [dy_ref[pl.ds(c[0], QB), lanes] for lanes, c in chains]
            dps = [lax.dot_general(dyv, c[4], (NT, ((), ())), preferred_element_type=F32)
                   for dyv, (_, c) in zip(dys, chains)]
            grads = []
            for (lanes, (q0, ks, q, k, v, s, valid)), dyv, dp in zip(chains, dys, dps):
                rows = pl.ds(q0, QB)
                delta = jnp.sum(dyv.astype(F32) * y_ref[rows, lanes].astype(F32), axis=-1, keepdims=True)
                p = jnp.where(valid, jnp.exp(s - jnp.tile(lj_ref[rows, lanes], (1, _key_rows(L) // HEAD_DIM))), 0.0)
                grads.append(((p * (dp - delta)).astype(BF), p.astype(BF)))
            for (lanes, (q0, ks, q, k, v, s, valid)), dyv, (ds, pb) in zip(chains, dys, grads):
                out_ref[0, pl.ds(q0, QB), lanes] = (jnp.dot(ds, k, preferred_element_type=F32) * SCALE).astype(BF)
                keys = pl.ds(ks, _key_rows(L))
                dk_acc[keys, lanes] += lax.dot_general(ds, q, (TN, ((), ())), preferred_element_type=F32) * SCALE
                dv_acc[keys, lanes] += lax.dot_general(pb, dyv, (TN, ((), ())), preferred_element_type=F32)
            return carry

        lax.fori_loop(0, L // QB // side, step, 0)
        out_ref[1] = dk_acc[...].astype(BF)
        out_ref[2] = dv_acc[...].astype(BF)

    per = 4 // heads
    width = heads * HEAD_DIM
    row = pl.BlockSpec((L, width), lambda r, j: (0, r * per + j))
    operands = [view, dy, y, lj]
    scratch = [pltpu.VMEM((L, width), F32), pltpu.VMEM((L, width), F32)]
    if d == 1:
        return ORDER.call(
            body, operands + [dqkv3], [blocks_spec, row, row, row, pl.BlockSpec(memory_space=pl.ANY)],
            name=f"attn_a_bwd_d{d}", grid=(d, per), out_specs=blocks_spec, out_shape=_sds((3, S, QKV_W), BF),
            scratch_shapes=scratch, input_output_aliases={4: 0}, compiler_params=_cparams(("parallel", "parallel")))
    return ORDER.call(
        body, operands, [blocks_spec, row, row, row], name=f"attn_a_bwd_d{d}", grid=(d, per),
        out_specs=blocks_spec, out_shape=_sds((3, L, d * 512), BF),
        scratch_shapes=scratch, compiler_params=_cparams(("parallel", "parallel")))


def _toeplitz_onehot():
    oh = np.zeros((64, GRID_W, 128), np.float32)
    for qc in range(GRID_W):
        for m in range(128):
            kc = m % GRID_W
            dc = int(np.clip(kc - qc, -(NA_COLS - 1), NA_COLS - 1)) + NA_COLS - 1
            oh[(m // GRID_W) * 32 + dc, qc, m] = 1.0
    return oh.reshape(64, GRID_W * 128)


def _nbr_scores(qkv_ref, e2_ref, r, rows, ok):
    rs = jnp.clip(r - NA_ROWS // 2, 0, rows - NA_ROWS)
    q0 = pl.multiple_of(r * GRID_W, GRID_W)
    k0 = pl.multiple_of(rs * GRID_W, GRID_W)
    q = qkv_ref[0, pl.ds(q0, GRID_W), :]
    k = qkv_ref[1, pl.ds(k0, NA_ROWS * GRID_W), :]
    v = qkv_ref[2, pl.ds(k0, NA_ROWS * GRID_W), :]
    s = lax.dot_general(q, k, (NT, ((), ())), preferred_element_type=F32) * SCALE
    first = rs - r + NA_ROWS - 1
    bias = jnp.concatenate([e2_ref[first + 2 * pair] for pair in range(NA_ROWS // 2)], axis=1)
    s = jnp.where(ok, s + bias, NEG)
    return q0, k0, first, q, k, v, s


def _nbr_col_ok():
    qc = lax.broadcasted_iota(jnp.int32, (GRID_W, NA_ROWS * GRID_W), 0)
    kc = lax.broadcasted_iota(jnp.int32, (GRID_W, NA_ROWS * GRID_W), 1) % GRID_W
    cs = jnp.clip(qc - NA_COLS // 2, 0, GRID_W - NA_COLS)
    return (kc >= cs) & (kc < cs + NA_COLS)


def _attn_b_fwd(qkv3, e2):
    _, S, _ = qkv3.shape
    rows = S // GRID_W
    assert rows >= NA_ROWS

    def body(qkv_ref, e2_ref, o_ref, lse_ref):
        ok = _nbr_col_ok()

        def step(i, carry):
            blocks = [_nbr_scores(qkv_ref, e2_ref, NBR_SIDE * i + u, rows, ok) for u in range(NBR_SIDE)]
            soft = []
            for q0, _, _, _, _, v, s in blocks:
                m = jnp.max(s, axis=-1, keepdims=True)
                p = jnp.exp(s - m)
                den = jnp.sum(p, axis=-1, keepdims=True)
                soft.append((q0, (p / den).astype(BF), v, m + jnp.log(den)))
            for q0, pn, v, lse in soft:
                o_ref[pl.ds(q0, GRID_W), :] = jnp.dot(pn, v, preferred_element_type=F32).astype(BF)
                lse_ref[pl.ds(q0, GRID_W), :] = jnp.broadcast_to(lse, (GRID_W, HEAD_DIM))
            return carry

        lax.fori_loop(0, rows // NBR_SIDE, step, 0)

    out = pl.BlockSpec((S, HEAD_DIM), lambda h: (0, h))
    return ORDER.call(
        body, [qkv3, e2],
        [pl.BlockSpec((3, S, HEAD_DIM), lambda h: (0, 0, N_HEADS_A + h)),
         pl.BlockSpec((None, RPB_ROWS - 1, GRID_W, 128), lambda h: (h, 0, 0, 0))],
        name="attn_b_fwd", grid=(4,),
        out_specs=[out, out], out_shape=[_sds((S, 512), BF), _sds((S, 512), F32)],
        compiler_params=_cparams(("parallel",)),
    )


def _attn_b_bwd(qkv3, e2, dy, y, lse, dqkv3):
    _, S, _ = qkv3.shape
    rows = S // GRID_W
    nk = NA_ROWS * GRID_W

    def body(qkv_ref, e2_ref, dy_ref, y_ref, lse_ref, _, out_ref, de2_ref, dk_acc, dv_acc):
        ok = _nbr_col_ok()
        dk_acc[...] = jnp.zeros_like(dk_acc)
        dv_acc[...] = jnp.zeros_like(dv_acc)
        de2_ref[...] = jnp.zeros_like(de2_ref)

        def step(i, carry):
            blocks = [_nbr_scores(qkv_ref, e2_ref, NBR_SIDE * i + u, rows, ok) for u in range(NBR_SIDE)]
            dys = [dy_ref[pl.ds(b[0], GRID_W), :] for b in blocks]
            dps = [lax.dot_general(dyv, b[5], (NT, ((), ())), preferred_element_type=F32) for dyv, b in zip(dys, blocks)]
            grads = []
            for (q0, k0, first, q, k, v, s), dyv, dp in zip(blocks, dys, dps):
                qrows = pl.ds(q0, GRID_W)
                delta = jnp.sum(dyv.astype(F32) * y_ref[qrows, :].astype(F32), axis=-1, keepdims=True)
                p = jnp.where(ok, jnp.exp(s - jnp.tile(lse_ref[qrows, :], (1, nk // HEAD_DIM))), 0.0)
                ds = p * (dp - delta)
                for pair in range(NA_ROWS // 2):
                    de2_ref[first + 2 * pair] += ds[:, pair * 128:(pair + 1) * 128]
                grads.append((ds.astype(BF), p.astype(BF)))
            for (q0, k0, first, q, k, v, s), dyv, (dsb, pb) in zip(blocks, dys, grads):
                out_ref[0, pl.ds(q0, GRID_W), :] = (jnp.dot(dsb, k, preferred_element_type=F32) * SCALE).astype(BF)
                keys = pl.ds(k0, nk)
                dk_acc[keys, :] += lax.dot_general(dsb, q, (TN, ((), ())), preferred_element_type=F32) * SCALE
                dv_acc[keys, :] += lax.dot_general(pb, dyv, (TN, ((), ())), preferred_element_type=F32)
            return carry

        lax.fori_loop(0, rows // NBR_SIDE, step, 0)
        out_ref[1] = dk_acc[...].astype(BF)
        out_ref[2] = dv_acc[...].astype(BF)

    heads = pl.BlockSpec((3, S, HEAD_DIM), lambda h: (0, 0, N_HEADS_A + h))
    row = pl.BlockSpec((S, HEAD_DIM), lambda h: (0, h))
    table = pl.BlockSpec((None, RPB_ROWS - 1, GRID_W, 128), lambda h: (h, 0, 0, 0))
    return ORDER.call(
        body, [qkv3, e2, dy, y, lse, dqkv3],
        [heads, table, row, row, row, pl.BlockSpec(memory_space=pl.ANY)], name="attn_b_bwd", grid=(4,),
        out_specs=[heads, table],
        out_shape=[_sds((3, S, QKV_W), BF), _sds((4, RPB_ROWS - 1, GRID_W, 128), F32)],
        scratch_shapes=[pltpu.VMEM((S, HEAD_DIM), F32), pltpu.VMEM((S, HEAD_DIM), F32)],
        input_output_aliases={5: 0},
        compiler_params=_cparams(("parallel",)), chain_output=1,
    )


def _rpb_to_table(rpb):
    pad = jnp.pad(rpb, ((0, 0), (0, 0), (0, 1)))
    pairs = jnp.concatenate([pad[:, :-1], pad[:, 1:]], axis=-1).reshape(4 * (RPB_ROWS - 1), 64)
    onehot = jnp.asarray(_toeplitz_onehot())
    n = onehot.shape[1]
    tn = 2048
    full = lambda i, j, k: (0, 0)
    (e2,) = _matmul("rpb_table", pairs, onehot, pl.BlockSpec(pairs.shape, full),
                    pl.BlockSpec((64, tn), lambda i, j, k: (0, j)), NN, (1, n // tn, 1), (pairs.shape[0], tn), [],
                    [(_sds((pairs.shape[0], n), F32), pl.BlockSpec((pairs.shape[0], tn), lambda i, j, k: (0, j)))],
                    _store(F32), precision=lax.Precision.HIGHEST)
    return e2.reshape(4, RPB_ROWS - 1, GRID_W, 128)


def _table_grad_to_rpb(de2):
    onehot = jnp.asarray(_toeplitz_onehot())
    n = onehot.shape[1]
    flat = de2.reshape(4 * (RPB_ROWS - 1), n)
    tk = 2048
    (dpairs,) = _matmul("rpb_table_grad", flat, onehot, pl.BlockSpec((flat.shape[0], tk), lambda i, j, k: (0, k)),
                        pl.BlockSpec((64, tk), lambda i, j, k: (0, k)), NT, (1, 1, n // tk), (flat.shape[0], 64), [],
                        [(_sds((flat.shape[0], 64), F32), pl.BlockSpec((flat.shape[0], 64), lambda i, j, k: (0, 0)))],
                        _store(F32), precision=lax.Precision.HIGHEST)
    dpairs = dpairs.reshape(4, RPB_ROWS - 1, 64)
    zero = jnp.zeros((4, 1, RPB_COLS), F32)
    return (jnp.concatenate([dpairs[:, :, :RPB_COLS], zero], axis=1)
            + jnp.concatenate([zero, dpairs[:, :, 32:32 + RPB_COLS]], axis=1))


HBM = pl.BlockSpec(memory_space=pl.ANY)


def _place():
    x, y, c = lax.axis_index("x"), lax.axis_index("y"), lax.axis_index("c")
    chips = [(1 - x, y), (x, 1 - y), (1 - x, 1 - y)]
    return x, y, c, chips


def _remote(src, dst, send_sem, recv_sem, to):
    return pltpu.make_async_remote_copy(src_ref=src, dst_ref=dst, send_sem=send_sem, recv_sem=recv_sem,
                                        device_id=to, device_id_type=MESH)


def _place_shard(name, w, me, plain=False):
    R, C = w.shape
    tr = _tile(R, 256)

    def body(me_ref, w_ref, *o_refs):
        for o_ref in o_refs:
            o_ref[...] = w_ref[...].astype(BF)

    row = pl.BlockSpec((tr, C), lambda i, mr: (i, 0))
    placed = pl.BlockSpec((None, tr, C), lambda i, mr: (mr[0], i, 0))
    return ORDER.call(
        body, [w], [row], prefetch=(me,), name=name, grid=(R // tr,),
        out_specs=[placed, row] if plain else [placed],
        out_shape=[_sds((N_CHIPS, R, C), BF)] + ([_sds((R, C), BF)] if plain else []),
        compiler_params=_cparams(("parallel",)),
    )


SEM = pl.BlockSpec(memory_space=pltpu.SEMAPHORE)
IN_HBM = pl.BlockSpec(memory_space=pltpu.HBM)
DATAFLOW = pltpu.SideEffectType.DATAFLOW_SIDE_EFFECTING


def _in_hbm(a):
    return pltpu.with_memory_space_constraint(a, pltpu.HBM)


def _copy_start(name, bufs, copies, n_copies, earlier=None):
    n = len(bufs)
    after = None if any(b is ORDER.last for b in bufs) else ORDER.last
    n_extra = (2 if earlier is not None else 0) + (1 if after is not None else 0)

    def body(*refs):
        ins = refs[:n]
        if earlier is not None:
            for k, (src, dst, to) in enumerate(earlier[0](ins)):
                cp = _remote(src, dst, refs[n].at[k], refs[n + 1].at[k], to)
                cp.wait_send()
                cp.wait_recv()
        send_sems, recv_sems = refs[n + n_extra], refs[n + n_extra + 1]
        for k, (src, dst, to) in enumerate(copies(ins)):
            _remote(src, dst, send_sems.at[k], recv_sems.at[k], to).start()
        refs[-1][...] = jnp.zeros((8, 128), F32)

    operands = [_in_hbm(b) for b in bufs]
    in_specs = [IN_HBM] * n
    if earlier is not None:
        operands += [earlier[1], earlier[2]]
        in_specs += [SEM, SEM]
    if after is not None:
        operands.append(after)
        in_specs.append(HBM)
    outs = pl.pallas_call(
        body, name=name,
        out_shape=(pltpu.SemaphoreType.DMA((n_copies,)), pltpu.SemaphoreType.DMA((n_copies,)),
                   *[pltpu.HBM(b.shape, b.dtype) for b in bufs], _sds((8, 128), F32)),
        in_specs=in_specs,
        out_specs=(SEM, SEM, *[IN_HBM] * n, pl.BlockSpec(memory_space=pltpu.VMEM)),
        input_output_aliases={i: 2 + i for i in range(n)},
        compiler_params=pltpu.CompilerParams(has_side_effects=DATAFLOW),
    )(*operands)
    ORDER.last = outs[-1]
    return outs[0], outs[1], list(outs[2:2 + n])


def _copy_wait(name, bufs, copies, send_sems, recv_sems):
    n = len(bufs)
    after = ORDER.last

    def body(*refs):
        ins = refs[:n]
        for k, (src, dst, to) in enumerate(copies(ins)):
            cp = _remote(src, dst, refs[n].at[k], refs[n + 1].at[k], to)
            cp.wait_send()
            cp.wait_recv()

    outs = list(pl.pallas_call(
        body, name=name,
        out_shape=tuple(pltpu.HBM(b.shape, b.dtype) for b in bufs),
        in_specs=[IN_HBM] * n + [SEM, SEM, HBM], out_specs=tuple([IN_HBM] * n),
        input_output_aliases={i: i for i in range(n)},
        compiler_params=pltpu.CompilerParams(has_side_effects=DATAFLOW),
    )(*bufs, send_sems, recv_sems, after))
    ORDER.last = outs[0]
    return outs


def _gather_hop1(bufs):
    x, y, c, chips = _place()
    out = []
    for b in bufs:
        half = b.shape[1] // 2
        mine = b.at[2 * x + y, pl.ds(c * half, half), :]
        out += [(mine, mine, (*chip, c)) for chip in chips]
    return out


def _gather_hop2(bufs):
    x, y, c, chips = _place()
    out = []
    for b in bufs:
        half = b.shape[1] // 2
        for chip in chips:
            landed = b.at[2 * chip[0] + chip[1], pl.ds(c * half, half), :]
            out.append((landed, landed, (x, y, 1 - c)))
    return out


def _swap_copies(bufs):
    x, y, c, _ = _place()
    n = len(bufs) // 2
    out = []
    for p, land in zip(bufs[:n], bufs[n:]):
        half = p.shape[1] // 2
        out.append((p.at[:, pl.ds((1 - c) * half, half), :], land, (x, y, 1 - c)))
    return out


def _scatter_copies(bufs):
    _, _, c, chips = _place()
    n = len(bufs) // 2
    out = []
    for s_, land in zip(bufs[:n], bufs[n:]):
        out += [(s_.at[2 * chip[0] + chip[1]], land.at[j], (*chip, c)) for j, chip in enumerate(chips)]
    return out


def _join_copies(bufs):
    x, y, c, _ = _place()
    out = []
    for b in bufs:
        half = b.shape[0] // 2
        mine = b.at[pl.ds(c * half, half), :]
        out.append((mine, mine, (x, y, 1 - c)))
    return out


def _gather_small(vec):
    m_per, n = vec.shape

    def body(x_ref, out_ref, send_sems, recv_sems, local_sem):
        x, y, c, chips = _place()
        me, sibling = (x, y, c), (x, y, 1 - c)

        def rows(px, py, pc):
            return out_ref.at[pl.ds((4 * px + 2 * py + pc) * m_per, m_per), :]

        def copy(k, block, to, src=None):
            return _remote(rows(*block) if src is None else src, rows(*block), send_sems.at[k], recv_sems.at[k], to)

        mine = pltpu.make_async_copy(x_ref, rows(*me), local_sem)
        mine.start()
        first = [copy(0, me, sibling, src=x_ref)]
        first += [copy(1 + j, me, (*chip, c), src=x_ref) for j, chip in enumerate(chips)]
        for cp in first:
            cp.start()
        passed = [copy(4 + j, (*chip, c), sibling) for j, chip in enumerate(chips)]
        for j, chip in enumerate(chips):
            copy(1 + j, (*chip, c), me).wait_recv()
            passed[j].start()
        copy(0, sibling, me).wait_recv()
        for j, chip in enumerate(chips):
            copy(4 + j, (*chip, 1 - c), me).wait_recv()
        for cp in first + passed:
            cp.wait_send()
        mine.wait()

    return ORDER.call(
        body, [vec], [pl.BlockSpec(memory_space=pltpu.VMEM)], name="gather_small_grads",
        out_shape=_sds((8 * m_per, n), vec.dtype), out_specs=pl.BlockSpec(memory_space=pltpu.VMEM),
        scratch_shapes=[pltpu.SemaphoreType.DMA((7,)), pltpu.SemaphoreType.DMA((7,)), pltpu.SemaphoreType.DMA],
    )


def _add_sibling(name, partial, received, c):
    _, R, C = partial.shape
    half = R // 2
    tr = _tile(half, 256)
    nb = half // tr

    def body(c_ref, p_ref, r_ref, o_ref):
        o_ref[...] = (p_ref[...].astype(F32) + r_ref[...].astype(F32)).astype(BF)

    return ORDER.call(
        body, [partial, received],
        [pl.BlockSpec((None, tr, C), lambda j, i, cr: (j, cr[0] * nb + i, 0)),
         pl.BlockSpec((None, tr, C), lambda j, i, cr: (j, i, 0))],
        prefetch=(c,), name=name, grid=(N_CHIPS, nb),
        out_specs=pl.BlockSpec((None, tr, C), lambda j, i, cr: (j, i, 0)),
        out_shape=_sds((N_CHIPS, half, C), BF), compiler_params=_cparams(("parallel", "parallel")),
    )


def _add_chips(name, sums, received, me_c):
    _, half, C = sums.shape
    tr = _tile(half, 256)
    nb = half // tr

    def body(mc_ref, s_ref, r_ref, o_ref):
        acc = s_ref[...].astype(F32)
        for j in range(3):
            acc = acc + r_ref[j].astype(F32)
        o_ref[...] = acc

    return ORDER.call(
        body, [sums, received],
        [pl.BlockSpec((None, tr, C), lambda i, mc: (mc[0], i, 0)),
         pl.BlockSpec((3, tr, C), lambda i, mc: (0, i, 0))],
        prefetch=(me_c,), name=name, grid=(nb,),
        out_specs=pl.BlockSpec((tr, C), lambda i, mc: (mc[1] * nb + i, 0)),
        out_shape=_sds((2 * half, C), F32), compiler_params=_cparams(("parallel",)),
    )


def _adamw_math(w, g, m, v):
    m = ADAM_B1 * m + (1.0 - ADAM_B1) * g
    v = ADAM_B2 * v + (1.0 - ADAM_B2) * (g * g)
    m_hat = m / (1.0 - ADAM_B1 ** ADAM_STEP)
    v_hat = v / (1.0 - ADAM_B2 ** ADAM_STEP)
    delta = -ADAM_LR * (m_hat / (jnp.sqrt(v_hat) + ADAM_EPS) + ADAM_WD * w)
    return delta, m, v


def _adamw(name, w, g, m, v):
    R, C = w.shape
    tr = _tile(R, 256)

    def body(w_ref, g_ref, m_ref, v_ref, go_ref, d_ref, mo_ref, vo_ref):
        gv = g_ref[...]
        go_ref[...] = gv
        d_ref[...], mo_ref[...], vo_ref[...] = _adamw_math(w_ref[...], gv, m_ref[...], v_ref[...])

    row = pl.BlockSpec((tr, C), lambda i: (i, 0))
    return ORDER.call(
        body, [w, g, m, v], [row] * 4, name=name, grid=(R // tr,), out_specs=[row] * 4,
        out_shape=[_sds((R, C), F32)] * 4, compiler_params=_cparams(("parallel",)), chain_output=1,
    )


def _adamw_small(gathered, w, m, v):
    rows, n = w.shape

    def body(ga_ref, w_ref, m_ref, v_ref, go_ref, d_ref, mo_ref, vo_ref):
        g = ga_ref[pl.ds(0, rows), :]
        for dev in range(1, 8):
            g = g + ga_ref[pl.ds(dev * rows, rows), :]
        go_ref[...] = g
        d_ref[...], mo_ref[...], vo_ref[...] = _adamw_math(w_ref[...], g, m_ref[...], v_ref[...])

    whole = pl.BlockSpec(memory_space=pltpu.VMEM)
    return ORDER.call(
        body, [gathered, w, m, v], [whole] * 4, name="adamw_small", out_specs=[whole] * 4,
        out_shape=[_sds((rows, n), F32)] * 4, compiler_params=_cparams(), chain_output=1,
    )


def _proj_merge(y_a, y_b, gpa, gpb, g3):
    S, K = y_a.shape
    _, _, Nq = gpa.shape
    D = N_CHIPS * Nq
    tm, tn = _tile(S, 1024), _tile(Nq, 512)
    q = Nq // tn

    def body(ya_ref, yb_ref, wa_ref, wb_ref, g_ref, merged_ref, c_ref):
        pa = jnp.dot(ya_ref[...], wa_ref[...], preferred_element_type=F32)
        pb = jnp.dot(yb_ref[...], wb_ref[...], preferred_element_type=F32)
        g = g_ref[...].astype(F32)
        merged_ref[...] = (g[0] * pa + g[1] * pb).astype(BF)
        c_ref[0] = (pa * g[0] * (1.0 - g[0])).astype(BF)
        c_ref[1] = (pb * g[1] * (1.0 - g[1])).astype(BF)

    rows = pl.BlockSpec((tm, K), lambda i, j: (i, 0))
    weight = pl.BlockSpec((None, K, tn), lambda i, j: (j // q, 0, j % q))
    pair = pl.BlockSpec((2, tm, tn), lambda i, j: (0, i, j))
    return ORDER.call(
        body, [y_a, y_b, gpa, gpb, g3], [rows, rows, weight, weight, pair], name="proj_merge",
        grid=(S // tm, N_CHIPS * q), out_specs=[pl.BlockSpec((tm, tn), lambda i, j: (i, j)), pair],
        out_shape=[_sds((S, D), BF), _sds((2, S, D), BF)], compiler_params=_cparams(("parallel", "parallel")))


def _out_proj_dx(dx1b, wout, g3, c3, gpa, gpb):
    S, D = dx1b.shape
    _, K, Nq = gpa.shape
    tm, tn = _tile(S, 1024), Nq
    nj = D // tn

    def body(a_ref, w_ref, g_ref, c_ref, wa_ref, wb_ref, dpa_ref, dpb_ref, dg_ref, db_ref, dya_ref, dyb_ref,
             acc_a, acc_b):
        j = pl.program_id(1)
        dm = lax.dot_general(a_ref[...], w_ref[...], (NT, ((), ())), preferred_element_type=F32)
        g, c = g_ref[...].astype(F32), c_ref[...].astype(F32)
        dpa, dpb = (dm * g[0]).astype(BF), (dm * g[1]).astype(BF)
        dpa_ref[...] = dpa
        dpb_ref[...] = dpb
        dga, dgb = dm * c[0], dm * c[1]
        dg_ref[0] = dga.astype(BF)
        dg_ref[1] = dgb.astype(BF)
        db_ref[...] = jnp.concatenate([jnp.sum(dga, axis=0, keepdims=True), jnp.sum(dgb, axis=0, keepdims=True)], 0)
        ya = lax.dot_general(dpa, wa_ref[...], (NT, ((), ())), preferred_element_type=F32)
        yb = lax.dot_general(dpb, wb_ref[...], (NT, ((), ())), preferred_element_type=F32)

        @pl.when(j == 0)
        def _():
            acc_a[...] = ya
            acc_b[...] = yb

        @pl.when(j > 0)
        def _():
            acc_a[...] += ya
            acc_b[...] += yb

        @pl.when(j == nj - 1)
        def _():
            dya_ref[...] = acc_a[...].astype(BF)
            dyb_ref[...] = acc_b[...].astype(BF)

    tile = pl.BlockSpec((tm, tn), lambda i, j: (i, j))
    pair = pl.BlockSpec((2, tm, tn), lambda i, j: (0, i, j))
    shard = pl.BlockSpec((None, K, tn), lambda i, j: (j, 0, 0))
    rows = pl.BlockSpec((tm, K), lambda i, j: (i, 0))
    return ORDER.call(
        body, [dx1b, wout, g3, c3, gpa, gpb],
        [pl.BlockSpec((tm, D), lambda i, j: (i, 0)), pl.BlockSpec((tn, D), lambda i, j: (j, 0)), pair, pair, shard, shard],
        name="out_proj_dx", grid=(S // tm, nj),
        out_specs=[tile, tile, pair, pl.BlockSpec((None, 2, tn), lambda i, j: (i, 0, j)), rows, rows],
        out_shape=[_sds((S, D), BF), _sds((S, D), BF), _sds((2, S, D), BF), _sds((S // tm, 2, D), F32),
                   _sds((S, K), BF), _sds((S, K), BF)],
        scratch_shapes=[pltpu.VMEM((tm, K), F32), pltpu.VMEM((tm, K), F32)],
        compiler_params=_cparams(("parallel", "arbitrary")))


class _Exchange:
    GATHER = (("qkv",), ("gate",), ("proj_a", "proj_b", "out"), ("up",), ("down",))
    REDUCE = {"mlp": ("down", "up"), "mix": ("out", "proj_a", "proj_b"), "in": ("qkv", "gate")}

    OWN_FIRST = ("qkv", "gate")

    def __init__(self, shards, me, c, moments):
        self.me, self.c = me, c
        self.shards, self.moments = shards, moments
        self.hop1, self.hop2, self.stage, self.grads, self.own, self.updates = {}, {}, {}, {}, {}, {}
        for g, names in enumerate(self.GATHER):
            bufs = []
            for n in names:
                placed = _place_shard(f"place_{n}", shards[n], me, plain=n in self.OWN_FIRST)
                bufs.append(placed[0])
                if n in self.OWN_FIRST:
                    self.own[n] = placed[1]
            self.hop1[g] = _copy_start(f"gather{g}_start", bufs, _gather_hop1, 3 * len(names))

    def forward(self, g):
        send, recv, thru = self.hop1.pop(g)
        self.hop2[g] = _copy_start(f"gather{g}_forward", thru, _gather_hop2, len(thru) * 3,
                                   earlier=(_gather_hop1, send, recv))

    def weights(self, g):
        send, recv, thru = self.hop2.pop(g)
        return _copy_wait(f"gather{g}_wait", thru, _gather_hop2, send, recv)

    def adamw_beside(self, name):
        def update(w, g, m, v):
            return (g,) + _adamw_math(w, g, m, v)
        return update, [self.shards[name], self.grads[name], *self.moments[name]], 4

    def reduce(self, key, partials=None):
        names = self.REDUCE[key]
        n = len(names)
        if partials is not None:
            lands = [lax.empty((p.shape[0], p.shape[1] // 2, p.shape[2]), p.dtype) for p in partials]
            self.stage[key] = ("swap",) + _copy_start(f"reduce_{key}_swap", list(partials) + lands, _swap_copies, n)
            return
        kind, send, recv, thru = self.stage.pop(key)
        if kind == "swap":
            thru = _copy_wait(f"reduce_{key}_swap_wait", thru, _swap_copies, send, recv)
            sums = [_add_sibling(f"reduce_{nm}_add_sibling", p, r, self.c)
                    for nm, p, r in zip(names, thru[:n], thru[n:])]
            lands = [lax.empty((3,) + s_.shape[1:], s_.dtype) for s_ in sums]
            self.stage[key] = ("scatter",) + _copy_start(f"reduce_{key}_scatter", sums + lands, _scatter_copies, 3 * n)
        elif kind == "scatter":
            thru = _copy_wait(f"reduce_{key}_scatter_wait", thru, _scatter_copies, send, recv)
            me_c = jnp.concatenate([self.me, self.c])
            halves = [_add_chips(f"reduce_{nm}_add_chips", s_, r, me_c)
                      for nm, s_, r in zip(names, thru[:n], thru[n:])]
            self.stage[key] = ("join",) + _copy_start(f"reduce_{key}_join", halves, _join_copies, n)
        else:
            thru = _copy_wait(f"reduce_{key}_join_wait", thru, _join_copies, send, recv)
            self.grads.update(zip(names, thru))


def _forward_backward(x, target, norm_mix, b_gate, rpb, norm_mlp, norm_final, ex):
    S, D = x.shape

    h1 = _rms_fwd("rms_mix", x, norm_mix)
    nq = QKV_W // 512
    qkv_out = (((3, S, QKV_W), BF), lambda i, T: (T // nq, i, T % nq))
    tg = _tile(ex.own["gate"].shape[1], 1024)
    ng = D // tg
    gate_out = (((2, S, D), BF), lambda i, T: (T // ng, i, T % ng))

    def gate_epilogue(acc, ex_, outs):
        outs[0][...] = jax.nn.sigmoid(acc + ex_[0][...]).astype(BF)

    qkv3 = _mm_nn_shards("qkv_own", h1, ex.own["qkv"], ex.me, True, *qkv_out, _store(BF))
    g3 = _mm_nn_shards("gate_own", h1, ex.own["gate"], ex.me, True, *gate_out, gate_epilogue, extras=[b_gate], tn=tg)
    ex.forward(0)
    e2 = _rpb_to_table(rpb)
    (gq,) = ex.weights(0)
    qkv3 = _mm_nn_shards("qkv", h1, gq, ex.me, False, *qkv_out, _store(BF), into=qkv3)

    ex.forward(1)
    outs_a = [_attn_a_fwd(qkv3, 0, DILATIONS[0])]
    (gg,) = ex.weights(1)
    g3 = _mm_nn_shards("gate", h1, gg, ex.me, False, *gate_out, gate_epilogue, extras=[b_gate], into=g3, tn=tg)

    ex.forward(2)
    qkv_views = _qkv_views("qkv_views", qkv3)
    outs_a += [_attn_a_fwd(qkv_views[d], grp, d) for grp, d in enumerate(DILATIONS) if grp > 0]
    y_a, lj = _attn_a_combine([o for o, _ in outs_a], [l for _, l in outs_a])
    y_b, lse_b = _attn_b_fwd(qkv3, e2)
    gpa, gpb, gout = ex.weights(2)
    wout = gout.reshape(D, D)
    merged, c3 = _proj_merge(y_a, y_b, gpa, gpb, g3)

    def residual_epilogue(acc, ex_, outs):
        outs[0][...] = acc + ex_[0][...]

    def nn_plain(name, a, w, res, bm=1024, bn=1024):
        M, K = a.shape
        N = w.shape[1]
        bm, bn, bk = _tile(M, bm), _tile(N, bn), _tile(K, 2048)
        t = pl.BlockSpec((bm, bn), lambda i, j, k: (i, j))
        return _matmul(name, a, w, pl.BlockSpec((bm, bk), lambda i, j, k: (i, k)),
                       pl.BlockSpec((bk, bn), lambda i, j, k: (k, j)), NN, (M // bm, N // bn, K // bk), (bm, bn),
                       [(res, t)], [(_sds((M, N), F32), t)], residual_epilogue)[0]

    ex.forward(3)
    x1 = nn_plain("out_proj", merged, wout, x, bm=512, bn=2048)
    h2 = _rms_fwd("rms_mlp", x1, norm_mlp)
    (gup,) = ex.weights(3)
    F = gup.shape[2] * N_CHIPS

    def up_epilogue(acc, ex_, outs):
        ru = jnp.maximum(acc, 0.0)
        outs[0][...] = (ru * ru).astype(BF)
        outs[1][...] = ru.astype(BF)

    tu = _tile(gup.shape[2], 2048)
    ut = pl.BlockSpec((_tile(S, 1024), tu), lambda i, j, k: (i, j))
    (act, ru), _ = _mm_nn_cols("mlp_up", h2, gup, BF, epilogue=up_epilogue, tn=tu,
                               outs=[(_sds((S, F), BF), ut), (_sds((S, F), BF), ut)])
    ex.forward(4)
    (gdown,) = ex.weights(4)
    wdown = gdown.reshape(F, D)
    x2 = nn_plain("mlp_down", act, wdown, x1)

    loss, dx2, dx2b, d_norm_final = _loss_head(x2, target, norm_final.reshape(1, D))

    def nt_rows(name, a, w, epilogue, extras, outs, bn=1024):
        M, N = a.shape
        K = w.shape[0]
        bm, bn, bk = _tile(M, 1024), _tile(K, bn), _tile(N, 2048)
        return _matmul(name, a, w, pl.BlockSpec((bm, bk), lambda i, j, k: (i, k)),
                       pl.BlockSpec((bn, bk), lambda i, j, k: (j, k)), NT, (M // bm, K // bn, N // bk), (bm, bn),
                       extras(bm, bn), outs(bm, bn), epilogue)

    def nt_cols(name, a_spec_fn, a, g, M, epilogue, extras, outs, bk, bn=1024, side=None):
        _, K, Nq = g.shape
        bm, bn, bk = _tile(M, 1024), _tile(K, bn), _tile(Nq, bk)
        q = Nq // bk
        return _matmul(name, a, g, a_spec_fn(bm, bk), pl.BlockSpec((None, bn, bk), lambda i, j, k: (k // q, j, k % q)),
                       NT, (M // bm, K // bn, N_CHIPS * q), (bm, bn), extras(bm, bn), outs(bm, bn), epilogue,
                       side=side)

    def tn_grad(name, a, a_spec_fn, b, b_spec_fn, Kin, N, out_shape, out_spec_fn, bn=1024):
        bm, bn, bk = _tile(Kin, 1024), _tile(N, bn), _tile(S, 4096)
        return _matmul(name, a, b, a_spec_fn(bk, bm), b_spec_fn(bk, bn), TN, (Kin // bm, N // bn, S // bk), (bm, bn),
                       [], [(_sds(out_shape, BF), out_spec_fn(bm, bn))], _store(BF))[0]

    plain_a = lambda bk, bm: pl.BlockSpec((bk, bm), lambda i, j, k: (k, i))
    plain_b = lambda bk, bn: pl.BlockSpec((bk, bn), lambda i, j, k: (k, j))
    plain_o = lambda bm, bn: pl.BlockSpec((bm, bn), lambda i, j, k: (i, j))
    a_rows = lambda bm, bk: pl.BlockSpec((bm, bk), lambda i, j, k: (i, k))

    def cols_o(Nq):
        def spec(bm, bn):
            q = Nq // bn
            return pl.BlockSpec((None, bm, bn), lambda i, j, k: (j // q, i, j % q))
        return spec

    def du_epilogue(acc, ex_, outs):
        outs[0][...] = (acc * (2.0 * ex_[0][...].astype(F32))).astype(BF)

    dw_down = tn_grad("mlp_down_dw", act, plain_a, dx2b, plain_b, F, D, (F, D), plain_o)
    (du,) = nt_rows("mlp_down_dx", dx2b, wdown, du_epilogue,
                    lambda bm, bn: [(ru, plain_o(bm, bn))], lambda bm, bn: [(_sds((S, F), BF), plain_o(bm, bn))],
                    bn=2048)

    fq = gup.shape[2]
    dw_up = tn_grad("mlp_up_dw", h2, plain_a, du, plain_b, D, F, (N_CHIPS, D, fq), cols_o(fq), bn=min(fq, 1024))
    ex.reduce("mlp", partials=[dw_down.reshape(N_CHIPS, F // N_CHIPS, D), dw_up])
    (dh2,) = nt_cols("mlp_up_dx", a_rows, du, gup, S, _store(F32), lambda bm, bn: [],
                     lambda bm, bn: [(_sds((S, D), F32), plain_o(bm, bn))], 1024, bn=2048)
    ex.reduce("mlp")
    dx1, dx1b, d_norm_mlp = _rms_bwd("rms_mlp_bwd", dh2, x1, norm_mlp, dx2)

    dpa, dpb, dg3, db_gate, dy_a, dy_b = _out_proj_dx(dx1b, wout, g3, c3, gpa, gpb)
    dw_out = tn_grad("out_proj_dw", merged, plain_a, dx1b, plain_b, D, D, (D, D), plain_o)

    pq = gpa.shape[2]
    dw_pa = tn_grad("proj_a_dw", y_a, plain_a, dpa, plain_b, 512, D, (N_CHIPS, 512, pq), cols_o(pq), bn=min(pq, 512))
    dw_pb = tn_grad("proj_b_dw", y_b, plain_a, dpb, plain_b, 512, D, (N_CHIPS, 512, pq), cols_o(pq), bn=min(pq, 512))
    ex.reduce("mix", partials=[dw_out.reshape(N_CHIPS, D // N_CHIPS, D), dw_pa, dw_pb])

    dqkv3 = lax.empty((3, S, QKV_W), BF)
    dqkv3 = _attn_a_bwd(qkv3, dy_a, y_a, lj, dqkv3, 0, DILATIONS[0])
    ex.reduce("mix")
    dy_views, y_views, lj_views = _dilated_rows("attn_a_bwd_rows", [dy_a, y_a, lj])
    dqkv_views = {d: _attn_a_bwd(qkv_views[d], dy_views[d], y_views[d], lj_views[d], None, grp, d)
                  for grp, d in enumerate(DILATIONS) if grp > 0}
    dqkv3 = _qkv_views("dqkv_from_views", dqkv3, dqkv_views)
    dqkv3, de2 = _attn_b_bwd(qkv3, e2, dy_b, y_b, lse_b, dqkv3)
    d_rpb = _table_grad_to_rpb(de2)

    def stacked_a(width):
        def spec(bm, bk):
            q = width // bk
            return pl.BlockSpec((None, bm, bk), lambda i, j, k: (k // q, i, k % q))
        return spec

    def stacked_b(width):
        def spec(bk, bn):
            q = width // bn
            return pl.BlockSpec((None, bk, bn), lambda i, j, k: (j // q, k, j % q))
        return spec

    ex.reduce("mlp")
    dw_qkv = tn_grad("qkv_dw", h1, plain_a, dqkv3, stacked_b(QKV_W), D, 3 * QKV_W, (N_CHIPS,) + gq.shape[1:],
                     cols_o(gq.shape[2]), bn=512)
    dw_gate = tn_grad("gate_dw", h1, plain_a, dg3, stacked_b(D), D, 2 * D, (N_CHIPS,) + gg.shape[1:],
                      cols_o(gg.shape[2]), bn=gg.shape[2])
    ex.reduce("in", partials=[dw_qkv, dw_gate])
    ex.reduce("mlp")
    dh1_q, *ex.updates["down"] = nt_cols(
        "qkv_dx", stacked_a(QKV_W), dqkv3, gq, S, _store(F32), lambda bm, bn: [],
        lambda bm, bn: [(_sds((S, D), F32), plain_o(bm, bn))], 512, bn=2048, side=ex.adamw_beside("down"))
    ex.reduce("in")
    ex.reduce("mix")

    def add_epilogue(acc, ex_, outs):
        outs[0][...] = acc + ex_[0][...]

    dh1, *ex.updates["up"] = nt_cols(
        "gate_dx", stacked_a(D), dg3, gg, S, add_epilogue, lambda bm, bn: [(dh1_q, plain_o(bm, bn))],
        lambda bm, bn: [(_sds((S, D), F32), plain_o(bm, bn))], gg.shape[2], side=ex.adamw_beside("up"))
    grad_x, _, d_norm_mix = _rms_bwd("rms_mix_bwd", dh1, x, norm_mix, dx1)
    ex.reduce("mix")

    small = [d_norm_mix, jnp.sum(db_gate, axis=0).reshape(1, 2 * D), d_rpb, d_norm_mlp, d_norm_final]
    return loss, grad_x, small


def _pack_small(parts, width):
    flat = jnp.concatenate([p.reshape(-1) for p in parts])
    return jnp.pad(flat, (0, 8 * width - flat.shape[0])).reshape(8, width)


def kernel(x, norm_mix, w_qkv, w_gate, b_gate, rpb, w_proj_a, w_proj_b, w_out, norm_mlp, w_up, w_down, norm_final, loss_target, m_norm_mix, m_w_qkv, m_w_gate, m_b_gate, m_rpb, m_w_proj_a, m_w_proj_b, m_w_out, m_norm_mlp, m_w_up, m_w_down, m_norm_final, v_norm_mix, v_w_qkv, v_w_gate, v_b_gate, v_rpb, v_w_proj_a, v_w_proj_b, v_w_out, v_norm_mlp, v_w_up, v_w_down, v_norm_final):
    names = ["qkv", "gate", "proj_a", "proj_b", "out", "up", "down"]
    big = dict(zip(names, [w_qkv[0], w_gate[0], w_proj_a[0], w_proj_b[0], w_out[0], w_up[0], w_down[0]]))
    big_m = dict(zip(names, [m_w_qkv[0], m_w_gate[0], m_w_proj_a[0], m_w_proj_b[0], m_w_out[0], m_w_up[0], m_w_down[0]]))
    big_v = dict(zip(names, [v_w_qkv[0], v_w_gate[0], v_w_proj_a[0], v_w_proj_b[0], v_w_out[0], v_w_up[0], v_w_down[0]]))

    c = lax.axis_index("c").astype(jnp.int32).reshape(1)
    me = (2 * lax.axis_index("x") + lax.axis_index("y")).astype(jnp.int32).reshape(1)
    ORDER.last = None
    ex = _Exchange(big, me, c, {n: (big_m[n], big_v[n]) for n in names})
    loss, grad_x, small = _forward_backward(x[0], loss_target[0], norm_mix, b_gate, rpb[0], norm_mlp, norm_final, ex)

    def adamw(group):
        return {n: ex.updates[n] if ex.updates.get(n) else _adamw(f"adamw_{n}", big[n], ex.grads[n], big_m[n], big_v[n])
                for n in _Exchange.REDUCE[group]}

    big_out = {**adamw("mlp"), **adamw("mix")}
    ex.reduce("in")

    small_w = [norm_mix, b_gate, rpb, norm_mlp, norm_final]
    count = sum(int(np.prod(p.shape)) for p in small_w)
    width = -(-count // (8 * 128)) * 128
    packed = _adamw_small(_gather_small(_pack_small(small, width)), _pack_small(small_w, width),
                          _pack_small([m_norm_mix, m_b_gate, m_rpb, m_norm_mlp, m_norm_final], width),
                          _pack_small([v_norm_mix, v_b_gate, v_rpb, v_norm_mlp, v_norm_final], width))
    ex.reduce("in")
    big_out.update(adamw("in"))

    def unpack(flat2d):
        flat, out, at = flat2d.reshape(-1), [], 0
        for p in small_w:
            size = int(np.prod(p.shape))
            out.append(flat[at:at + size].reshape(p.shape))
            at += size
        return out

    small_out = [unpack(a) for a in packed]

    def ordered(kind):
        sm = small_out[kind]
        bg = {n: o[kind][None] for n, o in big_out.items()}
        return [sm[0], bg["qkv"], bg["gate"], sm[1], sm[2], bg["proj_a"], bg["proj_b"], bg["out"], sm[3],
                bg["up"], bg["down"], sm[4]]

    total = lax.psum(loss[0, 0], ("x", "y", "c"))
    return (total, grad_x[None], *ordered(0), *ordered(1), *ordered(2), *ordered(3))
```

```python
import math

import numpy as np
import jax
import jax.numpy as jnp
from jax import lax
from jax.experimental import pallas as pl
from jax.experimental.pallas import tpu as pltpu

BF = jnp.bfloat16
F32 = jnp.float32
MESH = pl.DeviceIdType.MESH

HEAD_DIM = 128
N_HEADS = 16
N_HEADS_A = 12
QKV_W = N_HEADS * HEAD_DIM
DILATIONS = (1, 4, 16)
HALF_WINDOW = 64
GRID_W = 64
NA_ROWS = 8
NA_COLS = 16
RPB_ROWS = 2 * NA_ROWS - 1
RPB_COLS = 2 * NA_COLS - 1
EPS = 1e-6
NEG = -1e30
SCALE = HEAD_DIM ** -0.5

ADAM_LR = 0.001
ADAM_B1 = 0.9
ADAM_B2 = 0.999
ADAM_EPS = 1e-08
ADAM_WD = 0.01
ADAM_STEP = 10

N_CHIPS = 4
VMEM_LIMIT_BYTES = 48 * 1024 * 1024
QB = 256
NBR_SIDE = 16
ROW_TILE = 512


def _key_rows(L):
    return min(QB + 2 * HALF_WINDOW, L)


def _cparams(sem=None):
    return pltpu.CompilerParams(dimension_semantics=sem, vmem_limit_bytes=VMEM_LIMIT_BYTES)


def _tile(dim, want):
    t = min(dim, want)
    assert dim % t == 0, (dim, want)
    return t


class _ProgramOrder:
    def __init__(self):
        self.last = None

    def call(self, body, operands, in_specs, *, prefetch=(), grid=None, out_specs=None, chain_output=0, **kwargs):
        operands, in_specs = list(operands), list(in_specs)
        lead = len(prefetch) + len(operands)
        if self.last is not None and not any(op is self.last for op in operands):
            operands.append(self.last)
            in_specs.append(pl.BlockSpec(memory_space=pl.ANY))
            inner = body

            def body(*refs):
                return inner(*refs[:lead], *refs[lead + 1:])

        if prefetch:
            kwargs["grid_spec"] = pltpu.PrefetchScalarGridSpec(
                num_scalar_prefetch=len(prefetch), grid=grid, in_specs=in_specs, out_specs=out_specs)
        else:
            kwargs.update(in_specs=in_specs, out_specs=out_specs)
            if grid is not None:
                kwargs["grid"] = grid
        out = pl.pallas_call(body, **kwargs)(*prefetch, *operands)
        self.last = out[chain_output] if isinstance(out, (tuple, list)) else out
        return out


ORDER = _ProgramOrder()


NN = ((1,), (0,))
NT = ((1,), (1,))
TN = ((0,), (0,))


def _matmul(name, a, b, a_spec, b_spec, dims, grid, acc_shape, extras, outs, epilogue, precision=None,
            prefetch=(), into=None, side=None):
    n_ex, n_out, nk = len(extras), len(outs), grid[2]
    side_fn, side_in, n_side_out = side if side is not None else (None, [], 0)
    side_spec = None
    n_in = 2 + n_ex + len(side_in) + (into is not None)
    if side is not None:
        R, C = side_in[0].shape
        steps = grid[0] * grid[1] * grid[2]
        side_blocks = max(n for n in range(1, steps + 1) if R % n == 0 and (R // n) % 8 == 0)

        def side_step(*ids):
            return (ids[0] * grid[1] + ids[1]) * grid[2] + ids[2]

        side_spec = pl.BlockSpec((R // side_blocks, C),
                                 lambda *ids: (jnp.minimum(side_step(*ids), side_blocks - 1), 0))

    def body(*refs):
        refs = refs[len(prefetch):]
        a_ref, b_ref = refs[0], refs[1]
        ex_refs = refs[2:2 + n_ex]
        out_refs = refs[n_in:n_in + n_out]
        if side is not None:
            @pl.when(side_step(pl.program_id(0), pl.program_id(1), pl.program_id(2)) < side_blocks)
            def _():
                results = side_fn(*[r[...] for r in refs[2 + n_ex:2 + n_ex + len(side_in)]])
                for o_ref, value in zip(refs[n_in + n_out:n_in + n_out + n_side_out], results):
                    o_ref[...] = value

        def dot():
            return lax.dot_general(a_ref[...], b_ref[...], (dims, ((), ())),
                                   preferred_element_type=F32, precision=precision)

        if nk == 1:
            epilogue(dot(), ex_refs, out_refs)
            return
        acc_ref = refs[-1]
        k = pl.program_id(2)

        @pl.when(k == 0)
        def _():
            acc_ref[...] = dot()

        if nk > 2:
            @pl.when((k > 0) & (k < nk - 1))
            def _():
                acc_ref[...] += dot()

        @pl.when(k == nk - 1)
        def _():
            epilogue(acc_ref[...] + dot(), ex_refs, out_refs)

    operands = [a, b] + [e for e, _ in extras] + list(side_in)
    in_specs = [a_spec, b_spec] + [s for _, s in extras] + [side_spec] * len(side_in)
    kwargs = {}
    if into is not None:
        operands.append(into)
        in_specs.append(pl.BlockSpec(memory_space=pl.ANY))
        kwargs["input_output_aliases"] = {len(prefetch) + n_in - 1: 0}
    return ORDER.call(
        body, operands, in_specs, prefetch=prefetch, name=name, grid=grid,
        out_specs=[s for _, s in outs] + [side_spec] * n_side_out,
        out_shape=[sh for sh, _ in outs] + [_sds(s_.shape, F32) for s_ in side_in[:1]] * n_side_out,
        scratch_shapes=[pltpu.VMEM(acc_shape, F32)] if nk > 1 else [],
        compiler_params=_cparams(("parallel", "parallel", "arbitrary")), **kwargs,
    )


def _mm_nn_shards(name, a, w, me, own, out, out_block, epilogue, extras=(), into=None, tn=512, tm=1024):
    M, K = a.shape
    Nq = w.shape[-1]
    tm, tn = _tile(M, tm), _tile(Nq, tn)
    q = Nq // tn

    def tile(j, me_ref):
        shard = me_ref[0] if own else (me_ref[0] + 1 + j // q) % N_CHIPS
        return shard, j % q, shard * q + j % q

    if own:
        b_spec = pl.BlockSpec((K, tn), lambda i, j, k, me_ref: (0, j))
    else:
        b_spec = pl.BlockSpec((None, K, tn), lambda i, j, k, me_ref: (tile(j, me_ref)[0], 0, tile(j, me_ref)[1]))
    shape, dtype = out
    out_spec = pl.BlockSpec((None, tm, tn), lambda i, j, k, me_ref: out_block(i, tile(j, me_ref)[2]))
    ex = [(e, pl.BlockSpec((1, tn), lambda i, j, k, me_ref: (0, tile(j, me_ref)[2]))) for e in extras]
    return _matmul(name, a, w, pl.BlockSpec((tm, K), lambda i, j, k, me_ref: (i, 0)), b_spec, NN,
                   (M // tm, q if own else (N_CHIPS - 1) * q, 1), (tm, tn), ex, [(_sds(shape, dtype), out_spec)],
                   epilogue, prefetch=(me,), into=into)[0]


def _store(dtype):
    def epilogue(acc, ex, outs):
        outs[0][...] = acc.astype(dtype)
    return epilogue


def _sds(shape, dtype):
    return jax.ShapeDtypeStruct(shape, dtype)


def _mm_nn_cols(name, a, g, out_dtype, epilogue=None, extras=(), outs=None, tm=1024, tn=1024, tk=2048):
    M, K = a.shape
    _, _, Nq = g.shape
    tm, tn, tk = _tile(M, tm), _tile(Nq, tn), _tile(K, tk)
    q = Nq // tn
    grid = (M // tm, N_CHIPS * q, K // tk)
    if outs is None:
        outs = [(_sds((M, N_CHIPS * Nq), out_dtype), pl.BlockSpec((tm, tn), lambda i, j, k: (i, j)))]
    return _matmul(name, a, g, pl.BlockSpec((tm, tk), lambda i, j, k: (i, k)),
                   pl.BlockSpec((None, tk, tn), lambda i, j, k: (j // q, k, j % q)), NN, grid, (tm, tn),
                   list(extras), outs, epilogue or _store(out_dtype)), (tm, tn, tk)


def _rms_fwd(name, x, g):
    S, D = x.shape
    tm = _tile(S, ROW_TILE)

    def body(x_ref, g_ref, h_ref):
        xv = x_ref[...]
        r = lax.rsqrt(jnp.mean(xv * xv, axis=-1, keepdims=True) + EPS)
        h_ref[...] = ((xv * r) * g_ref[...]).astype(BF)

    row = pl.BlockSpec((tm, D), lambda i: (i, 0))
    return ORDER.call(
        body, [x, g], [row, pl.BlockSpec((1, D), lambda i: (0, 0))], name=name, grid=(S // tm,),
        out_specs=row, out_shape=_sds((S, D), BF), compiler_params=_cparams(("parallel",)),
    )


def _rms_bwd(name, dh, x, g, dres):
    S, D = x.shape
    tm = _tile(S, ROW_TILE // 2)

    def body(dh_ref, x_ref, g_ref, dres_ref, dx_ref, dxb_ref, dg_ref):
        xv = x_ref[...]
        r = lax.rsqrt(jnp.mean(xv * xv, axis=-1, keepdims=True) + EPS)
        n = xv * r
        dhv = dh_ref[...]
        dyg = dhv * g_ref[...]
        dx = dres_ref[...] + r * (dyg - n * jnp.mean(dyg * n, axis=-1, keepdims=True))
        dx_ref[...] = dx
        dxb_ref[...] = dx.astype(BF)

        @pl.when(pl.program_id(0) == 0)
        def _():
            dg_ref[...] = jnp.zeros_like(dg_ref)

        dg_ref[...] += jnp.sum(dhv * n, axis=0, keepdims=True)

    row = pl.BlockSpec((tm, D), lambda i: (i, 0))
    vec = pl.BlockSpec((1, D), lambda i: (0, 0))
    return ORDER.call(
        body, [dh, x, g, dres], [row, row, vec, row], name=name, grid=(S // tm,),
        out_specs=[row, row, vec],
        out_shape=[_sds((S, D), F32), _sds((S, D), BF), _sds((1, D), F32)],
        compiler_params=_cparams(("arbitrary",)),
    )


def _loss_head(x2, target, g):
    S, D = x2.shape
    tm = _tile(S, ROW_TILE)

    def body(x_ref, t_ref, g_ref, loss_ref, dx_ref, dxb_ref, dg_ref):
        xv = x_ref[...]
        gv = g_ref[...]
        r = lax.rsqrt(jnp.mean(xv * xv, axis=-1, keepdims=True) + EPS)
        n = xv * r
        e = n * gv - t_ref[...]
        dy = e * (1.0 / D)
        dyg = dy * gv
        dx = r * (dyg - n * jnp.mean(dyg * n, axis=-1, keepdims=True))
        dx_ref[...] = dx
        dxb_ref[...] = dx.astype(BF)

        @pl.when(pl.program_id(0) == 0)
        def _():
            dg_ref[...] = jnp.zeros_like(dg_ref)
            loss_ref[...] = jnp.zeros_like(loss_ref)

        dg_ref[...] += jnp.sum(dy * n, axis=0, keepdims=True)
        per_row = jnp.mean(e * e, axis=-1, keepdims=True)
        loss_ref[...] += 0.5 * jnp.sum(per_row, axis=0, keepdims=True)

    row = pl.BlockSpec((tm, D), lambda i: (i, 0))
    vec = pl.BlockSpec((1, D), lambda i: (0, 0))
    return ORDER.call(
        body, [x2, target, g], [row, row, vec], name="loss_head", grid=(S // tm,),
        out_specs=[pl.BlockSpec((1, 1), lambda i: (0, 0)), row, row, vec],
        out_shape=[_sds((1, 1), F32), _sds((S, D), F32), _sds((S, D), BF), _sds((1, D), F32)],
        compiler_params=_cparams(("arbitrary",)), chain_output=1,
    )


def _chains(L):
    side = min(8, L // QB)
    return side, max(1, 4 // side)


def _band_scores(qkv_ref, i, L, coef, head):
    KB = _key_rows(L)
    lanes = pl.ds(head * HEAD_DIM, HEAD_DIM)
    q0 = pl.multiple_of(i * QB, QB)
    ks = pl.multiple_of(jnp.clip(i * QB - HALF_WINDOW, 0, L - KB), HALF_WINDOW)
    q = qkv_ref[0, pl.ds(q0, QB), lanes]
    k = qkv_ref[1, pl.ds(ks, KB), lanes]
    v = qkv_ref[2, pl.ds(ks, KB), lanes]
    s = lax.dot_general(q, k, (NT, ((), ())), preferred_element_type=F32) * SCALE
    qpos = q0 + lax.broadcasted_iota(jnp.int32, (QB, KB), 0)
    kpos = ks + lax.broadcasted_iota(jnp.int32, (QB, KB), 1)
    rel = jnp.abs(kpos - qpos)
    valid = rel <= HALF_WINDOW
    s = jnp.where(valid, s - coef * rel.astype(F32), NEG)
    return q0, ks, q, k, v, s, valid


def _alibi_coefs(group, d, heads):
    first = 4 * group + 1 + pl.program_id(1) * heads
    scale = jnp.full((1, 1), -(8.0 / N_HEADS_A) * math.log(2.0), F32)
    return [jnp.exp(scale * (first + hh).astype(F32)) * float(d) for hh in range(heads)]


def _dilated_view(qkv3, group, d, heads):
    per = 4 // heads
    L = qkv3.shape[1]
    if d == 1:
        return qkv3, pl.BlockSpec((3, L, heads * HEAD_DIM), lambda r, j: (0, 0, per * group + j))
    return qkv3, pl.BlockSpec((3, L, heads * HEAD_DIM), lambda r, j: (0, 0, r * per + j))


def _qkv_views(name, qkv3, views=None):
    _, S, _ = qkv3.shape
    W = 512
    tm = _tile(S, 2 * ROW_TILE)
    dilated = [(g, d) for g, d in enumerate(DILATIONS) if d > 1]
    first = dilated[0][0]
    assert [g for g, _ in dilated] == list(range(first, first + len(dilated)))
    nc = W // 128
    to_views = views is None

    def body(*refs):
        scr = refs[-nc:]
        if to_views:
            src, outs = refs[0], refs[1:1 + len(dilated)]
        else:
            ins, dst = refs[:len(dilated)], refs[len(dilated) + 1]
        for k, (_, d) in enumerate(dilated):
            @pl.when(pl.program_id(1) == k)
            def _():
                for w in range(3):
                    for c in range(nc):
                        if to_views:
                            scr[c][...] = src[w, :, c * 128:(c + 1) * 128].astype(F32)
                    for r in range(d):
                        for c in range(nc):
                            at = r * W + c * 128
                            if to_views:
                                outs[k][w, :, at:at + 128] = scr[c][pl.ds(r, tm // d, stride=d), :].astype(BF)
                            else:
                                scr[c][pl.ds(r, tm // d, stride=d), :] = ins[k][w, :, at:at + 128].astype(F32)
                    for c in range(nc):
                        if not to_views:
                            dst[w, :, c * 128:(c + 1) * 128] = scr[c][...].astype(BF)

    cols = pl.BlockSpec((3, tm, W), lambda i, k: (0, i, first + k))
    rows = [pl.BlockSpec((3, tm // d, d * W), lambda i, k: (0, i, 0)) for _, d in dilated]
    shapes = [_sds((3, S // d, d * W), BF) for _, d in dilated]
    common = dict(name=name, grid=(S // tm, len(dilated)), scratch_shapes=[pltpu.VMEM((tm, 128), F32)] * nc,
                  compiler_params=_cparams(("parallel", "arbitrary")))
    if to_views:
        outs = ORDER.call(body, [qkv3], [cols], out_specs=rows, out_shape=shapes, **common)
        return {d: o for (_, d), o in zip(dilated, outs)}
    return ORDER.call(body, [views[d] for _, d in dilated] + [qkv3], rows + [pl.BlockSpec(memory_space=pl.ANY)],
                      out_specs=cols, out_shape=_sds(qkv3.shape, BF), input_output_aliases={len(dilated): 0}, **common)


def _attn_a_fwd(qkv3, group, d):
    L = qkv3.shape[1]
    S = L * d
    assert L % QB == 0
    side, heads = _chains(L)
    view, blocks_spec = _dilated_view(qkv3, group, d, heads)

    def body(qkv_ref, o_ref, lse_ref):
        coefs = _alibi_coefs(group, d, heads)

        def step(i, carry):
            chains = [(hh, _band_scores(qkv_ref, side * i + u, L, coefs[hh], hh))
                      for u in range(side) for hh in range(heads)]
            soft = []
            for hh, (q0, _, _, _, v, s, _) in chains:
                m = jnp.max(s, axis=-1, keepdims=True)
                p = jnp.exp(s - m)
                den = jnp.sum(p, axis=-1, keepdims=True)
                soft.append((hh, q0, (p / den).astype(BF), v, m + jnp.log(den)))
            for hh, q0, pn, v, lse in soft:
                lanes = pl.ds(hh * HEAD_DIM, HEAD_DIM)
                o_ref[pl.ds(q0, QB), lanes] = jnp.dot(pn, v, preferred_element_type=F32)
                lse_ref[pl.ds(q0, QB), lanes] = jnp.broadcast_to(lse, (QB, HEAD_DIM))
            return carry

        lax.fori_loop(0, L // QB // side, step, 0)

    per = 4 // heads
    out = pl.BlockSpec((L, heads * HEAD_DIM), lambda r, j: (0, r * per + j))
    o, lse = ORDER.call(
        body, [view], [blocks_spec],
        name=f"attn_a_fwd_d{d}", grid=(d, per),
        out_specs=[out, out],
        out_shape=[_sds((L, d * 512), F32), _sds((L, d * 512), F32)],
        compiler_params=_cparams(("parallel", "parallel")),
    )
    return o, lse


def _dilated_rows(name, arrays):
    S, W = arrays[0].shape
    tm = _tile(S, 2 * ROW_TILE)
    ds_ = [d for d in DILATIONS if d > 1]
    n = len(arrays)

    def body(*refs):
        nc = W // 128
        ins, outs, scr = refs[:n], refs[n:-nc], refs[-nc:]
        for a, src in enumerate(ins):
            for c in range(nc):
                scr[c][...] = src[:, c * 128:(c + 1) * 128].astype(F32)
            for k, d in enumerate(ds_):
                dst = outs[a * len(ds_) + k]
                for r in range(d):
                    for c in range(nc):
                        at = r * W + c * 128
                        dst[:, at:at + 128] = scr[c][pl.ds(r, tm // d, stride=d), :].astype(dst.dtype)

    row = pl.BlockSpec((tm, W), lambda i: (i, 0))
    out_specs, out_shape = [], []
    for a in arrays:
        for d in ds_:
            out_specs.append(pl.BlockSpec((tm // d, d * W), lambda i: (i, 0)))
            out_shape.append(_sds((S // d, d * W), a.dtype))
    outs = ORDER.call(body, list(arrays), [row] * n, name=name, grid=(S // tm,), out_specs=out_specs,
                      out_shape=out_shape, scratch_shapes=[pltpu.VMEM((tm, 128), F32)] * (W // 128),
                      compiler_params=_cparams(("parallel",)))
    return [{d: outs[a * len(ds_) + k] for k, d in enumerate(ds_)} for a in range(n)]


def _attn_a_combine(os_, lses):
    W = 512
    S = os_[0].shape[0] * DILATIONS[0]
    tm = _tile(S, ROW_TILE)
    nc = W // 128
    dilated = [g for g, d in enumerate(DILATIONS) if d > 1]

    def body(o0, o1, o2, l0, l1, l2, y_ref, lj_ref, *scr):
        def token_order(src, g, slot):
            d = DILATIONS[g]
            if d == 1:
                return src[...]
            bufs = scr[slot * nc:(slot + 1) * nc]
            for r in range(d):
                for c in range(nc):
                    at = r * W + c * 128
                    bufs[c][pl.ds(r, tm // d, stride=d), :] = src[:, at:at + 128]
            return jnp.concatenate([buf[...] for buf in bufs], axis=1)

        slots = {g: k for k, g in enumerate(dilated)}
        ls = [token_order(l, g, slots.get(g, 0)) for g, l in enumerate((l0, l1, l2))]
        os_tok = [token_order(o, g, len(dilated) + slots.get(g, 0)) for g, o in enumerate((o0, o1, o2))]
        m = jnp.maximum(jnp.maximum(ls[0], ls[1]), ls[2])
        es = [jnp.exp(l - m) for l in ls]
        den = es[0] + es[1] + es[2]
        y = (es[0] / den) * os_tok[0] + (es[1] / den) * os_tok[1] + (es[2] / den) * os_tok[2]
        y_ref[...] = y.astype(BF)
        lj_ref[...] = m + jnp.log(den)

    row = pl.BlockSpec((tm, W), lambda i: (i, 0))
    views = [pl.BlockSpec((tm // d, d * W), lambda i: (i, 0)) for d in DILATIONS]
    return ORDER.call(
        body, [*os_, *lses], views + views, name="attn_a_combine", grid=(S // tm,), out_specs=[row, row],
        out_shape=[_sds((S, W), BF), _sds((S, W), F32)],
        scratch_shapes=[pltpu.VMEM((tm, 128), F32)] * (2 * len(dilated) * nc),
        compiler_params=_cparams(("parallel",)),
    )


def _attn_a_bwd(qkv3, dy, y, lj, dqkv3, group, d):
    L = qkv3.shape[1]
    S = L * d
    side, heads = _chains(L)
    view, blocks_spec = _dilated_view(qkv3, group, d, heads)

    def body(qkv_ref, dy_ref, y_ref, lj_ref, *rest):
        out_ref, dk_acc, dv_acc = rest[-3:]
        coefs = _alibi_coefs(group, d, heads)
        dk_acc[...] = jnp.zeros_like(dk_acc)
        dv_acc[...] = jnp.zeros_like(dv_acc)

        def step(i, carry):
            chains = [(pl.ds(hh * HEAD_DIM, HEAD_DIM), _band_scores(qkv_ref, side * i + u, L, coefs[hh], hh))
                      for u in range(side) for hh in range(heads)]
            dys = [dy_ref[pl.ds(c[0], QB), lanes] for lanes, c in chains]
            dps = [lax.dot_general(dyv, c[4], (NT, ((), ())), preferred_element_type=F32)
                   for dyv, (_, c) in zip(dys, chains)]
            grads = []
            for (lanes, (q0, ks, q, k, v, s, valid)), dyv, dp in zip(chains, dys, dps):
                rows = pl.ds(q0, QB)
                delta = jnp.sum(dyv.astype(F32) * y_ref[rows, lanes].astype(F32), axis=-1, keepdims=True)
                p = jnp.where(valid, jnp.exp(s - jnp.tile(lj_ref[rows, lanes], (1, _key_rows(L) // HEAD_DIM))), 0.0)
                grads.append(((p * (dp - delta)).astype(BF), p.astype(BF)))
            for (lanes, (q0, ks, q, k, v, s, valid)), dyv, (ds, pb) in zip(chains, dys, grads):
                out_ref[0, pl.ds(q0, QB), lanes] = (jnp.dot(ds, k, preferred_element_type=F32) * SCALE).astype(BF)
                keys = pl.ds(ks, _key_rows(L))
                dk_acc[keys, lanes] += lax.dot_general(ds, q, (TN, ((), ())), preferred_element_type=F32) * SCALE
                dv_acc[keys, lanes] += lax.dot_general(pb, dyv, (TN, ((), ())), preferred_element_type=F32)
            return carry

        lax.fori_loop(0, L // QB // side, step, 0)
        out_ref[1] = dk_acc[...].astype(BF)
        out_ref[2] = dv_acc[...].astype(BF)

    per = 4 // heads
    width = heads * HEAD_DIM
    row = pl.BlockSpec((L, width), lambda r, j: (0, r * per + j))
    operands = [view, dy, y, lj]
    scratch = [pltpu.VMEM((L, width), F32), pltpu.VMEM((L, width), F32)]
    if d == 1:
        return ORDER.call(
            body, operands + [dqkv3], [blocks_spec, row, row, row, pl.BlockSpec(memory_space=pl.ANY)],
            name=f"attn_a_bwd_d{d}", grid=(d, per), out_specs=blocks_spec, out_shape=_sds((3, S, QKV_W), BF),
            scratch_shapes=scratch, input_output_aliases={4: 0}, compiler_params=_cparams(("parallel", "parallel")))
    return ORDER.call(
        body, operands, [blocks_spec, row, row, row], name=f"attn_a_bwd_d{d}", grid=(d, per),
        out_specs=blocks_spec, out_shape=_sds((3, L, d * 512), BF),
        scratch_shapes=scratch, compiler_params=_cparams(("parallel", "parallel")))


def _toeplitz_onehot():
    oh = np.zeros((64, GRID_W, 128), np.float32)
    for qc in range(GRID_W):
        for m in range(128):
            kc = m % GRID_W
            dc = int(np.clip(kc - qc, -(NA_COLS - 1), NA_COLS - 1)) + NA_COLS - 1
            oh[(m // GRID_W) * 32 + dc, qc, m] = 1.0
    return oh.reshape(64, GRID_W * 128)


def _nbr_scores(qkv_ref, e2_ref, r, rows, ok):
    rs = jnp.clip(r - NA_ROWS // 2, 0, rows - NA_ROWS)
    q0 = pl.multiple_of(r * GRID_W, GRID_W)
    k0 = pl.multiple_of(rs * GRID_W, GRID_W)
    q = qkv_ref[0, pl.ds(q0, GRID_W), :]
    k = qkv_ref[1, pl.ds(k0, NA_ROWS * GRID_W), :]
    v = qkv_ref[2, pl.ds(k0, NA_ROWS * GRID_W), :]
    s = lax.dot_general(q, k, (NT, ((), ())), preferred_element_type=F32) * SCALE
    first = rs - r + NA_ROWS - 1
    bias = jnp.concatenate([e2_ref[first + 2 * pair] for pair in range(NA_ROWS // 2)], axis=1)
    s = jnp.where(ok, s + bias, NEG)
    return q0, k0, first, q, k, v, s


def _nbr_col_ok():
    qc = lax.broadcasted_iota(jnp.int32, (GRID_W, NA_ROWS * GRID_W), 0)
    kc = lax.broadcasted_iota(jnp.int32, (GRID_W, NA_ROWS * GRID_W), 1) % GRID_W
    cs = jnp.clip(qc - NA_COLS // 2, 0, GRID_W - NA_COLS)
    return (kc >= cs) & (kc < cs + NA_COLS)


def _attn_b_fwd(qkv3, e2):
    _, S, _ = qkv3.shape
    rows = S // GRID_W
    assert rows >= NA_ROWS

    def body(qkv_ref, e2_ref, o_ref, lse_ref):
        ok = _nbr_col_ok()

        def step(i, carry):
            blocks = [_nbr_scores(qkv_ref, e2_ref, NBR_SIDE * i + u, rows, ok) for u in range(NBR_SIDE)]
            soft = []
            for q0, _, _, _, _, v, s in blocks:
                m = jnp.max(s, axis=-1, keepdims=True)
                p = jnp.exp(s - m)
                den = jnp.sum(p, axis=-1, keepdims=True)
                soft.append((q0, (p / den).astype(BF), v, m + jnp.log(den)))
            for q0, pn, v, lse in soft:
                o_ref[pl.ds(q0, GRID_W), :] = jnp.dot(pn, v, preferred_element_type=F32).astype(BF)
                lse_ref[pl.ds(q0, GRID_W), :] = jnp.broadcast_to(lse, (GRID_W, HEAD_DIM))
            return carry

        lax.fori_loop(0, rows // NBR_SIDE, step, 0)

    out = pl.BlockSpec((S, HEAD_DIM), lambda h: (0, h))
    return ORDER.call(
        body, [qkv3, e2],
        [pl.BlockSpec((3, S, HEAD_DIM), lambda h: (0, 0, N_HEADS_A + h)),
         pl.BlockSpec((None, RPB_ROWS - 1, GRID_W, 128), lambda h: (h, 0, 0, 0))],
        name="attn_b_fwd", grid=(4,),
        out_specs=[out, out], out_shape=[_sds((S, 512), BF), _sds((S, 512), F32)],
        compiler_params=_cparams(("parallel",)),
    )


def _attn_b_bwd(qkv3, e2, dy, y, lse, dqkv3):
    _, S, _ = qkv3.shape
    rows = S // GRID_W
    nk = NA_ROWS * GRID_W

    def body(qkv_ref, e2_ref, dy_ref, y_ref, lse_ref, _, out_ref, de2_ref, dk_acc, dv_acc):
        ok = _nbr_col_ok()
        dk_acc[...] = jnp.zeros_like(dk_acc)
        dv_acc[...] = jnp.zeros_like(dv_acc)
        de2_ref[...] = jnp.zeros_like(de2_ref)

        def step(i, carry):
            blocks = [_nbr_scores(qkv_ref, e2_ref, NBR_SIDE * i + u, rows, ok) for u in range(NBR_SIDE)]
            dys = [dy_ref[pl.ds(b[0], GRID_W), :] for b in blocks]
            dps = [lax.dot_general(dyv, b[5], (NT, ((), ())), preferred_element_type=F32) for dyv, b in zip(dys, blocks)]
            grads = []
            for (q0, k0, first, q, k, v, s), dyv, dp in zip(blocks, dys, dps):
                qrows = pl.ds(q0, GRID_W)
                delta = jnp.sum(dyv.astype(F32) * y_ref[qrows, :].astype(F32), axis=-1, keepdims=True)
                p = jnp.where(ok, jnp.exp(s - jnp.tile(lse_ref[qrows, :], (1, nk // HEAD_DIM))), 0.0)
                ds = p * (dp - delta)
                for pair in range(NA_ROWS // 2):
                    de2_ref[first + 2 * pair] += ds[:, pair * 128:(pair + 1) * 128]
                grads.append((ds.astype(BF), p.astype(BF)))
            for (q0, k0, first, q, k, v, s), dyv, (dsb, pb) in zip(blocks, dys, grads):
                out_ref[0, pl.ds(q0, GRID_W), :] = (jnp.dot(dsb, k, preferred_element_type=F32) * SCALE).astype(BF)
                keys = pl.ds(k0, nk)
                dk_acc[keys, :] += lax.dot_general(dsb, q, (TN, ((), ())), preferred_element_type=F32) * SCALE
                dv_acc[keys, :] += lax.dot_general(pb, dyv, (TN, ((), ())), preferred_element_type=F32)
            return carry

        lax.fori_loop(0, rows // NBR_SIDE, step, 0)
        out_ref[1] = dk_acc[...].astype(BF)
        out_ref[2] = dv_acc[...].astype(BF)

    heads = pl.BlockSpec((3, S, HEAD_DIM), lambda h: (0, 0, N_HEADS_A + h))
    row = pl.BlockSpec((S, HEAD_DIM), lambda h: (0, h))
    table = pl.BlockSpec((None, RPB_ROWS - 1, GRID_W, 128), lambda h: (h, 0, 0, 0))
    return ORDER.call(
        body, [qkv3, e2, dy, y, lse, dqkv3],
        [heads, table, row, row, row, pl.BlockSpec(memory_space=pl.ANY)], name="attn_b_bwd", grid=(4,),
        out_specs=[heads, table],
        out_shape=[_sds((3, S, QKV_W), BF), _sds((4, RPB_ROWS - 1, GRID_W, 128), F32)],
        scratch_shapes=[pltpu.VMEM((S, HEAD_DIM), F32), pltpu.VMEM((S, HEAD_DIM), F32)],
        input_output_aliases={5: 0},
        compiler_params=_cparams(("parallel",)), chain_output=1,
    )


def _rpb_to_table(rpb):
    pad = jnp.pad(rpb, ((0, 0), (0, 0), (0, 1)))
    pairs = jnp.concatenate([pad[:, :-1], pad[:, 1:]], axis=-1).reshape(4 * (RPB_ROWS - 1), 64)
    onehot = jnp.asarray(_toeplitz_onehot())
    n = onehot.shape[1]
    tn = 2048
    full = lambda i, j, k: (0, 0)
    (e2,) = _matmul("rpb_table", pairs, onehot, pl.BlockSpec(pairs.shape, full),
                    pl.BlockSpec((64, tn), lambda i, j, k: (0, j)), NN, (1, n // tn, 1), (pairs.shape[0], tn), [],
                    [(_sds((pairs.shape[0], n), F32), pl.BlockSpec((pairs.shape[0], tn), lambda i, j, k: (0, j)))],
                    _store(F32), precision=lax.Precision.HIGHEST)
    return e2.reshape(4, RPB_ROWS - 1, GRID_W, 128)


def _table_grad_to_rpb(de2):
    onehot = jnp.asarray(_toeplitz_onehot())
    n = onehot.shape[1]
    flat = de2.reshape(4 * (RPB_ROWS - 1), n)
    tk = 2048
    (dpairs,) = _matmul("rpb_table_grad", flat, onehot, pl.BlockSpec((flat.shape[0], tk), lambda i, j, k: (0, k)),
                        pl.BlockSpec((64, tk), lambda i, j, k: (0, k)), NT, (1, 1, n // tk), (flat.shape[0], 64), [],
                        [(_sds((flat.shape[0], 64), F32), pl.BlockSpec((flat.shape[0], 64), lambda i, j, k: (0, 0)))],
                        _store(F32), precision=lax.Precision.HIGHEST)
    dpairs = dpairs.reshape(4, RPB_ROWS - 1, 64)
    zero = jnp.zeros((4, 1, RPB_COLS), F32)
    return (jnp.concatenate([dpairs[:, :, :RPB_COLS], zero], axis=1)
            + jnp.concatenate([zero, dpairs[:, :, 32:32 + RPB_COLS]], axis=1))


HBM = pl.BlockSpec(memory_space=pl.ANY)


def _place():
    x, y, c = lax.axis_index("x"), lax.axis_index("y"), lax.axis_index("c")
    chips = [(1 - x, y), (x, 1 - y), (1 - x, 1 - y)]
    return x, y, c, chips


def _remote(src, dst, send_sem, recv_sem, to):
    return pltpu.make_async_remote_copy(src_ref=src, dst_ref=dst, send_sem=send_sem, recv_sem=recv_sem,
                                        device_id=to, device_id_type=MESH)


def _place_shard(name, w, me, plain=False):
    R, C = w.shape
    tr = _tile(R, 256)

    def body(me_ref, w_ref, *o_refs):
        for o_ref in o_refs:
            o_ref[...] = w_ref[...].astype(BF)

    row = pl.BlockSpec((tr, C), lambda i, mr: (i, 0))
    placed = pl.BlockSpec((None, tr, C), lambda i, mr: (mr[0], i, 0))
    return ORDER.call(
        body, [w], [row], prefetch=(me,), name=name, grid=(R // tr,),
        out_specs=[placed, row] if plain else [placed],
        out_shape=[_sds((N_CHIPS, R, C), BF)] + ([_sds((R, C), BF)] if plain else []),
        compiler_params=_cparams(("parallel",)),
    )


SEM = pl.BlockSpec(memory_space=pltpu.SEMAPHORE)
IN_HBM = pl.BlockSpec(memory_space=pltpu.HBM)
DATAFLOW = pltpu.SideEffectType.DATAFLOW_SIDE_EFFECTING


def _in_hbm(a):
    return pltpu.with_memory_space_constraint(a, pltpu.HBM)


def _copy_start(name, bufs, copies, n_copies, earlier=None):
    n = len(bufs)
    after = None if any(b is ORDER.last for b in bufs) else ORDER.last
    n_extra = (2 if earlier is not None else 0) + (1 if after is not None else 0)

    def body(*refs):
        ins = refs[:n]
        if earlier is not None:
            for k, (src, dst, to) in enumerate(earlier[0](ins)):
                cp = _remote(src, dst, refs[n].at[k], refs[n + 1].at[k], to)
                cp.wait_send()
                cp.wait_recv()
        send_sems, recv_sems = refs[n + n_extra], refs[n + n_extra + 1]
        for k, (src, dst, to) in enumerate(copies(ins)):
            _remote(src, dst, send_sems.at[k], recv_sems.at[k], to).start()
        refs[-1][...] = jnp.zeros((8, 128), F32)

    operands = [_in_hbm(b) for b in bufs]
    in_specs = [IN_HBM] * n
    if earlier is not None:
        operands += [earlier[1], earlier[2]]
        in_specs += [SEM, SEM]
    if after is not None:
        operands.append(after)
        in_specs.append(HBM)
    outs = pl.pallas_call(
        body, name=name,
        out_shape=(pltpu.SemaphoreType.DMA((n_copies,)), pltpu.SemaphoreType.DMA((n_copies,)),
                   *[pltpu.HBM(b.shape, b.dtype) for b in bufs], _sds((8, 128), F32)),
        in_specs=in_specs,
        out_specs=(SEM, SEM, *[IN_HBM] * n, pl.BlockSpec(memory_space=pltpu.VMEM)),
        input_output_aliases={i: 2 + i for i in range(n)},
        compiler_params=pltpu.CompilerParams(has_side_effects=DATAFLOW),
    )(*operands)
    ORDER.last = outs[-1]
    return outs[0], outs[1], list(outs[2:2 + n])


def _copy_wait(name, bufs, copies, send_sems, recv_sems):
    n = len(bufs)
    after = ORDER.last

    def body(*refs):
        ins = refs[:n]
        for k, (src, dst, to) in enumerate(copies(ins)):
            cp = _remote(src, dst, refs[n].at[k], refs[n + 1].at[k], to)
            cp.wait_send()
            cp.wait_recv()

    outs = list(pl.pallas_call(
        body, name=name,
        out_shape=tuple(pltpu.HBM(b.shape, b.dtype) for b in bufs),
        in_specs=[IN_HBM] * n + [SEM, SEM, HBM], out_specs=tuple([IN_HBM] * n),
        input_output_aliases={i: i for i in range(n)},
        compiler_params=pltpu.CompilerParams(has_side_effects=DATAFLOW),
    )(*bufs, send_sems, recv_sems, after))
    ORDER.last = outs[0]
    return outs


def _gather_hop1(bufs):
    x, y, c, chips = _place()
    out = []
    for b in bufs:
        half = b.shape[1] // 2
        mine = b.at[2 * x + y, pl.ds(c * half, half), :]
        out += [(mine, mine, (*chip, c)) for chip in chips]
    return out


def _gather_hop2(bufs):
    x, y, c, chips = _place()
    out = []
    for b in bufs:
        half = b.shape[1] // 2
        for chip in chips:
            landed = b.at[2 * chip[0] + chip[1], pl.ds(c * half, half), :]
            out.append((landed, landed, (x, y, 1 - c)))
    return out


def _swap_copies(bufs):
    x, y, c, _ = _place()
    n = len(bufs) // 2
    out = []
    for p, land in zip(bufs[:n], bufs[n:]):
        half = p.shape[1] // 2
        out.append((p.at[:, pl.ds((1 - c) * half, half), :], land, (x, y, 1 - c)))
    return out


def _scatter_copies(bufs):
    _, _, c, chips = _place()
    n = len(bufs) // 2
    out = []
    for s_, land in zip(bufs[:n], bufs[n:]):
        out += [(s_.at[2 * chip[0] + chip[1]], land.at[j], (*chip, c)) for j, chip in enumerate(chips)]
    return out


def _join_copies(bufs):
    x, y, c, _ = _place()
    out = []
    for b in bufs:
        half = b.shape[0] // 2
        mine = b.at[pl.ds(c * half, half), :]
        out.append((mine, mine, (x, y, 1 - c)))
    return out


def _gather_small(vec):
    m_per, n = vec.shape

    def body(x_ref, out_ref, send_sems, recv_sems, local_sem):
        x, y, c, chips = _place()
        me, sibling = (x, y, c), (x, y, 1 - c)

        def rows(px, py, pc):
            return out_ref.at[pl.ds((4 * px + 2 * py + pc) * m_per, m_per), :]

        def copy(k, block, to, src=None):
            return _remote(rows(*block) if src is None else src, rows(*block), send_sems.at[k], recv_sems.at[k], to)

        mine = pltpu.make_async_copy(x_ref, rows(*me), local_sem)
        mine.start()
        first = [copy(0, me, sibling, src=x_ref)]
        first += [copy(1 + j, me, (*chip, c), src=x_ref) for j, chip in enumerate(chips)]
        for cp in first:
            cp.start()
        passed = [copy(4 + j, (*chip, c), sibling) for j, chip in enumerate(chips)]
        for j, chip in enumerate(chips):
            copy(1 + j, (*chip, c), me).wait_recv()
            passed[j].start()
        copy(0, sibling, me).wait_recv()
        for j, chip in enumerate(chips):
            copy(4 + j, (*chip, 1 - c), me).wait_recv()
        for cp in first + passed:
            cp.wait_send()
        mine.wait()

    return ORDER.call(
        body, [vec], [pl.BlockSpec(memory_space=pltpu.VMEM)], name="gather_small_grads",
        out_shape=_sds((8 * m_per, n), vec.dtype), out_specs=pl.BlockSpec(memory_space=pltpu.VMEM),
        scratch_shapes=[pltpu.SemaphoreType.DMA((7,)), pltpu.SemaphoreType.DMA((7,)), pltpu.SemaphoreType.DMA],
    )


def _add_sibling(name, partial, received, c):
    _, R, C = partial.shape
    half = R // 2
    tr = _tile(half, 256)
    nb = half // tr

    def body(c_ref, p_ref, r_ref, o_ref):
        o_ref[...] = (p_ref[...].astype(F32) + r_ref[...].astype(F32)).astype(BF)

    return ORDER.call(
        body, [partial, received],
        [pl.BlockSpec((None, tr, C), lambda j, i, cr: (j, cr[0] * nb + i, 0)),
         pl.BlockSpec((None, tr, C), lambda j, i, cr: (j, i, 0))],
        prefetch=(c,), name=name, grid=(N_CHIPS, nb),
        out_specs=pl.BlockSpec((None, tr, C), lambda j, i, cr: (j, i, 0)),
        out_shape=_sds((N_CHIPS, half, C), BF), compiler_params=_cparams(("parallel", "parallel")),
    )


def _add_chips(name, sums, received, me_c):
    _, half, C = sums.shape
    tr = _tile(half, 256)
    nb = half // tr

    def body(mc_ref, s_ref, r_ref, o_ref):
        acc = s_ref[...].astype(F32)
        for j in range(3):
            acc = acc + r_ref[j].astype(F32)
        o_ref[...] = acc

    return ORDER.call(
        body, [sums, received],
        [pl.BlockSpec((None, tr, C), lambda i, mc: (mc[0], i, 0)),
         pl.BlockSpec((3, tr, C), lambda i, mc: (0, i, 0))],
        prefetch=(me_c,), name=name, grid=(nb,),
        out_specs=pl.BlockSpec((tr, C), lambda i, mc: (mc[1] * nb + i, 0)),
        out_shape=_sds((2 * half, C), F32), compiler_params=_cparams(("parallel",)),
    )


def _adamw_math(w, g, m, v):
    m = ADAM_B1 * m + (1.0 - ADAM_B1) * g
    v = ADAM_B2 * v + (1.0 - ADAM_B2) * (g * g)
    m_hat = m / (1.0 - ADAM_B1 ** ADAM_STEP)
    v_hat = v / (1.0 - ADAM_B2 ** ADAM_STEP)
    delta = -ADAM_LR * (m_hat / (jnp.sqrt(v_hat) + ADAM_EPS) + ADAM_WD * w)
    return delta, m, v


def _adamw(name, w, g, m, v):
    R, C = w.shape
    tr = _tile(R, 256)

    def body(w_ref, g_ref, m_ref, v_ref, go_ref, d_ref, mo_ref, vo_ref):
        gv = g_ref[...]
        go_ref[...] = gv
        d_ref[...], mo_ref[...], vo_ref[...] = _adamw_math(w_ref[...], gv, m_ref[...], v_ref[...])

    row = pl.BlockSpec((tr, C), lambda i: (i, 0))
    return ORDER.call(
        body, [w, g, m, v], [row] * 4, name=name, grid=(R // tr,), out_specs=[row] * 4,
        out_shape=[_sds((R, C), F32)] * 4, compiler_params=_cparams(("parallel",)), chain_output=1,
    )


def _adamw_small(gathered, w, m, v):
    rows, n = w.shape

    def body(ga_ref, w_ref, m_ref, v_ref, go_ref, d_ref, mo_ref, vo_ref):
        g = ga_ref[pl.ds(0, rows), :]
        for dev in range(1, 8):
            g = g + ga_ref[pl.ds(dev * rows, rows), :]
        go_ref[...] = g
        d_ref[...], mo_ref[...], vo_ref[...] = _adamw_math(w_ref[...], g, m_ref[...], v_ref[...])

    whole = pl.BlockSpec(memory_space=pltpu.VMEM)
    return ORDER.call(
        body, [gathered, w, m, v], [whole] * 4, name="adamw_small", out_specs=[whole] * 4,
        out_shape=[_sds((rows, n), F32)] * 4, compiler_params=_cparams(), chain_output=1,
    )


def _proj_merge(y_a, y_b, gpa, gpb, g3):
    S, K = y_a.shape
    _, _, Nq = gpa.shape
    D = N_CHIPS * Nq
    tm, tn = _tile(S, 1024), _tile(Nq, 512)
    q = Nq // tn

    def body(ya_ref, yb_ref, wa_ref, wb_ref, g_ref, merged_ref, c_ref):
        pa = jnp.dot(ya_ref[...], wa_ref[...], preferred_element_type=F32)
        pb = jnp.dot(yb_ref[...], wb_ref[...], preferred_element_type=F32)
        g = g_ref[...].astype(F32)
        merged_ref[...] = (g[0] * pa + g[1] * pb).astype(BF)
        c_ref[0] = (pa * g[0] * (1.0 - g[0])).astype(BF)
        c_ref[1] = (pb * g[1] * (1.0 - g[1])).astype(BF)

    rows = pl.BlockSpec((tm, K), lambda i, j: (i, 0))
    weight = pl.BlockSpec((None, K, tn), lambda i, j: (j // q, 0, j % q))
    pair = pl.BlockSpec((2, tm, tn), lambda i, j: (0, i, j))
    return ORDER.call(
        body, [y_a, y_b, gpa, gpb, g3], [rows, rows, weight, weight, pair], name="proj_merge",
        grid=(S // tm, N_CHIPS * q), out_specs=[pl.BlockSpec((tm, tn), lambda i, j: (i, j)), pair],
        out_shape=[_sds((S, D), BF), _sds((2, S, D), BF)], compiler_params=_cparams(("parallel", "parallel")))


def _out_proj_dx(dx1b, wout, g3, c3, gpa, gpb):
    S, D = dx1b.shape
    _, K, Nq = gpa.shape
    tm, tn = _tile(S, 1024), Nq
    nj = D // tn

    def body(a_ref, w_ref, g_ref, c_ref, wa_ref, wb_ref, dpa_ref, dpb_ref, dg_ref, db_ref, dya_ref, dyb_ref,
             acc_a, acc_b):
        j = pl.program_id(1)
        dm = lax.dot_general(a_ref[...], w_ref[...], (NT, ((), ())), preferred_element_type=F32)
        g, c = g_ref[...].astype(F32), c_ref[...].astype(F32)
        dpa, dpb = (dm * g[0]).astype(BF), (dm * g[1]).astype(BF)
        dpa_ref[...] = dpa
        dpb_ref[...] = dpb
        dga, dgb = dm * c[0], dm * c[1]
        dg_ref[0] = dga.astype(BF)
        dg_ref[1] = dgb.astype(BF)
        db_ref[...] = jnp.concatenate([jnp.sum(dga, axis=0, keepdims=True), jnp.sum(dgb, axis=0, keepdims=True)], 0)
        ya = lax.dot_general(dpa, wa_ref[...], (NT, ((), ())), preferred_element_type=F32)
        yb = lax.dot_general(dpb, wb_ref[...], (NT, ((), ())), preferred_element_type=F32)

        @pl.when(j == 0)
        def _():
            acc_a[...] = ya
            acc_b[...] = yb

        @pl.when(j > 0)
        def _():
            acc_a[...] += ya
            acc_b[...] += yb

        @pl.when(j == nj - 1)
        def _():
            dya_ref[...] = acc_a[...].astype(BF)
            dyb_ref[...] = acc_b[...].astype(BF)

    tile = pl.BlockSpec((tm, tn), lambda i, j: (i, j))
    pair = pl.BlockSpec((2, tm, tn), lambda i, j: (0, i, j))
    shard = pl.BlockSpec((None, K, tn), lambda i, j: (j, 0, 0))
    rows = pl.BlockSpec((tm, K), lambda i, j: (i, 0))
    return ORDER.call(
        body, [dx1b, wout, g3, c3, gpa, gpb],
        [pl.BlockSpec((tm, D), lambda i, j: (i, 0)), pl.BlockSpec((tn, D), lambda i, j: (j, 0)), pair, pair, shard, shard],
        name="out_proj_dx", grid=(S // tm, nj),
        out_specs=[tile, tile, pair, pl.BlockSpec((None, 2, tn), lambda i, j: (i, 0, j)), rows, rows],
        out_shape=[_sds((S, D), BF), _sds((S, D), BF), _sds((2, S, D), BF), _sds((S // tm, 2, D), F32),
                   _sds((S, K), BF), _sds((S, K), BF)],
        scratch_shapes=[pltpu.VMEM((tm, K), F32), pltpu.VMEM((tm, K), F32)],
        compiler_params=_cparams(("parallel", "arbitrary")))


class _Exchange:
    GATHER = (("qkv",), ("gate",), ("proj_a", "proj_b", "out"), ("up",), ("down",))
    REDUCE = {"mlp": ("down", "up"), "mix": ("out", "proj_a", "proj_b"), "in": ("qkv", "gate")}

    OWN_FIRST = ("qkv", "gate")

    def __init__(self, shards, me, c, moments):
        self.me, self.c = me, c
        self.shards, self.moments = shards, moments
        self.hop1, self.hop2, self.stage, self.grads, self.own, self.updates = {}, {}, {}, {}, {}, {}
        for g, names in enumerate(self.GATHER):
            bufs = []
            for n in names:
                placed = _place_shard(f"place_{n}", shards[n], me, plain=n in self.OWN_FIRST)
                bufs.append(placed[0])
                if n in self.OWN_FIRST:
                    self.own[n] = placed[1]
            self.hop1[g] = _copy_start(f"gather{g}_start", bufs, _gather_hop1, 3 * len(names))

    def forward(self, g):
        send, recv, thru = self.hop1.pop(g)
        self.hop2[g] = _copy_start(f"gather{g}_forward", thru, _gather_hop2, len(thru) * 3,
                                   earlier=(_gather_hop1, send, recv))

    def weights(self, g):
        send, recv, thru = self.hop2.pop(g)
        return _copy_wait(f"gather{g}_wait", thru, _gather_hop2, send, recv)

    def adamw_beside(self, name):
        def update(w, g, m, v):
            return (g,) + _adamw_math(w, g, m, v)
        return update, [self.shards[name], self.grads[name], *self.moments[name]], 4

    def reduce(self, key, partials=None):
        names = self.REDUCE[key]
        n = len(names)
        if partials is not None:
            lands = [lax.empty((p.shape[0], p.shape[1] // 2, p.shape[2]), p.dtype) for p in partials]
            self.stage[key] = ("swap",) + _copy_start(f"reduce_{key}_swap", list(partials) + lands, _swap_copies, n)
            return
        kind, send, recv, thru = self.stage.pop(key)
        if kind == "swap":
            thru = _copy_wait(f"reduce_{key}_swap_wait", thru, _swap_copies, send, recv)
            sums = [_add_sibling(f"reduce_{nm}_add_sibling", p, r, self.c)
                    for nm, p, r in zip(names, thru[:n], thru[n:])]
            lands = [lax.empty((3,) + s_.shape[1:], s_.dtype) for s_ in sums]
            self.stage[key] = ("scatter",) + _copy_start(f"reduce_{key}_scatter", sums + lands, _scatter_copies, 3 * n)
        elif kind == "scatter":
            thru = _copy_wait(f"reduce_{key}_scatter_wait", thru, _scatter_copies, send, recv)
            me_c = jnp.concatenate([self.me, self.c])
            halves = [_add_chips(f"reduce_{nm}_add_chips", s_, r, me_c)
                      for nm, s_, r in zip(names, thru[:n], thru[n:])]
            self.stage[key] = ("join",) + _copy_start(f"reduce_{key}_join", halves, _join_copies, n)
        else:
            thru = _copy_wait(f"reduce_{key}_join_wait", thru, _join_copies, send, recv)
            self.grads.update(zip(names, thru))


def _forward_backward(x, target, norm_mix, b_gate, rpb, norm_mlp, norm_final, ex):
    S, D = x.shape

    h1 = _rms_fwd("rms_mix", x, norm_mix)
    nq = QKV_W // 512
    qkv_out = (((3, S, QKV_W), BF), lambda i, T: (T // nq, i, T % nq))
    tg = _tile(ex.own["gate"].shape[1], 1024)
    ng = D // tg
    gate_out = (((2, S, D), BF), lambda i, T: (T // ng, i, T % ng))

    def gate_epilogue(acc, ex_, outs):
        outs[0][...] = jax.nn.sigmoid(acc + ex_[0][...]).astype(BF)

    qkv3 = _mm_nn_shards("qkv_own", h1, ex.own["qkv"], ex.me, True, *qkv_out, _store(BF), tm=2048)
    g3 = _mm_nn_shards("gate_own", h1, ex.own["gate"], ex.me, True, *gate_out, gate_epilogue, extras=[b_gate], tn=tg)
    ex.forward(0)
    e2 = _rpb_to_table(rpb)
    (gq,) = ex.weights(0)
    qkv3 = _mm_nn_shards("qkv", h1, gq, ex.me, False, *qkv_out, _store(BF), into=qkv3, tm=2048)

    ex.forward(1)
    outs_a = [_attn_a_fwd(qkv3, 0, DILATIONS[0])]
    (gg,) = ex.weights(1)
    g3 = _mm_nn_shards("gate", h1, gg, ex.me, False, *gate_out, gate_epilogue, extras=[b_gate], into=g3, tn=tg)

    ex.forward(2)
    qkv_views = _qkv_views("qkv_views", qkv3)
    outs_a += [_attn_a_fwd(qkv_views[d], grp, d) for grp, d in enumerate(DILATIONS) if grp > 0]
    y_a, lj = _attn_a_combine([o for o, _ in outs_a], [l for _, l in outs_a])
    y_b, lse_b = _attn_b_fwd(qkv3, e2)
    gpa, gpb, gout = ex.weights(2)
    wout = gout.reshape(D, D)
    merged, c3 = _proj_merge(y_a, y_b, gpa, gpb, g3)

    def residual_epilogue(acc, ex_, outs):
        outs[0][...] = acc + ex_[0][...]

    def nn_plain(name, a, w, res, bm=1024, bn=1024):
        M, K = a.shape
        N = w.shape[1]
        bm, bn, bk = _tile(M, bm), _tile(N, bn), _tile(K, 2048)
        t = pl.BlockSpec((bm, bn), lambda i, j, k: (i, j))
        return _matmul(name, a, w, pl.BlockSpec((bm, bk), lambda i, j, k: (i, k)),
                       pl.BlockSpec((bk, bn), lambda i, j, k: (k, j)), NN, (M // bm, N // bn, K // bk), (bm, bn),
                       [(res, t)], [(_sds((M, N), F32), t)], residual_epilogue)[0]

    ex.forward(3)
    x1 = nn_plain("out_proj", merged, wout, x, bm=512, bn=2048)
    h2 = _rms_fwd("rms_mlp", x1, norm_mlp)
    (gup,) = ex.weights(3)
    F = gup.shape[2] * N_CHIPS

    def up_epilogue(acc, ex_, outs):
        ru = jnp.maximum(acc, 0.0)
        outs[0][...] = (ru * ru).astype(BF)
        outs[1][...] = ru.astype(BF)

    tu = _tile(gup.shape[2], 2048)
    ut = pl.BlockSpec((_tile(S, 1024), tu), lambda i, j, k: (i, j))
    (act, ru), _ = _mm_nn_cols("mlp_up", h2, gup, BF, epilogue=up_epilogue, tn=tu,
                               outs=[(_sds((S, F), BF), ut), (_sds((S, F), BF), ut)])
    ex.forward(4)
    (gdown,) = ex.weights(4)
    wdown = gdown.reshape(F, D)
    x2 = nn_plain("mlp_down", act, wdown, x1)

    loss, dx2, dx2b, d_norm_final = _loss_head(x2, target, norm_final.reshape(1, D))

    def nt_rows(name, a, w, epilogue, extras, outs, bn=1024):
        M, N = a.shape
        K = w.shape[0]
        bm, bn, bk = _tile(M, 1024), _tile(K, bn), _tile(N, 2048)
        return _matmul(name, a, w, pl.BlockSpec((bm, bk), lambda i, j, k: (i, k)),
                       pl.BlockSpec((bn, bk), lambda i, j, k: (j, k)), NT, (M // bm, K // bn, N // bk), (bm, bn),
                       extras(bm, bn), outs(bm, bn), epilogue)

    def nt_cols(name, a_spec_fn, a, g, M, epilogue, extras, outs, bk, bn=1024, side=None):
        _, K, Nq = g.shape
        bm, bn, bk = _tile(M, 1024), _tile(K, bn), _tile(Nq, bk)
        q = Nq // bk
        return _matmul(name, a, g, a_spec_fn(bm, bk), pl.BlockSpec((None, bn, bk), lambda i, j, k: (k // q, j, k % q)),
                       NT, (M // bm, K // bn, N_CHIPS * q), (bm, bn), extras(bm, bn), outs(bm, bn), epilogue,
                       side=side)

    def tn_grad(name, a, a_spec_fn, b, b_spec_fn, Kin, N, out_shape, out_spec_fn, bn=1024):
        bm, bn, bk = _tile(Kin, 1024), _tile(N, bn), _tile(S, 4096)
        return _matmul(name, a, b, a_spec_fn(bk, bm), b_spec_fn(bk, bn), TN, (Kin // bm, N // bn, S // bk), (bm, bn),
                       [], [(_sds(out_shape, BF), out_spec_fn(bm, bn))], _store(BF))[0]

    plain_a = lambda bk, bm: pl.BlockSpec((bk, bm), lambda i, j, k: (k, i))
    plain_b = lambda bk, bn: pl.BlockSpec((bk, bn), lambda i, j, k: (k, j))
    plain_o = lambda bm, bn: pl.BlockSpec((bm, bn), lambda i, j, k: (i, j))
    a_rows = lambda bm, bk: pl.BlockSpec((bm, bk), lambda i, j, k: (i, k))

    def cols_o(Nq):
        def spec(bm, bn):
            q = Nq // bn
            return pl.BlockSpec((None, bm, bn), lambda i, j, k: (j // q, i, j % q))
        return spec

    def du_epilogue(acc, ex_, outs):
        outs[0][...] = (acc * (2.0 * ex_[0][...].astype(F32))).astype(BF)

    dw_down = tn_grad("mlp_down_dw", act, plain_a, dx2b, plain_b, F, D, (F, D), plain_o)
    (du,) = nt_rows("mlp_down_dx", dx2b, wdown, du_epilogue,
                    lambda bm, bn: [(ru, plain_o(bm, bn))], lambda bm, bn: [(_sds((S, F), BF), plain_o(bm, bn))],
                    bn=2048)

    fq = gup.shape[2]
    dw_up = tn_grad("mlp_up_dw", h2, plain_a, du, plain_b, D, F, (N_CHIPS, D, fq), cols_o(fq), bn=min(fq, 1024))
    ex.reduce("mlp", partials=[dw_down.reshape(N_CHIPS, F // N_CHIPS, D), dw_up])
    (dh2,) = nt_cols("mlp_up_dx", a_rows, du, gup, S, _store(F32), lambda bm, bn: [],
                     lambda bm, bn: [(_sds((S, D), F32), plain_o(bm, bn))], 1024, bn=2048)
    ex.reduce("mlp")
    dx1, dx1b, d_norm_mlp = _rms_bwd("rms_mlp_bwd", dh2, x1, norm_mlp, dx2)

    dpa, dpb, dg3, db_gate, dy_a, dy_b = _out_proj_dx(dx1b, wout, g3, c3, gpa, gpb)
    dw_out = tn_grad("out_proj_dw", merged, plain_a, dx1b, plain_b, D, D, (D, D), plain_o)

    pq = gpa.shape[2]
    dw_pa = tn_grad("proj_a_dw", y_a, plain_a, dpa, plain_b, 512, D, (N_CHIPS, 512, pq), cols_o(pq), bn=min(pq, 512))
    dw_pb = tn_grad("proj_b_dw", y_b, plain_a, dpb, plain_b, 512, D, (N_CHIPS, 512, pq), cols_o(pq), bn=min(pq, 512))
    ex.reduce("mix", partials=[dw_out.reshape(N_CHIPS, D // N_CHIPS, D), dw_pa, dw_pb])

    dqkv3 = lax.empty((3, S, QKV_W), BF)
    dqkv3 = _attn_a_bwd(qkv3, dy_a, y_a, lj, dqkv3, 0, DILATIONS[0])
    ex.reduce("mix")
    dy_views, y_views, lj_views = _dilated_rows("attn_a_bwd_rows", [dy_a, y_a, lj])
    dqkv_views = {d: _attn_a_bwd(qkv_views[d], dy_views[d], y_views[d], lj_views[d], None, grp, d)
                  for grp, d in enumerate(DILATIONS) if grp > 0}
    dqkv3 = _qkv_views("dqkv_from_views", dqkv3, dqkv_views)
    dqkv3, de2 = _attn_b_bwd(qkv3, e2, dy_b, y_b, lse_b, dqkv3)
    d_rpb = _table_grad_to_rpb(de2)

    def stacked_a(width):
        def spec(bm, bk):
            q = width // bk
            return pl.BlockSpec((None, bm, bk), lambda i, j, k: (k // q, i, k % q))
        return spec

    def stacked_b(width):
        def spec(bk, bn):
            q = width // bn
            return pl.BlockSpec((None, bk, bn), lambda i, j, k: (j // q, k, j % q))
        return spec

    ex.reduce("mlp")
    dw_qkv = tn_grad("qkv_dw", h1, plain_a, dqkv3, stacked_b(QKV_W), D, 3 * QKV_W, (N_CHIPS,) + gq.shape[1:],
                     cols_o(gq.shape[2]), bn=512)
    dw_gate = tn_grad("gate_dw", h1, plain_a, dg3, stacked_b(D), D, 2 * D, (N_CHIPS,) + gg.shape[1:],
                      cols_o(gg.shape[2]), bn=gg.shape[2])
    ex.reduce("in", partials=[dw_qkv, dw_gate])
    ex.reduce("mlp")
    dh1_q, *ex.updates["down"] = nt_cols(
        "qkv_dx", stacked_a(QKV_W), dqkv3, gq, S, _store(F32), lambda bm, bn: [],
        lambda bm, bn: [(_sds((S, D), F32), plain_o(bm, bn))], 512, bn=2048, side=ex.adamw_beside("down"))
    ex.reduce("in")
    ex.reduce("mix")

    def add_epilogue(acc, ex_, outs):
        outs[0][...] = acc + ex_[0][...]

    dh1, *ex.updates["up"] = nt_cols(
        "gate_dx", stacked_a(D), dg3, gg, S, add_epilogue, lambda bm, bn: [(dh1_q, plain_o(bm, bn))],
        lambda bm, bn: [(_sds((S, D), F32), plain_o(bm, bn))], gg.shape[2], side=ex.adamw_beside("up"))
    grad_x, _, d_norm_mix = _rms_bwd("rms_mix_bwd", dh1, x, norm_mix, dx1)
    ex.reduce("mix")

    small = [d_norm_mix, jnp.sum(db_gate, axis=0).reshape(1, 2 * D), d_rpb, d_norm_mlp, d_norm_final]
    return loss, grad_x, small


def _pack_small(parts, width):
    flat = jnp.concatenate([p.reshape(-1) for p in parts])
    return jnp.pad(flat, (0, 8 * width - flat.shape[0])).reshape(8, width)


def kernel(x, norm_mix, w_qkv, w_gate, b_gate, rpb, w_proj_a, w_proj_b, w_out, norm_mlp, w_up, w_down, norm_final, loss_target, m_norm_mix, m_w_qkv, m_w_gate, m_b_gate, m_rpb, m_w_proj_a, m_w_proj_b, m_w_out, m_norm_mlp, m_w_up, m_w_down, m_norm_final, v_norm_mix, v_w_qkv, v_w_gate, v_b_gate, v_rpb, v_w_proj_a, v_w_proj_b, v_w_out, v_norm_mlp, v_w_up, v_w_down, v_norm_final):
    names = ["qkv", "gate", "proj_a", "proj_b", "out", "up", "down"]
    big = dict(zip(names, [w_qkv[0], w_gate[0], w_proj_a[0], w_proj_b[0], w_out[0], w_up[0], w_down[0]]))
    big_m = dict(zip(names, [m_w_qkv[0], m_w_gate[0], m_w_proj_a[0], m_w_proj_b[0], m_w_out[0], m_w_up[0], m_w_down[0]]))
    big_v = dict(zip(names, [v_w_qkv[0], v_w_gate[0], v_w_proj_a[0], v_w_proj_b[0], v_w_out[0], v_w_up[0], v_w_down[0]]))

    c = lax.axis_index("c").astype(jnp.int32).reshape(1)
    me = (2 * lax.axis_index("x") + lax.axis_index("y")).astype(jnp.int32).reshape(1)
    ORDER.last = None
    ex = _Exchange(big, me, c, {n: (big_m[n], big_v[n]) for n in names})
    loss, grad_x, small = _forward_backward(x[0], loss_target[0], norm_mix, b_gate, rpb[0], norm_mlp, norm_final, ex)

    def adamw(group):
        return {n: ex.updates[n] if ex.updates.get(n) else _adamw(f"adamw_{n}", big[n], ex.grads[n], big_m[n], big_v[n])
                for n in _Exchange.REDUCE[group]}

    big_out = {**adamw("mlp"), **adamw("mix")}
    ex.reduce("in")

    small_w = [norm_mix, b_gate, rpb, norm_mlp, norm_final]
    count = sum(int(np.prod(p.shape)) for p in small_w)
    width = -(-count // (8 * 128)) * 128
    packed = _adamw_small(_gather_small(_pack_small(small, width)), _pack_small(small_w, width),
                          _pack_small([m_norm_mix, m_b_gate, m_rpb, m_norm_mlp, m_norm_final], width),
                          _pack_small([v_norm_mix, v_b_gate, v_rpb, v_norm_mlp, v_norm_final], width))
    ex.reduce("in")
    big_out.update(adamw("in"))

    def unpack(flat2d):
        flat, out, at = flat2d.reshape(-1), [], 0
        for p in small_w:
            size = int(np.prod(p.shape))
            out.append(flat[at:at + size].reshape(p.shape))
            at += size
        return out

    small_out = [unpack(a) for a in packed]

    def ordered(kind):
        sm = small_out[kind]
        bg = {n: o[kind][None] for n, o in big_out.items()}
        return [sm[0], bg["qkv"], bg["gate"], sm[1], sm[2], bg["proj_a"], bg["proj_b"], bg["out"], sm[3],
                bg["up"], bg["down"], sm[4]]

    total = lax.psum(loss[0, 0], ("x", "y", "c"))
    return (total, grad_x[None], *ordered(0), *ordered(1), *ordered(2), *ordered(3))
```

```python
import math

import numpy as np
import jax
import jax.numpy as jnp
from jax import lax
from jax.experimental import pallas as pl
from jax.experimental.pallas import tpu as pltpu

BF = jnp.bfloat16
F32 = jnp.float32
MESH = pl.DeviceIdType.MESH

HEAD_DIM = 128
N_HEADS = 16
N_HEADS_A = 12
QKV_W = N_HEADS * HEAD_DIM
DILATIONS = (1, 4, 16)
HALF_WINDOW = 64
GRID_W = 64
NA_ROWS = 8
NA_COLS = 16
RPB_ROWS = 2 * NA_ROWS - 1
RPB_COLS = 2 * NA_COLS - 1
EPS = 1e-6
NEG = -1e30
SCALE = HEAD_DIM ** -0.5

ADAM_LR = 0.001
ADAM_B1 = 0.9
ADAM_B2 = 0.999
ADAM_EPS = 1e-08
ADAM_WD = 0.01
ADAM_STEP = 10

N_CHIPS = 4
VMEM_LIMIT_BYTES = 48 * 1024 * 1024
QB = 256
NBR_SIDE = 16
ROW_TILE = 512


def _key_rows(L):
    return min(QB + 2 * HALF_WINDOW, L)


def _cparams(sem=None):
    return pltpu.CompilerParams(dimension_semantics=sem, vmem_limit_bytes=VMEM_LIMIT_BYTES)


def _tile(dim, want):
    t = min(dim, want)
    assert dim % t == 0, (dim, want)
    return t


class _ProgramOrder:
    def __init__(self):
        self.last = None

    def call(self, body, operands, in_specs, *, prefetch=(), grid=None, out_specs=None, chain_output=0, **kwargs):
        operands, in_specs = list(operands), list(in_specs)
        lead = len(prefetch) + len(operands)
        if self.last is not None and not any(op is self.last for op in operands):
            operands.append(self.last)
            in_specs.append(pl.BlockSpec(memory_space=pl.ANY))
            inner = body

            def body(*refs):
                return inner(*refs[:lead], *refs[lead + 1:])

        if prefetch:
            kwargs["grid_spec"] = pltpu.PrefetchScalarGridSpec(
                num_scalar_prefetch=len(prefetch), grid=grid, in_specs=in_specs, out_specs=out_specs)
        else:
            kwargs.update(in_specs=in_specs, out_specs=out_specs)
            if grid is not None:
                kwargs["grid"] = grid
        out = pl.pallas_call(body, **kwargs)(*prefetch, *operands)
        self.last = out[chain_output] if isinstance(out, (tuple, list)) else out
        return out


ORDER = _ProgramOrder()


NN = ((1,), (0,))
NT = ((1,), (1,))
TN = ((0,), (0,))


def _matmul(name, a, b, a_spec, b_spec, dims, grid, acc_shape, extras, outs, epilogue, precision=None,
            prefetch=(), into=None, side=None):
    n_ex, n_out, nk = len(extras), len(outs), grid[2]
    side_fn, side_in, n_side_out = side if side is not None else (None, [], 0)
    side_spec = None
    n_in = 2 + n_ex + len(side_in) + (into is not None)
    if side is not None:
        R, C = side_in[0].shape
        steps = grid[0] * grid[1] * grid[2]
        side_blocks = max(n for n in range(1, steps + 1) if R % n == 0 and (R // n) % 8 == 0)

        def side_step(*ids):
            return (ids[0] * grid[1] + ids[1]) * grid[2] + ids[2]

        side_spec = pl.BlockSpec((R // side_blocks, C),
                                 lambda *ids: (jnp.minimum(side_step(*ids), side_blocks - 1), 0))

    def body(*refs):
        refs = refs[len(prefetch):]
        a_ref, b_ref = refs[0], refs[1]
        ex_refs = refs[2:2 + n_ex]
        out_refs = refs[n_in:n_in + n_out]
        if side is not None:
            @pl.when(side_step(pl.program_id(0), pl.program_id(1), pl.program_id(2)) < side_blocks)
            def _():
                results = side_fn(*[r[...] for r in refs[2 + n_ex:2 + n_ex + len(side_in)]])
                for o_ref, value in zip(refs[n_in + n_out:n_in + n_out + n_side_out], results):
                    o_ref[...] = value

        def dot():
            return lax.dot_general(a_ref[...], b_ref[...], (dims, ((), ())),
                                   preferred_element_type=F32, precision=precision)

        if nk == 1:
            epilogue(dot(), ex_refs, out_refs)
            return
        acc_ref = refs[-1]
        k = pl.program_id(2)

        @pl.when(k == 0)
        def _():
            acc_ref[...] = dot()

        if nk > 2:
            @pl.when((k > 0) & (k < nk - 1))
            def _():
                acc_ref[...] += dot()

        @pl.when(k == nk - 1)
        def _():
            epilogue(acc_ref[...] + dot(), ex_refs, out_refs)

    operands = [a, b] + [e for e, _ in extras] + list(side_in)
    in_specs = [a_spec, b_spec] + [s for _, s in extras] + [side_spec] * len(side_in)
    kwargs = {}
    if into is not None:
        operands.append(into)
        in_specs.append(pl.BlockSpec(memory_space=pl.ANY))
        kwargs["input_output_aliases"] = {len(prefetch) + n_in - 1: 0}
    return ORDER.call(
        body, operands, in_specs, prefetch=prefetch, name=name, grid=grid,
        out_specs=[s for _, s in outs] + [side_spec] * n_side_out,
        out_shape=[sh for sh, _ in outs] + [_sds(s_.shape, F32) for s_ in side_in[:1]] * n_side_out,
        scratch_shapes=[pltpu.VMEM(acc_shape, F32)] if nk > 1 else [],
        compiler_params=_cparams(("parallel", "parallel", "arbitrary")), **kwargs,
    )


def _mm_nn_shards(name, a, w, me, own, out, out_block, epilogue, extras=(), into=None, tn=512, tm=1024):
    M, K = a.shape
    Nq = w.shape[-1]
    tm, tn = _tile(M, tm), _tile(Nq, tn)
    q = Nq // tn

    def tile(j, me_ref):
        shard = me_ref[0] if own else (me_ref[0] + 1 + j // q) % N_CHIPS
        return shard, j % q, shard * q + j % q

    if own:
        b_spec = pl.BlockSpec((K, tn), lambda i, j, k, me_ref: (0, j))
    else:
        b_spec = pl.BlockSpec((None, K, tn), lambda i, j, k, me_ref: (tile(j, me_ref)[0], 0, tile(j, me_ref)[1]))
    shape, dtype = out
    out_spec = pl.BlockSpec((None, tm, tn), lambda i, j, k, me_ref: out_block(i, tile(j, me_ref)[2]))
    ex = [(e, pl.BlockSpec((1, tn), lambda i, j, k, me_ref: (0, tile(j, me_ref)[2]))) for e in extras]
    return _matmul(name, a, w, pl.BlockSpec((tm, K), lambda i, j, k, me_ref: (i, 0)), b_spec, NN,
                   (M // tm, q if own else (N_CHIPS - 1) * q, 1), (tm, tn), ex, [(_sds(shape, dtype), out_spec)],
                   epilogue, prefetch=(me,), into=into)[0]


def _store(dtype):
    def epilogue(acc, ex, outs):
        outs[0][...] = acc.astype(dtype)
    return epilogue


def _sds(shape, dtype):
    return jax.ShapeDtypeStruct(shape, dtype)


def _mm_nn_cols(name, a, g, out_dtype, epilogue=None, extras=(), outs=None, tm=1024, tn=1024, tk=2048):
    M, K = a.shape
    _, _, Nq = g.shape
    tm, tn, tk = _tile(M, tm), _tile(Nq, tn), _tile(K, tk)
    q = Nq // tn
    grid = (M // tm, N_CHIPS * q, K // tk)
    if outs is None:
        outs = [(_sds((M, N_CHIPS * Nq), out_dtype), pl.BlockSpec((tm, tn), lambda i, j, k: (i, j)))]
    return _matmul(name, a, g, pl.BlockSpec((tm, tk), lambda i, j, k: (i, k)),
                   pl.BlockSpec((None, tk, tn), lambda i, j, k: (j // q, k, j % q)), NN, grid, (tm, tn),
                   list(extras), outs, epilogue or _store(out_dtype)), (tm, tn, tk)


def _rms_fwd(name, x, g):
    S, D = x.shape
    tm = _tile(S, ROW_TILE)

    def body(x_ref, g_ref, h_ref):
        xv = x_ref[...]
        r = lax.rsqrt(jnp.mean(xv * xv, axis=-1, keepdims=True) + EPS)
        h_ref[...] = ((xv * r) * g_ref[...]).astype(BF)

    row = pl.BlockSpec((tm, D), lambda i: (i, 0))
    return ORDER.call(
        body, [x, g], [row, pl.BlockSpec((1, D), lambda i: (0, 0))], name=name, grid=(S // tm,),
        out_specs=row, out_shape=_sds((S, D), BF), compiler_params=_cparams(("parallel",)),
    )


def _rms_bwd(name, dh, x, g, dres):
    S, D = x.shape
    tm = _tile(S, ROW_TILE // 2)

    def body(dh_ref, x_ref, g_ref, dres_ref, dx_ref, dxb_ref, dg_ref):
        xv = x_ref[...]
        r = lax.rsqrt(jnp.mean(xv * xv, axis=-1, keepdims=True) + EPS)
        n = xv * r
        dhv = dh_ref[...]
        dyg = dhv * g_ref[...]
        dx = dres_ref[...] + r * (dyg - n * jnp.mean(dyg * n, axis=-1, keepdims=True))
        dx_ref[...] = dx
        dxb_ref[...] = dx.astype(BF)

        @pl.when(pl.program_id(0) == 0)
        def _():
            dg_ref[...] = jnp.zeros_like(dg_ref)

        dg_ref[...] += jnp.sum(dhv * n, axis=0, keepdims=True)

    row = pl.BlockSpec((tm, D), lambda i: (i, 0))
    vec = pl.BlockSpec((1, D), lambda i: (0, 0))
    return ORDER.call(
        body, [dh, x, g, dres], [row, row, vec, row], name=name, grid=(S // tm,),
        out_specs=[row, row, vec],
        out_shape=[_sds((S, D), F32), _sds((S, D), BF), _sds((1, D), F32)],
        compiler_params=_cparams(("arbitrary",)),
    )


def _loss_head(x2, target, g):
    S, D = x2.shape
    tm = _tile(S, ROW_TILE)

    def body(x_ref, t_ref, g_ref, loss_ref, dx_ref, dxb_ref, dg_ref):
        xv = x_ref[...]
        gv = g_ref[...]
        r = lax.rsqrt(jnp.mean(xv * xv, axis=-1, keepdims=True) + EPS)
        n = xv * r
        e = n * gv - t_ref[...]
        dy = e * (1.0 / D)
        dyg = dy * gv
        dx = r * (dyg - n * jnp.mean(dyg * n, axis=-1, keepdims=True))
        dx_ref[...] = dx
        dxb_ref[...] = dx.astype(BF)

        @pl.when(pl.program_id(0) == 0)
        def _():
            dg_ref[...] = jnp.zeros_like(dg_ref)
            loss_ref[...] = jnp.zeros_like(loss_ref)

        dg_ref[...] += jnp.sum(dy * n, axis=0, keepdims=True)
        per_row = jnp.mean(e * e, axis=-1, keepdims=True)
        loss_ref[...] += 0.5 * jnp.sum(per_row, axis=0, keepdims=True)

    row = pl.BlockSpec((tm, D), lambda i: (i, 0))
    vec = pl.BlockSpec((1, D), lambda i: (0, 0))
    return ORDER.call(
        body, [x2, target, g], [row, row, vec], name="loss_head", grid=(S // tm,),
        out_specs=[pl.BlockSpec((1, 1), lambda i: (0, 0)), row, row, vec],
        out_shape=[_sds((1, 1), F32), _sds((S, D), F32), _sds((S, D), BF), _sds((1, D), F32)],
        compiler_params=_cparams(("arbitrary",)), chain_output=1,
    )


def _chains(L):
    side = min(8, L // QB)
    return side, max(1, 4 // side)


def _band_scores(qkv_ref, i, L, coef, head):
    KB = _key_rows(L)
    lanes = pl.ds(head * HEAD_DIM, HEAD_DIM)
    q0 = pl.multiple_of(i * QB, QB)
    ks = pl.multiple_of(jnp.clip(i * QB - HALF_WINDOW, 0, L - KB), HALF_WINDOW)
    q = qkv_ref[0, pl.ds(q0, QB), lanes]
    k = qkv_ref[1, pl.ds(ks, KB), lanes]
    v = qkv_ref[2, pl.ds(ks, KB), lanes]
    s = lax.dot_general(q, k, (NT, ((), ())), preferred_element_type=F32) * SCALE
    qpos = q0 + lax.broadcasted_iota(jnp.int32, (QB, KB), 0)
    kpos = ks + lax.broadcasted_iota(jnp.int32, (QB, KB), 1)
    rel = jnp.abs(kpos - qpos)
    valid = rel <= HALF_WINDOW
    s = jnp.where(valid, s - coef * rel.astype(F32), NEG)
    return q0, ks, q, k, v, s, valid


def _alibi_coefs(group, d, heads):
    first = 4 * group + 1 + pl.program_id(1) * heads
    scale = jnp.full((1, 1), -(8.0 / N_HEADS_A) * math.log(2.0), F32)
    return [jnp.exp(scale * (first + hh).astype(F32)) * float(d) for hh in range(heads)]


def _dilated_view(qkv3, group, d, heads):
    per = 4 // heads
    L = qkv3.shape[1]
    if d == 1:
        return qkv3, pl.BlockSpec((3, L, heads * HEAD_DIM), lambda r, j: (0, 0, per * group + j))
    return qkv3, pl.BlockSpec((3, L, heads * HEAD_DIM), lambda r, j: (0, 0, r * per + j))


def _qkv_views(name, qkv3, views=None):
    _, S, _ = qkv3.shape
    W = 512
    tm = _tile(S, 2 * ROW_TILE)
    dilated = [(g, d) for g, d in enumerate(DILATIONS) if d > 1]
    first = dilated[0][0]
    assert [g for g, _ in dilated] == list(range(first, first + len(dilated)))
    nc = W // 128
    to_views = views is None

    def body(*refs):
        scr = refs[-nc:]
        if to_views:
            src, outs = refs[0], refs[1:1 + len(dilated)]
        else:
            ins, dst = refs[:len(dilated)], refs[len(dilated) + 1]
        for k, (_, d) in enumerate(dilated):
            @pl.when(pl.program_id(1) == k)
            def _():
                for w in range(3):
                    for c in range(nc):
                        if to_views:
                            scr[c][...] = src[w, :, c * 128:(c + 1) * 128].astype(F32)
                    for r in range(d):
                        for c in range(nc):
                            at = r * W + c * 128
                            if to_views:
                                outs[k][w, :, at:at + 128] = scr[c][pl.ds(r, tm // d, stride=d), :].astype(BF)
                            else:
                                scr[c][pl.ds(r, tm // d, stride=d), :] = ins[k][w, :, at:at + 128].astype(F32)
                    for c in range(nc):
                        if not to_views:
                            dst[w, :, c * 128:(c + 1) * 128] = scr[c][...].astype(BF)

    cols = pl.BlockSpec((3, tm, W), lambda i, k: (0, i, first + k))
    rows = [pl.BlockSpec((3, tm // d, d * W), lambda i, k: (0, i, 0)) for _, d in dilated]
    shapes = [_sds((3, S // d, d * W), BF) for _, d in dilated]
    common = dict(name=name, grid=(S // tm, len(dilated)), scratch_shapes=[pltpu.VMEM((tm, 128), F32)] * nc,
                  compiler_params=_cparams(("parallel", "arbitrary")))
    if to_views:
        outs = ORDER.call(body, [qkv3], [cols], out_specs=rows, out_shape=shapes, **common)
        return {d: o for (_, d), o in zip(dilated, outs)}
    return ORDER.call(body, [views[d] for _, d in dilated] + [qkv3], rows + [pl.BlockSpec(memory_space=pl.ANY)],
                      out_specs=cols, out_shape=_sds(qkv3.shape, BF), input_output_aliases={len(dilated): 0}, **common)


def _attn_a_fwd(qkv3, group, d):
    L = qkv3.shape[1]
    S = L * d
    assert L % QB == 0
    side, heads = _chains(L)
    view, blocks_spec = _dilated_view(qkv3, group, d, heads)

    def body(qkv_ref, o_ref, lse_ref):
        coefs = _alibi_coefs(group, d, heads)

        def step(i, carry):
            chains = [(hh, _band_scores(qkv_ref, side * i + u, L, coefs[hh], hh))
                      for u in range(side) for hh in range(heads)]
            soft = []
            for hh, (q0, _, _, _, v, s, _) in chains:
                m = jnp.max(s, axis=-1, keepdims=True)
                p = jnp.exp(s - m)
                den = jnp.sum(p, axis=-1, keepdims=True)
                soft.append((hh, q0, (p / den).astype(BF), v, m + jnp.log(den)))
            for hh, q0, pn, v, lse in soft:
                lanes = pl.ds(hh * HEAD_DIM, HEAD_DIM)
                o_ref[pl.ds(q0, QB), lanes] = jnp.dot(pn, v, preferred_element_type=F32)
                lse_ref[pl.ds(q0, QB), lanes] = jnp.broadcast_to(lse, (QB, HEAD_DIM))
            return carry

        lax.fori_loop(0, L // QB // side, step, 0)

    per = 4 // heads
    out = pl.BlockSpec((L, heads * HEAD_DIM), lambda r, j: (0, r * per + j))
    o, lse = ORDER.call(
        body, [view], [blocks_spec],
        name=f"attn_a_fwd_d{d}", grid=(d, per),
        out_specs=[out, out],
        out_shape=[_sds((L, d * 512), F32), _sds((L, d * 512), F32)],
        compiler_params=_cparams(("parallel", "parallel")),
    )
    return o, lse


def _dilated_rows(name, arrays):
    S, W = arrays[0].shape
    tm = _tile(S, ROW_TILE)
    ds_ = [d for d in DILATIONS if d > 1]
    n = len(arrays)

    def body(*refs):
        nc = W // 128
        ins, outs, scr = refs[:n], refs[n:-nc], refs[-nc:]
        for a, src in enumerate(ins):
            for c in range(nc):
                scr[c][...] = src[:, c * 128:(c + 1) * 128].astype(F32)
            for k, d in enumerate(ds_):
                dst = outs[a * len(ds_) + k]
                for r in range(d):
                    for c in range(nc):
                        at = r * W + c * 128
                        dst[:, at:at + 128] = scr[c][pl.ds(r, tm // d, stride=d), :].astype(dst.dtype)

    row = pl.BlockSpec((tm, W), lambda i: (i, 0))
    out_specs, out_shape = [], []
    for a in arrays:
        for d in ds_:
            out_specs.append(pl.BlockSpec((tm // d, d * W), lambda i: (i, 0)))
            out_shape.append(_sds((S // d, d * W), a.dtype))
    outs = ORDER.call(body, list(arrays), [row] * n, name=name, grid=(S // tm,), out_specs=out_specs,
                      out_shape=out_shape, scratch_shapes=[pltpu.VMEM((tm, 128), F32)] * (W // 128),
                      compiler_params=_cparams(("parallel",)))
    return [{d: outs[a * len(ds_) + k] for k, d in enumerate(ds_)} for a in range(n)]


def _attn_a_combine(os_, lses):
    W = 512
    S = os_[0].shape[0] * DILATIONS[0]
    tm = _tile(S, ROW_TILE)
    nc = W // 128
    dilated = [g for g, d in enumerate(DILATIONS) if d > 1]

    def body(o0, o1, o2, l0, l1, l2, y_ref, lj_ref, *scr):
        def token_order(src, g, slot):
            d = DILATIONS[g]
            if d == 1:
                return src[...]
            bufs = scr[slot * nc:(slot + 1) * nc]
            for r in range(d):
                for c in range(nc):
                    at = r * W + c * 128
                    bufs[c][pl.ds(r, tm // d, stride=d), :] = src[:, at:at + 128]
            return jnp.concatenate([buf[...] for buf in bufs], axis=1)

        slots = {g: k for k, g in enumerate(dilated)}
        ls = [token_order(l, g, slots.get(g, 0)) for g, l in enumerate((l0, l1, l2))]
        os_tok = [token_order(o, g, len(dilated) + slots.get(g, 0)) for g, o in enumerate((o0, o1, o2))]
        m = jnp.maximum(jnp.maximum(ls[0], ls[1]), ls[2])
        es = [jnp.exp(l - m) for l in ls]
        den = es[0] + es[1] + es[2]
        y = (es[0] / den) * os_tok[0] + (es[1] / den) * os_tok[1] + (es[2] / den) * os_tok[2]
        y_ref[...] = y.astype(BF)
        lj_ref[...] = m + jnp.log(den)

    row = pl.BlockSpec((tm, W), lambda i: (i, 0))
    views = [pl.BlockSpec((tm // d, d * W), lambda i: (i, 0)) for d in DILATIONS]
    return ORDER.call(
        body, [*os_, *lses], views + views, name="attn_a_combine", grid=(S // tm,), out_specs=[row, row],
        out_shape=[_sds((S, W), BF), _sds((S, W), F32)],
        scratch_shapes=[pltpu.VMEM((tm, 128), F32)] * (2 * len(dilated) * nc),
        compiler_params=_cparams(("parallel",)),
    )


def _attn_a_bwd(qkv3, dy, y, lj, dqkv3, group, d):
    L = qkv3.shape[1]
    S = L * d
    side, heads = _chains(L)
    view, blocks_spec = _dilated_view(qkv3, group, d, heads)

    def body(qkv_ref, dy_ref, y_ref, lj_ref, *rest):
        out_ref, dk_acc, dv_acc = rest[-3:]
        coefs = _alibi_coefs(group, d, heads)
        dk_acc[...] = jnp.zeros_like(dk_acc)
        dv_acc[...] = jnp.zeros_like(dv_acc)

        def step(i, carry):
            chains = [(pl.ds(hh * HEAD_DIM, HEAD_DIM), _band_scores(qkv_ref, side * i + u, L, coefs[hh], hh))
                      for u in range(side) for hh in range(heads)]
            dys = [dy_ref[pl.ds(c[0], QB), lanes] for lanes, c in chains]
            dps = [lax.dot_general(dyv, c[4], (NT, ((), ())), preferred_element_type=F32)
                   for dyv, (_, c) in zip(dys, chains)]
            grads = []
            for (lanes, (q0, ks, q, k, v, s, valid)), dyv, dp in zip(chains, dys, dps):
                rows = pl.ds(q0, QB)
                delta = jnp.sum(dyv.astype(F32) * y_ref[rows, lanes].astype(F32), axis=-1, keepdims=True)
                p = jnp.where(valid, jnp.exp(s - jnp.tile(lj_ref[rows, lanes], (1, _key_rows(L) // HEAD_DIM))), 0.0)
                grads.append(((p * (dp - delta)).astype(BF), p.astype(BF)))
            for (lanes, (q0, ks, q, k, v, s, valid)), dyv, (ds, pb) in zip(chains, dys, grads):
                out_ref[0, pl.ds(q0, QB), lanes] = (jnp.dot(ds, k, preferred_element_type=F32) * SCALE).astype(BF)
                keys = pl.ds(ks, _key_rows(L))
                dk_acc[keys, lanes] += lax.dot_general(ds, q, (TN, ((), ())), preferred_element_type=F32) * SCALE
                dv_acc[keys, lanes] += lax.dot_general(pb, dyv, (TN, ((), ())), preferred_element_type=F32)
            return carry

        lax.fori_loop(0, L // QB // side, step, 0)
        out_ref[1] = dk_acc[...].astype(BF)
        out_ref[2] = dv_acc[...].astype(BF)

    per = 4 // heads
    width = heads * HEAD_DIM
    row = pl.BlockSpec((L, width), lambda r, j: (0, r * per + j))
    operands = [view, dy, y, lj]
    scratch = [pltpu.VMEM((L, width), F32), pltpu.VMEM((L, width), F32)]
    if d == 1:
        return ORDER.call(
            body, operands + [dqkv3], [blocks_spec, row, row, row, pl.BlockSpec(memory_space=pl.ANY)],
            name=f"attn_a_bwd_d{d}", grid=(d, per), out_specs=blocks_spec, out_shape=_sds((3, S, QKV_W), BF),
            scratch_shapes=scratch, input_output_aliases={4: 0}, compiler_params=_cparams(("parallel", "parallel")))
    return ORDER.call(
        body, operands, [blocks_spec, row, row, row], name=f"attn_a_bwd_d{d}", grid=(d, per),
        out_specs=blocks_spec, out_shape=_sds((3, L, d * 512), BF),
        scratch_shapes=scratch, compiler_params=_cparams(("parallel", "parallel")))


def _toeplitz_onehot():
    oh = np.zeros((64, GRID_W, 128), np.float32)
    for qc in range(GRID_W):
        for m in range(128):
            kc = m % GRID_W
            dc = int(np.clip(kc - qc, -(NA_COLS - 1), NA_COLS - 1)) + NA_COLS - 1
            oh[(m // GRID_W) * 32 + dc, qc, m] = 1.0
    return oh.reshape(64, GRID_W * 128)


def _nbr_scores(qkv_ref, e2_ref, r, rows, ok):
    rs = jnp.clip(r - NA_ROWS // 2, 0, rows - NA_ROWS)
    q0 = pl.multiple_of(r * GRID_W, GRID_W)
    k0 = pl.multiple_of(rs * GRID_W, GRID_W)
    q = qkv_ref[0, pl.ds(q0, GRID_W), :]
    k = qkv_ref[1, pl.ds(k0, NA_ROWS * GRID_W), :]
    v = qkv_ref[2, pl.ds(k0, NA_ROWS * GRID_W), :]
    s = lax.dot_general(q, k, (NT, ((), ())), preferred_element_type=F32) * SCALE
    first = rs - r + NA_ROWS - 1
    bias = jnp.concatenate([e2_ref[first + 2 * pair] for pair in range(NA_ROWS // 2)], axis=1)
    s = jnp.where(ok, s + bias, NEG)
    return q0, k0, first, q, k, v, s


def _nbr_col_ok():
    qc = lax.broadcasted_iota(jnp.int32, (GRID_W, NA_ROWS * GRID_W), 0)
    kc = lax.broadcasted_iota(jnp.int32, (GRID_W, NA_ROWS * GRID_W), 1) % GRID_W
    cs = jnp.clip(qc - NA_COLS // 2, 0, GRID_W - NA_COLS)
    return (kc >= cs) & (kc < cs + NA_COLS)


def _attn_b_fwd(qkv3, e2):
    _, S, _ = qkv3.shape
    rows = S // GRID_W
    assert rows >= NA_ROWS

    def body(qkv_ref, e2_ref, o_ref, lse_ref):
        ok = _nbr_col_ok()

        def step(i, carry):
            blocks = [_nbr_scores(qkv_ref, e2_ref, NBR_SIDE * i + u, rows, ok) for u in range(NBR_SIDE)]
            soft = []
            for q0, _, _, _, _, v, s in blocks:
                m = jnp.max(s, axis=-1, keepdims=True)
                p = jnp.exp(s - m)
                den = jnp.sum(p, axis=-1, keepdims=True)
                soft.append((q0, (p / den).astype(BF), v, m + jnp.log(den)))
            for q0, pn, v, lse in soft:
                o_ref[pl.ds(q0, GRID_W), :] = jnp.dot(pn, v, preferred_element_type=F32).astype(BF)
                lse_ref[pl.ds(q0, GRID_W), :] = jnp.broadcast_to(lse, (GRID_W, HEAD_DIM))
            return carry

        lax.fori_loop(0, rows // NBR_SIDE, step, 0)

    out = pl.BlockSpec((S, HEAD_DIM), lambda h: (0, h))
    return ORDER.call(
        body, [qkv3, e2],
        [pl.BlockSpec((3, S, HEAD_DIM), lambda h: (0, 0, N_HEADS_A + h)),
         pl.BlockSpec((None, RPB_ROWS - 1, GRID_W, 128), lambda h: (h, 0, 0, 0))],
        name="attn_b_fwd", grid=(4,),
        out_specs=[out, out], out_shape=[_sds((S, 512), BF), _sds((S, 512), F32)],
        compiler_params=_cparams(("parallel",)),
    )


def _attn_b_bwd(qkv3, e2, dy, y, lse, dqkv3):
    _, S, _ = qkv3.shape
    rows = S // GRID_W
    nk = NA_ROWS * GRID_W

    def body(qkv_ref, e2_ref, dy_ref, y_ref, lse_ref, _, out_ref, de2_ref, dk_acc, dv_acc):
        ok = _nbr_col_ok()
        dk_acc[...] = jnp.zeros_like(dk_acc)
        dv_acc[...] = jnp.zeros_like(dv_acc)
        de2_ref[...] = jnp.zeros_like(de2_ref)

        def step(i, carry):
            blocks = [_nbr_scores(qkv_ref, e2_ref, NBR_SIDE * i + u, rows, ok) for u in range(NBR_SIDE)]
            dys = [dy_ref[pl.ds(b[0], GRID_W), :] for b in blocks]
            dps = [lax.dot_general(dyv, b[5], (NT, ((), ())), preferred_element_type=F32) for dyv, b in zip(dys, blocks)]
            grads = []
            for (q0, k0, first, q, k, v, s), dyv, dp in zip(blocks, dys, dps):
                qrows = pl.ds(q0, GRID_W)
                delta = jnp.sum(dyv.astype(F32) * y_ref[qrows, :].astype(F32), axis=-1, keepdims=True)
                p = jnp.where(ok, jnp.exp(s - jnp.tile(lse_ref[qrows, :], (1, nk // HEAD_DIM))), 0.0)
                ds = p * (dp - delta)
                for pair in range(NA_ROWS // 2):
                    de2_ref[first + 2 * pair] += ds[:, pair * 128:(pair + 1) * 128]
                grads.append((ds.astype(BF), p.astype(BF)))
            for (q0, k0, first, q, k, v, s), dyv, (dsb, pb) in zip(blocks, dys, grads):
                out_ref[0, pl.ds(q0, GRID_W), :] = (jnp.dot(dsb, k, preferred_element_type=F32) * SCALE).astype(BF)
                keys = pl.ds(k0, nk)
                dk_acc[keys, :] += lax.dot_general(dsb, q, (TN, ((), ())), preferred_element_type=F32) * SCALE
                dv_acc[keys, :] += lax.dot_general(pb, dyv, (TN, ((), ())), preferred_element_type=F32)
            return carry

        lax.fori_loop(0, rows // NBR_SIDE, step, 0)
        out_ref[1] = dk_acc[...].astype(BF)
        out_ref[2] = dv_acc[...].astype(BF)

    heads = pl.BlockSpec((3, S, HEAD_DIM), lambda h: (0, 0, N_HEADS_A + h))
    row = pl.BlockSpec((S, HEAD_DIM), lambda h: (0, h))
    table = pl.BlockSpec((None, RPB_ROWS - 1, GRID_W, 128), lambda h: (h, 0, 0, 0))
    return ORDER.call(
        body, [qkv3, e2, dy, y, lse, dqkv3],
        [heads, table, row, row, row, pl.BlockSpec(memory_space=pl.ANY)], name="attn_b_bwd", grid=(4,),
        out_specs=[heads, table],
        out_shape=[_sds((3, S, QKV_W), BF), _sds((4, RPB_ROWS - 1, GRID_W, 128), F32)],
        scratch_shapes=[pltpu.VMEM((S, HEAD_DIM), F32), pltpu.VMEM((S, HEAD_DIM), F32)],
        input_output_aliases={5: 0},
        compiler_params=_cparams(("parallel",)), chain_output=1,
    )


def _rpb_to_table(rpb):
    pad = jnp.pad(rpb, ((0, 0), (0, 0), (0, 1)))
    pairs = jnp.concatenate([pad[:, :-1], pad[:, 1:]], axis=-1).reshape(4 * (RPB_ROWS - 1), 64)
    onehot = jnp.asarray(_toeplitz_onehot())
    n = onehot.shape[1]
    tn = 2048
    full = lambda i, j, k: (0, 0)
    (e2,) = _matmul("rpb_table", pairs, onehot, pl.BlockSpec(pairs.shape, full),
                    pl.BlockSpec((64, tn), lambda i, j, k: (0, j)), NN, (1, n // tn, 1), (pairs.shape[0], tn), [],
                    [(_sds((pairs.shape[0], n), F32), pl.BlockSpec((pairs.shape[0], tn), lambda i, j, k: (0, j)))],
                    _store(F32), precision=lax.Precision.HIGHEST)
    return e2.reshape(4, RPB_ROWS - 1, GRID_W, 128)


def _table_grad_to_rpb(de2):
    onehot = jnp.asarray(_toeplitz_onehot())
    n = onehot.shape[1]
    flat = de2.reshape(4 * (RPB_ROWS - 1), n)
    tk = 2048
    (dpairs,) = _matmul("rpb_table_grad", flat, onehot, pl.BlockSpec((flat.shape[0], tk), lambda i, j, k: (0, k)),
                        pl.BlockSpec((64, tk), lambda i, j, k: (0, k)), NT, (1, 1, n // tk), (flat.shape[0], 64), [],
                        [(_sds((flat.shape[0], 64), F32), pl.BlockSpec((flat.shape[0], 64), lambda i, j, k: (0, 0)))],
                        _store(F32), precision=lax.Precision.HIGHEST)
    dpairs = dpairs.reshape(4, RPB_ROWS - 1, 64)
    zero = jnp.zeros((4, 1, RPB_COLS), F32)
    return (jnp.concatenate([dpairs[:, :, :RPB_COLS], zero], axis=1)
            + jnp.concatenate([zero, dpairs[:, :, 32:32 + RPB_COLS]], axis=1))


HBM = pl.BlockSpec(memory_space=pl.ANY)


def _place():
    x, y, c = lax.axis_index("x"), lax.axis_index("y"), lax.axis_index("c")
    chips = [(1 - x, y), (x, 1 - y), (1 - x, 1 - y)]
    return x, y, c, chips


def _remote(src, dst, send_sem, recv_sem, to):
    return pltpu.make_async_remote_copy(src_ref=src, dst_ref=dst, send_sem=send_sem, recv_sem=recv_sem,
                                        device_id=to, device_id_type=MESH)


def _place_shard(name, w, me, plain=False):
    R, C = w.shape
    tr = _tile(R, 256)

    def body(me_ref, w_ref, *o_refs):
        for o_ref in o_refs:
            o_ref[...] = w_ref[...].astype(BF)

    row = pl.BlockSpec((tr, C), lambda i, mr: (i, 0))
    placed = pl.BlockSpec((None, tr, C), lambda i, mr: (mr[0], i, 0))
    return ORDER.call(
        body, [w], [row], prefetch=(me,), name=name, grid=(R // tr,),
        out_specs=[placed, row] if plain else [placed],
        out_shape=[_sds((N_CHIPS, R, C), BF)] + ([_sds((R, C), BF)] if plain else []),
        compiler_params=_cparams(("parallel",)),
    )


SEM = pl.BlockSpec(memory_space=pltpu.SEMAPHORE)
IN_HBM = pl.BlockSpec(memory_space=pltpu.HBM)
DATAFLOW = pltpu.SideEffectType.DATAFLOW_SIDE_EFFECTING


def _in_hbm(a):
    return pltpu.with_memory_space_constraint(a, pltpu.HBM)


def _copy_start(name, bufs, copies, n_copies, earlier=None):
    n = len(bufs)
    after = None if any(b is ORDER.last for b in bufs) else ORDER.last
    n_extra = (2 if earlier is not None else 0) + (1 if after is not None else 0)

    def body(*refs):
        ins = refs[:n]
        if earlier is not None:
            for k, (src, dst, to) in enumerate(earlier[0](ins)):
                cp = _remote(src, dst, refs[n].at[k], refs[n + 1].at[k], to)
                cp.wait_send()
                cp.wait_recv()
        send_sems, recv_sems = refs[n + n_extra], refs[n + n_extra + 1]
        for k, (src, dst, to) in enumerate(copies(ins)):
            _remote(src, dst, send_sems.at[k], recv_sems.at[k], to).start()
        refs[-1][...] = jnp.zeros((8, 128), F32)

    operands = [_in_hbm(b) for b in bufs]
    in_specs = [IN_HBM] * n
    if earlier is not None:
        operands += [earlier[1], earlier[2]]
        in_specs += [SEM, SEM]
    if after is not None:
        operands.append(after)
        in_specs.append(HBM)
    outs = pl.pallas_call(
        body, name=name,
        out_shape=(pltpu.SemaphoreType.DMA((n_copies,)), pltpu.SemaphoreType.DMA((n_copies,)),
                   *[pltpu.HBM(b.shape, b.dtype) for b in bufs], _sds((8, 128), F32)),
        in_specs=in_specs,
        out_specs=(SEM, SEM, *[IN_HBM] * n, pl.BlockSpec(memory_space=pltpu.VMEM)),
        input_output_aliases={i: 2 + i for i in range(n)},
        compiler_params=pltpu.CompilerParams(has_side_effects=DATAFLOW),
    )(*operands)
    ORDER.last = outs[-1]
    return outs[0], outs[1], list(outs[2:2 + n])


def _copy_wait(name, bufs, copies, send_sems, recv_sems):
    n = len(bufs)
    after = ORDER.last

    def body(*refs):
        ins = refs[:n]
        for k, (src, dst, to) in enumerate(copies(ins)):
            cp = _remote(src, dst, refs[n].at[k], refs[n + 1].at[k], to)
            cp.wait_send()
            cp.wait_recv()

    outs = list(pl.pallas_call(
        body, name=name,
        out_shape=tuple(pltpu.HBM(b.shape, b.dtype) for b in bufs),
        in_specs=[IN_HBM] * n + [SEM, SEM, HBM], out_specs=tuple([IN_HBM] * n),
        input_output_aliases={i: i for i in range(n)},
        compiler_params=pltpu.CompilerParams(has_side_effects=DATAFLOW),
    )(*bufs, send_sems, recv_sems, after))
    ORDER.last = outs[0]
    return outs


def _gather_hop1(bufs):
    x, y, c, chips = _place()
    out = []
    for b in bufs:
        half = b.shape[1] // 2
        mine = b.at[2 * x + y, pl.ds(c * half, half), :]
        out += [(mine, mine, (*chip, c)) for chip in chips]
    return out


def _gather_hop2(bufs):
    x, y, c, chips = _place()
    out = []
    for b in bufs:
        half = b.shape[1] // 2
        for chip in chips:
            landed = b.at[2 * chip[0] + chip[1], pl.ds(c * half, half), :]
            out.append((landed, landed, (x, y, 1 - c)))
    return out


def _swap_copies(bufs):
    x, y, c, _ = _place()
    n = len(bufs) // 2
    out = []
    for p, land in zip(bufs[:n], bufs[n:]):
        half = p.shape[1] // 2
        out.append((p.at[:, pl.ds((1 - c) * half, half), :], land, (x, y, 1 - c)))
    return out


def _scatter_copies(bufs):
    _, _, c, chips = _place()
    n = len(bufs) // 2
    out = []
    for s_, land in zip(bufs[:n], bufs[n:]):
        out += [(s_.at[2 * chip[0] + chip[1]], land.at[j], (*chip, c)) for j, chip in enumerate(chips)]
    return out


def _join_copies(bufs):
    x, y, c, _ = _place()
    out = []
    for b in bufs:
        half = b.shape[0] // 2
        mine = b.at[pl.ds(c * half, half), :]
        out.append((mine, mine, (x, y, 1 - c)))
    return out


def _gather_small(vec):
    m_per, n = vec.shape

    def body(x_ref, out_ref, send_sems, recv_sems, local_sem):
        x, y, c, chips = _place()
        me, sibling = (x, y, c), (x, y, 1 - c)

        def rows(px, py, pc):
            return out_ref.at[pl.ds((4 * px + 2 * py + pc) * m_per, m_per), :]

        def copy(k, block, to, src=None):
            return _remote(rows(*block) if src is None else src, rows(*block), send_sems.at[k], recv_sems.at[k], to)

        mine = pltpu.make_async_copy(x_ref, rows(*me), local_sem)
        mine.start()
        first = [copy(0, me, sibling, src=x_ref)]
        first += [copy(1 + j, me, (*chip, c), src=x_ref) for j, chip in enumerate(chips)]
        for cp in first:
            cp.start()
        passed = [copy(4 + j, (*chip, c), sibling) for j, chip in enumerate(chips)]
        for j, chip in enumerate(chips):
            copy(1 + j, (*chip, c), me).wait_recv()
            passed[j].start()
        copy(0, sibling, me).wait_recv()
        for j, chip in enumerate(chips):
            copy(4 + j, (*chip, 1 - c), me).wait_recv()
        for cp in first + passed:
            cp.wait_send()
        mine.wait()

    return ORDER.call(
        body, [vec], [pl.BlockSpec(memory_space=pltpu.VMEM)], name="gather_small_grads",
        out_shape=_sds((8 * m_per, n), vec.dtype), out_specs=pl.BlockSpec(memory_space=pltpu.VMEM),
        scratch_shapes=[pltpu.SemaphoreType.DMA((7,)), pltpu.SemaphoreType.DMA((7,)), pltpu.SemaphoreType.DMA],
    )


def _add_sibling(name, partial, received, c):
    _, R, C = partial.shape
    half = R // 2
    tr = _tile(half, 256)
    nb = half // tr

    def body(c_ref, p_ref, r_ref, o_ref):
        o_ref[...] = (p_ref[...].astype(F32) + r_ref[...].astype(F32)).astype(BF)

    return ORDER.call(
        body, [partial, received],
        [pl.BlockSpec((None, tr, C), lambda j, i, cr: (j, cr[0] * nb + i, 0)),
         pl.BlockSpec((None, tr, C), lambda j, i, cr: (j, i, 0))],
        prefetch=(c,), name=name, grid=(N_CHIPS, nb),
        out_specs=pl.BlockSpec((None, tr, C), lambda j, i, cr: (j, i, 0)),
        out_shape=_sds((N_CHIPS, half, C), BF), compiler_params=_cparams(("parallel", "parallel")),
    )


def _add_chips(name, sums, received, me_c):
    _, half, C = sums.shape
    tr = _tile(half, 256)
    nb = half // tr

    def body(mc_ref, s_ref, r_ref, o_ref):
        acc = s_ref[...].astype(F32)
        for j in range(3):
            acc = acc + r_ref[j].astype(F32)
        o_ref[...] = acc

    return ORDER.call(
        body, [sums, received],
        [pl.BlockSpec((None, tr, C), lambda i, mc: (mc[0], i, 0)),
         pl.BlockSpec((3, tr, C), lambda i, mc: (0, i, 0))],
        prefetch=(me_c,), name=name, grid=(nb,),
        out_specs=pl.BlockSpec((tr, C), lambda i, mc: (mc[1] * nb + i, 0)),
        out_shape=_sds((2 * half, C), F32), compiler_params=_cparams(("parallel",)),
    )


def _adamw_math(w, g, m, v):
    m = ADAM_B1 * m + (1.0 - ADAM_B1) * g
    v = ADAM_B2 * v + (1.0 - ADAM_B2) * (g * g)
    m_hat = m / (1.0 - ADAM_B1 ** ADAM_STEP)
    v_hat = v / (1.0 - ADAM_B2 ** ADAM_STEP)
    delta = -ADAM_LR * (m_hat / (jnp.sqrt(v_hat) + ADAM_EPS) + ADAM_WD * w)
    return delta, m, v


def _adamw(name, w, g, m, v):
    R, C = w.shape
    tr = _tile(R, 256)

    def body(w_ref, g_ref, m_ref, v_ref, go_ref, d_ref, mo_ref, vo_ref):
        gv = g_ref[...]
        go_ref[...] = gv
        d_ref[...], mo_ref[...], vo_ref[...] = _adamw_math(w_ref[...], gv, m_ref[...], v_ref[...])

    row = pl.BlockSpec((tr, C), lambda i: (i, 0))
    return ORDER.call(
        body, [w, g, m, v], [row] * 4, name=name, grid=(R // tr,), out_specs=[row] * 4,
        out_shape=[_sds((R, C), F32)] * 4, compiler_params=_cparams(("parallel",)), chain_output=1,
    )


def _adamw_small(gathered, w, m, v):
    rows, n = w.shape

    def body(ga_ref, w_ref, m_ref, v_ref, go_ref, d_ref, mo_ref, vo_ref):
        g = ga_ref[pl.ds(0, rows), :]
        for dev in range(1, 8):
            g = g + ga_ref[pl.ds(dev * rows, rows), :]
        go_ref[...] = g
        d_ref[...], mo_ref[...], vo_ref[...] = _adamw_math(w_ref[...], g, m_ref[...], v_ref[...])

    whole = pl.BlockSpec(memory_space=pltpu.VMEM)
    return ORDER.call(
        body, [gathered, w, m, v], [whole] * 4, name="adamw_small", out_specs=[whole] * 4,
        out_shape=[_sds((rows, n), F32)] * 4, compiler_params=_cparams(), chain_output=1,
    )


def _proj_merge(y_a, y_b, gpa, gpb, g3):
    S, K = y_a.shape
    _, _, Nq = gpa.shape
    D = N_CHIPS * Nq
    tm, tn = _tile(S, 1024), _tile(Nq, 512)
    q = Nq // tn

    def body(ya_ref, yb_ref, wa_ref, wb_ref, g_ref, merged_ref, c_ref):
        pa = jnp.dot(ya_ref[...], wa_ref[...], preferred_element_type=F32)
        pb = jnp.dot(yb_ref[...], wb_ref[...], preferred_element_type=F32)
        g = g_ref[...].astype(F32)
        merged_ref[...] = (g[0] * pa + g[1] * pb).astype(BF)
        c_ref[0] = (pa * g[0] * (1.0 - g[0])).astype(BF)
        c_ref[1] = (pb * g[1] * (1.0 - g[1])).astype(BF)

    rows = pl.BlockSpec((tm, K), lambda i, j: (i, 0))
    weight = pl.BlockSpec((None, K, tn), lambda i, j: (j // q, 0, j % q))
    pair = pl.BlockSpec((2, tm, tn), lambda i, j: (0, i, j))
    return ORDER.call(
        body, [y_a, y_b, gpa, gpb, g3], [rows, rows, weight, weight, pair], name="proj_merge",
        grid=(S // tm, N_CHIPS * q), out_specs=[pl.BlockSpec((tm, tn), lambda i, j: (i, j)), pair],
        out_shape=[_sds((S, D), BF), _sds((2, S, D), BF)], compiler_params=_cparams(("parallel", "parallel")))


def _out_proj_dx(dx1b, wout, g3, c3, gpa, gpb):
    S, D = dx1b.shape
    _, K, Nq = gpa.shape
    tm, tn = _tile(S, 1024), Nq
    nj = D // tn

    def body(a_ref, w_ref, g_ref, c_ref, wa_ref, wb_ref, dpa_ref, dpb_ref, dg_ref, db_ref, dya_ref, dyb_ref,
             acc_a, acc_b):
        j = pl.program_id(1)
        dm = lax.dot_general(a_ref[...], w_ref[...], (NT, ((), ())), preferred_element_type=F32)
        g, c = g_ref[...].astype(F32), c_ref[...].astype(F32)
        dpa, dpb = (dm * g[0]).astype(BF), (dm * g[1]).astype(BF)
        dpa_ref[...] = dpa
        dpb_ref[...] = dpb
        dga, dgb = dm * c[0], dm * c[1]
        dg_ref[0] = dga.astype(BF)
        dg_ref[1] = dgb.astype(BF)
        db_ref[...] = jnp.concatenate([jnp.sum(dga, axis=0, keepdims=True), jnp.sum(dgb, axis=0, keepdims=True)], 0)
        ya = lax.dot_general(dpa, wa_ref[...], (NT, ((), ())), preferred_element_type=F32)
        yb = lax.dot_general(dpb, wb_ref[...], (NT, ((), ())), preferred_element_type=F32)

        @pl.when(j == 0)
        def _():
            acc_a[...] = ya
            acc_b[...] = yb

        @pl.when(j > 0)
        def _():
            acc_a[...] += ya
            acc_b[...] += yb

        @pl.when(j == nj - 1)
        def _():
            dya_ref[...] = acc_a[...].astype(BF)
            dyb_ref[...] = acc_b[...].astype(BF)

    tile = pl.BlockSpec((tm, tn), lambda i, j: (i, j))
    pair = pl.BlockSpec((2, tm, tn), lambda i, j: (0, i, j))
    shard = pl.BlockSpec((None, K, tn), lambda i, j: (j, 0, 0))
    rows = pl.BlockSpec((tm, K), lambda i, j: (i, 0))
    return ORDER.call(
        body, [dx1b, wout, g3, c3, gpa, gpb],
        [pl.BlockSpec((tm, D), lambda i, j: (i, 0)), pl.BlockSpec((tn, D), lambda i, j: (j, 0)), pair, pair, shard, shard],
        name="out_proj_dx", grid=(S // tm, nj),
        out_specs=[tile, tile, pair, pl.BlockSpec((None, 2, tn), lambda i, j: (i, 0, j)), rows, rows],
        out_shape=[_sds((S, D), BF), _sds((S, D), BF), _sds((2, S, D), BF), _sds((S // tm, 2, D), F32),
                   _sds((S, K), BF), _sds((S, K), BF)],
        scratch_shapes=[pltpu.VMEM((tm, K), F32), pltpu.VMEM((tm, K), F32)],
        compiler_params=_cparams(("parallel", "arbitrary")))


class _Exchange:
    GATHER = (("qkv",), ("gate",), ("proj_a", "proj_b", "out"), ("up",), ("down",))
    REDUCE = {"mlp": ("down", "up"), "mix": ("out", "proj_a", "proj_b"), "in": ("qkv", "gate")}

    OWN_FIRST = ("qkv", "gate")

    def __init__(self, shards, me, c, moments):
        self.me, self.c = me, c
        self.shards, self.moments = shards, moments
        self.hop1, self.hop2, self.stage, self.grads, self.own, self.updates = {}, {}, {}, {}, {}, {}
        for g, names in enumerate(self.GATHER):
            bufs = []
            for n in names:
                placed = _place_shard(f"place_{n}", shards[n], me, plain=n in self.OWN_FIRST)
                bufs.append(placed[0])
                if n in self.OWN_FIRST:
                    self.own[n] = placed[1]
            self.hop1[g] = _copy_start(f"gather{g}_start", bufs, _gather_hop1, 3 * len(names))

    def forward(self, g):
        send, recv, thru = self.hop1.pop(g)
        self.hop2[g] = _copy_start(f"gather{g}_forward", thru, _gather_hop2, len(thru) * 3,
                                   earlier=(_gather_hop1, send, recv))

    def weights(self, g):
        send, recv, thru = self.hop2.pop(g)
        return _copy_wait(f"gather{g}_wait", thru, _gather_hop2, send, recv)

    def adamw_beside(self, name):
        def update(w, g, m, v):
            return (g,) + _adamw_math(w, g, m, v)
        return update, [self.shards[name], self.grads[name], *self.moments[name]], 4

    def reduce(self, key, partials=None):
        names = self.REDUCE[key]
        n = len(names)
        if partials is not None:
            lands = [lax.empty((p.shape[0], p.shape[1] // 2, p.shape[2]), p.dtype) for p in partials]
            self.stage[key] = ("swap",) + _copy_start(f"reduce_{key}_swap", list(partials) + lands, _swap_copies, n)
            return
        kind, send, recv, thru = self.stage.pop(key)
        if kind == "swap":
            thru = _copy_wait(f"reduce_{key}_swap_wait", thru, _swap_copies, send, recv)
            sums = [_add_sibling(f"reduce_{nm}_add_sibling", p, r, self.c)
                    for nm, p, r in zip(names, thru[:n], thru[n:])]
            lands = [lax.empty((3,) + s_.shape[1:], s_.dtype) for s_ in sums]
            self.stage[key] = ("scatter",) + _copy_start(f"reduce_{key}_scatter", sums + lands, _scatter_copies, 3 * n)
        elif kind == "scatter":
            thru = _copy_wait(f"reduce_{key}_scatter_wait", thru, _scatter_copies, send, recv)
            me_c = jnp.concatenate([self.me, self.c])
            halves = [_add_chips(f"reduce_{nm}_add_chips", s_, r, me_c)
                      for nm, s_, r in zip(names, thru[:n], thru[n:])]
            self.stage[key] = ("join",) + _copy_start(f"reduce_{key}_join", halves, _join_copies, n)
        else:
            thru = _copy_wait(f"reduce_{key}_join_wait", thru, _join_copies, send, recv)
            self.grads.update(zip(names, thru))


def _forward_backward(x, target, norm_mix, b_gate, rpb, norm_mlp, norm_final, ex):
    S, D = x.shape

    h1 = _rms_fwd("rms_mix", x, norm_mix)
    nq = QKV_W // 512
    qkv_out = (((3, S, QKV_W), BF), lambda i, T: (T // nq, i, T % nq))
    tg = _tile(ex.own["gate"].shape[1], 1024)
    ng = D // tg
    gate_out = (((2, S, D), BF), lambda i, T: (T // ng, i, T % ng))

    def gate_epilogue(acc, ex_, outs):
        outs[0][...] = jax.nn.sigmoid(acc + ex_[0][...]).astype(BF)

    qkv3 = _mm_nn_shards("qkv_own", h1, ex.own["qkv"], ex.me, True, *qkv_out, _store(BF), tm=2048)
    g3 = _mm_nn_shards("gate_own", h1, ex.own["gate"], ex.me, True, *gate_out, gate_epilogue, extras=[b_gate], tn=tg)
    ex.forward(0)
    e2 = _rpb_to_table(rpb)
    (gq,) = ex.weights(0)
    qkv3 = _mm_nn_shards("qkv", h1, gq, ex.me, False, *qkv_out, _store(BF), into=qkv3, tm=2048)

    ex.forward(1)
    outs_a = [_attn_a_fwd(qkv3, 0, DILATIONS[0])]
    (gg,) = ex.weights(1)
    g3 = _mm_nn_shards("gate", h1, gg, ex.me, False, *gate_out, gate_epilogue, extras=[b_gate], into=g3, tn=tg)

    ex.forward(2)
    qkv_views = _qkv_views("qkv_views", qkv3)
    outs_a += [_attn_a_fwd(qkv_views[d], grp, d) for grp, d in enumerate(DILATIONS) if grp > 0]
    y_a, lj = _attn_a_combine([o for o, _ in outs_a], [l for _, l in outs_a])
    y_b, lse_b = _attn_b_fwd(qkv3, e2)
    gpa, gpb, gout = ex.weights(2)
    wout = gout.reshape(D, D)
    merged, c3 = _proj_merge(y_a, y_b, gpa, gpb, g3)

    def residual_epilogue(acc, ex_, outs):
        outs[0][...] = acc + ex_[0][...]

    def residual_norm_epilogue(acc, ex_, outs):
        x1v = acc + ex_[0][...]
        outs[0][...] = x1v
        r = lax.rsqrt(jnp.mean(x1v * x1v, axis=-1, keepdims=True) + EPS)
        outs[1][...] = ((x1v * r) * ex_[1][...]).astype(BF)

    def nn_plain(name, a, w, res, bm=1024, bn=1024, norm=None):
        M, K = a.shape
        N = w.shape[1]
        bm, bn, bk = _tile(M, bm), _tile(N, bn), _tile(K, 2048)
        t = pl.BlockSpec((bm, bn), lambda i, j, k: (i, j))
        extras, outs, epilogue = [(res, t)], [(_sds((M, N), F32), t)], residual_epilogue
        if norm is not None:
            assert bn == N
            extras.append((norm, pl.BlockSpec((1, N), lambda i, j, k: (0, 0))))
            outs.append((_sds((M, N), BF), t))
            epilogue = residual_norm_epilogue
        result = _matmul(name, a, w, pl.BlockSpec((bm, bk), lambda i, j, k: (i, k)),
                         pl.BlockSpec((bk, bn), lambda i, j, k: (k, j)), NN, (M // bm, N // bn, K // bk), (bm, bn),
                         extras, outs, epilogue)
        return result[0] if norm is None else result

    ex.forward(3)
    x1, h2 = nn_plain("out_proj", merged, wout, x, bm=512, bn=2048, norm=norm_mlp)
    (gup,) = ex.weights(3)
    F = gup.shape[2] * N_CHIPS

    def up_epilogue(acc, ex_, outs):
        ru = jnp.maximum(acc, 0.0)
        outs[0][...] = (ru * ru).astype(BF)
        outs[1][...] = ru.astype(BF)

    tu = _tile(gup.shape[2], 2048)
    ut = pl.BlockSpec((_tile(S, 1024), tu), lambda i, j, k: (i, j))
    (act, ru), _ = _mm_nn_cols("mlp_up", h2, gup, BF, epilogue=up_epilogue, tn=tu,
                               outs=[(_sds((S, F), BF), ut), (_sds((S, F), BF), ut)])
    ex.forward(4)
    (gdown,) = ex.weights(4)
    wdown = gdown.reshape(F, D)
    x2 = nn_plain("mlp_down", act, wdown, x1)

    loss, dx2, dx2b, d_norm_final = _loss_head(x2, target, norm_final.reshape(1, D))

    def nt_rows(name, a, w, epilogue, extras, outs, bn=1024):
        M, N = a.shape
        K = w.shape[0]
        bm, bn, bk = _tile(M, 1024), _tile(K, bn), _tile(N, 2048)
        return _matmul(name, a, w, pl.BlockSpec((bm, bk), lambda i, j, k: (i, k)),
                       pl.BlockSpec((bn, bk), lambda i, j, k: (j, k)), NT, (M // bm, K // bn, N // bk), (bm, bn),
                       extras(bm, bn), outs(bm, bn), epilogue)

    def nt_cols(name, a_spec_fn, a, g, M, epilogue, extras, outs, bk, bn=1024, side=None):
        _, K, Nq = g.shape
        bm, bn, bk = _tile(M, 1024), _tile(K, bn), _tile(Nq, bk)
        q = Nq // bk
        return _matmul(name, a, g, a_spec_fn(bm, bk), pl.BlockSpec((None, bn, bk), lambda i, j, k: (k // q, j, k % q)),
                       NT, (M // bm, K // bn, N_CHIPS * q), (bm, bn), extras(bm, bn), outs(bm, bn), epilogue,
                       side=side)

    def tn_grad(name, a, a_spec_fn, b, b_spec_fn, Kin, N, out_shape, out_spec_fn, bn=1024):
        bm, bn, bk = _tile(Kin, 1024), _tile(N, bn), _tile(S, 4096)
        return _matmul(name, a, b, a_spec_fn(bk, bm), b_spec_fn(bk, bn), TN, (Kin // bm, N // bn, S // bk), (bm, bn),
                       [], [(_sds(out_shape, BF), out_spec_fn(bm, bn))], _store(BF))[0]

    plain_a = lambda bk, bm: pl.BlockSpec((bk, bm), lambda i, j, k: (k, i))
    plain_b = lambda bk, bn: pl.BlockSpec((bk, bn), lambda i, j, k: (k, j))
    plain_o = lambda bm, bn: pl.BlockSpec((bm, bn), lambda i, j, k: (i, j))
    a_rows = lambda bm, bk: pl.BlockSpec((bm, bk), lambda i, j, k: (i, k))

    def cols_o(Nq):
        def spec(bm, bn):
            q = Nq // bn
            return pl.BlockSpec((None, bm, bn), lambda i, j, k: (j // q, i, j % q))
        return spec

    def du_epilogue(acc, ex_, outs):
        outs[0][...] = (acc * (2.0 * ex_[0][...].astype(F32))).astype(BF)

    dw_down = tn_grad("mlp_down_dw", act, plain_a, dx2b, plain_b, F, D, (F, D), plain_o)
    (du,) = nt_rows("mlp_down_dx", dx2b, wdown, du_epilogue,
                    lambda bm, bn: [(ru, plain_o(bm, bn))], lambda bm, bn: [(_sds((S, F), BF), plain_o(bm, bn))],
                    bn=2048)

    fq = gup.shape[2]
    dw_up = tn_grad("mlp_up_dw", h2, plain_a, du, plain_b, D, F, (N_CHIPS, D, fq), cols_o(fq), bn=min(fq, 1024))
    ex.reduce("mlp", partials=[dw_down.reshape(N_CHIPS, F // N_CHIPS, D), dw_up])
    (dh2,) = nt_cols("mlp_up_dx", a_rows, du, gup, S, _store(F32), lambda bm, bn: [],
                     lambda bm, bn: [(_sds((S, D), F32), plain_o(bm, bn))], 1024, bn=2048)
    ex.reduce("mlp")
    dx1, dx1b, d_norm_mlp = _rms_bwd("rms_mlp_bwd", dh2, x1, norm_mlp, dx2)

    dpa, dpb, dg3, db_gate, dy_a, dy_b = _out_proj_dx(dx1b, wout, g3, c3, gpa, gpb)
    dw_out = tn_grad("out_proj_dw", merged, plain_a, dx1b, plain_b, D, D, (D, D), plain_o)

    pq = gpa.shape[2]
    dw_pa = tn_grad("proj_a_dw", y_a, plain_a, dpa, plain_b, 512, D, (N_CHIPS, 512, pq), cols_o(pq), bn=min(pq, 512))
    dw_pb = tn_grad("proj_b_dw", y_b, plain_a, dpb, plain_b, 512, D, (N_CHIPS, 512, pq), cols_o(pq), bn=min(pq, 512))
    ex.reduce("mix", partials=[dw_out.reshape(N_CHIPS, D // N_CHIPS, D), dw_pa, dw_pb])

    dqkv3 = lax.empty((3, S, QKV_W), BF)
    dqkv3 = _attn_a_bwd(qkv3, dy_a, y_a, lj, dqkv3, 0, DILATIONS[0])
    ex.reduce("mix")
    dy_views, y_views, lj_views = _dilated_rows("attn_a_bwd_rows", [dy_a, y_a, lj])
    dqkv_views = {d: _attn_a_bwd(qkv_views[d], dy_views[d], y_views[d], lj_views[d], None, grp, d)
                  for grp, d in enumerate(DILATIONS) if grp > 0}
    dqkv3 = _qkv_views("dqkv_from_views", dqkv3, dqkv_views)
    dqkv3, de2 = _attn_b_bwd(qkv3, e2, dy_b, y_b, lse_b, dqkv3)
    d_rpb = _table_grad_to_rpb(de2)

    def stacked_a(width):
        def spec(bm, bk):
            q = width // bk
            return pl.BlockSpec((None, bm, bk), lambda i, j, k: (k // q, i, k % q))
        return spec

    def stacked_b(width):
        def spec(bk, bn):
            q = width // bn
            return pl.BlockSpec((None, bk, bn), lambda i, j, k: (j // q, k, j % q))
        return spec

    ex.reduce("mlp")
    dw_qkv = tn_grad("qkv_dw", h1, plain_a, dqkv3, stacked_b(QKV_W), D, 3 * QKV_W, (N_CHIPS,) + gq.shape[1:],
                     cols_o(gq.shape[2]), bn=512)
    dw_gate = tn_grad("gate_dw", h1, plain_a, dg3, stacked_b(D), D, 2 * D, (N_CHIPS,) + gg.shape[1:],
                      cols_o(gg.shape[2]), bn=gg.shape[2])
    ex.reduce("in", partials=[dw_qkv, dw_gate])
    ex.reduce("mlp")
    dh1_q, *ex.updates["down"] = nt_cols(
        "qkv_dx", stacked_a(QKV_W), dqkv3, gq, S, _store(F32), lambda bm, bn: [],
        lambda bm, bn: [(_sds((S, D), F32), plain_o(bm, bn))], 512, bn=2048, side=ex.adamw_beside("down"))
    ex.reduce("in")
    ex.reduce("mix")

    def add_epilogue(acc, ex_, outs):
        outs[0][...] = acc + ex_[0][...]

    dh1, *ex.updates["up"] = nt_cols(
        "gate_dx", stacked_a(D), dg3, gg, S, add_epilogue, lambda bm, bn: [(dh1_q, plain_o(bm, bn))],
        lambda bm, bn: [(_sds((S, D), F32), plain_o(bm, bn))], gg.shape[2], side=ex.adamw_beside("up"))
    grad_x, _, d_norm_mix = _rms_bwd("rms_mix_bwd", dh1, x, norm_mix, dx1)
    ex.reduce("mix")

    small = [d_norm_mix, jnp.sum(db_gate, axis=0).reshape(1, 2 * D), d_rpb, d_norm_mlp, d_norm_final]
    return loss, grad_x, small


def _pack_small(parts, width):
    flat = jnp.concatenate([p.reshape(-1) for p in parts])
    return jnp.pad(flat, (0, 8 * width - flat.shape[0])).reshape(8, width)


def kernel(x, norm_mix, w_qkv, w_gate, b_gate, rpb, w_proj_a, w_proj_b, w_out, norm_mlp, w_up, w_down, norm_final, loss_target, m_norm_mix, m_w_qkv, m_w_gate, m_b_gate, m_rpb, m_w_proj_a, m_w_proj_b, m_w_out, m_norm_mlp, m_w_up, m_w_down, m_norm_final, v_norm_mix, v_w_qkv, v_w_gate, v_b_gate, v_rpb, v_w_proj_a, v_w_proj_b, v_w_out, v_norm_mlp, v_w_up, v_w_down, v_norm_final):
    names = ["qkv", "gate", "proj_a", "proj_b", "out", "up", "down"]
    big = dict(zip(names, [w_qkv[0], w_gate[0], w_proj_a[0], w_proj_b[0], w_out[0], w_up[0], w_down[0]]))
    big_m = dict(zip(names, [m_w_qkv[0], m_w_gate[0], m_w_proj_a[0], m_w_proj_b[0], m_w_out[0], m_w_up[0], m_w_down[0]]))
    big_v = dict(zip(names, [v_w_qkv[0], v_w_gate[0], v_w_proj_a[0], v_w_proj_b[0], v_w_out[0], v_w_up[0], v_w_down[0]]))

    c = lax.axis_index("c").astype(jnp.int32).reshape(1)
    me = (2 * lax.axis_index("x") + lax.axis_index("y")).astype(jnp.int32).reshape(1)
    ORDER.last = None
    ex = _Exchange(big, me, c, {n: (big_m[n], big_v[n]) for n in names})
    loss, grad_x, small = _forward_backward(x[0], loss_target[0], norm_mix, b_gate, rpb[0], norm_mlp, norm_final, ex)

    def adamw(group):
        return {n: ex.updates[n] if ex.updates.get(n) else _adamw(f"adamw_{n}", big[n], ex.grads[n], big_m[n], big_v[n])
                for n in _Exchange.REDUCE[group]}

    big_out = {**adamw("mlp"), **adamw("mix")}
    ex.reduce("in")

    small_w = [norm_mix, b_gate, rpb, norm_mlp, norm_final]
    count = sum(int(np.prod(p.shape)) for p in small_w)
    width = -(-count // (8 * 128)) * 128
    packed = _adamw_small(_gather_small(_pack_small(small, width)), _pack_small(small_w, width),
                          _pack_small([m_norm_mix, m_b_gate, m_rpb, m_norm_mlp, m_norm_final], width),
                          _pack_small([v_norm_mix, v_b_gate, v_rpb, v_norm_mlp, v_norm_final], width))
    ex.reduce("in")
    big_out.update(adamw("in"))

    def unpack(flat2d):
        flat, out, at = flat2d.reshape(-1), [], 0
        for p in small_w:
            size = int(np.prod(p.shape))
            out.append(flat[at:at + size].reshape(p.shape))
            at += size
        return out

    small_out = [unpack(a) for a in packed]

    def ordered(kind):
        sm = small_out[kind]
        bg = {n: o[kind][None] for n, o in big_out.items()}
        return [sm[0], bg["qkv"], bg["gate"], sm[1], sm[2], bg["proj_a"], bg["proj_b"], bg["out"], sm[3],
                bg["up"], bg["down"], sm[4]]

    total = lax.psum(loss[0, 0], ("x", "y", "c"))
    return (total, grad_x[None], *ordered(0), *ordered(1), *ordered(2), *ordered(3))
```

```python
import math

import numpy as np
import jax
import jax.numpy as jnp
from jax import lax
from jax.experimental import pallas as pl
from jax.experimental.pallas import tpu as pltpu

BF = jnp.bfloat16
F32 = jnp.float32
MESH = pl.DeviceIdType.MESH

HEAD_DIM = 128
N_HEADS = 16
N_HEADS_A = 12
QKV_W = N_HEADS * HEAD_DIM
DILATIONS = (1, 4, 16)
HALF_WINDOW = 64
GRID_W = 64
NA_ROWS = 8
NA_COLS = 16
RPB_ROWS = 2 * NA_ROWS - 1
RPB_COLS = 2 * NA_COLS - 1
EPS = 1e-6
NEG = -1e30
SCALE = HEAD_DIM ** -0.5

ADAM_LR = 0.001
ADAM_B1 = 0.9
ADAM_B2 = 0.999
ADAM_EPS = 1e-08
ADAM_WD = 0.01
ADAM_STEP = 10

N_CHIPS = 4
VMEM_LIMIT_BYTES = 48 * 1024 * 1024
QB = 256
NBR_SIDE = 16
ROW_TILE = 512


def _key_rows(L):
    return min(QB + 2 * HALF_WINDOW, L)


def _cparams(sem=None):
    return pltpu.CompilerParams(dimension_semantics=sem, vmem_limit_bytes=VMEM_LIMIT_BYTES)


def _tile(dim, want):
    t = min(dim, want)
    assert dim % t == 0, (dim, want)
    return t


class _ProgramOrder:
    def __init__(self):
        self.last = None

    def call(self, body, operands, in_specs, *, prefetch=(), grid=None, out_specs=None, chain_output=0, **kwargs):
        operands, in_specs = list(operands), list(in_specs)
        lead = len(prefetch) + len(operands)
        if self.last is not None and not any(op is self.last for op in operands):
            operands.append(self.last)
            in_specs.append(pl.BlockSpec(memory_space=pl.ANY))
            inner = body

            def body(*refs):
                return inner(*refs[:lead], *refs[lead + 1:])

        if prefetch:
            kwargs["grid_spec"] = pltpu.PrefetchScalarGridSpec(
                num_scalar_prefetch=len(prefetch), grid=grid, in_specs=in_specs, out_specs=out_specs)
        else:
            kwargs.update(in_specs=in_specs, out_specs=out_specs)
            if grid is not None:
                kwargs["grid"] = grid
        out = pl.pallas_call(body, **kwargs)(*prefetch, *operands)
        self.last = out[chain_output] if isinstance(out, (tuple, list)) else out
        return out


ORDER = _ProgramOrder()


NN = ((1,), (0,))
NT = ((1,), (1,))
TN = ((0,), (0,))


def _matmul(name, a, b, a_spec, b_spec, dims, grid, acc_shape, extras, outs, epilogue, precision=None,
            prefetch=(), into=None, side=None, epilogue_on_refs=False):
    n_ex, n_out, nk = len(extras), len(outs), grid[2]
    assert nk > 1 or not epilogue_on_refs
    side_fn, side_in, n_side_out = side if side is not None else (None, [], 0)
    side_spec = None
    n_in = 2 + n_ex + len(side_in) + (into is not None)
    if side is not None:
        R, C = side_in[0].shape
        steps = grid[0] * grid[1] * grid[2]
        side_blocks = max(n for n in range(1, steps + 1) if R % n == 0 and (R // n) % 8 == 0)

        def side_step(*ids):
            return (ids[0] * grid[1] + ids[1]) * grid[2] + ids[2]

        side_spec = pl.BlockSpec((R // side_blocks, C),
                                 lambda *ids: (jnp.minimum(side_step(*ids), side_blocks - 1), 0))

    def body(*refs):
        refs = refs[len(prefetch):]
        a_ref, b_ref = refs[0], refs[1]
        ex_refs = refs[2:2 + n_ex]
        out_refs = refs[n_in:n_in + n_out]
        if side is not None:
            @pl.when(side_step(pl.program_id(0), pl.program_id(1), pl.program_id(2)) < side_blocks)
            def _():
                results = side_fn(*[r[...] for r in refs[2 + n_ex:2 + n_ex + len(side_in)]])
                for o_ref, value in zip(refs[n_in + n_out:n_in + n_out + n_side_out], results):
                    o_ref[...] = value

        def dot():
            return lax.dot_general(a_ref[...], b_ref[...], (dims, ((), ())),
                                   preferred_element_type=F32, precision=precision)

        if nk == 1:
            epilogue(dot(), ex_refs, out_refs)
            return
        acc_ref = refs[-1]
        k = pl.program_id(2)

        @pl.when(k == 0)
        def _():
            acc_ref[...] = dot()

        if nk > 2:
            @pl.when((k > 0) & (k < nk - 1))
            def _():
                acc_ref[...] += dot()

        @pl.when(k == nk - 1)
        def _():
            if epilogue_on_refs:
                acc_ref[...] += dot()
                epilogue(acc_ref, ex_refs, out_refs)
            else:
                epilogue(acc_ref[...] + dot(), ex_refs, out_refs)

    operands = [a, b] + [e for e, _ in extras] + list(side_in)
    in_specs = [a_spec, b_spec] + [s for _, s in extras] + [side_spec] * len(side_in)
    kwargs = {}
    if into is not None:
        operands.append(into)
        in_specs.append(pl.BlockSpec(memory_space=pl.ANY))
        kwargs["input_output_aliases"] = {len(prefetch) + n_in - 1: 0}
    return ORDER.call(
        body, operands, in_specs, prefetch=prefetch, name=name, grid=grid,
        out_specs=[s for _, s in outs] + [side_spec] * n_side_out,
        out_shape=[sh for sh, _ in outs] + [_sds(s_.shape, F32) for s_ in side_in[:1]] * n_side_out,
        scratch_shapes=[pltpu.VMEM(acc_shape, F32)] if nk > 1 else [],
        compiler_params=_cparams(("parallel", "parallel", "arbitrary")), **kwargs,
    )


def _mm_nn_shards(name, a, w, me, own, out, out_block, epilogue, extras=(), into=None, tn=512, tm=1024):
    M, K = a.shape
    Nq = w.shape[-1]
    tm, tn = _tile(M, tm), _tile(Nq, tn)
    q = Nq // tn

    def tile(j, me_ref):
        shard = me_ref[0] if own else (me_ref[0] + 1 + j // q) % N_CHIPS
        return shard, j % q, shard * q + j % q

    if own:
        b_spec = pl.BlockSpec((K, tn), lambda i, j, k, me_ref: (0, j))
    else:
        b_spec = pl.BlockSpec((None, K, tn), lambda i, j, k, me_ref: (tile(j, me_ref)[0], 0, tile(j, me_ref)[1]))
    shape, dtype = out
    out_spec = pl.BlockSpec((None, tm, tn), lambda i, j, k, me_ref: out_block(i, tile(j, me_ref)[2]))
    ex = [(e, pl.BlockSpec((1, tn), lambda i, j, k, me_ref: (0, tile(j, me_ref)[2]))) for e in extras]
    return _matmul(name, a, w, pl.BlockSpec((tm, K), lambda i, j, k, me_ref: (i, 0)), b_spec, NN,
                   (M // tm, q if own else (N_CHIPS - 1) * q, 1), (tm, tn), ex, [(_sds(shape, dtype), out_spec)],
                   epilogue, prefetch=(me,), into=into)[0]


def _store(dtype):
    def epilogue(acc, ex, outs):
        outs[0][...] = acc.astype(dtype)
    return epilogue


def _sds(shape, dtype):
    return jax.ShapeDtypeStruct(shape, dtype)


def _mm_nn_cols(name, a, g, out_dtype, epilogue=None, extras=(), outs=None, tm=1024, tn=1024, tk=2048):
    M, K = a.shape
    _, _, Nq = g.shape
    tm, tn, tk = _tile(M, tm), _tile(Nq, tn), _tile(K, tk)
    q = Nq // tn
    grid = (M // tm, N_CHIPS * q, K // tk)
    if outs is None:
        outs = [(_sds((M, N_CHIPS * Nq), out_dtype), pl.BlockSpec((tm, tn), lambda i, j, k: (i, j)))]
    return _matmul(name, a, g, pl.BlockSpec((tm, tk), lambda i, j, k: (i, k)),
                   pl.BlockSpec((None, tk, tn), lambda i, j, k: (j // q, k, j % q)), NN, grid, (tm, tn),
                   list(extras), outs, epilogue or _store(out_dtype)), (tm, tn, tk)


def _rms_fwd(name, x, g):
    S, D = x.shape
    tm = _tile(S, ROW_TILE)

    def body(x_ref, g_ref, h_ref):
        xv = x_ref[...]
        r = lax.rsqrt(jnp.mean(xv * xv, axis=-1, keepdims=True) + EPS)
        h_ref[...] = ((xv * r) * g_ref[...]).astype(BF)

    row = pl.BlockSpec((tm, D), lambda i: (i, 0))
    return ORDER.call(
        body, [x, g], [row, pl.BlockSpec((1, D), lambda i: (0, 0))], name=name, grid=(S // tm,),
        out_specs=row, out_shape=_sds((S, D), BF), compiler_params=_cparams(("parallel",)),
    )


def _rms_bwd(name, dh, x, g, dres):
    S, D = x.shape
    tm = _tile(S, ROW_TILE // 2)

    def body(dh_ref, x_ref, g_ref, dres_ref, dx_ref, dxb_ref, dg_ref):
        xv = x_ref[...]
        r = lax.rsqrt(jnp.mean(xv * xv, axis=-1, keepdims=True) + EPS)
        n = xv * r
        dhv = dh_ref[...]
        dyg = dhv * g_ref[...]
        dx = dres_ref[...] + r * (dyg - n * jnp.mean(dyg * n, axis=-1, keepdims=True))
        dx_ref[...] = dx
        dxb_ref[...] = dx.astype(BF)

        @pl.when(pl.program_id(0) == 0)
        def _():
            dg_ref[...] = jnp.zeros_like(dg_ref)

        dg_ref[...] += jnp.sum(dhv * n, axis=0, keepdims=True)

    row = pl.BlockSpec((tm, D), lambda i: (i, 0))
    vec = pl.BlockSpec((1, D), lambda i: (0, 0))
    return ORDER.call(
        body, [dh, x, g, dres], [row, row, vec, row], name=name, grid=(S // tm,),
        out_specs=[row, row, vec],
        out_shape=[_sds((S, D), F32), _sds((S, D), BF), _sds((1, D), F32)],
        compiler_params=_cparams(("arbitrary",)),
    )


def _loss_head(x2, target, g):
    S, D = x2.shape
    tm = _tile(S, ROW_TILE)

    def body(x_ref, t_ref, g_ref, loss_ref, dx_ref, dxb_ref, dg_ref):
        xv = x_ref[...]
        gv = g_ref[...]
        r = lax.rsqrt(jnp.mean(xv * xv, axis=-1, keepdims=True) + EPS)
        n = xv * r
        e = n * gv - t_ref[...]
        dy = e * (1.0 / D)
        dyg = dy * gv
        dx = r * (dyg - n * jnp.mean(dyg * n, axis=-1, keepdims=True))
        dx_ref[...] = dx
        dxb_ref[...] = dx.astype(BF)

        @pl.when(pl.program_id(0) == 0)
        def _():
            dg_ref[...] = jnp.zeros_like(dg_ref)
            loss_ref[...] = jnp.zeros_like(loss_ref)

        dg_ref[...] += jnp.sum(dy * n, axis=0, keepdims=True)
        per_row = jnp.mean(e * e, axis=-1, keepdims=True)
        loss_ref[...] += 0.5 * jnp.sum(per_row, axis=0, keepdims=True)

    row = pl.BlockSpec((tm, D), lambda i: (i, 0))
    vec = pl.BlockSpec((1, D), lambda i: (0, 0))
    return ORDER.call(
        body, [x2, target, g], [row, row, vec], name="loss_head", grid=(S // tm,),
        out_specs=[pl.BlockSpec((1, 1), lambda i: (0, 0)), row, row, vec],
        out_shape=[_sds((1, 1), F32), _sds((S, D), F32), _sds((S, D), BF), _sds((1, D), F32)],
        compiler_params=_cparams(("arbitrary",)), chain_output=1,
    )


def _chains(L):
    side = min(8, L // QB)
    return side, max(1, 4 // side)


def _band_scores(qkv_ref, i, L, coef, head):
    KB = _key_rows(L)
    lanes = pl.ds(head * HEAD_DIM, HEAD_DIM)
    q0 = pl.multiple_of(i * QB, QB)
    ks = pl.multiple_of(jnp.clip(i * QB - HALF_WINDOW, 0, L - KB), HALF_WINDOW)
    q = qkv_ref[0, pl.ds(q0, QB), lanes]
    k = qkv_ref[1, pl.ds(ks, KB), lanes]
    v = qkv_ref[2, pl.ds(ks, KB), lanes]
    s = lax.dot_general(q, k, (NT, ((), ())), preferred_element_type=F32) * SCALE
    qpos = q0 + lax.broadcasted_iota(jnp.int32, (QB, KB), 0)
    kpos = ks + lax.broadcasted_iota(jnp.int32, (QB, KB), 1)
    rel = jnp.abs(kpos - qpos)
    valid = rel <= HALF_WINDOW
    s = jnp.where(valid, s - coef * rel.astype(F32), NEG)
    return q0, ks, q, k, v, s, valid


def _alibi_coefs(group, d, heads):
    first = 4 * group + 1 + pl.program_id(1) * heads
    scale = jnp.full((1, 1), -(8.0 / N_HEADS_A) * math.log(2.0), F32)
    return [jnp.exp(scale * (first + hh).astype(F32)) * float(d) for hh in range(heads)]


def _dilated_view(qkv3, group, d, heads):
    per = 4 // heads
    L = qkv3.shape[1]
    if d == 1:
        return qkv3, pl.BlockSpec((3, L, heads * HEAD_DIM), lambda r, j: (0, 0, per * group + j))
    return qkv3, pl.BlockSpec((3, L, heads * HEAD_DIM), lambda r, j: (0, 0, r * per + j))


def _qkv_views(name, qkv3, views=None):
    _, S, _ = qkv3.shape
    W = 512
    tm = _tile(S, 2 * ROW_TILE)
    dilated = [(g, d) for g, d in enumerate(DILATIONS) if d > 1]
    first = dilated[0][0]
    assert [g for g, _ in dilated] == list(range(first, first + len(dilated)))
    nc = W // 128
    to_views = views is None

    def body(*refs):
        scr = refs[-nc:]
        if to_views:
            src, outs = refs[0], refs[1:1 + len(dilated)]
        else:
            ins, dst = refs[:len(dilated)], refs[len(dilated) + 1]
        for k, (_, d) in enumerate(dilated):
            @pl.when(pl.program_id(1) == k)
            def _():
                for w in range(3):
                    for c in range(nc):
                        if to_views:
                            scr[c][...] = src[w, :, c * 128:(c + 1) * 128].astype(F32)
                    for r in range(d):
                        for c in range(nc):
                            at = r * W + c * 128
                            if to_views:
                                outs[k][w, :, at:at + 128] = scr[c][pl.ds(r, tm // d, stride=d), :].astype(BF)
                            else:
                                scr[c][pl.ds(r, tm // d, stride=d), :] = ins[k][w, :, at:at + 128].astype(F32)
                    for c in range(nc):
                        if not to_views:
                            dst[w, :, c * 128:(c + 1) * 128] = scr[c][...].astype(BF)

    cols = pl.BlockSpec((3, tm, W), lambda i, k: (0, i, first + k))
    rows = [pl.BlockSpec((3, tm // d, d * W), lambda i, k: (0, i, 0)) for _, d in dilated]
    shapes = [_sds((3, S // d, d * W), BF) for _, d in dilated]
    common = dict(name=name, grid=(S // tm, len(dilated)), scratch_shapes=[pltpu.VMEM((tm, 128), F32)] * nc,
                  compiler_params=_cparams(("parallel", "arbitrary")))
    if to_views:
        outs = ORDER.call(body, [qkv3], [cols], out_specs=rows, out_shape=shapes, **common)
        return {d: o for (_, d), o in zip(dilated, outs)}
    return ORDER.call(body, [views[d] for _, d in dilated] + [qkv3], rows + [pl.BlockSpec(memory_space=pl.ANY)],
                      out_specs=cols, out_shape=_sds(qkv3.shape, BF), input_output_aliases={len(dilated): 0}, **common)


def _attn_a_fwd(qkv3, group, d):
    L = qkv3.shape[1]
    S = L * d
    assert L % QB == 0
    side, heads = _chains(L)
    view, blocks_spec = _dilated_view(qkv3, group, d, heads)

    def body(qkv_ref, o_ref, lse_ref):
        coefs = _alibi_coefs(group, d, heads)

        def step(i, carry):
            chains = [(hh, _band_scores(qkv_ref, side * i + u, L, coefs[hh], hh))
                      for u in range(side) for hh in range(heads)]
            soft = []
            for hh, (q0, _, _, _, v, s, _) in chains:
                m = jnp.max(s, axis=-1, keepdims=True)
                p = jnp.exp(s - m)
                den = jnp.sum(p, axis=-1, keepdims=True)
                soft.append((hh, q0, (p / den).astype(BF), v, m + jnp.log(den)))
            for hh, q0, pn, v, lse in soft:
                lanes = pl.ds(hh * HEAD_DIM, HEAD_DIM)
                o_ref[pl.ds(q0, QB), lanes] = jnp.dot(pn, v, preferred_element_type=F32)
                lse_ref[pl.ds(q0, QB), lanes] = jnp.broadcast_to(lse, (QB, HEAD_DIM))
            return carry

        lax.fori_loop(0, L // QB // side, step, 0)

    per = 4 // heads
    out = pl.BlockSpec((L, heads * HEAD_DIM), lambda r, j: (0, r * per + j))
    o, lse = ORDER.call(
        body, [view], [blocks_spec],
        name=f"attn_a_fwd_d{d}", grid=(d, per),
        out_specs=[out, out],
        out_shape=[_sds((L, d * 512), F32), _sds((L, d * 512), F32)],
        compiler_params=_cparams(("parallel", "parallel")),
    )
    return o, lse


def _dilated_rows(name, arrays):
    S, W = arrays[0].shape
    tm = _tile(S, ROW_TILE)
    ds_ = [d for d in DILATIONS if d > 1]
    n = len(arrays)

    def body(*refs):
        nc = W // 128
        ins, outs, scr = refs[:n], refs[n:-nc], refs[-nc:]
        for a, src in enumerate(ins):
            for c in range(nc):
                scr[c][...] = src[:, c * 128:(c + 1) * 128].astype(F32)
            for k, d in enumerate(ds_):
                dst = outs[a * len(ds_) + k]
                for r in range(d):
                    for c in range(nc):
                        at = r * W + c * 128
                        dst[:, at:at + 128] = scr[c][pl.ds(r, tm // d, stride=d), :].astype(dst.dtype)

    row = pl.BlockSpec((tm, W), lambda i: (i, 0))
    out_specs, out_shape = [], []
    for a in arrays:
        for d in ds_:
            out_specs.append(pl.BlockSpec((tm // d, d * W), lambda i: (i, 0)))
            out_shape.append(_sds((S // d, d * W), a.dtype))
    outs = ORDER.call(body, list(arrays), [row] * n, name=name, grid=(S // tm,), out_specs=out_specs,
                      out_shape=out_shape, scratch_shapes=[pltpu.VMEM((tm, 128), F32)] * (W // 128),
                      compiler_params=_cparams(("parallel",)))
    return [{d: outs[a * len(ds_) + k] for k, d in enumerate(ds_)} for a in range(n)]


def _attn_a_combine(os_, lses):
    W = 512
    S = os_[0].shape[0] * DILATIONS[0]
    tm = _tile(S, ROW_TILE)
    nc = W // 128
    dilated = [g for g, d in enumerate(DILATIONS) if d > 1]

    def body(o0, o1, o2, l0, l1, l2, y_ref, lj_ref, *scr):
        def token_order(src, g, slot):
            d = DILATIONS[g]
            if d == 1:
                return src[...]
            bufs = scr[slot * nc:(slot + 1) * nc]
            for r in range(d):
                for c in range(nc):
                    at = r * W + c * 128
                    bufs[c][pl.ds(r, tm // d, stride=d), :] = src[:, at:at + 128]
            return jnp.concatenate([buf[...] for buf in bufs], axis=1)

        slots = {g: k for k, g in enumerate(dilated)}
        ls = [token_order(l, g, slots.get(g, 0)) for g, l in enumerate((l0, l1, l2))]
        os_tok = [token_order(o, g, len(dilated) + slots.get(g, 0)) for g, o in enumerate((o0, o1, o2))]
        m = jnp.maximum(jnp.maximum(ls[0], ls[1]), ls[2])
        es = [jnp.exp(l - m) for l in ls]
        den = es[0] + es[1] + es[2]
        y = (es[0] / den) * os_tok[0] + (es[1] / den) * os_tok[1] + (es[2] / den) * os_tok[2]
        y_ref[...] = y.astype(BF)
        lj_ref[...] = m + jnp.log(den)

    row = pl.BlockSpec((tm, W), lambda i: (i, 0))
    views = [pl.BlockSpec((tm // d, d * W), lambda i: (i, 0)) for d in DILATIONS]
    return ORDER.call(
        body, [*os_, *lses], views + views, name="attn_a_combine", grid=(S // tm,), out_specs=[row, row],
        out_shape=[_sds((S, W), BF), _sds((S, W), F32)],
        scratch_shapes=[pltpu.VMEM((tm, 128), F32)] * (2 * len(dilated) * nc),
        compiler_params=_cparams(("parallel",)),
    )


def _attn_a_bwd(qkv3, dy, y, lj, dqkv3, group, d):
    L = qkv3.shape[1]
    S = L * d
    side, heads = _chains(L)
    view, blocks_spec = _dilated_view(qkv3, group, d, heads)

    def body(qkv_ref, dy_ref, y_ref, lj_ref, *rest):
        out_ref, dk_acc, dv_acc = rest[-3:]
        coefs = _alibi_coefs(group, d, heads)
        dk_acc[...] = jnp.zeros_like(dk_acc)
        dv_acc[...] = jnp.zeros_like(dv_acc)

        def step(i, carry):
            chains = [(pl.ds(hh * HEAD_DIM, HEAD_DIM), _band_scores(qkv_ref, side * i + u, L, coefs[hh], hh))
                      for u in range(side) for hh in range(heads)]
            dys = [dy_ref[pl.ds(c[0], QB), lanes] for lanes, c in chains]
            dps = [lax.dot_general(dyv, c[4], (NT, ((), ())), preferred_element_type=F32)
                   for dyv, (_, c) in zip(dys, chains)]
            grads = []
            for (lanes, (q0, ks, q, k, v, s, valid)), dyv, dp in zip(chains, dys, dps):
                rows = pl.ds(q0, QB)
                delta = jnp.sum(dyv.astype(F32) * y_ref[rows, lanes].astype(F32), axis=-1, keepdims=True)
                p = jnp.where(valid, jnp.exp(s - jnp.tile(lj_ref[rows, lanes], (1, _key_rows(L) // HEAD_DIM))), 0.0)
                grads.append(((p * (dp - delta)).astype(BF), p.astype(BF)))
            for (lanes, (q0, ks, q, k, v, s, valid)), dyv, (ds, pb) in zip(chains, dys, grads):
                out_ref[0, pl.ds(q0, QB), lanes] = (jnp.dot(ds, k, preferred_element_type=F32) * SCALE).astype(BF)
                keys = pl.ds(ks, _key_rows(L))
                dk_acc[keys, lanes] += lax.dot_general(ds, q, (TN, ((), ())), preferred_element_type=F32) * SCALE
                dv_acc[keys, lanes] += lax.dot_general(pb, dyv, (TN, ((), ())), preferred_element_type=F32)
            return carry

        lax.fori_loop(0, L // QB // side, step, 0)
        out_ref[1] = dk_acc[...].astype(BF)
        out_ref[2] = dv_acc[...].astype(BF)

    per = 4 // heads
    width = heads * HEAD_DIM
    row = pl.BlockSpec((L, width), lambda r, j: (0, r * per + j))
    operands = [view, dy, y, lj]
    scratch = [pltpu.VMEM((L, width), F32), pltpu.VMEM((L, width), F32)]
    if d == 1:
        return ORDER.call(
            body, operands + [dqkv3], [blocks_spec, row, row, row, pl.BlockSpec(memory_space=pl.ANY)],
            name=f"attn_a_bwd_d{d}", grid=(d, per), out_specs=blocks_spec, out_shape=_sds((3, S, QKV_W), BF),
            scratch_shapes=scratch, input_output_aliases={4: 0}, compiler_params=_cparams(("parallel", "parallel")))
    return ORDER.call(
        body, operands, [blocks_spec, row, row, row], name=f"attn_a_bwd_d{d}", grid=(d, per),
        out_specs=blocks_spec, out_shape=_sds((3, L, d * 512), BF),
        scratch_shapes=scratch, compiler_params=_cparams(("parallel", "parallel")))


def _toeplitz_onehot():
    oh = np.zeros((64, GRID_W, 128), np.float32)
    for qc in range(GRID_W):
        for m in range(128):
            kc = m % GRID_W
            dc = int(np.clip(kc - qc, -(NA_COLS - 1), NA_COLS - 1)) + NA_COLS - 1
            oh[(m // GRID_W) * 32 + dc, qc, m] = 1.0
    return oh.reshape(64, GRID_W * 128)


def _nbr_scores(qkv_ref, e2_ref, r, rows, ok):
    rs = jnp.clip(r - NA_ROWS // 2, 0, rows - NA_ROWS)
    q0 = pl.multiple_of(r * GRID_W, GRID_W)
    k0 = pl.multiple_of(rs * GRID_W, GRID_W)
    q = qkv_ref[0, pl.ds(q0, GRID_W), :]
    k = qkv_ref[1, pl.ds(k0, NA_ROWS * GRID_W), :]
    v = qkv_ref[2, pl.ds(k0, NA_ROWS * GRID_W), :]
    s = lax.dot_general(q, k, (NT, ((), ())), preferred_element_type=F32) * SCALE
    first = rs - r + NA_ROWS - 1
    bias = jnp.concatenate([e2_ref[first + 2 * pair] for pair in range(NA_ROWS // 2)], axis=1)
    s = jnp.where(ok, s + bias, NEG)
    return q0, k0, first, q, k, v, s


def _nbr_col_ok():
    qc = lax.broadcasted_iota(jnp.int32, (GRID_W, NA_ROWS * GRID_W), 0)
    kc = lax.broadcasted_iota(jnp.int32, (GRID_W, NA_ROWS * GRID_W), 1) % GRID_W
    cs = jnp.clip(qc - NA_COLS // 2, 0, GRID_W - NA_COLS)
    return (kc >= cs) & (kc < cs + NA_COLS)


def _attn_b_fwd(qkv3, e2):
    _, S, _ = qkv3.shape
    rows = S // GRID_W
    assert rows >= NA_ROWS

    def body(qkv_ref, e2_ref, o_ref, lse_ref):
        ok = _nbr_col_ok()

        def step(i, carry):
            blocks = [_nbr_scores(qkv_ref, e2_ref, NBR_SIDE * i + u, rows, ok) for u in range(NBR_SIDE)]
            soft = []
            for q0, _, _, _, _, v, s in blocks:
                m = jnp.max(s, axis=-1, keepdims=True)
                p = jnp.exp(s - m)
                den = jnp.sum(p, axis=-1, keepdims=True)
                soft.append((q0, (p / den).astype(BF), v, m + jnp.log(den)))
            for q0, pn, v, lse in soft:
                o_ref[pl.ds(q0, GRID_W), :] = jnp.dot(pn, v, preferred_element_type=F32).astype(BF)
                lse_ref[pl.ds(q0, GRID_W), :] = jnp.broadcast_to(lse, (GRID_W, HEAD_DIM))
            return carry

        lax.fori_loop(0, rows // NBR_SIDE, step, 0)

    out = pl.BlockSpec((S, HEAD_DIM), lambda h: (0, h))
    return ORDER.call(
        body, [qkv3, e2],
        [pl.BlockSpec((3, S, HEAD_DIM), lambda h: (0, 0, N_HEADS_A + h)),
         pl.BlockSpec((None, RPB_ROWS - 1, GRID_W, 128), lambda h: (h, 0, 0, 0))],
        name="attn_b_fwd", grid=(4,),
        out_specs=[out, out], out_shape=[_sds((S, 512), BF), _sds((S, 512), F32)],
        compiler_params=_cparams(("parallel",)),
    )


def _attn_b_bwd(qkv3, e2, dy, y, lse, dqkv3):
    _, S, _ = qkv3.shape
    rows = S // GRID_W
    nk = NA_ROWS * GRID_W

    def body(qkv_ref, e2_ref, dy_ref, y_ref, lse_ref, _, out_ref, de2_ref, dk_acc, dv_acc):
        ok = _nbr_col_ok()
        dk_acc[...] = jnp.zeros_like(dk_acc)
        dv_acc[...] = jnp.zeros_like(dv_acc)
        de2_ref[...] = jnp.zeros_like(de2_ref)

        def step(i, carry):
            blocks = [_nbr_scores(qkv_ref, e2_ref, NBR_SIDE * i + u, rows, ok) for u in range(NBR_SIDE)]
            dys = [dy_ref[pl.ds(b[0], GRID_W), :] for b in blocks]
            dps = [lax.dot_general(dyv, b[5], (NT, ((), ())), preferred_element_type=F32) for dyv, b in zip(dys, blocks)]
            grads = []
            for (q0, k0, first, q, k, v, s), dyv, dp in zip(blocks, dys, dps):
                qrows = pl.ds(q0, GRID_W)
                delta = jnp.sum(dyv.astype(F32) * y_ref[qrows, :].astype(F32), axis=-1, keepdims=True)
                p = jnp.where(ok, jnp.exp(s - jnp.tile(lse_ref[qrows, :], (1, nk // HEAD_DIM))), 0.0)
                ds = p * (dp - delta)
                for pair in range(NA_ROWS // 2):
                    de2_ref[first + 2 * pair] += ds[:, pair * 128:(pair + 1) * 128]
                grads.append((ds.astype(BF), p.astype(BF)))
            for (q0, k0, first, q, k, v, s), dyv, (dsb, pb) in zip(blocks, dys, grads):
                out_ref[0, pl.ds(q0, GRID_W), :] = (jnp.dot(dsb, k, preferred_element_type=F32) * SCALE).astype(BF)
                keys = pl.ds(k0, nk)
                dk_acc[keys, :] += lax.dot_general(dsb, q, (TN, ((), ())), preferred_element_type=F32) * SCALE
                dv_acc[keys, :] += lax.dot_general(pb, dyv, (TN, ((), ())), preferred_element_type=F32)
            return carry

        lax.fori_loop(0, rows // NBR_SIDE, step, 0)
        out_ref[1] = dk_acc[...].astype(BF)
        out_ref[2] = dv_acc[...].astype(BF)

    heads = pl.BlockSpec((3, S, HEAD_DIM), lambda h: (0, 0, N_HEADS_A + h))
    row = pl.BlockSpec((S, HEAD_DIM), lambda h: (0, h))
    table = pl.BlockSpec((None, RPB_ROWS - 1, GRID_W, 128), lambda h: (h, 0, 0, 0))
    return ORDER.call(
        body, [qkv3, e2, dy, y, lse, dqkv3],
        [heads, table, row, row, row, pl.BlockSpec(memory_space=pl.ANY)], name="attn_b_bwd", grid=(4,),
        out_specs=[heads, table],
        out_shape=[_sds((3, S, QKV_W), BF), _sds((4, RPB_ROWS - 1, GRID_W, 128), F32)],
        scratch_shapes=[pltpu.VMEM((S, HEAD_DIM), F32), pltpu.VMEM((S, HEAD_DIM), F32)],
        input_output_aliases={5: 0},
        compiler_params=_cparams(("parallel",)), chain_output=1,
    )


def _rpb_to_table(rpb):
    pad = jnp.pad(rpb, ((0, 0), (0, 0), (0, 1)))
    pairs = jnp.concatenate([pad[:, :-1], pad[:, 1:]], axis=-1).reshape(4 * (RPB_ROWS - 1), 64)
    onehot = jnp.asarray(_toeplitz_onehot())
    n = onehot.shape[1]
    tn = 2048
    full = lambda i, j, k: (0, 0)
    (e2,) = _matmul("rpb_table", pairs, onehot, pl.BlockSpec(pairs.shape, full),
                    pl.BlockSpec((64, tn), lambda i, j, k: (0, j)), NN, (1, n // tn, 1), (pairs.shape[0], tn), [],
                    [(_sds((pairs.shape[0], n), F32), pl.BlockSpec((pairs.shape[0], tn), lambda i, j, k: (0, j)))],
                    _store(F32), precision=lax.Precision.HIGHEST)
    return e2.reshape(4, RPB_ROWS - 1, GRID_W, 128)


def _table_grad_to_rpb(de2):
    onehot = jnp.asarray(_toeplitz_onehot())
    n = onehot.shape[1]
    flat = de2.reshape(4 * (RPB_ROWS - 1), n)
    tk = 2048
    (dpairs,) = _matmul("rpb_table_grad", flat, onehot, pl.BlockSpec((flat.shape[0], tk), lambda i, j, k: (0, k)),
                        pl.BlockSpec((64, tk), lambda i, j, k: (0, k)), NT, (1, 1, n // tk), (flat.shape[0], 64), [],
                        [(_sds((flat.shape[0], 64), F32), pl.BlockSpec((flat.shape[0], 64), lambda i, j, k: (0, 0)))],
                        _store(F32), precision=lax.Precision.HIGHEST)
    dpairs = dpairs.reshape(4, RPB_ROWS - 1, 64)
    zero = jnp.zeros((4, 1, RPB_COLS), F32)
    return (jnp.concatenate([dpairs[:, :, :RPB_COLS], zero], axis=1)
            + jnp.concatenate([zero, dpairs[:, :, 32:32 + RPB_COLS]], axis=1))


HBM = pl.BlockSpec(memory_space=pl.ANY)


def _place():
    x, y, c = lax.axis_index("x"), lax.axis_index("y"), lax.axis_index("c")
    chips = [(1 - x, y), (x, 1 - y), (1 - x, 1 - y)]
    return x, y, c, chips


def _remote(src, dst, send_sem, recv_sem, to):
    return pltpu.make_async_remote_copy(src_ref=src, dst_ref=dst, send_sem=send_sem, recv_sem=recv_sem,
                                        device_id=to, device_id_type=MESH)


def _place_shard(name, w, me, plain=False):
    R, C = w.shape
    tr = _tile(R, 256)

    def body(me_ref, w_ref, *o_refs):
        for o_ref in o_refs:
            o_ref[...] = w_ref[...].astype(BF)

    row = pl.BlockSpec((tr, C), lambda i, mr: (i, 0))
    placed = pl.BlockSpec((None, tr, C), lambda i, mr: (mr[0], i, 0))
    return ORDER.call(
        body, [w], [row], prefetch=(me,), name=name, grid=(R // tr,),
        out_specs=[placed, row] if plain else [placed],
        out_shape=[_sds((N_CHIPS, R, C), BF)] + ([_sds((R, C), BF)] if plain else []),
        compiler_params=_cparams(("parallel",)),
    )


SEM = pl.BlockSpec(memory_space=pltpu.SEMAPHORE)
IN_HBM = pl.BlockSpec(memory_space=pltpu.HBM)
DATAFLOW = pltpu.SideEffectType.DATAFLOW_SIDE_EFFECTING


def _in_hbm(a):
    return pltpu.with_memory_space_constraint(a, pltpu.HBM)


def _copy_start(name, bufs, copies, n_copies, earlier=None):
    n = len(bufs)
    after = None if any(b is ORDER.last for b in bufs) else ORDER.last
    n_extra = (2 if earlier is not None else 0) + (1 if after is not None else 0)

    def body(*refs):
        ins = refs[:n]
        if earlier is not None:
            for k, (src, dst, to) in enumerate(earlier[0](ins)):
                cp = _remote(src, dst, refs[n].at[k], refs[n + 1].at[k], to)
                cp.wait_send()
                cp.wait_recv()
        send_sems, recv_sems = refs[n + n_extra], refs[n + n_extra + 1]
        for k, (src, dst, to) in enumerate(copies(ins)):
            _remote(src, dst, send_sems.at[k], recv_sems.at[k], to).start()
        refs[-1][...] = jnp.zeros((8, 128), F32)

    operands = [_in_hbm(b) for b in bufs]
    in_specs = [IN_HBM] * n
    if earlier is not None:
        operands += [earlier[1], earlier[2]]
        in_specs += [SEM, SEM]
    if after is not None:
        operands.append(after)
        in_specs.append(HBM)
    outs = pl.pallas_call(
        body, name=name,
        out_shape=(pltpu.SemaphoreType.DMA((n_copies,)), pltpu.SemaphoreType.DMA((n_copies,)),
                   *[pltpu.HBM(b.shape, b.dtype) for b in bufs], _sds((8, 128), F32)),
        in_specs=in_specs,
        out_specs=(SEM, SEM, *[IN_HBM] * n, pl.BlockSpec(memory_space=pltpu.VMEM)),
        input_output_aliases={i: 2 + i for i in range(n)},
        compiler_params=pltpu.CompilerParams(has_side_effects=DATAFLOW),
    )(*operands)
    ORDER.last = outs[-1]
    return outs[0], outs[1], list(outs[2:2 + n])


def _copy_wait(name, bufs, copies, send_sems, recv_sems):
    n = len(bufs)
    after = ORDER.last

    def body(*refs):
        ins = refs[:n]
        for k, (src, dst, to) in enumerate(copies(ins)):
            cp = _remote(src, dst, refs[n].at[k], refs[n + 1].at[k], to)
            cp.wait_send()
            cp.wait_recv()

    outs = list(pl.pallas_call(
        body, name=name,
        out_shape=tuple(pltpu.HBM(b.shape, b.dtype) for b in bufs),
        in_specs=[IN_HBM] * n + [SEM, SEM, HBM], out_specs=tuple([IN_HBM] * n),
        input_output_aliases={i: i for i in range(n)},
        compiler_params=pltpu.CompilerParams(has_side_effects=DATAFLOW),
    )(*bufs, send_sems, recv_sems, after))
    ORDER.last = outs[0]
    return outs


def _gather_hop1(bufs):
    x, y, c, chips = _place()
    out = []
    for b in bufs:
        half = b.shape[1] // 2
        mine = b.at[2 * x + y, pl.ds(c * half, half), :]
        out += [(mine, mine, (*chip, c)) for chip in chips]
    return out


def _gather_hop2(bufs):
    x, y, c, chips = _place()
    out = []
    for b in bufs:
        half = b.shape[1] // 2
        for chip in chips:
            landed = b.at[2 * chip[0] + chip[1], pl.ds(c * half, half), :]
            out.append((landed, landed, (x, y, 1 - c)))
    return out


def _swap_copies(bufs):
    x, y, c, _ = _place()
    n = len(bufs) // 2
    out = []
    for p, land in zip(bufs[:n], bufs[n:]):
        half = p.shape[1] // 2
        out.append((p.at[:, pl.ds((1 - c) * half, half), :], land, (x, y, 1 - c)))
    return out


def _scatter_copies(bufs):
    _, _, c, chips = _place()
    n = len(bufs) // 2
    out = []
    for s_, land in zip(bufs[:n], bufs[n:]):
        out += [(s_.at[2 * chip[0] + chip[1]], land.at[j], (*chip, c)) for j, chip in enumerate(chips)]
    return out


def _join_copies(bufs):
    x, y, c, _ = _place()
    out = []
    for b in bufs:
        half = b.shape[0] // 2
        mine = b.at[pl.ds(c * half, half), :]
        out.append((mine, mine, (x, y, 1 - c)))
    return out


def _gather_small(vec):
    m_per, n = vec.shape

    def body(x_ref, out_ref, send_sems, recv_sems, local_sem):
        x, y, c, chips = _place()
        me, sibling = (x, y, c), (x, y, 1 - c)

        def rows(px, py, pc):
            return out_ref.at[pl.ds((4 * px + 2 * py + pc) * m_per, m_per), :]

        def copy(k, block, to, src=None):
            return _remote(rows(*block) if src is None else src, rows(*block), send_sems.at[k], recv_sems.at[k], to)

        mine = pltpu.make_async_copy(x_ref, rows(*me), local_sem)
        mine.start()
        first = [copy(0, me, sibling, src=x_ref)]
        first += [copy(1 + j, me, (*chip, c), src=x_ref) for j, chip in enumerate(chips)]
        for cp in first:
            cp.start()
        passed = [copy(4 + j, (*chip, c), sibling) for j, chip in enumerate(chips)]
        for j, chip in enumerate(chips):
            copy(1 + j, (*chip, c), me).wait_recv()
            passed[j].start()
        copy(0, sibling, me).wait_recv()
        for j, chip in enumerate(chips):
            copy(4 + j, (*chip, 1 - c), me).wait_recv()
        for cp in first + passed:
            cp.wait_send()
        mine.wait()

    return ORDER.call(
        body, [vec], [pl.BlockSpec(memory_space=pltpu.VMEM)], name="gather_small_grads",
        out_shape=_sds((8 * m_per, n), vec.dtype), out_specs=pl.BlockSpec(memory_space=pltpu.VMEM),
        scratch_shapes=[pltpu.SemaphoreType.DMA((7,)), pltpu.SemaphoreType.DMA((7,)), pltpu.SemaphoreType.DMA],
    )


def _add_sibling(name, partial, received, c):
    _, R, C = partial.shape
    half = R // 2
    tr = _tile(half, 256)
    nb = half // tr

    def body(c_ref, p_ref, r_ref, o_ref):
        o_ref[...] = (p_ref[...].astype(F32) + r_ref[...].astype(F32)).astype(BF)

    return ORDER.call(
        body, [partial, received],
        [pl.BlockSpec((None, tr, C), lambda j, i, cr: (j, cr[0] * nb + i, 0)),
         pl.BlockSpec((None, tr, C), lambda j, i, cr: (j, i, 0))],
        prefetch=(c,), name=name, grid=(N_CHIPS, nb),
        out_specs=pl.BlockSpec((None, tr, C), lambda j, i, cr: (j, i, 0)),
        out_shape=_sds((N_CHIPS, half, C), BF), compiler_params=_cparams(("parallel", "parallel")),
    )


def _add_chips(name, sums, received, me_c):
    _, half, C = sums.shape
    tr = _tile(half, 256)
    nb = half // tr

    def body(mc_ref, s_ref, r_ref, o_ref):
        acc = s_ref[...].astype(F32)
        for j in range(3):
            acc = acc + r_ref[j].astype(F32)
        o_ref[...] = acc

    return ORDER.call(
        body, [sums, received],
        [pl.BlockSpec((None, tr, C), lambda i, mc: (mc[0], i, 0)),
         pl.BlockSpec((3, tr, C), lambda i, mc: (0, i, 0))],
        prefetch=(me_c,), name=name, grid=(nb,),
        out_specs=pl.BlockSpec((tr, C), lambda i, mc: (mc[1] * nb + i, 0)),
        out_shape=_sds((2 * half, C), F32), compiler_params=_cparams(("parallel",)),
    )


def _adamw_math(w, g, m, v):
    m = ADAM_B1 * m + (1.0 - ADAM_B1) * g
    v = ADAM_B2 * v + (1.0 - ADAM_B2) * (g * g)
    m_hat = m / (1.0 - ADAM_B1 ** ADAM_STEP)
    v_hat = v / (1.0 - ADAM_B2 ** ADAM_STEP)
    delta = -ADAM_LR * (m_hat / (jnp.sqrt(v_hat) + ADAM_EPS) + ADAM_WD * w)
    return delta, m, v


def _adamw(name, w, g, m, v):
    R, C = w.shape
    tr = _tile(R, 256)

    def body(w_ref, g_ref, m_ref, v_ref, go_ref, d_ref, mo_ref, vo_ref):
        gv = g_ref[...]
        go_ref[...] = gv
        d_ref[...], mo_ref[...], vo_ref[...] = _adamw_math(w_ref[...], gv, m_ref[...], v_ref[...])

    row = pl.BlockSpec((tr, C), lambda i: (i, 0))
    return ORDER.call(
        body, [w, g, m, v], [row] * 4, name=name, grid=(R // tr,), out_specs=[row] * 4,
        out_shape=[_sds((R, C), F32)] * 4, compiler_params=_cparams(("parallel",)), chain_output=1,
    )


def _adamw_small(gathered, w, m, v):
    rows, n = w.shape

    def body(ga_ref, w_ref, m_ref, v_ref, go_ref, d_ref, mo_ref, vo_ref):
        g = ga_ref[pl.ds(0, rows), :]
        for dev in range(1, 8):
            g = g + ga_ref[pl.ds(dev * rows, rows), :]
        go_ref[...] = g
        d_ref[...], mo_ref[...], vo_ref[...] = _adamw_math(w_ref[...], g, m_ref[...], v_ref[...])

    whole = pl.BlockSpec(memory_space=pltpu.VMEM)
    return ORDER.call(
        body, [gathered, w, m, v], [whole] * 4, name="adamw_small", out_specs=[whole] * 4,
        out_shape=[_sds((rows, n), F32)] * 4, compiler_params=_cparams(), chain_output=1,
    )


def _proj_merge(y_a, y_b, gpa, gpb, g3):
    S, K = y_a.shape
    _, _, Nq = gpa.shape
    D = N_CHIPS * Nq
    tm, tn = _tile(S, 1024), _tile(Nq, 512)
    q = Nq // tn

    def body(ya_ref, yb_ref, wa_ref, wb_ref, g_ref, merged_ref, c_ref):
        pa = jnp.dot(ya_ref[...], wa_ref[...], preferred_element_type=F32)
        pb = jnp.dot(yb_ref[...], wb_ref[...], preferred_element_type=F32)
        g = g_ref[...].astype(F32)
        merged_ref[...] = (g[0] * pa + g[1] * pb).astype(BF)
        c_ref[0] = (pa * g[0] * (1.0 - g[0])).astype(BF)
        c_ref[1] = (pb * g[1] * (1.0 - g[1])).astype(BF)

    rows = pl.BlockSpec((tm, K), lambda i, j: (i, 0))
    weight = pl.BlockSpec((None, K, tn), lambda i, j: (j // q, 0, j % q))
    pair = pl.BlockSpec((2, tm, tn), lambda i, j: (0, i, j))
    return ORDER.call(
        body, [y_a, y_b, gpa, gpb, g3], [rows, rows, weight, weight, pair], name="proj_merge",
        grid=(S // tm, N_CHIPS * q), out_specs=[pl.BlockSpec((tm, tn), lambda i, j: (i, j)), pair],
        out_shape=[_sds((S, D), BF), _sds((2, S, D), BF)], compiler_params=_cparams(("parallel", "parallel")))


def _out_proj_dx(dx1b, wout, g3, c3, gpa, gpb):
    S, D = dx1b.shape
    _, K, Nq = gpa.shape
    tm, tn = _tile(S, 1024), Nq
    nj = D // tn

    def body(a_ref, w_ref, g_ref, c_ref, wa_ref, wb_ref, dpa_ref, dpb_ref, dg_ref, db_ref, dya_ref, dyb_ref,
             acc_a, acc_b):
        j = pl.program_id(1)
        dm = lax.dot_general(a_ref[...], w_ref[...], (NT, ((), ())), preferred_element_type=F32)
        g, c = g_ref[...].astype(F32), c_ref[...].astype(F32)
        dpa, dpb = (dm * g[0]).astype(BF), (dm * g[1]).astype(BF)
        dpa_ref[...] = dpa
        dpb_ref[...] = dpb
        dga, dgb = dm * c[0], dm * c[1]
        dg_ref[0] = dga.astype(BF)
        dg_ref[1] = dgb.astype(BF)
        db_ref[...] = jnp.concatenate([jnp.sum(dga, axis=0, keepdims=True), jnp.sum(dgb, axis=0, keepdims=True)], 0)
        ya = lax.dot_general(dpa, wa_ref[...], (NT, ((), ())), preferred_element_type=F32)
        yb = lax.dot_general(dpb, wb_ref[...], (NT, ((), ())), preferred_element_type=F32)

        @pl.when(j == 0)
        def _():
            acc_a[...] = ya
            acc_b[...] = yb

        @pl.when(j > 0)
        def _():
            acc_a[...] += ya
            acc_b[...] += yb

        @pl.when(j == nj - 1)
        def _():
            dya_ref[...] = acc_a[...].astype(BF)
            dyb_ref[...] = acc_b[...].astype(BF)

    tile = pl.BlockSpec((tm, tn), lambda i, j: (i, j))
    pair = pl.BlockSpec((2, tm, tn), lambda i, j: (0, i, j))
    shard = pl.BlockSpec((None, K, tn), lambda i, j: (j, 0, 0))
    rows = pl.BlockSpec((tm, K), lambda i, j: (i, 0))
    return ORDER.call(
        body, [dx1b, wout, g3, c3, gpa, gpb],
        [pl.BlockSpec((tm, D), lambda i, j: (i, 0)), pl.BlockSpec((tn, D), lambda i, j: (j, 0)), pair, pair, shard, shard],
        name="out_proj_dx", grid=(S // tm, nj),
        out_specs=[tile, tile, pair, pl.BlockSpec((None, 2, tn), lambda i, j: (i, 0, j)), rows, rows],
        out_shape=[_sds((S, D), BF), _sds((S, D), BF), _sds((2, S, D), BF), _sds((S // tm, 2, D), F32),
                   _sds((S, K), BF), _sds((S, K), BF)],
        scratch_shapes=[pltpu.VMEM((tm, K), F32), pltpu.VMEM((tm, K), F32)],
        compiler_params=_cparams(("parallel", "arbitrary")))


class _Exchange:
    GATHER = (("qkv",), ("gate",), ("proj_a", "proj_b", "out"), ("up",), ("down",))
    REDUCE = {"mlp": ("down", "up"), "mix": ("out", "proj_a", "proj_b"), "in": ("qkv", "gate")}

    OWN_FIRST = ("qkv", "gate")

    def __init__(self, shards, me, c, moments):
        self.me, self.c = me, c
        self.shards, self.moments = shards, moments
        self.hop1, self.hop2, self.stage, self.grads, self.own, self.updates = {}, {}, {}, {}, {}, {}
        for g, names in enumerate(self.GATHER):
            bufs = []
            for n in names:
                placed = _place_shard(f"place_{n}", shards[n], me, plain=n in self.OWN_FIRST)
                bufs.append(placed[0])
                if n in self.OWN_FIRST:
                    self.own[n] = placed[1]
            self.hop1[g] = _copy_start(f"gather{g}_start", bufs, _gather_hop1, 3 * len(names))

    def forward(self, g):
        send, recv, thru = self.hop1.pop(g)
        self.hop2[g] = _copy_start(f"gather{g}_forward", thru, _gather_hop2, len(thru) * 3,
                                   earlier=(_gather_hop1, send, recv))

    def weights(self, g):
        send, recv, thru = self.hop2.pop(g)
        return _copy_wait(f"gather{g}_wait", thru, _gather_hop2, send, recv)

    def adamw_beside(self, name):
        def update(w, g, m, v):
            return (g,) + _adamw_math(w, g, m, v)
        return update, [self.shards[name], self.grads[name], *self.moments[name]], 4

    def reduce(self, key, partials=None):
        names = self.REDUCE[key]
        n = len(names)
        if partials is not None:
            lands = [lax.empty((p.shape[0], p.shape[1] // 2, p.shape[2]), p.dtype) for p in partials]
            self.stage[key] = ("swap",) + _copy_start(f"reduce_{key}_swap", list(partials) + lands, _swap_copies, n)
            return
        kind, send, recv, thru = self.stage.pop(key)
        if kind == "swap":
            thru = _copy_wait(f"reduce_{key}_swap_wait", thru, _swap_copies, send, recv)
            sums = [_add_sibling(f"reduce_{nm}_add_sibling", p, r, self.c)
                    for nm, p, r in zip(names, thru[:n], thru[n:])]
            lands = [lax.empty((3,) + s_.shape[1:], s_.dtype) for s_ in sums]
            self.stage[key] = ("scatter",) + _copy_start(f"reduce_{key}_scatter", sums + lands, _scatter_copies, 3 * n)
        elif kind == "scatter":
            thru = _copy_wait(f"reduce_{key}_scatter_wait", thru, _scatter_copies, send, recv)
            me_c = jnp.concatenate([self.me, self.c])
            halves = [_add_chips(f"reduce_{nm}_add_chips", s_, r, me_c)
                      for nm, s_, r in zip(names, thru[:n], thru[n:])]
            self.stage[key] = ("join",) + _copy_start(f"reduce_{key}_join", halves, _join_copies, n)
        else:
            thru = _copy_wait(f"reduce_{key}_join_wait", thru, _join_copies, send, recv)
            self.grads.update(zip(names, thru))


def _forward_backward(x, target, norm_mix, b_gate, rpb, norm_mlp, norm_final, ex):
    S, D = x.shape

    h1 = _rms_fwd("rms_mix", x, norm_mix)
    nq = QKV_W // 512
    qkv_out = (((3, S, QKV_W), BF), lambda i, T: (T // nq, i, T % nq))
    tg = _tile(ex.own["gate"].shape[1], 1024)
    ng = D // tg
    gate_out = (((2, S, D), BF), lambda i, T: (T // ng, i, T % ng))

    def gate_epilogue(acc, ex_, outs):
        outs[0][...] = jax.nn.sigmoid(acc + ex_[0][...]).astype(BF)

    qkv3 = _mm_nn_shards("qkv_own", h1, ex.own["qkv"], ex.me, True, *qkv_out, _store(BF), tm=2048)
    g3 = _mm_nn_shards("gate_own", h1, ex.own["gate"], ex.me, True, *gate_out, gate_epilogue, extras=[b_gate], tn=tg)
    ex.forward(0)
    e2 = _rpb_to_table(rpb)
    (gq,) = ex.weights(0)
    qkv3 = _mm_nn_shards("qkv", h1, gq, ex.me, False, *qkv_out, _store(BF), into=qkv3, tm=2048)

    ex.forward(1)
    outs_a = [_attn_a_fwd(qkv3, 0, DILATIONS[0])]
    (gg,) = ex.weights(1)
    g3 = _mm_nn_shards("gate", h1, gg, ex.me, False, *gate_out, gate_epilogue, extras=[b_gate], into=g3, tn=tg)

    ex.forward(2)
    qkv_views = _qkv_views("qkv_views", qkv3)
    outs_a += [_attn_a_fwd(qkv_views[d], grp, d) for grp, d in enumerate(DILATIONS) if grp > 0]
    y_a, lj = _attn_a_combine([o for o, _ in outs_a], [l for _, l in outs_a])
    y_b, lse_b = _attn_b_fwd(qkv3, e2)
    gpa, gpb, gout = ex.weights(2)
    wout = gout.reshape(D, D)
    merged, c3 = _proj_merge(y_a, y_b, gpa, gpb, g3)

    def residual_epilogue(acc, ex_, outs):
        outs[0][...] = acc + ex_[0][...]

    def residual_norm_epilogue(acc, ex_, outs):
        x1v = acc + ex_[0][...]
        outs[0][...] = x1v
        r = lax.rsqrt(jnp.mean(x1v * x1v, axis=-1, keepdims=True) + EPS)
        outs[1][...] = ((x1v * r) * ex_[1][...]).astype(BF)

    def nn_plain(name, a, w, res, bm=1024, bn=1024, norm=None):
        M, K = a.shape
        N = w.shape[1]
        bm, bn, bk = _tile(M, bm), _tile(N, bn), _tile(K, 2048)
        t = pl.BlockSpec((bm, bn), lambda i, j, k: (i, j))
        extras, outs, epilogue = [(res, t)], [(_sds((M, N), F32), t)], residual_epilogue
        if norm is not None:
            assert bn == N
            extras.append((norm, pl.BlockSpec((1, N), lambda i, j, k: (0, 0))))
            outs.append((_sds((M, N), BF), t))
            epilogue = residual_norm_epilogue
        result = _matmul(name, a, w, pl.BlockSpec((bm, bk), lambda i, j, k: (i, k)),
                         pl.BlockSpec((bk, bn), lambda i, j, k: (k, j)), NN, (M // bm, N // bn, K // bk), (bm, bn),
                         extras, outs, epilogue)
        return result[0] if norm is None else result

    ex.forward(3)
    x1, h2 = nn_plain("out_proj", merged, wout, x, bm=512, bn=2048, norm=norm_mlp)
    (gup,) = ex.weights(3)
    F = gup.shape[2] * N_CHIPS

    def up_epilogue(acc, ex_, outs):
        ru = jnp.maximum(acc, 0.0)
        outs[0][...] = (ru * ru).astype(BF)
        outs[1][...] = ru.astype(BF)

    tu = _tile(gup.shape[2], 2048)
    ut = pl.BlockSpec((_tile(S, 1024), tu), lambda i, j, k: (i, j))
    (act, ru), _ = _mm_nn_cols("mlp_up", h2, gup, BF, epilogue=up_epilogue, tn=tu,
                               outs=[(_sds((S, F), BF), ut), (_sds((S, F), BF), ut)])
    ex.forward(4)
    (gdown,) = ex.weights(4)
    wdown = gdown.reshape(F, D)
    x2 = nn_plain("mlp_down", act, wdown, x1)

    loss, dx2, dx2b, d_norm_final = _loss_head(x2, target, norm_final.reshape(1, D))

    def nt_rows(name, a, w, epilogue, extras, outs, bn=1024):
        M, N = a.shape
        K = w.shape[0]
        bm, bn, bk = _tile(M, 1024), _tile(K, bn), _tile(N, 2048)
        return _matmul(name, a, w, pl.BlockSpec((bm, bk), lambda i, j, k: (i, k)),
                       pl.BlockSpec((bn, bk), lambda i, j, k: (j, k)), NT, (M // bm, K // bn, N // bk), (bm, bn),
                       extras(bm, bn), outs(bm, bn), epilogue)

    def nt_cols(name, a_spec_fn, a, g, M, epilogue, extras, outs, bk, bn=1024, side=None, bm=1024, on_refs=False):
        _, K, Nq = g.shape
        bm, bn, bk = _tile(M, bm), _tile(K, bn), _tile(Nq, bk)
        q = Nq // bk
        return _matmul(name, a, g, a_spec_fn(bm, bk), pl.BlockSpec((None, bn, bk), lambda i, j, k: (k // q, j, k % q)),
                       NT, (M // bm, K // bn, N_CHIPS * q), (bm, bn), extras(bm, bn), outs(bm, bn), epilogue,
                       side=side, epilogue_on_refs=on_refs)

    def tn_grad(name, a, a_spec_fn, b, b_spec_fn, Kin, N, out_shape, out_spec_fn, bn=1024):
        bm, bn, bk = _tile(Kin, 1024), _tile(N, bn), _tile(S, 4096)
        return _matmul(name, a, b, a_spec_fn(bk, bm), b_spec_fn(bk, bn), TN, (Kin // bm, N // bn, S // bk), (bm, bn),
                       [], [(_sds(out_shape, BF), out_spec_fn(bm, bn))], _store(BF))[0]

    plain_a = lambda bk, bm: pl.BlockSpec((bk, bm), lambda i, j, k: (k, i))
    plain_b = lambda bk, bn: pl.BlockSpec((bk, bn), lambda i, j, k: (k, j))
    plain_o = lambda bm, bn: pl.BlockSpec((bm, bn), lambda i, j, k: (i, j))
    a_rows = lambda bm, bk: pl.BlockSpec((bm, bk), lambda i, j, k: (i, k))

    def cols_o(Nq):
        def spec(bm, bn):
            q = Nq // bn
            return pl.BlockSpec((None, bm, bn), lambda i, j, k: (j // q, i, j % q))
        return spec

    def du_epilogue(acc, ex_, outs):
        outs[0][...] = (acc * (2.0 * ex_[0][...].astype(F32))).astype(BF)

    dw_down = tn_grad("mlp_down_dw", act, plain_a, dx2b, plain_b, F, D, (F, D), plain_o)
    (du,) = nt_rows("mlp_down_dx", dx2b, wdown, du_epilogue,
                    lambda bm, bn: [(ru, plain_o(bm, bn))], lambda bm, bn: [(_sds((S, F), BF), plain_o(bm, bn))],
                    bn=2048)

    fq = gup.shape[2]
    dw_up = tn_grad("mlp_up_dw", h2, plain_a, du, plain_b, D, F, (N_CHIPS, D, fq), cols_o(fq), bn=min(fq, 1024))
    ex.reduce("mlp", partials=[dw_down.reshape(N_CHIPS, F // N_CHIPS, D), dw_up])
    def norm_bwd_epilogue(acc_ref, ex_, outs):
        gv = ex_[2][...]
        dg = jnp.zeros_like(gv)
        chunk = 128
        for r0 in range(0, acc_ref.shape[0], chunk):
            rows = pl.ds(r0, chunk)
            xv, dh = ex_[0][rows, :], acc_ref[rows, :]
            r = lax.rsqrt(jnp.mean(xv * xv, axis=-1, keepdims=True) + EPS)
            n = xv * r
            dyg = dh * gv
            dx = ex_[1][rows, :] + r * (dyg - n * jnp.mean(dyg * n, axis=-1, keepdims=True))
            outs[0][rows, :] = dx
            outs[1][rows, :] = dx.astype(BF)
            dg = dg + jnp.sum(dh * n, axis=0, keepdims=True)
        outs[2][...] = dg

    vec = pl.BlockSpec((1, D), lambda i, j, k: (0, 0))
    dx1, dx1b, d_norm_mlp = nt_cols(
        "mlp_up_dx", a_rows, du, gup, S, norm_bwd_epilogue,
        lambda bm, bn: [(x1, plain_o(bm, bn)), (dx2, plain_o(bm, bn)), (norm_mlp, vec)],
        lambda bm, bn: [(_sds((S, D), F32), plain_o(bm, bn)), (_sds((S, D), BF), plain_o(bm, bn)),
                        (_sds((S // bm, 1, D), F32), pl.BlockSpec((None, 1, D), lambda i, j, k: (i, 0, 0)))],
        512, bn=2048, bm=512, on_refs=True)
    d_norm_mlp = jnp.sum(d_norm_mlp, axis=0)
    ex.reduce("mlp")

    dpa, dpb, dg3, db_gate, dy_a, dy_b = _out_proj_dx(dx1b, wout, g3, c3, gpa, gpb)
    dw_out = tn_grad("out_proj_dw", merged, plain_a, dx1b, plain_b, D, D, (D, D), plain_o)

    pq = gpa.shape[2]
    dw_pa = tn_grad("proj_a_dw", y_a, plain_a, dpa, plain_b, 512, D, (N_CHIPS, 512, pq), cols_o(pq), bn=min(pq, 512))
    dw_pb = tn_grad("proj_b_dw", y_b, plain_a, dpb, plain_b, 512, D, (N_CHIPS, 512, pq), cols_o(pq), bn=min(pq, 512))
    ex.reduce("mix", partials=[dw_out.reshape(N_CHIPS, D // N_CHIPS, D), dw_pa, dw_pb])

    dqkv3 = lax.empty((3, S, QKV_W), BF)
    dqkv3 = _attn_a_bwd(qkv3, dy_a, y_a, lj, dqkv3, 0, DILATIONS[0])
    ex.reduce("mix")
    dy_views, y_views, lj_views = _dilated_rows("attn_a_bwd_rows", [dy_a, y_a, lj])
    dqkv_views = {d: _attn_a_bwd(qkv_views[d], dy_views[d], y_views[d], lj_views[d], None, grp, d)
                  for grp, d in enumerate(DILATIONS) if grp > 0}
    dqkv3 = _qkv_views("dqkv_from_views", dqkv3, dqkv_views)
    dqkv3, de2 = _attn_b_bwd(qkv3, e2, dy_b, y_b, lse_b, dqkv3)
    d_rpb = _table_grad_to_rpb(de2)

    def stacked_a(width):
        def spec(bm, bk):
            q = width // bk
            return pl.BlockSpec((None, bm, bk), lambda i, j, k: (k // q, i, k % q))
        return spec

    def stacked_b(width):
        def spec(bk, bn):
            q = width // bn
            return pl.BlockSpec((None, bk, bn), lambda i, j, k: (j // q, k, j % q))
        return spec

    ex.reduce("mlp")
    dw_qkv = tn_grad("qkv_dw", h1, plain_a, dqkv3, stacked_b(QKV_W), D, 3 * QKV_W, (N_CHIPS,) + gq.shape[1:],
                     cols_o(gq.shape[2]), bn=512)
    dw_gate = tn_grad("gate_dw", h1, plain_a, dg3, stacked_b(D), D, 2 * D, (N_CHIPS,) + gg.shape[1:],
                      cols_o(gg.shape[2]), bn=gg.shape[2])
    ex.reduce("in", partials=[dw_qkv, dw_gate])
    ex.reduce("mlp")
    dh1_q, *ex.updates["down"] = nt_cols(
        "qkv_dx", stacked_a(QKV_W), dqkv3, gq, S, _store(F32), lambda bm, bn: [],
        lambda bm, bn: [(_sds((S, D), F32), plain_o(bm, bn))], 512, bn=2048, side=ex.adamw_beside("down"))
    ex.reduce("in")
    ex.reduce("mix")

    def add_epilogue(acc, ex_, outs):
        outs[0][...] = acc + ex_[0][...]

    dh1, *ex.updates["up"] = nt_cols(
        "gate_dx", stacked_a(D), dg3, gg, S, add_epilogue, lambda bm, bn: [(dh1_q, plain_o(bm, bn))],
        lambda bm, bn: [(_sds((S, D), F32), plain_o(bm, bn))], gg.shape[2], side=ex.adamw_beside("up"))
    grad_x, _, d_norm_mix = _rms_bwd("rms_mix_bwd", dh1, x, norm_mix, dx1)
    ex.reduce("mix")

    small = [d_norm_mix, jnp.sum(db_gate, axis=0).reshape(1, 2 * D), d_rpb, d_norm_mlp, d_norm_final]
    return loss, grad_x, small


def _pack_small(parts, width):
    flat = jnp.concatenate([p.reshape(-1) for p in parts])
    return jnp.pad(flat, (0, 8 * width - flat.shape[0])).reshape(8, width)


def kernel(x, norm_mix, w_qkv, w_gate, b_gate, rpb, w_proj_a, w_proj_b, w_out, norm_mlp, w_up, w_down, norm_final, loss_target, m_norm_mix, m_w_qkv, m_w_gate, m_b_gate, m_rpb, m_w_proj_a, m_w_proj_b, m_w_out, m_norm_mlp, m_w_up, m_w_down, m_norm_final, v_norm_mix, v_w_qkv, v_w_gate, v_b_gate, v_rpb, v_w_proj_a, v_w_proj_b, v_w_out, v_norm_mlp, v_w_up, v_w_down, v_norm_final):
    names = ["qkv", "gate", "proj_a", "proj_b", "out", "up", "down"]
    big = dict(zip(names, [w_qkv[0], w_gate[0], w_proj_a[0], w_proj_b[0], w_out[0], w_up[0], w_down[0]]))
    big_m = dict(zip(names, [m_w_qkv[0], m_w_gate[0], m_w_proj_a[0], m_w_proj_b[0], m_w_out[0], m_w_up[0], m_w_down[0]]))
    big_v = dict(zip(names, [v_w_qkv[0], v_w_gate[0], v_w_proj_a[0], v_w_proj_b[0], v_w_out[0], v_w_up[0], v_w_down[0]]))

    c = lax.axis_index("c").astype(jnp.int32).reshape(1)
    me = (2 * lax.axis_index("x") + lax.axis_index("y")).astype(jnp.int32).reshape(1)
    ORDER.last = None
    ex = _Exchange(big, me, c, {n: (big_m[n], big_v[n]) for n in names})
    loss, grad_x, small = _forward_backward(x[0], loss_target[0], norm_mix, b_gate, rpb[0], norm_mlp, norm_final, ex)

    def adamw(group):
        return {n: ex.updates[n] if ex.updates.get(n) else _adamw(f"adamw_{n}", big[n], ex.grads[n], big_m[n], big_v[n])
                for n in _Exchange.REDUCE[group]}

    big_out = {**adamw("mlp"), **adamw("mix")}
    ex.reduce("in")

    small_w = [norm_mix, b_gate, rpb, norm_mlp, norm_final]
    count = sum(int(np.prod(p.shape)) for p in small_w)
    width = -(-count // (8 * 128)) * 128
    packed = _adamw_small(_gather_small(_pack_small(small, width)), _pack_small(small_w, width),
                          _pack_small([m_norm_mix, m_b_gate, m_rpb, m_norm_mlp, m_norm_final], width),
                          _pack_small([v_norm_mix, v_b_gate, v_rpb, v_norm_mlp, v_norm_final], width))
    ex.reduce("in")
    big_out.update(adamw("in"))

    def unpack(flat2d):
        flat, out, at = flat2d.reshape(-1), [], 0
        for p in small_w:
            size = int(np.prod(p.shape))
            out.append(flat[at:at + size].reshape(p.shape))
            at += size
        return out

    small_out = [unpack(a) for a in packed]

    def ordered(kind):
        sm = small_out[kind]
        bg = {n: o[kind][None] for n, o in big_out.items()}
        return [sm[0], bg["qkv"], bg["gate"], sm[1], sm[2], bg["proj_a"], bg["proj_b"], bg["out"], sm[3],
                bg["up"], bg["down"], sm[4]]

    total = lax.psum(loss[0, 0], ("x", "y", "c"))
    return (total, grad_x[None], *ordered(0), *ordered(1), *ordered(2), *ordered(3))
```

```python
import math

import numpy as np
import jax
import jax.numpy as jnp
from jax import lax
from jax.experimental import pallas as pl
from jax.experimental.pallas import tpu as pltpu

BF = jnp.bfloat16
F32 = jnp.float32
MESH = pl.DeviceIdType.MESH

HEAD_DIM = 128
N_HEADS = 16
N_HEADS_A = 12
QKV_W = N_HEADS * HEAD_DIM
DILATIONS = (1, 4, 16)
HALF_WINDOW = 64
GRID_W = 64
NA_ROWS = 8
NA_COLS = 16
RPB_ROWS = 2 * NA_ROWS - 1
RPB_COLS = 2 * NA_COLS - 1
EPS = 1e-6
NEG = -1e30
SCALE = HEAD_DIM ** -0.5

ADAM_LR = 0.001
ADAM_B1 = 0.9
ADAM_B2 = 0.999
ADAM_EPS = 1e-08
ADAM_WD = 0.01
ADAM_STEP = 10

N_CHIPS = 4
VMEM_LIMIT_BYTES = 48 * 1024 * 1024
QB = 256
NBR_SIDE = 32
ROW_TILE = 512


def _key_rows(L):
    return min(QB + 2 * HALF_WINDOW, L)


def _cparams(sem=None):
    return pltpu.CompilerParams(dimension_semantics=sem, vmem_limit_bytes=VMEM_LIMIT_BYTES)


def _tile(dim, want):
    t = min(dim, want)
    assert dim % t == 0, (dim, want)
    return t


class _ProgramOrder:
    def __init__(self):
        self.last = None

    def call(self, body, operands, in_specs, *, prefetch=(), grid=None, out_specs=None, chain_output=0, **kwargs):
        operands, in_specs = list(operands), list(in_specs)
        lead = len(prefetch) + len(operands)
        if self.last is not None and not any(op is self.last for op in operands):
            operands.append(self.last)
            in_specs.append(pl.BlockSpec(memory_space=pl.ANY))
            inner = body

            def body(*refs):
                return inner(*refs[:lead], *refs[lead + 1:])

        if prefetch:
            kwargs["grid_spec"] = pltpu.PrefetchScalarGridSpec(
                num_scalar_prefetch=len(prefetch), grid=grid, in_specs=in_specs, out_specs=out_specs)
        else:
            kwargs.update(in_specs=in_specs, out_specs=out_specs)
            if grid is not None:
                kwargs["grid"] = grid
        out = pl.pallas_call(body, **kwargs)(*prefetch, *operands)
        self.last = out[chain_output] if isinstance(out, (tuple, list)) else out
        return out


ORDER = _ProgramOrder()


NN = ((1,), (0,))
NT = ((1,), (1,))
TN = ((0,), (0,))


def _matmul(name, a, b, a_spec, b_spec, dims, grid, acc_shape, extras, outs, epilogue, precision=None,
            prefetch=(), into=None, side=None):
    n_ex, n_out, nk = len(extras), len(outs), grid[2]
    side_fn, side_in, n_side_out = side if side is not None else (None, [], 0)
    side_spec = None
    n_in = 2 + n_ex + len(side_in) + (into is not None)
    if side is not None:
        R, C = side_in[0].shape
        steps = grid[0] * grid[1] * grid[2]
        side_blocks = max(n for n in range(1, steps + 1) if R % n == 0 and (R // n) % 8 == 0)

        def side_step(*ids):
            return (ids[0] * grid[1] + ids[1]) * grid[2] + ids[2]

        side_spec = pl.BlockSpec((R // side_blocks, C),
                                 lambda *ids: (jnp.minimum(side_step(*ids), side_blocks - 1), 0))

    def body(*refs):
        refs = refs[len(prefetch):]
        a_ref, b_ref = refs[0], refs[1]
        ex_refs = refs[2:2 + n_ex]
        out_refs = refs[n_in:n_in + n_out]
        if side is not None:
            @pl.when(side_step(pl.program_id(0), pl.program_id(1), pl.program_id(2)) < side_blocks)
            def _():
                results = side_fn(*[r[...] for r in refs[2 + n_ex:2 + n_ex + len(side_in)]])
                for o_ref, value in zip(refs[n_in + n_out:n_in + n_out + n_side_out], results):
                    o_ref[...] = value

        def dot():
            return lax.dot_general(a_ref[...], b_ref[...], (dims, ((), ())),
                                   preferred_element_type=F32, precision=precision)

        if nk == 1:
            epilogue(dot(), ex_refs, out_refs)
            return
        acc_ref = refs[-1]
        k = pl.program_id(2)

        @pl.when(k == 0)
        def _():
            acc_ref[...] = dot()

        if nk > 2:
            @pl.when((k > 0) & (k < nk - 1))
            def _():
                acc_ref[...] += dot()

        @pl.when(k == nk - 1)
        def _():
            epilogue(acc_ref[...] + dot(), ex_refs, out_refs)

    operands = [a, b] + [e for e, _ in extras] + list(side_in)
    in_specs = [a_spec, b_spec] + [s for _, s in extras] + [side_spec] * len(side_in)
    kwargs = {}
    if into is not None:
        operands.append(into)
        in_specs.append(pl.BlockSpec(memory_space=pl.ANY))
        kwargs["input_output_aliases"] = {len(prefetch) + n_in - 1: 0}
    return ORDER.call(
        body, operands, in_specs, prefetch=prefetch, name=name, grid=grid,
        out_specs=[s for _, s in outs] + [side_spec] * n_side_out,
        out_shape=[sh for sh, _ in outs] + [_sds(s_.shape, F32) for s_ in side_in[:1]] * n_side_out,
        scratch_shapes=[pltpu.VMEM(acc_shape, F32)] if nk > 1 else [],
        compiler_params=_cparams(("parallel", "parallel", "arbitrary")), **kwargs,
    )


def _mm_nn_shards(name, a, w, me, own, out, out_block, epilogue, extras=(), into=None, tn=512, tm=1024):
    M, K = a.shape
    Nq = w.shape[-1]
    tm, tn = _tile(M, tm), _tile(Nq, tn)
    q = Nq // tn

    def tile(j, me_ref):
        shard = me_ref[0] if own else (me_ref[0] + 1 + j // q) % N_CHIPS
        return shard, j % q, shard * q + j % q

    if own:
        b_spec = pl.BlockSpec((K, tn), lambda i, j, k, me_ref: (0, j))
    else:
        b_spec = pl.BlockSpec((None, K, tn), lambda i, j, k, me_ref: (tile(j, me_ref)[0], 0, tile(j, me_ref)[1]))
    shape, dtype = out
    out_spec = pl.BlockSpec((None, tm, tn), lambda i, j, k, me_ref: out_block(i, tile(j, me_ref)[2]))
    ex = [(e, pl.BlockSpec((1, tn), lambda i, j, k, me_ref: (0, tile(j, me_ref)[2]))) for e in extras]
    return _matmul(name, a, w, pl.BlockSpec((tm, K), lambda i, j, k, me_ref: (i, 0)), b_spec, NN,
                   (M // tm, q if own else (N_CHIPS - 1) * q, 1), (tm, tn), ex, [(_sds(shape, dtype), out_spec)],
                   epilogue, prefetch=(me,), into=into)[0]


def _store(dtype):
    def epilogue(acc, ex, outs):
        outs[0][...] = acc.astype(dtype)
    return epilogue


def _sds(shape, dtype):
    return jax.ShapeDtypeStruct(shape, dtype)


def _mm_nn_cols(name, a, g, out_dtype, epilogue=None, extras=(), outs=None, tm=1024, tn=1024, tk=2048):
    M, K = a.shape
    _, _, Nq = g.shape
    tm, tn, tk = _tile(M, tm), _tile(Nq, tn), _tile(K, tk)
    q = Nq // tn
    grid = (M // tm, N_CHIPS * q, K // tk)
    if outs is None:
        outs = [(_sds((M, N_CHIPS * Nq), out_dtype), pl.BlockSpec((tm, tn), lambda i, j, k: (i, j)))]
    return _matmul(name, a, g, pl.BlockSpec((tm, tk), lambda i, j, k: (i, k)),
                   pl.BlockSpec((None, tk, tn), lambda i, j, k: (j // q, k, j % q)), NN, grid, (tm, tn),
                   list(extras), outs, epilogue or _store(out_dtype)), (tm, tn, tk)


def _rms_fwd(name, x, g):
    S, D = x.shape
    tm = _tile(S, ROW_TILE)

    def body(x_ref, g_ref, h_ref):
        xv = x_ref[...]
        r = lax.rsqrt(jnp.mean(xv * xv, axis=-1, keepdims=True) + EPS)
        h_ref[...] = ((xv * r) * g_ref[...]).astype(BF)

    row = pl.BlockSpec((tm, D), lambda i: (i, 0))
    return ORDER.call(
        body, [x, g], [row, pl.BlockSpec((1, D), lambda i: (0, 0))], name=name, grid=(S // tm,),
        out_specs=row, out_shape=_sds((S, D), BF), compiler_params=_cparams(("parallel",)),
    )


def _rms_bwd(name, dh, x, g, dres):
    S, D = x.shape
    tm = _tile(S, ROW_TILE // 2)

    def body(dh_ref, x_ref, g_ref, dres_ref, dx_ref, dxb_ref, dg_ref):
        xv = x_ref[...]
        r = lax.rsqrt(jnp.mean(xv * xv, axis=-1, keepdims=True) + EPS)
        n = xv * r
        dhv = dh_ref[...]
        dyg = dhv * g_ref[...]
        dx = dres_ref[...] + r * (dyg - n * jnp.mean(dyg * n, axis=-1, keepdims=True))
        dx_ref[...] = dx
        dxb_ref[...] = dx.astype(BF)

        @pl.when(pl.program_id(0) == 0)
        def _():
            dg_ref[...] = jnp.zeros_like(dg_ref)

        dg_ref[...] += jnp.sum(dhv * n, axis=0, keepdims=True)

    row = pl.BlockSpec((tm, D), lambda i: (i, 0))
    vec = pl.BlockSpec((1, D), lambda i: (0, 0))
    return ORDER.call(
        body, [dh, x, g, dres], [row, row, vec, row], name=name, grid=(S // tm,),
        out_specs=[row, row, vec],
        out_shape=[_sds((S, D), F32), _sds((S, D), BF), _sds((1, D), F32)],
        compiler_params=_cparams(("arbitrary",)),
    )


def _loss_head(x2, target, g):
    S, D = x2.shape
    tm = _tile(S, ROW_TILE)

    def body(x_ref, t_ref, g_ref, loss_ref, dx_ref, dxb_ref, dg_ref):
        xv = x_ref[...]
        gv = g_ref[...]
        r = lax.rsqrt(jnp.mean(xv * xv, axis=-1, keepdims=True) + EPS)
        n = xv * r
        e = n * gv - t_ref[...]
        dy = e * (1.0 / D)
        dyg = dy * gv
        dx = r * (dyg - n * jnp.mean(dyg * n, axis=-1, keepdims=True))
        dx_ref[...] = dx
        dxb_ref[...] = dx.astype(BF)

        @pl.when(pl.program_id(0) == 0)
        def _():
            dg_ref[...] = jnp.zeros_like(dg_ref)
            loss_ref[...] = jnp.zeros_like(loss_ref)

        dg_ref[...] += jnp.sum(dy * n, axis=0, keepdims=True)
        per_row = jnp.mean(e * e, axis=-1, keepdims=True)
        loss_ref[...] += 0.5 * jnp.sum(per_row, axis=0, keepdims=True)

    row = pl.BlockSpec((tm, D), lambda i: (i, 0))
    vec = pl.BlockSpec((1, D), lambda i: (0, 0))
    return ORDER.call(
        body, [x2, target, g], [row, row, vec], name="loss_head", grid=(S // tm,),
        out_specs=[pl.BlockSpec((1, 1), lambda i: (0, 0)), row, row, vec],
        out_shape=[_sds((1, 1), F32), _sds((S, D), F32), _sds((S, D), BF), _sds((1, D), F32)],
        compiler_params=_cparams(("arbitrary",)), chain_output=1,
    )


def _chains(L):
    side = min(8, L // QB)
    return side, max(1, 4 // side)


def _band_scores(qkv_ref, i, L, coef, head):
    KB = _key_rows(L)
    lanes = pl.ds(head * HEAD_DIM, HEAD_DIM)
    q0 = pl.multiple_of(i * QB, QB)
    ks = pl.multiple_of(jnp.clip(i * QB - HALF_WINDOW, 0, L - KB), HALF_WINDOW)
    q = qkv_ref[0, pl.ds(q0, QB), lanes]
    k = qkv_ref[1, pl.ds(ks, KB), lanes]
    v = qkv_ref[2, pl.ds(ks, KB), lanes]
    s = lax.dot_general(q, k, (NT, ((), ())), preferred_element_type=F32) * SCALE
    qpos = q0 + lax.broadcasted_iota(jnp.int32, (QB, KB), 0)
    kpos = ks + lax.broadcasted_iota(jnp.int32, (QB, KB), 1)
    rel = jnp.abs(kpos - qpos)
    valid = rel <= HALF_WINDOW
    s = jnp.where(valid, s - coef * rel.astype(F32), NEG)
    return q0, ks, q, k, v, s, valid


def _alibi_coefs(group, d, heads):
    first = 4 * group + 1 + pl.program_id(1) * heads
    scale = jnp.full((1, 1), -(8.0 / N_HEADS_A) * math.log(2.0), F32)
    return [jnp.exp(scale * (first + hh).astype(F32)) * float(d) for hh in range(heads)]


def _dilated_view(qkv3, group, d, heads):
    per = 4 // heads
    L = qkv3.shape[1]
    if d == 1:
        return qkv3, pl.BlockSpec((3, L, heads * HEAD_DIM), lambda r, j: (0, 0, per * group + j))
    return qkv3, pl.BlockSpec((3, L, heads * HEAD_DIM), lambda r, j: (0, 0, r * per + j))


def _qkv_views(name, qkv3, views=None):
    _, S, _ = qkv3.shape
    W = 512
    tm = _tile(S, 2 * ROW_TILE)
    dilated = [(g, d) for g, d in enumerate(DILATIONS) if d > 1]
    first = dilated[0][0]
    assert [g for g, _ in dilated] == list(range(first, first + len(dilated)))
    nc = W // 128
    to_views = views is None

    def body(*refs):
        scr = refs[-nc:]
        if to_views:
            src, outs = refs[0], refs[1:1 + len(dilated)]
        else:
            ins, dst = refs[:len(dilated)], refs[len(dilated) + 1]
        for k, (_, d) in enumerate(dilated):
            @pl.when(pl.program_id(1) == k)
            def _():
                for w in range(3):
                    for c in range(nc):
                        if to_views:
                            scr[c][...] = src[w, :, c * 128:(c + 1) * 128].astype(F32)
                    for r in range(d):
                        for c in range(nc):
                            at = r * W + c * 128
                            if to_views:
                                outs[k][w, :, at:at + 128] = scr[c][pl.ds(r, tm // d, stride=d), :].astype(BF)
                            else:
                                scr[c][pl.ds(r, tm // d, stride=d), :] = ins[k][w, :, at:at + 128].astype(F32)
                    for c in range(nc):
                        if not to_views:
                            dst[w, :, c * 128:(c + 1) * 128] = scr[c][...].astype(BF)

    cols = pl.BlockSpec((3, tm, W), lambda i, k: (0, i, first + k))
    rows = [pl.BlockSpec((3, tm // d, d * W), lambda i, k: (0, i, 0)) for _, d in dilated]
    shapes = [_sds((3, S // d, d * W), BF) for _, d in dilated]
    common = dict(name=name, grid=(S // tm, len(dilated)), scratch_shapes=[pltpu.VMEM((tm, 128), F32)] * nc,
                  compiler_params=_cparams(("parallel", "arbitrary")))
    if to_views:
        outs = ORDER.call(body, [qkv3], [cols], out_specs=rows, out_shape=shapes, **common)
        return {d: o for (_, d), o in zip(dilated, outs)}
    return ORDER.call(body, [views[d] for _, d in dilated] + [qkv3], rows + [pl.BlockSpec(memory_space=pl.ANY)],
                      out_specs=cols, out_shape=_sds(qkv3.shape, BF), input_output_aliases={len(dilated): 0}, **common)


def _attn_a_fwd(qkv3, group, d):
    L = qkv3.shape[1]
    S = L * d
    assert L % QB == 0
    side, heads = _chains(L)
    view, blocks_spec = _dilated_view(qkv3, group, d, heads)

    def body(qkv_ref, o_ref, lse_ref):
        coefs = _alibi_coefs(group, d, heads)

        def step(i, carry):
            chains = [(hh, _band_scores(qkv_ref, side * i + u, L, coefs[hh], hh))
                      for u in range(side) for hh in range(heads)]
            soft = []
            for hh, (q0, _, _, _, v, s, _) in chains:
                m = jnp.max(s, axis=-1, keepdims=True)
                p = jnp.exp(s - m)
                den = jnp.sum(p, axis=-1, keepdims=True)
                soft.append((hh, q0, (p / den).astype(BF), v, m + jnp.log(den)))
            for hh, q0, pn, v, lse in soft:
                lanes = pl.ds(hh * HEAD_DIM, HEAD_DIM)
                o_ref[pl.ds(q0, QB), lanes] = jnp.dot(pn, v, preferred_element_type=F32)
                lse_ref[pl.ds(q0, QB), lanes] = jnp.broadcast_to(lse, (QB, HEAD_DIM))
            return carry

        lax.fori_loop(0, L // QB // side, step, 0)

    per = 4 // heads
    out = pl.BlockSpec((L, heads * HEAD_DIM), lambda r, j: (0, r * per + j))
    o, lse = ORDER.call(
        body, [view], [blocks_spec],
        name=f"attn_a_fwd_d{d}", grid=(d, per),
        out_specs=[out, out],
        out_shape=[_sds((L, d * 512), F32), _sds((L, d * 512), F32)],
        compiler_params=_cparams(("parallel", "parallel")),
    )
    return o, lse


def _dilated_rows(name, arrays):
    S, W = arrays[0].shape
    tm = _tile(S, ROW_TILE)
    ds_ = [d for d in DILATIONS if d > 1]
    n = len(arrays)

    def body(*refs):
        nc = W // 128
        ins, outs, scr = refs[:n], refs[n:-nc], refs[-nc:]
        for a, src in enumerate(ins):
            for c in range(nc):
                scr[c][...] = src[:, c * 128:(c + 1) * 128].astype(F32)
            for k, d in enumerate(ds_):
                dst = outs[a * len(ds_) + k]
                for r in range(d):
                    for c in range(nc):
                        at = r * W + c * 128
                        dst[:, at:at + 128] = scr[c][pl.ds(r, tm // d, stride=d), :].astype(dst.dtype)

    row = pl.BlockSpec((tm, W), lambda i: (i, 0))
    out_specs, out_shape = [], []
    for a in arrays:
        for d in ds_:
            out_specs.append(pl.BlockSpec((tm // d, d * W), lambda i: (i, 0)))
            out_shape.append(_sds((S // d, d * W), a.dtype))
    outs = ORDER.call(body, list(arrays), [row] * n, name=name, grid=(S // tm,), out_specs=out_specs,
                      out_shape=out_shape, scratch_shapes=[pltpu.VMEM((tm, 128), F32)] * (W // 128),
                      compiler_params=_cparams(("parallel",)))
    return [{d: outs[a * len(ds_) + k] for k, d in enumerate(ds_)} for a in range(n)]


def _attn_a_combine(os_, lses):
    W = 512
    S = os_[0].shape[0] * DILATIONS[0]
    tm = _tile(S, ROW_TILE)
    nc = W // 128
    dilated = [g for g, d in enumerate(DILATIONS) if d > 1]

    def body(o0, o1, o2, l0, l1, l2, y_ref, lj_ref, *scr):
        def token_order(src, g, slot):
            d = DILATIONS[g]
            if d == 1:
                return src[...]
            bufs = scr[slot * nc:(slot + 1) * nc]
            for r in range(d):
                for c in range(nc):
                    at = r * W + c * 128
                    bufs[c][pl.ds(r, tm // d, stride=d), :] = src[:, at:at + 128]
            return jnp.concatenate([buf[...] for buf in bufs], axis=1)

        slots = {g: k for k, g in enumerate(dilated)}
        ls = [token_order(l, g, slots.get(g, 0)) for g, l in enumerate((l0, l1, l2))]
        os_tok = [token_order(o, g, len(dilated) + slots.get(g, 0)) for g, o in enumerate((o0, o1, o2))]
        m = jnp.maximum(jnp.maximum(ls[0], ls[1]), ls[2])
        es = [jnp.exp(l - m) for l in ls]
        den = es[0] + es[1] + es[2]
        y = (es[0] / den) * os_tok[0] + (es[1] / den) * os_tok[1] + (es[2] / den) * os_tok[2]
        y_ref[...] = y.astype(BF)
        lj_ref[...] = m + jnp.log(den)

    row = pl.BlockSpec((tm, W), lambda i: (i, 0))
    views = [pl.BlockSpec((tm // d, d * W), lambda i: (i, 0)) for d in DILATIONS]
    return ORDER.call(
        body, [*os_, *lses], views + views, name="attn_a_combine", grid=(S // tm,), out_specs=[row, row],
        out_shape=[_sds((S, W), BF), _sds((S, W), F32)],
        scratch_shapes=[pltpu.VMEM((tm, 128), F32)] * (2 * len(dilated) * nc),
        compiler_params=_cparams(("parallel",)),
    )


def _attn_a_bwd(qkv3, dy, y, lj, dqkv3, group, d):
    L = qkv3.shape[1]
    S = L * d
    side, heads = _chains(L)
    view, blocks_spec = _dilated_view(qkv3, group, d, heads)

    def body(qkv_ref, dy_ref, y_ref, lj_ref, *rest):
        out_ref, dk_acc, dv_acc = rest[-3:]
        coefs = _alibi_coefs(group, d, heads)
        dk_acc[...] = jnp.zeros_like(dk_acc)
        dv_acc[...] = jnp.zeros_like(dv_acc)

        def step(i, carry):
            chains = [(pl.ds(hh * HEAD_DIM, HEAD_DIM), _band_scores(qkv_ref, side * i + u, L, coefs[hh], hh))
                      for u in range(side) for hh in range(heads)]
            dys = [dy_ref[pl.ds(c[0], QB), lanes] for lanes, c in chains]
            dps = [lax.dot_general(dyv, c[4], (NT, ((), ())), preferred_element_type=F32)
                   for dyv, (_, c) in zip(dys, chains)]
            grads = []
            for (lanes, (q0, ks, q, k, v, s, valid)), dyv, dp in zip(chains, dys, dps):
                rows = pl.ds(q0, QB)
                delta = jnp.sum(dyv.astype(F32) * y_ref[rows, lanes].astype(F32), axis=-1, keepdims=True)
                p = jnp.where(valid, jnp.exp(s - jnp.tile(lj_ref[rows, lanes], (1, _key_rows(L) // HEAD_DIM))), 0.0)
                grads.append(((p * (dp - delta)).astype(BF), p.astype(BF)))
            for (lanes, (q0, ks, q, k, v, s, valid)), dyv, (ds, pb) in zip(chains, dys, grads):
                out_ref[0, pl.ds(q0, QB), lanes] = (jnp.dot(ds, k, preferred_element_type=F32) * SCALE).astype(BF)
                keys = pl.ds(ks, _key_rows(L))
                dk_acc[keys, lanes] += lax.dot_general(ds, q, (TN, ((), ())), preferred_element_type=F32) * SCALE
                dv_acc[keys, lanes] += lax.dot_general(pb, dyv, (TN, ((), ())), preferred_element_type=F32)
            return carry

        lax.fori_loop(0, L // QB // side, step, 0)
        out_ref[1] = dk_acc[...].astype(BF)
        out_ref[2] = dv_acc[...].astype(BF)

    per = 4 // heads
    width = heads * HEAD_DIM
    row = pl.BlockSpec((L, width), lambda r, j: (0, r * per + j))
    operands = [view, dy, y, lj]
    scratch = [pltpu.VMEM((L, width), F32), pltpu.VMEM((L, width), F32)]
    if d == 1:
        return ORDER.call(
            body, operands + [dqkv3], [blocks_spec, row, row, row, pl.BlockSpec(memory_space=pl.ANY)],
            name=f"attn_a_bwd_d{d}", grid=(d, per), out_specs=blocks_spec, out_shape=_sds((3, S, QKV_W), BF),
            scratch_shapes=scratch, input_output_aliases={4: 0}, compiler_params=_cparams(("parallel", "parallel")))
    return ORDER.call(
        body, operands, [blocks_spec, row, row, row], name=f"attn_a_bwd_d{d}", grid=(d, per),
        out_specs=blocks_spec, out_shape=_sds((3, L, d * 512), BF),
        scratch_shapes=scratch, compiler_params=_cparams(("parallel", "parallel")))


def _toeplitz_onehot():
    oh = np.zeros((64, GRID_W, 128), np.float32)
    for qc in range(GRID_W):
        for m in range(128):
            kc = m % GRID_W
            dc = int(np.clip(kc - qc, -(NA_COLS - 1), NA_COLS - 1)) + NA_COLS - 1
            oh[(m // GRID_W) * 32 + dc, qc, m] = 1.0
    return oh.reshape(64, GRID_W * 128)


def _nbr_scores(qkv_ref, e2_ref, r, rows, ok):
    rs = jnp.clip(r - NA_ROWS // 2, 0, rows - NA_ROWS)
    q0 = pl.multiple_of(r * GRID_W, GRID_W)
    k0 = pl.multiple_of(rs * GRID_W, GRID_W)
    q = qkv_ref[0, pl.ds(q0, GRID_W), :]
    k = qkv_ref[1, pl.ds(k0, NA_ROWS * GRID_W), :]
    v = qkv_ref[2, pl.ds(k0, NA_ROWS * GRID_W), :]
    s = lax.dot_general(q, k, (NT, ((), ())), preferred_element_type=F32) * SCALE
    first = rs - r + NA_ROWS - 1
    bias = jnp.concatenate([e2_ref[first + 2 * pair] for pair in range(NA_ROWS // 2)], axis=1)
    s = jnp.where(ok, s + bias, NEG)
    return q0, k0, first, q, k, v, s


def _nbr_col_ok():
    qc = lax.broadcasted_iota(jnp.int32, (GRID_W, NA_ROWS * GRID_W), 0)
    kc = lax.broadcasted_iota(jnp.int32, (GRID_W, NA_ROWS * GRID_W), 1) % GRID_W
    cs = jnp.clip(qc - NA_COLS // 2, 0, GRID_W - NA_COLS)
    return (kc >= cs) & (kc < cs + NA_COLS)


def _attn_b_fwd(qkv3, e2):
    _, S, _ = qkv3.shape
    rows = S // GRID_W
    assert rows >= NA_ROWS

    def body(qkv_ref, e2_ref, o_ref, lse_ref):
        ok = _nbr_col_ok()

        def step(i, carry):
            blocks = [_nbr_scores(qkv_ref, e2_ref, NBR_SIDE * i + u, rows, ok) for u in range(NBR_SIDE)]
            soft = []
            for q0, _, _, _, _, v, s in blocks:
                m = jnp.max(s, axis=-1, keepdims=True)
                p = jnp.exp(s - m)
                den = jnp.sum(p, axis=-1, keepdims=True)
                soft.append((q0, (p / den).astype(BF), v, m + jnp.log(den)))
            for q0, pn, v, lse in soft:
                o_ref[pl.ds(q0, GRID_W), :] = jnp.dot(pn, v, preferred_element_type=F32).astype(BF)
                lse_ref[pl.ds(q0, GRID_W), :] = jnp.broadcast_to(lse, (GRID_W, HEAD_DIM))
            return carry

        lax.fori_loop(0, rows // NBR_SIDE, step, 0)

    out = pl.BlockSpec((S, HEAD_DIM), lambda h: (0, h))
    return ORDER.call(
        body, [qkv3, e2],
        [pl.BlockSpec((3, S, HEAD_DIM), lambda h: (0, 0, N_HEADS_A + h)),
         pl.BlockSpec((None, RPB_ROWS - 1, GRID_W, 128), lambda h: (h, 0, 0, 0))],
        name="attn_b_fwd", grid=(4,),
        out_specs=[out, out], out_shape=[_sds((S, 512), BF), _sds((S, 512), F32)],
        compiler_params=_cparams(("parallel",)),
    )


def _attn_b_bwd(qkv3, e2, dy, y, lse, dqkv3):
    _, S, _ = qkv3.shape
    rows = S // GRID_W
    nk = NA_ROWS * GRID_W

    def body(qkv_ref, e2_ref, dy_ref, y_ref, lse_ref, _, out_ref, de2_ref, dk_acc, dv_acc):
        ok = _nbr_col_ok()
        dk_acc[...] = jnp.zeros_like(dk_acc)
        dv_acc[...] = jnp.zeros_like(dv_acc)
        de2_ref[...] = jnp.zeros_like(de2_ref)

        def step(i, carry):
            blocks = [_nbr_scores(qkv_ref, e2_ref, NBR_SIDE * i + u, rows, ok) for u in range(NBR_SIDE)]
            dys = [dy_ref[pl.ds(b[0], GRID_W), :] for b in blocks]
            dps = [lax.dot_general(dyv, b[5], (NT, ((), ())), preferred_element_type=F32) for dyv, b in zip(dys, blocks)]
            grads = []
            for (q0, k0, first, q, k, v, s), dyv, dp in zip(blocks, dys, dps):
                qrows = pl.ds(q0, GRID_W)
                delta = jnp.sum(dyv.astype(F32) * y_ref[qrows, :].astype(F32), axis=-1, keepdims=True)
                p = jnp.where(ok, jnp.exp(s - jnp.tile(lse_ref[qrows, :], (1, nk // HEAD_DIM))), 0.0)
                ds = p * (dp - delta)
                for pair in range(NA_ROWS // 2):
                    de2_ref[first + 2 * pair] += ds[:, pair * 128:(pair + 1) * 128]
                grads.append((ds.astype(BF), p.astype(BF)))
            for (q0, k0, first, q, k, v, s), dyv, (dsb, pb) in zip(blocks, dys, grads):
                out_ref[0, pl.ds(q0, GRID_W), :] = (jnp.dot(dsb, k, preferred_element_type=F32) * SCALE).astype(BF)
                keys = pl.ds(k0, nk)
                dk_acc[keys, :] += lax.dot_general(dsb, q, (TN, ((), ())), preferred_element_type=F32) * SCALE
                dv_acc[keys, :] += lax.dot_general(pb, dyv, (TN, ((), ())), preferred_element_type=F32)
            return carry

        lax.fori_loop(0, rows // NBR_SIDE, step, 0)
        out_ref[1] = dk_acc[...].astype(BF)
        out_ref[2] = dv_acc[...].astype(BF)

    heads = pl.BlockSpec((3, S, HEAD_DIM), lambda h: (0, 0, N_HEADS_A + h))
    row = pl.BlockSpec((S, HEAD_DIM), lambda h: (0, h))
    table = pl.BlockSpec((None, RPB_ROWS - 1, GRID_W, 128), lambda h: (h, 0, 0, 0))
    return ORDER.call(
        body, [qkv3, e2, dy, y, lse, dqkv3],
        [heads, table, row, row, row, pl.BlockSpec(memory_space=pl.ANY)], name="attn_b_bwd", grid=(4,),
        out_specs=[heads, table],
        out_shape=[_sds((3, S, QKV_W), BF), _sds((4, RPB_ROWS - 1, GRID_W, 128), F32)],
        scratch_shapes=[pltpu.VMEM((S, HEAD_DIM), F32), pltpu.VMEM((S, HEAD_DIM), F32)],
        input_output_aliases={5: 0},
        compiler_params=_cparams(("parallel",)), chain_output=1,
    )


def _rpb_to_table(rpb):
    pad = jnp.pad(rpb, ((0, 0), (0, 0), (0, 1)))
    pairs = jnp.concatenate([pad[:, :-1], pad[:, 1:]], axis=-1).reshape(4 * (RPB_ROWS - 1), 64)
    onehot = jnp.asarray(_toeplitz_onehot())
    n = onehot.shape[1]
    tn = 2048
    full = lambda i, j, k: (0, 0)
    (e2,) = _matmul("rpb_table", pairs, onehot, pl.BlockSpec(pairs.shape, full),
                    pl.BlockSpec((64, tn), lambda i, j, k: (0, j)), NN, (1, n // tn, 1), (pairs.shape[0], tn), [],
                    [(_sds((pairs.shape[0], n), F32), pl.BlockSpec((pairs.shape[0], tn), lambda i, j, k: (0, j)))],
                    _store(F32), precision=lax.Precision.HIGHEST)
    return e2.reshape(4, RPB_ROWS - 1, GRID_W, 128)


def _table_grad_to_rpb(de2):
    onehot = jnp.asarray(_toeplitz_onehot())
    n = onehot.shape[1]
    flat = de2.reshape(4 * (RPB_ROWS - 1), n)
    tk = 2048
    (dpairs,) = _matmul("rpb_table_grad", flat, onehot, pl.BlockSpec((flat.shape[0], tk), lambda i, j, k: (0, k)),
                        pl.BlockSpec((64, tk), lambda i, j, k: (0, k)), NT, (1, 1, n // tk), (flat.shape[0], 64), [],
                        [(_sds((flat.shape[0], 64), F32), pl.BlockSpec((flat.shape[0], 64), lambda i, j, k: (0, 0)))],
                        _store(F32), precision=lax.Precision.HIGHEST)
    dpairs = dpairs.reshape(4, RPB_ROWS - 1, 64)
    zero = jnp.zeros((4, 1, RPB_COLS), F32)
    return (jnp.concatenate([dpairs[:, :, :RPB_COLS], zero], axis=1)
            + jnp.concatenate([zero, dpairs[:, :, 32:32 + RPB_COLS]], axis=1))


HBM = pl.BlockSpec(memory_space=pl.ANY)


def _place():
    x, y, c = lax.axis_index("x"), lax.axis_index("y"), lax.axis_index("c")
    chips = [(1 - x, y), (x, 1 - y), (1 - x, 1 - y)]
    return x, y, c, chips


def _remote(src, dst, send_sem, recv_sem, to):
    return pltpu.make_async_remote_copy(src_ref=src, dst_ref=dst, send_sem=send_sem, recv_sem=recv_sem,
                                        device_id=to, device_id_type=MESH)


def _place_shard(name, w, me, plain=False):
    R, C = w.shape
    tr = _tile(R, 256)

    def body(me_ref, w_ref, *o_refs):
        for o_ref in o_refs:
            o_ref[...] = w_ref[...].astype(BF)

    row = pl.BlockSpec((tr, C), lambda i, mr: (i, 0))
    placed = pl.BlockSpec((None, tr, C), lambda i, mr: (mr[0], i, 0))
    return ORDER.call(
        body, [w], [row], prefetch=(me,), name=name, grid=(R // tr,),
        out_specs=[placed, row] if plain else [placed],
        out_shape=[_sds((N_CHIPS, R, C), BF)] + ([_sds((R, C), BF)] if plain else []),
        compiler_params=_cparams(("parallel",)),
    )


SEM = pl.BlockSpec(memory_space=pltpu.SEMAPHORE)
IN_HBM = pl.BlockSpec(memory_space=pltpu.HBM)
DATAFLOW = pltpu.SideEffectType.DATAFLOW_SIDE_EFFECTING


def _in_hbm(a):
    return pltpu.with_memory_space_constraint(a, pltpu.HBM)


def _copy_start(name, bufs, copies, n_copies, earlier=None):
    n = len(bufs)
    after = None if any(b is ORDER.last for b in bufs) else ORDER.last
    n_extra = (2 if earlier is not None else 0) + (1 if after is not None else 0)

    def body(*refs):
        ins = refs[:n]
        if earlier is not None:
            for k, (src, dst, to) in enumerate(earlier[0](ins)):
                cp = _remote(src, dst, refs[n].at[k], refs[n + 1].at[k], to)
                cp.wait_send()
                cp.wait_recv()
        send_sems, recv_sems = refs[n + n_extra], refs[n + n_extra + 1]
        for k, (src, dst, to) in enumerate(copies(ins)):
            _remote(src, dst, send_sems.at[k], recv_sems.at[k], to).start()
        refs[-1][...] = jnp.zeros((8, 128), F32)

    operands = [_in_hbm(b) for b in bufs]
    in_specs = [IN_HBM] * n
    if earlier is not None:
        operands += [earlier[1], earlier[2]]
        in_specs += [SEM, SEM]
    if after is not None:
        operands.append(after)
        in_specs.append(HBM)
    outs = pl.pallas_call(
        body, name=name,
        out_shape=(pltpu.SemaphoreType.DMA((n_copies,)), pltpu.SemaphoreType.DMA((n_copies,)),
                   *[pltpu.HBM(b.shape, b.dtype) for b in bufs], _sds((8, 128), F32)),
        in_specs=in_specs,
        out_specs=(SEM, SEM, *[IN_HBM] * n, pl.BlockSpec(memory_space=pltpu.VMEM)),
        input_output_aliases={i: 2 + i for i in range(n)},
        compiler_params=pltpu.CompilerParams(has_side_effects=DATAFLOW),
    )(*operands)
    ORDER.last = outs[-1]
    return outs[0], outs[1], list(outs[2:2 + n])


def _copy_wait(name, bufs, copies, send_sems, recv_sems):
    n = len(bufs)
    after = ORDER.last

    def body(*refs):
        ins = refs[:n]
        for k, (src, dst, to) in enumerate(copies(ins)):
            cp = _remote(src, dst, refs[n].at[k], refs[n + 1].at[k], to)
            cp.wait_send()
            cp.wait_recv()

    outs = list(pl.pallas_call(
        body, name=name,
        out_shape=tuple(pltpu.HBM(b.shape, b.dtype) for b in bufs),
        in_specs=[IN_HBM] * n + [SEM, SEM, HBM], out_specs=tuple([IN_HBM] * n),
        input_output_aliases={i: i for i in range(n)},
        compiler_params=pltpu.CompilerParams(has_side_effects=DATAFLOW),
    )(*bufs, send_sems, recv_sems, after))
    ORDER.last = outs[0]
    return outs


def _gather_hop1(bufs):
    x, y, c, chips = _place()
    out = []
    for b in bufs:
        half = b.shape[1] // 2
        mine = b.at[2 * x + y, pl.ds(c * half, half), :]
        out += [(mine, mine, (*chip, c)) for chip in chips]
    return out


def _gather_hop2(bufs):
    x, y, c, chips = _place()
    out = []
    for b in bufs:
        half = b.shape[1] // 2
        for chip in chips:
            landed = b.at[2 * chip[0] + chip[1], pl.ds(c * half, half), :]
            out.append((landed, landed, (x, y, 1 - c)))
    return out


def _swap_copies(bufs):
    x, y, c, _ = _place()
    n = len(bufs) // 2
    out = []
    for p, land in zip(bufs[:n], bufs[n:]):
        half = p.shape[1] // 2
        out.append((p.at[:, pl.ds((1 - c) * half, half), :], land, (x, y, 1 - c)))
    return out


def _scatter_copies(bufs):
    _, _, c, chips = _place()
    n = len(bufs) // 2
    out = []
    for s_, land in zip(bufs[:n], bufs[n:]):
        out += [(s_.at[2 * chip[0] + chip[1]], land.at[j], (*chip, c)) for j, chip in enumerate(chips)]
    return out


def _join_copies(bufs):
    x, y, c, _ = _place()
    out = []
    for b in bufs:
        half = b.shape[0] // 2
        mine = b.at[pl.ds(c * half, half), :]
        out.append((mine, mine, (x, y, 1 - c)))
    return out


def _gather_small(vec):
    m_per, n = vec.shape

    def body(x_ref, out_ref, send_sems, recv_sems, local_sem):
        x, y, c, chips = _place()
        me, sibling = (x, y, c), (x, y, 1 - c)

        def rows(px, py, pc):
            return out_ref.at[pl.ds((4 * px + 2 * py + pc) * m_per, m_per), :]

        def copy(k, block, to, src=None):
            return _remote(rows(*block) if src is None else src, rows(*block), send_sems.at[k], recv_sems.at[k], to)

        mine = pltpu.make_async_copy(x_ref, rows(*me), local_sem)
        mine.start()
        first = [copy(0, me, sibling, src=x_ref)]
        first += [copy(1 + j, me, (*chip, c), src=x_ref) for j, chip in enumerate(chips)]
        for cp in first:
            cp.start()
        passed = [copy(4 + j, (*chip, c), sibling) for j, chip in enumerate(chips)]
        for j, chip in enumerate(chips):
            copy(1 + j, (*chip, c), me).wait_recv()
            passed[j].start()
        copy(0, sibling, me).wait_recv()
        for j, chip in enumerate(chips):
            copy(4 + j, (*chip, 1 - c), me).wait_recv()
        for cp in first + passed:
            cp.wait_send()
        mine.wait()

    return ORDER.call(
        body, [vec], [pl.BlockSpec(memory_space=pltpu.VMEM)], name="gather_small_grads",
        out_shape=_sds((8 * m_per, n), vec.dtype), out_specs=pl.BlockSpec(memory_space=pltpu.VMEM),
        scratch_shapes=[pltpu.SemaphoreType.DMA((7,)), pltpu.SemaphoreType.DMA((7,)), pltpu.SemaphoreType.DMA],
    )


def _add_sibling(name, partial, received, c):
    _, R, C = partial.shape
    half = R // 2
    tr = _tile(half, 256)
    nb = half // tr

    def body(c_ref, p_ref, r_ref, o_ref):
        o_ref[...] = (p_ref[...].astype(F32) + r_ref[...].astype(F32)).astype(BF)

    return ORDER.call(
        body, [partial, received],
        [pl.BlockSpec((None, tr, C), lambda j, i, cr: (j, cr[0] * nb + i, 0)),
         pl.BlockSpec((None, tr, C), lambda j, i, cr: (j, i, 0))],
        prefetch=(c,), name=name, grid=(N_CHIPS, nb),
        out_specs=pl.BlockSpec((None, tr, C), lambda j, i, cr: (j, i, 0)),
        out_shape=_sds((N_CHIPS, half, C), BF), compiler_params=_cparams(("parallel", "parallel")),
    )


def _add_chips(name, sums, received, me_c):
    _, half, C = sums.shape
    tr = _tile(half, 256)
    nb = half // tr

    def body(mc_ref, s_ref, r_ref, o_ref):
        acc = s_ref[...].astype(F32)
        for j in range(3):
            acc = acc + r_ref[j].astype(F32)
        o_ref[...] = acc

    return ORDER.call(
        body, [sums, received],
        [pl.BlockSpec((None, tr, C), lambda i, mc: (mc[0], i, 0)),
         pl.BlockSpec((3, tr, C), lambda i, mc: (0, i, 0))],
        prefetch=(me_c,), name=name, grid=(nb,),
        out_specs=pl.BlockSpec((tr, C), lambda i, mc: (mc[1] * nb + i, 0)),
        out_shape=_sds((2 * half, C), F32), compiler_params=_cparams(("parallel",)),
    )


def _adamw_math(w, g, m, v):
    m = ADAM_B1 * m + (1.0 - ADAM_B1) * g
    v = ADAM_B2 * v + (1.0 - ADAM_B2) * (g * g)
    m_hat = m / (1.0 - ADAM_B1 ** ADAM_STEP)
    v_hat = v / (1.0 - ADAM_B2 ** ADAM_STEP)
    delta = -ADAM_LR * (m_hat / (jnp.sqrt(v_hat) + ADAM_EPS) + ADAM_WD * w)
    return delta, m, v


def _adamw(name, w, g, m, v):
    R, C = w.shape
    tr = _tile(R, 256)

    def body(w_ref, g_ref, m_ref, v_ref, go_ref, d_ref, mo_ref, vo_ref):
        gv = g_ref[...]
        go_ref[...] = gv
        d_ref[...], mo_ref[...], vo_ref[...] = _adamw_math(w_ref[...], gv, m_ref[...], v_ref[...])

    row = pl.BlockSpec((tr, C), lambda i: (i, 0))
    return ORDER.call(
        body, [w, g, m, v], [row] * 4, name=name, grid=(R // tr,), out_specs=[row] * 4,
        out_shape=[_sds((R, C), F32)] * 4, compiler_params=_cparams(("parallel",)), chain_output=1,
    )


def _adamw_small(gathered, w, m, v):
    rows, n = w.shape

    def body(ga_ref, w_ref, m_ref, v_ref, go_ref, d_ref, mo_ref, vo_ref):
        g = ga_ref[pl.ds(0, rows), :]
        for dev in range(1, 8):
            g = g + ga_ref[pl.ds(dev * rows, rows), :]
        go_ref[...] = g
        d_ref[...], mo_ref[...], vo_ref[...] = _adamw_math(w_ref[...], g, m_ref[...], v_ref[...])

    whole = pl.BlockSpec(memory_space=pltpu.VMEM)
    return ORDER.call(
        body, [gathered, w, m, v], [whole] * 4, name="adamw_small", out_specs=[whole] * 4,
        out_shape=[_sds((rows, n), F32)] * 4, compiler_params=_cparams(), chain_output=1,
    )


def _proj_merge(y_a, y_b, gpa, gpb, g3):
    S, K = y_a.shape
    _, _, Nq = gpa.shape
    D = N_CHIPS * Nq
    tm, tn = _tile(S, 1024), _tile(Nq, 512)
    q = Nq // tn

    def body(ya_ref, yb_ref, wa_ref, wb_ref, g_ref, merged_ref, c_ref):
        pa = jnp.dot(ya_ref[...], wa_ref[...], preferred_element_type=F32)
        pb = jnp.dot(yb_ref[...], wb_ref[...], preferred_element_type=F32)
        g = g_ref[...].astype(F32)
        merged_ref[...] = (g[0] * pa + g[1] * pb).astype(BF)
        c_ref[0] = (pa * g[0] * (1.0 - g[0])).astype(BF)
        c_ref[1] = (pb * g[1] * (1.0 - g[1])).astype(BF)

    rows = pl.BlockSpec((tm, K), lambda i, j: (i, 0))
    weight = pl.BlockSpec((None, K, tn), lambda i, j: (j // q, 0, j % q))
    pair = pl.BlockSpec((2, tm, tn), lambda i, j: (0, i, j))
    return ORDER.call(
        body, [y_a, y_b, gpa, gpb, g3], [rows, rows, weight, weight, pair], name="proj_merge",
        grid=(S // tm, N_CHIPS * q), out_specs=[pl.BlockSpec((tm, tn), lambda i, j: (i, j)), pair],
        out_shape=[_sds((S, D), BF), _sds((2, S, D), BF)], compiler_params=_cparams(("parallel", "parallel")))


def _out_proj_dx(dx1b, wout, g3, c3, gpa, gpb):
    S, D = dx1b.shape
    _, K, Nq = gpa.shape
    tm, tn = _tile(S, 1024), Nq
    nj = D // tn

    def body(a_ref, w_ref, g_ref, c_ref, wa_ref, wb_ref, dpa_ref, dpb_ref, dg_ref, db_ref, dya_ref, dyb_ref,
             acc_a, acc_b):
        j = pl.program_id(1)
        dm = lax.dot_general(a_ref[...], w_ref[...], (NT, ((), ())), preferred_element_type=F32)
        g, c = g_ref[...].astype(F32), c_ref[...].astype(F32)
        dpa, dpb = (dm * g[0]).astype(BF), (dm * g[1]).astype(BF)
        dpa_ref[...] = dpa
        dpb_ref[...] = dpb
        dga, dgb = dm * c[0], dm * c[1]
        dg_ref[0] = dga.astype(BF)
        dg_ref[1] = dgb.astype(BF)
        db_ref[...] = jnp.concatenate([jnp.sum(dga, axis=0, keepdims=True), jnp.sum(dgb, axis=0, keepdims=True)], 0)
        ya = lax.dot_general(dpa, wa_ref[...], (NT, ((), ())), preferred_element_type=F32)
        yb = lax.dot_general(dpb, wb_ref[...], (NT, ((), ())), preferred_element_type=F32)

        @pl.when(j == 0)
        def _():
            acc_a[...] = ya
            acc_b[...] = yb

        @pl.when(j > 0)
        def _():
            acc_a[...] += ya
            acc_b[...] += yb

        @pl.when(j == nj - 1)
        def _():
            dya_ref[...] = acc_a[...].astype(BF)
            dyb_ref[...] = acc_b[...].astype(BF)

    tile = pl.BlockSpec((tm, tn), lambda i, j: (i, j))
    pair = pl.BlockSpec((2, tm, tn), lambda i, j: (0, i, j))
    shard = pl.BlockSpec((None, K, tn), lambda i, j: (j, 0, 0))
    rows = pl.BlockSpec((tm, K), lambda i, j: (i, 0))
    return ORDER.call(
        body, [dx1b, wout, g3, c3, gpa, gpb],
        [pl.BlockSpec((tm, D), lambda i, j: (i, 0)), pl.BlockSpec((tn, D), lambda i, j: (j, 0)), pair, pair, shard, shard],
        name="out_proj_dx", grid=(S // tm, nj),
        out_specs=[tile, tile, pair, pl.BlockSpec((None, 2, tn), lambda i, j: (i, 0, j)), rows, rows],
        out_shape=[_sds((S, D), BF), _sds((S, D), BF), _sds((2, S, D), BF), _sds((S // tm, 2, D), F32),
                   _sds((S, K), BF), _sds((S, K), BF)],
        scratch_shapes=[pltpu.VMEM((tm, K), F32), pltpu.VMEM((tm, K), F32)],
        compiler_params=_cparams(("parallel", "arbitrary")))


class _Exchange:
    GATHER = (("qkv",), ("gate",), ("proj_a", "proj_b", "out"), ("up",), ("down",))
    REDUCE = {"mlp": ("down", "up"), "mix": ("out", "proj_a", "proj_b"), "in": ("qkv", "gate")}

    OWN_FIRST = ("qkv", "gate")

    def __init__(self, shards, me, c, moments):
        self.me, self.c = me, c
        self.shards, self.moments = shards, moments
        self.hop1, self.hop2, self.stage, self.grads, self.own, self.updates = {}, {}, {}, {}, {}, {}
        for g, names in enumerate(self.GATHER):
            bufs = []
            for n in names:
                placed = _place_shard(f"place_{n}", shards[n], me, plain=n in self.OWN_FIRST)
                bufs.append(placed[0])
                if n in self.OWN_FIRST:
                    self.own[n] = placed[1]
            self.hop1[g] = _copy_start(f"gather{g}_start", bufs, _gather_hop1, 3 * len(names))

    def forward(self, g):
        send, recv, thru = self.hop1.pop(g)
        self.hop2[g] = _copy_start(f"gather{g}_forward", thru, _gather_hop2, len(thru) * 3,
                                   earlier=(_gather_hop1, send, recv))

    def weights(self, g):
        send, recv, thru = self.hop2.pop(g)
        return _copy_wait(f"gather{g}_wait", thru, _gather_hop2, send, recv)

    def adamw_beside(self, name):
        def update(w, g, m, v):
            return (g,) + _adamw_math(w, g, m, v)
        return update, [self.shards[name], self.grads[name], *self.moments[name]], 4

    def reduce(self, key, partials=None):
        names = self.REDUCE[key]
        n = len(names)
        if partials is not None:
            lands = [lax.empty((p.shape[0], p.shape[1] // 2, p.shape[2]), p.dtype) for p in partials]
            self.stage[key] = ("swap",) + _copy_start(f"reduce_{key}_swap", list(partials) + lands, _swap_copies, n)
            return
        kind, send, recv, thru = self.stage.pop(key)
        if kind == "swap":
            thru = _copy_wait(f"reduce_{key}_swap_wait", thru, _swap_copies, send, recv)
            sums = [_add_sibling(f"reduce_{nm}_add_sibling", p, r, self.c)
                    for nm, p, r in zip(names, thru[:n], thru[n:])]
            lands = [lax.empty((3,) + s_.shape[1:], s_.dtype) for s_ in sums]
            self.stage[key] = ("scatter",) + _copy_start(f"reduce_{key}_scatter", sums + lands, _scatter_copies, 3 * n)
        elif kind == "scatter":
            thru = _copy_wait(f"reduce_{key}_scatter_wait", thru, _scatter_copies, send, recv)
            me_c = jnp.concatenate([self.me, self.c])
            halves = [_add_chips(f"reduce_{nm}_add_chips", s_, r, me_c)
                      for nm, s_, r in zip(names, thru[:n], thru[n:])]
            self.stage[key] = ("join",) + _copy_start(f"reduce_{key}_join", halves, _join_copies, n)
        else:
            thru = _copy_wait(f"reduce_{key}_join_wait", thru, _join_copies, send, recv)
            self.grads.update(zip(names, thru))


def _forward_backward(x, target, norm_mix, b_gate, rpb, norm_mlp, norm_final, ex):
    S, D = x.shape

    h1 = _rms_fwd("rms_mix", x, norm_mix)
    nq = QKV_W // 512
    qkv_out = (((3, S, QKV_W), BF), lambda i, T: (T // nq, i, T % nq))
    tg = _tile(ex.own["gate"].shape[1], 1024)
    ng = D // tg
    gate_out = (((2, S, D), BF), lambda i, T: (T // ng, i, T % ng))

    def gate_epilogue(acc, ex_, outs):
        outs[0][...] = jax.nn.sigmoid(acc + ex_[0][...]).astype(BF)

    qkv3 = _mm_nn_shards("qkv_own", h1, ex.own["qkv"], ex.me, True, *qkv_out, _store(BF), tm=2048)
    g3 = _mm_nn_shards("gate_own", h1, ex.own["gate"], ex.me, True, *gate_out, gate_epilogue, extras=[b_gate], tn=tg)
    ex.forward(0)
    e2 = _rpb_to_table(rpb)
    (gq,) = ex.weights(0)
    qkv3 = _mm_nn_shards("qkv", h1, gq, ex.me, False, *qkv_out, _store(BF), into=qkv3, tm=2048)

    ex.forward(1)
    outs_a = [_attn_a_fwd(qkv3, 0, DILATIONS[0])]
    (gg,) = ex.weights(1)
    g3 = _mm_nn_shards("gate", h1, gg, ex.me, False, *gate_out, gate_epilogue, extras=[b_gate], into=g3, tn=tg)

    ex.forward(2)
    qkv_views = _qkv_views("qkv_views", qkv3)
    outs_a += [_attn_a_fwd(qkv_views[d], grp, d) for grp, d in enumerate(DILATIONS) if grp > 0]
    y_a, lj = _attn_a_combine([o for o, _ in outs_a], [l for _, l in outs_a])
    y_b, lse_b = _attn_b_fwd(qkv3, e2)
    gpa, gpb, gout = ex.weights(2)
    wout = gout.reshape(D, D)
    merged, c3 = _proj_merge(y_a, y_b, gpa, gpb, g3)

    def residual_epilogue(acc, ex_, outs):
        outs[0][...] = acc + ex_[0][...]

    def residual_norm_epilogue(acc, ex_, outs):
        x1v = acc + ex_[0][...]
        outs[0][...] = x1v
        r = lax.rsqrt(jnp.mean(x1v * x1v, axis=-1, keepdims=True) + EPS)
        outs[1][...] = ((x1v * r) * ex_[1][...]).astype(BF)

    def nn_plain(name, a, w, res, bm=1024, bn=1024, norm=None):
        M, K = a.shape
        N = w.shape[1]
        bm, bn, bk = _tile(M, bm), _tile(N, bn), _tile(K, 2048)
        t = pl.BlockSpec((bm, bn), lambda i, j, k: (i, j))
        extras, outs, epilogue = [(res, t)], [(_sds((M, N), F32), t)], residual_epilogue
        if norm is not None:
            assert bn == N
            extras.append((norm, pl.BlockSpec((1, N), lambda i, j, k: (0, 0))))
            outs.append((_sds((M, N), BF), t))
            epilogue = residual_norm_epilogue
        result = _matmul(name, a, w, pl.BlockSpec((bm, bk), lambda i, j, k: (i, k)),
                         pl.BlockSpec((bk, bn), lambda i, j, k: (k, j)), NN, (M // bm, N // bn, K // bk), (bm, bn),
                         extras, outs, epilogue)
        return result[0] if norm is None else result

    ex.forward(3)
    x1, h2 = nn_plain("out_proj", merged, wout, x, bm=512, bn=2048, norm=norm_mlp)
    (gup,) = ex.weights(3)
    F = gup.shape[2] * N_CHIPS

    def up_epilogue(acc, ex_, outs):
        ru = jnp.maximum(acc, 0.0)
        outs[0][...] = (ru * ru).astype(BF)
        outs[1][...] = ru.astype(BF)

    tu = _tile(gup.shape[2], 2048)
    ut = pl.BlockSpec((_tile(S, 1024), tu), lambda i, j, k: (i, j))
    (act, ru), _ = _mm_nn_cols("mlp_up", h2, gup, BF, epilogue=up_epilogue, tn=tu,
                               outs=[(_sds((S, F), BF), ut), (_sds((S, F), BF), ut)])
    ex.forward(4)
    (gdown,) = ex.weights(4)
    wdown = gdown.reshape(F, D)
    x2 = nn_plain("mlp_down", act, wdown, x1)

    loss, dx2, dx2b, d_norm_final = _loss_head(x2, target, norm_final.reshape(1, D))

    def nt_rows(name, a, w, epilogue, extras, outs, bn=1024):
        M, N = a.shape
        K = w.shape[0]
        bm, bn, bk = _tile(M, 1024), _tile(K, bn), _tile(N, 2048)
        return _matmul(name, a, w, pl.BlockSpec((bm, bk), lambda i, j, k: (i, k)),
                       pl.BlockSpec((bn, bk), lambda i, j, k: (j, k)), NT, (M // bm, K // bn, N // bk), (bm, bn),
                       extras(bm, bn), outs(bm, bn), epilogue)

    def nt_cols(name, a_spec_fn, a, g, M, epilogue, extras, outs, bk, bn=1024, side=None):
        _, K, Nq = g.shape
        bm, bn, bk = _tile(M, 1024), _tile(K, bn), _tile(Nq, bk)
        q = Nq // bk
        return _matmul(name, a, g, a_spec_fn(bm, bk), pl.BlockSpec((None, bn, bk), lambda i, j, k: (k // q, j, k % q)),
                       NT, (M // bm, K // bn, N_CHIPS * q), (bm, bn), extras(bm, bn), outs(bm, bn), epilogue,
                       side=side)

    def tn_grad(name, a, a_spec_fn, b, b_spec_fn, Kin, N, out_shape, out_spec_fn, bn=1024):
        bm, bn, bk = _tile(Kin, 1024), _tile(N, bn), _tile(S, 4096)
        return _matmul(name, a, b, a_spec_fn(bk, bm), b_spec_fn(bk, bn), TN, (Kin // bm, N // bn, S // bk), (bm, bn),
                       [], [(_sds(out_shape, BF), out_spec_fn(bm, bn))], _store(BF))[0]

    plain_a = lambda bk, bm: pl.BlockSpec((bk, bm), lambda i, j, k: (k, i))
    plain_b = lambda bk, bn: pl.BlockSpec((bk, bn), lambda i, j, k: (k, j))
    plain_o = lambda bm, bn: pl.BlockSpec((bm, bn), lambda i, j, k: (i, j))
    a_rows = lambda bm, bk: pl.BlockSpec((bm, bk), lambda i, j, k: (i, k))

    def cols_o(Nq):
        def spec(bm, bn):
            q = Nq // bn
            return pl.BlockSpec((None, bm, bn), lambda i, j, k: (j // q, i, j % q))
        return spec

    def du_epilogue(acc, ex_, outs):
        outs[0][...] = (acc * (2.0 * ex_[0][...].astype(F32))).astype(BF)

    dw_down = tn_grad("mlp_down_dw", act, plain_a, dx2b, plain_b, F, D, (F, D), plain_o)
    (du,) = nt_rows("mlp_down_dx", dx2b, wdown, du_epilogue,
                    lambda bm, bn: [(ru, plain_o(bm, bn))], lambda bm, bn: [(_sds((S, F), BF), plain_o(bm, bn))],
                    bn=2048)

    fq = gup.shape[2]
    dw_up = tn_grad("mlp_up_dw", h2, plain_a, du, plain_b, D, F, (N_CHIPS, D, fq), cols_o(fq), bn=min(fq, 1024))
    ex.reduce("mlp", partials=[dw_down.reshape(N_CHIPS, F // N_CHIPS, D), dw_up])
    (dh2,) = nt_cols("mlp_up_dx", a_rows, du, gup, S, _store(F32), lambda bm, bn: [],
                     lambda bm, bn: [(_sds((S, D), F32), plain_o(bm, bn))], 1024, bn=2048)
    ex.reduce("mlp")
    dx1, dx1b, d_norm_mlp = _rms_bwd("rms_mlp_bwd", dh2, x1, norm_mlp, dx2)

    dpa, dpb, dg3, db_gate, dy_a, dy_b = _out_proj_dx(dx1b, wout, g3, c3, gpa, gpb)
    dw_out = tn_grad("out_proj_dw", merged, plain_a, dx1b, plain_b, D, D, (D, D), plain_o)

    pq = gpa.shape[2]
    dw_pa = tn_grad("proj_a_dw", y_a, plain_a, dpa, plain_b, 512, D, (N_CHIPS, 512, pq), cols_o(pq), bn=min(pq, 512))
    dw_pb = tn_grad("proj_b_dw", y_b, plain_a, dpb, plain_b, 512, D, (N_CHIPS, 512, pq), cols_o(pq), bn=min(pq, 512))
    ex.reduce("mix", partials=[dw_out.reshape(N_CHIPS, D // N_CHIPS, D), dw_pa, dw_pb])

    dqkv3 = lax.empty((3, S, QKV_W), BF)
    dqkv3 = _attn_a_bwd(qkv3, dy_a, y_a, lj, dqkv3, 0, DILATIONS[0])
    ex.reduce("mix")
    dy_views, y_views, lj_views = _dilated_rows("attn_a_bwd_rows", [dy_a, y_a, lj])
    dqkv_views = {d: _attn_a_bwd(qkv_views[d], dy_views[d], y_views[d], lj_views[d], None, grp, d)
                  for grp, d in enumerate(DILATIONS) if grp > 0}
    dqkv3 = _qkv_views("dqkv_from_views", dqkv3, dqkv_views)
    dqkv3, de2 = _attn_b_bwd(qkv3, e2, dy_b, y_b, lse_b, dqkv3)
    d_rpb = _table_grad_to_rpb(de2)

    def stacked_a(width):
        def spec(bm, bk):
            q = width // bk
            return pl.BlockSpec((None, bm, bk), lambda i, j, k: (k // q, i, k % q))
        return spec

    def stacked_b(width):
        def spec(bk, bn):
            q = width // bn
            return pl.BlockSpec((None, bk, bn), lambda i, j, k: (j // q, k, j % q))
        return spec

    ex.reduce("mlp")
    dw_qkv = tn_grad("qkv_dw", h1, plain_a, dqkv3, stacked_b(QKV_W), D, 3 * QKV_W, (N_CHIPS,) + gq.shape[1:],
                     cols_o(gq.shape[2]), bn=512)
    dw_gate = tn_grad("gate_dw", h1, plain_a, dg3, stacked_b(D), D, 2 * D, (N_CHIPS,) + gg.shape[1:],
                      cols_o(gg.shape[2]), bn=gg.shape[2])
    ex.reduce("in", partials=[dw_qkv, dw_gate])
    ex.reduce("mlp")
    dh1_q, *ex.updates["down"] = nt_cols(
        "qkv_dx", stacked_a(QKV_W), dqkv3, gq, S, _store(F32), lambda bm, bn: [],
        lambda bm, bn: [(_sds((S, D), F32), plain_o(bm, bn))], 512, bn=2048, side=ex.adamw_beside("down"))
    ex.reduce("in")
    ex.reduce("mix")

    def add_epilogue(acc, ex_, outs):
        outs[0][...] = acc + ex_[0][...]

    dh1, *ex.updates["up"] = nt_cols(
        "gate_dx", stacked_a(D), dg3, gg, S, add_epilogue, lambda bm, bn: [(dh1_q, plain_o(bm, bn))],
        lambda bm, bn: [(_sds((S, D), F32), plain_o(bm, bn))], gg.shape[2], side=ex.adamw_beside("up"))
    grad_x, _, d_norm_mix = _rms_bwd("rms_mix_bwd", dh1, x, norm_mix, dx1)
    ex.reduce("mix")

    small = [d_norm_mix, jnp.sum(db_gate, axis=0).reshape(1, 2 * D), d_rpb, d_norm_mlp, d_norm_final]
    return loss, grad_x, small


def _pack_small(parts, width):
    flat = jnp.concatenate([p.reshape(-1) for p in parts])
    return jnp.pad(flat, (0, 8 * width - flat.shape[0])).reshape(8, width)


def kernel(x, norm_mix, w_qkv, w_gate, b_gate, rpb, w_proj_a, w_proj_b, w_out, norm_mlp, w_up, w_down, norm_final, loss_target, m_norm_mix, m_w_qkv, m_w_gate, m_b_gate, m_rpb, m_w_proj_a, m_w_proj_b, m_w_out, m_norm_mlp, m_w_up, m_w_down, m_norm_final, v_norm_mix, v_w_qkv, v_w_gate, v_b_gate, v_rpb, v_w_proj_a, v_w_proj_b, v_w_out, v_norm_mlp, v_w_up, v_w_down, v_norm_final):
    names = ["qkv", "gate", "proj_a", "proj_b", "out", "up", "down"]
    big = dict(zip(names, [w_qkv[0], w_gate[0], w_proj_a[0], w_proj_b[0], w_out[0], w_up[0], w_down[0]]))
    big_m = dict(zip(names, [m_w_qkv[0], m_w_gate[0], m_w_proj_a[0], m_w_proj_b[0], m_w_out[0], m_w_up[0], m_w_down[0]]))
    big_v = dict(zip(names, [v_w_qkv[0], v_w_gate[0], v_w_proj_a[0], v_w_proj_b[0], v_w_out[0], v_w_up[0], v_w_down[0]]))

    c = lax.axis_index("c").astype(jnp.int32).reshape(1)
    me = (2 * lax.axis_index("x") + lax.axis_index("y")).astype(jnp.int32).reshape(1)
    ORDER.last = None
    ex = _Exchange(big, me, c, {n: (big_m[n], big_v[n]) for n in names})
    loss, grad_x, small = _forward_backward(x[0], loss_target[0], norm_mix, b_gate, rpb[0], norm_mlp, norm_final, ex)

    def adamw(group):
        return {n: ex.updates[n] if ex.updates.get(n) else _adamw(f"adamw_{n}", big[n], ex.grads[n], big_m[n], big_v[n])
                for n in _Exchange.REDUCE[group]}

    big_out = {**adamw("mlp"), **adamw("mix")}
    ex.reduce("in")

    small_w = [norm_mix, b_gate, rpb, norm_mlp, norm_final]
    count = sum(int(np.prod(p.shape)) for p in small_w)
    width = -(-count // (8 * 128)) * 128
    packed = _adamw_small(_gather_small(_pack_small(small, width)), _pack_small(small_w, width),
                          _pack_small([m_norm_mix, m_b_gate, m_rpb, m_norm_mlp, m_norm_final], width),
                          _pack_small([v_norm_mix, v_b_gate, v_rpb, v_norm_mlp, v_norm_final], width))
    ex.reduce("in")
    big_out.update(adamw("in"))

    def unpack(flat2d):
        flat, out, at = flat2d.reshape(-1), [], 0
        for p in small_w:
            size = int(np.prod(p.shape))
            out.append(flat[at:at + size].reshape(p.shape))
            at += size
        return out

    small_out = [unpack(a) for a in packed]

    def ordered(kind):
        sm = small_out[kind]
        bg = {n: o[kind][None] for n, o in big_out.items()}
        return [sm[0], bg["qkv"], bg["gate"], sm[1], sm[2], bg["proj_a"], bg["proj_b"], bg["out"], sm[3],
                bg["up"], bg["down"], sm[4]]

    total = lax.psum(loss[0, 0], ("x", "y", "c"))
    return (total, grad_x[None], *ordered(0), *ordered(1), *ordered(2), *ordered(3))
```

```python
import math

import numpy as np
import jax
import jax.numpy as jnp
from jax import lax
from jax.experimental import pallas as pl
from jax.experimental.pallas import tpu as pltpu

BF = jnp.bfloat16
F32 = jnp.float32
MESH = pl.DeviceIdType.MESH

HEAD_DIM = 128
N_HEADS = 16
N_HEADS_A = 12
QKV_W = N_HEADS * HEAD_DIM
DILATIONS = (1, 4, 16)
HALF_WINDOW = 64
GRID_W = 64
NA_ROWS = 8
NA_COLS = 16
RPB_ROWS = 2 * NA_ROWS - 1
RPB_COLS = 2 * NA_COLS - 1
EPS = 1e-6
NEG = -1e30
SCALE = HEAD_DIM ** -0.5

ADAM_LR = 0.001
ADAM_B1 = 0.9
ADAM_B2 = 0.999
ADAM_EPS = 1e-08
ADAM_WD = 0.01
ADAM_STEP = 10

N_CHIPS = 4
VMEM_LIMIT_BYTES = 48 * 1024 * 1024
QB = 256
NBR_SIDE = 64
ROW_TILE = 512


def _key_rows(L):
    return min(QB + 2 * HALF_WINDOW, L)


def _cparams(sem=None):
    return pltpu.CompilerParams(dimension_semantics=sem, vmem_limit_bytes=VMEM_LIMIT_BYTES)


def _tile(dim, want):
    t = min(dim, want)
    assert dim % t == 0, (dim, want)
    return t


class _ProgramOrder:
    def __init__(self):
        self.last = None

    def call(self, body, operands, in_specs, *, prefetch=(), grid=None, out_specs=None, chain_output=0, **kwargs):
        operands, in_specs = list(operands), list(in_specs)
        lead = len(prefetch) + len(operands)
        if self.last is not None and not any(op is self.last for op in operands):
            operands.append(self.last)
            in_specs.append(pl.BlockSpec(memory_space=pl.ANY))
            inner = body

            def body(*refs):
                return inner(*refs[:lead], *refs[lead + 1:])

        if prefetch:
            kwargs["grid_spec"] = pltpu.PrefetchScalarGridSpec(
                num_scalar_prefetch=len(prefetch), grid=grid, in_specs=in_specs, out_specs=out_specs)
        else:
            kwargs.update(in_specs=in_specs, out_specs=out_specs)
            if grid is not None:
                kwargs["grid"] = grid
        out = pl.pallas_call(body, **kwargs)(*prefetch, *operands)
        self.last = out[chain_output] if isinstance(out, (tuple, list)) else out
        return out


ORDER = _ProgramOrder()


NN = ((1,), (0,))
NT = ((1,), (1,))
TN = ((0,), (0,))


def _matmul(name, a, b, a_spec, b_spec, dims, grid, acc_shape, extras, outs, epilogue, precision=None,
            prefetch=(), into=None, side=None):
    n_ex, n_out, nk = len(extras), len(outs), grid[2]
    side_fn, side_in, n_side_out = side if side is not None else (None, [], 0)
    side_spec = None
    n_in = 2 + n_ex + len(side_in) + (into is not None)
    if side is not None:
        R, C = side_in[0].shape
        steps = grid[0] * grid[1] * grid[2]
        side_blocks = max(n for n in range(1, steps + 1) if R % n == 0 and (R // n) % 8 == 0)

        def side_step(*ids):
            return (ids[0] * grid[1] + ids[1]) * grid[2] + ids[2]

        side_spec = pl.BlockSpec((R // side_blocks, C),
                                 lambda *ids: (jnp.minimum(side_step(*ids), side_blocks - 1), 0))

    def body(*refs):
        refs = refs[len(prefetch):]
        a_ref, b_ref = refs[0], refs[1]
        ex_refs = refs[2:2 + n_ex]
        out_refs = refs[n_in:n_in + n_out]
        if side is not None:
            @pl.when(side_step(pl.program_id(0), pl.program_id(1), pl.program_id(2)) < side_blocks)
            def _():
                results = side_fn(*[r[...] for r in refs[2 + n_ex:2 + n_ex + len(side_in)]])
                for o_ref, value in zip(refs[n_in + n_out:n_in + n_out + n_side_out], results):
                    o_ref[...] = value

        def dot():
            return lax.dot_general(a_ref[...], b_ref[...], (dims, ((), ())),
                                   preferred_element_type=F32, precision=precision)

        if nk == 1:
            epilogue(dot(), ex_refs, out_refs)
            return
        acc_ref = refs[-1]
        k = pl.program_id(2)

        @pl.when(k == 0)
        def _():
            acc_ref[...] = dot()

        if nk > 2:
            @pl.when((k > 0) & (k < nk - 1))
            def _():
                acc_ref[...] += dot()

        @pl.when(k == nk - 1)
        def _():
            epilogue(acc_ref[...] + dot(), ex_refs, out_refs)

    operands = [a, b] + [e for e, _ in extras] + list(side_in)
    in_specs = [a_spec, b_spec] + [s for _, s in extras] + [side_spec] * len(side_in)
    kwargs = {}
    if into is not None:
        operands.append(into)
        in_specs.append(pl.BlockSpec(memory_space=pl.ANY))
        kwargs["input_output_aliases"] = {len(prefetch) + n_in - 1: 0}
    return ORDER.call(
        body, operands, in_specs, prefetch=prefetch, name=name, grid=grid,
        out_specs=[s for _, s in outs] + [side_spec] * n_side_out,
        out_shape=[sh for sh, _ in outs] + [_sds(s_.shape, F32) for s_ in side_in[:1]] * n_side_out,
        scratch_shapes=[pltpu.VMEM(acc_shape, F32)] if nk > 1 else [],
        compiler_params=_cparams(("parallel", "parallel", "arbitrary")), **kwargs,
    )


def _mm_nn_shards(name, a, w, me, own, out, out_block, epilogue, extras=(), into=None, tn=512, tm=1024):
    M, K = a.shape
    Nq = w.shape[-1]
    tm, tn = _tile(M, tm), _tile(Nq, tn)
    q = Nq // tn

    def tile(j, me_ref):
        shard = me_ref[0] if own else (me_ref[0] + 1 + j // q) % N_CHIPS
        return shard, j % q, shard * q + j % q

    if own:
        b_spec = pl.BlockSpec((K, tn), lambda i, j, k, me_ref: (0, j))
    else:
        b_spec = pl.BlockSpec((None, K, tn), lambda i, j, k, me_ref: (tile(j, me_ref)[0], 0, tile(j, me_ref)[1]))
    shape, dtype = out
    out_spec = pl.BlockSpec((None, tm, tn), lambda i, j, k, me_ref: out_block(i, tile(j, me_ref)[2]))
    ex = [(e, pl.BlockSpec((1, tn), lambda i, j, k, me_ref: (0, tile(j, me_ref)[2]))) for e in extras]
    return _matmul(name, a, w, pl.BlockSpec((tm, K), lambda i, j, k, me_ref: (i, 0)), b_spec, NN,
                   (M // tm, q if own else (N_CHIPS - 1) * q, 1), (tm, tn), ex, [(_sds(shape, dtype), out_spec)],
                   epilogue, prefetch=(me,), into=into)[0]


def _store(dtype):
    def epilogue(acc, ex, outs):
        outs[0][...] = acc.astype(dtype)
    return epilogue


def _sds(shape, dtype):
    return jax.ShapeDtypeStruct(shape, dtype)


def _mm_nn_cols(name, a, g, out_dtype, epilogue=None, extras=(), outs=None, tm=1024, tn=1024, tk=2048):
    M, K = a.shape
    _, _, Nq = g.shape
    tm, tn, tk = _tile(M, tm), _tile(Nq, tn), _tile(K, tk)
    q = Nq // tn
    grid = (M // tm, N_CHIPS * q, K // tk)
    if outs is None:
        outs = [(_sds((M, N_CHIPS * Nq), out_dtype), pl.BlockSpec((tm, tn), lambda i, j, k: (i, j)))]
    return _matmul(name, a, g, pl.BlockSpec((tm, tk), lambda i, j, k: (i, k)),
                   pl.BlockSpec((None, tk, tn), lambda i, j, k: (j // q, k, j % q)), NN, grid, (tm, tn),
                   list(extras), outs, epilogue or _store(out_dtype)), (tm, tn, tk)


def _rms_fwd(name, x, g):
    S, D = x.shape
    tm = _tile(S, ROW_TILE)

    def body(x_ref, g_ref, h_ref):
        xv = x_ref[...]
        r = lax.rsqrt(jnp.mean(xv * xv, axis=-1, keepdims=True) + EPS)
        h_ref[...] = ((xv * r) * g_ref[...]).astype(BF)

    row = pl.BlockSpec((tm, D), lambda i: (i, 0))
    return ORDER.call(
        body, [x, g], [row, pl.BlockSpec((1, D), lambda i: (0, 0))], name=name, grid=(S // tm,),
        out_specs=row, out_shape=_sds((S, D), BF), compiler_params=_cparams(("parallel",)),
    )


def _rms_bwd(name, dh, x, g, dres):
    S, D = x.shape
    tm = _tile(S, ROW_TILE // 2)

    def body(dh_ref, x_ref, g_ref, dres_ref, dx_ref, dxb_ref, dg_ref):
        xv = x_ref[...]
        r = lax.rsqrt(jnp.mean(xv * xv, axis=-1, keepdims=True) + EPS)
        n = xv * r
        dhv = dh_ref[...]
        dyg = dhv * g_ref[...]
        dx = dres_ref[...] + r * (dyg - n * jnp.mean(dyg * n, axis=-1, keepdims=True))
        dx_ref[...] = dx
        dxb_ref[...] = dx.astype(BF)

        @pl.when(pl.program_id(0) == 0)
        def _():
            dg_ref[...] = jnp.zeros_like(dg_ref)

        dg_ref[...] += jnp.sum(dhv * n, axis=0, keepdims=True)

    row = pl.BlockSpec((tm, D), lambda i: (i, 0))
    vec = pl.BlockSpec((1, D), lambda i: (0, 0))
    return ORDER.call(
        body, [dh, x, g, dres], [row, row, vec, row], name=name, grid=(S // tm,),
        out_specs=[row, row, vec],
        out_shape=[_sds((S, D), F32), _sds((S, D), BF), _sds((1, D), F32)],
        compiler_params=_cparams(("arbitrary",)),
    )


def _loss_head(x2, target, g):
    S, D = x2.shape
    tm = _tile(S, ROW_TILE)

    def body(x_ref, t_ref, g_ref, loss_ref, dx_ref, dxb_ref, dg_ref):
        xv = x_ref[...]
        gv = g_ref[...]
        r = lax.rsqrt(jnp.mean(xv * xv, axis=-1, keepdims=True) + EPS)
        n = xv * r
        e = n * gv - t_ref[...]
        dy = e * (1.0 / D)
        dyg = dy * gv
        dx = r * (dyg - n * jnp.mean(dyg * n, axis=-1, keepdims=True))
        dx_ref[...] = dx
        dxb_ref[...] = dx.astype(BF)

        @pl.when(pl.program_id(0) == 0)
        def _():
            dg_ref[...] = jnp.zeros_like(dg_ref)
            loss_ref[...] = jnp.zeros_like(loss_ref)

        dg_ref[...] += jnp.sum(dy * n, axis=0, keepdims=True)
        per_row = jnp.mean(e * e, axis=-1, keepdims=True)
        loss_ref[...] += 0.5 * jnp.sum(per_row, axis=0, keepdims=True)

    row = pl.BlockSpec((tm, D), lambda i: (i, 0))
    vec = pl.BlockSpec((1, D), lambda i: (0, 0))
    return ORDER.call(
        body, [x2, target, g], [row, row, vec], name="loss_head", grid=(S // tm,),
        out_specs=[pl.BlockSpec((1, 1), lambda i: (0, 0)), row, row, vec],
        out_shape=[_sds((1, 1), F32), _sds((S, D), F32), _sds((S, D), BF), _sds((1, D), F32)],
        compiler_params=_cparams(("arbitrary",)), chain_output=1,
    )


def _chains(L):
    side = min(8, L // QB)
    return side, max(1, 4 // side)


def _band_scores(qkv_ref, i, L, coef, head):
    KB = _key_rows(L)
    lanes = pl.ds(head * HEAD_DIM, HEAD_DIM)
    q0 = pl.multiple_of(i * QB, QB)
    ks = pl.multiple_of(jnp.clip(i * QB - HALF_WINDOW, 0, L - KB), HALF_WINDOW)
    q = qkv_ref[0, pl.ds(q0, QB), lanes]
    k = qkv_ref[1, pl.ds(ks, KB), lanes]
    v = qkv_ref[2, pl.ds(ks, KB), lanes]
    s = lax.dot_general(q, k, (NT, ((), ())), preferred_element_type=F32) * SCALE
    qpos = q0 + lax.broadcasted_iota(jnp.int32, (QB, KB), 0)
    kpos = ks + lax.broadcasted_iota(jnp.int32, (QB, KB), 1)
    rel = jnp.abs(kpos - qpos)
    valid = rel <= HALF_WINDOW
    s = jnp.where(valid, s - coef * rel.astype(F32), NEG)
    return q0, ks, q, k, v, s, valid


def _alibi_coefs(group, d, heads):
    first = 4 * group + 1 + pl.program_id(1) * heads
    scale = jnp.full((1, 1), -(8.0 / N_HEADS_A) * math.log(2.0), F32)
    return [jnp.exp(scale * (first + hh).astype(F32)) * float(d) for hh in range(heads)]


def _dilated_view(qkv3, group, d, heads):
    per = 4 // heads
    L = qkv3.shape[1]
    if d == 1:
        return qkv3, pl.BlockSpec((3, L, heads * HEAD_DIM), lambda r, j: (0, 0, per * group + j))
    return qkv3, pl.BlockSpec((3, L, heads * HEAD_DIM), lambda r, j: (0, 0, r * per + j))


def _qkv_views(name, qkv3, views=None):
    _, S, _ = qkv3.shape
    W = 512
    tm = _tile(S, 2 * ROW_TILE)
    dilated = [(g, d) for g, d in enumerate(DILATIONS) if d > 1]
    first = dilated[0][0]
    assert [g for g, _ in dilated] == list(range(first, first + len(dilated)))
    nc = W // 128
    to_views = views is None

    def body(*refs):
        scr = refs[-nc:]
        if to_views:
            src, outs = refs[0], refs[1:1 + len(dilated)]
        else:
            ins, dst = refs[:len(dilated)], refs[len(dilated) + 1]
        for k, (_, d) in enumerate(dilated):
            @pl.when(pl.program_id(1) == k)
            def _():
                for w in range(3):
                    for c in range(nc):
                        if to_views:
                            scr[c][...] = src[w, :, c * 128:(c + 1) * 128].astype(F32)
                    for r in range(d):
                        for c in range(nc):
                            at = r * W + c * 128
                            if to_views:
                                outs[k][w, :, at:at + 128] = scr[c][pl.ds(r, tm // d, stride=d), :].astype(BF)
                            else:
                                scr[c][pl.ds(r, tm // d, stride=d), :] = ins[k][w, :, at:at + 128].astype(F32)
                    for c in range(nc):
                        if not to_views:
                            dst[w, :, c * 128:(c + 1) * 128] = scr[c][...].astype(BF)

    cols = pl.BlockSpec((3, tm, W), lambda i, k: (0, i, first + k))
    rows = [pl.BlockSpec((3, tm // d, d * W), lambda i, k: (0, i, 0)) for _, d in dilated]
    shapes = [_sds((3, S // d, d * W), BF) for _, d in dilated]
    common = dict(name=name, grid=(S // tm, len(dilated)), scratch_shapes=[pltpu.VMEM((tm, 128), F32)] * nc,
                  compiler_params=_cparams(("parallel", "arbitrary")))
    if to_views:
        outs = ORDER.call(body, [qkv3], [cols], out_specs=rows, out_shape=shapes, **common)
        return {d: o for (_, d), o in zip(dilated, outs)}
    return ORDER.call(body, [views[d] for _, d in dilated] + [qkv3], rows + [pl.BlockSpec(memory_space=pl.ANY)],
                      out_specs=cols, out_shape=_sds(qkv3.shape, BF), input_output_aliases={len(dilated): 0}, **common)


def _attn_a_fwd(qkv3, group, d):
    L = qkv3.shape[1]
    S = L * d
    assert L % QB == 0
    side, heads = _chains(L)
    view, blocks_spec = _dilated_view(qkv3, group, d, heads)

    def body(qkv_ref, o_ref, lse_ref):
        coefs = _alibi_coefs(group, d, heads)

        def step(i, carry):
            chains = [(hh, _band_scores(qkv_ref, side * i + u, L, coefs[hh], hh))
                      for u in range(side) for hh in range(heads)]
            soft = []
            for hh, (q0, _, _, _, v, s, _) in chains:
                m = jnp.max(s, axis=-1, keepdims=True)
                p = jnp.exp(s - m)
                den = jnp.sum(p, axis=-1, keepdims=True)
                soft.append((hh, q0, (p / den).astype(BF), v, m + jnp.log(den)))
            for hh, q0, pn, v, lse in soft:
                lanes = pl.ds(hh * HEAD_DIM, HEAD_DIM)
                o_ref[pl.ds(q0, QB), lanes] = jnp.dot(pn, v, preferred_element_type=F32)
                lse_ref[pl.ds(q0, QB), lanes] = jnp.broadcast_to(lse, (QB, HEAD_DIM))
            return carry

        lax.fori_loop(0, L // QB // side, step, 0)

    per = 4 // heads
    out = pl.BlockSpec((L, heads * HEAD_DIM), lambda r, j: (0, r * per + j))
    o, lse = ORDER.call(
        body, [view], [blocks_spec],
        name=f"attn_a_fwd_d{d}", grid=(d, per),
        out_specs=[out, out],
        out_shape=[_sds((L, d * 512), F32), _sds((L, d * 512), F32)],
        compiler_params=_cparams(("parallel", "parallel")),
    )
    return o, lse


def _dilated_rows(name, arrays):
    S, W = arrays[0].shape
    tm = _tile(S, ROW_TILE)
    ds_ = [d for d in DILATIONS if d > 1]
    n = len(arrays)

    def body(*refs):
        nc = W // 128
        ins, outs, scr = refs[:n], refs[n:-nc], refs[-nc:]
        for a, src in enumerate(ins):
            for c in range(nc):
                scr[c][...] = src[:, c * 128:(c + 1) * 128].astype(F32)
            for k, d in enumerate(ds_):
                dst = outs[a * len(ds_) + k]
                for r in range(d):
                    for c in range(nc):
                        at = r * W + c * 128
                        dst[:, at:at + 128] = scr[c][pl.ds(r, tm // d, stride=d), :].astype(dst.dtype)

    row = pl.BlockSpec((tm, W), lambda i: (i, 0))
    out_specs, out_shape = [], []
    for a in arrays:
        for d in ds_:
            out_specs.append(pl.BlockSpec((tm // d, d * W), lambda i: (i, 0)))
            out_shape.append(_sds((S // d, d * W), a.dtype))
    outs = ORDER.call(body, list(arrays), [row] * n, name=name, grid=(S // tm,), out_specs=out_specs,
                      out_shape=out_shape, scratch_shapes=[pltpu.VMEM((tm, 128), F32)] * (W // 128),
                      compiler_params=_cparams(("parallel",)))
    return [{d: outs[a * len(ds_) + k] for k, d in enumerate(ds_)} for a in range(n)]


def _attn_a_combine(os_, lses):
    W = 512
    S = os_[0].shape[0] * DILATIONS[0]
    tm = _tile(S, ROW_TILE)
    nc = W // 128
    dilated = [g for g, d in enumerate(DILATIONS) if d > 1]

    def body(o0, o1, o2, l0, l1, l2, y_ref, lj_ref, *scr):
        def token_order(src, g, slot):
            d = DILATIONS[g]
            if d == 1:
                return src[...]
            bufs = scr[slot * nc:(slot + 1) * nc]
            for r in range(d):
                for c in range(nc):
                    at = r * W + c * 128
                    bufs[c][pl.ds(r, tm // d, stride=d), :] = src[:, at:at + 128]
            return jnp.concatenate([buf[...] for buf in bufs], axis=1)

        slots = {g: k for k, g in enumerate(dilated)}
        ls = [token_order(l, g, slots.get(g, 0)) for g, l in enumerate((l0, l1, l2))]
        os_tok = [token_order(o, g, len(dilated) + slots.get(g, 0)) for g, o in enumerate((o0, o1, o2))]
        m = jnp.maximum(jnp.maximum(ls[0], ls[1]), ls[2])
        es = [jnp.exp(l - m) for l in ls]
        den = es[0] + es[1] + es[2]
        y = (es[0] / den) * os_tok[0] + (es[1] / den) * os_tok[1] + (es[2] / den) * os_tok[2]
        y_ref[...] = y.astype(BF)
        lj_ref[...] = m + jnp.log(den)

    row = pl.BlockSpec((tm, W), lambda i: (i, 0))
    views = [pl.BlockSpec((tm // d, d * W), lambda i: (i, 0)) for d in DILATIONS]
    return ORDER.call(
        body, [*os_, *lses], views + views, name="attn_a_combine", grid=(S // tm,), out_specs=[row, row],
        out_shape=[_sds((S, W), BF), _sds((S, W), F32)],
        scratch_shapes=[pltpu.VMEM((tm, 128), F32)] * (2 * len(dilated) * nc),
        compiler_params=_cparams(("parallel",)),
    )


def _attn_a_bwd(qkv3, dy, y, lj, dqkv3, group, d):
    L = qkv3.shape[1]
    S = L * d
    side, heads = _chains(L)
    view, blocks_spec = _dilated_view(qkv3, group, d, heads)

    def body(qkv_ref, dy_ref, y_ref, lj_ref, *rest):
        out_ref, dk_acc, dv_acc = rest[-3:]
        coefs = _alibi_coefs(group, d, heads)
        dk_acc[...] = jnp.zeros_like(dk_acc)
        dv_acc[...] = jnp.zeros_like(dv_acc)

        def step(i, carry):
            chains = [(pl.ds(hh * HEAD_DIM, HEAD_DIM), _band_scores(qkv_ref, side * i + u, L, coefs[hh], hh))
                      for u in range(side) for hh in range(heads)]
            dys = [dy_ref[pl.ds(c[0], QB), lanes] for lanes, c in chains]
            dps = [lax.dot_general(dyv, c[4], (NT, ((), ())), preferred_element_type=F32)
                   for dyv, (_, c) in zip(dys, chains)]
            grads = []
            for (lanes, (q0, ks, q, k, v, s, valid)), dyv, dp in zip(chains, dys, dps):
                rows = pl.ds(q0, QB)
                delta = jnp.sum(dyv.astype(F32) * y_ref[rows, lanes].astype(F32), axis=-1, keepdims=True)
                p = jnp.where(valid, jnp.exp(s - jnp.tile(lj_ref[rows, lanes], (1, _key_rows(L) // HEAD_DIM))), 0.0)
                grads.append(((p * (dp - delta)).astype(BF), p.astype(BF)))
            for (lanes, (q0, ks, q, k, v, s, valid)), dyv, (ds, pb) in zip(chains, dys, grads):
                out_ref[0, pl.ds(q0, QB), lanes] = (jnp.dot(ds, k, preferred_element_type=F32) * SCALE).astype(BF)
                keys = pl.ds(ks, _key_rows(L))
                dk_acc[keys, lanes] += lax.dot_general(ds, q, (TN, ((), ())), preferred_element_type=F32) * SCALE
                dv_acc[keys, lanes] += lax.dot_general(pb, dyv, (TN, ((), ())), preferred_element_type=F32)
            return carry

        lax.fori_loop(0, L // QB // side, step, 0)
        out_ref[1] = dk_acc[...].astype(BF)
        out_ref[2] = dv_acc[...].astype(BF)

    per = 4 // heads
    width = heads * HEAD_DIM
    row = pl.BlockSpec((L, width), lambda r, j: (0, r * per + j))
    operands = [view, dy, y, lj]
    scratch = [pltpu.VMEM((L, width), F32), pltpu.VMEM((L, width), F32)]
    if d == 1:
        return ORDER.call(
            body, operands + [dqkv3], [blocks_spec, row, row, row, pl.BlockSpec(memory_space=pl.ANY)],
            name=f"attn_a_bwd_d{d}", grid=(d, per), out_specs=blocks_spec, out_shape=_sds((3, S, QKV_W), BF),
            scratch_shapes=scratch, input_output_aliases={4: 0}, compiler_params=_cparams(("parallel", "parallel")))
    return ORDER.call(
        body, operands, [blocks_spec, row, row, row], name=f"attn_a_bwd_d{d}", grid=(d, per),
        out_specs=blocks_spec, out_shape=_sds((3, L, d * 512), BF),
        scratch_shapes=scratch, compiler_params=_cparams(("parallel", "parallel")))


def _toeplitz_onehot():
    oh = np.zeros((64, GRID_W, 128), np.float32)
    for qc in range(GRID_W):
        for m in range(128):
            kc = m % GRID_W
            dc = int(np.clip(kc - qc, -(NA_COLS - 1), NA_COLS - 1)) + NA_COLS - 1
            oh[(m // GRID_W) * 32 + dc, qc, m] = 1.0
    return oh.reshape(64, GRID_W * 128)


def _nbr_scores(qkv_ref, e2_ref, r, rows, ok):
    rs = jnp.clip(r - NA_ROWS // 2, 0, rows - NA_ROWS)
    q0 = pl.multiple_of(r * GRID_W, GRID_W)
    k0 = pl.multiple_of(rs * GRID_W, GRID_W)
    q = qkv_ref[0, pl.ds(q0, GRID_W), :]
    k = qkv_ref[1, pl.ds(k0, NA_ROWS * GRID_W), :]
    v = qkv_ref[2, pl.ds(k0, NA_ROWS * GRID_W), :]
    s = lax.dot_general(q, k, (NT, ((), ())), preferred_element_type=F32) * SCALE
    first = rs - r + NA_ROWS - 1
    bias = jnp.concatenate([e2_ref[first + 2 * pair] for pair in range(NA_ROWS // 2)], axis=1)
    s = jnp.where(ok, s + bias, NEG)
    return q0, k0, first, q, k, v, s


def _nbr_col_ok():
    qc = lax.broadcasted_iota(jnp.int32, (GRID_W, NA_ROWS * GRID_W), 0)
    kc = lax.broadcasted_iota(jnp.int32, (GRID_W, NA_ROWS * GRID_W), 1) % GRID_W
    cs = jnp.clip(qc - NA_COLS // 2, 0, GRID_W - NA_COLS)
    return (kc >= cs) & (kc < cs + NA_COLS)


def _attn_b_fwd(qkv3, e2):
    _, S, _ = qkv3.shape
    rows = S // GRID_W
    assert rows >= NA_ROWS

    def body(qkv_ref, e2_ref, o_ref, lse_ref):
        ok = _nbr_col_ok()

        def step(i, carry):
            blocks = [_nbr_scores(qkv_ref, e2_ref, NBR_SIDE * i + u, rows, ok) for u in range(NBR_SIDE)]
            soft = []
            for q0, _, _, _, _, v, s in blocks:
                m = jnp.max(s, axis=-1, keepdims=True)
                p = jnp.exp(s - m)
                den = jnp.sum(p, axis=-1, keepdims=True)
                soft.append((q0, (p / den).astype(BF), v, m + jnp.log(den)))
            for q0, pn, v, lse in soft:
                o_ref[pl.ds(q0, GRID_W), :] = jnp.dot(pn, v, preferred_element_type=F32).astype(BF)
                lse_ref[pl.ds(q0, GRID_W), :] = jnp.broadcast_to(lse, (GRID_W, HEAD_DIM))
            return carry

        lax.fori_loop(0, rows // NBR_SIDE, step, 0)

    out = pl.BlockSpec((S, HEAD_DIM), lambda h: (0, h))
    return ORDER.call(
        body, [qkv3, e2],
        [pl.BlockSpec((3, S, HEAD_DIM), lambda h: (0, 0, N_HEADS_A + h)),
         pl.BlockSpec((None, RPB_ROWS - 1, GRID_W, 128), lambda h: (h, 0, 0, 0))],
        name="attn_b_fwd", grid=(4,),
        out_specs=[out, out], out_shape=[_sds((S, 512), BF), _sds((S, 512), F32)],
        compiler_params=_cparams(("parallel",)),
    )


def _attn_b_bwd(qkv3, e2, dy, y, lse, dqkv3):
    _, S, _ = qkv3.shape
    rows = S // GRID_W
    nk = NA_ROWS * GRID_W

    def body(qkv_ref, e2_ref, dy_ref, y_ref, lse_ref, _, out_ref, de2_ref, dk_acc, dv_acc):
        ok = _nbr_col_ok()
        dk_acc[...] = jnp.zeros_like(dk_acc)
        dv_acc[...] = jnp.zeros_like(dv_acc)
        de2_ref[...] = jnp.zeros_like(de2_ref)

        def step(i, carry):
            blocks = [_nbr_scores(qkv_ref, e2_ref, NBR_SIDE * i + u, rows, ok) for u in range(NBR_SIDE)]
            dys = [dy_ref[pl.ds(b[0], GRID_W), :] for b in blocks]
            dps = [lax.dot_general(dyv, b[5], (NT, ((), ())), preferred_element_type=F32) for dyv, b in zip(dys, blocks)]
            grads = []
            for (q0, k0, first, q, k, v, s), dyv, dp in zip(blocks, dys, dps):
                qrows = pl.ds(q0, GRID_W)
                delta = jnp.sum(dyv.astype(F32) * y_ref[qrows, :].astype(F32), axis=-1, keepdims=True)
                p = jnp.where(ok, jnp.exp(s - jnp.tile(lse_ref[qrows, :], (1, nk // HEAD_DIM))), 0.0)
                ds = p * (dp - delta)
                for pair in range(NA_ROWS // 2):
                    de2_ref[first + 2 * pair] += ds[:, pair * 128:(pair + 1) * 128]
                grads.append((ds.astype(BF), p.astype(BF)))
            for (q0, k0, first, q, k, v, s), dyv, (dsb, pb) in zip(blocks, dys, grads):
                out_ref[0, pl.ds(q0, GRID_W), :] = (jnp.dot(dsb, k, preferred_element_type=F32) * SCALE).astype(BF)
                keys = pl.ds(k0, nk)
                dk_acc[keys, :] += lax.dot_general(dsb, q, (TN, ((), ())), preferred_element_type=F32) * SCALE
                dv_acc[keys, :] += lax.dot_general(pb, dyv, (TN, ((), ())), preferred_element_type=F32)
            return carry

        lax.fori_loop(0, rows // NBR_SIDE, step, 0)
        out_ref[1] = dk_acc[...].astype(BF)
        out_ref[2] = dv_acc[...].astype(BF)

    heads = pl.BlockSpec((3, S, HEAD_DIM), lambda h: (0, 0, N_HEADS_A + h))
    row = pl.BlockSpec((S, HEAD_DIM), lambda h: (0, h))
    table = pl.BlockSpec((None, RPB_ROWS - 1, GRID_W, 128), lambda h: (h, 0, 0, 0))
    return ORDER.call(
        body, [qkv3, e2, dy, y, lse, dqkv3],
        [heads, table, row, row, row, pl.BlockSpec(memory_space=pl.ANY)], name="attn_b_bwd", grid=(4,),
        out_specs=[heads, table],
        out_shape=[_sds((3, S, QKV_W), BF), _sds((4, RPB_ROWS - 1, GRID_W, 128), F32)],
        scratch_shapes=[pltpu.VMEM((S, HEAD_DIM), F32), pltpu.VMEM((S, HEAD_DIM), F32)],
        input_output_aliases={5: 0},
        compiler_params=_cparams(("parallel",)), chain_output=1,
    )


def _rpb_to_table(rpb):
    pad = jnp.pad(rpb, ((0, 0), (0, 0), (0, 1)))
    pairs = jnp.concatenate([pad[:, :-1], pad[:, 1:]], axis=-1).reshape(4 * (RPB_ROWS - 1), 64)
    onehot = jnp.asarray(_toeplitz_onehot())
    n = onehot.shape[1]
    tn = 2048
    full = lambda i, j, k: (0, 0)
    (e2,) = _matmul("rpb_table", pairs, onehot, pl.BlockSpec(pairs.shape, full),
                    pl.BlockSpec((64, tn), lambda i, j, k: (0, j)), NN, (1, n // tn, 1), (pairs.shape[0], tn), [],
                    [(_sds((pairs.shape[0], n), F32), pl.BlockSpec((pairs.shape[0], tn), lambda i, j, k: (0, j)))],
                    _store(F32), precision=lax.Precision.HIGHEST)
    return e2.reshape(4, RPB_ROWS - 1, GRID_W, 128)


def _table_grad_to_rpb(de2):
    onehot = jnp.asarray(_toeplitz_onehot())
    n = onehot.shape[1]
    flat = de2.reshape(4 * (RPB_ROWS - 1), n)
    tk = 2048
    (dpairs,) = _matmul("rpb_table_grad", flat, onehot, pl.BlockSpec((flat.shape[0], tk), lambda i, j, k: (0, k)),
                        pl.BlockSpec((64, tk), lambda i, j, k: (0, k)), NT, (1, 1, n // tk), (flat.shape[0], 64), [],
                        [(_sds((flat.shape[0], 64), F32), pl.BlockSpec((flat.shape[0], 64), lambda i, j, k: (0, 0)))],
                        _store(F32), precision=lax.Precision.HIGHEST)
    dpairs = dpairs.reshape(4, RPB_ROWS - 1, 64)
    zero = jnp.zeros((4, 1, RPB_COLS), F32)
    return (jnp.concatenate([dpairs[:, :, :RPB_COLS], zero], axis=1)
            + jnp.concatenate([zero, dpairs[:, :, 32:32 + RPB_COLS]], axis=1))


HBM = pl.BlockSpec(memory_space=pl.ANY)


def _place():
    x, y, c = lax.axis_index("x"), lax.axis_index("y"), lax.axis_index("c")
    chips = [(1 - x, y), (x, 1 - y), (1 - x, 1 - y)]
    return x, y, c, chips


def _remote(src, dst, send_sem, recv_sem, to):
    return pltpu.make_async_remote_copy(src_ref=src, dst_ref=dst, send_sem=send_sem, recv_sem=recv_sem,
                                        device_id=to, device_id_type=MESH)


def _place_shard(name, w, me, plain=False):
    R, C = w.shape
    tr = _tile(R, 256)

    def body(me_ref, w_ref, *o_refs):
        for o_ref in o_refs:
            o_ref[...] = w_ref[...].astype(BF)

    row = pl.BlockSpec((tr, C), lambda i, mr: (i, 0))
    placed = pl.BlockSpec((None, tr, C), lambda i, mr: (mr[0], i, 0))
    return ORDER.call(
        body, [w], [row], prefetch=(me,), name=name, grid=(R // tr,),
        out_specs=[placed, row] if plain else [placed],
        out_shape=[_sds((N_CHIPS, R, C), BF)] + ([_sds((R, C), BF)] if plain else []),
        compiler_params=_cparams(("parallel",)),
    )


SEM = pl.BlockSpec(memory_space=pltpu.SEMAPHORE)
IN_HBM = pl.BlockSpec(memory_space=pltpu.HBM)
DATAFLOW = pltpu.SideEffectType.DATAFLOW_SIDE_EFFECTING


def _in_hbm(a):
    return pltpu.with_memory_space_constraint(a, pltpu.HBM)


def _copy_start(name, bufs, copies, n_copies, earlier=None):
    n = len(bufs)
    after = None if any(b is ORDER.last for b in bufs) else ORDER.last
    n_extra = (2 if earlier is not None else 0) + (1 if after is not None else 0)

    def body(*refs):
        ins = refs[:n]
        if earlier is not None:
            for k, (src, dst, to) in enumerate(earlier[0](ins)):
                cp = _remote(src, dst, refs[n].at[k], refs[n + 1].at[k], to)
                cp.wait_send()
                cp.wait_recv()
        send_sems, recv_sems = refs[n + n_extra], refs[n + n_extra + 1]
        for k, (src, dst, to) in enumerate(copies(ins)):
            _remote(src, dst, send_sems.at[k], recv_sems.at[k], to).start()
        refs[-1][...] = jnp.zeros((8, 128), F32)

    operands = [_in_hbm(b) for b in bufs]
    in_specs = [IN_HBM] * n
    if earlier is not None:
        operands += [earlier[1], earlier[2]]
        in_specs += [SEM, SEM]
    if after is not None:
        operands.append(after)
        in_specs.append(HBM)
    outs = pl.pallas_call(
        body, name=name,
        out_shape=(pltpu.SemaphoreType.DMA((n_copies,)), pltpu.SemaphoreType.DMA((n_copies,)),
                   *[pltpu.HBM(b.shape, b.dtype) for b in bufs], _sds((8, 128), F32)),
        in_specs=in_specs,
        out_specs=(SEM, SEM, *[IN_HBM] * n, pl.BlockSpec(memory_space=pltpu.VMEM)),
        input_output_aliases={i: 2 + i for i in range(n)},
        compiler_params=pltpu.CompilerParams(has_side_effects=DATAFLOW),
    )(*operands)
    ORDER.last = outs[-1]
    return outs[0], outs[1], list(outs[2:2 + n])


def _copy_wait(name, bufs, copies, send_sems, recv_sems):
    n = len(bufs)
    after = ORDER.last

    def body(*refs):
        ins = refs[:n]
        for k, (src, dst, to) in enumerate(copies(ins)):
            cp = _remote(src, dst, refs[n].at[k], refs[n + 1].at[k], to)
            cp.wait_send()
            cp.wait_recv()

    outs = list(pl.pallas_call(
        body, name=name,
        out_shape=tuple(pltpu.HBM(b.shape, b.dtype) for b in bufs),
        in_specs=[IN_HBM] * n + [SEM, SEM, HBM], out_specs=tuple([IN_HBM] * n),
        input_output_aliases={i: i for i in range(n)},
        compiler_params=pltpu.CompilerParams(has_side_effects=DATAFLOW),
    )(*bufs, send_sems, recv_sems, after))
    ORDER.last = outs[0]
    return outs


def _gather_hop1(bufs):
    x, y, c, chips = _place()
    out = []
    for b in bufs:
        half = b.shape[1] // 2
        mine = b.at[2 * x + y, pl.ds(c * half, half), :]
        out += [(mine, mine, (*chip, c)) for chip in chips]
    return out


def _gather_hop2(bufs):
    x, y, c, chips = _place()
    out = []
    for b in bufs:
        half = b.shape[1] // 2
        for chip in chips:
            landed = b.at[2 * chip[0] + chip[1], pl.ds(c * half, half), :]
            out.append((landed, landed, (x, y, 1 - c)))
    return out


def _swap_copies(bufs):
    x, y, c, _ = _place()
    n = len(bufs) // 2
    out = []
    for p, land in zip(bufs[:n], bufs[n:]):
        half = p.shape[1] // 2
        out.append((p.at[:, pl.ds((1 - c) * half, half), :], land, (x, y, 1 - c)))
    return out


def _scatter_copies(bufs):
    _, _, c, chips = _place()
    n = len(bufs) // 2
    out = []
    for s_, land in zip(bufs[:n], bufs[n:]):
        out += [(s_.at[2 * chip[0] + chip[1]], land.at[j], (*chip, c)) for j, chip in enumerate(chips)]
    return out


def _join_copies(bufs):
    x, y, c, _ = _place()
    out = []
    for b in bufs:
        half = b.shape[0] // 2
        mine = b.at[pl.ds(c * half, half), :]
        out.append((mine, mine, (x, y, 1 - c)))
    return out


def _gather_small(vec):
    m_per, n = vec.shape

    def body(x_ref, out_ref, send_sems, recv_sems, local_sem):
        x, y, c, chips = _place()
        me, sibling = (x, y, c), (x, y, 1 - c)

        def rows(px, py, pc):
            return out_ref.at[pl.ds((4 * px + 2 * py + pc) * m_per, m_per), :]

        def copy(k, block, to, src=None):
            return _remote(rows(*block) if src is None else src, rows(*block), send_sems.at[k], recv_sems.at[k], to)

        mine = pltpu.make_async_copy(x_ref, rows(*me), local_sem)
        mine.start()
        first = [copy(0, me, sibling, src=x_ref)]
        first += [copy(1 + j, me, (*chip, c), src=x_ref) for j, chip in enumerate(chips)]
        for cp in first:
            cp.start()
        passed = [copy(4 + j, (*chip, c), sibling) for j, chip in enumerate(chips)]
        for j, chip in enumerate(chips):
            copy(1 + j, (*chip, c), me).wait_recv()
            passed[j].start()
        copy(0, sibling, me).wait_recv()
        for j, chip in enumerate(chips):
            copy(4 + j, (*chip, 1 - c), me).wait_recv()
        for cp in first + passed:
            cp.wait_send()
        mine.wait()

    return ORDER.call(
        body, [vec], [pl.BlockSpec(memory_space=pltpu.VMEM)], name="gather_small_grads",
        out_shape=_sds((8 * m_per, n), vec.dtype), out_specs=pl.BlockSpec(memory_space=pltpu.VMEM),
        scratch_shapes=[pltpu.SemaphoreType.DMA((7,)), pltpu.SemaphoreType.DMA((7,)), pltpu.SemaphoreType.DMA],
    )


def _add_sibling(name, partial, received, c):
    _, R, C = partial.shape
    half = R // 2
    tr = _tile(half, 256)
    nb = half // tr

    def body(c_ref, p_ref, r_ref, o_ref):
        o_ref[...] = (p_ref[...].astype(F32) + r_ref[...].astype(F32)).astype(BF)

    return ORDER.call(
        body, [partial, received],
        [pl.BlockSpec((None, tr, C), lambda j, i, cr: (j, cr[0] * nb + i, 0)),
         pl.BlockSpec((None, tr, C), lambda j, i, cr: (j, i, 0))],
        prefetch=(c,), name=name, grid=(N_CHIPS, nb),
        out_specs=pl.BlockSpec((None, tr, C), lambda j, i, cr: (j, i, 0)),
        out_shape=_sds((N_CHIPS, half, C), BF), compiler_params=_cparams(("parallel", "parallel")),
    )


def _add_chips(name, sums, received, me_c):
    _, half, C = sums.shape
    tr = _tile(half, 256)
    nb = half // tr

    def body(mc_ref, s_ref, r_ref, o_ref):
        acc = s_ref[...].astype(F32)
        for j in range(3):
            acc = acc + r_ref[j].astype(F32)
        o_ref[...] = acc

    return ORDER.call(
        body, [sums, received],
        [pl.BlockSpec((None, tr, C), lambda i, mc: (mc[0], i, 0)),
         pl.BlockSpec((3, tr, C), lambda i, mc: (0, i, 0))],
        prefetch=(me_c,), name=name, grid=(nb,),
        out_specs=pl.BlockSpec((tr, C), lambda i, mc: (mc[1] * nb + i, 0)),
        out_shape=_sds((2 * half, C), F32), compiler_params=_cparams(("parallel",)),
    )


def _adamw_math(w, g, m, v):
    m = ADAM_B1 * m + (1.0 - ADAM_B1) * g
    v = ADAM_B2 * v + (1.0 - ADAM_B2) * (g * g)
    m_hat = m / (1.0 - ADAM_B1 ** ADAM_STEP)
    v_hat = v / (1.0 - ADAM_B2 ** ADAM_STEP)
    delta = -ADAM_LR * (m_hat / (jnp.sqrt(v_hat) + ADAM_EPS) + ADAM_WD * w)
    return delta, m, v


def _adamw(name, w, g, m, v):
    R, C = w.shape
    tr = _tile(R, 256)

    def body(w_ref, g_ref, m_ref, v_ref, go_ref, d_ref, mo_ref, vo_ref):
        gv = g_ref[...]
        go_ref[...] = gv
        d_ref[...], mo_ref[...], vo_ref[...] = _adamw_math(w_ref[...], gv, m_ref[...], v_ref[...])

    row = pl.BlockSpec((tr, C), lambda i: (i, 0))
    return ORDER.call(
        body, [w, g, m, v], [row] * 4, name=name, grid=(R // tr,), out_specs=[row] * 4,
        out_shape=[_sds((R, C), F32)] * 4, compiler_params=_cparams(("parallel",)), chain_output=1,
    )


def _adamw_small(gathered, w, m, v):
    rows, n = w.shape

    def body(ga_ref, w_ref, m_ref, v_ref, go_ref, d_ref, mo_ref, vo_ref):
        g = ga_ref[pl.ds(0, rows), :]
        for dev in range(1, 8):
            g = g + ga_ref[pl.ds(dev * rows, rows), :]
        go_ref[...] = g
        d_ref[...], mo_ref[...], vo_ref[...] = _adamw_math(w_ref[...], g, m_ref[...], v_ref[...])

    whole = pl.BlockSpec(memory_space=pltpu.VMEM)
    return ORDER.call(
        body, [gathered, w, m, v], [whole] * 4, name="adamw_small", out_specs=[whole] * 4,
        out_shape=[_sds((rows, n), F32)] * 4, compiler_params=_cparams(), chain_output=1,
    )


def _proj_merge(y_a, y_b, gpa, gpb, g3):
    S, K = y_a.shape
    _, _, Nq = gpa.shape
    D = N_CHIPS * Nq
    tm, tn = _tile(S, 1024), _tile(Nq, 512)
    q = Nq // tn

    def body(ya_ref, yb_ref, wa_ref, wb_ref, g_ref, merged_ref, c_ref):
        pa = jnp.dot(ya_ref[...], wa_ref[...], preferred_element_type=F32)
        pb = jnp.dot(yb_ref[...], wb_ref[...], preferred_element_type=F32)
        g = g_ref[...].astype(F32)
        merged_ref[...] = (g[0] * pa + g[1] * pb).astype(BF)
        c_ref[0] = (pa * g[0] * (1.0 - g[0])).astype(BF)
        c_ref[1] = (pb * g[1] * (1.0 - g[1])).astype(BF)

    rows = pl.BlockSpec((tm, K), lambda i, j: (i, 0))
    weight = pl.BlockSpec((None, K, tn), lambda i, j: (j // q, 0, j % q))
    pair = pl.BlockSpec((2, tm, tn), lambda i, j: (0, i, j))
    return ORDER.call(
        body, [y_a, y_b, gpa, gpb, g3], [rows, rows, weight, weight, pair], name="proj_merge",
        grid=(S // tm, N_CHIPS * q), out_specs=[pl.BlockSpec((tm, tn), lambda i, j: (i, j)), pair],
        out_shape=[_sds((S, D), BF), _sds((2, S, D), BF)], compiler_params=_cparams(("parallel", "parallel")))


def _out_proj_dx(dx1b, wout, g3, c3, gpa, gpb):
    S, D = dx1b.shape
    _, K, Nq = gpa.shape
    tm, tn = _tile(S, 1024), Nq
    nj = D // tn

    def body(a_ref, w_ref, g_ref, c_ref, wa_ref, wb_ref, dpa_ref, dpb_ref, dg_ref, db_ref, dya_ref, dyb_ref,
             acc_a, acc_b):
        j = pl.program_id(1)
        dm = lax.dot_general(a_ref[...], w_ref[...], (NT, ((), ())), preferred_element_type=F32)
        g, c = g_ref[...].astype(F32), c_ref[...].astype(F32)
        dpa, dpb = (dm * g[0]).astype(BF), (dm * g[1]).astype(BF)
        dpa_ref[...] = dpa
        dpb_ref[...] = dpb
        dga, dgb = dm * c[0], dm * c[1]
        dg_ref[0] = dga.astype(BF)
        dg_ref[1] = dgb.astype(BF)
        db_ref[...] = jnp.concatenate([jnp.sum(dga, axis=0, keepdims=True), jnp.sum(dgb, axis=0, keepdims=True)], 0)
        ya = lax.dot_general(dpa, wa_ref[...], (NT, ((), ())), preferred_element_type=F32)
        yb = lax.dot_general(dpb, wb_ref[...], (NT, ((), ())), preferred_element_type=F32)

        @pl.when(j == 0)
        def _():
            acc_a[...] = ya
            acc_b[...] = yb

        @pl.when(j > 0)
        def _():
            acc_a[...] += ya
            acc_b[...] += yb

        @pl.when(j == nj - 1)
        def _():
            dya_ref[...] = acc_a[...].astype(BF)
            dyb_ref[...] = acc_b[...].astype(BF)

    tile = pl.BlockSpec((tm, tn), lambda i, j: (i, j))
    pair = pl.BlockSpec((2, tm, tn), lambda i, j: (0, i, j))
    shard = pl.BlockSpec((None, K, tn), lambda i, j: (j, 0, 0))
    rows = pl.BlockSpec((tm, K), lambda i, j: (i, 0))
    return ORDER.call(
        body, [dx1b, wout, g3, c3, gpa, gpb],
        [pl.BlockSpec((tm, D), lambda i, j: (i, 0)), pl.BlockSpec((tn, D), lambda i, j: (j, 0)), pair, pair, shard, shard],
        name="out_proj_dx", grid=(S // tm, nj),
        out_specs=[tile, tile, pair, pl.BlockSpec((None, 2, tn), lambda i, j: (i, 0, j)), rows, rows],
        out_shape=[_sds((S, D), BF), _sds((S, D), BF), _sds((2, S, D), BF), _sds((S // tm, 2, D), F32),
                   _sds((S, K), BF), _sds((S, K), BF)],
        scratch_shapes=[pltpu.VMEM((tm, K), F32), pltpu.VMEM((tm, K), F32)],
        compiler_params=_cparams(("parallel", "arbitrary")))


class _Exchange:
    GATHER = (("qkv",), ("gate",), ("proj_a", "proj_b", "out"), ("up",), ("down",))
    REDUCE = {"mlp": ("down", "up"), "mix": ("out", "proj_a", "proj_b"), "in": ("qkv", "gate")}

    OWN_FIRST = ("qkv", "gate")

    def __init__(self, shards, me, c, moments):
        self.me, self.c = me, c
        self.shards, self.moments = shards, moments
        self.hop1, self.hop2, self.stage, self.grads, self.own, self.updates = {}, {}, {}, {}, {}, {}
        for g, names in enumerate(self.GATHER):
            bufs = []
            for n in names:
                placed = _place_shard(f"place_{n}", shards[n], me, plain=n in self.OWN_FIRST)
                bufs.append(placed[0])
                if n in self.OWN_FIRST:
                    self.own[n] = placed[1]
            self.hop1[g] = _copy_start(f"gather{g}_start", bufs, _gather_hop1, 3 * len(names))

    def forward(self, g):
        send, recv, thru = self.hop1.pop(g)
        self.hop2[g] = _copy_start(f"gather{g}_forward", thru, _gather_hop2, len(thru) * 3,
                                   earlier=(_gather_hop1, send, recv))

    def weights(self, g):
        send, recv, thru = self.hop2.pop(g)
        return _copy_wait(f"gather{g}_wait", thru, _gather_hop2, send, recv)

    def adamw_beside(self, name):
        def update(w, g, m, v):
            return (g,) + _adamw_math(w, g, m, v)
        return update, [self.shards[name], self.grads[name], *self.moments[name]], 4

    def reduce(self, key, partials=None):
        names = self.REDUCE[key]
        n = len(names)
        if partials is not None:
            lands = [lax.empty((p.shape[0], p.shape[1] // 2, p.shape[2]), p.dtype) for p in partials]
            self.stage[key] = ("swap",) + _copy_start(f"reduce_{key}_swap", list(partials) + lands, _swap_copies, n)
            return
        kind, send, recv, thru = self.stage.pop(key)
        if kind == "swap":
            thru = _copy_wait(f"reduce_{key}_swap_wait", thru, _swap_copies, send, recv)
            sums = [_add_sibling(f"reduce_{nm}_add_sibling", p, r, self.c)
                    for nm, p, r in zip(names, thru[:n], thru[n:])]
            lands = [lax.empty((3,) + s_.shape[1:], s_.dtype) for s_ in sums]
            self.stage[key] = ("scatter",) + _copy_start(f"reduce_{key}_scatter", sums + lands, _scatter_copies, 3 * n)
        elif kind == "scatter":
            thru = _copy_wait(f"reduce_{key}_scatter_wait", thru, _scatter_copies, send, recv)
            me_c = jnp.concatenate([self.me, self.c])
            halves = [_add_chips(f"reduce_{nm}_add_chips", s_, r, me_c)
                      for nm, s_, r in zip(names, thru[:n], thru[n:])]
            self.stage[key] = ("join",) + _copy_start(f"reduce_{key}_join", halves, _join_copies, n)
        else:
            thru = _copy_wait(f"reduce_{key}_join_wait", thru, _join_copies, send, recv)
            self.grads.update(zip(names, thru))


def _forward_backward(x, target, norm_mix, b_gate, rpb, norm_mlp, norm_final, ex):
    S, D = x.shape

    h1 = _rms_fwd("rms_mix", x, norm_mix)
    nq = QKV_W // 512
    qkv_out = (((3, S, QKV_W), BF), lambda i, T: (T // nq, i, T % nq))
    tg = _tile(ex.own["gate"].shape[1], 1024)
    ng = D // tg
    gate_out = (((2, S, D), BF), lambda i, T: (T // ng, i, T % ng))

    def gate_epilogue(acc, ex_, outs):
        outs[0][...] = jax.nn.sigmoid(acc + ex_[0][...]).astype(BF)

    qkv3 = _mm_nn_shards("qkv_own", h1, ex.own["qkv"], ex.me, True, *qkv_out, _store(BF), tm=2048)
    g3 = _mm_nn_shards("gate_own", h1, ex.own["gate"], ex.me, True, *gate_out, gate_epilogue, extras=[b_gate], tn=tg)
    ex.forward(0)
    e2 = _rpb_to_table(rpb)
    (gq,) = ex.weights(0)
    qkv3 = _mm_nn_shards("qkv", h1, gq, ex.me, False, *qkv_out, _store(BF), into=qkv3, tm=2048)

    ex.forward(1)
    outs_a = [_attn_a_fwd(qkv3, 0, DILATIONS[0])]
    (gg,) = ex.weights(1)
    g3 = _mm_nn_shards("gate", h1, gg, ex.me, False, *gate_out, gate_epilogue, extras=[b_gate], into=g3, tn=tg)

    ex.forward(2)
    qkv_views = _qkv_views("qkv_views", qkv3)
    outs_a += [_attn_a_fwd(qkv_views[d], grp, d) for grp, d in enumerate(DILATIONS) if grp > 0]
    y_a, lj = _attn_a_combine([o for o, _ in outs_a], [l for _, l in outs_a])
    y_b, lse_b = _attn_b_fwd(qkv3, e2)
    gpa, gpb, gout = ex.weights(2)
    wout = gout.reshape(D, D)
    merged, c3 = _proj_merge(y_a, y_b, gpa, gpb, g3)

    def residual_epilogue(acc, ex_, outs):
        outs[0][...] = acc + ex_[0][...]

    def residual_norm_epilogue(acc, ex_, outs):
        x1v = acc + ex_[0][...]
        outs[0][...] = x1v
        r = lax.rsqrt(jnp.mean(x1v * x1v, axis=-1, keepdims=True) + EPS)
        outs[1][...] = ((x1v * r) * ex_[1][...]).astype(BF)

    def nn_plain(name, a, w, res, bm=1024, bn=1024, norm=None):
        M, K = a.shape
        N = w.shape[1]
        bm, bn, bk = _tile(M, bm), _tile(N, bn), _tile(K, 2048)
        t = pl.BlockSpec((bm, bn), lambda i, j, k: (i, j))
        extras, outs, epilogue = [(res, t)], [(_sds((M, N), F32), t)], residual_epilogue
        if norm is not None:
            assert bn == N
            extras.append((norm, pl.BlockSpec((1, N), lambda i, j, k: (0, 0))))
            outs.append((_sds((M, N), BF), t))
            epilogue = residual_norm_epilogue
        result = _matmul(name, a, w, pl.BlockSpec((bm, bk), lambda i, j, k: (i, k)),
                         pl.BlockSpec((bk, bn), lambda i, j, k: (k, j)), NN, (M // bm, N // bn, K // bk), (bm, bn),
                         extras, outs, epilogue)
        return result[0] if norm is None else result

    ex.forward(3)
    x1, h2 = nn_plain("out_proj", merged, wout, x, bm=512, bn=2048, norm=norm_mlp)
    (gup,) = ex.weights(3)
    F = gup.shape[2] * N_CHIPS

    def up_epilogue(acc, ex_, outs):
        ru = jnp.maximum(acc, 0.0)
        outs[0][...] = (ru * ru).astype(BF)
        outs[1][...] = ru.astype(BF)

    tu = _tile(gup.shape[2], 2048)
    ut = pl.BlockSpec((_tile(S, 1024), tu), lambda i, j, k: (i, j))
    (act, ru), _ = _mm_nn_cols("mlp_up", h2, gup, BF, epilogue=up_epilogue, tn=tu,
                               outs=[(_sds((S, F), BF), ut), (_sds((S, F), BF), ut)])
    ex.forward(4)
    (gdown,) = ex.weights(4)
    wdown = gdown.reshape(F, D)
    x2 = nn_plain("mlp_down", act, wdown, x1)

    loss, dx2, dx2b, d_norm_final = _loss_head(x2, target, norm_final.reshape(1, D))

    def nt_rows(name, a, w, epilogue, extras, outs, bn=1024):
        M, N = a.shape
        K = w.shape[0]
        bm, bn, bk = _tile(M, 1024), _tile(K, bn), _tile(N, 2048)
        return _matmul(name, a, w, pl.BlockSpec((bm, bk), lambda i, j, k: (i, k)),
                       pl.BlockSpec((bn, bk), lambda i, j, k: (j, k)), NT, (M // bm, K // bn, N // bk), (bm, bn),
                       extras(bm, bn), outs(bm, bn), epilogue)

    def nt_cols(name, a_spec_fn, a, g, M, epilogue, extras, outs, bk, bn=1024, side=None):
        _, K, Nq = g.shape
        bm, bn, bk = _tile(M, 1024), _tile(K, bn), _tile(Nq, bk)
        q = Nq // bk
        return _matmul(name, a, g, a_spec_fn(bm, bk), pl.BlockSpec((None, bn, bk), lambda i, j, k: (k // q, j, k % q)),
                       NT, (M // bm, K // bn, N_CHIPS * q), (bm, bn), extras(bm, bn), outs(bm, bn), epilogue,
                       side=side)

    def tn_grad(name, a, a_spec_fn, b, b_spec_fn, Kin, N, out_shape, out_spec_fn, bn=1024):
        bm, bn, bk = _tile(Kin, 1024), _tile(N, bn), _tile(S, 4096)
        return _matmul(name, a, b, a_spec_fn(bk, bm), b_spec_fn(bk, bn), TN, (Kin // bm, N // bn, S // bk), (bm, bn),
                       [], [(_sds(out_shape, BF), out_spec_fn(bm, bn))], _store(BF))[0]

    plain_a = lambda bk, bm: pl.BlockSpec((bk, bm), lambda i, j, k: (k, i))
    plain_b = lambda bk, bn: pl.BlockSpec((bk, bn), lambda i, j, k: (k, j))
    plain_o = lambda bm, bn: pl.BlockSpec((bm, bn), lambda i, j, k: (i, j))
    a_rows = lambda bm, bk: pl.BlockSpec((bm, bk), lambda i, j, k: (i, k))

    def cols_o(Nq):
        def spec(bm, bn):
            q = Nq // bn
            return pl.BlockSpec((None, bm, bn), lambda i, j, k: (j // q, i, j % q))
        return spec

    def du_epilogue(acc, ex_, outs):
        outs[0][...] = (acc * (2.0 * ex_[0][...].astype(F32))).astype(BF)

    dw_down = tn_grad("mlp_down_dw", act, plain_a, dx2b, plain_b, F, D, (F, D), plain_o)
    (du,) = nt_rows("mlp_down_dx", dx2b, wdown, du_epilogue,
                    lambda bm, bn: [(ru, plain_o(bm, bn))], lambda bm, bn: [(_sds((S, F), BF), plain_o(bm, bn))],
                    bn=2048)

    fq = gup.shape[2]
    dw_up = tn_grad("mlp_up_dw", h2, plain_a, du, plain_b, D, F, (N_CHIPS, D, fq), cols_o(fq), bn=min(fq, 1024))
    ex.reduce("mlp", partials=[dw_down.reshape(N_CHIPS, F // N_CHIPS, D), dw_up])
    (dh2,) = nt_cols("mlp_up_dx", a_rows, du, gup, S, _store(F32), lambda bm, bn: [],
                     lambda bm, bn: [(_sds((S, D), F32), plain_o(bm, bn))], 1024, bn=2048)
    ex.reduce("mlp")
    dx1, dx1b, d_norm_mlp = _rms_bwd("rms_mlp_bwd", dh2, x1, norm_mlp, dx2)

    dpa, dpb, dg3, db_gate, dy_a, dy_b = _out_proj_dx(dx1b, wout, g3, c3, gpa, gpb)
    dw_out = tn_grad("out_proj_dw", merged, plain_a, dx1b, plain_b, D, D, (D, D), plain_o)

    pq = gpa.shape[2]
    dw_pa = tn_grad("proj_a_dw", y_a, plain_a, dpa, plain_b, 512, D, (N_CHIPS, 512, pq), cols_o(pq), bn=min(pq, 512))
    dw_pb = tn_grad("proj_b_dw", y_b, plain_a, dpb, plain_b, 512, D, (N_CHIPS, 512, pq), cols_o(pq), bn=min(pq, 512))
    ex.reduce("mix", partials=[dw_out.reshape(N_CHIPS, D // N_CHIPS, D), dw_pa, dw_pb])

    dqkv3 = lax.empty((3, S, QKV_W), BF)
    dqkv3 = _attn_a_bwd(qkv3, dy_a, y_a, lj, dqkv3, 0, DILATIONS[0])
    ex.reduce("mix")
    dy_views, y_views, lj_views = _dilated_rows("attn_a_bwd_rows", [dy_a, y_a, lj])
    dqkv_views = {d: _attn_a_bwd(qkv_views[d], dy_views[d], y_views[d], lj_views[d], None, grp, d)
                  for grp, d in enumerate(DILATIONS) if grp > 0}
    dqkv3 = _qkv_views("dqkv_from_views", dqkv3, dqkv_views)
    dqkv3, de2 = _attn_b_bwd(qkv3, e2, dy_b, y_b, lse_b, dqkv3)
    d_rpb = _table_grad_to_rpb(de2)

    def stacked_a(width):
        def spec(bm, bk):
            q = width // bk
            return pl.BlockSpec((None, bm, bk), lambda i, j, k: (k // q, i, k % q))
        return spec

    def stacked_b(width):
        def spec(bk, bn):
            q = width // bn
            return pl.BlockSpec((None, bk, bn), lambda i, j, k: (j // q, k, j % q))
        return spec

    ex.reduce("mlp")
    dw_qkv = tn_grad("qkv_dw", h1, plain_a, dqkv3, stacked_b(QKV_W), D, 3 * QKV_W, (N_CHIPS,) + gq.shape[1:],
                     cols_o(gq.shape[2]), bn=512)
    dw_gate = tn_grad("gate_dw", h1, plain_a, dg3, stacked_b(D), D, 2 * D, (N_CHIPS,) + gg.shape[1:],
                      cols_o(gg.shape[2]), bn=gg.shape[2])
    ex.reduce("in", partials=[dw_qkv, dw_gate])
    ex.reduce("mlp")
    dh1_q, *ex.updates["down"] = nt_cols(
        "qkv_dx", stacked_a(QKV_W), dqkv3, gq, S, _store(F32), lambda bm, bn: [],
        lambda bm, bn: [(_sds((S, D), F32), plain_o(bm, bn))], 512, bn=2048, side=ex.adamw_beside("down"))
    ex.reduce("in")
    ex.reduce("mix")

    def add_epilogue(acc, ex_, outs):
        outs[0][...] = acc + ex_[0][...]

    dh1, *ex.updates["up"] = nt_cols(
        "gate_dx", stacked_a(D), dg3, gg, S, add_epilogue, lambda bm, bn: [(dh1_q, plain_o(bm, bn))],
        lambda bm, bn: [(_sds((S, D), F32), plain_o(bm, bn))], gg.shape[2], side=ex.adamw_beside("up"))
    grad_x, _, d_norm_mix = _rms_bwd("rms_mix_bwd", dh1, x, norm_mix, dx1)
    ex.reduce("mix")

    small = [d_norm_mix, jnp.sum(db_gate, axis=0).reshape(1, 2 * D), d_rpb, d_norm_mlp, d_norm_final]
    return loss, grad_x, small


def _pack_small(parts, width):
    flat = jnp.concatenate([p.reshape(-1) for p in parts])
    return jnp.pad(flat, (0, 8 * width - flat.shape[0])).reshape(8, width)


def kernel(x, norm_mix, w_qkv, w_gate, b_gate, rpb, w_proj_a, w_proj_b, w_out, norm_mlp, w_up, w_down, norm_final, loss_target, m_norm_mix, m_w_qkv, m_w_gate, m_b_gate, m_rpb, m_w_proj_a, m_w_proj_b, m_w_out, m_norm_mlp, m_w_up, m_w_down, m_norm_final, v_norm_mix, v_w_qkv, v_w_gate, v_b_gate, v_rpb, v_w_proj_a, v_w_proj_b, v_w_out, v_norm_mlp, v_w_up, v_w_down, v_norm_final):
    names = ["qkv", "gate", "proj_a", "proj_b", "out", "up", "down"]
    big = dict(zip(names, [w_qkv[0], w_gate[0], w_proj_a[0], w_proj_b[0], w_out[0], w_up[0], w_down[0]]))
    big_m = dict(zip(names, [m_w_qkv[0], m_w_gate[0], m_w_proj_a[0], m_w_proj_b[0], m_w_out[0], m_w_up[0], m_w_down[0]]))
    big_v = dict(zip(names, [v_w_qkv[0], v_w_gate[0], v_w_proj_a[0], v_w_proj_b[0], v_w_out[0], v_w_up[0], v_w_down[0]]))

    c = lax.axis_index("c").astype(jnp.int32).reshape(1)
    me = (2 * lax.axis_index("x") + lax.axis_index("y")).astype(jnp.int32).reshape(1)
    ORDER.last = None
    ex = _Exchange(big, me, c, {n: (big_m[n], big_v[n]) for n in names})
    loss, grad_x, small = _forward_backward(x[0], loss_target[0], norm_mix, b_gate, rpb[0], norm_mlp, norm_final, ex)

    def adamw(group):
        return {n: ex.updates[n] if ex.updates.get(n) else _adamw(f"adamw_{n}", big[n], ex.grads[n], big_m[n], big_v[n])
                for n in _Exchange.REDUCE[group]}

    big_out = {**adamw("mlp"), **adamw("mix")}
    ex.reduce("in")

    small_w = [norm_mix, b_gate, rpb, norm_mlp, norm_final]
    count = sum(int(np.prod(p.shape)) for p in small_w)
    width = -(-count // (8 * 128)) * 128
    packed = _adamw_small(_gather_small(_pack_small(small, width)), _pack_small(small_w, width),
                          _pack_small([m_norm_mix, m_b_gate, m_rpb, m_norm_mlp, m_norm_final], width),
                          _pack_small([v_norm_mix, v_b_gate, v_rpb, v_norm_mlp, v_norm_final], width))
    ex.reduce("in")
    big_out.update(adamw("in"))

    def unpack(flat2d):
        flat, out, at = flat2d.reshape(-1), [], 0
        for p in small_w:
            size = int(np.prod(p.shape))
            out.append(flat[at:at + size].reshape(p.shape))
            at += size
        return out

    small_out = [unpack(a) for a in packed]

    def ordered(kind):
        sm = small_out[kind]
        bg = {n: o[kind][None] for n, o in big_out.items()}
        return [sm[0], bg["qkv"], bg["gate"], sm[1], sm[2], bg["proj_a"], bg["proj_b"], bg["out"], sm[3],
                bg["up"], bg["down"], sm[4]]

    total = lax.psum(loss[0, 0], ("x", "y", "c"))
    return (total, grad_x[None], *ordered(0), *ordered(1), *ordered(2), *ordered(3))
```

```python
import math

import numpy as np
import jax
import jax.numpy as jnp
from jax import lax
from jax.experimental import pallas as pl
from jax.experimental.pallas import tpu as pltpu

BF = jnp.bfloat16
F32 = jnp.float32
MESH = pl.DeviceIdType.MESH

HEAD_DIM = 128
N_HEADS = 16
N_HEADS_A = 12
QKV_W = N_HEADS * HEAD_DIM
DILATIONS = (1, 4, 16)
HALF_WINDOW = 64
GRID_W = 64
NA_ROWS = 8
NA_COLS = 16
RPB_ROWS = 2 * NA_ROWS - 1
RPB_COLS = 2 * NA_COLS - 1
EPS = 1e-6
NEG = -1e30
SCALE = HEAD_DIM ** -0.5

ADAM_LR = 0.001
ADAM_B1 = 0.9
ADAM_B2 = 0.999
ADAM_EPS = 1e-08
ADAM_WD = 0.01
ADAM_STEP = 10

N_CHIPS = 4
VMEM_LIMIT_BYTES = 48 * 1024 * 1024
QB = 256
NBR_SIDE = 64
ROW_TILE = 512


def _key_rows(L):
    return min(QB + 2 * HALF_WINDOW, L)


def _cparams(sem=None):
    return pltpu.CompilerParams(dimension_semantics=sem, vmem_limit_bytes=VMEM_LIMIT_BYTES)


def _tile(dim, want):
    t = min(dim, want)
    assert dim % t == 0, (dim, want)
    return t


class _ProgramOrder:
    def __init__(self):
        self.last = None

    def call(self, body, operands, in_specs, *, prefetch=(), grid=None, out_specs=None, chain_output=0, **kwargs):
        operands, in_specs = list(operands), list(in_specs)
        lead = len(prefetch) + len(operands)
        if self.last is not None and not any(op is self.last for op in operands):
            operands.append(self.last)
            in_specs.append(pl.BlockSpec(memory_space=pl.ANY))
            inner = body

            def body(*refs):
                return inner(*refs[:lead], *refs[lead + 1:])

        if prefetch:
            kwargs["grid_spec"] = pltpu.PrefetchScalarGridSpec(
                num_scalar_prefetch=len(prefetch), grid=grid, in_specs=in_specs, out_specs=out_specs)
        else:
            kwargs.update(in_specs=in_specs, out_specs=out_specs)
            if grid is not None:
                kwargs["grid"] = grid
        out = pl.pallas_call(body, **kwargs)(*prefetch, *operands)
        self.last = out[chain_output] if isinstance(out, (tuple, list)) else out
        return out


ORDER = _ProgramOrder()


NN = ((1,), (0,))
NT = ((1,), (1,))
TN = ((0,), (0,))


def _matmul(name, a, b, a_spec, b_spec, dims, grid, acc_shape, extras, outs, epilogue, precision=None,
            prefetch=(), into=None, side=None):
    n_ex, n_out, nk = len(extras), len(outs), grid[2]
    side_fn, side_in, n_side_out = side if side is not None else (None, [], 0)
    side_spec = None
    n_in = 2 + n_ex + len(side_in) + (into is not None)
    if side is not None:
        R, C = side_in[0].shape
        steps = grid[0] * grid[1] * grid[2]
        side_blocks = max(n for n in range(1, steps + 1) if R % n == 0 and (R // n) % 8 == 0)

        def side_step(*ids):
            return (ids[0] * grid[1] + ids[1]) * grid[2] + ids[2]

        side_spec = pl.BlockSpec((R // side_blocks, C),
                                 lambda *ids: (jnp.minimum(side_step(*ids), side_blocks - 1), 0))

    def body(*refs):
        refs = refs[len(prefetch):]
        a_ref, b_ref = refs[0], refs[1]
        ex_refs = refs[2:2 + n_ex]
        out_refs = refs[n_in:n_in + n_out]
        if side is not None:
            @pl.when(side_step(pl.program_id(0), pl.program_id(1), pl.program_id(2)) < side_blocks)
            def _():
                results = side_fn(*[r[...] for r in refs[2 + n_ex:2 + n_ex + len(side_in)]])
                for o_ref, value in zip(refs[n_in + n_out:n_in + n_out + n_side_out], results):
                    o_ref[...] = value

        def dot():
            return lax.dot_general(a_ref[...], b_ref[...], (dims, ((), ())),
                                   preferred_element_type=F32, precision=precision)

        if nk == 1:
            epilogue(dot(), ex_refs, out_refs)
            return
        acc_ref = refs[-1]
        k = pl.program_id(2)

        @pl.when(k == 0)
        def _():
            acc_ref[...] = dot()

        if nk > 2:
            @pl.when((k > 0) & (k < nk - 1))
            def _():
                acc_ref[...] += dot()

        @pl.when(k == nk - 1)
        def _():
            epilogue(acc_ref[...] + dot(), ex_refs, out_refs)

    operands = [a, b] + [e for e, _ in extras] + list(side_in)
    in_specs = [a_spec, b_spec] + [s for _, s in extras] + [side_spec] * len(side_in)
    kwargs = {}
    if into is not None:
        operands.append(into)
        in_specs.append(pl.BlockSpec(memory_space=pl.ANY))
        kwargs["input_output_aliases"] = {len(prefetch) + n_in - 1: 0}
    return ORDER.call(
        body, operands, in_specs, prefetch=prefetch, name=name, grid=grid,
        out_specs=[s for _, s in outs] + [side_spec] * n_side_out,
        out_shape=[sh for sh, _ in outs] + [_sds(s_.shape, F32) for s_ in side_in[:1]] * n_side_out,
        scratch_shapes=[pltpu.VMEM(acc_shape, F32)] if nk > 1 else [],
        compiler_params=_cparams(("parallel", "parallel", "arbitrary")), **kwargs,
    )


def _mm_nn_shards(name, a, w, me, own, out, out_block, epilogue, extras=(), into=None, tn=512, tm=1024):
    M, K = a.shape
    Nq = w.shape[-1]
    tm, tn = _tile(M, tm), _tile(Nq, tn)
    q = Nq // tn

    def tile(j, me_ref):
        shard = me_ref[0] if own else (me_ref[0] + 1 + j // q) % N_CHIPS
        return shard, j % q, shard * q + j % q

    if own:
        b_spec = pl.BlockSpec((K, tn), lambda i, j, k, me_ref: (0, j))
    else:
        b_spec = pl.BlockSpec((None, K, tn), lambda i, j, k, me_ref: (tile(j, me_ref)[0], 0, tile(j, me_ref)[1]))
    shape, dtype = out
    out_spec = pl.BlockSpec((None, tm, tn), lambda i, j, k, me_ref: out_block(i, tile(j, me_ref)[2]))
    ex = [(e, pl.BlockSpec((1, tn), lambda i, j, k, me_ref: (0, tile(j, me_ref)[2]))) for e in extras]
    return _matmul(name, a, w, pl.BlockSpec((tm, K), lambda i, j, k, me_ref: (i, 0)), b_spec, NN,
                   (M // tm, q if own else (N_CHIPS - 1) * q, 1), (tm, tn), ex, [(_sds(shape, dtype), out_spec)],
                   epilogue, prefetch=(me,), into=into)[0]


def _store(dtype):
    def epilogue(acc, ex, outs):
        outs[0][...] = acc.astype(dtype)
    return epilogue


def _sds(shape, dtype):
    return jax.ShapeDtypeStruct(shape, dtype)


def _mm_nn_cols(name, a, g, out_dtype, epilogue=None, extras=(), outs=None, tm=1024, tn=1024, tk=2048):
    M, K = a.shape
    _, _, Nq = g.shape
    tm, tn, tk = _tile(M, tm), _tile(Nq, tn), _tile(K, tk)
    q = Nq // tn
    grid = (M // tm, N_CHIPS * q, K // tk)
    if outs is None:
        outs = [(_sds((M, N_CHIPS * Nq), out_dtype), pl.BlockSpec((tm, tn), lambda i, j, k: (i, j)))]
    return _matmul(name, a, g, pl.BlockSpec((tm, tk), lambda i, j, k: (i, k)),
                   pl.BlockSpec((None, tk, tn), lambda i, j, k: (j // q, k, j % q)), NN, grid, (tm, tn),
                   list(extras), outs, epilogue or _store(out_dtype)), (tm, tn, tk)


def _rms_fwd(name, x, g):
    S, D = x.shape
    tm = _tile(S, ROW_TILE)

    def body(x_ref, g_ref, h_ref):
        xv = x_ref[...]
        r = lax.rsqrt(jnp.mean(xv * xv, axis=-1, keepdims=True) + EPS)
        h_ref[...] = ((xv * r) * g_ref[...]).astype(BF)

    row = pl.BlockSpec((tm, D), lambda i: (i, 0))
    return ORDER.call(
        body, [x, g], [row, pl.BlockSpec((1, D), lambda i: (0, 0))], name=name, grid=(S // tm,),
        out_specs=row, out_shape=_sds((S, D), BF), compiler_params=_cparams(("parallel",)),
    )


def _rms_bwd(name, dh, x, g, dres):
    S, D = x.shape
    tm = _tile(S, ROW_TILE // 2)

    def body(dh_ref, x_ref, g_ref, dres_ref, dx_ref, dxb_ref, dg_ref):
        xv = x_ref[...]
        r = lax.rsqrt(jnp.mean(xv * xv, axis=-1, keepdims=True) + EPS)
        n = xv * r
        dhv = dh_ref[...]
        dyg = dhv * g_ref[...]
        dx = dres_ref[...] + r * (dyg - n * jnp.mean(dyg * n, axis=-1, keepdims=True))
        dx_ref[...] = dx
        dxb_ref[...] = dx.astype(BF)

        @pl.when(pl.program_id(0) == 0)
        def _():
            dg_ref[...] = jnp.zeros_like(dg_ref)

        dg_ref[...] += jnp.sum(dhv * n, axis=0, keepdims=True)

    row = pl.BlockSpec((tm, D), lambda i: (i, 0))
    vec = pl.BlockSpec((1, D), lambda i: (0, 0))
    return ORDER.call(
        body, [dh, x, g, dres], [row, row, vec, row], name=name, grid=(S // tm,),
        out_specs=[row, row, vec],
        out_shape=[_sds((S, D), F32), _sds((S, D), BF), _sds((1, D), F32)],
        compiler_params=_cparams(("arbitrary",)),
    )


def _loss_head(x2, target, g):
    S, D = x2.shape
    tm = _tile(S, ROW_TILE)

    def body(x_ref, t_ref, g_ref, loss_ref, dx_ref, dxb_ref, dg_ref):
        xv = x_ref[...]
        gv = g_ref[...]
        r = lax.rsqrt(jnp.mean(xv * xv, axis=-1, keepdims=True) + EPS)
        n = xv * r
        e = n * gv - t_ref[...]
        dy = e * (1.0 / D)
        dyg = dy * gv
        dx = r * (dyg - n * jnp.mean(dyg * n, axis=-1, keepdims=True))
        dx_ref[...] = dx
        dxb_ref[...] = dx.astype(BF)

        @pl.when(pl.program_id(0) == 0)
        def _():
            dg_ref[...] = jnp.zeros_like(dg_ref)
            loss_ref[...] = jnp.zeros_like(loss_ref)

        dg_ref[...] += jnp.sum(dy * n, axis=0, keepdims=True)
        per_row = jnp.mean(e * e, axis=-1, keepdims=True)
        loss_ref[...] += 0.5 * jnp.sum(per_row, axis=0, keepdims=True)

    row = pl.BlockSpec((tm, D), lambda i: (i, 0))
    vec = pl.BlockSpec((1, D), lambda i: (0, 0))
    return ORDER.call(
        body, [x2, target, g], [row, row, vec], name="loss_head", grid=(S // tm,),
        out_specs=[pl.BlockSpec((1, 1), lambda i: (0, 0)), row, row, vec],
        out_shape=[_sds((1, 1), F32), _sds((S, D), F32), _sds((S, D), BF), _sds((1, D), F32)],
        compiler_params=_cparams(("arbitrary",)), chain_output=1,
    )


def _chains(L):
    side = min(8, L // QB)
    return side, min(4, max(1, 8 // side))


def _band_scores(qkv_ref, i, L, coef, head):
    KB = _key_rows(L)
    lanes = pl.ds(head * HEAD_DIM, HEAD_DIM)
    q0 = pl.multiple_of(i * QB, QB)
    ks = pl.multiple_of(jnp.clip(i * QB - HALF_WINDOW, 0, L - KB), HALF_WINDOW)
    q = qkv_ref[0, pl.ds(q0, QB), lanes]
    k = qkv_ref[1, pl.ds(ks, KB), lanes]
    v = qkv_ref[2, pl.ds(ks, KB), lanes]
    s = lax.dot_general(q, k, (NT, ((), ())), preferred_element_type=F32) * SCALE
    qpos = q0 + lax.broadcasted_iota(jnp.int32, (QB, KB), 0)
    kpos = ks + lax.broadcasted_iota(jnp.int32, (QB, KB), 1)
    rel = jnp.abs(kpos - qpos)
    valid = rel <= HALF_WINDOW
    s = jnp.where(valid, s - coef * rel.astype(F32), NEG)
    return q0, ks, q, k, v, s, valid


def _alibi_coefs(group, d, heads):
    first = 4 * group + 1 + pl.program_id(1) * heads
    scale = jnp.full((1, 1), -(8.0 / N_HEADS_A) * math.log(2.0), F32)
    return [jnp.exp(scale * (first + hh).astype(F32)) * float(d) for hh in range(heads)]


def _dilated_view(qkv3, group, d, heads):
    per = 4 // heads
    L = qkv3.shape[1]
    if d == 1:
        return qkv3, pl.BlockSpec((3, L, heads * HEAD_DIM), lambda r, j: (0, 0, per * group + j))
    return qkv3, pl.BlockSpec((3, L, heads * HEAD_DIM), lambda r, j: (0, 0, r * per + j))


def _qkv_views(name, qkv3, views=None):
    _, S, _ = qkv3.shape
    W = 512
    tm = _tile(S, 2 * ROW_TILE)
    dilated = [(g, d) for g, d in enumerate(DILATIONS) if d > 1]
    first = dilated[0][0]
    assert [g for g, _ in dilated] == list(range(first, first + len(dilated)))
    nc = W // 128
    to_views = views is None

    def body(*refs):
        scr = refs[-nc:]
        if to_views:
            src, outs = refs[0], refs[1:1 + len(dilated)]
        else:
            ins, dst = refs[:len(dilated)], refs[len(dilated) + 1]
        for k, (_, d) in enumerate(dilated):
            @pl.when(pl.program_id(1) == k)
            def _():
                for w in range(3):
                    for c in range(nc):
                        if to_views:
                            scr[c][...] = src[w, :, c * 128:(c + 1) * 128].astype(F32)
                    for r in range(d):
                        for c in range(nc):
                            at = r * W + c * 128
                            if to_views:
                                outs[k][w, :, at:at + 128] = scr[c][pl.ds(r, tm // d, stride=d), :].astype(BF)
                            else:
                                scr[c][pl.ds(r, tm // d, stride=d), :] = ins[k][w, :, at:at + 128].astype(F32)
                    for c in range(nc):
                        if not to_views:
                            dst[w, :, c * 128:(c + 1) * 128] = scr[c][...].astype(BF)

    cols = pl.BlockSpec((3, tm, W), lambda i, k: (0, i, first + k))
    rows = [pl.BlockSpec((3, tm // d, d * W), lambda i, k: (0, i, 0)) for _, d in dilated]
    shapes = [_sds((3, S // d, d * W), BF) for _, d in dilated]
    common = dict(name=name, grid=(S // tm, len(dilated)), scratch_shapes=[pltpu.VMEM((tm, 128), F32)] * nc,
                  compiler_params=_cparams(("parallel", "arbitrary")))
    if to_views:
        outs = ORDER.call(body, [qkv3], [cols], out_specs=rows, out_shape=shapes, **common)
        return {d: o for (_, d), o in zip(dilated, outs)}
    return ORDER.call(body, [views[d] for _, d in dilated] + [qkv3], rows + [pl.BlockSpec(memory_space=pl.ANY)],
                      out_specs=cols, out_shape=_sds(qkv3.shape, BF), input_output_aliases={len(dilated): 0}, **common)


def _attn_a_fwd(qkv3, group, d):
    L = qkv3.shape[1]
    S = L * d
    assert L % QB == 0
    side, heads = _chains(L)
    view, blocks_spec = _dilated_view(qkv3, group, d, heads)

    def body(qkv_ref, o_ref, lse_ref):
        coefs = _alibi_coefs(group, d, heads)

        def step(i, carry):
            chains = [(hh, _band_scores(qkv_ref, side * i + u, L, coefs[hh], hh))
                      for u in range(side) for hh in range(heads)]
            soft = []
            for hh, (q0, _, _, _, v, s, _) in chains:
                m = jnp.max(s, axis=-1, keepdims=True)
                p = jnp.exp(s - m)
                den = jnp.sum(p, axis=-1, keepdims=True)
                soft.append((hh, q0, (p / den).astype(BF), v, m + jnp.log(den)))
            for hh, q0, pn, v, lse in soft:
                lanes = pl.ds(hh * HEAD_DIM, HEAD_DIM)
                o_ref[pl.ds(q0, QB), lanes] = jnp.dot(pn, v, preferred_element_type=F32)
                lse_ref[pl.ds(q0, QB), lanes] = jnp.broadcast_to(lse, (QB, HEAD_DIM))
            return carry

        lax.fori_loop(0, L // QB // side, step, 0)

    per = 4 // heads
    out = pl.BlockSpec((L, heads * HEAD_DIM), lambda r, j: (0, r * per + j))
    o, lse = ORDER.call(
        body, [view], [blocks_spec],
        name=f"attn_a_fwd_d{d}", grid=(d, per),
        out_specs=[out, out],
        out_shape=[_sds((L, d * 512), F32), _sds((L, d * 512), F32)],
        compiler_params=_cparams(("parallel", "parallel")),
    )
    return o, lse


def _dilated_rows(name, arrays):
    S, W = arrays[0].shape
    tm = _tile(S, ROW_TILE)
    ds_ = [d for d in DILATIONS if d > 1]
    n = len(arrays)

    def body(*refs):
        nc = W // 128
        ins, outs, scr = refs[:n], refs[n:-nc], refs[-nc:]
        for a, src in enumerate(ins):
            for c in range(nc):
                scr[c][...] = src[:, c * 128:(c + 1) * 128].astype(F32)
            for k, d in enumerate(ds_):
                dst = outs[a * len(ds_) + k]
                for r in range(d):
                    for c in range(nc):
                        at = r * W + c * 128
                        dst[:, at:at + 128] = scr[c][pl.ds(r, tm // d, stride=d), :].astype(dst.dtype)

    row = pl.BlockSpec((tm, W), lambda i: (i, 0))
    out_specs, out_shape = [], []
    for a in arrays:
        for d in ds_:
            out_specs.append(pl.BlockSpec((tm // d, d * W), lambda i: (i, 0)))
            out_shape.append(_sds((S // d, d * W), a.dtype))
    outs = ORDER.call(body, list(arrays), [row] * n, name=name, grid=(S // tm,), out_specs=out_specs,
                      out_shape=out_shape, scratch_shapes=[pltpu.VMEM((tm, 128), F32)] * (W // 128),
                      compiler_params=_cparams(("parallel",)))
    return [{d: outs[a * len(ds_) + k] for k, d in enumerate(ds_)} for a in range(n)]


def _attn_a_combine(os_, lses):
    W = 512
    S = os_[0].shape[0] * DILATIONS[0]
    tm = _tile(S, ROW_TILE)
    nc = W // 128
    dilated = [g for g, d in enumerate(DILATIONS) if d > 1]

    def body(o0, o1, o2, l0, l1, l2, y_ref, lj_ref, *scr):
        def token_order(src, g, slot):
            d = DILATIONS[g]
            if d == 1:
                return src[...]
            bufs = scr[slot * nc:(slot + 1) * nc]
            for r in range(d):
                for c in range(nc):
                    at = r * W + c * 128
                    bufs[c][pl.ds(r, tm // d, stride=d), :] = src[:, at:at + 128]
            return jnp.concatenate([buf[...] for buf in bufs], axis=1)

        slots = {g: k for k, g in enumerate(dilated)}
        ls = [token_order(l, g, slots.get(g, 0)) for g, l in enumerate((l0, l1, l2))]
        os_tok = [token_order(o, g, len(dilated) + slots.get(g, 0)) for g, o in enumerate((o0, o1, o2))]
        m = jnp.maximum(jnp.maximum(ls[0], ls[1]), ls[2])
        es = [jnp.exp(l - m) for l in ls]
        den = es[0] + es[1] + es[2]
        y = (es[0] / den) * os_tok[0] + (es[1] / den) * os_tok[1] + (es[2] / den) * os_tok[2]
        y_ref[...] = y.astype(BF)
        lj_ref[...] = m + jnp.log(den)

    row = pl.BlockSpec((tm, W), lambda i: (i, 0))
    views = [pl.BlockSpec((tm // d, d * W), lambda i: (i, 0)) for d in DILATIONS]
    return ORDER.call(
        body, [*os_, *lses], views + views, name="attn_a_combine", grid=(S // tm,), out_specs=[row, row],
        out_shape=[_sds((S, W), BF), _sds((S, W), F32)],
        scratch_shapes=[pltpu.VMEM((tm, 128), F32)] * (2 * len(dilated) * nc),
        compiler_params=_cparams(("parallel",)),
    )


def _attn_a_bwd(qkv3, dy, y, lj, dqkv3, group, d):
    L = qkv3.shape[1]
    S = L * d
    side, heads = _chains(L)
    view, blocks_spec = _dilated_view(qkv3, group, d, heads)

    def body(qkv_ref, dy_ref, y_ref, lj_ref, *rest):
        out_ref, dk_acc, dv_acc = rest[-3:]
        coefs = _alibi_coefs(group, d, heads)
        dk_acc[...] = jnp.zeros_like(dk_acc)
        dv_acc[...] = jnp.zeros_like(dv_acc)

        def step(i, carry):
            chains = [(pl.ds(hh * HEAD_DIM, HEAD_DIM), _band_scores(qkv_ref, side * i + u, L, coefs[hh], hh))
                      for u in range(side) for hh in range(heads)]
            dys = [dy_ref[pl.ds(c[0], QB), lanes] for lanes, c in chains]
            dps = [lax.dot_general(dyv, c[4], (NT, ((), ())), preferred_element_type=F32)
                   for dyv, (_, c) in zip(dys, chains)]
            grads = []
            for (lanes, (q0, ks, q, k, v, s, valid)), dyv, dp in zip(chains, dys, dps):
                rows = pl.ds(q0, QB)
                delta = jnp.sum(dyv.astype(F32) * y_ref[rows, lanes].astype(F32), axis=-1, keepdims=True)
                p = jnp.where(valid, jnp.exp(s - jnp.tile(lj_ref[rows, lanes], (1, _key_rows(L) // HEAD_DIM))), 0.0)
                grads.append(((p * (dp - delta)).astype(BF), p.astype(BF)))
            for (lanes, (q0, ks, q, k, v, s, valid)), dyv, (ds, pb) in zip(chains, dys, grads):
                out_ref[0, pl.ds(q0, QB), lanes] = (jnp.dot(ds, k, preferred_element_type=F32) * SCALE).astype(BF)
                keys = pl.ds(ks, _key_rows(L))
                dk_acc[keys, lanes] += lax.dot_general(ds, q, (TN, ((), ())), preferred_element_type=F32) * SCALE
                dv_acc[keys, lanes] += lax.dot_general(pb, dyv, (TN, ((), ())), preferred_element_type=F32)
            return carry

        lax.fori_loop(0, L // QB // side, step, 0)
        out_ref[1] = dk_acc[...].astype(BF)
        out_ref[2] = dv_acc[...].astype(BF)

    per = 4 // heads
    width = heads * HEAD_DIM
    row = pl.BlockSpec((L, width), lambda r, j: (0, r * per + j))
    operands = [view, dy, y, lj]
    scratch = [pltpu.VMEM((L, width), F32), pltpu.VMEM((L, width), F32)]
    if d == 1:
        return ORDER.call(
            body, operands + [dqkv3], [blocks_spec, row, row, row, pl.BlockSpec(memory_space=pl.ANY)],
            name=f"attn_a_bwd_d{d}", grid=(d, per), out_specs=blocks_spec, out_shape=_sds((3, S, QKV_W), BF),
            scratch_shapes=scratch, input_output_aliases={4: 0}, compiler_params=_cparams(("parallel", "parallel")))
    return ORDER.call(
        body, operands, [blocks_spec, row, row, row], name=f"attn_a_bwd_d{d}", grid=(d, per),
        out_specs=blocks_spec, out_shape=_sds((3, L, d * 512), BF),
        scratch_shapes=scratch, compiler_params=_cparams(("parallel", "parallel")))


def _toeplitz_onehot():
    oh = np.zeros((64, GRID_W, 128), np.float32)
    for qc in range(GRID_W):
        for m in range(128):
            kc = m % GRID_W
            dc = int(np.clip(kc - qc, -(NA_COLS - 1), NA_COLS - 1)) + NA_COLS - 1
            oh[(m // GRID_W) * 32 + dc, qc, m] = 1.0
    return oh.reshape(64, GRID_W * 128)


def _nbr_scores(qkv_ref, e2_ref, r, rows, ok):
    rs = jnp.clip(r - NA_ROWS // 2, 0, rows - NA_ROWS)
    q0 = pl.multiple_of(r * GRID_W, GRID_W)
    k0 = pl.multiple_of(rs * GRID_W, GRID_W)
    q = qkv_ref[0, pl.ds(q0, GRID_W), :]
    k = qkv_ref[1, pl.ds(k0, NA_ROWS * GRID_W), :]
    v = qkv_ref[2, pl.ds(k0, NA_ROWS * GRID_W), :]
    s = lax.dot_general(q, k, (NT, ((), ())), preferred_element_type=F32) * SCALE
    first = rs - r + NA_ROWS - 1
    bias = jnp.concatenate([e2_ref[first + 2 * pair] for pair in range(NA_ROWS // 2)], axis=1)
    s = jnp.where(ok, s + bias, NEG)
    return q0, k0, first, q, k, v, s


def _nbr_col_ok():
    qc = lax.broadcasted_iota(jnp.int32, (GRID_W, NA_ROWS * GRID_W), 0)
    kc = lax.broadcasted_iota(jnp.int32, (GRID_W, NA_ROWS * GRID_W), 1) % GRID_W
    cs = jnp.clip(qc - NA_COLS // 2, 0, GRID_W - NA_COLS)
    return (kc >= cs) & (kc < cs + NA_COLS)


def _attn_b_fwd(qkv3, e2):
    _, S, _ = qkv3.shape
    rows = S // GRID_W
    assert rows >= NA_ROWS

    def body(qkv_ref, e2_ref, o_ref, lse_ref):
        ok = _nbr_col_ok()

        def step(i, carry):
            blocks = [_nbr_scores(qkv_ref, e2_ref, NBR_SIDE * i + u, rows, ok) for u in range(NBR_SIDE)]
            soft = []
            for q0, _, _, _, _, v, s in blocks:
                m = jnp.max(s, axis=-1, keepdims=True)
                p = jnp.exp(s - m)
                den = jnp.sum(p, axis=-1, keepdims=True)
                soft.append((q0, (p / den).astype(BF), v, m + jnp.log(den)))
            for q0, pn, v, lse in soft:
                o_ref[pl.ds(q0, GRID_W), :] = jnp.dot(pn, v, preferred_element_type=F32).astype(BF)
                lse_ref[pl.ds(q0, GRID_W), :] = jnp.broadcast_to(lse, (GRID_W, HEAD_DIM))
            return carry

        lax.fori_loop(0, rows // NBR_SIDE, step, 0)

    out = pl.BlockSpec((S, HEAD_DIM), lambda h: (0, h))
    return ORDER.call(
        body, [qkv3, e2],
        [pl.BlockSpec((3, S, HEAD_DIM), lambda h: (0, 0, N_HEADS_A + h)),
         pl.BlockSpec((None, RPB_ROWS - 1, GRID_W, 128), lambda h: (h, 0, 0, 0))],
        name="attn_b_fwd", grid=(4,),
        out_specs=[out, out], out_shape=[_sds((S, 512), BF), _sds((S, 512), F32)],
        compiler_params=_cparams(("parallel",)),
    )


def _attn_b_bwd(qkv3, e2, dy, y, lse, dqkv3):
    _, S, _ = qkv3.shape
    rows = S // GRID_W
    nk = NA_ROWS * GRID_W

    def body(qkv_ref, e2_ref, dy_ref, y_ref, lse_ref, _, out_ref, de2_ref, dk_acc, dv_acc):
        ok = _nbr_col_ok()
        dk_acc[...] = jnp.zeros_like(dk_acc)
        dv_acc[...] = jnp.zeros_like(dv_acc)
        de2_ref[...] = jnp.zeros_like(de2_ref)

        def step(i, carry):
            blocks = [_nbr_scores(qkv_ref, e2_ref, NBR_SIDE * i + u, rows, ok) for u in range(NBR_SIDE)]
            dys = [dy_ref[pl.ds(b[0], GRID_W), :] for b in blocks]
            dps = [lax.dot_general(dyv, b[5], (NT, ((), ())), preferred_element_type=F32) for dyv, b in zip(dys, blocks)]
            grads = []
            for (q0, k0, first, q, k, v, s), dyv, dp in zip(blocks, dys, dps):
                qrows = pl.ds(q0, GRID_W)
                delta = jnp.sum(dyv.astype(F32) * y_ref[qrows, :].astype(F32), axis=-1, keepdims=True)
                p = jnp.where(ok, jnp.exp(s - jnp.tile(lse_ref[qrows, :], (1, nk // HEAD_DIM))), 0.0)
                ds = p * (dp - delta)
                for pair in range(NA_ROWS // 2):
                    de2_ref[first + 2 * pair] += ds[:, pair * 128:(pair + 1) * 128]
                grads.append((ds.astype(BF), p.astype(BF)))
            for (q0, k0, first, q, k, v, s), dyv, (dsb, pb) in zip(blocks, dys, grads):
                out_ref[0, pl.ds(q0, GRID_W), :] = (jnp.dot(dsb, k, preferred_element_type=F32) * SCALE).astype(BF)
                keys = pl.ds(k0, nk)
                dk_acc[keys, :] += lax.dot_general(dsb, q, (TN, ((), ())), preferred_element_type=F32) * SCALE
                dv_acc[keys, :] += lax.dot_general(pb, dyv, (TN, ((), ())), preferred_element_type=F32)
            return carry

        lax.fori_loop(0, rows // NBR_SIDE, step, 0)
        out_ref[1] = dk_acc[...].astype(BF)
        out_ref[2] = dv_acc[...].astype(BF)

    heads = pl.BlockSpec((3, S, HEAD_DIM), lambda h: (0, 0, N_HEADS_A + h))
    row = pl.BlockSpec((S, HEAD_DIM), lambda h: (0, h))
    table = pl.BlockSpec((None, RPB_ROWS - 1, GRID_W, 128), lambda h: (h, 0, 0, 0))
    return ORDER.call(
        body, [qkv3, e2, dy, y, lse, dqkv3],
        [heads, table, row, row, row, pl.BlockSpec(memory_space=pl.ANY)], name="attn_b_bwd", grid=(4,),
        out_specs=[heads, table],
        out_shape=[_sds((3, S, QKV_W), BF), _sds((4, RPB_ROWS - 1, GRID_W, 128), F32)],
        scratch_shapes=[pltpu.VMEM((S, HEAD_DIM), F32), pltpu.VMEM((S, HEAD_DIM), F32)],
        input_output_aliases={5: 0},
        compiler_params=_cparams(("parallel",)), chain_output=1,
    )


def _rpb_to_table(rpb):
    pad = jnp.pad(rpb, ((0, 0), (0, 0), (0, 1)))
    pairs = jnp.concatenate([pad[:, :-1], pad[:, 1:]], axis=-1).reshape(4 * (RPB_ROWS - 1), 64)
    onehot = jnp.asarray(_toeplitz_onehot())
    n = onehot.shape[1]
    tn = 2048
    full = lambda i, j, k: (0, 0)
    (e2,) = _matmul("rpb_table", pairs, onehot, pl.BlockSpec(pairs.shape, full),
                    pl.BlockSpec((64, tn), lambda i, j, k: (0, j)), NN, (1, n // tn, 1), (pairs.shape[0], tn), [],
                    [(_sds((pairs.shape[0], n), F32), pl.BlockSpec((pairs.shape[0], tn), lambda i, j, k: (0, j)))],
                    _store(F32), precision=lax.Precision.HIGHEST)
    return e2.reshape(4, RPB_ROWS - 1, GRID_W, 128)


def _table_grad_to_rpb(de2):
    onehot = jnp.asarray(_toeplitz_onehot())
    n = onehot.shape[1]
    flat = de2.reshape(4 * (RPB_ROWS - 1), n)
    tk = 2048
    (dpairs,) = _matmul("rpb_table_grad", flat, onehot, pl.BlockSpec((flat.shape[0], tk), lambda i, j, k: (0, k)),
                        pl.BlockSpec((64, tk), lambda i, j, k: (0, k)), NT, (1, 1, n // tk), (flat.shape[0], 64), [],
                        [(_sds((flat.shape[0], 64), F32), pl.BlockSpec((flat.shape[0], 64), lambda i, j, k: (0, 0)))],
                        _store(F32), precision=lax.Precision.HIGHEST)
    dpairs = dpairs.reshape(4, RPB_ROWS - 1, 64)
    zero = jnp.zeros((4, 1, RPB_COLS), F32)
    return (jnp.concatenate([dpairs[:, :, :RPB_COLS], zero], axis=1)
            + jnp.concatenate([zero, dpairs[:, :, 32:32 + RPB_COLS]], axis=1))


HBM = pl.BlockSpec(memory_space=pl.ANY)


def _place():
    x, y, c = lax.axis_index("x"), lax.axis_index("y"), lax.axis_index("c")
    chips = [(1 - x, y), (x, 1 - y), (1 - x, 1 - y)]
    return x, y, c, chips


def _remote(src, dst, send_sem, recv_sem, to):
    return pltpu.make_async_remote_copy(src_ref=src, dst_ref=dst, send_sem=send_sem, recv_sem=recv_sem,
                                        device_id=to, device_id_type=MESH)


def _place_shard(name, w, me, plain=False):
    R, C = w.shape
    tr = _tile(R, 256)

    def body(me_ref, w_ref, *o_refs):
        for o_ref in o_refs:
            o_ref[...] = w_ref[...].astype(BF)

    row = pl.BlockSpec((tr, C), lambda i, mr: (i, 0))
    placed = pl.BlockSpec((None, tr, C), lambda i, mr: (mr[0], i, 0))
    return ORDER.call(
        body, [w], [row], prefetch=(me,), name=name, grid=(R // tr,),
        out_specs=[placed, row] if plain else [placed],
        out_shape=[_sds((N_CHIPS, R, C), BF)] + ([_sds((R, C), BF)] if plain else []),
        compiler_params=_cparams(("parallel",)),
    )


SEM = pl.BlockSpec(memory_space=pltpu.SEMAPHORE)
IN_HBM = pl.BlockSpec(memory_space=pltpu.HBM)
DATAFLOW = pltpu.SideEffectType.DATAFLOW_SIDE_EFFECTING


def _in_hbm(a):
    return pltpu.with_memory_space_constraint(a, pltpu.HBM)


def _copy_start(name, bufs, copies, n_copies, earlier=None):
    n = len(bufs)
    after = None if any(b is ORDER.last for b in bufs) else ORDER.last
    n_extra = (2 if earlier is not None else 0) + (1 if after is not None else 0)

    def body(*refs):
        ins = refs[:n]
        if earlier is not None:
            for k, (src, dst, to) in enumerate(earlier[0](ins)):
                cp = _remote(src, dst, refs[n].at[k], refs[n + 1].at[k], to)
                cp.wait_send()
                cp.wait_recv()
        send_sems, recv_sems = refs[n + n_extra], refs[n + n_extra + 1]
        for k, (src, dst, to) in enumerate(copies(ins)):
            _remote(src, dst, send_sems.at[k], recv_sems.at[k], to).start()
        refs[-1][...] = jnp.zeros((8, 128), F32)

    operands = [_in_hbm(b) for b in bufs]
    in_specs = [IN_HBM] * n
    if earlier is not None:
        operands += [earlier[1], earlier[2]]
        in_specs += [SEM, SEM]
    if after is not None:
        operands.append(after)
        in_specs.append(HBM)
    outs = pl.pallas_call(
        body, name=name,
        out_shape=(pltpu.SemaphoreType.DMA((n_copies,)), pltpu.SemaphoreType.DMA((n_copies,)),
                   *[pltpu.HBM(b.shape, b.dtype) for b in bufs], _sds((8, 128), F32)),
        in_specs=in_specs,
        out_specs=(SEM, SEM, *[IN_HBM] * n, pl.BlockSpec(memory_space=pltpu.VMEM)),
        input_output_aliases={i: 2 + i for i in range(n)},
        compiler_params=pltpu.CompilerParams(has_side_effects=DATAFLOW),
    )(*operands)
    ORDER.last = outs[-1]
    return outs[0], outs[1], list(outs[2:2 + n])


def _copy_wait(name, bufs, copies, send_sems, recv_sems):
    n = len(bufs)
    after = ORDER.last

    def body(*refs):
        ins = refs[:n]
        for k, (src, dst, to) in enumerate(copies(ins)):
            cp = _remote(src, dst, refs[n].at[k], refs[n + 1].at[k], to)
            cp.wait_send()
            cp.wait_recv()

    outs = list(pl.pallas_call(
        body, name=name,
        out_shape=tuple(pltpu.HBM(b.shape, b.dtype) for b in bufs),
        in_specs=[IN_HBM] * n + [SEM, SEM, HBM], out_specs=tuple([IN_HBM] * n),
        input_output_aliases={i: i for i in range(n)},
        compiler_params=pltpu.CompilerParams(has_side_effects=DATAFLOW),
    )(*bufs, send_sems, recv_sems, after))
    ORDER.last = outs[0]
    return outs


def _gather_hop1(bufs):
    x, y, c, chips = _place()
    out = []
    for b in bufs:
        half = b.shape[1] // 2
        mine = b.at[2 * x + y, pl.ds(c * half, half), :]
        out += [(mine, mine, (*chip, c)) for chip in chips]
    return out


def _gather_hop2(bufs):
    x, y, c, chips = _place()
    out = []
    for b in bufs:
        half = b.shape[1] // 2
        for chip in chips:
            landed = b.at[2 * chip[0] + chip[1], pl.ds(c * half, half), :]
            out.append((landed, landed, (x, y, 1 - c)))
    return out


def _swap_copies(bufs):
    x, y, c, _ = _place()
    n = len(bufs) // 2
    out = []
    for p, land in zip(bufs[:n], bufs[n:]):
        half = p.shape[1] // 2
        out.append((p.at[:, pl.ds((1 - c) * half, half), :], land, (x, y, 1 - c)))
    return out


def _scatter_copies(bufs):
    _, _, c, chips = _place()
    n = len(bufs) // 2
    out = []
    for s_, land in zip(bufs[:n], bufs[n:]):
        out += [(s_.at[2 * chip[0] + chip[1]], land.at[j], (*chip, c)) for j, chip in enumerate(chips)]
    return out


def _join_copies(bufs):
    x, y, c, _ = _place()
    out = []
    for b in bufs:
        half = b.shape[0] // 2
        mine = b.at[pl.ds(c * half, half), :]
        out.append((mine, mine, (x, y, 1 - c)))
    return out


def _gather_small(vec):
    m_per, n = vec.shape

    def body(x_ref, out_ref, send_sems, recv_sems, local_sem):
        x, y, c, chips = _place()
        me, sibling = (x, y, c), (x, y, 1 - c)

        def rows(px, py, pc):
            return out_ref.at[pl.ds((4 * px + 2 * py + pc) * m_per, m_per), :]

        def copy(k, block, to, src=None):
            return _remote(rows(*block) if src is None else src, rows(*block), send_sems.at[k], recv_sems.at[k], to)

        mine = pltpu.make_async_copy(x_ref, rows(*me), local_sem)
        mine.start()
        first = [copy(0, me, sibling, src=x_ref)]
        first += [copy(1 + j, me, (*chip, c), src=x_ref) for j, chip in enumerate(chips)]
        for cp in first:
            cp.start()
        passed = [copy(4 + j, (*chip, c), sibling) for j, chip in enumerate(chips)]
        for j, chip in enumerate(chips):
            copy(1 + j, (*chip, c), me).wait_recv()
            passed[j].start()
        copy(0, sibling, me).wait_recv()
        for j, chip in enumerate(chips):
            copy(4 + j, (*chip, 1 - c), me).wait_recv()
        for cp in first + passed:
            cp.wait_send()
        mine.wait()

    return ORDER.call(
        body, [vec], [pl.BlockSpec(memory_space=pltpu.VMEM)], name="gather_small_grads",
        out_shape=_sds((8 * m_per, n), vec.dtype), out_specs=pl.BlockSpec(memory_space=pltpu.VMEM),
        scratch_shapes=[pltpu.SemaphoreType.DMA((7,)), pltpu.SemaphoreType.DMA((7,)), pltpu.SemaphoreType.DMA],
    )


def _add_sibling(name, partial, received, c):
    _, R, C = partial.shape
    half = R // 2
    tr = _tile(half, 256)
    nb = half // tr

    def body(c_ref, p_ref, r_ref, o_ref):
        o_ref[...] = (p_ref[...].astype(F32) + r_ref[...].astype(F32)).astype(BF)

    return ORDER.call(
        body, [partial, received],
        [pl.BlockSpec((None, tr, C), lambda j, i, cr: (j, cr[0] * nb + i, 0)),
         pl.BlockSpec((None, tr, C), lambda j, i, cr: (j, i, 0))],
        prefetch=(c,), name=name, grid=(N_CHIPS, nb),
        out_specs=pl.BlockSpec((None, tr, C), lambda j, i, cr: (j, i, 0)),
        out_shape=_sds((N_CHIPS, half, C), BF), compiler_params=_cparams(("parallel", "parallel")),
    )


def _add_chips(name, sums, received, me_c):
    _, half, C = sums.shape
    tr = _tile(half, 256)
    nb = half // tr

    def body(mc_ref, s_ref, r_ref, o_ref):
        acc = s_ref[...].astype(F32)
        for j in range(3):
            acc = acc + r_ref[j].astype(F32)
        o_ref[...] = acc

    return ORDER.call(
        body, [sums, received],
        [pl.BlockSpec((None, tr, C), lambda i, mc: (mc[0], i, 0)),
         pl.BlockSpec((3, tr, C), lambda i, mc: (0, i, 0))],
        prefetch=(me_c,), name=name, grid=(nb,),
        out_specs=pl.BlockSpec((tr, C), lambda i, mc: (mc[1] * nb + i, 0)),
        out_shape=_sds((2 * half, C), F32), compiler_params=_cparams(("parallel",)),
    )


def _adamw_math(w, g, m, v):
    m = ADAM_B1 * m + (1.0 - ADAM_B1) * g
    v = ADAM_B2 * v + (1.0 - ADAM_B2) * (g * g)
    m_hat = m / (1.0 - ADAM_B1 ** ADAM_STEP)
    v_hat = v / (1.0 - ADAM_B2 ** ADAM_STEP)
    delta = -ADAM_LR * (m_hat / (jnp.sqrt(v_hat) + ADAM_EPS) + ADAM_WD * w)
    return delta, m, v


def _adamw(name, w, g, m, v):
    R, C = w.shape
    tr = _tile(R, 256)

    def body(w_ref, g_ref, m_ref, v_ref, go_ref, d_ref, mo_ref, vo_ref):
        gv = g_ref[...]
        go_ref[...] = gv
        d_ref[...], mo_ref[...], vo_ref[...] = _adamw_math(w_ref[...], gv, m_ref[...], v_ref[...])

    row = pl.BlockSpec((tr, C), lambda i: (i, 0))
    return ORDER.call(
        body, [w, g, m, v], [row] * 4, name=name, grid=(R // tr,), out_specs=[row] * 4,
        out_shape=[_sds((R, C), F32)] * 4, compiler_params=_cparams(("parallel",)), chain_output=1,
    )


def _adamw_small(gathered, w, m, v):
    rows, n = w.shape

    def body(ga_ref, w_ref, m_ref, v_ref, go_ref, d_ref, mo_ref, vo_ref):
        g = ga_ref[pl.ds(0, rows), :]
        for dev in range(1, 8):
            g = g + ga_ref[pl.ds(dev * rows, rows), :]
        go_ref[...] = g
        d_ref[...], mo_ref[...], vo_ref[...] = _adamw_math(w_ref[...], g, m_ref[...], v_ref[...])

    whole = pl.BlockSpec(memory_space=pltpu.VMEM)
    return ORDER.call(
        body, [gathered, w, m, v], [whole] * 4, name="adamw_small", out_specs=[whole] * 4,
        out_shape=[_sds((rows, n), F32)] * 4, compiler_params=_cparams(), chain_output=1,
    )


def _proj_merge(y_a, y_b, gpa, gpb, g3):
    S, K = y_a.shape
    _, _, Nq = gpa.shape
    D = N_CHIPS * Nq
    tm, tn = _tile(S, 1024), _tile(Nq, 512)
    q = Nq // tn

    def body(ya_ref, yb_ref, wa_ref, wb_ref, g_ref, merged_ref, c_ref):
        pa = jnp.dot(ya_ref[...], wa_ref[...], preferred_element_type=F32)
        pb = jnp.dot(yb_ref[...], wb_ref[...], preferred_element_type=F32)
        g = g_ref[...].astype(F32)
        merged_ref[...] = (g[0] * pa + g[1] * pb).astype(BF)
        c_ref[0] = (pa * g[0] * (1.0 - g[0])).astype(BF)
        c_ref[1] = (pb * g[1] * (1.0 - g[1])).astype(BF)

    rows = pl.BlockSpec((tm, K), lambda i, j: (i, 0))
    weight = pl.BlockSpec((None, K, tn), lambda i, j: (j // q, 0, j % q))
    pair = pl.BlockSpec((2, tm, tn), lambda i, j: (0, i, j))
    return ORDER.call(
        body, [y_a, y_b, gpa, gpb, g3], [rows, rows, weight, weight, pair], name="proj_merge",
        grid=(S // tm, N_CHIPS * q), out_specs=[pl.BlockSpec((tm, tn), lambda i, j: (i, j)), pair],
        out_shape=[_sds((S, D), BF), _sds((2, S, D), BF)], compiler_params=_cparams(("parallel", "parallel")))


def _out_proj_dx(dx1b, wout, g3, c3, gpa, gpb):
    S, D = dx1b.shape
    _, K, Nq = gpa.shape
    tm, tn = _tile(S, 1024), Nq
    nj = D // tn

    def body(a_ref, w_ref, g_ref, c_ref, wa_ref, wb_ref, dpa_ref, dpb_ref, dg_ref, db_ref, dya_ref, dyb_ref,
             acc_a, acc_b):
        j = pl.program_id(1)
        dm = lax.dot_general(a_ref[...], w_ref[...], (NT, ((), ())), preferred_element_type=F32)
        g, c = g_ref[...].astype(F32), c_ref[...].astype(F32)
        dpa, dpb = (dm * g[0]).astype(BF), (dm * g[1]).astype(BF)
        dpa_ref[...] = dpa
        dpb_ref[...] = dpb
        dga, dgb = dm * c[0], dm * c[1]
        dg_ref[0] = dga.astype(BF)
        dg_ref[1] = dgb.astype(BF)
        db_ref[...] = jnp.concatenate([jnp.sum(dga, axis=0, keepdims=True), jnp.sum(dgb, axis=0, keepdims=True)], 0)
        ya = lax.dot_general(dpa, wa_ref[...], (NT, ((), ())), preferred_element_type=F32)
        yb = lax.dot_general(dpb, wb_ref[...], (NT, ((), ())), preferred_element_type=F32)

        @pl.when(j == 0)
        def _():
            acc_a[...] = ya
            acc_b[...] = yb

        @pl.when(j > 0)
        def _():
            acc_a[...] += ya
            acc_b[...] += yb

        @pl.when(j == nj - 1)
        def _():
            dya_ref[...] = acc_a[...].astype(BF)
            dyb_ref[...] = acc_b[...].astype(BF)

    tile = pl.BlockSpec((tm, tn), lambda i, j: (i, j))
    pair = pl.BlockSpec((2, tm, tn), lambda i, j: (0, i, j))
    shard = pl.BlockSpec((None, K, tn), lambda i, j: (j, 0, 0))
    rows = pl.BlockSpec((tm, K), lambda i, j: (i, 0))
    return ORDER.call(
        body, [dx1b, wout, g3, c3, gpa, gpb],
        [pl.BlockSpec((tm, D), lambda i, j: (i, 0)), pl.BlockSpec((tn, D), lambda i, j: (j, 0)), pair, pair, shard, shard],
        name="out_proj_dx", grid=(S // tm, nj),
        out_specs=[tile, tile, pair, pl.BlockSpec((None, 2, tn), lambda i, j: (i, 0, j)), rows, rows],
        out_shape=[_sds((S, D), BF), _sds((S, D), BF), _sds((2, S, D), BF), _sds((S // tm, 2, D), F32),
                   _sds((S, K), BF), _sds((S, K), BF)],
        scratch_shapes=[pltpu.VMEM((tm, K), F32), pltpu.VMEM((tm, K), F32)],
        compiler_params=_cparams(("parallel", "arbitrary")))


class _Exchange:
    GATHER = (("qkv",), ("gate",), ("proj_a", "proj_b", "out"), ("up",), ("down",))
    REDUCE = {"mlp": ("down", "up"), "mix": ("out", "proj_a", "proj_b"), "in": ("qkv", "gate")}

    OWN_FIRST = ("qkv", "gate")

    def __init__(self, shards, me, c, moments):
        self.me, self.c = me, c
        self.shards, self.moments = shards, moments
        self.hop1, self.hop2, self.stage, self.grads, self.own, self.updates = {}, {}, {}, {}, {}, {}
        for g, names in enumerate(self.GATHER):
            bufs = []
            for n in names:
                placed = _place_shard(f"place_{n}", shards[n], me, plain=n in self.OWN_FIRST)
                bufs.append(placed[0])
                if n in self.OWN_FIRST:
                    self.own[n] = placed[1]
            self.hop1[g] = _copy_start(f"gather{g}_start", bufs, _gather_hop1, 3 * len(names))

    def forward(self, g):
        send, recv, thru = self.hop1.pop(g)
        self.hop2[g] = _copy_start(f"gather{g}_forward", thru, _gather_hop2, len(thru) * 3,
                                   earlier=(_gather_hop1, send, recv))

    def weights(self, g):
        send, recv, thru = self.hop2.pop(g)
        return _copy_wait(f"gather{g}_wait", thru, _gather_hop2, send, recv)

    def adamw_beside(self, name):
        def update(w, g, m, v):
            return (g,) + _adamw_math(w, g, m, v)
        return update, [self.shards[name], self.grads[name], *self.moments[name]], 4

    def reduce(self, key, partials=None):
        names = self.REDUCE[key]
        n = len(names)
        if partials is not None:
            lands = [lax.empty((p.shape[0], p.shape[1] // 2, p.shape[2]), p.dtype) for p in partials]
            self.stage[key] = ("swap",) + _copy_start(f"reduce_{key}_swap", list(partials) + lands, _swap_copies, n)
            return
        kind, send, recv, thru = self.stage.pop(key)
        if kind == "swap":
            thru = _copy_wait(f"reduce_{key}_swap_wait", thru, _swap_copies, send, recv)
            sums = [_add_sibling(f"reduce_{nm}_add_sibling", p, r, self.c)
                    for nm, p, r in zip(names, thru[:n], thru[n:])]
            lands = [lax.empty((3,) + s_.shape[1:], s_.dtype) for s_ in sums]
            self.stage[key] = ("scatter",) + _copy_start(f"reduce_{key}_scatter", sums + lands, _scatter_copies, 3 * n)
        elif kind == "scatter":
            thru = _copy_wait(f"reduce_{key}_scatter_wait", thru, _scatter_copies, send, recv)
            me_c = jnp.concatenate([self.me, self.c])
            halves = [_add_chips(f"reduce_{nm}_add_chips", s_, r, me_c)
                      for nm, s_, r in zip(names, thru[:n], thru[n:])]
            self.stage[key] = ("join",) + _copy_start(f"reduce_{key}_join", halves, _join_copies, n)
        else:
            thru = _copy_wait(f"reduce_{key}_join_wait", thru, _join_copies, send, recv)
            self.grads.update(zip(names, thru))


def _forward_backward(x, target, norm_mix, b_gate, rpb, norm_mlp, norm_final, ex):
    S, D = x.shape

    h1 = _rms_fwd("rms_mix", x, norm_mix)
    nq = QKV_W // 512
    qkv_out = (((3, S, QKV_W), BF), lambda i, T: (T // nq, i, T % nq))
    tg = _tile(ex.own["gate"].shape[1], 1024)
    ng = D // tg
    gate_out = (((2, S, D), BF), lambda i, T: (T // ng, i, T % ng))

    def gate_epilogue(acc, ex_, outs):
        outs[0][...] = jax.nn.sigmoid(acc + ex_[0][...]).astype(BF)

    qkv3 = _mm_nn_shards("qkv_own", h1, ex.own["qkv"], ex.me, True, *qkv_out, _store(BF), tm=2048)
    g3 = _mm_nn_shards("gate_own", h1, ex.own["gate"], ex.me, True, *gate_out, gate_epilogue, extras=[b_gate], tn=tg)
    ex.forward(0)
    e2 = _rpb_to_table(rpb)
    (gq,) = ex.weights(0)
    qkv3 = _mm_nn_shards("qkv", h1, gq, ex.me, False, *qkv_out, _store(BF), into=qkv3, tm=2048)

    ex.forward(1)
    outs_a = [_attn_a_fwd(qkv3, 0, DILATIONS[0])]
    (gg,) = ex.weights(1)
    g3 = _mm_nn_shards("gate", h1, gg, ex.me, False, *gate_out, gate_epilogue, extras=[b_gate], into=g3, tn=tg)

    ex.forward(2)
    qkv_views = _qkv_views("qkv_views", qkv3)
    outs_a += [_attn_a_fwd(qkv_views[d], grp, d) for grp, d in enumerate(DILATIONS) if grp > 0]
    y_a, lj = _attn_a_combine([o for o, _ in outs_a], [l for _, l in outs_a])
    y_b, lse_b = _attn_b_fwd(qkv3, e2)
    gpa, gpb, gout = ex.weights(2)
    wout = gout.reshape(D, D)
    merged, c3 = _proj_merge(y_a, y_b, gpa, gpb, g3)

    def residual_epilogue(acc, ex_, outs):
        outs[0][...] = acc + ex_[0][...]

    def residual_norm_epilogue(acc, ex_, outs):
        x1v = acc + ex_[0][...]
        outs[0][...] = x1v
        r = lax.rsqrt(jnp.mean(x1v * x1v, axis=-1, keepdims=True) + EPS)
        outs[1][...] = ((x1v * r) * ex_[1][...]).astype(BF)

    def nn_plain(name, a, w, res, bm=1024, bn=1024, norm=None):
        M, K = a.shape
        N = w.shape[1]
        bm, bn, bk = _tile(M, bm), _tile(N, bn), _tile(K, 2048)
        t = pl.BlockSpec((bm, bn), lambda i, j, k: (i, j))
        extras, outs, epilogue = [(res, t)], [(_sds((M, N), F32), t)], residual_epilogue
        if norm is not None:
            assert bn == N
            extras.append((norm, pl.BlockSpec((1, N), lambda i, j, k: (0, 0))))
            outs.append((_sds((M, N), BF), t))
            epilogue = residual_norm_epilogue
        result = _matmul(name, a, w, pl.BlockSpec((bm, bk), lambda i, j, k: (i, k)),
                         pl.BlockSpec((bk, bn), lambda i, j, k: (k, j)), NN, (M // bm, N // bn, K // bk), (bm, bn),
                         extras, outs, epilogue)
        return result[0] if norm is None else result

    ex.forward(3)
    x1, h2 = nn_plain("out_proj", merged, wout, x, bm=512, bn=2048, norm=norm_mlp)
    (gup,) = ex.weights(3)
    F = gup.shape[2] * N_CHIPS

    def up_epilogue(acc, ex_, outs):
        ru = jnp.maximum(acc, 0.0)
        outs[0][...] = (ru * ru).astype(BF)
        outs[1][...] = ru.astype(BF)

    tu = _tile(gup.shape[2], 2048)
    ut = pl.BlockSpec((_tile(S, 1024), tu), lambda i, j, k: (i, j))
    (act, ru), _ = _mm_nn_cols("mlp_up", h2, gup, BF, epilogue=up_epilogue, tn=tu,
                               outs=[(_sds((S, F), BF), ut), (_sds((S, F), BF), ut)])
    ex.forward(4)
    (gdown,) = ex.weights(4)
    wdown = gdown.reshape(F, D)
    x2 = nn_plain("mlp_down", act, wdown, x1)

    loss, dx2, dx2b, d_norm_final = _loss_head(x2, target, norm_final.reshape(1, D))

    def nt_rows(name, a, w, epilogue, extras, outs, bn=1024):
        M, N = a.shape
        K = w.shape[0]
        bm, bn, bk = _tile(M, 1024), _tile(K, bn), _tile(N, 2048)
        return _matmul(name, a, w, pl.BlockSpec((bm, bk), lambda i, j, k: (i, k)),
                       pl.BlockSpec((bn, bk), lambda i, j, k: (j, k)), NT, (M // bm, K // bn, N // bk), (bm, bn),
                       extras(bm, bn), outs(bm, bn), epilogue)

    def nt_cols(name, a_spec_fn, a, g, M, epilogue, extras, outs, bk, bn=1024, side=None):
        _, K, Nq = g.shape
        bm, bn, bk = _tile(M, 1024), _tile(K, bn), _tile(Nq, bk)
        q = Nq // bk
        return _matmul(name, a, g, a_spec_fn(bm, bk), pl.BlockSpec((None, bn, bk), lambda i, j, k: (k // q, j, k % q)),
                       NT, (M // bm, K // bn, N_CHIPS * q), (bm, bn), extras(bm, bn), outs(bm, bn), epilogue,
                       side=side)

    def tn_grad(name, a, a_spec_fn, b, b_spec_fn, Kin, N, out_shape, out_spec_fn, bn=1024):
        bm, bn, bk = _tile(Kin, 1024), _tile(N, bn), _tile(S, 4096)
        return _matmul(name, a, b, a_spec_fn(bk, bm), b_spec_fn(bk, bn), TN, (Kin // bm, N // bn, S // bk), (bm, bn),
                       [], [(_sds(out_shape, BF), out_spec_fn(bm, bn))], _store(BF))[0]

    plain_a = lambda bk, bm: pl.BlockSpec((bk, bm), lambda i, j, k: (k, i))
    plain_b = lambda bk, bn: pl.BlockSpec((bk, bn), lambda i, j, k: (k, j))
    plain_o = lambda bm, bn: pl.BlockSpec((bm, bn), lambda i, j, k: (i, j))
    a_rows = lambda bm, bk: pl.BlockSpec((bm, bk), lambda i, j, k: (i, k))

    def cols_o(Nq):
        def spec(bm, bn):
            q = Nq // bn
            return pl.BlockSpec((None, bm, bn), lambda i, j, k: (j // q, i, j % q))
        return spec

    def du_epilogue(acc, ex_, outs):
        outs[0][...] = (acc * (2.0 * ex_[0][...].astype(F32))).astype(BF)

    dw_down = tn_grad("mlp_down_dw", act, plain_a, dx2b, plain_b, F, D, (F, D), plain_o)
    (du,) = nt_rows("mlp_down_dx", dx2b, wdown, du_epilogue,
                    lambda bm, bn: [(ru, plain_o(bm, bn))], lambda bm, bn: [(_sds((S, F), BF), plain_o(bm, bn))],
                    bn=2048)

    fq = gup.shape[2]
    dw_up = tn_grad("mlp_up_dw", h2, plain_a, du, plain_b, D, F, (N_CHIPS, D, fq), cols_o(fq), bn=min(fq, 1024))
    ex.reduce("mlp", partials=[dw_down.reshape(N_CHIPS, F // N_CHIPS, D), dw_up])
    (dh2,) = nt_cols("mlp_up_dx", a_rows, du, gup, S, _store(F32), lambda bm, bn: [],
                     lambda bm, bn: [(_sds((S, D), F32), plain_o(bm, bn))], 1024, bn=2048)
    ex.reduce("mlp")
    dx1, dx1b, d_norm_mlp = _rms_bwd("rms_mlp_bwd", dh2, x1, norm_mlp, dx2)

    dpa, dpb, dg3, db_gate, dy_a, dy_b = _out_proj_dx(dx1b, wout, g3, c3, gpa, gpb)
    dw_out = tn_grad("out_proj_dw", merged, plain_a, dx1b, plain_b, D, D, (D, D), plain_o)

    pq = gpa.shape[2]
    dw_pa = tn_grad("proj_a_dw", y_a, plain_a, dpa, plain_b, 512, D, (N_CHIPS, 512, pq), cols_o(pq), bn=min(pq, 512))
    dw_pb = tn_grad("proj_b_dw", y_b, plain_a, dpb, plain_b, 512, D, (N_CHIPS, 512, pq), cols_o(pq), bn=min(pq, 512))
    ex.reduce("mix", partials=[dw_out.reshape(N_CHIPS, D // N_CHIPS, D), dw_pa, dw_pb])

    dqkv3 = lax.empty((3, S, QKV_W), BF)
    dqkv3 = _attn_a_bwd(qkv3, dy_a, y_a, lj, dqkv3, 0, DILATIONS[0])
    ex.reduce("mix")
    dy_views, y_views, lj_views = _dilated_rows("attn_a_bwd_rows", [dy_a, y_a, lj])
    dqkv_views = {d: _attn_a_bwd(qkv_views[d], dy_views[d], y_views[d], lj_views[d], None, grp, d)
                  for grp, d in enumerate(DILATIONS) if grp > 0}
    dqkv3 = _qkv_views("dqkv_from_views", dqkv3, dqkv_views)
    dqkv3, de2 = _attn_b_bwd(qkv3, e2, dy_b, y_b, lse_b, dqkv3)
    d_rpb = _table_grad_to_rpb(de2)

    def stacked_a(width):
        def spec(bm, bk):
            q = width // bk
            return pl.BlockSpec((None, bm, bk), lambda i, j, k: (k // q, i, k % q))
        return spec

    def stacked_b(width):
        def spec(bk, bn):
            q = width // bn
            return pl.BlockSpec((None, bk, bn), lambda i, j, k: (j // q, k, j % q))
        return spec

    ex.reduce("mlp")
    dw_qkv = tn_grad("qkv_dw", h1, plain_a, dqkv3, stacked_b(QKV_W), D, 3 * QKV_W, (N_CHIPS,) + gq.shape[1:],
                     cols_o(gq.shape[2]), bn=512)
    dw_gate = tn_grad("gate_dw", h1, plain_a, dg3, stacked_b(D), D, 2 * D, (N_CHIPS,) + gg.shape[1:],
                      cols_o(gg.shape[2]), bn=gg.shape[2])
    ex.reduce("in", partials=[dw_qkv, dw_gate])
    ex.reduce("mlp")
    dh1_q, *ex.updates["down"] = nt_cols(
        "qkv_dx", stacked_a(QKV_W), dqkv3, gq, S, _store(F32), lambda bm, bn: [],
        lambda bm, bn: [(_sds((S, D), F32), plain_o(bm, bn))], 512, bn=2048, side=ex.adamw_beside("down"))
    ex.reduce("in")
    ex.reduce("mix")

    def add_epilogue(acc, ex_, outs):
        outs[0][...] = acc + ex_[0][...]

    dh1, *ex.updates["up"] = nt_cols(
        "gate_dx", stacked_a(D), dg3, gg, S, add_epilogue, lambda bm, bn: [(dh1_q, plain_o(bm, bn))],
        lambda bm, bn: [(_sds((S, D), F32), plain_o(bm, bn))], gg.shape[2], side=ex.adamw_beside("up"))
    grad_x, _, d_norm_mix = _rms_bwd("rms_mix_bwd", dh1, x, norm_mix, dx1)
    ex.reduce("mix")

    small = [d_norm_mix, jnp.sum(db_gate, axis=0).reshape(1, 2 * D), d_rpb, d_norm_mlp, d_norm_final]
    return loss, grad_x, small


def _pack_small(parts, width):
    flat = jnp.concatenate([p.reshape(-1) for p in parts])
    return jnp.pad(flat, (0, 8 * width - flat.shape[0])).reshape(8, width)


def kernel(x, norm_mix, w_qkv, w_gate, b_gate, rpb, w_proj_a, w_proj_b, w_out, norm_mlp, w_up, w_down, norm_final, loss_target, m_norm_mix, m_w_qkv, m_w_gate, m_b_gate, m_rpb, m_w_proj_a, m_w_proj_b, m_w_out, m_norm_mlp, m_w_up, m_w_down, m_norm_final, v_norm_mix, v_w_qkv, v_w_gate, v_b_gate, v_rpb, v_w_proj_a, v_w_proj_b, v_w_out, v_norm_mlp, v_w_up, v_w_down, v_norm_final):
    names = ["qkv", "gate", "proj_a", "proj_b", "out", "up", "down"]
    big = dict(zip(names, [w_qkv[0], w_gate[0], w_proj_a[0], w_proj_b[0], w_out[0], w_up[0], w_down[0]]))
    big_m = dict(zip(names, [m_w_qkv[0], m_w_gate[0], m_w_proj_a[0], m_w_proj_b[0], m_w_out[0], m_w_up[0], m_w_down[0]]))
    big_v = dict(zip(names, [v_w_qkv[0], v_w_gate[0], v_w_proj_a[0], v_w_proj_b[0], v_w_out[0], v_w_up[0], v_w_down[0]]))

    c = lax.axis_index("c").astype(jnp.int32).reshape(1)
    me = (2 * lax.axis_index("x") + lax.axis_index("y")).astype(jnp.int32).reshape(1)
    ORDER.last = None
    ex = _Exchange(big, me, c, {n: (big_m[n], big_v[n]) for n in names})
    loss, grad_x, small = _forward_backward(x[0], loss_target[0], norm_mix, b_gate, rpb[0], norm_mlp, norm_final, ex)

    def adamw(group):
        return {n: ex.updates[n] if ex.updates.get(n) else _adamw(f"adamw_{n}", big[n], ex.grads[n], big_m[n], big_v[n])
                for n in _Exchange.REDUCE[group]}

    big_out = {**adamw("mlp"), **adamw("mix")}
    ex.reduce("in")

    small_w = [norm_mix, b_gate, rpb, norm_mlp, norm_final]
    count = sum(int(np.prod(p.shape)) for p in small_w)
    width = -(-count // (8 * 128)) * 128
    packed = _adamw_small(_gather_small(_pack_small(small, width)), _pack_small(small_w, width),
                          _pack_small([m_norm_mix, m_b_gate, m_rpb, m_norm_mlp, m_norm_final], width),
                          _pack_small([v_norm_mix, v_b_gate, v_rpb, v_norm_mlp, v_norm_final], width))
    ex.reduce("in")
    big_out.update(adamw("in"))

    def unpack(flat2d):
        flat, out, at = flat2d.reshape(-1), [], 0
        for p in small_w:
            size = int(np.prod(p.shape))
            out.append(flat[at:at + size].reshape(p.shape))
            at += size
        return out

    small_out = [unpack(a) for a in packed]

    def ordered(kind):
        sm = small_out[kind]
        bg = {n: o[kind][None] for n, o in big_out.items()}
        return [sm[0], bg["qkv"], bg["gate"], sm[1], sm[2], bg["proj_a"], bg["proj_b"], bg["out"], sm[3],
                bg["up"], bg["down"], sm[4]]

    total = lax.psum(loss[0, 0], ("x", "y", "c"))
    return (total, grad_x[None], *ordered(0), *ordered(1), *ordered(2), *ordered(3))
```

```python
import math

import numpy as np
import jax
import jax.numpy as jnp
from jax import lax
from jax.experimental import pallas as pl
from jax.experimental.pallas import tpu as pltpu

BF = jnp.bfloat16
F32 = jnp.float32
MESH = pl.DeviceIdType.MESH

HEAD_DIM = 128
N_HEADS = 16
N_HEADS_A = 12
QKV_W = N_HEADS * HEAD_DIM
DILATIONS = (1, 4, 16)
HALF_WINDOW = 64
GRID_W = 64
NA_ROWS = 8
NA_COLS = 16
RPB_ROWS = 2 * NA_ROWS - 1
RPB_COLS = 2 * NA_COLS - 1
EPS = 1e-6
NEG = -1e30
SCALE = HEAD_DIM ** -0.5

ADAM_LR = 0.001
ADAM_B1 = 0.9
ADAM_B2 = 0.999
ADAM_EPS = 1e-08
ADAM_WD = 0.01
ADAM_STEP = 10

N_CHIPS = 4
VMEM_LIMIT_BYTES = 48 * 1024 * 1024
QB = 256
NBR_SIDE = 64
ROW_TILE = 512


def _key_rows(L):
    return min(QB + 2 * HALF_WINDOW, L)


def _cparams(sem=None):
    return pltpu.CompilerParams(dimension_semantics=sem, vmem_limit_bytes=VMEM_LIMIT_BYTES)


def _tile(dim, want):
    t = min(dim, want)
    assert dim % t == 0, (dim, want)
    return t


class _ProgramOrder:
    def __init__(self):
        self.last = None

    def call(self, body, operands, in_specs, *, prefetch=(), grid=None, out_specs=None, chain_output=0, **kwargs):
        operands, in_specs = list(operands), list(in_specs)
        lead = len(prefetch) + len(operands)
        if self.last is not None and not any(op is self.last for op in operands):
            operands.append(self.last)
            in_specs.append(pl.BlockSpec(memory_space=pl.ANY))
            inner = body

            def body(*refs):
                return inner(*refs[:lead], *refs[lead + 1:])

        if prefetch:
            kwargs["grid_spec"] = pltpu.PrefetchScalarGridSpec(
                num_scalar_prefetch=len(prefetch), grid=grid, in_specs=in_specs, out_specs=out_specs)
        else:
            kwargs.update(in_specs=in_specs, out_specs=out_specs)
            if grid is not None:
                kwargs["grid"] = grid
        out = pl.pallas_call(body, **kwargs)(*prefetch, *operands)
        self.last = out[chain_output] if isinstance(out, (tuple, list)) else out
        return out


ORDER = _ProgramOrder()


NN = ((1,), (0,))
NT = ((1,), (1,))
TN = ((0,), (0,))


def _matmul(name, a, b, a_spec, b_spec, dims, grid, acc_shape, extras, outs, epilogue, precision=None,
            prefetch=(), into=None, side=None):
    n_ex, n_out, nk = len(extras), len(outs), grid[2]
    side_fn, side_in, n_side_out = side if side is not None else (None, [], 0)
    side_spec = None
    n_in = 2 + n_ex + len(side_in) + (into is not None)
    if side is not None:
        R, C = side_in[0].shape
        steps = grid[0] * grid[1] * grid[2]
        side_blocks = max(n for n in range(1, steps + 1) if R % n == 0 and (R // n) % 8 == 0)

        def side_step(*ids):
            return (ids[0] * grid[1] + ids[1]) * grid[2] + ids[2]

        side_spec = pl.BlockSpec((R // side_blocks, C),
                                 lambda *ids: (jnp.minimum(side_step(*ids), side_blocks - 1), 0))

    def body(*refs):
        refs = refs[len(prefetch):]
        a_ref, b_ref = refs[0], refs[1]
        ex_refs = refs[2:2 + n_ex]
        out_refs = refs[n_in:n_in + n_out]
        if side is not None:
            @pl.when(side_step(pl.program_id(0), pl.program_id(1), pl.program_id(2)) < side_blocks)
            def _():
                results = side_fn(*[r[...] for r in refs[2 + n_ex:2 + n_ex + len(side_in)]])
                for o_ref, value in zip(refs[n_in + n_out:n_in + n_out + n_side_out], results):
                    o_ref[...] = value

        def dot():
            return lax.dot_general(a_ref[...], b_ref[...], (dims, ((), ())),
                                   preferred_element_type=F32, precision=precision)

        if nk == 1:
            epilogue(dot(), ex_refs, out_refs)
            return
        acc_ref = refs[-1]
        k = pl.program_id(2)

        @pl.when(k == 0)
        def _():
            acc_ref[...] = dot()

        if nk > 2:
            @pl.when((k > 0) & (k < nk - 1))
            def _():
                acc_ref[...] += dot()

        @pl.when(k == nk - 1)
        def _():
            epilogue(acc_ref[...] + dot(), ex_refs, out_refs)

    operands = [a, b] + [e for e, _ in extras] + list(side_in)
    in_specs = [a_spec, b_spec] + [s for _, s in extras] + [side_spec] * len(side_in)
    kwargs = {}
    if into is not None:
        operands.append(into)
        in_specs.append(pl.BlockSpec(memory_space=pl.ANY))
        kwargs["input_output_aliases"] = {len(prefetch) + n_in - 1: 0}
    return ORDER.call(
        body, operands, in_specs, prefetch=prefetch, name=name, grid=grid,
        out_specs=[s for _, s in outs] + [side_spec] * n_side_out,
        out_shape=[sh for sh, _ in outs] + [_sds(s_.shape, F32) for s_ in side_in[:1]] * n_side_out,
        scratch_shapes=[pltpu.VMEM(acc_shape, F32)] if nk > 1 else [],
        compiler_params=_cparams(("parallel", "parallel", "arbitrary")), **kwargs,
    )


def _mm_nn_shards(name, a, w, me, own, out, out_block, epilogue, extras=(), into=None, tn=512, tm=1024):
    M, K = a.shape
    Nq = w.shape[-1]
    tm, tn = _tile(M, tm), _tile(Nq, tn)
    q = Nq // tn

    def tile(j, me_ref):
        shard = me_ref[0] if own else (me_ref[0] + 1 + j // q) % N_CHIPS
        return shard, j % q, shard * q + j % q

    if own:
        b_spec = pl.BlockSpec((K, tn), lambda i, j, k, me_ref: (0, j))
    else:
        b_spec = pl.BlockSpec((None, K, tn), lambda i, j, k, me_ref: (tile(j, me_ref)[0], 0, tile(j, me_ref)[1]))
    shape, dtype = out
    out_spec = pl.BlockSpec((None, tm, tn), lambda i, j, k, me_ref: out_block(i, tile(j, me_ref)[2]))
    ex = [(e, pl.BlockSpec((1, tn), lambda i, j, k, me_ref: (0, tile(j, me_ref)[2]))) for e in extras]
    return _matmul(name, a, w, pl.BlockSpec((tm, K), lambda i, j, k, me_ref: (i, 0)), b_spec, NN,
                   (M // tm, q if own else (N_CHIPS - 1) * q, 1), (tm, tn), ex, [(_sds(shape, dtype), out_spec)],
                   epilogue, prefetch=(me,), into=into)[0]


def _store(dtype):
    def epilogue(acc, ex, outs):
        outs[0][...] = acc.astype(dtype)
    return epilogue


def _sds(shape, dtype):
    return jax.ShapeDtypeStruct(shape, dtype)


def _mm_nn_cols(name, a, g, out_dtype, epilogue=None, extras=(), outs=None, tm=1024, tn=1024, tk=2048):
    M, K = a.shape
    _, _, Nq = g.shape
    tm, tn, tk = _tile(M, tm), _tile(Nq, tn), _tile(K, tk)
    q = Nq // tn
    grid = (M // tm, N_CHIPS * q, K // tk)
    if outs is None:
        outs = [(_sds((M, N_CHIPS * Nq), out_dtype), pl.BlockSpec((tm, tn), lambda i, j, k: (i, j)))]
    return _matmul(name, a, g, pl.BlockSpec((tm, tk), lambda i, j, k: (i, k)),
                   pl.BlockSpec((None, tk, tn), lambda i, j, k: (j // q, k, j % q)), NN, grid, (tm, tn),
                   list(extras), outs, epilogue or _store(out_dtype)), (tm, tn, tk)


def _rms_fwd(name, x, g):
    S, D = x.shape
    tm = _tile(S, ROW_TILE)

    def body(x_ref, g_ref, h_ref):
        xv = x_ref[...]
        r = lax.rsqrt(jnp.mean(xv * xv, axis=-1, keepdims=True) + EPS)
        h_ref[...] = ((xv * r) * g_ref[...]).astype(BF)

    row = pl.BlockSpec((tm, D), lambda i: (i, 0))
    return ORDER.call(
        body, [x, g], [row, pl.BlockSpec((1, D), lambda i: (0, 0))], name=name, grid=(S // tm,),
        out_specs=row, out_shape=_sds((S, D), BF), compiler_params=_cparams(("parallel",)),
    )


def _rms_bwd(name, dh, x, g, dres):
    S, D = x.shape
    tm = _tile(S, ROW_TILE // 2)

    def body(dh_ref, x_ref, g_ref, dres_ref, dx_ref, dxb_ref, dg_ref):
        xv = x_ref[...]
        r = lax.rsqrt(jnp.mean(xv * xv, axis=-1, keepdims=True) + EPS)
        n = xv * r
        dhv = dh_ref[...]
        dyg = dhv * g_ref[...]
        dx = dres_ref[...] + r * (dyg - n * jnp.mean(dyg * n, axis=-1, keepdims=True))
        dx_ref[...] = dx
        dxb_ref[...] = dx.astype(BF)

        @pl.when(pl.program_id(0) == 0)
        def _():
            dg_ref[...] = jnp.zeros_like(dg_ref)

        dg_ref[...] += jnp.sum(dhv * n, axis=0, keepdims=True)

    row = pl.BlockSpec((tm, D), lambda i: (i, 0))
    vec = pl.BlockSpec((1, D), lambda i: (0, 0))
    return ORDER.call(
        body, [dh, x, g, dres], [row, row, vec, row], name=name, grid=(S // tm,),
        out_specs=[row, row, vec],
        out_shape=[_sds((S, D), F32), _sds((S, D), BF), _sds((1, D), F32)],
        compiler_params=_cparams(("arbitrary",)),
    )


def _loss_head(x2, target, g):
    S, D = x2.shape
    tm = _tile(S, ROW_TILE)

    def body(x_ref, t_ref, g_ref, loss_ref, dx_ref, dxb_ref, dg_ref):
        xv = x_ref[...]
        gv = g_ref[...]
        r = lax.rsqrt(jnp.mean(xv * xv, axis=-1, keepdims=True) + EPS)
        n = xv * r
        e = n * gv - t_ref[...]
        dy = e * (1.0 / D)
        dyg = dy * gv
        dx = r * (dyg - n * jnp.mean(dyg * n, axis=-1, keepdims=True))
        dx_ref[...] = dx
        dxb_ref[...] = dx.astype(BF)

        @pl.when(pl.program_id(0) == 0)
        def _():
            dg_ref[...] = jnp.zeros_like(dg_ref)
            loss_ref[...] = jnp.zeros_like(loss_ref)

        dg_ref[...] += jnp.sum(dy * n, axis=0, keepdims=True)
        per_row = jnp.mean(e * e, axis=-1, keepdims=True)
        loss_ref[...] += 0.5 * jnp.sum(per_row, axis=0, keepdims=True)

    row = pl.BlockSpec((tm, D), lambda i: (i, 0))
    vec = pl.BlockSpec((1, D), lambda i: (0, 0))
    return ORDER.call(
        body, [x2, target, g], [row, row, vec], name="loss_head", grid=(S // tm,),
        out_specs=[pl.BlockSpec((1, 1), lambda i: (0, 0)), row, row, vec],
        out_shape=[_sds((1, 1), F32), _sds((S, D), F32), _sds((S, D), BF), _sds((1, D), F32)],
        compiler_params=_cparams(("arbitrary",)), chain_output=1,
    )


def _chains(L):
    side = min(8, L // QB)
    return side, min(4, max(1, 8 // side))


def _band_scores(qkv_ref, i, L, coef, head):
    KB = _key_rows(L)
    lanes = pl.ds(head * HEAD_DIM, HEAD_DIM)
    q0 = pl.multiple_of(i * QB, QB)
    ks = pl.multiple_of(jnp.clip(i * QB - HALF_WINDOW, 0, L - KB), HALF_WINDOW)
    q = qkv_ref[0, pl.ds(q0, QB), lanes]
    k = qkv_ref[1, pl.ds(ks, KB), lanes]
    v = qkv_ref[2, pl.ds(ks, KB), lanes]
    s = lax.dot_general(q, k, (NT, ((), ())), preferred_element_type=F32) * SCALE
    qpos = q0 + lax.broadcasted_iota(jnp.int32, (QB, KB), 0)
    kpos = ks + lax.broadcasted_iota(jnp.int32, (QB, KB), 1)
    rel = jnp.abs(kpos - qpos)
    valid = rel <= HALF_WINDOW
    s = jnp.where(valid, s - coef * rel.astype(F32), NEG)
    return q0, ks, q, k, v, s, valid


def _alibi_coefs(group, d, heads):
    first = 4 * group + 1 + pl.program_id(1) * heads
    scale = jnp.full((1, 1), -(8.0 / N_HEADS_A) * math.log(2.0), F32)
    return [jnp.exp(scale * (first + hh).astype(F32)) * float(d) for hh in range(heads)]


def _dilated_view(qkv3, group, d, heads):
    per = 4 // heads
    L = qkv3.shape[1]
    if d == 1:
        return qkv3, pl.BlockSpec((3, L, heads * HEAD_DIM), lambda r, j: (0, 0, per * group + j))
    return qkv3, pl.BlockSpec((3, L, heads * HEAD_DIM), lambda r, j: (0, 0, r * per + j))


def _qkv_views(name, qkv3, views=None):
    _, S, _ = qkv3.shape
    W = 512
    tm = _tile(S, 2 * ROW_TILE)
    dilated = [(g, d) for g, d in enumerate(DILATIONS) if d > 1]
    first = dilated[0][0]
    assert [g for g, _ in dilated] == list(range(first, first + len(dilated)))
    nc = W // 128
    to_views = views is None

    def body(*refs):
        scr = refs[-nc:]
        if to_views:
            src, outs = refs[0], refs[1:1 + len(dilated)]
        else:
            ins, dst = refs[:len(dilated)], refs[len(dilated) + 1]
        for k, (_, d) in enumerate(dilated):
            @pl.when(pl.program_id(1) == k)
            def _():
                for w in range(3):
                    for c in range(nc):
                        if to_views:
                            scr[c][...] = src[w, :, c * 128:(c + 1) * 128].astype(F32)
                    for r in range(d):
                        for c in range(nc):
                            at = r * W + c * 128
                            if to_views:
                                outs[k][w, :, at:at + 128] = scr[c][pl.ds(r, tm // d, stride=d), :].astype(BF)
                            else:
                                scr[c][pl.ds(r, tm // d, stride=d), :] = ins[k][w, :, at:at + 128].astype(F32)
                    for c in range(nc):
                        if not to_views:
                            dst[w, :, c * 128:(c + 1) * 128] = scr[c][...].astype(BF)

    cols = pl.BlockSpec((3, tm, W), lambda i, k: (0, i, first + k))
    rows = [pl.BlockSpec((3, tm // d, d * W), lambda i, k: (0, i, 0)) for _, d in dilated]
    shapes = [_sds((3, S // d, d * W), BF) for _, d in dilated]
    common = dict(name=name, grid=(S // tm, len(dilated)), scratch_shapes=[pltpu.VMEM((tm, 128), F32)] * nc,
                  compiler_params=_cparams(("parallel", "arbitrary")))
    if to_views:
        outs = ORDER.call(body, [qkv3], [cols], out_specs=rows, out_shape=shapes, **common)
        return {d: o for (_, d), o in zip(dilated, outs)}
    return ORDER.call(body, [views[d] for _, d in dilated] + [qkv3], rows + [pl.BlockSpec(memory_space=pl.ANY)],
                      out_specs=cols, out_shape=_sds(qkv3.shape, BF), input_output_aliases={len(dilated): 0}, **common)


def _attn_a_fwd(qkv3, group, d):
    L = qkv3.shape[1]
    S = L * d
    assert L % QB == 0
    side, heads = _chains(L)
    view, blocks_spec = _dilated_view(qkv3, group, d, heads)

    def body(qkv_ref, o_ref, lse_ref):
        coefs = _alibi_coefs(group, d, heads)

        def step(i, carry):
            chains = [(hh, _band_scores(qkv_ref, side * i + u, L, coefs[hh], hh))
                      for u in range(side) for hh in range(heads)]
            soft = []
            for hh, (q0, _, _, _, v, s, _) in chains:
                m = jnp.max(s, axis=-1, keepdims=True)
                p = jnp.exp(s - m)
                den = jnp.sum(p, axis=-1, keepdims=True)
                soft.append((hh, q0, (p / den).astype(BF), v, m + jnp.log(den)))
            for hh, q0, pn, v, lse in soft:
                lanes = pl.ds(hh * HEAD_DIM, HEAD_DIM)
                o_ref[pl.ds(q0, QB), lanes] = jnp.dot(pn, v, preferred_element_type=F32)
                lse_ref[pl.ds(q0, QB), lanes] = jnp.broadcast_to(lse, (QB, HEAD_DIM))
            return carry

        lax.fori_loop(0, L // QB // side, step, 0)

    per = 4 // heads
    out = pl.BlockSpec((L, heads * HEAD_DIM), lambda r, j: (0, r * per + j))
    o, lse = ORDER.call(
        body, [view], [blocks_spec],
        name=f"attn_a_fwd_d{d}", grid=(d, per),
        out_specs=[out, out],
        out_shape=[_sds((L, d * 512), F32), _sds((L, d * 512), F32)],
        compiler_params=_cparams(("parallel", "parallel")),
    )
    return o, lse


def _dilated_rows(name, arrays):
    S, W = arrays[0].shape
    tm = _tile(S, ROW_TILE)
    ds_ = [d for d in DILATIONS if d > 1]
    n = len(arrays)

    def body(*refs):
        nc = W // 128
        ins, outs, scr = refs[:n], refs[n:-nc], refs[-nc:]
        for a, src in enumerate(ins):
            for c in range(nc):
                scr[c][...] = src[:, c * 128:(c + 1) * 128].astype(F32)
            for k, d in enumerate(ds_):
                dst = outs[a * len(ds_) + k]
                for r in range(d):
                    for c in range(nc):
                        at = r * W + c * 128
                        dst[:, at:at + 128] = scr[c][pl.ds(r, tm // d, stride=d), :].astype(dst.dtype)

    row = pl.BlockSpec((tm, W), lambda i: (i, 0))
    out_specs, out_shape = [], []
    for a in arrays:
        for d in ds_:
            out_specs.append(pl.BlockSpec((tm // d, d * W), lambda i: (i, 0)))
            out_shape.append(_sds((S // d, d * W), a.dtype))
    outs = ORDER.call(body, list(arrays), [row] * n, name=name, grid=(S // tm,), out_specs=out_specs,
                      out_shape=out_shape, scratch_shapes=[pltpu.VMEM((tm, 128), F32)] * (W // 128),
                      compiler_params=_cparams(("parallel",)))
    return [{d: outs[a * len(ds_) + k] for k, d in enumerate(ds_)} for a in range(n)]


def _attn_a_combine(os_, lses):
    W = 512
    S = os_[0].shape[0] * DILATIONS[0]
    tm = _tile(S, ROW_TILE)
    nc = W // 128
    dilated = [g for g, d in enumerate(DILATIONS) if d > 1]

    def body(o0, o1, o2, l0, l1, l2, y_ref, lj_ref, *scr):
        def token_order(src, g, slot):
            d = DILATIONS[g]
            if d == 1:
                return src[...]
            bufs = scr[slot * nc:(slot + 1) * nc]
            for r in range(d):
                for c in range(nc):
                    at = r * W + c * 128
                    bufs[c][pl.ds(r, tm // d, stride=d), :] = src[:, at:at + 128]
            return jnp.concatenate([buf[...] for buf in bufs], axis=1)

        slots = {g: k for k, g in enumerate(dilated)}
        ls = [token_order(l, g, slots.get(g, 0)) for g, l in enumerate((l0, l1, l2))]
        os_tok = [token_order(o, g, len(dilated) + slots.get(g, 0)) for g, o in enumerate((o0, o1, o2))]
        m = jnp.maximum(jnp.maximum(ls[0], ls[1]), ls[2])
        es = [jnp.exp(l - m) for l in ls]
        den = es[0] + es[1] + es[2]
        y = (es[0] / den) * os_tok[0] + (es[1] / den) * os_tok[1] + (es[2] / den) * os_tok[2]
        y_ref[...] = y.astype(BF)
        lj_ref[...] = m + jnp.log(den)

    row = pl.BlockSpec((tm, W), lambda i: (i, 0))
    views = [pl.BlockSpec((tm // d, d * W), lambda i: (i, 0)) for d in DILATIONS]
    return ORDER.call(
        body, [*os_, *lses], views + views, name="attn_a_combine", grid=(S // tm,), out_specs=[row, row],
        out_shape=[_sds((S, W), BF), _sds((S, W), F32)],
        scratch_shapes=[pltpu.VMEM((tm, 128), F32)] * (2 * len(dilated) * nc),
        compiler_params=_cparams(("parallel",)),
    )


def _attn_a_bwd(qkv3, dy, y, lj, dqkv3, group, d):
    L = qkv3.shape[1]
    S = L * d
    side, heads = _chains(L)
    view, blocks_spec = _dilated_view(qkv3, group, d, heads)

    def body(qkv_ref, dy_ref, y_ref, lj_ref, *rest):
        out_ref, dk_acc, dv_acc = rest[-3:]
        coefs = _alibi_coefs(group, d, heads)
        dk_acc[...] = jnp.zeros_like(dk_acc)
        dv_acc[...] = jnp.zeros_like(dv_acc)

        def step(i, carry):
            chains = [(pl.ds(hh * HEAD_DIM, HEAD_DIM), _band_scores(qkv_ref, side * i + u, L, coefs[hh], hh))
                      for u in range(side) for hh in range(heads)]
            dys = [dy_ref[pl.ds(c[0], QB), lanes] for lanes, c in chains]
            dps = [lax.dot_general(dyv, c[4], (NT, ((), ())), preferred_element_type=F32)
                   for dyv, (_, c) in zip(dys, chains)]
            grads = []
            for (lanes, (q0, ks, q, k, v, s, valid)), dyv, dp in zip(chains, dys, dps):
                rows = pl.ds(q0, QB)
                delta = jnp.sum(dyv.astype(F32) * y_ref[rows, lanes].astype(F32), axis=-1, keepdims=True)
                p = jnp.where(valid, jnp.exp(s - jnp.tile(lj_ref[rows, lanes], (1, _key_rows(L) // HEAD_DIM))), 0.0)
                grads.append(((p * (dp - delta)).astype(BF), p.astype(BF)))
            for (lanes, (q0, ks, q, k, v, s, valid)), dyv, (ds, pb) in zip(chains, dys, grads):
                out_ref[0, pl.ds(q0, QB), lanes] = (jnp.dot(ds, k, preferred_element_type=F32) * SCALE).astype(BF)
                keys = pl.ds(ks, _key_rows(L))
                dk_acc[keys, lanes] += lax.dot_general(ds, q, (TN, ((), ())), preferred_element_type=F32) * SCALE
                dv_acc[keys, lanes] += lax.dot_general(pb, dyv, (TN, ((), ())), preferred_element_type=F32)
            return carry

        lax.fori_loop(0, L // QB // side, step, 0)
        out_ref[1] = dk_acc[...].astype(BF)
        out_ref[2] = dv_acc[...].astype(BF)

    per = 4 // heads
    width = heads * HEAD_DIM
    row = pl.BlockSpec((L, width), lambda r, j: (0, r * per + j))
    operands = [view, dy, y, lj]
    scratch = [pltpu.VMEM((L, width), F32), pltpu.VMEM((L, width), F32)]
    if d == 1:
        return ORDER.call(
            body, operands + [dqkv3], [blocks_spec, row, row, row, pl.BlockSpec(memory_space=pl.ANY)],
            name=f"attn_a_bwd_d{d}", grid=(d, per), out_specs=blocks_spec, out_shape=_sds((3, S, QKV_W), BF),
            scratch_shapes=scratch, input_output_aliases={4: 0}, compiler_params=_cparams(("parallel", "parallel")))
    return ORDER.call(
        body, operands, [blocks_spec, row, row, row], name=f"attn_a_bwd_d{d}", grid=(d, per),
        out_specs=blocks_spec, out_shape=_sds((3, L, d * 512), BF),
        scratch_shapes=scratch, compiler_params=_cparams(("parallel", "parallel")))


def _toeplitz_onehot():
    oh = np.zeros((64, GRID_W, 128), np.float32)
    for qc in range(GRID_W):
        for m in range(128):
            kc = m % GRID_W
            dc = int(np.clip(kc - qc, -(NA_COLS - 1), NA_COLS - 1)) + NA_COLS - 1
            oh[(m // GRID_W) * 32 + dc, qc, m] = 1.0
    return oh.reshape(64, GRID_W * 128)


def _nbr_scores(qkv_ref, e2_ref, r, rows, ok):
    rs = jnp.clip(r - NA_ROWS // 2, 0, rows - NA_ROWS)
    q0 = pl.multiple_of(r * GRID_W, GRID_W)
    k0 = pl.multiple_of(rs * GRID_W, GRID_W)
    q = qkv_ref[0, pl.ds(q0, GRID_W), :]
    k = qkv_ref[1, pl.ds(k0, NA_ROWS * GRID_W), :]
    v = qkv_ref[2, pl.ds(k0, NA_ROWS * GRID_W), :]
    s = lax.dot_general(q, k, (NT, ((), ())), preferred_element_type=F32) * SCALE
    first = rs - r + NA_ROWS - 1
    bias = jnp.concatenate([e2_ref[first + 2 * pair] for pair in range(NA_ROWS // 2)], axis=1)
    s = jnp.where(ok, s + bias, NEG)
    return q0, k0, first, q, k, v, s


def _nbr_col_ok():
    qc = lax.broadcasted_iota(jnp.int32, (GRID_W, NA_ROWS * GRID_W), 0)
    kc = lax.broadcasted_iota(jnp.int32, (GRID_W, NA_ROWS * GRID_W), 1) % GRID_W
    cs = jnp.clip(qc - NA_COLS // 2, 0, GRID_W - NA_COLS)
    return (kc >= cs) & (kc < cs + NA_COLS)


def _attn_b_fwd(qkv3, e2):
    _, S, _ = qkv3.shape
    rows = S // GRID_W
    assert rows >= NA_ROWS

    def body(qkv_ref, e2_ref, o_ref, lse_ref):
        ok = _nbr_col_ok()

        def step(i, carry):
            blocks = [_nbr_scores(qkv_ref, e2_ref, NBR_SIDE * i + u, rows, ok) for u in range(NBR_SIDE)]
            soft = []
            for q0, _, _, _, _, v, s in blocks:
                m = jnp.max(s, axis=-1, keepdims=True)
                p = jnp.exp(s - m)
                den = jnp.sum(p, axis=-1, keepdims=True)
                soft.append((q0, (p / den).astype(BF), v, m + jnp.log(den)))
            for q0, pn, v, lse in soft:
                o_ref[pl.ds(q0, GRID_W), :] = jnp.dot(pn, v, preferred_element_type=F32).astype(BF)
                lse_ref[pl.ds(q0, GRID_W), :] = jnp.broadcast_to(lse, (GRID_W, HEAD_DIM))
            return carry

        lax.fori_loop(0, rows // NBR_SIDE, step, 0)

    out = pl.BlockSpec((S, HEAD_DIM), lambda h: (0, h))
    return ORDER.call(
        body, [qkv3, e2],
        [pl.BlockSpec((3, S, HEAD_DIM), lambda h: (0, 0, N_HEADS_A + h)),
         pl.BlockSpec((None, RPB_ROWS - 1, GRID_W, 128), lambda h: (h, 0, 0, 0))],
        name="attn_b_fwd", grid=(4,),
        out_specs=[out, out], out_shape=[_sds((S, 512), BF), _sds((S, 512), F32)],
        compiler_params=_cparams(("parallel",)),
    )


def _attn_b_bwd(qkv3, e2, dy, y, lse, dqkv3):
    _, S, _ = qkv3.shape
    rows = S // GRID_W
    nk = NA_ROWS * GRID_W

    def body(qkv_ref, e2_ref, dy_ref, y_ref, lse_ref, _, out_ref, de2_ref, dk_acc, dv_acc):
        ok = _nbr_col_ok()
        dk_acc[...] = jnp.zeros_like(dk_acc)
        dv_acc[...] = jnp.zeros_like(dv_acc)
        de2_ref[...] = jnp.zeros_like(de2_ref)

        def step(i, carry):
            blocks = [_nbr_scores(qkv_ref, e2_ref, NBR_SIDE * i + u, rows, ok) for u in range(NBR_SIDE)]
            dys = [dy_ref[pl.ds(b[0], GRID_W), :] for b in blocks]
            dps = [lax.dot_general(dyv, b[5], (NT, ((), ())), preferred_element_type=F32) for dyv, b in zip(dys, blocks)]
            grads = []
            for (q0, k0, first, q, k, v, s), dyv, dp in zip(blocks, dys, dps):
                qrows = pl.ds(q0, GRID_W)
                delta = jnp.sum(dyv.astype(F32) * y_ref[qrows, :].astype(F32), axis=-1, keepdims=True)
                p = jnp.where(ok, jnp.exp(s - jnp.tile(lse_ref[qrows, :], (1, nk // HEAD_DIM))), 0.0)
                ds = p * (dp - delta)
                for pair in range(NA_ROWS // 2):
                    de2_ref[first + 2 * pair] += ds[:, pair * 128:(pair + 1) * 128]
                grads.append((ds.astype(BF), p.astype(BF)))
            for (q0, k0, first, q, k, v, s), dyv, (dsb, pb) in zip(blocks, dys, grads):
                out_ref[0, pl.ds(q0, GRID_W), :] = (jnp.dot(dsb, k, preferred_element_type=F32) * SCALE).astype(BF)
                keys = pl.ds(k0, nk)
                dk_acc[keys, :] += lax.dot_general(dsb, q, (TN, ((), ())), preferred_element_type=F32) * SCALE
                dv_acc[keys, :] += lax.dot_general(pb, dyv, (TN, ((), ())), preferred_element_type=F32)
            return carry

        lax.fori_loop(0, rows // NBR_SIDE, step, 0)
        out_ref[1] = dk_acc[...].astype(BF)
        out_ref[2] = dv_acc[...].astype(BF)

    heads = pl.BlockSpec((3, S, HEAD_DIM), lambda h: (0, 0, N_HEADS_A + h))
    row = pl.BlockSpec((S, HEAD_DIM), lambda h: (0, h))
    table = pl.BlockSpec((None, RPB_ROWS - 1, GRID_W, 128), lambda h: (h, 0, 0, 0))
    return ORDER.call(
        body, [qkv3, e2, dy, y, lse, dqkv3],
        [heads, table, row, row, row, pl.BlockSpec(memory_space=pl.ANY)], name="attn_b_bwd", grid=(4,),
        out_specs=[heads, table],
        out_shape=[_sds((3, S, QKV_W), BF), _sds((4, RPB_ROWS - 1, GRID_W, 128), F32)],
        scratch_shapes=[pltpu.VMEM((S, HEAD_DIM), F32), pltpu.VMEM((S, HEAD_DIM), F32)],
        input_output_aliases={5: 0},
        compiler_params=_cparams(("parallel",)), chain_output=1,
    )


def _rpb_to_table(rpb):
    pad = jnp.pad(rpb, ((0, 0), (0, 0), (0, 1)))
    pairs = jnp.concatenate([pad[:, :-1], pad[:, 1:]], axis=-1).reshape(4 * (RPB_ROWS - 1), 64)
    onehot = jnp.asarray(_toeplitz_onehot())
    n = onehot.shape[1]
    tn = 2048
    full = lambda i, j, k: (0, 0)
    (e2,) = _matmul("rpb_table", pairs, onehot, pl.BlockSpec(pairs.shape, full),
                    pl.BlockSpec((64, tn), lambda i, j, k: (0, j)), NN, (1, n // tn, 1), (pairs.shape[0], tn), [],
                    [(_sds((pairs.shape[0], n), F32), pl.BlockSpec((pairs.shape[0], tn), lambda i, j, k: (0, j)))],
                    _store(F32), precision=lax.Precision.HIGHEST)
    return e2.reshape(4, RPB_ROWS - 1, GRID_W, 128)


def _table_grad_to_rpb(de2):
    onehot = jnp.asarray(_toeplitz_onehot())
    n = onehot.shape[1]
    flat = de2.reshape(4 * (RPB_ROWS - 1), n)
    tk = 2048
    (dpairs,) = _matmul("rpb_table_grad", flat, onehot, pl.BlockSpec((flat.shape[0], tk), lambda i, j, k: (0, k)),
                        pl.BlockSpec((64, tk), lambda i, j, k: (0, k)), NT, (1, 1, n // tk), (flat.shape[0], 64), [],
                        [(_sds((flat.shape[0], 64), F32), pl.BlockSpec((flat.shape[0], 64), lambda i, j, k: (0, 0)))],
                        _store(F32), precision=lax.Precision.HIGHEST)
    dpairs = dpairs.reshape(4, RPB_ROWS - 1, 64)
    zero = jnp.zeros((4, 1, RPB_COLS), F32)
    return (jnp.concatenate([dpairs[:, :, :RPB_COLS], zero], axis=1)
            + jnp.concatenate([zero, dpairs[:, :, 32:32 + RPB_COLS]], axis=1))


HBM = pl.BlockSpec(memory_space=pl.ANY)


def _place():
    x, y, c = lax.axis_index("x"), lax.axis_index("y"), lax.axis_index("c")
    chips = [(1 - x, y), (x, 1 - y), (1 - x, 1 - y)]
    return x, y, c, chips


def _remote(src, dst, send_sem, recv_sem, to):
    return pltpu.make_async_remote_copy(src_ref=src, dst_ref=dst, send_sem=send_sem, recv_sem=recv_sem,
                                        device_id=to, device_id_type=MESH)


def _place_shards(name, ws, me, plain=False):
    R = ws[0].shape[0]
    assert all(w.shape[0] == R for w in ws)
    tr = _tile(R, 256)
    n = len(ws)

    def body(me_ref, *refs):
        for k in range(n):
            value = refs[k][...].astype(BF)
            refs[n + k][...] = value
            if plain:
                refs[2 * n + k][...] = value

    rows = [pl.BlockSpec((tr, w.shape[1]), lambda i, mr: (i, 0)) for w in ws]
    placed = [pl.BlockSpec((None, tr, w.shape[1]), lambda i, mr: (mr[0], i, 0)) for w in ws]
    return ORDER.call(
        body, list(ws), rows, prefetch=(me,), name=name, grid=(R // tr,),
        out_specs=placed + (rows if plain else []),
        out_shape=[_sds((N_CHIPS,) + w.shape, BF) for w in ws] + ([_sds(w.shape, BF) for w in ws] if plain else []),
        compiler_params=_cparams(("parallel",)),
    )


SEM = pl.BlockSpec(memory_space=pltpu.SEMAPHORE)
IN_HBM = pl.BlockSpec(memory_space=pltpu.HBM)
DATAFLOW = pltpu.SideEffectType.DATAFLOW_SIDE_EFFECTING


def _in_hbm(a):
    return pltpu.with_memory_space_constraint(a, pltpu.HBM)


def _copy_start(name, bufs, copies, n_copies, earlier=None):
    n = len(bufs)
    after = None if any(b is ORDER.last for b in bufs) else ORDER.last
    n_extra = (2 if earlier is not None else 0) + (1 if after is not None else 0)

    def body(*refs):
        ins = refs[:n]
        if earlier is not None:
            for k, (src, dst, to) in enumerate(earlier[0](ins)):
                cp = _remote(src, dst, refs[n].at[k], refs[n + 1].at[k], to)
                cp.wait_send()
                cp.wait_recv()
        send_sems, recv_sems = refs[n + n_extra], refs[n + n_extra + 1]
        for k, (src, dst, to) in enumerate(copies(ins)):
            _remote(src, dst, send_sems.at[k], recv_sems.at[k], to).start()
        refs[-1][...] = jnp.zeros((8, 128), F32)

    operands = [_in_hbm(b) for b in bufs]
    in_specs = [IN_HBM] * n
    if earlier is not None:
        operands += [earlier[1], earlier[2]]
        in_specs += [SEM, SEM]
    if after is not None:
        operands.append(after)
        in_specs.append(HBM)
    outs = pl.pallas_call(
        body, name=name,
        out_shape=(pltpu.SemaphoreType.DMA((n_copies,)), pltpu.SemaphoreType.DMA((n_copies,)),
                   *[pltpu.HBM(b.shape, b.dtype) for b in bufs], _sds((8, 128), F32)),
        in_specs=in_specs,
        out_specs=(SEM, SEM, *[IN_HBM] * n, pl.BlockSpec(memory_space=pltpu.VMEM)),
        input_output_aliases={i: 2 + i for i in range(n)},
        compiler_params=pltpu.CompilerParams(has_side_effects=DATAFLOW),
    )(*operands)
    ORDER.last = outs[-1]
    return outs[0], outs[1], list(outs[2:2 + n])


def _copy_wait(name, bufs, copies, send_sems, recv_sems):
    n = len(bufs)
    after = ORDER.last

    def body(*refs):
        ins = refs[:n]
        for k, (src, dst, to) in enumerate(copies(ins)):
            cp = _remote(src, dst, refs[n].at[k], refs[n + 1].at[k], to)
            cp.wait_send()
            cp.wait_recv()

    outs = list(pl.pallas_call(
        body, name=name,
        out_shape=tuple(pltpu.HBM(b.shape, b.dtype) for b in bufs),
        in_specs=[IN_HBM] * n + [SEM, SEM, HBM], out_specs=tuple([IN_HBM] * n),
        input_output_aliases={i: i for i in range(n)},
        compiler_params=pltpu.CompilerParams(has_side_effects=DATAFLOW),
    )(*bufs, send_sems, recv_sems, after))
    ORDER.last = outs[0]
    return outs


def _gather_hop1(bufs):
    x, y, c, chips = _place()
    out = []
    for b in bufs:
        half = b.shape[1] // 2
        mine = b.at[2 * x + y, pl.ds(c * half, half), :]
        out += [(mine, mine, (*chip, c)) for chip in chips]
    return out


def _gather_hop2(bufs):
    x, y, c, chips = _place()
    out = []
    for b in bufs:
        half = b.shape[1] // 2
        for chip in chips:
            landed = b.at[2 * chip[0] + chip[1], pl.ds(c * half, half), :]
            out.append((landed, landed, (x, y, 1 - c)))
    return out


def _swap_copies(bufs):
    x, y, c, _ = _place()
    n = len(bufs) // 2
    out = []
    for p, land in zip(bufs[:n], bufs[n:]):
        half = p.shape[1] // 2
        out.append((p.at[:, pl.ds((1 - c) * half, half), :], land, (x, y, 1 - c)))
    return out


def _scatter_copies(bufs):
    _, _, c, chips = _place()
    n = len(bufs) // 2
    out = []
    for s_, land in zip(bufs[:n], bufs[n:]):
        out += [(s_.at[2 * chip[0] + chip[1]], land.at[j], (*chip, c)) for j, chip in enumerate(chips)]
    return out


def _join_copies(bufs):
    x, y, c, _ = _place()
    out = []
    for b in bufs:
        half = b.shape[0] // 2
        mine = b.at[pl.ds(c * half, half), :]
        out.append((mine, mine, (x, y, 1 - c)))
    return out


def _gather_small(vec):
    m_per, n = vec.shape

    def body(x_ref, out_ref, send_sems, recv_sems, local_sem):
        x, y, c, chips = _place()
        me, sibling = (x, y, c), (x, y, 1 - c)

        def rows(px, py, pc):
            return out_ref.at[pl.ds((4 * px + 2 * py + pc) * m_per, m_per), :]

        def copy(k, block, to, src=None):
            return _remote(rows(*block) if src is None else src, rows(*block), send_sems.at[k], recv_sems.at[k], to)

        mine = pltpu.make_async_copy(x_ref, rows(*me), local_sem)
        mine.start()
        first = [copy(0, me, sibling, src=x_ref)]
        first += [copy(1 + j, me, (*chip, c), src=x_ref) for j, chip in enumerate(chips)]
        for cp in first:
            cp.start()
        passed = [copy(4 + j, (*chip, c), sibling) for j, chip in enumerate(chips)]
        for j, chip in enumerate(chips):
            copy(1 + j, (*chip, c), me).wait_recv()
            passed[j].start()
        copy(0, sibling, me).wait_recv()
        for j, chip in enumerate(chips):
            copy(4 + j, (*chip, 1 - c), me).wait_recv()
        for cp in first + passed:
            cp.wait_send()
        mine.wait()

    return ORDER.call(
        body, [vec], [pl.BlockSpec(memory_space=pltpu.VMEM)], name="gather_small_grads",
        out_shape=_sds((8 * m_per, n), vec.dtype), out_specs=pl.BlockSpec(memory_space=pltpu.VMEM),
        scratch_shapes=[pltpu.SemaphoreType.DMA((7,)), pltpu.SemaphoreType.DMA((7,)), pltpu.SemaphoreType.DMA],
    )


def _add_sibling(name, partial, received, c):
    _, R, C = partial.shape
    half = R // 2
    tr = _tile(half, 256)
    nb = half // tr

    def body(c_ref, p_ref, r_ref, o_ref):
        o_ref[...] = (p_ref[...].astype(F32) + r_ref[...].astype(F32)).astype(BF)

    return ORDER.call(
        body, [partial, received],
        [pl.BlockSpec((None, tr, C), lambda j, i, cr: (j, cr[0] * nb + i, 0)),
         pl.BlockSpec((None, tr, C), lambda j, i, cr: (j, i, 0))],
        prefetch=(c,), name=name, grid=(N_CHIPS, nb),
        out_specs=pl.BlockSpec((None, tr, C), lambda j, i, cr: (j, i, 0)),
        out_shape=_sds((N_CHIPS, half, C), BF), compiler_params=_cparams(("parallel", "parallel")),
    )


def _add_chips(name, sums, received, me_c):
    _, half, C = sums.shape
    tr = _tile(half, 256)
    nb = half // tr

    def body(mc_ref, s_ref, r_ref, o_ref):
        acc = s_ref[...].astype(F32)
        for j in range(3):
            acc = acc + r_ref[j].astype(F32)
        o_ref[...] = acc

    return ORDER.call(
        body, [sums, received],
        [pl.BlockSpec((None, tr, C), lambda i, mc: (mc[0], i, 0)),
         pl.BlockSpec((3, tr, C), lambda i, mc: (0, i, 0))],
        prefetch=(me_c,), name=name, grid=(nb,),
        out_specs=pl.BlockSpec((tr, C), lambda i, mc: (mc[1] * nb + i, 0)),
        out_shape=_sds((2 * half, C), F32), compiler_params=_cparams(("parallel",)),
    )


def _adamw_math(w, g, m, v):
    m = ADAM_B1 * m + (1.0 - ADAM_B1) * g
    v = ADAM_B2 * v + (1.0 - ADAM_B2) * (g * g)
    m_hat = m / (1.0 - ADAM_B1 ** ADAM_STEP)
    v_hat = v / (1.0 - ADAM_B2 ** ADAM_STEP)
    delta = -ADAM_LR * (m_hat / (jnp.sqrt(v_hat) + ADAM_EPS) + ADAM_WD * w)
    return delta, m, v


def _adamw(name, w, g, m, v):
    R, C = w.shape
    tr = _tile(R, 256)

    def body(w_ref, g_ref, m_ref, v_ref, go_ref, d_ref, mo_ref, vo_ref):
        gv = g_ref[...]
        go_ref[...] = gv
        d_ref[...], mo_ref[...], vo_ref[...] = _adamw_math(w_ref[...], gv, m_ref[...], v_ref[...])

    row = pl.BlockSpec((tr, C), lambda i: (i, 0))
    return ORDER.call(
        body, [w, g, m, v], [row] * 4, name=name, grid=(R // tr,), out_specs=[row] * 4,
        out_shape=[_sds((R, C), F32)] * 4, compiler_params=_cparams(("parallel",)), chain_output=1,
    )


def _adamw_small(gathered, w, m, v):
    rows, n = w.shape

    def body(ga_ref, w_ref, m_ref, v_ref, go_ref, d_ref, mo_ref, vo_ref):
        g = ga_ref[pl.ds(0, rows), :]
        for dev in range(1, 8):
            g = g + ga_ref[pl.ds(dev * rows, rows), :]
        go_ref[...] = g
        d_ref[...], mo_ref[...], vo_ref[...] = _adamw_math(w_ref[...], g, m_ref[...], v_ref[...])

    whole = pl.BlockSpec(memory_space=pltpu.VMEM)
    return ORDER.call(
        body, [gathered, w, m, v], [whole] * 4, name="adamw_small", out_specs=[whole] * 4,
        out_shape=[_sds((rows, n), F32)] * 4, compiler_params=_cparams(), chain_output=1,
    )


def _proj_merge(y_a, y_b, gpa, gpb, g3):
    S, K = y_a.shape
    _, _, Nq = gpa.shape
    D = N_CHIPS * Nq
    tm, tn = _tile(S, 1024), _tile(Nq, 512)
    q = Nq // tn

    def body(ya_ref, yb_ref, wa_ref, wb_ref, g_ref, merged_ref, c_ref):
        pa = jnp.dot(ya_ref[...], wa_ref[...], preferred_element_type=F32)
        pb = jnp.dot(yb_ref[...], wb_ref[...], preferred_element_type=F32)
        g = g_ref[...].astype(F32)
        merged_ref[...] = (g[0] * pa + g[1] * pb).astype(BF)
        c_ref[0] = (pa * g[0] * (1.0 - g[0])).astype(BF)
        c_ref[1] = (pb * g[1] * (1.0 - g[1])).astype(BF)

    rows = pl.BlockSpec((tm, K), lambda i, j: (i, 0))
    weight = pl.BlockSpec((None, K, tn), lambda i, j: (j // q, 0, j % q))
    pair = pl.BlockSpec((2, tm, tn), lambda i, j: (0, i, j))
    return ORDER.call(
        body, [y_a, y_b, gpa, gpb, g3], [rows, rows, weight, weight, pair], name="proj_merge",
        grid=(S // tm, N_CHIPS * q), out_specs=[pl.BlockSpec((tm, tn), lambda i, j: (i, j)), pair],
        out_shape=[_sds((S, D), BF), _sds((2, S, D), BF)], compiler_params=_cparams(("parallel", "parallel")))


def _out_proj_dx(dx1b, wout, g3, c3, gpa, gpb):
    S, D = dx1b.shape
    _, K, Nq = gpa.shape
    tm, tn = _tile(S, 1024), Nq
    nj = D // tn

    def body(a_ref, w_ref, g_ref, c_ref, wa_ref, wb_ref, dpa_ref, dpb_ref, dg_ref, db_ref, dya_ref, dyb_ref,
             acc_a, acc_b):
        j = pl.program_id(1)
        dm = lax.dot_general(a_ref[...], w_ref[...], (NT, ((), ())), preferred_element_type=F32)
        g, c = g_ref[...].astype(F32), c_ref[...].astype(F32)
        dpa, dpb = (dm * g[0]).astype(BF), (dm * g[1]).astype(BF)
        dpa_ref[...] = dpa
        dpb_ref[...] = dpb
        dga, dgb = dm * c[0], dm * c[1]
        dg_ref[0] = dga.astype(BF)
        dg_ref[1] = dgb.astype(BF)
        db_ref[...] = jnp.concatenate([jnp.sum(dga, axis=0, keepdims=True), jnp.sum(dgb, axis=0, keepdims=True)], 0)
        ya = lax.dot_general(dpa, wa_ref[...], (NT, ((), ())), preferred_element_type=F32)
        yb = lax.dot_general(dpb, wb_ref[...], (NT, ((), ())), preferred_element_type=F32)

        @pl.when(j == 0)
        def _():
            acc_a[...] = ya
            acc_b[...] = yb

        @pl.when(j > 0)
        def _():
            acc_a[...] += ya
            acc_b[...] += yb

        @pl.when(j == nj - 1)
        def _():
            dya_ref[...] = acc_a[...].astype(BF)
            dyb_ref[...] = acc_b[...].astype(BF)

    tile = pl.BlockSpec((tm, tn), lambda i, j: (i, j))
    pair = pl.BlockSpec((2, tm, tn), lambda i, j: (0, i, j))
    shard = pl.BlockSpec((None, K, tn), lambda i, j: (j, 0, 0))
    rows = pl.BlockSpec((tm, K), lambda i, j: (i, 0))
    return ORDER.call(
        body, [dx1b, wout, g3, c3, gpa, gpb],
        [pl.BlockSpec((tm, D), lambda i, j: (i, 0)), pl.BlockSpec((tn, D), lambda i, j: (j, 0)), pair, pair, shard, shard],
        name="out_proj_dx", grid=(S // tm, nj),
        out_specs=[tile, tile, pair, pl.BlockSpec((None, 2, tn), lambda i, j: (i, 0, j)), rows, rows],
        out_shape=[_sds((S, D), BF), _sds((S, D), BF), _sds((2, S, D), BF), _sds((S // tm, 2, D), F32),
                   _sds((S, K), BF), _sds((S, K), BF)],
        scratch_shapes=[pltpu.VMEM((tm, K), F32), pltpu.VMEM((tm, K), F32)],
        compiler_params=_cparams(("parallel", "arbitrary")))


class _Exchange:
    GATHER = (("qkv",), ("gate",), ("proj_a", "proj_b", "out"), ("up",), ("down",))
    REDUCE = {"mlp": ("down", "up"), "mix": ("out", "proj_a", "proj_b"), "in": ("qkv", "gate")}

    OWN_FIRST = ("qkv", "gate")
    PLACE = (("qkv",), ("gate",), ("proj_a", "proj_b", "out"), ("up", "down"))

    def __init__(self, shards, me, c, moments):
        self.me, self.c = me, c
        self.shards, self.moments = shards, moments
        self.hop1, self.hop2, self.stage, self.grads, self.own, self.updates = {}, {}, {}, {}, {}, {}
        placed = {}
        for g, names in enumerate(self.GATHER):
            if names[0] not in placed:
                group = next(p for p in self.PLACE if names[0] in p)
                plain = group[0] in self.OWN_FIRST
                outs = _place_shards("place_" + "_".join(group), [shards[n] for n in group], me, plain=plain)
                placed.update(zip(group, outs[:len(group)]))
                if plain:
                    self.own.update(zip(group, outs[len(group):]))
            self.hop1[g] = _copy_start(f"gather{g}_start", [placed[n] for n in names], _gather_hop1, 3 * len(names))

    def forward(self, g):
        send, recv, thru = self.hop1.pop(g)
        self.hop2[g] = _copy_start(f"gather{g}_forward", thru, _gather_hop2, len(thru) * 3,
                                   earlier=(_gather_hop1, send, recv))

    def weights(self, g):
        send, recv, thru = self.hop2.pop(g)
        return _copy_wait(f"gather{g}_wait", thru, _gather_hop2, send, recv)

    def adamw_beside(self, name):
        def update(w, g, m, v):
            return (g,) + _adamw_math(w, g, m, v)
        return update, [self.shards[name], self.grads[name], *self.moments[name]], 4

    def reduce(self, key, partials=None):
        names = self.REDUCE[key]
        n = len(names)
        if partials is not None:
            lands = [lax.empty((p.shape[0], p.shape[1] // 2, p.shape[2]), p.dtype) for p in partials]
            self.stage[key] = ("swap",) + _copy_start(f"reduce_{key}_swap", list(partials) + lands, _swap_copies, n)
            return
        kind, send, recv, thru = self.stage.pop(key)
        if kind == "swap":
            thru = _copy_wait(f"reduce_{key}_swap_wait", thru, _swap_copies, send, recv)
            sums = [_add_sibling(f"reduce_{nm}_add_sibling", p, r, self.c)
                    for nm, p, r in zip(names, thru[:n], thru[n:])]
            lands = [lax.empty((3,) + s_.shape[1:], s_.dtype) for s_ in sums]
            self.stage[key] = ("scatter",) + _copy_start(f"reduce_{key}_scatter", sums + lands, _scatter_copies, 3 * n)
        elif kind == "scatter":
            thru = _copy_wait(f"reduce_{key}_scatter_wait", thru, _scatter_copies, send, recv)
            me_c = jnp.concatenate([self.me, self.c])
            halves = [_add_chips(f"reduce_{nm}_add_chips", s_, r, me_c)
                      for nm, s_, r in zip(names, thru[:n], thru[n:])]
            self.stage[key] = ("join",) + _copy_start(f"reduce_{key}_join", halves, _join_copies, n)
        else:
            thru = _copy_wait(f"reduce_{key}_join_wait", thru, _join_copies, send, recv)
            self.grads.update(zip(names, thru))


def _forward_backward(x, target, norm_mix, b_gate, rpb, norm_mlp, norm_final, ex):
    S, D = x.shape

    h1 = _rms_fwd("rms_mix", x, norm_mix)
    nq = QKV_W // 512
    qkv_out = (((3, S, QKV_W), BF), lambda i, T: (T // nq, i, T % nq))
    tg = _tile(ex.own["gate"].shape[1], 1024)
    ng = D // tg
    gate_out = (((2, S, D), BF), lambda i, T: (T // ng, i, T % ng))

    def gate_epilogue(acc, ex_, outs):
        outs[0][...] = jax.nn.sigmoid(acc + ex_[0][...]).astype(BF)

    qkv3 = _mm_nn_shards("qkv_own", h1, ex.own["qkv"], ex.me, True, *qkv_out, _store(BF), tm=2048)
    g3 = _mm_nn_shards("gate_own", h1, ex.own["gate"], ex.me, True, *gate_out, gate_epilogue, extras=[b_gate], tn=tg)
    ex.forward(0)
    e2 = _rpb_to_table(rpb)
    (gq,) = ex.weights(0)
    qkv3 = _mm_nn_shards("qkv", h1, gq, ex.me, False, *qkv_out, _store(BF), into=qkv3, tm=2048)

    ex.forward(1)
    outs_a = [_attn_a_fwd(qkv3, 0, DILATIONS[0])]
    (gg,) = ex.weights(1)
    g3 = _mm_nn_shards("gate", h1, gg, ex.me, False, *gate_out, gate_epilogue, extras=[b_gate], into=g3, tn=tg)

    ex.forward(2)
    qkv_views = _qkv_views("qkv_views", qkv3)
    outs_a += [_attn_a_fwd(qkv_views[d], grp, d) for grp, d in enumerate(DILATIONS) if grp > 0]
    y_a, lj = _attn_a_combine([o for o, _ in outs_a], [l for _, l in outs_a])
    y_b, lse_b = _attn_b_fwd(qkv3, e2)
    gpa, gpb, gout = ex.weights(2)
    wout = gout.reshape(D, D)
    merged, c3 = _proj_merge(y_a, y_b, gpa, gpb, g3)

    def residual_epilogue(acc, ex_, outs):
        outs[0][...] = acc + ex_[0][...]

    def residual_norm_epilogue(acc, ex_, outs):
        x1v = acc + ex_[0][...]
        outs[0][...] = x1v
        r = lax.rsqrt(jnp.mean(x1v * x1v, axis=-1, keepdims=True) + EPS)
        outs[1][...] = ((x1v * r) * ex_[1][...]).astype(BF)

    def nn_plain(name, a, w, res, bm=1024, bn=1024, norm=None):
        M, K = a.shape
        N = w.shape[1]
        bm, bn, bk = _tile(M, bm), _tile(N, bn), _tile(K, 2048)
        t = pl.BlockSpec((bm, bn), lambda i, j, k: (i, j))
        extras, outs, epilogue = [(res, t)], [(_sds((M, N), F32), t)], residual_epilogue
        if norm is not None:
            assert bn == N
            extras.append((norm, pl.BlockSpec((1, N), lambda i, j, k: (0, 0))))
            outs.append((_sds((M, N), BF), t))
            epilogue = residual_norm_epilogue
        result = _matmul(name, a, w, pl.BlockSpec((bm, bk), lambda i, j, k: (i, k)),
                         pl.BlockSpec((bk, bn), lambda i, j, k: (k, j)), NN, (M // bm, N // bn, K // bk), (bm, bn),
                         extras, outs, epilogue)
        return result[0] if norm is None else result

    ex.forward(3)
    x1, h2 = nn_plain("out_proj", merged, wout, x, bm=512, bn=2048, norm=norm_mlp)
    (gup,) = ex.weights(3)
    F = gup.shape[2] * N_CHIPS

    def up_epilogue(acc, ex_, outs):
        ru = jnp.maximum(acc, 0.0)
        outs[0][...] = (ru * ru).astype(BF)
        outs[1][...] = ru.astype(BF)

    tu = _tile(gup.shape[2], 2048)
    ut = pl.BlockSpec((_tile(S, 1024), tu), lambda i, j, k: (i, j))
    (act, ru), _ = _mm_nn_cols("mlp_up", h2, gup, BF, epilogue=up_epilogue, tn=tu,
                               outs=[(_sds((S, F), BF), ut), (_sds((S, F), BF), ut)])
    ex.forward(4)
    (gdown,) = ex.weights(4)
    wdown = gdown.reshape(F, D)
    x2 = nn_plain("mlp_down", act, wdown, x1)

    loss, dx2, dx2b, d_norm_final = _loss_head(x2, target, norm_final.reshape(1, D))

    def nt_rows(name, a, w, epilogue, extras, outs, bn=1024):
        M, N = a.shape
        K = w.shape[0]
        bm, bn, bk = _tile(M, 1024), _tile(K, bn), _tile(N, 2048)
        return _matmul(name, a, w, pl.BlockSpec((bm, bk), lambda i, j, k: (i, k)),
                       pl.BlockSpec((bn, bk), lambda i, j, k: (j, k)), NT, (M // bm, K // bn, N // bk), (bm, bn),
                       extras(bm, bn), outs(bm, bn), epilogue)

    def nt_cols(name, a_spec_fn, a, g, M, epilogue, extras, outs, bk, bn=1024, side=None):
        _, K, Nq = g.shape
        bm, bn, bk = _tile(M, 1024), _tile(K, bn), _tile(Nq, bk)
        q = Nq // bk
        return _matmul(name, a, g, a_spec_fn(bm, bk), pl.BlockSpec((None, bn, bk), lambda i, j, k: (k // q, j, k % q)),
                       NT, (M // bm, K // bn, N_CHIPS * q), (bm, bn), extras(bm, bn), outs(bm, bn), epilogue,
                       side=side)

    def tn_grad(name, a, a_spec_fn, b, b_spec_fn, Kin, N, out_shape, out_spec_fn, bn=1024):
        bm, bn, bk = _tile(Kin, 1024), _tile(N, bn), _tile(S, 4096)
        return _matmul(name, a, b, a_spec_fn(bk, bm), b_spec_fn(bk, bn), TN, (Kin // bm, N // bn, S // bk), (bm, bn),
                       [], [(_sds(out_shape, BF), out_spec_fn(bm, bn))], _store(BF))[0]

    plain_a = lambda bk, bm: pl.BlockSpec((bk, bm), lambda i, j, k: (k, i))
    plain_b = lambda bk, bn: pl.BlockSpec((bk, bn), lambda i, j, k: (k, j))
    plain_o = lambda bm, bn: pl.BlockSpec((bm, bn), lambda i, j, k: (i, j))
    a_rows = lambda bm, bk: pl.BlockSpec((bm, bk), lambda i, j, k: (i, k))

    def cols_o(Nq):
        def spec(bm, bn):
            q = Nq // bn
            return pl.BlockSpec((None, bm, bn), lambda i, j, k: (j // q, i, j % q))
        return spec

    def du_epilogue(acc, ex_, outs):
        outs[0][...] = (acc * (2.0 * ex_[0][...].astype(F32))).astype(BF)

    dw_down = tn_grad("mlp_down_dw", act, plain_a, dx2b, plain_b, F, D, (F, D), plain_o)
    (du,) = nt_rows("mlp_down_dx", dx2b, wdown, du_epilogue,
                    lambda bm, bn: [(ru, plain_o(bm, bn))], lambda bm, bn: [(_sds((S, F), BF), plain_o(bm, bn))],
                    bn=2048)

    fq = gup.shape[2]
    dw_up = tn_grad("mlp_up_dw", h2, plain_a, du, plain_b, D, F, (N_CHIPS, D, fq), cols_o(fq), bn=min(fq, 1024))
    ex.reduce("mlp", partials=[dw_down.reshape(N_CHIPS, F // N_CHIPS, D), dw_up])
    (dh2,) = nt_cols("mlp_up_dx", a_rows, du, gup, S, _store(F32), lambda bm, bn: [],
                     lambda bm, bn: [(_sds((S, D), F32), plain_o(bm, bn))], 1024, bn=2048)
    ex.reduce("mlp")
    dx1, dx1b, d_norm_mlp = _rms_bwd("rms_mlp_bwd", dh2, x1, norm_mlp, dx2)

    dpa, dpb, dg3, db_gate, dy_a, dy_b = _out_proj_dx(dx1b, wout, g3, c3, gpa, gpb)
    dw_out = tn_grad("out_proj_dw", merged, plain_a, dx1b, plain_b, D, D, (D, D), plain_o)

    pq = gpa.shape[2]
    dw_pa = tn_grad("proj_a_dw", y_a, plain_a, dpa, plain_b, 512, D, (N_CHIPS, 512, pq), cols_o(pq), bn=min(pq, 512))
    dw_pb = tn_grad("proj_b_dw", y_b, plain_a, dpb, plain_b, 512, D, (N_CHIPS, 512, pq), cols_o(pq), bn=min(pq, 512))
    ex.reduce("mix", partials=[dw_out.reshape(N_CHIPS, D // N_CHIPS, D), dw_pa, dw_pb])

    dqkv3 = lax.empty((3, S, QKV_W), BF)
    dqkv3 = _attn_a_bwd(qkv3, dy_a, y_a, lj, dqkv3, 0, DILATIONS[0])
    ex.reduce("mix")
    dy_views, y_views, lj_views = _dilated_rows("attn_a_bwd_rows", [dy_a, y_a, lj])
    dqkv_views = {d: _attn_a_bwd(qkv_views[d], dy_views[d], y_views[d], lj_views[d], None, grp, d)
                  for grp, d in enumerate(DILATIONS) if grp > 0}
    dqkv3 = _qkv_views("dqkv_from_views", dqkv3, dqkv_views)
    dqkv3, de2 = _attn_b_bwd(qkv3, e2, dy_b, y_b, lse_b, dqkv3)
    d_rpb = _table_grad_to_rpb(de2)

    def stacked_a(width):
        def spec(bm, bk):
            q = width // bk
            return pl.BlockSpec((None, bm, bk), lambda i, j, k: (k // q, i, k % q))
        return spec

    def stacked_b(width):
        def spec(bk, bn):
            q = width // bn
            return pl.BlockSpec((None, bk, bn), lambda i, j, k: (j // q, k, j % q))
        return spec

    ex.reduce("mlp")
    dw_qkv = tn_grad("qkv_dw", h1, plain_a, dqkv3, stacked_b(QKV_W), D, 3 * QKV_W, (N_CHIPS,) + gq.shape[1:],
                     cols_o(gq.shape[2]), bn=512)
    dw_gate = tn_grad("gate_dw", h1, plain_a, dg3, stacked_b(D), D, 2 * D, (N_CHIPS,) + gg.shape[1:],
                      cols_o(gg.shape[2]), bn=gg.shape[2])
    ex.reduce("in", partials=[dw_qkv, dw_gate])
    ex.reduce("mlp")
    dh1_q, *ex.updates["down"] = nt_cols(
        "qkv_dx", stacked_a(QKV_W), dqkv3, gq, S, _store(F32), lambda bm, bn: [],
        lambda bm, bn: [(_sds((S, D), F32), plain_o(bm, bn))], 512, bn=2048, side=ex.adamw_beside("down"))
    ex.reduce("in")
    ex.reduce("mix")

    def add_epilogue(acc, ex_, outs):
        outs[0][...] = acc + ex_[0][...]

    dh1, *ex.updates["up"] = nt_cols(
        "gate_dx", stacked_a(D), dg3, gg, S, add_epilogue, lambda bm, bn: [(dh1_q, plain_o(bm, bn))],
        lambda bm, bn: [(_sds((S, D), F32), plain_o(bm, bn))], gg.shape[2], side=ex.adamw_beside("up"))
    grad_x, _, d_norm_mix = _rms_bwd("rms_mix_bwd", dh1, x, norm_mix, dx1)
    ex.reduce("mix")

    small = [d_norm_mix, jnp.sum(db_gate, axis=0).reshape(1, 2 * D), d_rpb, d_norm_mlp, d_norm_final]
    return loss, grad_x, small


def _pack_small(parts, width):
    flat = jnp.concatenate([p.reshape(-1) for p in parts])
    return jnp.pad(flat, (0, 8 * width - flat.shape[0])).reshape(8, width)


def kernel(x, norm_mix, w_qkv, w_gate, b_gate, rpb, w_proj_a, w_proj_b, w_out, norm_mlp, w_up, w_down, norm_final, loss_target, m_norm_mix, m_w_qkv, m_w_gate, m_b_gate, m_rpb, m_w_proj_a, m_w_proj_b, m_w_out, m_norm_mlp, m_w_up, m_w_down, m_norm_final, v_norm_mix, v_w_qkv, v_w_gate, v_b_gate, v_rpb, v_w_proj_a, v_w_proj_b, v_w_out, v_norm_mlp, v_w_up, v_w_down, v_norm_final):
    names = ["qkv", "gate", "proj_a", "proj_b", "out", "up", "down"]
    big = dict(zip(names, [w_qkv[0], w_gate[0], w_proj_a[0], w_proj_b[0], w_out[0], w_up[0], w_down[0]]))
    big_m = dict(zip(names, [m_w_qkv[0], m_w_gate[0], m_w_proj_a[0], m_w_proj_b[0], m_w_out[0], m_w_up[0], m_w_down[0]]))
    big_v = dict(zip(names, [v_w_qkv[0], v_w_gate[0], v_w_proj_a[0], v_w_proj_b[0], v_w_out[0], v_w_up[0], v_w_down[0]]))

    c = lax.axis_index("c").astype(jnp.int32).reshape(1)
    me = (2 * lax.axis_index("x") + lax.axis_index("y")).astype(jnp.int32).reshape(1)
    ORDER.last = None
    ex = _Exchange(big, me, c, {n: (big_m[n], big_v[n]) for n in names})
    loss, grad_x, small = _forward_backward(x[0], loss_target[0], norm_mix, b_gate, rpb[0], norm_mlp, norm_final, ex)

    def adamw(group):
        return {n: ex.updates[n] if ex.updates.get(n) else _adamw(f"adamw_{n}", big[n], ex.grads[n], big_m[n], big_v[n])
                for n in _Exchange.REDUCE[group]}

    big_out = {**adamw("mlp"), **adamw("mix")}
    ex.reduce("in")

    small_w = [norm_mix, b_gate, rpb, norm_mlp, norm_final]
    count = sum(int(np.prod(p.shape)) for p in small_w)
    width = -(-count // (8 * 128)) * 128
    packed = _adamw_small(_gather_small(_pack_small(small, width)), _pack_small(small_w, width),
                          _pack_small([m_norm_mix, m_b_gate, m_rpb, m_norm_mlp, m_norm_final], width),
                          _pack_small([v_norm_mix, v_b_gate, v_rpb, v_norm_mlp, v_norm_final], width))
    ex.reduce("in")
    big_out.update(adamw("in"))

    def unpack(flat2d):
        flat, out, at = flat2d.reshape(-1), [], 0
        for p in small_w:
            size = int(np.prod(p.shape))
            out.append(flat[at:at + size].reshape(p.shape))
            at += size
        return out

    small_out = [unpack(a) for a in packed]

    def ordered(kind):
        sm = small_out[kind]
        bg = {n: o[kind][None] for n, o in big_out.items()}
        return [sm[0], bg["qkv"], bg["gate"], sm[1], sm[2], bg["proj_a"], bg["proj_b"], bg["out"], sm[3],
                bg["up"], bg["down"], sm[4]]

    total = lax.psum(loss[0, 0], ("x", "y", "c"))
    return (total, grad_x[None], *ordered(0), *ordered(1), *ordered(2), *ordered(3))
```
